```python
import jax, jax.numpy as jnp
from jax import lax
import numpy as np

D_MODEL = 1024
BATCH = 16
SEQ = 4096
DEPTH = 4

N_MIXERS = 3
N_POOL_LAYERS = (DEPTH + 2) // 3
N_MLA_LAYERS = (DEPTH + 1) // 3
N_CONV_LAYERS = DEPTH // 3

POOL_WINDOWS = (2, 4, 8, 16)
N_POOL_GROUPS = len(POOL_WINDOWS)
POOL_GROUP = D_MODEL // N_POOL_GROUPS

MLA_HEADS = D_MODEL // 64
QK_NOPE = 64
QK_ROPE = 32
V_HEAD = 64
Q_LORA = 3 * D_MODEL // 4
KV_LORA = D_MODEL // 4
ROPE_THETA = 10000.0
Q_BLOCK = 128

CONV_WIDTH = 3
FFN_HIDDEN = 2816

DEEPNORM_ALPHA = (2 * DEPTH) ** 0.25
DEEPNORM_BETA = (8 * DEPTH) ** -0.25
LN_EPS = 1e-5
RMS_EPS = 1e-6

kernel_name = 'hybrid_pool_mla_shortconv_deepnorm_adaln'


def layer_norm(x, g, b):
    xf = x.astype(jnp.float32)
    mu = jnp.mean(xf, axis=-1, keepdims=True)
    var = jnp.mean(jnp.square(xf - mu), axis=-1, keepdims=True)
    return ((xf - mu) * lax.rsqrt(var + LN_EPS) * g + b).astype(x.dtype)


def rms_norm(x, g):
    xf = x.astype(jnp.float32)
    y = xf * lax.rsqrt(jnp.mean(jnp.square(xf), axis=-1, keepdims=True) + RMS_EPS)
    return (y * g).astype(x.dtype)


def causal_dwconv(u, w):
    ch = u.shape[-1]
    return lax.conv_general_dilated(
        u, w[:, None, :].astype(u.dtype), window_strides=(1,), padding=[(CONV_WIDTH - 1, 0)],
        dimension_numbers=('NWC', 'WIO', 'NWC'), feature_group_count=ch)


def rope_tables(positions):
    inv_freq = ROPE_THETA ** (-jnp.arange(0, QK_ROPE, 2, dtype=jnp.float32) / QK_ROPE)
    ang = positions.astype(jnp.float32)[..., None] * inv_freq
    return jnp.cos(ang), jnp.sin(ang)


def apply_rope(x, cos, sin):
    half = x.shape[-1] // 2
    x1, x2 = x[..., :half], x[..., half:]
    cos = cos.astype(x.dtype)
    sin = sin.astype(x.dtype)
    return jnp.concatenate([x1 * cos - x2 * sin, x1 * sin + x2 * cos], axis=-1)


def pool_mixer(u, w_groups, scale):
    b, s, d = u.shape
    ug = u.reshape(b, s, N_POOL_GROUPS, POOL_GROUP)
    cs = jnp.cumsum(ug.astype(jnp.float32), axis=1)
    t = jnp.arange(s)
    means = []
    for g, w in enumerate(POOL_WINDOWS):
        csg = cs[:, :, g]
        prev = jnp.pad(csg, ((0, 0), (w, 0), (0, 0)))[:, :s]
        count = jnp.minimum(t + 1, w).astype(jnp.float32)[None, :, None]
        means.append((csg - prev) / count)
    pooled = jnp.stack(means, axis=2).astype(u.dtype) - ug
    y = jnp.einsum('bsgc,gcd->bsgd', pooled, w_groups).reshape(b, s, d)
    return y * scale


def mla_mixer(u, positions, w_a, q_norm, w_uq, kv_norm, w_ukv, w_o):
    b, s, _ = u.shape
    a = u @ w_a
    cq = rms_norm(a[..., :Q_LORA], q_norm)
    ckv = rms_norm(a[..., Q_LORA:Q_LORA + KV_LORA], kv_norm)
    k_pe = a[..., Q_LORA + KV_LORA:]
    q = (cq @ w_uq).reshape(b, s, MLA_HEADS, QK_NOPE + QK_ROPE)
    q_nope, q_pe = q[..., :QK_NOPE], q[..., QK_NOPE:]
    kv = (ckv @ w_ukv).reshape(b, s, MLA_HEADS, QK_NOPE + V_HEAD)
    k_nope, v = kv[..., :QK_NOPE], kv[..., QK_NOPE:]
    cos, sin = rope_tables(positions)
    q_pe = apply_rope(q_pe, cos[:, :, None], sin[:, :, None])
    k_pe = apply_rope(k_pe, cos, sin)
    sm_scale = (QK_NOPE + QK_ROPE) ** -0.5
    nb = s // Q_BLOCK
    qn_blocks = jnp.moveaxis(q_nope.reshape(b, nb, Q_BLOCK, MLA_HEADS, QK_NOPE), 1, 0)
    qp_blocks = jnp.moveaxis(q_pe.reshape(b, nb, Q_BLOCK, MLA_HEADS, QK_ROPE), 1, 0)
    key_idx = jnp.arange(s)
    neg = jnp.finfo(jnp.float32).min

    def attend(args):
        qn, qp, blk = args
        sc = (jnp.einsum('bqhd,bkhd->bhqk', qn, k_nope)
              + jnp.einsum('bqhr,bkr->bhqk', qp, k_pe)).astype(jnp.float32) * sm_scale
        q_idx = blk * Q_BLOCK + jnp.arange(Q_BLOCK)
        sc = jnp.where(key_idx[None, :] <= q_idx[:, None], sc, neg)
        p = jax.nn.softmax(sc, axis=-1).astype(v.dtype)
        return jnp.einsum('bhqk,bkhd->bqhd', p, v)

    o = lax.map(attend, (qn_blocks, qp_blocks, jnp.arange(nb)))
    o = jnp.moveaxis(o, 0, 1).reshape(b, s, MLA_HEADS * V_HEAD)
    return o @ w_o


def short_conv_mixer(u, w_in, conv_w, w_out):
    gb, gc, h = jnp.split(u @ w_in, 3, axis=-1)
    return (gb * causal_dwconv(gc * h, conv_w)) @ w_out


def conv_glu_ffn(u, w_up, conv_w, conv_b, w_down):
    h = causal_dwconv(u @ w_up, conv_w) + conv_b
    val, gate = jnp.split(h, 2, axis=-1)
    return (jax.nn.silu(gate) * val) @ w_down


def _fwd_setup_inputs(seed: int = 0) -> dict:
    key = jax.random.key(seed)
    ks = iter(jax.random.split(key, 32))

    def nrm(shape, scale):
        return jax.random.normal(next(ks), shape, jnp.float32) * scale

    D, H, F = D_MODEL, MLA_HEADS, FFN_HIDDEN
    x = nrm((BATCH, SEQ, D), 1.0)
    c = nrm((BATCH, D), 1.0)
    offsets = jax.random.randint(next(ks), (BATCH, 1), 0, 1024)
    positions = (offsets + jnp.arange(SEQ)[None, :]).astype(jnp.int32)
    mod_w = nrm((DEPTH, D, 6 * D), 0.1 * D ** -0.5)
    mod_b = nrm((DEPTH, 6 * D), 0.02)
    ln_g = 1.0 + nrm((DEPTH, 2, D), 0.02)
    ln_b = nrm((DEPTH, 2, D), 0.02)
    pool_w = nrm((N_POOL_LAYERS, N_POOL_GROUPS, POOL_GROUP, POOL_GROUP), POOL_GROUP ** -0.5 * DEEPNORM_BETA)
    pool_scale = 1.0 + nrm((N_POOL_LAYERS, D), 0.1)
    mla_w_a = nrm((N_MLA_LAYERS, D, Q_LORA + KV_LORA + QK_ROPE), D ** -0.5)
    mla_q_norm = 1.0 + nrm((N_MLA_LAYERS, Q_LORA), 0.02)
    mla_w_uq = nrm((N_MLA_LAYERS, Q_LORA, H * (QK_NOPE + QK_ROPE)), Q_LORA ** -0.5)
    mla_kv_norm = 1.0 + nrm((N_MLA_LAYERS, KV_LORA), 0.02)
    mla_w_ukv = nrm((N_MLA_LAYERS, KV_LORA, H * (QK_NOPE + V_HEAD)), KV_LORA ** -0.5)
    mla_w_o = nrm((N_MLA_LAYERS, H * V_HEAD, D), (H * V_HEAD) ** -0.5 * DEEPNORM_BETA)
    sc_w_in = nrm((N_CONV_LAYERS, D, 3 * D), D ** -0.5)
    sc_conv = nrm((N_CONV_LAYERS, CONV_WIDTH, D), CONV_WIDTH ** -0.5)
    sc_w_out = nrm((N_CONV_LAYERS, D, D), D ** -0.5 * DEEPNORM_BETA)
    ffn_w_up = nrm((DEPTH, D, 2 * F), D ** -0.5)
    ffn_conv = nrm((DEPTH, CONV_WIDTH, 2 * F), CONV_WIDTH ** -0.5)
    ffn_conv_b = nrm((DEPTH, 2 * F), 0.02)
    ffn_w_down = nrm((DEPTH, F, D), F ** -0.5 * DEEPNORM_BETA)
    return {'x': x, 'c': c, 'positions': positions, 'mod_w': mod_w, 'mod_b': mod_b,
            'ln_g': ln_g, 'ln_b': ln_b, 'pool_w': pool_w, 'pool_scale': pool_scale,
            'mla_w_a': mla_w_a, 'mla_q_norm': mla_q_norm, 'mla_w_uq': mla_w_uq,
            'mla_kv_norm': mla_kv_norm, 'mla_w_ukv': mla_w_ukv, 'mla_w_o': mla_w_o,
            'sc_w_in': sc_w_in, 'sc_conv': sc_conv, 'sc_w_out': sc_w_out,
            'ffn_w_up': ffn_w_up, 'ffn_conv': ffn_conv, 'ffn_conv_b': ffn_conv_b, 'ffn_w_down': ffn_w_down}


def _fwd_reference(x, c, positions, mod_w, mod_b, ln_g, ln_b, pool_w, pool_scale,
              mla_w_a, mla_q_norm, mla_w_uq, mla_kv_norm, mla_w_ukv, mla_w_o,
              sc_w_in, sc_conv, sc_w_out, ffn_w_up, ffn_conv, ffn_conv_b, ffn_w_down):
    cond = jax.nn.silu(c)
    for i in range(DEPTH):
        mod = (cond @ mod_w[i] + mod_b[i])[:, None, :]
        sh1, sc1, g1, sh2, sc2, g2 = jnp.split(mod, 6, axis=-1)
        u = x * (1.0 + sc1) + sh1
        kind, j = i % N_MIXERS, i // N_MIXERS
        if kind == 0:
            y = pool_mixer(u, pool_w[j], pool_scale[j])
        elif kind == 1:
            y = mla_mixer(u, positions, mla_w_a[j], mla_q_norm[j], mla_w_uq[j],
                          mla_kv_norm[j], mla_w_ukv[j], mla_w_o[j])
        else:
            y = short_conv_mixer(u, sc_w_in[j], sc_conv[j], sc_w_out[j])
        x = layer_norm(DEEPNORM_ALPHA * x + (1.0 + g1) * y, ln_g[i, 0], ln_b[i, 0])
        u = x * (1.0 + sc2) + sh2
        y = conv_glu_ffn(u, ffn_w_up[i], ffn_conv[i], ffn_conv_b[i], ffn_w_down[i])
        x = layer_norm(DEEPNORM_ALPHA * x + (1.0 + g2) * y, ln_g[i, 1], ln_b[i, 1])
    return x


import jax as _jax
import jax.numpy as _jnp

TWIN_FORMAT = 'train_step'
FWD_PARAMS = ['x', 'c', 'positions', 'mod_w', 'mod_b', 'ln_g', 'ln_b', 'pool_w', 'pool_scale', 'mla_w_a', 'mla_q_norm', 'mla_w_uq', 'mla_kv_norm', 'mla_w_ukv', 'mla_w_o', 'sc_w_in', 'sc_conv', 'sc_w_out', 'ffn_w_up', 'ffn_conv', 'ffn_conv_b', 'ffn_w_down']
TWIN_WEIGHTS = ['mod_w', 'mod_b', 'ln_g', 'ln_b', 'pool_w', 'pool_scale', 'mla_w_a', 'mla_q_norm', 'mla_w_uq', 'mla_kv_norm', 'mla_w_ukv', 'mla_w_o', 'sc_w_in', 'sc_conv', 'sc_w_out', 'ffn_w_up', 'ffn_conv', 'ffn_conv_b', 'ffn_w_down']
TWIN_DIFF_INPUT = 'x'
TWIN_INPUTS = ['x', 'c', 'positions', 'mod_w', 'mod_b', 'ln_g', 'ln_b', 'pool_w', 'pool_scale', 'mla_w_a', 'mla_q_norm', 'mla_w_uq', 'mla_kv_norm', 'mla_w_ukv', 'mla_w_o', 'sc_w_in', 'sc_conv', 'sc_w_out', 'ffn_w_up', 'ffn_conv', 'ffn_conv_b', 'ffn_w_down', 'loss_target', 'm_mod_w', 'm_mod_b', 'm_ln_g', 'm_ln_b', 'm_pool_w', 'm_pool_scale', 'm_mla_w_a', 'm_mla_q_norm', 'm_mla_w_uq', 'm_mla_kv_norm', 'm_mla_w_ukv', 'm_mla_w_o', 'm_sc_w_in', 'm_sc_conv', 'm_sc_w_out', 'm_ffn_w_up', 'm_ffn_conv', 'm_ffn_conv_b', 'm_ffn_w_down', 'v_mod_w', 'v_mod_b', 'v_ln_g', 'v_ln_b', 'v_pool_w', 'v_pool_scale', 'v_mla_w_a', 'v_mla_q_norm', 'v_mla_w_uq', 'v_mla_kv_norm', 'v_mla_w_ukv', 'v_mla_w_o', 'v_sc_w_in', 'v_sc_conv', 'v_sc_w_out', 'v_ffn_w_up', 'v_ffn_conv', 'v_ffn_conv_b', 'v_ffn_w_down']
TWIN_OUTPUTS = ['loss', 'grad_x', 'grad_mod_w', 'grad_mod_b', 'grad_ln_g', 'grad_ln_b', 'grad_pool_w', 'grad_pool_scale', 'grad_mla_w_a', 'grad_mla_q_norm', 'grad_mla_w_uq', 'grad_mla_kv_norm', 'grad_mla_w_ukv', 'grad_mla_w_o', 'grad_sc_w_in', 'grad_sc_conv', 'grad_sc_w_out', 'grad_ffn_w_up', 'grad_ffn_conv', 'grad_ffn_conv_b', 'grad_ffn_w_down', 'delta_mod_w', 'delta_mod_b', 'delta_ln_g', 'delta_ln_b', 'delta_pool_w', 'delta_pool_scale', 'delta_mla_w_a', 'delta_mla_q_norm', 'delta_mla_w_uq', 'delta_mla_kv_norm', 'delta_mla_w_ukv', 'delta_mla_w_o', 'delta_sc_w_in', 'delta_sc_conv', 'delta_sc_w_out', 'delta_ffn_w_up', 'delta_ffn_conv', 'delta_ffn_conv_b', 'delta_ffn_w_down', 'new_m_mod_w', 'new_m_mod_b', 'new_m_ln_g', 'new_m_ln_b', 'new_m_pool_w', 'new_m_pool_scale', 'new_m_mla_w_a', 'new_m_mla_q_norm', 'new_m_mla_w_uq', 'new_m_mla_kv_norm', 'new_m_mla_w_ukv', 'new_m_mla_w_o', 'new_m_sc_w_in', 'new_m_sc_conv', 'new_m_sc_w_out', 'new_m_ffn_w_up', 'new_m_ffn_conv', 'new_m_ffn_conv_b', 'new_m_ffn_w_down', 'new_v_mod_w', 'new_v_mod_b', 'new_v_ln_g', 'new_v_ln_b', 'new_v_pool_w', 'new_v_pool_scale', 'new_v_mla_w_a', 'new_v_mla_q_norm', 'new_v_mla_w_uq', 'new_v_mla_kv_norm', 'new_v_mla_w_ukv', 'new_v_mla_w_o', 'new_v_sc_w_in', 'new_v_sc_conv', 'new_v_sc_w_out', 'new_v_ffn_w_up', 'new_v_ffn_conv', 'new_v_ffn_conv_b', 'new_v_ffn_w_down']
TWIN_LEAF_KINDS = {'loss': 'loss', 'grad_x': 'grad_x', 'grad_mod_w': 'grad_w', 'grad_mod_b': 'grad_w', 'grad_ln_g': 'grad_w', 'grad_ln_b': 'grad_w', 'grad_pool_w': 'grad_w', 'grad_pool_scale': 'grad_w', 'grad_mla_w_a': 'grad_w', 'grad_mla_q_norm': 'grad_w', 'grad_mla_w_uq': 'grad_w', 'grad_mla_kv_norm': 'grad_w', 'grad_mla_w_ukv': 'grad_w', 'grad_mla_w_o': 'grad_w', 'grad_sc_w_in': 'grad_w', 'grad_sc_conv': 'grad_w', 'grad_sc_w_out': 'grad_w', 'grad_ffn_w_up': 'grad_w', 'grad_ffn_conv': 'grad_w', 'grad_ffn_conv_b': 'grad_w', 'grad_ffn_w_down': 'grad_w', 'delta_mod_w': 'delta_w', 'delta_mod_b': 'delta_w', 'delta_ln_g': 'delta_w', 'delta_ln_b': 'delta_w', 'delta_pool_w': 'delta_w', 'delta_pool_scale': 'delta_w', 'delta_mla_w_a': 'delta_w', 'delta_mla_q_norm': 'delta_w', 'delta_mla_w_uq': 'delta_w', 'delta_mla_kv_norm': 'delta_w', 'delta_mla_w_ukv': 'delta_w', 'delta_mla_w_o': 'delta_w', 'delta_sc_w_in': 'delta_w', 'delta_sc_conv': 'delta_w', 'delta_sc_w_out': 'delta_w', 'delta_ffn_w_up': 'delta_w', 'delta_ffn_conv': 'delta_w', 'delta_ffn_conv_b': 'delta_w', 'delta_ffn_w_down': 'delta_w', 'new_m_mod_w': 'new_m', 'new_m_mod_b': 'new_m', 'new_m_ln_g': 'new_m', 'new_m_ln_b': 'new_m', 'new_m_pool_w': 'new_m', 'new_m_pool_scale': 'new_m', 'new_m_mla_w_a': 'new_m', 'new_m_mla_q_norm': 'new_m', 'new_m_mla_w_uq': 'new_m', 'new_m_mla_kv_norm': 'new_m', 'new_m_mla_w_ukv': 'new_m', 'new_m_mla_w_o': 'new_m', 'new_m_sc_w_in': 'new_m', 'new_m_sc_conv': 'new_m', 'new_m_sc_w_out': 'new_m', 'new_m_ffn_w_up': 'new_m', 'new_m_ffn_conv': 'new_m', 'new_m_ffn_conv_b': 'new_m', 'new_m_ffn_w_down': 'new_m', 'new_v_mod_w': 'new_v', 'new_v_mod_b': 'new_v', 'new_v_ln_g': 'new_v', 'new_v_ln_b': 'new_v', 'new_v_pool_w': 'new_v', 'new_v_pool_scale': 'new_v', 'new_v_mla_w_a': 'new_v', 'new_v_mla_q_norm': 'new_v', 'new_v_mla_w_uq': 'new_v', 'new_v_mla_kv_norm': 'new_v', 'new_v_mla_w_ukv': 'new_v', 'new_v_mla_w_o': 'new_v', 'new_v_sc_w_in': 'new_v', 'new_v_sc_conv': 'new_v', 'new_v_sc_w_out': 'new_v', 'new_v_ffn_w_up': 'new_v', 'new_v_ffn_conv': 'new_v', 'new_v_ffn_conv_b': 'new_v', 'new_v_ffn_w_down': 'new_v'}


def _forward(args):
    return _fwd_reference(*[args[k] for k in FWD_PARAMS])


def _output_shape():
    out = _jax.eval_shape(lambda: _forward(_fwd_setup_inputs(0)))
    return out.shape, out.dtype

N_MICROBATCH = 1
ADAM_LR = 0.001
ADAM_B1 = 0.9
ADAM_B2 = 0.999
ADAM_EPS = 1e-08
ADAM_WD = 0.01
ADAM_STEP = 10
PER_EXAMPLE_BATCH_AXIS = {'x': 0, 'c': 0, 'positions': 0, 'loss_target': 0}
SHARED_INPUTS = []
_WEIGHT_DTYPES = {'mod_w': _jnp.float32, 'mod_b': _jnp.float32, 'ln_g': _jnp.float32, 'ln_b': _jnp.float32, 'pool_w': _jnp.float32, 'pool_scale': _jnp.float32, 'mla_w_a': _jnp.float32, 'mla_q_norm': _jnp.float32, 'mla_w_uq': _jnp.float32, 'mla_kv_norm': _jnp.float32, 'mla_w_ukv': _jnp.float32, 'mla_w_o': _jnp.float32, 'sc_w_in': _jnp.float32, 'sc_conv': _jnp.float32, 'sc_w_out': _jnp.float32, 'ffn_w_up': _jnp.float32, 'ffn_conv': _jnp.float32, 'ffn_conv_b': _jnp.float32, 'ffn_w_down': _jnp.float32}
MOMENT_SCALE = {'mod_w': 3.702835e-02, 'mod_b': 6.651604e-02, 'ln_g': 2.265612e+01, 'ln_b': 1.046097e+00, 'pool_w': 1.404088e-01, 'pool_scale': 8.376504e-02, 'mla_w_a': 2.325556e-02, 'mla_q_norm': 1.497607e-02, 'mla_w_uq': 1.028412e-02, 'mla_kv_norm': 3.480360e-02, 'mla_w_ukv': 1.348299e-02, 'mla_w_o': 3.777121e-02, 'sc_w_in': 6.204948e-02, 'sc_conv': 6.799896e-02, 'sc_w_out': 1.470876e-01, 'ffn_w_up': 2.485762e-02, 'ffn_conv': 2.460537e-02, 'ffn_conv_b': 2.680775e-02, 'ffn_w_down': 9.668301e-02}


def _to_microbatches(a, axis):
    t = _jnp.moveaxis(a, axis, 0)
    t = t.reshape((N_MICROBATCH, t.shape[0] // N_MICROBATCH) + t.shape[1:])
    return _jnp.moveaxis(t, 1, axis + 1)


def setup_inputs(seed: int = 0) -> dict:
    inp = _fwd_setup_inputs(seed)
    key = _jax.random.fold_in(_jax.random.key(seed), 7919)
    shape, _ = _output_shape()
    out = dict(inp)
    out["loss_target"] = _jax.random.normal(_jax.random.fold_in(key, 0), shape, _jnp.float32)
    for i, name in enumerate(TWIN_WEIGHTS):
        w = inp[name].astype(_jnp.float32)
        if MOMENT_SCALE is None:
            s = _jnp.sqrt(_jnp.mean(_jnp.square(w)) + 1e-30)
        else:
            s = MOMENT_SCALE[name]
        km, kv = _jax.random.split(_jax.random.fold_in(key, i + 1))
        out[name] = w
        out["m_" + name] = s * _jax.random.normal(km, w.shape, _jnp.float32)
        out["v_" + name] = (s * s) * _jax.random.uniform(kv, w.shape, _jnp.float32, 0.5, 1.5)
    if N_MICROBATCH > 1:
        for name, axis in PER_EXAMPLE_BATCH_AXIS.items():
            out[name] = _to_microbatches(out[name], axis)
    return {'x': out['x'], 'c': out['c'], 'positions': out['positions'], 'mod_w': out['mod_w'], 'mod_b': out['mod_b'], 'ln_g': out['ln_g'], 'ln_b': out['ln_b'], 'pool_w': out['pool_w'], 'pool_scale': out['pool_scale'], 'mla_w_a': out['mla_w_a'], 'mla_q_norm': out['mla_q_norm'], 'mla_w_uq': out['mla_w_uq'], 'mla_kv_norm': out['mla_kv_norm'], 'mla_w_ukv': out['mla_w_ukv'], 'mla_w_o': out['mla_w_o'], 'sc_w_in': out['sc_w_in'], 'sc_conv': out['sc_conv'], 'sc_w_out': out['sc_w_out'], 'ffn_w_up': out['ffn_w_up'], 'ffn_conv': out['ffn_conv'], 'ffn_conv_b': out['ffn_conv_b'], 'ffn_w_down': out['ffn_w_down'], 'loss_target': out['loss_target'], 'm_mod_w': out['m_mod_w'], 'm_mod_b': out['m_mod_b'], 'm_ln_g': out['m_ln_g'], 'm_ln_b': out['m_ln_b'], 'm_pool_w': out['m_pool_w'], 'm_pool_scale': out['m_pool_scale'], 'm_mla_w_a': out['m_mla_w_a'], 'm_mla_q_norm': out['m_mla_q_norm'], 'm_mla_w_uq': out['m_mla_w_uq'], 'm_mla_kv_norm': out['m_mla_kv_norm'], 'm_mla_w_ukv': out['m_mla_w_ukv'], 'm_mla_w_o': out['m_mla_w_o'], 'm_sc_w_in': out['m_sc_w_in'], 'm_sc_conv': out['m_sc_conv'], 'm_sc_w_out': out['m_sc_w_out'], 'm_ffn_w_up': out['m_ffn_w_up'], 'm_ffn_conv': out['m_ffn_conv'], 'm_ffn_conv_b': out['m_ffn_conv_b'], 'm_ffn_w_down': out['m_ffn_w_down'], 'v_mod_w': out['v_mod_w'], 'v_mod_b': out['v_mod_b'], 'v_ln_g': out['v_ln_g'], 'v_ln_b': out['v_ln_b'], 'v_pool_w': out['v_pool_w'], 'v_pool_scale': out['v_pool_scale'], 'v_mla_w_a': out['v_mla_w_a'], 'v_mla_q_norm': out['v_mla_q_norm'], 'v_mla_w_uq': out['v_mla_w_uq'], 'v_mla_kv_norm': out['v_mla_kv_norm'], 'v_mla_w_ukv': out['v_mla_w_ukv'], 'v_mla_w_o': out['v_mla_w_o'], 'v_sc_w_in': out['v_sc_w_in'], 'v_sc_conv': out['v_sc_conv'], 'v_sc_w_out': out['v_sc_w_out'], 'v_ffn_w_up': out['v_ffn_w_up'], 'v_ffn_conv': out['v_ffn_conv'], 'v_ffn_conv_b': out['v_ffn_conv_b'], 'v_ffn_w_down': out['v_ffn_w_down']}


def _loss(weights, diff, rest, loss_target):
    with _jax.named_scope("forward"):
        args = {**rest, TWIN_DIFF_INPUT: diff, **{k: w.astype(_WEIGHT_DTYPES[k]) for k, w in weights.items()}}
        y = _forward(args)
    with _jax.named_scope("loss_head"):
        err = _jnp.square(y.astype(_jnp.float32) - loss_target)
        return 0.5 * _jnp.sum(_jnp.mean(err, axis=-1)) if err.ndim else 0.5 * err


def _adamw(w, g, m, v):
    m = ADAM_B1 * m + (1.0 - ADAM_B1) * g
    v = ADAM_B2 * v + (1.0 - ADAM_B2) * _jnp.square(g)
    m_hat = m / (1.0 - ADAM_B1 ** ADAM_STEP)
    v_hat = v / (1.0 - ADAM_B2 ** ADAM_STEP)
    delta = -ADAM_LR * (m_hat / (_jnp.sqrt(v_hat) + ADAM_EPS) + ADAM_WD * w)
    return delta, m, v


def reference(x, c, positions, mod_w, mod_b, ln_g, ln_b, pool_w, pool_scale, mla_w_a, mla_q_norm, mla_w_uq, mla_kv_norm, mla_w_ukv, mla_w_o, sc_w_in, sc_conv, sc_w_out, ffn_w_up, ffn_conv, ffn_conv_b, ffn_w_down, loss_target, m_mod_w, m_mod_b, m_ln_g, m_ln_b, m_pool_w, m_pool_scale, m_mla_w_a, m_mla_q_norm, m_mla_w_uq, m_mla_kv_norm, m_mla_w_ukv, m_mla_w_o, m_sc_w_in, m_sc_conv, m_sc_w_out, m_ffn_w_up, m_ffn_conv, m_ffn_conv_b, m_ffn_w_down, v_mod_w, v_mod_b, v_ln_g, v_ln_b, v_pool_w, v_pool_scale, v_mla_w_a, v_mla_q_norm, v_mla_w_uq, v_mla_kv_norm, v_mla_w_ukv, v_mla_w_o, v_sc_w_in, v_sc_conv, v_sc_w_out, v_ffn_w_up, v_ffn_conv, v_ffn_conv_b, v_ffn_w_down):
    given = dict(x=x, c=c, positions=positions, mod_w=mod_w, mod_b=mod_b, ln_g=ln_g, ln_b=ln_b, pool_w=pool_w, pool_scale=pool_scale, mla_w_a=mla_w_a, mla_q_norm=mla_q_norm, mla_w_uq=mla_w_uq, mla_kv_norm=mla_kv_norm, mla_w_ukv=mla_w_ukv, mla_w_o=mla_w_o, sc_w_in=sc_w_in, sc_conv=sc_conv, sc_w_out=sc_w_out, ffn_w_up=ffn_w_up, ffn_conv=ffn_conv, ffn_conv_b=ffn_conv_b, ffn_w_down=ffn_w_down, loss_target=loss_target, m_mod_w=m_mod_w, m_mod_b=m_mod_b, m_ln_g=m_ln_g, m_ln_b=m_ln_b, m_pool_w=m_pool_w, m_pool_scale=m_pool_scale, m_mla_w_a=m_mla_w_a, m_mla_q_norm=m_mla_q_norm, m_mla_w_uq=m_mla_w_uq, m_mla_kv_norm=m_mla_kv_norm, m_mla_w_ukv=m_mla_w_ukv, m_mla_w_o=m_mla_w_o, m_sc_w_in=m_sc_w_in, m_sc_conv=m_sc_conv, m_sc_w_out=m_sc_w_out, m_ffn_w_up=m_ffn_w_up, m_ffn_conv=m_ffn_conv, m_ffn_conv_b=m_ffn_conv_b, m_ffn_w_down=m_ffn_w_down, v_mod_w=v_mod_w, v_mod_b=v_mod_b, v_ln_g=v_ln_g, v_ln_b=v_ln_b, v_pool_w=v_pool_w, v_pool_scale=v_pool_scale, v_mla_w_a=v_mla_w_a, v_mla_q_norm=v_mla_q_norm, v_mla_w_uq=v_mla_w_uq, v_mla_kv_norm=v_mla_kv_norm, v_mla_w_ukv=v_mla_w_ukv, v_mla_w_o=v_mla_w_o, v_sc_w_in=v_sc_w_in, v_sc_conv=v_sc_conv, v_sc_w_out=v_sc_w_out, v_ffn_w_up=v_ffn_w_up, v_ffn_conv=v_ffn_conv, v_ffn_conv_b=v_ffn_conv_b, v_ffn_w_down=v_ffn_w_down)
    weights = {n: given[n] for n in TWIN_WEIGHTS}
    shared = {n: given[n] for n in SHARED_INPUTS}
    per_example = {n: given[n] for n in ['x', 'c', 'positions']}
    grad_fn = _jax.value_and_grad(_loss, argnums=(0, 1))

    def one_microbatch(ex, loss_target):
        ex = dict(ex)
        diff = ex.pop(TWIN_DIFF_INPUT)
        return grad_fn(weights, diff, {**shared, **ex}, loss_target)

    if N_MICROBATCH == 1:
        loss, (grad_w, grad_x) = one_microbatch(per_example, given["loss_target"])
    else:
        def body(carry, xs):
            loss_sum, grad_sum = carry
            l_k, (gw_k, gx_k) = one_microbatch(xs[0], xs[1])
            with _jax.named_scope("update"):
                return (loss_sum + l_k, _jax.tree.map(_jnp.add, grad_sum, gw_k)), gx_k

        init = (_jnp.zeros((), _jnp.float32), _jax.tree.map(_jnp.zeros_like, weights))
        (loss, grad_w), grad_x = _jax.lax.scan(body, init, (per_example, given["loss_target"]))
    with _jax.named_scope("update"):
        delta_w, new_m, new_v = {}, {}, {}
        for n in TWIN_WEIGHTS:
            delta_w[n], new_m[n], new_v[n] = _adamw(weights[n], grad_w[n], given["m_" + n], given["v_" + n])
    return (loss, grad_x, *[grad_w[n] for n in TWIN_WEIGHTS], *[delta_w[n] for n in TWIN_WEIGHTS],
            *[new_m[n] for n in TWIN_WEIGHTS], *[new_v[n] for n in TWIN_WEIGHTS])
```

```python
import functools

import jax
import jax.numpy as jnp
from jax import lax
from jax.experimental import pallas as pl
from jax.experimental.pallas import tpu as pltpu

F32 = jnp.float32
BF16 = jnp.bfloat16
MESH = pl.DeviceIdType.MESH

N_DEV = 8
N_CHIP = 4
LANE = 128
SUBLANE = 8
VMEM_LIMIT_BYTES = 56 * 2 ** 20
PACK_COLS = 1024

LN_EPS = 1e-5
RMS_EPS = 1e-6
QK_NOPE, QK_ROPE, V_HEAD = 64, 32, 64
ROPE_THETA = 10000.0
HEAD_PAD = 128
POOL_GROUPS = 4
POOL_HALO = 16
CONV_HALO = 8
ADAM_LR, ADAM_B1, ADAM_B2, ADAM_EPS, ADAM_WD, ADAM_STEP = 0.001, 0.9, 0.999, 1e-08, 0.01, 10

WEIGHTS = ['mod_w', 'mod_b', 'ln_g', 'ln_b', 'pool_w', 'pool_scale', 'mla_w_a', 'mla_q_norm', 'mla_w_uq',
           'mla_kv_norm', 'mla_w_ukv', 'mla_w_o', 'sc_w_in', 'sc_conv', 'sc_w_out', 'ffn_w_up', 'ffn_conv',
           'ffn_conv_b', 'ffn_w_down']
BIG = {'pool_w': 2, 'mla_w_a': 2, 'mla_w_uq': 2, 'mla_w_ukv': 2, 'mla_w_o': 1, 'sc_w_in': 2, 'sc_w_out': 1,
       'ffn_w_up': 2, 'ffn_w_down': 1}
SMALL_SHARDED = {'ln_g': 2, 'ln_b': 2, 'pool_scale': 1, 'sc_conv': 2, 'ffn_conv': 2}
REPLICATED = ['mod_b', 'mla_q_norm', 'mla_kv_norm', 'ffn_conv_b']


def _pc(body, **kw):
    return pl.pallas_call(body, **kw)


def _cp(*sem):
    return pltpu.CompilerParams(dimension_semantics=sem, vmem_limit_bytes=VMEM_LIMIT_BYTES)


def _div(n, cap, mult):
    best = None
    for d in range(mult, min(n, cap) + 1, mult):
        if n % d == 0:
            best = d
    return best if best is not None else n


def _sds(shape, dtype):
    return jax.ShapeDtypeStruct(tuple(shape), dtype)


def _flip(v, bit):
    return 1 - v if bit else v


def _all_gather8(name, x_shard, in_vmem):
    m_per, n = x_shard.shape
    space = pltpu.VMEM if in_vmem else pltpu.HBM

    def body(x_ref, out_ref, send_sems, recv_sems, local_sem):
        x, y, c = lax.axis_index("x"), lax.axis_index("y"), lax.axis_index("c")
        me, sibling = (x, y, c), (x, y, 1 - c)
        chips = [(1 - x, y), (x, 1 - y), (1 - x, 1 - y)]

        def rows(px, py, pc_):
            return out_ref.at[pl.ds((4 * px + 2 * py + pc_) * m_per, m_per), :]

        def copy(k, block, to, src=None):
            return pltpu.make_async_remote_copy(
                src_ref=rows(*block) if src is None else src, dst_ref=rows(*block),
                send_sem=send_sems.at[k], recv_sem=recv_sems.at[k], device_id=to, device_id_type=MESH)

        mine = pltpu.make_async_copy(x_ref, rows(*me), local_sem)
        mine.start()
        first = [copy(0, me, sibling, src=x_ref)]
        first += [copy(1 + j, me, (*chip, c), src=x_ref) for j, chip in enumerate(chips)]
        for cp in first:
            cp.start()
        passed = [copy(4 + j, (*chip, c), sibling) for j, chip in enumerate(chips)]
        for j, chip in enumerate(chips):
            copy(1 + j, (*chip, c), me).wait_recv()
            passed[j].start()
        copy(0, sibling, me).wait_recv()
        for j, chip in enumerate(chips):
            copy(4 + j, (*chip, 1 - c), me).wait_recv()
        for cp in first + passed:
            cp.wait_send()
        mine.wait()

    return _pc(
        body, name=name, out_shape=_sds((N_DEV * m_per, n), x_shard.dtype),
        in_specs=[pl.BlockSpec(memory_space=space)], out_specs=pl.BlockSpec(memory_space=space),
        scratch_shapes=[pltpu.SemaphoreType.DMA((7,)), pltpu.SemaphoreType.DMA((7,)), pltpu.SemaphoreType.DMA],
        compiler_params=pltpu.CompilerParams(vmem_limit_bytes=VMEM_LIMIT_BYTES),
    )(x_shard)


def _all_to_all8(name, send):
    m_per = send.shape[0] // N_DEV

    def body(s_ref, r_ref, send_sems, recv_sems, local_sem):
        x, y, c = lax.axis_index("x"), lax.axis_index("y"), lax.axis_index("c")
        me = 4 * x + 2 * y + c

        def rows(ref, d):
            return ref.at[pl.ds(d * m_per, m_per), :]

        mine = pltpu.make_async_copy(rows(s_ref, me), rows(r_ref, me), local_sem)
        mine.start()
        sends, recvs = [], []
        for k in range(1, N_DEV):
            px, py, pcc = _flip(x, (k >> 2) & 1), _flip(y, (k >> 1) & 1), _flip(c, k & 1)
            peer = 4 * px + 2 * py + pcc
            sends.append(pltpu.make_async_remote_copy(
                src_ref=rows(s_ref, peer), dst_ref=rows(r_ref, me), send_sem=send_sems.at[k - 1],
                recv_sem=recv_sems.at[k - 1], device_id=(px, py, pcc), device_id_type=MESH))
            recvs.append(pltpu.make_async_remote_copy(
                src_ref=rows(s_ref, me), dst_ref=rows(r_ref, peer), send_sem=send_sems.at[k - 1],
                recv_sem=recv_sems.at[k - 1], device_id=(px, py, pcc), device_id_type=MESH))
        for cp in sends:
            cp.start()
        for cp in recvs:
            cp.wait_recv()
        for cp in sends:
            cp.wait_send()
        mine.wait()

    hbm = pl.BlockSpec(memory_space=pltpu.HBM)
    return _pc(
        body, name=name, out_shape=_sds(send.shape, send.dtype), in_specs=[hbm], out_specs=hbm,
        scratch_shapes=[pltpu.SemaphoreType.DMA((7,)), pltpu.SemaphoreType.DMA((7,)), pltpu.SemaphoreType.DMA],
    )(send)


def _sibling_gather(name, half):
    m_per, n = half.shape

    def body(h_ref, o_ref, send_sem, recv_sem, local_sem):
        x, y, c = lax.axis_index("x"), lax.axis_index("y"), lax.axis_index("c")

        def rows(cc):
            return o_ref.at[pl.ds(cc * m_per, m_per), :]

        mine = pltpu.make_async_copy(h_ref, rows(c), local_sem)
        mine.start()
        out = pltpu.make_async_remote_copy(src_ref=h_ref, dst_ref=rows(c), send_sem=send_sem, recv_sem=recv_sem,
                                           device_id=(x, y, 1 - c), device_id_type=MESH)
        out.start()
        pltpu.make_async_remote_copy(src_ref=h_ref, dst_ref=rows(1 - c), send_sem=send_sem, recv_sem=recv_sem,
                                     device_id=(x, y, 1 - c), device_id_type=MESH).wait_recv()
        out.wait_send()
        mine.wait()

    hbm = pl.BlockSpec(memory_space=pltpu.HBM)
    return _pc(
        body, name=name, out_shape=_sds((2 * m_per, n), half.dtype), in_specs=[hbm], out_specs=hbm,
        scratch_shapes=[pltpu.SemaphoreType.DMA, pltpu.SemaphoreType.DMA, pltpu.SemaphoreType.DMA],
    )(half)


def _sum8(name, parts):
    _, m, n = parts.shape
    tm = _div(m, 256, SUBLANE)

    def body(p_ref, o_ref):
        acc = p_ref[0]
        for s in range(1, N_DEV):
            acc = acc + p_ref[s]
        o_ref[...] = acc

    return _pc(body, name=name, grid=(m // tm,), out_shape=_sds((m, n), F32),
               in_specs=[pl.BlockSpec((N_DEV, tm, n), lambda i: (0, i, 0))],
               out_specs=pl.BlockSpec((tm, n), lambda i: (i, 0)), compiler_params=_cp("parallel"))(parts)


def _pack_rows(arrays, dtype, row_mult):
    flat, spans, off = [], [], 0
    for a in arrays:
        flat.append(a.reshape(-1).astype(dtype))
        spans.append((off, a.shape))
        off += a.size
    quantum = row_mult * PACK_COLS
    total = -(-off // quantum) * quantum
    if total > off:
        flat.append(jnp.zeros((total - off,), dtype))
    return jnp.concatenate(flat).reshape(total // PACK_COLS, PACK_COLS), spans


def _unpack(flat, spans):
    return [flat[off:off + _size(shape)].reshape(shape) for off, shape in spans]


def _size(shape):
    n = 1
    for s in shape:
        n *= s
    return n


def _join_chips(blocks, axis):
    return jnp.concatenate([blocks[j] for j in range(N_CHIP)], axis=axis)


def _split_chips(full, axis):
    return jnp.stack(jnp.split(full, N_CHIP, axis=axis))


def _mm_nn(name, pairs, out_dtype, tm_cap=1024, tn_cap=1536):
    m, n = pairs[0][0].shape[0], pairs[0][1].shape[1]
    tm, tn = _div(m, tm_cap, 16), _div(n, tn_cap, LANE)
    n_pairs = len(pairs)

    def body(*refs):
        o_ref = refs[-1]
        acc = jnp.dot(refs[0][...], refs[1][...], preferred_element_type=F32)
        for i in range(1, n_pairs):
            acc = acc + jnp.dot(refs[2 * i][...], refs[2 * i + 1][...], preferred_element_type=F32)
        o_ref[...] = acc.astype(o_ref.dtype)

    in_specs, args = [], []
    for a, b in pairs:
        k = a.shape[1]
        in_specs += [pl.BlockSpec((tm, k), lambda j, i: (i, 0)), pl.BlockSpec((k, tn), lambda j, i: (0, j))]
        args += [a, b]
    return _pc(body, name=name, grid=(n // tn, m // tm), out_shape=_sds((m, n), out_dtype), in_specs=in_specs,
               out_specs=pl.BlockSpec((tm, tn), lambda j, i: (i, j)), compiler_params=_cp("parallel", "parallel"))(*args)


def _mm_tn(name, x, y, tt_cap=512):
    t, k = x.shape
    n = y.shape[1]
    tk, tn, tt = _div(k, 1024, LANE), _div(n, 1536, LANE), _div(t, tt_cap, 16)

    def body(x_ref, y_ref, o_ref):
        @pl.when(pl.program_id(2) == 0)
        def _():
            o_ref[...] = jnp.zeros_like(o_ref)
        o_ref[...] += lax.dot_general(x_ref[...], y_ref[...], (((0,), (0,)), ((), ())), preferred_element_type=F32)

    return _pc(body, name=name, grid=(k // tk, n // tn, t // tt), out_shape=_sds((k, n), F32),
               in_specs=[pl.BlockSpec((tt, tk), lambda a, b, s: (s, a)), pl.BlockSpec((tt, tn), lambda a, b, s: (s, b))],
               out_specs=pl.BlockSpec((tk, tn), lambda a, b, s: (a, b)),
               compiler_params=_cp("parallel", "parallel", "arbitrary"))(x, y)


def _tok_spec(ts, d):
    return pl.BlockSpec((1, ts, d), lambda b, i: (b, i, 0))


def _seq_spec(d):
    return pl.BlockSpec((1, 1, d), lambda b, i: (b, 0, 0))


def _vec_spec(d):
    return pl.BlockSpec((1, d), lambda b, i: (0, 0))


def _ln_stats(z):
    mu = jnp.mean(z, axis=-1, keepdims=True)
    zc = z - mu
    var = jnp.mean(zc * zc, axis=-1, keepdims=True)
    rstd = lax.rsqrt(var + LN_EPS)
    return zc * rstd, rstd


def _modulate(name, x, sc, sh):
    b, s, d = x.shape
    ts = _div(s, 512, 16)

    def body(x_ref, sc_ref, sh_ref, u_ref):
        u_ref[0] = (x_ref[0] * (1.0 + sc_ref[0]) + sh_ref[0]).astype(BF16)

    return _pc(body, name=name, grid=(b, s // ts), out_shape=_sds(x.shape, BF16),
               in_specs=[_tok_spec(ts, d), _seq_spec(d), _seq_spec(d)], out_specs=_tok_spec(ts, d),
               compiler_params=_cp("parallel", "parallel"))(x, sc, sh)


def _ln_mod_fwd(name, alpha, x, y, g, lng, lnb, sc, sh):
    b, s, d = x.shape
    ts = _div(s, 512, 16)

    def body(x_ref, y_ref, g_ref, lng_ref, lnb_ref, sc_ref, sh_ref, z_ref, xn_ref, u_ref):
        z = alpha * x_ref[0] + (1.0 + g_ref[0]) * y_ref[0]
        xhat, _ = _ln_stats(z)
        xn = xhat * lng_ref[...] + lnb_ref[...]
        z_ref[0] = z
        xn_ref[0] = xn
        u_ref[0] = (xn * (1.0 + sc_ref[0]) + sh_ref[0]).astype(BF16)

    tok, seq, vec = _tok_spec(ts, d), _seq_spec(d), _vec_spec(d)
    return _pc(body, name=name, grid=(b, s // ts),
               out_shape=(_sds(x.shape, F32), _sds(x.shape, F32), _sds(x.shape, BF16)),
               in_specs=[tok, tok, seq, vec, vec, seq, seq], out_specs=(tok, tok, tok),
               compiler_params=_cp("parallel", "parallel"))(x, y, g, lng, lnb, sc, sh)


def _ln_loss_fwd(name, alpha, x, y, g, lng, lnb, target):
    b, s, d = x.shape
    ts = _div(s, 512, 16)

    def body(x_ref, y_ref, g_ref, lng_ref, lnb_ref, t_ref, z_ref, ct_ref, loss_ref):
        @pl.when((pl.program_id(0) == 0) & (pl.program_id(1) == 0))
        def _():
            loss_ref[...] = jnp.zeros_like(loss_ref)
        z = alpha * x_ref[0] + (1.0 + g_ref[0]) * y_ref[0]
        xhat, _ = _ln_stats(z)
        err = xhat * lng_ref[...] + lnb_ref[...] - t_ref[0]
        z_ref[0] = z
        ct_ref[0] = err / d
        part = 0.5 * jnp.sum(jnp.mean(err * err, axis=-1, keepdims=True))
        loss_ref[...] += jnp.full(loss_ref.shape, part, F32)

    tok, seq, vec = _tok_spec(ts, d), _seq_spec(d), _vec_spec(d)
    return _pc(body, name=name, grid=(b, s // ts),
               out_shape=(_sds(x.shape, F32), _sds(x.shape, F32), _sds((SUBLANE, LANE), F32)),
               in_specs=[tok, tok, seq, vec, vec, tok],
               out_specs=(tok, tok, pl.BlockSpec((SUBLANE, LANE), lambda b, i: (0, 0))),
               compiler_params=_cp("arbitrary", "arbitrary"))(x, y, g, lng, lnb, target)


def _sub_bwd(name, alpha, upstream, z, y, g, lng):
    b, s, d = z.shape
    ts = _div(s, 512, 16)
    last = len(upstream) == 1

    def body(*refs):
        if last:
            ct_ref, z_ref, y_ref, g_ref, lng_ref, dz_ref, dy_ref, dg_ref, dlng_ref, dlnb_ref = refs
        else:
            (dzn_ref, dun_ref, xn_ref, scn_ref, z_ref, y_ref, g_ref, lng_ref,
             dz_ref, dy_ref, dg_ref, dlng_ref, dlnb_ref, dsc_ref, dsh_ref) = refs
        first_tile = pl.program_id(1) == 0

        @pl.when(first_tile & (pl.program_id(0) == 0))
        def _():
            dlng_ref[...] = jnp.zeros_like(dlng_ref)
            dlnb_ref[...] = jnp.zeros_like(dlnb_ref)

        @pl.when(first_tile)
        def _():
            dg_ref[...] = jnp.zeros_like(dg_ref)
            if not last:
                dsc_ref[...] = jnp.zeros_like(dsc_ref)
                dsh_ref[...] = jnp.zeros_like(dsh_ref)

        if last:
            ct = ct_ref[0]
        else:
            dun = dun_ref[0]
            ct = alpha * dzn_ref[0] + dun * (1.0 + scn_ref[0])
            dsc_ref[0] += jnp.sum(dun * xn_ref[0], axis=0, keepdims=True)
            dsh_ref[0] += jnp.sum(dun, axis=0, keepdims=True)
        xhat, rstd = _ln_stats(z_ref[0])
        dlng_ref[...] += jnp.sum(ct * xhat, axis=0, keepdims=True)
        dlnb_ref[...] += jnp.sum(ct, axis=0, keepdims=True)
        dxhat = ct * lng_ref[...]
        dz = rstd * (dxhat - jnp.mean(dxhat, axis=-1, keepdims=True)
                     - xhat * jnp.mean(dxhat * xhat, axis=-1, keepdims=True))
        dz_ref[0] = dz
        dy_ref[0] = ((1.0 + g_ref[0]) * dz).astype(BF16)
        dg_ref[0] += jnp.sum(dz * y_ref[0], axis=0, keepdims=True)

    tok, seq, vec = _tok_spec(ts, d), _seq_spec(d), _vec_spec(d)
    seq_out = _sds((b, 1, d), F32)
    out_shape = [_sds(z.shape, F32), _sds(z.shape, BF16), seq_out, _sds((1, d), F32), _sds((1, d), F32)]
    out_specs = [tok, tok, seq, vec, vec]
    if last:
        in_specs = [tok, tok, tok, seq, vec]
    else:
        in_specs = [tok, tok, tok, seq, tok, tok, seq, vec]
        out_shape += [seq_out, seq_out]
        out_specs += [seq, seq]
    return _pc(body, name=name, grid=(b, s // ts), out_shape=tuple(out_shape), in_specs=in_specs,
               out_specs=tuple(out_specs), compiler_params=_cp("arbitrary", "arbitrary"))(*upstream, z, y, g, lng)


def _input_bwd(name, alpha, dz, du, x, sc):
    b, s, d = x.shape
    ts = _div(s, 512, 16)

    def body(dz_ref, du_ref, x_ref, sc_ref, gx_ref, dsc_ref, dsh_ref):
        @pl.when(pl.program_id(1) == 0)
        def _():
            dsc_ref[...] = jnp.zeros_like(dsc_ref)
            dsh_ref[...] = jnp.zeros_like(dsh_ref)
        du_ = du_ref[0]
        gx_ref[0] = alpha * dz_ref[0] + du_ * (1.0 + sc_ref[0])
        dsc_ref[0] += jnp.sum(du_ * x_ref[0], axis=0, keepdims=True)
        dsh_ref[0] += jnp.sum(du_, axis=0, keepdims=True)

    tok, seq = _tok_spec(ts, d), _seq_spec(d)
    seq_out = _sds((b, 1, d), F32)
    return _pc(body, name=name, grid=(b, s // ts), out_shape=(_sds(x.shape, F32), seq_out, seq_out),
               in_specs=[tok, tok, tok, seq], out_specs=(tok, seq, seq),
               compiler_params=_cp("parallel", "arbitrary"))(dz, du, x, sc)


def _rows_iota(shape):
    return lax.broadcasted_iota(jnp.int32, shape, 0)


def _back(v, k):
    return pltpu.roll(v, k, axis=0)


def _ahead(v, k):
    return pltpu.roll(v, v.shape[0] - k, axis=0)


def _conv3(ext, w_ref):
    return w_ref[2:3, :] * ext + w_ref[1:2, :] * _back(ext, 1) + w_ref[0:1, :] * _back(ext, 2)


def _conv3_t(dh_ext, w_ref):
    return w_ref[2:3, :] * dh_ext + w_ref[1:2, :] * _ahead(dh_ext, 1) + w_ref[0:1, :] * _ahead(dh_ext, 2)


def _flag(cond):
    return jnp.where(cond, 1.0, 0.0).astype(F32)


def _sigmoid(v):
    return 1.0 / (1.0 + jnp.exp(-v))


def _halo_specs(ts, tc, halo, n_s, col):
    per = ts // halo
    tile = pl.BlockSpec((1, ts, tc), lambda b, i, j: (b, i, col(j)))
    prev = pl.BlockSpec((1, halo, tc), lambda b, i, j: (b, jnp.maximum(i * per - 1, 0), col(j)))
    nxt = pl.BlockSpec((1, halo, tc), lambda b, i, j: (b, jnp.minimum((i + 1) * per, n_s * per - 1), col(j)))
    return tile, prev, nxt


def _convglu_fwd(name, p, cw, cb):
    b, s, f2 = p.shape
    f = f2 // 2
    ts, tc = _div(s, 512, CONV_HALO), _div(f, 256, LANE)
    n_s, n_c = s // ts, f // tc

    def body(pv_ref, pvh_ref, pg_ref, pgh_ref, wv_ref, wg_ref, bv_ref, bg_ref, a_ref):
        keep = _flag(pl.program_id(1) > 0)

        def conv(t_ref, h_ref, w_ref, b_ref):
            ext = jnp.concatenate([h_ref[0] * keep, t_ref[0]], axis=0)
            return _conv3(ext, w_ref)[CONV_HALO:] + b_ref[...]

        val = conv(pv_ref, pvh_ref, wv_ref, bv_ref)
        gate = conv(pg_ref, pgh_ref, wg_ref, bg_ref)
        a_ref[0] = (gate * _sigmoid(gate) * val).astype(BF16)

    tv, hv, _ = _halo_specs(ts, tc, CONV_HALO, n_s, lambda j: j)
    tg, hg, _ = _halo_specs(ts, tc, CONV_HALO, n_s, lambda j: j + n_c)
    wv = pl.BlockSpec((3, tc), lambda b, i, j: (0, j))
    wg = pl.BlockSpec((3, tc), lambda b, i, j: (0, j + n_c))
    bv = pl.BlockSpec((1, tc), lambda b, i, j: (0, j))
    bg = pl.BlockSpec((1, tc), lambda b, i, j: (0, j + n_c))
    return _pc(body, name=name, grid=(b, n_s, n_c), out_shape=_sds((b, s, f), BF16),
               in_specs=[tv, hv, tg, hg, wv, wg, bv, bg], out_specs=pl.BlockSpec((1, ts, tc), lambda b, i, j: (b, i, j)),
               compiler_params=_cp("parallel", "parallel", "parallel"))(p, p, p, p, cw, cw, cb, cb)


def _convglu_bwd(name, p, da, cw, cb):
    b, s, f2 = p.shape
    f = f2 // 2
    ts, tc = _div(s, 512, CONV_HALO), _div(f, 256, LANE)
    n_s, n_c = s // ts, f // tc

    def body(pv_ref, pvp_ref, pvn_ref, pg_ref, pgp_ref, pgn_ref, da_ref, dan_ref, wv_ref, wg_ref, bv_ref, bg_ref,
             dpv_ref, dpg_ref, dwv_ref, dwg_ref, dbv_ref, dbg_ref):
        bi, i = pl.program_id(1), pl.program_id(2)

        @pl.when((bi == 0) & (i == 0))
        def _():
            for r in (dwv_ref, dwg_ref, dbv_ref, dbg_ref):
                r[...] = jnp.zeros_like(r)

        keep_prev = _flag(i > 0)
        keep_next = _flag(i < n_s - 1)
        pv_ext = jnp.concatenate([pvp_ref[0] * keep_prev, pv_ref[0], pvn_ref[0]], axis=0)
        pg_ext = jnp.concatenate([pgp_ref[0] * keep_prev, pg_ref[0], pgn_ref[0]], axis=0)
        val = _conv3(pv_ext, wv_ref)[CONV_HALO:] + bv_ref[...]
        gate = _conv3(pg_ext, wg_ref)[CONV_HALO:] + bg_ref[...]
        da_ext = jnp.concatenate([da_ref[0], dan_ref[0] * keep_next], axis=0)
        sg = _sigmoid(gate)
        dval = da_ext * gate * sg
        dgate = da_ext * val * (sg * (1.0 + gate * (1.0 - sg)))
        dpv_ref[0] = _conv3_t(dval, wv_ref)[:ts].astype(BF16)
        dpg_ref[0] = _conv3_t(dgate, wg_ref)[:ts].astype(BF16)
        for dh, p_ext, dw_ref, db_ref in ((dval[:ts], pv_ext, dwv_ref, dbv_ref), (dgate[:ts], pg_ext, dwg_ref, dbg_ref)):
            db_ref[...] += jnp.sum(dh, axis=0, keepdims=True)
            for k in range(3):
                shifted = p_ext if k == 2 else _back(p_ext, 2 - k)
                dw_ref[k:k + 1, :] += jnp.sum(dh * shifted[CONV_HALO:CONV_HALO + ts], axis=0, keepdims=True)

    def specs(col):
        per = ts // CONV_HALO
        tile = pl.BlockSpec((1, ts, tc), lambda j, b, i: (b, i, col(j)))
        prev = pl.BlockSpec((1, CONV_HALO, tc), lambda j, b, i: (b, jnp.maximum(i * per - 1, 0), col(j)))
        nxt = pl.BlockSpec((1, CONV_HALO, tc), lambda j, b, i: (b, jnp.minimum((i + 1) * per, n_s * per - 1), col(j)))
        return tile, prev, nxt

    tv, pvp, pvn = specs(lambda j: j)
    tg, pgp, pgn = specs(lambda j: j + n_c)
    wv = pl.BlockSpec((3, tc), lambda j, b, i: (0, j))
    wg = pl.BlockSpec((3, tc), lambda j, b, i: (0, j + n_c))
    bv = pl.BlockSpec((1, tc), lambda j, b, i: (0, j))
    bg = pl.BlockSpec((1, tc), lambda j, b, i: (0, j + n_c))
    out_tile = pl.BlockSpec((1, ts, tc), lambda j, b, i: (b, i, j))
    acc3, acc1 = pl.BlockSpec((3, tc), lambda j, b, i: (0, j)), pl.BlockSpec((1, tc), lambda j, b, i: (0, j))
    dpv, dpg, dwv, dwg, dbv, dbg = _pc(
        body, name=name, grid=(n_c, b, n_s),
        out_shape=(_sds((b, s, f), BF16), _sds((b, s, f), BF16), _sds((3, f), F32), _sds((3, f), F32),
                   _sds((1, f), F32), _sds((1, f), F32)),
        in_specs=[tv, pvp, pvn, tg, pgp, pgn, tv, pvn, wv, wg, bv, bg],
        out_specs=(out_tile, out_tile, acc3, acc3, acc1, acc1),
        compiler_params=_cp("parallel", "arbitrary", "arbitrary"))(p, p, p, p, p, p, da, da, cw, cw, cb, cb)
    return dpv, dpg, jnp.concatenate([dwv, dwg], axis=1), jnp.concatenate([dbv, dbg], axis=1)


def _shortconv_fwd(name, q, cw):
    b, s, d3 = q.shape
    d = d3 // 3
    ts, tc = _div(s, 512, CONV_HALO), _div(d, 256, LANE)
    n_s, n_c = s // ts, d // tc

    def body(gb_ref, gc_ref, gch_ref, h_ref, hh_ref, w_ref, r_ref):
        keep = _flag(pl.program_id(1) > 0)
        m_ext = jnp.concatenate([gch_ref[0] * hh_ref[0] * keep, gc_ref[0] * h_ref[0]], axis=0)
        r_ref[0] = (gb_ref[0] * _conv3(m_ext, w_ref)[CONV_HALO:]).astype(BF16)

    tb, _, _ = _halo_specs(ts, tc, CONV_HALO, n_s, lambda j: j)
    tcc, hc, _ = _halo_specs(ts, tc, CONV_HALO, n_s, lambda j: j + n_c)
    th, hh, _ = _halo_specs(ts, tc, CONV_HALO, n_s, lambda j: j + 2 * n_c)
    w = pl.BlockSpec((3, tc), lambda b, i, j: (0, j))
    return _pc(body, name=name, grid=(b, n_s, n_c), out_shape=_sds((b, s, d), BF16),
               in_specs=[tb, tcc, hc, th, hh, w], out_specs=pl.BlockSpec((1, ts, tc), lambda b, i, j: (b, i, j)),
               compiler_params=_cp("parallel", "parallel", "parallel"))(q, q, q, q, q, cw)


def _shortconv_bwd(name, q, dr, cw):
    b, s, d3 = q.shape
    d = d3 // 3
    ts, tc = _div(s, 512, CONV_HALO), _div(d, 256, LANE)
    n_s, n_c = s // ts, d // tc

    def body(gb_ref, gbn_ref, gc_ref, gcp_ref, h_ref, hp_ref, dr_ref, drn_ref, w_ref,
             dgb_ref, dgc_ref, dh_ref, dw_ref):
        bi, i = pl.program_id(1), pl.program_id(2)

        @pl.when((bi == 0) & (i == 0))
        def _():
            dw_ref[...] = jnp.zeros_like(dw_ref)

        keep_prev = _flag(i > 0)
        keep_next = _flag(i < n_s - 1)
        gc, h = gc_ref[0], h_ref[0]
        m_ext = jnp.concatenate([gcp_ref[0] * hp_ref[0] * keep_prev, gc * h], axis=0)
        cm = _conv3(m_ext, w_ref)[CONV_HALO:]
        dr_ = dr_ref[0]
        dgb_ref[0] = (dr_ * cm).astype(BF16)
        dcv_ext = jnp.concatenate([dr_ * gb_ref[0], drn_ref[0] * gbn_ref[0] * keep_next], axis=0)
        dm = _conv3_t(dcv_ext, w_ref)[:ts]
        dgc_ref[0] = (dm * h).astype(BF16)
        dh_ref[0] = (dm * gc).astype(BF16)
        dcv = dcv_ext[:ts]
        for k in range(3):
            shifted = m_ext if k == 2 else _back(m_ext, 2 - k)
            dw_ref[k:k + 1, :] += jnp.sum(dcv * shifted[CONV_HALO:], axis=0, keepdims=True)

    def specs(col):
        per = ts // CONV_HALO
        tile = pl.BlockSpec((1, ts, tc), lambda j, b, i: (b, i, col(j)))
        prev = pl.BlockSpec((1, CONV_HALO, tc), lambda j, b, i: (b, jnp.maximum(i * per - 1, 0), col(j)))
        nxt = pl.BlockSpec((1, CONV_HALO, tc), lambda j, b, i: (b, jnp.minimum((i + 1) * per, n_s * per - 1), col(j)))
        return tile, prev, nxt

    tb, _, nb = specs(lambda j: j)
    tcc, pc_, _ = specs(lambda j: j + n_c)
    th, ph, _ = specs(lambda j: j + 2 * n_c)
    w = pl.BlockSpec((3, tc), lambda j, b, i: (0, j))
    out_tile = pl.BlockSpec((1, ts, tc), lambda j, b, i: (b, i, j))
    o = _sds((b, s, d), BF16)
    return _pc(body, name=name, grid=(n_c, b, n_s), out_shape=(o, o, o, _sds((3, d), F32)),
               in_specs=[tb, nb, tcc, pc_, th, ph, tb, nb, w], out_specs=(out_tile, out_tile, out_tile, w),
               compiler_params=_cp("parallel", "arbitrary", "arbitrary"))(q, q, q, q, q, q, dr, dr, cw)


def _pick_window(group, cands):
    gid = jnp.full(cands[0].shape, group, jnp.int32)
    out = cands[-1]
    for k in range(len(cands) - 2, -1, -1):
        out = jnp.where(gid == k, cands[k], out)
    return out


def _window_sums(v, shift):
    s1 = v + shift(v, 1)
    s2 = s1 + shift(s1, 2)
    s3 = s2 + shift(s2, 4)
    s4 = s3 + shift(s3, 8)
    return [s1, s2, s3, s4]


def _pool_counts(group, first_row, n_rows, cols):
    t = _rows_iota((n_rows, cols)) + first_row
    window = _pick_window(group, [jnp.full((n_rows, cols), 2 << k, jnp.int32) for k in range(POOL_GROUPS)])
    return jnp.minimum(t + 1, window).astype(F32)


def _pool_fwd(name, x, sc, sh, w, scale):
    b, s, d = x.shape
    tc = d // POOL_GROUPS
    ts = _div(s, 512, POOL_HALO)
    n_s = s // ts

    def body(x_ref, xp_ref, sc_ref, sh_ref, w_ref, scale_ref, y_ref):
        i, grp = pl.program_id(1), pl.program_id(2)
        keep = _flag(i > 0)
        mod = 1.0 + sc_ref[0]
        u = x_ref[0] * mod + sh_ref[0]
        u_ext = jnp.concatenate([(xp_ref[0] * mod + sh_ref[0]) * keep, u], axis=0)
        summed = _pick_window(grp, _window_sums(u_ext, _back))[POOL_HALO:]
        pooled = summed / _pool_counts(grp, i * ts, ts, tc) - u
        y_ref[0] = jnp.dot(pooled.astype(BF16), w_ref[0], preferred_element_type=F32) * scale_ref[...]

    tile, prev, _ = _halo_specs(ts, tc, POOL_HALO, n_s, lambda j: j)
    seq = pl.BlockSpec((1, 1, tc), lambda b, i, j: (b, 0, j))
    return _pc(body, name=name, grid=(b, n_s, POOL_GROUPS), out_shape=_sds(x.shape, F32),
               in_specs=[tile, prev, seq, seq, pl.BlockSpec((1, tc, tc), lambda b, i, j: (j, 0, 0)),
                         pl.BlockSpec((1, tc), lambda b, i, j: (0, j))],
               out_specs=pl.BlockSpec((1, ts, tc), lambda b, i, j: (b, i, j)),
               compiler_params=_cp("parallel", "parallel", "parallel"))(x, x, sc, sh, w, scale)


def _pool_bwd(name, x, sc, sh, dy, w, w_t, scale):
    b, s, d = x.shape
    tc = d // POOL_GROUPS
    ts = _div(s, 512, POOL_HALO)
    n_s = s // ts

    def body(x_ref, xp_ref, sc_ref, sh_ref, dy_ref, dyn_ref, w_ref, wt_ref, scale_ref, du_ref, dw_ref, dscale_ref):
        grp, bi, i = pl.program_id(0), pl.program_id(1), pl.program_id(2)

        @pl.when((bi == 0) & (i == 0))
        def _():
            dw_ref[...] = jnp.zeros_like(dw_ref)
            dscale_ref[...] = jnp.zeros_like(dscale_ref)

        keep_prev = _flag(i > 0)
        keep_next = _flag(i < n_s - 1)
        mod = 1.0 + sc_ref[0]
        u = x_ref[0] * mod + sh_ref[0]
        u_ext = jnp.concatenate([(xp_ref[0] * mod + sh_ref[0]) * keep_prev, u], axis=0)
        summed = _pick_window(grp, _window_sums(u_ext, _back))[POOL_HALO:]
        pooled = (summed / _pool_counts(grp, i * ts, ts, tc) - u).astype(BF16)
        dy_ = dy_ref[0].astype(F32)
        ymat = jnp.dot(pooled, w_ref[0], preferred_element_type=F32)
        dscale_ref[...] += jnp.sum(dy_ * ymat, axis=0, keepdims=True)
        dys_ext = (jnp.concatenate([dy_, dyn_ref[0].astype(F32) * keep_next], axis=0) * scale_ref[...]).astype(BF16)
        dw_ref[0] += lax.dot_general(pooled, dys_ext[:ts], (((0,), (0,)), ((), ())), preferred_element_type=F32)
        dpooled = jnp.dot(dys_ext, wt_ref[0], preferred_element_type=F32)
        e = dpooled / _pool_counts(grp, i * ts, ts + POOL_HALO, tc)
        du_ref[0] = _pick_window(grp, _window_sums(e, _ahead))[:ts] - dpooled[:ts]

    per = ts // POOL_HALO
    tile = pl.BlockSpec((1, ts, tc), lambda j, b, i: (b, i, j))
    prev = pl.BlockSpec((1, POOL_HALO, tc), lambda j, b, i: (b, jnp.maximum(i * per - 1, 0), j))
    nxt = pl.BlockSpec((1, POOL_HALO, tc), lambda j, b, i: (b, jnp.minimum((i + 1) * per, n_s * per - 1), j))
    seq = pl.BlockSpec((1, 1, tc), lambda j, b, i: (b, 0, j))
    wsp = pl.BlockSpec((1, tc, tc), lambda j, b, i: (j, 0, 0))
    vec = pl.BlockSpec((1, tc), lambda j, b, i: (0, j))
    return _pc(body, name=name, grid=(POOL_GROUPS, b, n_s),
               out_shape=(_sds(x.shape, F32), _sds((POOL_GROUPS, tc, tc), F32), _sds((1, d), F32)),
               in_specs=[tile, prev, seq, seq, tile, nxt, wsp, wsp, vec], out_specs=(tile, wsp, vec),
               compiler_params=_cp("parallel", "arbitrary", "arbitrary"))(x, x, sc, sh, dy, dy, w, w_t, scale)


def _rope_swap(v):
    lane = lax.broadcasted_iota(jnp.int32, v.shape, v.ndim - 1)
    lo, hi = QK_NOPE, QK_NOPE + QK_ROPE // 2
    from_above = pltpu.roll(v, HEAD_PAD - QK_ROPE // 2, axis=v.ndim - 1)
    from_below = pltpu.roll(v, QK_ROPE // 2, axis=v.ndim - 1)
    return jnp.where((lane >= lo) & (lane < hi), from_above,
                     jnp.where((lane >= hi) & (lane < hi + QK_ROPE // 2), from_below, 0.0))


def _rope(v, cos_t, sin_t):
    return v * cos_t + _rope_swap(v) * sin_t


def _rope_t(dv, cos_t, sin_t):
    return dv * cos_t + _rope_swap(dv * sin_t)


def _rms(v, g):
    r = lax.rsqrt(jnp.mean(v * v, axis=-1, keepdims=True) + RMS_EPS)
    return v * r, r


def _mla_norm_fwd(name, a, qn, kvn, cos_t, sin_t):
    b, s, wa = a.shape
    ql, kvl = qn.shape[1], kvn.shape[1]
    ts = _div(s, 512, 16)

    def body(aq_ref, akv_ref, ape_ref, qn_ref, kvn_ref, cos_ref, sin_ref, cq_ref, ckv_ref, kpe_ref):
        yq, _ = _rms(aq_ref[0], None)
        cq_ref[0] = (yq * qn_ref[...]).astype(BF16)
        ykv, _ = _rms(akv_ref[0], None)
        ckv_ref[0] = (ykv * kvn_ref[...]).astype(BF16)
        kpe_ref[0] = _rope(ape_ref[0], cos_ref[0], sin_ref[0])

    tok = lambda w, col: pl.BlockSpec((1, ts, w), lambda b, i: (b, i, col))
    return _pc(body, name=name, grid=(b, s // ts),
               out_shape=(_sds((b, s, ql), BF16), _sds((b, s, kvl), BF16), _sds((b, s, HEAD_PAD), F32)),
               in_specs=[tok(ql, 0), tok(kvl, ql // kvl), tok(HEAD_PAD, (ql + kvl) // HEAD_PAD), _vec_spec(ql),
                         _vec_spec(kvl), tok(HEAD_PAD, 0), tok(HEAD_PAD, 0)],
               out_specs=(tok(ql, 0), tok(kvl, 0), tok(HEAD_PAD, 0)),
               compiler_params=_cp("parallel", "parallel"))(a, a, a, qn, kvn, cos_t, sin_t)


def _mla_norm_bwd(name, a, dcq, dckv, dkpe, qn, kvn):
    b, s, wa = a.shape
    ql, kvl = qn.shape[1], kvn.shape[1]
    ts = _div(s, 512, 16)

    def body(a_ref, dcq_ref, dckv_ref, dkpe_ref, qn_ref, kvn_ref, da_ref, dqn_ref, dkvn_ref):
        @pl.when((pl.program_id(0) == 0) & (pl.program_id(1) == 0))
        def _():
            dqn_ref[...] = jnp.zeros_like(dqn_ref)
            dkvn_ref[...] = jnp.zeros_like(dkvn_ref)

        def one(v, dc, g_ref, dg_ref):
            yv, r = _rms(v, None)
            dg_ref[...] += jnp.sum(dc * yv, axis=0, keepdims=True)
            dyv = dc * g_ref[...]
            return r * (dyv - yv * jnp.mean(dyv * yv, axis=-1, keepdims=True))

        av = a_ref[0]
        da_ref[0, :, 0:ql] = one(av[:, 0:ql], dcq_ref[0], qn_ref, dqn_ref).astype(BF16)
        da_ref[0, :, ql:ql + kvl] = one(av[:, ql:ql + kvl], dckv_ref[0], kvn_ref, dkvn_ref).astype(BF16)
        da_ref[0, :, ql + kvl:] = dkpe_ref[0].astype(BF16)

    return _pc(body, name=name, grid=(b, s // ts),
               out_shape=(_sds(a.shape, BF16), _sds((1, ql), F32), _sds((1, kvl), F32)),
               in_specs=[_tok_spec(ts, wa), _tok_spec(ts, ql), _tok_spec(ts, kvl), _tok_spec(ts, HEAD_PAD),
                         _vec_spec(ql), _vec_spec(kvl)],
               out_specs=(_tok_spec(ts, wa), _vec_spec(ql), _vec_spec(kvl)),
               compiler_params=_cp("arbitrary", "arbitrary"))(a, dcq, dckv, dkpe, qn, kvn)


def _mla_prep_fwd(name, q_raw, kv_raw, kpe, cos_t, sin_t):
    b, s, h, _ = q_raw.shape
    ts = _div(s, 128, 16)

    def body(q_ref, k_ref, v_ref, kpe_ref, cos_ref, sin_ref, qo_ref, ko_ref, vo_ref):
        cos_b, sin_b = cos_ref[0][:, None, :], sin_ref[0][:, None, :]
        qo_ref[0] = _rope(q_ref[0], cos_b, sin_b).astype(BF16)
        ko_ref[0] = (k_ref[0] + kpe_ref[0][:, None, :]).astype(BF16)
        vo_ref[0] = v_ref[0].astype(BF16)

    heads = lambda part: pl.BlockSpec((1, ts, h, HEAD_PAD), lambda b, i: (b, i, part, 0))
    tok = pl.BlockSpec((1, ts, HEAD_PAD), lambda b, i: (b, i, 0))
    o = _sds(q_raw.shape, BF16)
    return _pc(body, name=name, grid=(b, s // ts), out_shape=(o, o, o),
               in_specs=[heads(0), heads(0), heads(1), tok, tok, tok], out_specs=(heads(0), heads(0), heads(0)),
               compiler_params=_cp("parallel", "parallel"))(q_raw, kv_raw, kv_raw, kpe, cos_t, sin_t)


def _mla_prep_bwd(name, dq, dk, dv, cos_t, sin_t):
    b, s, h, _ = dq.shape
    ts = _div(s, 128, 16)

    def body(dq_ref, dk_ref, dv_ref, cos_ref, sin_ref, dqr_ref, dkv_ref, dkpe_ref):
        cos_t_, sin_t_ = cos_ref[0], sin_ref[0]
        dqr_ref[0] = _rope_t(dq_ref[0], cos_t_[:, None, :], sin_t_[:, None, :]).astype(BF16)
        dk_ = dk_ref[0]
        dkv_ref[0, :, 0:h, :] = dk_.astype(BF16)
        dkv_ref[0, :, h:2 * h, :] = dv_ref[0]
        dkpe_ref[0] = _rope_t(jnp.sum(dk_, axis=1), cos_t_, sin_t_)

    heads = pl.BlockSpec((1, ts, h, HEAD_PAD), lambda b, i: (b, i, 0, 0))
    both = pl.BlockSpec((1, ts, 2 * h, HEAD_PAD), lambda b, i: (b, i, 0, 0))
    tok = pl.BlockSpec((1, ts, HEAD_PAD), lambda b, i: (b, i, 0))
    return _pc(body, name=name, grid=(b, s // ts),
               out_shape=(_sds(dq.shape, BF16), _sds((b, s, 2 * h, HEAD_PAD), BF16), _sds((b, s, HEAD_PAD), F32)),
               in_specs=[heads, heads, heads, tok, tok], out_specs=(heads, both, tok),
               compiler_params=_cp("parallel", "parallel"))(dq, dk, dv, cos_t, sin_t)


def _causal_mask(i, j, tq, tk):
    rows = lax.broadcasted_iota(jnp.int32, (tq, tk), 0) + i * tq
    cols = lax.broadcasted_iota(jnp.int32, (tq, tk), 1) + j * tk
    return cols <= rows


def _nt(a, b):
    return lax.dot_general(a, b, (((1,), (1,)), ((), ())), preferred_element_type=F32)


def _tn(a, b):
    return lax.dot_general(a, b, (((0,), (0,)), ((), ())), preferred_element_type=F32)


def _flash_fwd(name, q, k, v, n_heads, sm_scale):
    b, s, _ = q.shape
    t = _div(s, 512, LANE)
    n = s // t
    neg = float(jnp.finfo(jnp.float32).min)

    def body(q_ref, k_ref, v_ref, o_ref, lse_ref, m_ref, l_ref, acc_ref):
        i, j = pl.program_id(2), pl.program_id(3)

        @pl.when(j == 0)
        def _():
            m_ref[...] = jnp.full(m_ref.shape, neg, F32)
            l_ref[...] = jnp.zeros_like(l_ref)
            acc_ref[...] = jnp.zeros_like(acc_ref)

        @pl.when(j <= i)
        def _():
            sc = _nt(q_ref[0], k_ref[0]) * sm_scale
            sc = jnp.where(_causal_mask(i, j, t, t), sc, neg)
            m_old = m_ref[...]
            m_new = jnp.maximum(m_old, jnp.max(sc, axis=-1, keepdims=True))
            p = jnp.exp(sc - m_new)
            corr = jnp.exp(m_old - m_new)
            l_ref[...] = corr * l_ref[...] + jnp.sum(p, axis=-1, keepdims=True)
            acc_ref[...] = corr * acc_ref[...] + jnp.dot(p.astype(BF16), v_ref[0], preferred_element_type=F32)
            m_ref[...] = m_new

        @pl.when(j == n - 1)
        def _():
            o_ref[0] = (acc_ref[...] / l_ref[...]).astype(BF16)
            lse_ref[0] = jnp.broadcast_to(m_ref[...] + jnp.log(l_ref[...]), (t, LANE))

    qs = pl.BlockSpec((1, t, HEAD_PAD), lambda b, h, i, j: (b, i, h))
    ks = pl.BlockSpec((1, t, HEAD_PAD), lambda b, h, i, j: (b, jnp.minimum(j, i), h))
    return _pc(body, name=name, grid=(b, n_heads, n, n),
               out_shape=(_sds(q.shape, BF16), _sds(q.shape, F32)), in_specs=[qs, ks, ks], out_specs=(qs, qs),
               scratch_shapes=[pltpu.VMEM((t, 1), F32), pltpu.VMEM((t, 1), F32), pltpu.VMEM((t, HEAD_PAD), F32)],
               compiler_params=_cp("parallel", "parallel", "parallel", "arbitrary"))(q, k, v)


def _flash_dq(name, q, k, v, o, lse, do, n_heads, sm_scale):
    b, s, _ = q.shape
    t = _div(s, 512, LANE)
    n = s // t

    def body(q_ref, k_ref, v_ref, o_ref, lse_ref, do_ref, dq_ref, acc_ref, delta_ref):
        i, j = pl.program_id(2), pl.program_id(3)

        @pl.when(j == 0)
        def _():
            acc_ref[...] = jnp.zeros_like(acc_ref)
            delta_ref[...] = jnp.sum(do_ref[0].astype(F32) * o_ref[0].astype(F32), axis=-1, keepdims=True)

        @pl.when(j <= i)
        def _():
            sc = _nt(q_ref[0], k_ref[0]) * sm_scale
            p = jnp.where(_causal_mask(i, j, t, t), jnp.exp(sc - lse_ref[0][:, 0:1]), 0.0)
            dp = _nt(do_ref[0], v_ref[0])
            ds = p * (dp - delta_ref[...]) * sm_scale
            acc_ref[...] += jnp.dot(ds.astype(BF16), k_ref[0], preferred_element_type=F32)

        @pl.when(j == n - 1)
        def _():
            dq_ref[0] = acc_ref[...]

    qs = pl.BlockSpec((1, t, HEAD_PAD), lambda b, h, i, j: (b, i, h))
    ks = pl.BlockSpec((1, t, HEAD_PAD), lambda b, h, i, j: (b, jnp.minimum(j, i), h))
    return _pc(body, name=name, grid=(b, n_heads, n, n), out_shape=_sds(q.shape, F32),
               in_specs=[qs, ks, ks, qs, qs, qs], out_specs=qs,
               scratch_shapes=[pltpu.VMEM((t, HEAD_PAD), F32), pltpu.VMEM((t, 1), F32)],
               compiler_params=_cp("parallel", "parallel", "parallel", "arbitrary"))(q, k, v, o, lse, do)


def _flash_dkv(name, q, k, v, o, lse, do, n_heads, sm_scale):
    b, s, _ = q.shape
    t = _div(s, 512, LANE)
    n = s // t

    def body(q_ref, k_ref, v_ref, o_ref, lse_ref, do_ref, dk_ref, dv_ref, dk_acc, dv_acc):
        j, i = pl.program_id(2), pl.program_id(3)

        @pl.when(i == 0)
        def _():
            dk_acc[...] = jnp.zeros_like(dk_acc)
            dv_acc[...] = jnp.zeros_like(dv_acc)

        @pl.when(i >= j)
        def _():
            do_ = do_ref[0]
            delta = jnp.sum(do_.astype(F32) * o_ref[0].astype(F32), axis=-1, keepdims=True)
            sc = _nt(q_ref[0], k_ref[0]) * sm_scale
            p = jnp.where(_causal_mask(i, j, t, t), jnp.exp(sc - lse_ref[0][:, 0:1]), 0.0)
            dv_acc[...] += _tn(p.astype(BF16), do_)
            dp = _nt(do_, v_ref[0])
            ds = p * (dp - delta) * sm_scale
            dk_acc[...] += _tn(ds.astype(BF16), q_ref[0])

        @pl.when(i == n - 1)
        def _():
            dk_ref[0] = dk_acc[...]
            dv_ref[0] = dv_acc[...].astype(BF16)

    qs = pl.BlockSpec((1, t, HEAD_PAD), lambda b, h, j, i: (b, jnp.maximum(i, j), h))
    ks = pl.BlockSpec((1, t, HEAD_PAD), lambda b, h, j, i: (b, j, h))
    return _pc(body, name=name, grid=(b, n_heads, n, n), out_shape=(_sds(q.shape, F32), _sds(q.shape, BF16)),
               in_specs=[qs, ks, ks, qs, qs, qs], out_specs=(ks, ks),
               scratch_shapes=[pltpu.VMEM((t, HEAD_PAD), F32), pltpu.VMEM((t, HEAD_PAD), F32)],
               compiler_params=_cp("parallel", "parallel", "parallel", "arbitrary"))(q, k, v, o, lse, do)


def _mod_fwd(name, c_all, w, bias):
    depth, d, n = w.shape
    rows = c_all.shape[0]

    def body(c_ref, w_ref, b_ref, o_ref):
        cv = c_ref[...]
        cond = (cv * _sigmoid(cv)).astype(BF16)
        o_ref[0] = jnp.dot(cond, w_ref[0].astype(BF16), preferred_element_type=F32) + b_ref[0]

    return _pc(body, name=name, grid=(depth,), out_shape=_sds((depth, rows, n), F32),
               in_specs=[pl.BlockSpec((rows, d), lambda l: (0, 0)), pl.BlockSpec((1, d, n), lambda l: (l, 0, 0)),
                         pl.BlockSpec((1, 1, n), lambda l: (l, 0, 0))],
               out_specs=pl.BlockSpec((1, rows, n), lambda l: (l, 0, 0)), compiler_params=_cp("parallel"))(c_all, w, bias)


def _mod_bwd(name, c_all, dmod_cols, dmod_all):
    depth, rows, n = dmod_cols.shape
    d = c_all.shape[1]
    n_all = dmod_all.shape[2]
    tn = _div(n, 512, LANE)

    def body(c_ref, dm_ref, dma_ref, gw_ref, gb_ref):
        cv = c_ref[...]
        cond = (cv * _sigmoid(cv)).astype(BF16)
        gw_ref[0] = _tn(cond, dm_ref[0].astype(BF16))

        @pl.when(pl.program_id(1) == 0)
        def _():
            gb_ref[0] = jnp.sum(dma_ref[0], axis=0, keepdims=True)

    return _pc(body, name=name, grid=(depth, n // tn),
               out_shape=(_sds((depth, d, n), F32), _sds((depth, 1, n_all), F32)),
               in_specs=[pl.BlockSpec((rows, d), lambda l, j: (0, 0)), pl.BlockSpec((1, rows, tn), lambda l, j: (l, 0, j)),
                         pl.BlockSpec((1, rows, n_all), lambda l, j: (l, 0, 0))],
               out_specs=(pl.BlockSpec((1, d, tn), lambda l, j: (l, 0, j)), pl.BlockSpec((1, 1, n_all), lambda l, j: (l, 0, 0))),
               compiler_params=_cp("parallel", "arbitrary"))(c_all, dmod_cols, dmod_all)


def _adamw(name, w, g, m, v):
    shape = w.shape
    cols = shape[-1]
    rows = _size(shape) // cols
    tr = _div(rows, max(SUBLANE, (2 ** 19) // cols // SUBLANE * SUBLANE), SUBLANE)
    c1 = 1.0 - ADAM_B1 ** ADAM_STEP
    c2 = 1.0 - ADAM_B2 ** ADAM_STEP

    def body(w_ref, g_ref, m_ref, v_ref, d_ref, mo_ref, vo_ref):
        gv = g_ref[...]
        m_new = ADAM_B1 * m_ref[...] + (1.0 - ADAM_B1) * gv
        v_new = ADAM_B2 * v_ref[...] + (1.0 - ADAM_B2) * (gv * gv)
        m_hat = m_new / c1
        v_hat = v_new / c2
        d_ref[...] = -ADAM_LR * (m_hat / (jnp.sqrt(v_hat) + ADAM_EPS) + ADAM_WD * w_ref[...])
        mo_ref[...] = m_new
        vo_ref[...] = v_new

    spec = pl.BlockSpec((tr, cols), lambda i: (i, 0))
    o = _sds((rows, cols), F32)
    outs = _pc(body, name=name, grid=(rows // tr,), out_shape=(o, o, o), in_specs=[spec] * 4, out_specs=(spec,) * 3,
               compiler_params=_cp("parallel"))(*[a.reshape(rows, cols) for a in (w, g, m, v)])
    return tuple(a.reshape(shape) for a in outs)


def _rope_tables(positions):
    half = QK_ROPE // 2
    inv_freq = ROPE_THETA ** (-jnp.arange(0, QK_ROPE, 2, dtype=F32) / QK_ROPE)
    ang = positions.astype(F32)[..., None] * inv_freq
    cos, sin = jnp.cos(ang), jnp.sin(ang)
    lead = positions.shape
    ones = jnp.ones(lead + (QK_NOPE,), F32)
    tail_one = jnp.ones(lead + (HEAD_PAD - QK_NOPE - QK_ROPE,), F32)
    cos_t = jnp.concatenate([ones, cos, cos, tail_one], axis=-1)
    sin_t = jnp.concatenate([0 * ones, -sin, sin, 0 * tail_one], axis=-1)
    return cos_t, sin_t


def _pad_heads(w, n_heads, parts, axis):
    w = jnp.moveaxis(w, axis, -1)
    lead = w.shape[:-1]
    per = w.shape[-1] // n_heads
    w = w.reshape(lead + (n_heads, per))
    kept = jnp.concatenate([w[..., a:b_] for a, b_ in parts], axis=-1)
    pad = HEAD_PAD - kept.shape[-1]
    kept = jnp.concatenate([kept, jnp.zeros(lead + (n_heads, pad), w.dtype)], axis=-1)
    return jnp.moveaxis(kept.reshape(lead + (n_heads * HEAD_PAD,)), -1, axis)


def _unpad_heads(g, n_heads, width, axis):
    g = jnp.moveaxis(g, axis, -1)
    lead = g.shape[:-1]
    g = g.reshape(lead + (n_heads, HEAD_PAD))[..., :width]
    return g, lead


def kernel(x, c, positions, mod_w, mod_b, ln_g, ln_b, pool_w, pool_scale, mla_w_a, mla_q_norm, mla_w_uq, mla_kv_norm, mla_w_ukv, mla_w_o, sc_w_in, sc_conv, sc_w_out, ffn_w_up, ffn_conv, ffn_conv_b, ffn_w_down, loss_target, m_mod_w, m_mod_b, m_ln_g, m_ln_b, m_pool_w, m_pool_scale, m_mla_w_a, m_mla_q_norm, m_mla_w_uq, m_mla_kv_norm, m_mla_w_ukv, m_mla_w_o, m_sc_w_in, m_sc_conv, m_sc_w_out, m_ffn_w_up, m_ffn_conv, m_ffn_conv_b, m_ffn_w_down, v_mod_w, v_mod_b, v_ln_g, v_ln_b, v_pool_w, v_pool_scale, v_mla_w_a, v_mla_q_norm, v_mla_w_uq, v_mla_kv_norm, v_mla_w_ukv, v_mla_w_o, v_sc_w_in, v_sc_conv, v_sc_w_out, v_ffn_w_up, v_ffn_conv, v_ffn_conv_b, v_ffn_w_down):
    wts = dict(mod_w=mod_w, mod_b=mod_b, ln_g=ln_g, ln_b=ln_b, pool_w=pool_w, pool_scale=pool_scale, mla_w_a=mla_w_a,
               mla_q_norm=mla_q_norm, mla_w_uq=mla_w_uq, mla_kv_norm=mla_kv_norm, mla_w_ukv=mla_w_ukv, mla_w_o=mla_w_o,
               sc_w_in=sc_w_in, sc_conv=sc_conv, sc_w_out=sc_w_out, ffn_w_up=ffn_w_up, ffn_conv=ffn_conv,
               ffn_conv_b=ffn_conv_b, ffn_w_down=ffn_w_down)
    mom1 = dict(mod_w=m_mod_w, mod_b=m_mod_b, ln_g=m_ln_g, ln_b=m_ln_b, pool_w=m_pool_w, pool_scale=m_pool_scale,
                mla_w_a=m_mla_w_a, mla_q_norm=m_mla_q_norm, mla_w_uq=m_mla_w_uq, mla_kv_norm=m_mla_kv_norm,
                mla_w_ukv=m_mla_w_ukv, mla_w_o=m_mla_w_o, sc_w_in=m_sc_w_in, sc_conv=m_sc_conv, sc_w_out=m_sc_w_out,
                ffn_w_up=m_ffn_w_up, ffn_conv=m_ffn_conv, ffn_conv_b=m_ffn_conv_b, ffn_w_down=m_ffn_w_down)
    mom2 = dict(mod_w=v_mod_w, mod_b=v_mod_b, ln_g=v_ln_g, ln_b=v_ln_b, pool_w=v_pool_w, pool_scale=v_pool_scale,
                mla_w_a=v_mla_w_a, mla_q_norm=v_mla_q_norm, mla_w_uq=v_mla_w_uq, mla_kv_norm=v_mla_kv_norm,
                mla_w_ukv=v_mla_w_ukv, mla_w_o=v_mla_w_o, sc_w_in=v_sc_w_in, sc_conv=v_sc_conv, sc_w_out=v_sc_w_out,
                ffn_w_up=v_ffn_w_up, ffn_conv=v_ffn_conv, ffn_conv_b=v_ffn_conv_b, ffn_w_down=v_ffn_w_down)

    bsz, seq, d = x.shape
    depth = mod_b.shape[0]
    n_tok = bsz * seq
    n_heads = d // V_HEAD
    ql, kvl = mla_q_norm.shape[1], mla_kv_norm.shape[1]
    alpha = float((2 * depth) ** 0.25)
    sm_scale = float((QK_NOPE + QK_ROPE) ** -0.5)
    mx, my, mc = lax.axis_index("x"), lax.axis_index("y"), lax.axis_index("c")
    chip = 2 * mx + my
    dev = 2 * chip + mc

    small_names = list(SMALL_SHARDED)
    small_pack, small_spans = _pack_rows([c] + [wts[n] for n in small_names], F32, SUBLANE)
    rows_small = small_pack.shape[0]
    small_all = _all_gather8("gather_small_params", small_pack, True).reshape(N_DEV, rows_small * PACK_COLS)
    c_all = small_all[:, :c.size].reshape(N_DEV * bsz, d)
    per_chip = small_all[0::2]
    full = dict(wts)
    for n, (off, shape) in zip(small_names, small_spans[1:]):
        blocks = per_chip[:, off:off + _size(shape)].reshape((N_CHIP,) + tuple(shape))
        full[n] = _join_chips(blocks, SMALL_SHARDED[n])

    n_mod = mod_w.shape[2]
    bias_cols = lax.dynamic_slice_in_dim(mod_b, chip * n_mod, n_mod, axis=1)[:, None, :]
    mod_cols = _mod_fwd("mod_fwd", c_all, mod_w, bias_cols)
    half_rows = (N_DEV * bsz) // 2
    mod_half = lax.dynamic_slice_in_dim(mod_cols, mc * half_rows, half_rows, axis=1).reshape(depth * half_rows, n_mod)
    mod_all = _all_gather8("gather_mod", mod_half, True).reshape(N_CHIP, 2, depth, half_rows, n_mod)
    mod_all = jnp.transpose(mod_all, (2, 1, 3, 0, 4)).reshape(depth, N_DEV * bsz, N_CHIP * n_mod)
    mod_mine = lax.dynamic_slice_in_dim(mod_all, dev * bsz, bsz, axis=1)
    mods = [[mod_mine[l, :, k * d:(k + 1) * d][:, None, :] for k in range(6)] for l in range(depth)]

    big_names = list(BIG)
    big_pack, big_spans = _pack_rows([wts[n] for n in big_names], BF16, 2 * 16)
    half_big = big_pack.shape[0] // 2
    my_half = lax.dynamic_slice_in_dim(big_pack, mc * half_big, half_big, axis=0)
    big_all = _all_gather8("gather_weights", my_half, False).reshape(N_CHIP, 2 * half_big * PACK_COLS)
    for n, (off, shape) in zip(big_names, big_spans):
        blocks = big_all[:, off:off + _size(shape)].reshape((N_CHIP,) + tuple(shape))
        full[n] = _join_chips(blocks, BIG[n])

    nope_rope = [(0, QK_NOPE + QK_ROPE)]
    cos_t, sin_t = _rope_tables(positions)

    def t2(a):
        return a.reshape(n_tok, a.shape[-1])

    def t3(a):
        return a.reshape(bsz, seq, a.shape[-1])

    saved = []
    xin = x
    u = _modulate("modulate_in", x, mods[0][1], mods[0][0])
    loss_acc = None
    for l in range(depth):
        sh1, sc1, g1, sh2, sc2, g2 = mods[l]
        kind, j = l % 3, l // 3
        st = dict(x=xin)
        if kind == 0:
            w = full['pool_w'][j]
            st.update(w=w, w_t=jnp.swapaxes(w, 1, 2), scale=full['pool_scale'][j][None, :])
            y = _pool_fwd(f"pool_fwd_{l}", xin, sc1, sh1, st['w'], st['scale'])
        elif kind == 1:
            wa = full['mla_w_a'][j]
            zeros = jnp.zeros((d, QK_NOPE), BF16)
            w_a = jnp.concatenate([wa[:, :ql + kvl], zeros, wa[:, ql + kvl:], zeros[:, :HEAD_PAD - QK_NOPE - QK_ROPE]], axis=1)
            w_uq = _pad_heads(full['mla_w_uq'][j], n_heads, nope_rope, 1)
            w_kv = jnp.concatenate([_pad_heads(full['mla_w_ukv'][j], n_heads, [(0, QK_NOPE)], 1),
                                    _pad_heads(full['mla_w_ukv'][j], n_heads, [(QK_NOPE, QK_NOPE + V_HEAD)], 1)], axis=1)
            w_o = _pad_heads(full['mla_w_o'][j], n_heads, [(0, V_HEAD)], 0)
            qn, kvn = mla_q_norm[j][None, :], mla_kv_norm[j][None, :]
            a = t3(_mm_nn(f"mla_a_{l}", [(t2(u), w_a)], F32))
            cq, ckv, kpe = _mla_norm_fwd(f"mla_norm_fwd_{l}", a, qn, kvn, cos_t, sin_t)
            q_raw = _mm_nn(f"mla_q_{l}", [(t2(cq), w_uq)], F32).reshape(bsz, seq, n_heads, HEAD_PAD)
            kv_raw = _mm_nn(f"mla_kv_{l}", [(t2(ckv), w_kv)], F32).reshape(bsz, seq, 2 * n_heads, HEAD_PAD)
            qh, kh, vh = [t.reshape(bsz, seq, n_heads * HEAD_PAD)
                          for t in _mla_prep_fwd(f"mla_prep_fwd_{l}", q_raw, kv_raw, kpe, cos_t, sin_t)]
            o, lse = _flash_fwd(f"flash_fwd_{l}", qh, kh, vh, n_heads, sm_scale)
            y = t3(_mm_nn(f"mla_o_{l}", [(t2(o), w_o)], F32))
            st.update(u=u, w_a=w_a, w_uq=w_uq, w_kv=w_kv, w_o=w_o, qn=qn, kvn=kvn, a=a, cq=cq, ckv=ckv,
                      qh=qh, kh=kh, vh=vh, o=o, lse=lse)
        else:
            w_in, w_out, cw = full['sc_w_in'][j], full['sc_w_out'][j], full['sc_conv'][j]
            q = t3(_mm_nn(f"sc_in_{l}", [(t2(u), w_in)], F32))
            r = _shortconv_fwd(f"shortconv_fwd_{l}", q, cw)
            y = t3(_mm_nn(f"sc_out_{l}", [(t2(r), w_out)], F32))
            st.update(u=u, w_in=w_in, w_out=w_out, cw=cw, q=q, r=r)
        lng, lnb = full['ln_g'][l], full['ln_b'][l]
        z1, xmid, u2 = _ln_mod_fwd(f"ln_mod_a_{l}", alpha, xin, y, g1, lng[0:1], lnb[0:1], sc2, sh2)
        w_up, w_down = full['ffn_w_up'][l], full['ffn_w_down'][l]
        cwf, cbf = full['ffn_conv'][l], ffn_conv_b[l][None, :]
        p = t3(_mm_nn(f"ffn_up_{l}", [(t2(u2), w_up)], F32))
        act = _convglu_fwd(f"convglu_fwd_{l}", p, cwf, cbf)
        y2 = t3(_mm_nn(f"ffn_down_{l}", [(t2(act), w_down)], F32))
        st.update(y1=y, z1=z1, xmid=xmid, u2=u2, p=p, act=act, y2=y2, w_up=w_up, w_down=w_down, cwf=cwf, cbf=cbf,
                  lng=lng, lnb=lnb)
        if l + 1 < depth:
            nsh1, nsc1 = mods[l + 1][0], mods[l + 1][1]
            z2, xin, u = _ln_mod_fwd(f"ln_mod_b_{l}", alpha, xmid, y2, g2, lng[1:2], lnb[1:2], nsc1, nsh1)
        else:
            z2, ct, loss_acc = _ln_loss_fwd("ln_loss", alpha, xmid, y2, g2, lng[1:2], lnb[1:2], loss_target)
        st.update(z2=z2)
        saved.append(st)
    loss = lax.psum(loss_acc[0, 0], ("x", "y", "c"))

    grads = {}
    dmods = [[None] * 6 for _ in range(depth)]
    g_ln_g = [[None, None] for _ in range(depth)]
    g_ln_b = [[None, None] for _ in range(depth)]
    stack = {n: [None] * wts[n].shape[0] for n in WEIGHTS if n not in ('mod_w', 'mod_b', 'ln_g', 'ln_b')}
    upstream = (ct,)
    for l in reversed(range(depth)):
        st = saved[l]
        sh1, sc1, g1, sh2, sc2, g2 = mods[l]
        kind, j = l % 3, l // 3
        res = _sub_bwd(f"sub_bwd_b_{l}", alpha, upstream, st['z2'], st['y2'], g2, st['lng'][1:2])
        dz2, dy2, dmods[l][5], g_ln_g[l][1], g_ln_b[l][1] = res[:5]
        if l + 1 < depth:
            dmods[l + 1][1], dmods[l + 1][0] = res[5], res[6]
        dy2f = t2(dy2)
        da = t3(_mm_nn(f"ffn_down_bwd_{l}", [(dy2f, jnp.swapaxes(st['w_down'], 0, 1))], F32))
        stack['ffn_w_down'][l] = _mm_tn(f"ffn_down_dw_{l}", t2(st['act']), dy2f)
        dpv, dpg, dcw, dcb = _convglu_bwd(f"convglu_bwd_{l}", st['p'], da, st['cwf'], st['cbf'])
        stack['ffn_conv'][l], stack['ffn_conv_b'][l] = dcw, dcb[0]
        w_up_t = jnp.swapaxes(st['w_up'], 0, 1)
        f = w_up_t.shape[0] // 2
        du2 = t3(_mm_nn(f"ffn_up_bwd_{l}", [(t2(dpv), w_up_t[:f]), (t2(dpg), w_up_t[f:])], F32))
        u2f = t2(st['u2'])
        stack['ffn_w_up'][l] = jnp.concatenate([_mm_tn(f"ffn_up_dw_val_{l}", u2f, t2(dpv)),
                                                _mm_tn(f"ffn_up_dw_gate_{l}", u2f, t2(dpg))], axis=1)
        res = _sub_bwd(f"sub_bwd_a_{l}", alpha, (dz2, du2, st['xmid'], sc2), st['z1'], st['y1'], g1, st['lng'][0:1])
        dz1, dy1, dmods[l][2], g_ln_g[l][0], g_ln_b[l][0], dmods[l][4], dmods[l][3] = res
        dy1f = t2(dy1)
        if kind == 0:
            du1, dw, dscale = _pool_bwd(f"pool_bwd_{l}", st['x'], sc1, sh1, dy1, st['w'], st['w_t'], st['scale'])
            stack['pool_w'][j], stack['pool_scale'][j] = dw, dscale[0]
        elif kind == 1:
            do = t3(_mm_nn(f"mla_o_bwd_{l}", [(dy1f, jnp.swapaxes(st['w_o'], 0, 1))], BF16))
            gwo, _ = _unpad_heads(_mm_tn(f"mla_o_dw_{l}", t2(st['o']), dy1f), n_heads, V_HEAD, 0)
            stack['mla_w_o'][j] = jnp.moveaxis(gwo.reshape(d, n_heads * V_HEAD), -1, 0)
            fa = (st['qh'], st['kh'], st['vh'], st['o'], st['lse'], do, n_heads, sm_scale)
            dq = _flash_dq(f"flash_dq_{l}", *fa)
            dk, dv = _flash_dkv(f"flash_dkv_{l}", *fa)
            hs = (bsz, seq, n_heads, HEAD_PAD)
            dq_raw, dkv_raw, dkpe = _mla_prep_bwd(f"mla_prep_bwd_{l}", dq.reshape(hs), dk.reshape(hs), dv.reshape(hs),
                                                  cos_t, sin_t)
            dq_raw = dq_raw.reshape(n_tok, n_heads * HEAD_PAD)
            dkv_raw = dkv_raw.reshape(n_tok, 2 * n_heads * HEAD_PAD)
            dcq = t3(_mm_nn(f"mla_q_bwd_{l}", [(dq_raw, jnp.swapaxes(st['w_uq'], 0, 1))], F32))
            dckv = t3(_mm_nn(f"mla_kv_bwd_{l}", [(dkv_raw, jnp.swapaxes(st['w_kv'], 0, 1))], F32))
            gq, _ = _unpad_heads(_mm_tn(f"mla_q_dw_{l}", t2(st['cq']), dq_raw), n_heads, QK_NOPE + QK_ROPE, 1)
            stack['mla_w_uq'][j] = gq.reshape(ql, n_heads * (QK_NOPE + QK_ROPE))
            gkv = _mm_tn(f"mla_kv_dw_{l}", t2(st['ckv']), dkv_raw)
            gk, _ = _unpad_heads(gkv[:, :n_heads * HEAD_PAD], n_heads, QK_NOPE, 1)
            gv, _ = _unpad_heads(gkv[:, n_heads * HEAD_PAD:], n_heads, V_HEAD, 1)
            stack['mla_w_ukv'][j] = jnp.concatenate([gk, gv], axis=-1).reshape(kvl, n_heads * (QK_NOPE + V_HEAD))
            da_, dqn, dkvn = _mla_norm_bwd(f"mla_norm_bwd_{l}", st['a'], dcq, dckv, dkpe, st['qn'], st['kvn'])
            stack['mla_q_norm'][j], stack['mla_kv_norm'][j] = dqn[0], dkvn[0]
            du1 = t3(_mm_nn(f"mla_a_bwd_{l}", [(t2(da_), jnp.swapaxes(st['w_a'], 0, 1))], F32))
            gwa = _mm_tn(f"mla_a_dw_{l}", t2(st['u']), t2(da_))
            stack['mla_w_a'][j] = jnp.concatenate(
                [gwa[:, :ql + kvl], gwa[:, ql + kvl + QK_NOPE:ql + kvl + QK_NOPE + QK_ROPE]], axis=1)
        else:
            dr = t3(_mm_nn(f"sc_out_bwd_{l}", [(dy1f, jnp.swapaxes(st['w_out'], 0, 1))], F32))
            stack['sc_w_out'][j] = _mm_tn(f"sc_out_dw_{l}", t2(st['r']), dy1f)
            dgb, dgc, dh, dcw = _shortconv_bwd(f"shortconv_bwd_{l}", st['q'], dr, st['cw'])
            stack['sc_conv'][j] = dcw
            w_in_t = jnp.swapaxes(st['w_in'], 0, 1)
            parts = [t2(dgb), t2(dgc), t2(dh)]
            du1 = t3(_mm_nn(f"sc_in_bwd_{l}", [(parts[k], w_in_t[k * d:(k + 1) * d]) for k in range(3)], F32))
            uf = t2(st['u'])
            stack['sc_w_in'][j] = jnp.concatenate(
                [_mm_tn(f"sc_in_dw_{k}_{l}", uf, parts[k]) for k in range(3)], axis=1)
        upstream = (dz1, du1, st['x'], sc1)
    grad_x, dmods[0][1], dmods[0][0] = _input_bwd("input_bwd", alpha, upstream[0], upstream[1], x, mods[0][1])

    for n, parts in stack.items():
        grads[n] = jnp.stack(parts)
    grads['ln_g'] = jnp.stack([jnp.concatenate(r, axis=0) for r in g_ln_g])
    grads['ln_b'] = jnp.stack([jnp.concatenate(r, axis=0) for r in g_ln_b])
    dmod_mine = jnp.stack([jnp.concatenate([t[:, 0, :] for t in dmods[l]], axis=-1) for l in range(depth)])

    small_grad_names = small_names + ['mla_q_norm', 'mla_kv_norm', 'ffn_conv_b']
    sg_pack, sg_spans = _pack_rows([dmod_mine] + [grads[n] for n in small_grad_names], F32, SUBLANE)
    rows_sg = sg_pack.shape[0]
    sg_all = _all_gather8("gather_small_grads", sg_pack, True).reshape(N_DEV, rows_sg, PACK_COLS)
    dmod_all = sg_all.reshape(N_DEV, -1)[:, :dmod_mine.size].reshape(N_DEV, depth, bsz, 6 * d)
    dmod_all = jnp.transpose(dmod_all, (1, 0, 2, 3)).reshape(depth, N_DEV * bsz, 6 * d)
    sg_sum = _sum8("sum_small_grads", sg_all).reshape(-1)
    for n, (off, shape) in zip(small_grad_names, sg_spans[1:]):
        g_full = sg_sum[off:off + _size(shape)].reshape(shape)
        if n in SMALL_SHARDED:
            ax = SMALL_SHARDED[n]
            width = shape[ax] // N_CHIP
            g_full = lax.dynamic_slice_in_dim(g_full, chip * width, width, axis=ax)
        grads[n] = g_full
    dmod_cols = lax.dynamic_slice_in_dim(dmod_all, chip * n_mod, n_mod, axis=2)
    grads['mod_w'], gb = _mod_bwd("mod_bwd", c_all, dmod_cols, dmod_all)
    grads['mod_b'] = gb[:, 0, :]

    pieces = []
    for n in big_names:
        blocks = _split_chips(grads[n], BIG[n])
        pieces.append(blocks.reshape(N_DEV, -1))
    piece_len = sum(p.shape[1] for p in pieces)
    quantum = 256 * PACK_COLS
    padded = -(-piece_len // quantum) * quantum
    if padded > piece_len:
        pieces.append(jnp.zeros((N_DEV, padded - piece_len), F32))
    rows_piece = padded // PACK_COLS
    send = jnp.concatenate(pieces, axis=1).reshape(N_DEV * rows_piece, PACK_COLS)
    recv = _all_to_all8("scatter_big_grads", send).reshape(N_DEV, rows_piece, PACK_COLS)
    reduced = _sum8("sum_big_grads", recv)
    both = _sibling_gather("swap_big_grad_halves", reduced).reshape(2, padded)
    off = 0
    for n in big_names:
        shard_shape = wts[n].shape
        half = _size(shard_shape) // 2
        grads[n] = both[:, off:off + half].reshape(shard_shape)
        off += half

    deltas, new_m, new_v = {}, {}, {}
    for n in WEIGHTS:
        deltas[n], new_m[n], new_v[n] = _adamw(f"adamw_{n}", wts[n], grads[n], mom1[n], mom2[n])
    return (loss, grad_x, *[grads[n] for n in WEIGHTS], *[deltas[n] for n in WEIGHTS],
            *[new_m[n] for n in WEIGHTS], *[new_v[n] for n in WEIGHTS])
```

```python
import functools

import jax
import jax.numpy as jnp
from jax import lax
from jax.experimental import pallas as pl
from jax.experimental.pallas import tpu as pltpu

F32 = jnp.float32
BF16 = jnp.bfloat16
MESH = pl.DeviceIdType.MESH

N_DEV = 8
N_CHIP = 4
LANE = 128
SUBLANE = 8
VMEM_LIMIT_BYTES = 56 * 2 ** 20
PACK_COLS = 1024

LN_EPS = 1e-5
RMS_EPS = 1e-6
QK_NOPE, QK_ROPE, V_HEAD = 64, 32, 64
ROPE_THETA = 10000.0
HEAD_PAD = 128
POOL_GROUPS = 4
POOL_HALO = 16
CONV_HALO = 8
ADAM_LR, ADAM_B1, ADAM_B2, ADAM_EPS, ADAM_WD, ADAM_STEP = 0.001, 0.9, 0.999, 1e-08, 0.01, 10

WEIGHTS = ['mod_w', 'mod_b', 'ln_g', 'ln_b', 'pool_w', 'pool_scale', 'mla_w_a', 'mla_q_norm', 'mla_w_uq',
           'mla_kv_norm', 'mla_w_ukv', 'mla_w_o', 'sc_w_in', 'sc_conv', 'sc_w_out', 'ffn_w_up', 'ffn_conv',
           'ffn_conv_b', 'ffn_w_down']
BIG = {'pool_w': 2, 'mla_w_a': 2, 'mla_w_uq': 2, 'mla_w_ukv': 2, 'mla_w_o': 1, 'sc_w_in': 2, 'sc_w_out': 1,
       'ffn_w_up': 2, 'ffn_w_down': 1}
SMALL_SHARDED = {'ln_g': 2, 'ln_b': 2, 'pool_scale': 1, 'sc_conv': 2, 'ffn_conv': 2}
REPLICATED = ['mod_b', 'mla_q_norm', 'mla_kv_norm', 'ffn_conv_b']


def _pc(body, **kw):
    return pl.pallas_call(body, **kw)


def _cp(*sem):
    return pltpu.CompilerParams(dimension_semantics=sem, vmem_limit_bytes=VMEM_LIMIT_BYTES)


def _div(n, cap, mult):
    best = None
    for d in range(mult, min(n, cap) + 1, mult):
        if n % d == 0:
            best = d
    return best if best is not None else n


def _sds(shape, dtype):
    return jax.ShapeDtypeStruct(tuple(shape), dtype)


def _flip(v, bit):
    return 1 - v if bit else v


def _all_gather8(name, x_shard, in_vmem):
    m_per, n = x_shard.shape
    space = pltpu.VMEM if in_vmem else pltpu.HBM

    def body(x_ref, out_ref, send_sems, recv_sems, local_sem):
        x, y, c = lax.axis_index("x"), lax.axis_index("y"), lax.axis_index("c")
        me, sibling = (x, y, c), (x, y, 1 - c)
        chips = [(1 - x, y), (x, 1 - y), (1 - x, 1 - y)]

        def rows(px, py, pc_):
            return out_ref.at[pl.ds((4 * px + 2 * py + pc_) * m_per, m_per), :]

        def copy(k, block, to, src=None):
            return pltpu.make_async_remote_copy(
                src_ref=rows(*block) if src is None else src, dst_ref=rows(*block),
                send_sem=send_sems.at[k], recv_sem=recv_sems.at[k], device_id=to, device_id_type=MESH)

        mine = pltpu.make_async_copy(x_ref, rows(*me), local_sem)
        mine.start()
        first = [copy(0, me, sibling, src=x_ref)]
        first += [copy(1 + j, me, (*chip, c), src=x_ref) for j, chip in enumerate(chips)]
        for cp in first:
            cp.start()
        passed = [copy(4 + j, (*chip, c), sibling) for j, chip in enumerate(chips)]
        for j, chip in enumerate(chips):
            copy(1 + j, (*chip, c), me).wait_recv()
            passed[j].start()
        copy(0, sibling, me).wait_recv()
        for j, chip in enumerate(chips):
            copy(4 + j, (*chip, 1 - c), me).wait_recv()
        for cp in first + passed:
            cp.wait_send()
        mine.wait()

    return _pc(
        body, name=name, out_shape=_sds((N_DEV * m_per, n), x_shard.dtype),
        in_specs=[pl.BlockSpec(memory_space=space)], out_specs=pl.BlockSpec(memory_space=space),
        scratch_shapes=[pltpu.SemaphoreType.DMA((7,)), pltpu.SemaphoreType.DMA((7,)), pltpu.SemaphoreType.DMA],
        compiler_params=pltpu.CompilerParams(vmem_limit_bytes=VMEM_LIMIT_BYTES),
    )(x_shard)


def _gather_weights(name, shards):
    n_t = len(shards)
    halves = [s.shape[0] // 2 for s in shards]

    def body(*refs):
        x_refs, o_refs = refs[:n_t], refs[n_t:2 * n_t]
        send_sems, recv_sems, local_sems = refs[2 * n_t:]
        x, y, c = lax.axis_index("x"), lax.axis_index("y"), lax.axis_index("c")
        me, sibling = (x, y, c), (x, y, 1 - c)
        chips = [(1 - x, y), (x, 1 - y), (1 - x, 1 - y)]

        def slot(t, px, py, pc_):
            return o_refs[t].at[4 * px + 2 * py + pc_]

        def my_rows(t):
            return x_refs[t].at[pl.ds(c * halves[t], halves[t]), :]

        def copy(t, k, block, to, src=None):
            return pltpu.make_async_remote_copy(
                src_ref=slot(t, *block) if src is None else src, dst_ref=slot(t, *block),
                send_sem=send_sems.at[t, k], recv_sem=recv_sems.at[t, k], device_id=to, device_id_type=MESH)

        local = [pltpu.make_async_copy(my_rows(t), slot(t, *me), local_sems.at[t]) for t in range(n_t)]
        for cp in local:
            cp.start()
        first = []
        for t in range(n_t):
            first += [copy(t, 1 + j, me, (*chip, c), src=my_rows(t)) for j, chip in enumerate(chips)]
            first.append(copy(t, 0, me, sibling, src=my_rows(t)))
        for cp in first:
            cp.start()
        passed = []
        for j, chip in enumerate(chips):
            for t in range(n_t):
                copy(t, 1 + j, (*chip, c), me).wait_recv()
                passed.append(copy(t, 4 + j, (*chip, c), sibling))
                passed[-1].start()
        for t in range(n_t):
            copy(t, 0, sibling, me).wait_recv()
        for j, chip in enumerate(chips):
            for t in range(n_t):
                copy(t, 4 + j, (*chip, 1 - c), me).wait_recv()
        for cp in first + passed:
            cp.wait_send()
        for cp in local:
            cp.wait()

    hbm = pl.BlockSpec(memory_space=pltpu.HBM)
    return _pc(
        body, name=name, out_shape=tuple(_sds((N_DEV, h, s.shape[1]), s.dtype) for h, s in zip(halves, shards)),
        in_specs=[hbm] * n_t, out_specs=(hbm,) * n_t,
        scratch_shapes=[pltpu.SemaphoreType.DMA((n_t, 7)), pltpu.SemaphoreType.DMA((n_t, 7)),
                        pltpu.SemaphoreType.DMA((n_t,))],
    )(*shards)


def _scatter_grads(name, units):
    n_u = len(units)
    halves = [u.shape[1] // 2 for u in units]

    def body(*refs):
        u_refs, r_refs = refs[:n_u], refs[n_u:2 * n_u]
        send_sems, recv_sems, local_sems = refs[2 * n_u:]
        x, y, c = lax.axis_index("x"), lax.axis_index("y"), lax.axis_index("c")
        me = 4 * x + 2 * y + c

        def piece(t, chip, core):
            return u_refs[t].at[chip, pl.ds(core * halves[t], halves[t]), :]

        local = [pltpu.make_async_copy(piece(t, 2 * x + y, c), r_refs[t].at[me], local_sems.at[t]) for t in range(n_u)]
        for cp in local:
            cp.start()
        sends, recvs = [], []
        for k in range(1, N_DEV):
            px, py, pcc = _flip(x, (k >> 2) & 1), _flip(y, (k >> 1) & 1), _flip(c, k & 1)
            peer = 4 * px + 2 * py + pcc
            for t in range(n_u):
                sends.append(pltpu.make_async_remote_copy(
                    src_ref=piece(t, 2 * px + py, pcc), dst_ref=r_refs[t].at[me], send_sem=send_sems.at[t, k - 1],
                    recv_sem=recv_sems.at[t, k - 1], device_id=(px, py, pcc), device_id_type=MESH))
                recvs.append(pltpu.make_async_remote_copy(
                    src_ref=piece(t, 2 * x + y, c), dst_ref=r_refs[t].at[peer], send_sem=send_sems.at[t, k - 1],
                    recv_sem=recv_sems.at[t, k - 1], device_id=(px, py, pcc), device_id_type=MESH))
        for cp in sends:
            cp.start()
        for cp in recvs:
            cp.wait_recv()
        for cp in sends:
            cp.wait_send()
        for cp in local:
            cp.wait()

    hbm = pl.BlockSpec(memory_space=pltpu.HBM)
    return _pc(
        body, name=name, out_shape=tuple(_sds((N_DEV, h, u.shape[2]), u.dtype) for h, u in zip(halves, units)),
        in_specs=[hbm] * n_u, out_specs=(hbm,) * n_u,
        scratch_shapes=[pltpu.SemaphoreType.DMA((n_u, 7)), pltpu.SemaphoreType.DMA((n_u, 7)),
                        pltpu.SemaphoreType.DMA((n_u,))],
    )(*units)


def _swap_halves(name, reduced, layers_of):
    n_u, n_t = len(reduced), len(layers_of)
    where = {u: (t, l) for t, us in enumerate(layers_of) for l, u in enumerate(us)}

    def body(*refs):
        r_refs, o_refs = refs[:n_u], refs[n_u:n_u + n_t]
        send_sems, recv_sems, local_sems = refs[n_u + n_t:]
        x, y, c = lax.axis_index("x"), lax.axis_index("y"), lax.axis_index("c")

        def rows(u, core):
            t, l = where[u]
            h = reduced[u].shape[0]
            return o_refs[t].at[l, pl.ds(core * h, h), :]

        local = [pltpu.make_async_copy(r_refs[u], rows(u, c), local_sems.at[u]) for u in range(n_u)]
        sends = [pltpu.make_async_remote_copy(src_ref=r_refs[u], dst_ref=rows(u, c), send_sem=send_sems.at[u],
                                              recv_sem=recv_sems.at[u], device_id=(x, y, 1 - c), device_id_type=MESH)
                 for u in range(n_u)]
        recvs = [pltpu.make_async_remote_copy(src_ref=r_refs[u], dst_ref=rows(u, 1 - c), send_sem=send_sems.at[u],
                                              recv_sem=recv_sems.at[u], device_id=(x, y, 1 - c), device_id_type=MESH)
                 for u in range(n_u)]
        for cp in local + sends:
            cp.start()
        for cp in recvs:
            cp.wait_recv()
        for cp in sends:
            cp.wait_send()
        for cp in local:
            cp.wait()

    hbm = pl.BlockSpec(memory_space=pltpu.HBM)
    out_shape = tuple(_sds((len(us), 2 * reduced[us[0]].shape[0], reduced[us[0]].shape[1]), F32) for us in layers_of)
    return _pc(
        body, name=name, out_shape=out_shape, in_specs=[hbm] * n_u, out_specs=(hbm,) * n_t,
        scratch_shapes=[pltpu.SemaphoreType.DMA((n_u,)), pltpu.SemaphoreType.DMA((n_u,)), pltpu.SemaphoreType.DMA((n_u,))],
    )(*reduced)


def _sum8(name, parts):
    _, m, n = parts.shape
    tm = _div(m, 256, SUBLANE)

    def body(p_ref, o_ref):
        acc = p_ref[0]
        for s in range(1, N_DEV):
            acc = acc + p_ref[s]
        o_ref[...] = acc

    return _pc(body, name=name, grid=(m // tm,), out_shape=_sds((m, n), F32),
               in_specs=[pl.BlockSpec((N_DEV, tm, n), lambda i: (0, i, 0))],
               out_specs=pl.BlockSpec((tm, n), lambda i: (i, 0)), compiler_params=_cp("parallel"))(parts)


def _pack_rows(arrays, dtype, row_mult):
    flat, spans, off = [], [], 0
    for a in arrays:
        flat.append(a.reshape(-1).astype(dtype))
        spans.append((off, a.shape))
        off += a.size
    quantum = row_mult * PACK_COLS
    total = -(-off // quantum) * quantum
    if total > off:
        flat.append(jnp.zeros((total - off,), dtype))
    return jnp.concatenate(flat).reshape(total // PACK_COLS, PACK_COLS), spans


def _size(shape):
    n = 1
    for s in shape:
        n *= s
    return n


def _join_chips(blocks, axis):
    return jnp.concatenate([blocks[j] for j in range(N_CHIP)], axis=axis)


def _cols_by_chip(g):
    k, n = g.shape
    return jnp.transpose(g.reshape(k, N_CHIP, n // N_CHIP), (1, 0, 2))


def _mm_nn(name, pairs, out_dtype, tm_cap=1024, tn_cap=1536):
    m = pairs[0][0].shape[0]
    nb, _, n4 = pairs[0][1][0].shape
    tm, tn = _div(m, tm_cap, 16), _div(n4, tn_cap, LANE)
    per = n4 // tn
    n_pairs = len(pairs)

    def body(*refs):
        o_ref = refs[-1]
        acc = jnp.dot(refs[0][...], refs[1][0], preferred_element_type=F32)
        for i in range(1, n_pairs):
            acc = acc + jnp.dot(refs[2 * i][...], refs[2 * i + 1][0], preferred_element_type=F32)
        o_ref[...] = acc.astype(o_ref.dtype)

    in_specs, args = [], []
    for a, (w, r) in pairs:
        k = a.shape[1]
        assert w.shape[0] == nb and w.shape[2] == n4 and w.shape[1] % k == 0
        in_specs += [pl.BlockSpec((tm, k), lambda j, i: (i, 0)),
                     pl.BlockSpec((1, k, tn), functools.partial(lambda j, i, r_: (j // per, r_, j % per), r_=r))]
        args += [a, w]
    return _pc(body, name=name, grid=(nb * per, m // tm), out_shape=_sds((m, nb * n4), out_dtype), in_specs=in_specs,
               out_specs=pl.BlockSpec((tm, tn), lambda j, i: (i, j)), compiler_params=_cp("parallel", "parallel"))(*args)


def _mm_tn(name, x, ys, n_blocks=1, tt_cap=512):
    t, k = x.shape
    widths = [y.shape[1] for y in ys]
    n4 = sum(widths) // n_blocks
    common = n4
    for w in widths:
        common = _gcd(common, w)
    tk, tn, tt = _div(k, 1024, LANE), _div(common, 1536, LANE), _div(t, tt_cap, 16)
    per = n4 // tn
    starts, acc_w = [], 0
    for w in widths:
        starts.append(acc_w // tn)
        acc_w += w
    counts = [w // tn for w in widths]
    n_y = len(ys)

    def active(i, j):
        return (j >= starts[i]) & (j < starts[i] + counts[i])

    def body(*refs):
        x_ref, y_refs, o_ref = refs[0], refs[1:1 + n_y], refs[-1]
        j = pl.program_id(1)

        @pl.when(pl.program_id(2) == 0)
        def _():
            o_ref[...] = jnp.zeros_like(o_ref)

        for i in range(n_y):
            @pl.when(active(i, j))
            def _():
                o_ref[0] += lax.dot_general(x_ref[...], y_refs[i][...], (((0,), (0,)), ((), ())),
                                            preferred_element_type=F32)

    def y_spec(i):
        def index(a, j, s):
            on = active(i, j)
            return jnp.where(on, s, 0), jnp.where(on, j - starts[i], 0)
        return pl.BlockSpec((tt, tn), index)

    return _pc(body, name=name, grid=(k // tk, n_blocks * per, t // tt), out_shape=_sds((n_blocks, k, n4), F32),
               in_specs=[pl.BlockSpec((tt, tk), lambda a, j, s: (s, a))] + [y_spec(i) for i in range(n_y)],
               out_specs=pl.BlockSpec((1, tk, tn), lambda a, j, s: (j // per, a, j % per)),
               compiler_params=_cp("parallel", "parallel", "arbitrary"))(x, *ys)


def _gcd(a, b):
    while b:
        a, b = b, a % b
    return a


def _w2(w):
    return (w[None], 0)


def _tok_spec(ts, d):
    return pl.BlockSpec((1, ts, d), lambda b, i: (b, i, 0))


def _seq_spec(d):
    return pl.BlockSpec((1, 1, d), lambda b, i: (b, 0, 0))


def _vec_spec(d):
    return pl.BlockSpec((1, d), lambda b, i: (0, 0))


def _ln_stats(z):
    mu = jnp.mean(z, axis=-1, keepdims=True)
    zc = z - mu
    var = jnp.mean(zc * zc, axis=-1, keepdims=True)
    rstd = lax.rsqrt(var + LN_EPS)
    return zc * rstd, rstd


def _modulate(name, x, sc, sh):
    b, s, d = x.shape
    ts = _div(s, 512, 16)

    def body(x_ref, sc_ref, sh_ref, u_ref):
        u_ref[0] = (x_ref[0] * (1.0 + sc_ref[0]) + sh_ref[0]).astype(BF16)

    return _pc(body, name=name, grid=(b, s // ts), out_shape=_sds(x.shape, BF16),
               in_specs=[_tok_spec(ts, d), _seq_spec(d), _seq_spec(d)], out_specs=_tok_spec(ts, d),
               compiler_params=_cp("parallel", "parallel"))(x, sc, sh)


def _ln_mod_fwd(name, alpha, x, y, g, lng, lnb, sc, sh):
    b, s, d = x.shape
    ts = _div(s, 512, 16)

    def body(x_ref, y_ref, g_ref, lng_ref, lnb_ref, sc_ref, sh_ref, z_ref, xn_ref, u_ref):
        z = alpha * x_ref[0] + (1.0 + g_ref[0]) * y_ref[0]
        xhat, _ = _ln_stats(z)
        xn = xhat * lng_ref[...] + lnb_ref[...]
        z_ref[0] = z
        xn_ref[0] = xn
        u_ref[0] = (xn * (1.0 + sc_ref[0]) + sh_ref[0]).astype(BF16)

    tok, seq, vec = _tok_spec(ts, d), _seq_spec(d), _vec_spec(d)
    return _pc(body, name=name, grid=(b, s // ts),
               out_shape=(_sds(x.shape, F32), _sds(x.shape, F32), _sds(x.shape, BF16)),
               in_specs=[tok, tok, seq, vec, vec, seq, seq], out_specs=(tok, tok, tok),
               compiler_params=_cp("parallel", "parallel"))(x, y, g, lng, lnb, sc, sh)


def _ln_loss_fwd(name, alpha, x, y, g, lng, lnb, target):
    b, s, d = x.shape
    ts = _div(s, 512, 16)

    def body(x_ref, y_ref, g_ref, lng_ref, lnb_ref, t_ref, z_ref, ct_ref, loss_ref):
        @pl.when((pl.program_id(0) == 0) & (pl.program_id(1) == 0))
        def _():
            loss_ref[...] = jnp.zeros_like(loss_ref)
        z = alpha * x_ref[0] + (1.0 + g_ref[0]) * y_ref[0]
        xhat, _ = _ln_stats(z)
        err = xhat * lng_ref[...] + lnb_ref[...] - t_ref[0]
        z_ref[0] = z
        ct_ref[0] = err / d
        part = 0.5 * jnp.sum(jnp.mean(err * err, axis=-1, keepdims=True))
        loss_ref[...] += jnp.full(loss_ref.shape, part, F32)

    tok, seq, vec = _tok_spec(ts, d), _seq_spec(d), _vec_spec(d)
    return _pc(body, name=name, grid=(b, s // ts),
               out_shape=(_sds(x.shape, F32), _sds(x.shape, F32), _sds((SUBLANE, LANE), F32)),
               in_specs=[tok, tok, seq, vec, vec, tok],
               out_specs=(tok, tok, pl.BlockSpec((SUBLANE, LANE), lambda b, i: (0, 0))),
               compiler_params=_cp("arbitrary", "arbitrary"))(x, y, g, lng, lnb, target)


def _sub_bwd(name, alpha, upstream, z, y, g, lng):
    b, s, d = z.shape
    ts = _div(s, 512, 16)
    last = len(upstream) == 1

    def body(*refs):
        if last:
            ct_ref, z_ref, y_ref, g_ref, lng_ref, dz_ref, dy_ref, dg_ref, dlng_ref, dlnb_ref = refs
        else:
            (dzn_ref, dun_ref, xn_ref, scn_ref, z_ref, y_ref, g_ref, lng_ref,
             dz_ref, dy_ref, dg_ref, dlng_ref, dlnb_ref, dsc_ref, dsh_ref) = refs
        first_tile = pl.program_id(1) == 0

        @pl.when(first_tile & (pl.program_id(0) == 0))
        def _():
            dlng_ref[...] = jnp.zeros_like(dlng_ref)
            dlnb_ref[...] = jnp.zeros_like(dlnb_ref)

        @pl.when(first_tile)
        def _():
            dg_ref[...] = jnp.zeros_like(dg_ref)
            if not last:
                dsc_ref[...] = jnp.zeros_like(dsc_ref)
                dsh_ref[...] = jnp.zeros_like(dsh_ref)

        if last:
            ct = ct_ref[0]
        else:
            dun = dun_ref[0]
            ct = alpha * dzn_ref[0] + dun * (1.0 + scn_ref[0])
            dsc_ref[0] += jnp.sum(dun * xn_ref[0], axis=0, keepdims=True)
            dsh_ref[0] += jnp.sum(dun, axis=0, keepdims=True)
        xhat, rstd = _ln_stats(z_ref[0])
        dlng_ref[...] += jnp.sum(ct * xhat, axis=0, keepdims=True)
        dlnb_ref[...] += jnp.sum(ct, axis=0, keepdims=True)
        dxhat = ct * lng_ref[...]
        dz = rstd * (dxhat - jnp.mean(dxhat, axis=-1, keepdims=True)
                     - xhat * jnp.mean(dxhat * xhat, axis=-1, keepdims=True))
        dz_ref[0] = dz
        dy_ref[0] = ((1.0 + g_ref[0]) * dz).astype(BF16)
        dg_ref[0] += jnp.sum(dz * y_ref[0], axis=0, keepdims=True)

    tok, seq, vec = _tok_spec(ts, d), _seq_spec(d), _vec_spec(d)
    seq_out = _sds((b, 1, d), F32)
    out_shape = [_sds(z.shape, F32), _sds(z.shape, BF16), seq_out, _sds((1, d), F32), _sds((1, d), F32)]
    out_specs = [tok, tok, seq, vec, vec]
    if last:
        in_specs = [tok, tok, tok, seq, vec]
    else:
        in_specs = [tok, tok, tok, seq, tok, tok, seq, vec]
        out_shape += [seq_out, seq_out]
        out_specs += [seq, seq]
    return _pc(body, name=name, grid=(b, s // ts), out_shape=tuple(out_shape), in_specs=in_specs,
               out_specs=tuple(out_specs), compiler_params=_cp("arbitrary", "arbitrary"))(*upstream, z, y, g, lng)


def _input_bwd(name, alpha, dz, du, x, sc):
    b, s, d = x.shape
    ts = _div(s, 512, 16)

    def body(dz_ref, du_ref, x_ref, sc_ref, gx_ref, dsc_ref, dsh_ref):
        @pl.when(pl.program_id(1) == 0)
        def _():
            dsc_ref[...] = jnp.zeros_like(dsc_ref)
            dsh_ref[...] = jnp.zeros_like(dsh_ref)
        du_ = du_ref[0]
        gx_ref[0] = alpha * dz_ref[0] + du_ * (1.0 + sc_ref[0])
        dsc_ref[0] += jnp.sum(du_ * x_ref[0], axis=0, keepdims=True)
        dsh_ref[0] += jnp.sum(du_, axis=0, keepdims=True)

    tok, seq = _tok_spec(ts, d), _seq_spec(d)
    seq_out = _sds((b, 1, d), F32)
    return _pc(body, name=name, grid=(b, s // ts), out_shape=(_sds(x.shape, F32), seq_out, seq_out),
               in_specs=[tok, tok, tok, seq], out_specs=(tok, seq, seq),
               compiler_params=_cp("parallel", "arbitrary"))(dz, du, x, sc)


def _rows_iota(shape):
    return lax.broadcasted_iota(jnp.int32, shape, 0)


def _back(v, k):
    return pltpu.roll(v, k, axis=0)


def _ahead(v, k):
    return pltpu.roll(v, v.shape[0] - k, axis=0)


def _conv3(ext, w_ref):
    return w_ref[2:3, :] * ext + w_ref[1:2, :] * _back(ext, 1) + w_ref[0:1, :] * _back(ext, 2)


def _conv3_t(dh_ext, w_ref):
    return w_ref[2:3, :] * dh_ext + w_ref[1:2, :] * _ahead(dh_ext, 1) + w_ref[0:1, :] * _ahead(dh_ext, 2)


def _flag(cond):
    return jnp.where(cond, 1.0, 0.0).astype(F32)


def _sigmoid(v):
    return 1.0 / (1.0 + jnp.exp(-v))


def _halo_specs(ts, tc, halo, n_s, col):
    per = ts // halo
    tile = pl.BlockSpec((1, ts, tc), lambda b, i, j: (b, i, col(j)))
    prev = pl.BlockSpec((1, halo, tc), lambda b, i, j: (b, jnp.maximum(i * per - 1, 0), col(j)))
    nxt = pl.BlockSpec((1, halo, tc), lambda b, i, j: (b, jnp.minimum((i + 1) * per, n_s * per - 1), col(j)))
    return tile, prev, nxt


def _convglu_fwd(name, p, cw, cb):
    b, s, f2 = p.shape
    f = f2 // 2
    ts, tc = _div(s, 512, CONV_HALO), _div(f, 256, LANE)
    n_s, n_c = s // ts, f // tc

    def body(pv_ref, pvh_ref, pg_ref, pgh_ref, wv_ref, wg_ref, bv_ref, bg_ref, a_ref):
        keep = _flag(pl.program_id(1) > 0)

        def conv(t_ref, h_ref, w_ref, b_ref):
            ext = jnp.concatenate([h_ref[0] * keep, t_ref[0]], axis=0)
            return _conv3(ext, w_ref)[CONV_HALO:] + b_ref[...]

        val = conv(pv_ref, pvh_ref, wv_ref, bv_ref)
        gate = conv(pg_ref, pgh_ref, wg_ref, bg_ref)
        a_ref[0] = (gate * _sigmoid(gate) * val).astype(BF16)

    tv, hv, _ = _halo_specs(ts, tc, CONV_HALO, n_s, lambda j: j)
    tg, hg, _ = _halo_specs(ts, tc, CONV_HALO, n_s, lambda j: j + n_c)
    wv = pl.BlockSpec((3, tc), lambda b, i, j: (0, j))
    wg = pl.BlockSpec((3, tc), lambda b, i, j: (0, j + n_c))
    bv = pl.BlockSpec((1, tc), lambda b, i, j: (0, j))
    bg = pl.BlockSpec((1, tc), lambda b, i, j: (0, j + n_c))
    return _pc(body, name=name, grid=(b, n_s, n_c), out_shape=_sds((b, s, f), BF16),
               in_specs=[tv, hv, tg, hg, wv, wg, bv, bg], out_specs=pl.BlockSpec((1, ts, tc), lambda b, i, j: (b, i, j)),
               compiler_params=_cp("parallel", "parallel", "parallel"))(p, p, p, p, cw, cw, cb, cb)


def _convglu_bwd(name, p, da, cw, cb):
    b, s, f2 = p.shape
    f = f2 // 2
    ts, tc = _div(s, 512, CONV_HALO), _div(f, 256, LANE)
    n_s, n_c = s // ts, f // tc

    def body(pv_ref, pvp_ref, pvn_ref, pg_ref, pgp_ref, pgn_ref, da_ref, dan_ref, wv_ref, wg_ref, bv_ref, bg_ref,
             dpv_ref, dpg_ref, dwv_ref, dwg_ref, dbv_ref, dbg_ref):
        bi, i = pl.program_id(1), pl.program_id(2)

        @pl.when((bi == 0) & (i == 0))
        def _():
            for r in (dwv_ref, dwg_ref, dbv_ref, dbg_ref):
                r[...] = jnp.zeros_like(r)

        keep_prev = _flag(i > 0)
        keep_next = _flag(i < n_s - 1)
        pv_ext = jnp.concatenate([pvp_ref[0] * keep_prev, pv_ref[0], pvn_ref[0]], axis=0)
        pg_ext = jnp.concatenate([pgp_ref[0] * keep_prev, pg_ref[0], pgn_ref[0]], axis=0)
        val = _conv3(pv_ext, wv_ref)[CONV_HALO:] + bv_ref[...]
        gate = _conv3(pg_ext, wg_ref)[CONV_HALO:] + bg_ref[...]
        da_ext = jnp.concatenate([da_ref[0], dan_ref[0] * keep_next], axis=0)
        sg = _sigmoid(gate)
        dval = da_ext * gate * sg
        dgate = da_ext * val * (sg * (1.0 + gate * (1.0 - sg)))
        dpv_ref[0] = _conv3_t(dval, wv_ref)[:ts].astype(BF16)
        dpg_ref[0] = _conv3_t(dgate, wg_ref)[:ts].astype(BF16)
        for dh, p_ext, dw_ref, db_ref in ((dval[:ts], pv_ext, dwv_ref, dbv_ref), (dgate[:ts], pg_ext, dwg_ref, dbg_ref)):
            db_ref[...] += jnp.sum(dh, axis=0, keepdims=True)
            for k in range(3):
                shifted = p_ext if k == 2 else _back(p_ext, 2 - k)
                dw_ref[k:k + 1, :] += jnp.sum(dh * shifted[CONV_HALO:CONV_HALO + ts], axis=0, keepdims=True)

    def specs(col):
        per = ts // CONV_HALO
        tile = pl.BlockSpec((1, ts, tc), lambda j, b, i: (b, i, col(j)))
        prev = pl.BlockSpec((1, CONV_HALO, tc), lambda j, b, i: (b, jnp.maximum(i * per - 1, 0), col(j)))
        nxt = pl.BlockSpec((1, CONV_HALO, tc), lambda j, b, i: (b, jnp.minimum((i + 1) * per, n_s * per - 1), col(j)))
        return tile, prev, nxt

    tv, pvp, pvn = specs(lambda j: j)
    tg, pgp, pgn = specs(lambda j: j + n_c)
    wv = pl.BlockSpec((3, tc), lambda j, b, i: (0, j))
    wg = pl.BlockSpec((3, tc), lambda j, b, i: (0, j + n_c))
    bv = pl.BlockSpec((1, tc), lambda j, b, i: (0, j))
    bg = pl.BlockSpec((1, tc), lambda j, b, i: (0, j + n_c))
    out_tile = pl.BlockSpec((1, ts, tc), lambda j, b, i: (b, i, j))
    acc3, acc1 = pl.BlockSpec((3, tc), lambda j, b, i: (0, j)), pl.BlockSpec((1, tc), lambda j, b, i: (0, j))
    dpv, dpg, dwv, dwg, dbv, dbg = _pc(
        body, name=name, grid=(n_c, b, n_s),
        out_shape=(_sds((b, s, f), BF16), _sds((b, s, f), BF16), _sds((3, f), F32), _sds((3, f), F32),
                   _sds((1, f), F32), _sds((1, f), F32)),
        in_specs=[tv, pvp, pvn, tg, pgp, pgn, tv, pvn, wv, wg, bv, bg],
        out_specs=(out_tile, out_tile, acc3, acc3, acc1, acc1),
        compiler_params=_cp("parallel", "arbitrary", "arbitrary"))(p, p, p, p, p, p, da, da, cw, cw, cb, cb)
    return dpv, dpg, jnp.concatenate([dwv, dwg], axis=1), jnp.concatenate([dbv, dbg], axis=1)


def _shortconv_fwd(name, q, cw):
    b, s, d3 = q.shape
    d = d3 // 3
    ts, tc = _div(s, 512, CONV_HALO), _div(d, 256, LANE)
    n_s, n_c = s // ts, d // tc

    def body(gb_ref, gc_ref, gch_ref, h_ref, hh_ref, w_ref, r_ref):
        keep = _flag(pl.program_id(1) > 0)
        m_ext = jnp.concatenate([gch_ref[0] * hh_ref[0] * keep, gc_ref[0] * h_ref[0]], axis=0)
        r_ref[0] = (gb_ref[0] * _conv3(m_ext, w_ref)[CONV_HALO:]).astype(BF16)

    tb, _, _ = _halo_specs(ts, tc, CONV_HALO, n_s, lambda j: j)
    tcc, hc, _ = _halo_specs(ts, tc, CONV_HALO, n_s, lambda j: j + n_c)
    th, hh, _ = _halo_specs(ts, tc, CONV_HALO, n_s, lambda j: j + 2 * n_c)
    w = pl.BlockSpec((3, tc), lambda b, i, j: (0, j))
    return _pc(body, name=name, grid=(b, n_s, n_c), out_shape=_sds((b, s, d), BF16),
               in_specs=[tb, tcc, hc, th, hh, w], out_specs=pl.BlockSpec((1, ts, tc), lambda b, i, j: (b, i, j)),
               compiler_params=_cp("parallel", "parallel", "parallel"))(q, q, q, q, q, cw)


def _shortconv_bwd(name, q, dr, cw):
    b, s, d3 = q.shape
    d = d3 // 3
    ts, tc = _div(s, 512, CONV_HALO), _div(d, 256, LANE)
    n_s, n_c = s // ts, d // tc

    def body(gb_ref, gbn_ref, gc_ref, gcp_ref, h_ref, hp_ref, dr_ref, drn_ref, w_ref,
             dgb_ref, dgc_ref, dh_ref, dw_ref):
        bi, i = pl.program_id(1), pl.program_id(2)

        @pl.when((bi == 0) & (i == 0))
        def _():
            dw_ref[...] = jnp.zeros_like(dw_ref)

        keep_prev = _flag(i > 0)
        keep_next = _flag(i < n_s - 1)
        gc, h = gc_ref[0], h_ref[0]
        m_ext = jnp.concatenate([gcp_ref[0] * hp_ref[0] * keep_prev, gc * h], axis=0)
        cm = _conv3(m_ext, w_ref)[CONV_HALO:]
        dr_ = dr_ref[0]
        dgb_ref[0] = (dr_ * cm).astype(BF16)
        dcv_ext = jnp.concatenate([dr_ * gb_ref[0], drn_ref[0] * gbn_ref[0] * keep_next], axis=0)
        dm = _conv3_t(dcv_ext, w_ref)[:ts]
        dgc_ref[0] = (dm * h).astype(BF16)
        dh_ref[0] = (dm * gc).astype(BF16)
        dcv = dcv_ext[:ts]
        for k in range(3):
            shifted = m_ext if k == 2 else _back(m_ext, 2 - k)
            dw_ref[k:k + 1, :] += jnp.sum(dcv * shifted[CONV_HALO:], axis=0, keepdims=True)

    def specs(col):
        per = ts // CONV_HALO
        tile = pl.BlockSpec((1, ts, tc), lambda j, b, i: (b, i, col(j)))
        prev = pl.BlockSpec((1, CONV_HALO, tc), lambda j, b, i: (b, jnp.maximum(i * per - 1, 0), col(j)))
        nxt = pl.BlockSpec((1, CONV_HALO, tc), lambda j, b, i: (b, jnp.minimum((i + 1) * per, n_s * per - 1), col(j)))
        return tile, prev, nxt

    tb, _, nb = specs(lambda j: j)
    tcc, pc_, _ = specs(lambda j: j + n_c)
    th, ph, _ = specs(lambda j: j + 2 * n_c)
    w = pl.BlockSpec((3, tc), lambda j, b, i: (0, j))
    out_tile = pl.BlockSpec((1, ts, tc), lambda j, b, i: (b, i, j))
    o = _sds((b, s, d), BF16)
    return _pc(body, name=name, grid=(n_c, b, n_s), out_shape=(o, o, o, _sds((3, d), F32)),
               in_specs=[tb, nb, tcc, pc_, th, ph, tb, nb, w], out_specs=(out_tile, out_tile, out_tile, w),
               compiler_params=_cp("parallel", "arbitrary", "arbitrary"))(q, q, q, q, q, q, dr, dr, cw)


def _pick_window(group, cands):
    gid = jnp.full(cands[0].shape, group, jnp.int32)
    out = cands[-1]
    for k in range(len(cands) - 2, -1, -1):
        out = jnp.where(gid == k, cands[k], out)
    return out


def _window_sums(v, shift):
    s1 = v + shift(v, 1)
    s2 = s1 + shift(s1, 2)
    s3 = s2 + shift(s2, 4)
    s4 = s3 + shift(s3, 8)
    return [s1, s2, s3, s4]


def _pool_counts(group, first_row, n_rows, cols):
    t = _rows_iota((n_rows, cols)) + first_row
    window = _pick_window(group, [jnp.full((n_rows, cols), 2 << k, jnp.int32) for k in range(POOL_GROUPS)])
    return jnp.minimum(t + 1, window).astype(F32)


def _pool_fwd(name, x, sc, sh, w, scale):
    b, s, d = x.shape
    tc = d // POOL_GROUPS
    ts = _div(s, 512, POOL_HALO)
    n_s = s // ts

    def body(x_ref, xp_ref, sc_ref, sh_ref, w_ref, scale_ref, y_ref):
        i, grp = pl.program_id(1), pl.program_id(2)
        keep = _flag(i > 0)
        mod = 1.0 + sc_ref[0]
        u = x_ref[0] * mod + sh_ref[0]
        u_ext = jnp.concatenate([(xp_ref[0] * mod + sh_ref[0]) * keep, u], axis=0)
        summed = _pick_window(grp, _window_sums(u_ext, _back))[POOL_HALO:]
        pooled = summed / _pool_counts(grp, i * ts, ts, tc) - u
        y_ref[0] = jnp.dot(pooled.astype(BF16), w_ref[0], preferred_element_type=F32) * scale_ref[...]

    tile, prev, _ = _halo_specs(ts, tc, POOL_HALO, n_s, lambda j: j)
    seq = pl.BlockSpec((1, 1, tc), lambda b, i, j: (b, 0, j))
    return _pc(body, name=name, grid=(b, n_s, POOL_GROUPS), out_shape=_sds(x.shape, F32),
               in_specs=[tile, prev, seq, seq, pl.BlockSpec((1, tc, tc), lambda b, i, j: (j, 0, 0)),
                         pl.BlockSpec((1, tc), lambda b, i, j: (0, j))],
               out_specs=pl.BlockSpec((1, ts, tc), lambda b, i, j: (b, i, j)),
               compiler_params=_cp("parallel", "parallel", "parallel"))(x, x, sc, sh, w, scale)


def _pool_bwd(name, x, sc, sh, dy, w, w_t, scale):
    b, s, d = x.shape
    tc = d // POOL_GROUPS
    ts = _div(s, 512, POOL_HALO)
    n_s = s // ts

    def body(x_ref, xp_ref, sc_ref, sh_ref, dy_ref, dyn_ref, w_ref, wt_ref, scale_ref, du_ref, dw_ref, dscale_ref):
        grp, bi, i = pl.program_id(0), pl.program_id(1), pl.program_id(2)

        @pl.when((bi == 0) & (i == 0))
        def _():
            dw_ref[...] = jnp.zeros_like(dw_ref)
            dscale_ref[...] = jnp.zeros_like(dscale_ref)

        keep_prev = _flag(i > 0)
        keep_next = _flag(i < n_s - 1)
        mod = 1.0 + sc_ref[0]
        u = x_ref[0] * mod + sh_ref[0]
        u_ext = jnp.concatenate([(xp_ref[0] * mod + sh_ref[0]) * keep_prev, u], axis=0)
        summed = _pick_window(grp, _window_sums(u_ext, _back))[POOL_HALO:]
        pooled = (summed / _pool_counts(grp, i * ts, ts, tc) - u).astype(BF16)
        dy_ = dy_ref[0].astype(F32)
        ymat = jnp.dot(pooled, w_ref[0], preferred_element_type=F32)
        dscale_ref[...] += jnp.sum(dy_ * ymat, axis=0, keepdims=True)
        dys_ext = (jnp.concatenate([dy_, dyn_ref[0].astype(F32) * keep_next], axis=0) * scale_ref[...]).astype(BF16)
        dw_ref[0] += lax.dot_general(pooled, dys_ext[:ts], (((0,), (0,)), ((), ())), preferred_element_type=F32)
        dpooled = jnp.dot(dys_ext, wt_ref[0], preferred_element_type=F32)
        e = dpooled / _pool_counts(grp, i * ts, ts + POOL_HALO, tc)
        du_ref[0] = _pick_window(grp, _window_sums(e, _ahead))[:ts] - dpooled[:ts]

    per = ts // POOL_HALO
    tile = pl.BlockSpec((1, ts, tc), lambda j, b, i: (b, i, j))
    prev = pl.BlockSpec((1, POOL_HALO, tc), lambda j, b, i: (b, jnp.maximum(i * per - 1, 0), j))
    nxt = pl.BlockSpec((1, POOL_HALO, tc), lambda j, b, i: (b, jnp.minimum((i + 1) * per, n_s * per - 1), j))
    seq = pl.BlockSpec((1, 1, tc), lambda j, b, i: (b, 0, j))
    wsp = pl.BlockSpec((1, tc, tc), lambda j, b, i: (j, 0, 0))
    vec = pl.BlockSpec((1, tc), lambda j, b, i: (0, j))
    return _pc(body, name=name, grid=(POOL_GROUPS, b, n_s),
               out_shape=(_sds(x.shape, F32), _sds((POOL_GROUPS, tc, tc), F32), _sds((1, d), F32)),
               in_specs=[tile, prev, seq, seq, tile, nxt, wsp, wsp, vec], out_specs=(tile, wsp, vec),
               compiler_params=_cp("parallel", "arbitrary", "arbitrary"))(x, x, sc, sh, dy, dy, w, w_t, scale)


def _rope_swap(v):
    lane = lax.broadcasted_iota(jnp.int32, v.shape, v.ndim - 1)
    lo, hi = QK_NOPE, QK_NOPE + QK_ROPE // 2
    from_above = pltpu.roll(v, HEAD_PAD - QK_ROPE // 2, axis=v.ndim - 1)
    from_below = pltpu.roll(v, QK_ROPE // 2, axis=v.ndim - 1)
    return jnp.where((lane >= lo) & (lane < hi), from_above,
                     jnp.where((lane >= hi) & (lane < hi + QK_ROPE // 2), from_below, 0.0))


def _rope(v, cos_t, sin_t):
    return v * cos_t + _rope_swap(v) * sin_t


def _rope_t(dv, cos_t, sin_t):
    return dv * cos_t + _rope_swap(dv * sin_t)


def _rms(v, g):
    r = lax.rsqrt(jnp.mean(v * v, axis=-1, keepdims=True) + RMS_EPS)
    return v * r, r


def _mla_norm_fwd(name, a, qn, kvn, cos_t, sin_t):
    b, s, wa = a.shape
    ql, kvl = qn.shape[1], kvn.shape[1]
    ts = _div(s, 512, 16)

    def body(aq_ref, akv_ref, ape_ref, qn_ref, kvn_ref, cos_ref, sin_ref, cq_ref, ckv_ref, kpe_ref):
        yq, _ = _rms(aq_ref[0], None)
        cq_ref[0] = (yq * qn_ref[...]).astype(BF16)
        ykv, _ = _rms(akv_ref[0], None)
        ckv_ref[0] = (ykv * kvn_ref[...]).astype(BF16)
        kpe_ref[0] = _rope(ape_ref[0], cos_ref[0], sin_ref[0])

    tok = lambda w, col: pl.BlockSpec((1, ts, w), lambda b, i: (b, i, col))
    return _pc(body, name=name, grid=(b, s // ts),
               out_shape=(_sds((b, s, ql), BF16), _sds((b, s, kvl), BF16), _sds((b, s, HEAD_PAD), F32)),
               in_specs=[tok(ql, 0), tok(kvl, ql // kvl), tok(HEAD_PAD, (ql + kvl) // HEAD_PAD), _vec_spec(ql),
                         _vec_spec(kvl), tok(HEAD_PAD, 0), tok(HEAD_PAD, 0)],
               out_specs=(tok(ql, 0), tok(kvl, 0), tok(HEAD_PAD, 0)),
               compiler_params=_cp("parallel", "parallel"))(a, a, a, qn, kvn, cos_t, sin_t)


def _mla_norm_bwd(name, a, dcq, dckv, dkpe, qn, kvn):
    b, s, wa = a.shape
    ql, kvl = qn.shape[1], kvn.shape[1]
    ts = _div(s, 512, 16)

    def body(a_ref, dcq_ref, dckv_ref, dkpe_ref, qn_ref, kvn_ref, da_ref, dqn_ref, dkvn_ref):
        @pl.when((pl.program_id(0) == 0) & (pl.program_id(1) == 0))
        def _():
            dqn_ref[...] = jnp.zeros_like(dqn_ref)
            dkvn_ref[...] = jnp.zeros_like(dkvn_ref)

        def one(v, dc, g_ref, dg_ref):
            yv, r = _rms(v, None)
            dg_ref[...] += jnp.sum(dc * yv, axis=0, keepdims=True)
            dyv = dc * g_ref[...]
            return r * (dyv - yv * jnp.mean(dyv * yv, axis=-1, keepdims=True))

        av = a_ref[0]
        da_ref[0, :, 0:ql] = one(av[:, 0:ql], dcq_ref[0], qn_ref, dqn_ref).astype(BF16)
        da_ref[0, :, ql:ql + kvl] = one(av[:, ql:ql + kvl], dckv_ref[0], kvn_ref, dkvn_ref).astype(BF16)
        da_ref[0, :, ql + kvl:] = dkpe_ref[0].astype(BF16)

    return _pc(body, name=name, grid=(b, s // ts),
               out_shape=(_sds(a.shape, BF16), _sds((1, ql), F32), _sds((1, kvl), F32)),
               in_specs=[_tok_spec(ts, wa), _tok_spec(ts, ql), _tok_spec(ts, kvl), _tok_spec(ts, HEAD_PAD),
                         _vec_spec(ql), _vec_spec(kvl)],
               out_specs=(_tok_spec(ts, wa), _vec_spec(ql), _vec_spec(kvl)),
               compiler_params=_cp("arbitrary", "arbitrary"))(a, dcq, dckv, dkpe, qn, kvn)


def _mla_prep_fwd(name, q_raw, kv_raw, kpe, cos_t, sin_t, n_heads):
    b, s, wq = q_raw.shape
    ts = _div(s, 256, 16)

    def body(q_ref, k_ref, v_ref, kpe_ref, cos_ref, sin_ref, qo_ref, ko_ref, vo_ref):
        cos_, sin_, kpe_ = cos_ref[0], sin_ref[0], kpe_ref[0]
        for h in range(n_heads):
            lanes = slice(h * HEAD_PAD, (h + 1) * HEAD_PAD)
            qo_ref[0, :, lanes] = _rope(q_ref[0, :, lanes], cos_, sin_).astype(BF16)
            ko_ref[0, :, lanes] = (k_ref[0, :, lanes] + kpe_).astype(BF16)
        vo_ref[0] = v_ref[0].astype(BF16)

    wide = lambda part: pl.BlockSpec((1, ts, wq), lambda b, i: (b, i, part))
    tok = pl.BlockSpec((1, ts, HEAD_PAD), lambda b, i: (b, i, 0))
    o = _sds(q_raw.shape, BF16)
    return _pc(body, name=name, grid=(b, s // ts), out_shape=(o, o, o),
               in_specs=[wide(0), wide(0), wide(1), tok, tok, tok], out_specs=(wide(0), wide(0), wide(0)),
               compiler_params=_cp("parallel", "parallel"))(q_raw, kv_raw, kv_raw, kpe, cos_t, sin_t)


def _mla_prep_bwd(name, dq, dk, dv, cos_t, sin_t, n_heads):
    b, s, wq = dq.shape
    ts = _div(s, 256, 16)

    def body(dq_ref, dk_ref, dv_ref, cos_ref, sin_ref, dqr_ref, dkv_ref, dkpe_ref):
        cos_, sin_ = cos_ref[0], sin_ref[0]
        dk_sum = None
        for h in range(n_heads):
            lanes = slice(h * HEAD_PAD, (h + 1) * HEAD_PAD)
            dqr_ref[0, :, lanes] = _rope_t(dq_ref[0, :, lanes], cos_, sin_).astype(BF16)
            dk_h = dk_ref[0, :, lanes]
            dkv_ref[0, :, lanes] = dk_h.astype(BF16)
            dk_sum = dk_h if dk_sum is None else dk_sum + dk_h
        dkv_ref[0, :, wq:2 * wq] = dv_ref[0]
        dkpe_ref[0] = _rope_t(dk_sum, cos_, sin_)

    wide = pl.BlockSpec((1, ts, wq), lambda b, i: (b, i, 0))
    both = pl.BlockSpec((1, ts, 2 * wq), lambda b, i: (b, i, 0))
    tok = pl.BlockSpec((1, ts, HEAD_PAD), lambda b, i: (b, i, 0))
    return _pc(body, name=name, grid=(b, s // ts),
               out_shape=(_sds(dq.shape, BF16), _sds((b, s, 2 * wq), BF16), _sds((b, s, HEAD_PAD), F32)),
               in_specs=[wide, wide, wide, tok, tok], out_specs=(wide, both, tok),
               compiler_params=_cp("parallel", "parallel"))(dq, dk, dv, cos_t, sin_t)


def _causal_mask(i, j, tq, tk):
    rows = lax.broadcasted_iota(jnp.int32, (tq, tk), 0) + i * tq
    cols = lax.broadcasted_iota(jnp.int32, (tq, tk), 1) + j * tk
    return cols <= rows


def _nt(a, b):
    return lax.dot_general(a, b, (((1,), (1,)), ((), ())), preferred_element_type=F32)


def _tn(a, b):
    return lax.dot_general(a, b, (((0,), (0,)), ((), ())), preferred_element_type=F32)


def _flash_fwd(name, q, k, v, n_heads, sm_scale):
    b, s, _ = q.shape
    t = _div(s, 512, LANE)
    n = s // t
    neg = float(jnp.finfo(jnp.float32).min)

    def body(q_ref, k_ref, v_ref, o_ref, lse_ref, m_ref, l_ref, acc_ref):
        i, j = pl.program_id(2), pl.program_id(3)

        @pl.when(j == 0)
        def _():
            m_ref[...] = jnp.full(m_ref.shape, neg, F32)
            l_ref[...] = jnp.zeros_like(l_ref)
            acc_ref[...] = jnp.zeros_like(acc_ref)

        def block(on_diagonal):
            sc = _nt(q_ref[0], k_ref[0]) * sm_scale
            if on_diagonal:
                sc = jnp.where(_causal_mask(i, j, t, t), sc, neg)
            m_old = m_ref[...]
            m_new = jnp.maximum(m_old, jnp.max(sc, axis=-1, keepdims=True))
            p = jnp.exp(sc - m_new)
            corr = jnp.exp(m_old - m_new)
            l_ref[...] = corr * l_ref[...] + jnp.sum(p, axis=-1, keepdims=True)
            acc_ref[...] = corr * acc_ref[...] + jnp.dot(p.astype(BF16), v_ref[0], preferred_element_type=F32)
            m_ref[...] = m_new

        pl.when(j < i)(functools.partial(block, False))
        pl.when(j == i)(functools.partial(block, True))

        @pl.when(j == n - 1)
        def _():
            o_ref[0] = (acc_ref[...] / l_ref[...]).astype(BF16)
            lse_ref[0] = jnp.broadcast_to(m_ref[...] + jnp.log(l_ref[...]), (t, LANE))

    qs = pl.BlockSpec((1, t, HEAD_PAD), lambda b, h, i, j: (b, i, h))
    ks = pl.BlockSpec((1, t, HEAD_PAD), lambda b, h, i, j: (b, jnp.minimum(j, i), h))
    return _pc(body, name=name, grid=(b, n_heads, n, n),
               out_shape=(_sds(q.shape, BF16), _sds(q.shape, F32)), in_specs=[qs, ks, ks], out_specs=(qs, qs),
               scratch_shapes=[pltpu.VMEM((t, 1), F32), pltpu.VMEM((t, 1), F32), pltpu.VMEM((t, HEAD_PAD), F32)],
               compiler_params=_cp("parallel", "parallel", "parallel", "arbitrary"))(q, k, v)


def _flash_dq(name, q, k, v, o, lse, do, n_heads, sm_scale):
    b, s, _ = q.shape
    t = _div(s, 512, LANE)
    n = s // t

    def body(q_ref, k_ref, v_ref, o_ref, lse_ref, do_ref, dq_ref, acc_ref, delta_ref):
        i, j = pl.program_id(2), pl.program_id(3)

        @pl.when(j == 0)
        def _():
            acc_ref[...] = jnp.zeros_like(acc_ref)
            delta_ref[...] = jnp.sum(do_ref[0].astype(F32) * o_ref[0].astype(F32), axis=-1, keepdims=True)

        def block(on_diagonal):
            sc = _nt(q_ref[0], k_ref[0]) * sm_scale
            p = jnp.exp(sc - lse_ref[0][:, 0:1])
            if on_diagonal:
                p = jnp.where(_causal_mask(i, j, t, t), p, 0.0)
            dp = _nt(do_ref[0], v_ref[0])
            ds = p * (dp - delta_ref[...]) * sm_scale
            acc_ref[...] += jnp.dot(ds.astype(BF16), k_ref[0], preferred_element_type=F32)

        pl.when(j < i)(functools.partial(block, False))
        pl.when(j == i)(functools.partial(block, True))

        @pl.when(j == n - 1)
        def _():
            dq_ref[0] = acc_ref[...]

    qs = pl.BlockSpec((1, t, HEAD_PAD), lambda b, h, i, j: (b, i, h))
    ks = pl.BlockSpec((1, t, HEAD_PAD), lambda b, h, i, j: (b, jnp.minimum(j, i), h))
    return _pc(body, name=name, grid=(b, n_heads, n, n), out_shape=_sds(q.shape, F32),
               in_specs=[qs, ks, ks, qs, qs, qs], out_specs=qs,
               scratch_shapes=[pltpu.VMEM((t, HEAD_PAD), F32), pltpu.VMEM((t, 1), F32)],
               compiler_params=_cp("parallel", "parallel", "parallel", "arbitrary"))(q, k, v, o, lse, do)


def _flash_dkv(name, q, k, v, o, lse, do, n_heads, sm_scale):
    b, s, _ = q.shape
    t = _div(s, 512, LANE)
    n = s // t

    def body(q_ref, k_ref, v_ref, o_ref, lse_ref, do_ref, dk_ref, dv_ref, dk_acc, dv_acc):
        j, i = pl.program_id(2), pl.program_id(3)

        @pl.when(i == 0)
        def _():
            dk_acc[...] = jnp.zeros_like(dk_acc)
            dv_acc[...] = jnp.zeros_like(dv_acc)

        def block(on_diagonal):
            do_ = do_ref[0]
            delta = jnp.sum(do_.astype(F32) * o_ref[0].astype(F32), axis=-1, keepdims=True)
            sc = _nt(q_ref[0], k_ref[0]) * sm_scale
            p = jnp.exp(sc - lse_ref[0][:, 0:1])
            if on_diagonal:
                p = jnp.where(_causal_mask(i, j, t, t), p, 0.0)
            dv_acc[...] += _tn(p.astype(BF16), do_)
            dp = _nt(do_, v_ref[0])
            ds = p * (dp - delta) * sm_scale
            dk_acc[...] += _tn(ds.astype(BF16), q_ref[0])

        pl.when(i > j)(functools.partial(block, False))
        pl.when(i == j)(functools.partial(block, True))

        @pl.when(i == n - 1)
        def _():
            dk_ref[0] = dk_acc[...]
            dv_ref[0] = dv_acc[...].astype(BF16)

    qs = pl.BlockSpec((1, t, HEAD_PAD), lambda b, h, j, i: (b, jnp.maximum(i, j), h))
    ks = pl.BlockSpec((1, t, HEAD_PAD), lambda b, h, j, i: (b, j, h))
    return _pc(body, name=name, grid=(b, n_heads, n, n), out_shape=(_sds(q.shape, F32), _sds(q.shape, BF16)),
               in_specs=[qs, ks, ks, qs, qs, qs], out_specs=(ks, ks),
               scratch_shapes=[pltpu.VMEM((t, HEAD_PAD), F32), pltpu.VMEM((t, HEAD_PAD), F32)],
               compiler_params=_cp("parallel", "parallel", "parallel", "arbitrary"))(q, k, v, o, lse, do)


def _mod_fwd(name, c_all, w, bias):
    depth, d, n = w.shape
    rows = c_all.shape[0]

    def body(c_ref, w_ref, b_ref, o_ref):
        cv = c_ref[...]
        cond = (cv * _sigmoid(cv)).astype(BF16)
        o_ref[0] = jnp.dot(cond, w_ref[0].astype(BF16), preferred_element_type=F32) + b_ref[0]

    return _pc(body, name=name, grid=(depth,), out_shape=_sds((depth, rows, n), F32),
               in_specs=[pl.BlockSpec((rows, d), lambda l: (0, 0)), pl.BlockSpec((1, d, n), lambda l: (l, 0, 0)),
                         pl.BlockSpec((1, 1, n), lambda l: (l, 0, 0))],
               out_specs=pl.BlockSpec((1, rows, n), lambda l: (l, 0, 0)), compiler_params=_cp("parallel"))(c_all, w, bias)


def _mod_bwd(name, c_all, dmod_cols, dmod_all):
    depth, rows, n = dmod_cols.shape
    d = c_all.shape[1]
    n_all = dmod_all.shape[2]
    tn = _div(n, 512, LANE)

    def body(c_ref, dm_ref, dma_ref, gw_ref, gb_ref):
        cv = c_ref[...]
        cond = (cv * _sigmoid(cv)).astype(BF16)
        gw_ref[0] = _tn(cond, dm_ref[0].astype(BF16))

        @pl.when(pl.program_id(1) == 0)
        def _():
            gb_ref[0] = jnp.sum(dma_ref[0], axis=0, keepdims=True)

    return _pc(body, name=name, grid=(depth, n // tn),
               out_shape=(_sds((depth, d, n), F32), _sds((depth, 1, n_all), F32)),
               in_specs=[pl.BlockSpec((rows, d), lambda l, j: (0, 0)), pl.BlockSpec((1, rows, tn), lambda l, j: (l, 0, j)),
                         pl.BlockSpec((1, rows, n_all), lambda l, j: (l, 0, 0))],
               out_specs=(pl.BlockSpec((1, d, tn), lambda l, j: (l, 0, j)), pl.BlockSpec((1, 1, n_all), lambda l, j: (l, 0, 0))),
               compiler_params=_cp("parallel", "arbitrary"))(c_all, dmod_cols, dmod_all)


def _adamw(name, w, g, m, v):
    shape = w.shape
    cols = shape[-1]
    rows = _size(shape) // cols
    tr = _div(rows, max(SUBLANE, (2 ** 19) // cols // SUBLANE * SUBLANE), SUBLANE)
    c1 = 1.0 - ADAM_B1 ** ADAM_STEP
    c2 = 1.0 - ADAM_B2 ** ADAM_STEP

    def body(w_ref, g_ref, m_ref, v_ref, d_ref, mo_ref, vo_ref):
        gv = g_ref[...]
        m_new = ADAM_B1 * m_ref[...] + (1.0 - ADAM_B1) * gv
        v_new = ADAM_B2 * v_ref[...] + (1.0 - ADAM_B2) * (gv * gv)
        m_hat = m_new / c1
        v_hat = v_new / c2
        d_ref[...] = -ADAM_LR * (m_hat / (jnp.sqrt(v_hat) + ADAM_EPS) + ADAM_WD * w_ref[...])
        mo_ref[...] = m_new
        vo_ref[...] = v_new

    spec = pl.BlockSpec((tr, cols), lambda i: (i, 0))
    o = _sds((rows, cols), F32)
    outs = _pc(body, name=name, grid=(rows // tr,), out_shape=(o, o, o), in_specs=[spec] * 4, out_specs=(spec,) * 3,
               compiler_params=_cp("parallel"))(*[a.reshape(rows, cols) for a in (w, g, m, v)])
    return tuple(a.reshape(shape) for a in outs)


def _rope_tables(positions):
    half = QK_ROPE // 2
    inv_freq = ROPE_THETA ** (-jnp.arange(0, QK_ROPE, 2, dtype=F32) / QK_ROPE)
    ang = positions.astype(F32)[..., None] * inv_freq
    cos, sin = jnp.cos(ang), jnp.sin(ang)
    lead = positions.shape
    ones = jnp.ones(lead + (QK_NOPE,), F32)
    tail_one = jnp.ones(lead + (HEAD_PAD - QK_NOPE - QK_ROPE,), F32)
    cos_t = jnp.concatenate([ones, cos, cos, tail_one], axis=-1)
    sin_t = jnp.concatenate([0 * ones, -sin, sin, 0 * tail_one], axis=-1)
    return cos_t, sin_t


def _pad_heads(w, n_heads, parts, axis):
    w = jnp.moveaxis(w, axis, -1)
    lead = w.shape[:-1]
    per = w.shape[-1] // n_heads
    w = w.reshape(lead + (n_heads, per))
    kept = jnp.concatenate([w[..., a:b_] for a, b_ in parts], axis=-1)
    pad = HEAD_PAD - kept.shape[-1]
    kept = jnp.concatenate([kept, jnp.zeros(lead + (n_heads, pad), w.dtype)], axis=-1)
    return jnp.moveaxis(kept.reshape(lead + (n_heads * HEAD_PAD,)), -1, axis)


def _unpad_heads(g, n_heads, width, axis):
    g = jnp.moveaxis(g, axis, -1)
    lead = g.shape[:-1]
    g = g.reshape(lead + (n_heads, HEAD_PAD))[..., :width]
    return g, lead


def kernel(x, c, positions, mod_w, mod_b, ln_g, ln_b, pool_w, pool_scale, mla_w_a, mla_q_norm, mla_w_uq, mla_kv_norm, mla_w_ukv, mla_w_o, sc_w_in, sc_conv, sc_w_out, ffn_w_up, ffn_conv, ffn_conv_b, ffn_w_down, loss_target, m_mod_w, m_mod_b, m_ln_g, m_ln_b, m_pool_w, m_pool_scale, m_mla_w_a, m_mla_q_norm, m_mla_w_uq, m_mla_kv_norm, m_mla_w_ukv, m_mla_w_o, m_sc_w_in, m_sc_conv, m_sc_w_out, m_ffn_w_up, m_ffn_conv, m_ffn_conv_b, m_ffn_w_down, v_mod_w, v_mod_b, v_ln_g, v_ln_b, v_pool_w, v_pool_scale, v_mla_w_a, v_mla_q_norm, v_mla_w_uq, v_mla_kv_norm, v_mla_w_ukv, v_mla_w_o, v_sc_w_in, v_sc_conv, v_sc_w_out, v_ffn_w_up, v_ffn_conv, v_ffn_conv_b, v_ffn_w_down):
    wts = dict(mod_w=mod_w, mod_b=mod_b, ln_g=ln_g, ln_b=ln_b, pool_w=pool_w, pool_scale=pool_scale, mla_w_a=mla_w_a,
               mla_q_norm=mla_q_norm, mla_w_uq=mla_w_uq, mla_kv_norm=mla_kv_norm, mla_w_ukv=mla_w_ukv, mla_w_o=mla_w_o,
               sc_w_in=sc_w_in, sc_conv=sc_conv, sc_w_out=sc_w_out, ffn_w_up=ffn_w_up, ffn_conv=ffn_conv,
               ffn_conv_b=ffn_conv_b, ffn_w_down=ffn_w_down)
    mom1 = dict(mod_w=m_mod_w, mod_b=m_mod_b, ln_g=m_ln_g, ln_b=m_ln_b, pool_w=m_pool_w, pool_scale=m_pool_scale,
                mla_w_a=m_mla_w_a, mla_q_norm=m_mla_q_norm, mla_w_uq=m_mla_w_uq, mla_kv_norm=m_mla_kv_norm,
                mla_w_ukv=m_mla_w_ukv, mla_w_o=m_mla_w_o, sc_w_in=m_sc_w_in, sc_conv=m_sc_conv, sc_w_out=m_sc_w_out,
                ffn_w_up=m_ffn_w_up, ffn_conv=m_ffn_conv, ffn_conv_b=m_ffn_conv_b, ffn_w_down=m_ffn_w_down)
    mom2 = dict(mod_w=v_mod_w, mod_b=v_mod_b, ln_g=v_ln_g, ln_b=v_ln_b, pool_w=v_pool_w, pool_scale=v_pool_scale,
                mla_w_a=v_mla_w_a, mla_q_norm=v_mla_q_norm, mla_w_uq=v_mla_w_uq, mla_kv_norm=v_mla_kv_norm,
                mla_w_ukv=v_mla_w_ukv, mla_w_o=v_mla_w_o, sc_w_in=v_sc_w_in, sc_conv=v_sc_conv, sc_w_out=v_sc_w_out,
                ffn_w_up=v_ffn_w_up, ffn_conv=v_ffn_conv, ffn_conv_b=v_ffn_conv_b, ffn_w_down=v_ffn_w_down)

    bsz, seq, d = x.shape
    depth = mod_b.shape[0]
    n_tok = bsz * seq
    n_heads = d // V_HEAD
    ql, kvl = mla_q_norm.shape[1], mla_kv_norm.shape[1]
    alpha = float((2 * depth) ** 0.25)
    sm_scale = float((QK_NOPE + QK_ROPE) ** -0.5)
    mx, my, mc = lax.axis_index("x"), lax.axis_index("y"), lax.axis_index("c")
    chip = 2 * mx + my
    dev = 2 * chip + mc

    small_names = list(SMALL_SHARDED)
    small_pack, small_spans = _pack_rows([c] + [wts[n] for n in small_names], F32, SUBLANE)
    rows_small = small_pack.shape[0]
    small_all = _all_gather8("gather_small_params", small_pack, True).reshape(N_DEV, rows_small * PACK_COLS)
    c_all = small_all[:, :c.size].reshape(N_DEV * bsz, d)
    per_chip = small_all[0::2]
    full = dict(wts)
    for n, (off, shape) in zip(small_names, small_spans[1:]):
        blocks = per_chip[:, off:off + _size(shape)].reshape((N_CHIP,) + tuple(shape))
        full[n] = _join_chips(blocks, SMALL_SHARDED[n])

    n_mod = mod_w.shape[2]
    bias_cols = lax.dynamic_slice_in_dim(mod_b, chip * n_mod, n_mod, axis=1)[:, None, :]
    mod_cols = _mod_fwd("mod_fwd", c_all, mod_w, bias_cols)
    half_rows = (N_DEV * bsz) // 2
    mod_half = lax.dynamic_slice_in_dim(mod_cols, mc * half_rows, half_rows, axis=1).reshape(depth * half_rows, n_mod)
    mod_all = _all_gather8("gather_mod", mod_half, True).reshape(N_CHIP, 2, depth, half_rows, n_mod)
    mod_all = jnp.transpose(mod_all, (2, 1, 3, 0, 4)).reshape(depth, N_DEV * bsz, N_CHIP * n_mod)
    mod_mine = lax.dynamic_slice_in_dim(mod_all, dev * bsz, bsz, axis=1)
    mods = [[mod_mine[l, :, k * d:(k + 1) * d][:, None, :] for k in range(6)] for l in range(depth)]

    big_names = list(BIG)
    gathered = _gather_weights("gather_weights", [wts[n].astype(BF16).reshape(-1, wts[n].shape[-1]) for n in big_names])
    by_chip = {n: g.reshape((N_CHIP,) + wts[n].shape) for n, g in zip(big_names, gathered)}
    f_hid = ffn_w_down.shape[1] * N_CHIP
    for n in ('pool_w', 'mla_w_a', 'mla_w_uq', 'mla_w_ukv', 'mla_w_o', 'sc_w_out'):
        full[n] = jnp.concatenate([by_chip[n][j] for j in range(N_CHIP)], axis=BIG[n])
    w_up_cols = by_chip['ffn_w_up'].reshape(N_CHIP, depth * d, -1)
    w_up_rows = jnp.transpose(by_chip['ffn_w_up'], (1, 0, 3, 2)).reshape(1, depth * 2 * f_hid, d)
    w_down_rows = jnp.transpose(by_chip['ffn_w_down'], (1, 0, 2, 3)).reshape(1, depth * f_hid, d)
    w_down_t = jnp.transpose(by_chip['ffn_w_down'], (1, 3, 0, 2)).reshape(1, depth * d, f_hid)
    n_sc = sc_w_in.shape[0]
    w_in_cols = by_chip['sc_w_in'].reshape(N_CHIP, n_sc * d, -1)
    w_in_rows = jnp.transpose(by_chip['sc_w_in'], (1, 0, 3, 2)).reshape(1, n_sc * 3 * d, d)

    nope_rope = [(0, QK_NOPE + QK_ROPE)]
    cos_t, sin_t = _rope_tables(positions)

    def t2(a):
        return a.reshape(n_tok, a.shape[-1])

    def t3(a):
        return a.reshape(bsz, seq, a.shape[-1])

    saved = []
    xin = x
    u = _modulate("modulate_in", x, mods[0][1], mods[0][0])
    loss_acc = None
    for l in range(depth):
        sh1, sc1, g1, sh2, sc2, g2 = mods[l]
        kind, j = l % 3, l // 3
        st = dict(x=xin)
        if kind == 0:
            w = full['pool_w'][j]
            st.update(w=w, w_t=jnp.swapaxes(w, 1, 2), scale=full['pool_scale'][j][None, :])
            y = _pool_fwd(f"pool_fwd_{l}", xin, sc1, sh1, st['w'], st['scale'])
        elif kind == 1:
            wa = full['mla_w_a'][j]
            zeros = jnp.zeros((d, QK_NOPE), BF16)
            w_a = jnp.concatenate([wa[:, :ql + kvl], zeros, wa[:, ql + kvl:], zeros[:, :HEAD_PAD - QK_NOPE - QK_ROPE]], axis=1)
            w_uq = _pad_heads(full['mla_w_uq'][j], n_heads, nope_rope, 1)
            w_kv = jnp.concatenate([_pad_heads(full['mla_w_ukv'][j], n_heads, [(0, QK_NOPE)], 1),
                                    _pad_heads(full['mla_w_ukv'][j], n_heads, [(QK_NOPE, QK_NOPE + V_HEAD)], 1)], axis=1)
            w_o = _pad_heads(full['mla_w_o'][j], n_heads, [(0, V_HEAD)], 0)
            qn, kvn = mla_q_norm[j][None, :], mla_kv_norm[j][None, :]
            a = t3(_mm_nn(f"mla_a_{l}", [(t2(u), _w2(w_a))], F32))
            cq, ckv, kpe = _mla_norm_fwd(f"mla_norm_fwd_{l}", a, qn, kvn, cos_t, sin_t)
            q_raw = t3(_mm_nn(f"mla_q_{l}", [(t2(cq), _w2(w_uq))], F32))
            kv_raw = t3(_mm_nn(f"mla_kv_{l}", [(t2(ckv), _w2(w_kv))], F32))
            qh, kh, vh = _mla_prep_fwd(f"mla_prep_fwd_{l}", q_raw, kv_raw, kpe, cos_t, sin_t, n_heads)
            o, lse = _flash_fwd(f"flash_fwd_{l}", qh, kh, vh, n_heads, sm_scale)
            y = t3(_mm_nn(f"mla_o_{l}", [(t2(o), _w2(w_o))], F32))
            st.update(u=u, w_a=w_a, w_uq=w_uq, w_kv=w_kv, w_o=w_o, qn=qn, kvn=kvn, a=a, cq=cq, ckv=ckv,
                      qh=qh, kh=kh, vh=vh, o=o, lse=lse)
        else:
            w_out, cw = full['sc_w_out'][j], full['sc_conv'][j]
            q = t3(_mm_nn(f"sc_in_{l}", [(t2(u), (w_in_cols, j))], F32))
            r = _shortconv_fwd(f"shortconv_fwd_{l}", q, cw)
            y = t3(_mm_nn(f"sc_out_{l}", [(t2(r), _w2(w_out))], F32))
            st.update(u=u, w_out=w_out, cw=cw, q=q, r=r)
        lng, lnb = full['ln_g'][l], full['ln_b'][l]
        z1, xmid, u2 = _ln_mod_fwd(f"ln_mod_a_{l}", alpha, xin, y, g1, lng[0:1], lnb[0:1], sc2, sh2)
        cwf, cbf = full['ffn_conv'][l], ffn_conv_b[l][None, :]
        p = t3(_mm_nn(f"ffn_up_{l}", [(t2(u2), (w_up_cols, l))], F32))
        act = _convglu_fwd(f"convglu_fwd_{l}", p, cwf, cbf)
        y2 = t3(_mm_nn(f"ffn_down_{l}", [(t2(act), (w_down_rows, l))], F32))
        st.update(y1=y, z1=z1, xmid=xmid, u2=u2, p=p, act=act, y2=y2, cwf=cwf, cbf=cbf, lng=lng, lnb=lnb)
        if l + 1 < depth:
            nsh1, nsc1 = mods[l + 1][0], mods[l + 1][1]
            z2, xin, u = _ln_mod_fwd(f"ln_mod_b_{l}", alpha, xmid, y2, g2, lng[1:2], lnb[1:2], nsc1, nsh1)
        else:
            z2, ct, loss_acc = _ln_loss_fwd("ln_loss", alpha, xmid, y2, g2, lng[1:2], lnb[1:2], loss_target)
        st.update(z2=z2)
        saved.append(st)
    loss = lax.psum(loss_acc[0, 0], ("x", "y", "c"))

    grads = {}
    dmods = [[None] * 6 for _ in range(depth)]
    g_ln_g = [[None, None] for _ in range(depth)]
    g_ln_b = [[None, None] for _ in range(depth)]
    stack = {n: [None] * wts[n].shape[0] for n in ('pool_scale', 'mla_q_norm', 'mla_kv_norm', 'sc_conv', 'ffn_conv',
                                                    'ffn_conv_b')}
    units = {n: [None] * wts[n].shape[0] for n in big_names}
    upstream = (ct,)
    for l in reversed(range(depth)):
        st = saved[l]
        sh1, sc1, g1, sh2, sc2, g2 = mods[l]
        kind, j = l % 3, l // 3
        res = _sub_bwd(f"sub_bwd_b_{l}", alpha, upstream, st['z2'], st['y2'], g2, st['lng'][1:2])
        dz2, dy2, dmods[l][5], g_ln_g[l][1], g_ln_b[l][1] = res[:5]
        if l + 1 < depth:
            dmods[l + 1][1], dmods[l + 1][0] = res[5], res[6]
        dy2f = t2(dy2)
        da = t3(_mm_nn(f"ffn_down_bwd_{l}", [(dy2f, (w_down_t, l))], F32))
        units['ffn_w_down'][l] = _mm_tn(f"ffn_down_dw_{l}", t2(st['act']), [dy2f]).reshape(N_CHIP, f_hid // N_CHIP, d)
        dpv, dpg, dcw, dcb = _convglu_bwd(f"convglu_bwd_{l}", st['p'], da, st['cwf'], st['cbf'])
        stack['ffn_conv'][l], stack['ffn_conv_b'][l] = dcw, dcb[0]
        du2 = t3(_mm_nn(f"ffn_up_bwd_{l}", [(t2(dpv), (w_up_rows, 2 * l)), (t2(dpg), (w_up_rows, 2 * l + 1))], F32))
        units['ffn_w_up'][l] = _mm_tn(f"ffn_up_dw_{l}", t2(st['u2']), [t2(dpv), t2(dpg)], N_CHIP)
        res = _sub_bwd(f"sub_bwd_a_{l}", alpha, (dz2, du2, st['xmid'], sc2), st['z1'], st['y1'], g1, st['lng'][0:1])
        dz1, dy1, dmods[l][2], g_ln_g[l][0], g_ln_b[l][0], dmods[l][4], dmods[l][3] = res
        dy1f = t2(dy1)
        if kind == 0:
            du1, dw, dscale = _pool_bwd(f"pool_bwd_{l}", st['x'], sc1, sh1, dy1, st['w'], st['w_t'], st['scale'])
            stack['pool_scale'][j] = dscale[0]
            grp = dw.shape[1] // N_CHIP
            units['pool_w'][j] = jnp.transpose(dw.reshape(POOL_GROUPS, N_CHIP, grp, dw.shape[2]),
                                               (1, 0, 2, 3)).reshape(N_CHIP, POOL_GROUPS * grp, dw.shape[2])
        elif kind == 1:
            do = t3(_mm_nn(f"mla_o_bwd_{l}", [(dy1f, _w2(jnp.swapaxes(st['w_o'], 0, 1)))], BF16))
            gwo, _ = _unpad_heads(_mm_tn(f"mla_o_dw_{l}", t2(st['o']), [dy1f])[0], n_heads, V_HEAD, 0)
            units['mla_w_o'][j] = jnp.moveaxis(gwo.reshape(d, n_heads * V_HEAD), -1, 0).reshape(N_CHIP, -1, d)
            fa = (st['qh'], st['kh'], st['vh'], st['o'], st['lse'], do, n_heads, sm_scale)
            dq = _flash_dq(f"flash_dq_{l}", *fa)
            dk, dv = _flash_dkv(f"flash_dkv_{l}", *fa)
            dq_raw, dkv_raw, dkpe = _mla_prep_bwd(f"mla_prep_bwd_{l}", dq, dk, dv, cos_t, sin_t, n_heads)
            dq_raw, dkv_raw = t2(dq_raw), t2(dkv_raw)
            dcq = t3(_mm_nn(f"mla_q_bwd_{l}", [(dq_raw, _w2(jnp.swapaxes(st['w_uq'], 0, 1)))], F32))
            dckv = t3(_mm_nn(f"mla_kv_bwd_{l}", [(dkv_raw, _w2(jnp.swapaxes(st['w_kv'], 0, 1)))], F32))
            gq, _ = _unpad_heads(_mm_tn(f"mla_q_dw_{l}", t2(st['cq']), [dq_raw])[0], n_heads, QK_NOPE + QK_ROPE, 1)
            units['mla_w_uq'][j] = _cols_by_chip(gq.reshape(ql, n_heads * (QK_NOPE + QK_ROPE)))
            gkv = _mm_tn(f"mla_kv_dw_{l}", t2(st['ckv']), [dkv_raw])[0]
            gk, _ = _unpad_heads(gkv[:, :n_heads * HEAD_PAD], n_heads, QK_NOPE, 1)
            gv, _ = _unpad_heads(gkv[:, n_heads * HEAD_PAD:], n_heads, V_HEAD, 1)
            units['mla_w_ukv'][j] = _cols_by_chip(
                jnp.concatenate([gk, gv], axis=-1).reshape(kvl, n_heads * (QK_NOPE + V_HEAD)))
            da_, dqn, dkvn = _mla_norm_bwd(f"mla_norm_bwd_{l}", st['a'], dcq, dckv, dkpe, st['qn'], st['kvn'])
            stack['mla_q_norm'][j], stack['mla_kv_norm'][j] = dqn[0], dkvn[0]
            du1 = t3(_mm_nn(f"mla_a_bwd_{l}", [(t2(da_), _w2(jnp.swapaxes(st['w_a'], 0, 1)))], F32))
            gwa = _mm_tn(f"mla_a_dw_{l}", t2(st['u']), [t2(da_)])[0]
            units['mla_w_a'][j] = _cols_by_chip(jnp.concatenate(
                [gwa[:, :ql + kvl], gwa[:, ql + kvl + QK_NOPE:ql + kvl + QK_NOPE + QK_ROPE]], axis=1))
        else:
            dr = t3(_mm_nn(f"sc_out_bwd_{l}", [(dy1f, _w2(jnp.swapaxes(st['w_out'], 0, 1)))], F32))
            units['sc_w_out'][j] = _mm_tn(f"sc_out_dw_{l}", t2(st['r']), [dy1f]).reshape(N_CHIP, -1, d)
            dgb, dgc, dh, dcw = _shortconv_bwd(f"shortconv_bwd_{l}", st['q'], dr, st['cw'])
            stack['sc_conv'][j] = dcw
            parts = [t2(dgb), t2(dgc), t2(dh)]
            du1 = t3(_mm_nn(f"sc_in_bwd_{l}", [(parts[k], (w_in_rows, 3 * j + k)) for k in range(3)], F32))
            units['sc_w_in'][j] = _mm_tn(f"sc_in_dw_{l}", t2(st['u']), parts, N_CHIP)
        upstream = (dz1, du1, st['x'], sc1)
    grad_x, dmods[0][1], dmods[0][0] = _input_bwd("input_bwd", alpha, upstream[0], upstream[1], x, mods[0][1])

    for n, parts in stack.items():
        grads[n] = jnp.stack(parts)
    grads['ln_g'] = jnp.stack([jnp.concatenate(r, axis=0) for r in g_ln_g])
    grads['ln_b'] = jnp.stack([jnp.concatenate(r, axis=0) for r in g_ln_b])
    dmod_mine = jnp.stack([jnp.concatenate([t[:, 0, :] for t in dmods[l]], axis=-1) for l in range(depth)])

    small_grad_names = small_names + ['mla_q_norm', 'mla_kv_norm', 'ffn_conv_b']
    sg_pack, sg_spans = _pack_rows([dmod_mine] + [grads[n] for n in small_grad_names], F32, SUBLANE)
    rows_sg = sg_pack.shape[0]
    sg_all = _all_gather8("gather_small_grads", sg_pack, True).reshape(N_DEV, rows_sg, PACK_COLS)
    dmod_all = sg_all.reshape(N_DEV, -1)[:, :dmod_mine.size].reshape(N_DEV, depth, bsz, 6 * d)
    dmod_all = jnp.transpose(dmod_all, (1, 0, 2, 3)).reshape(depth, N_DEV * bsz, 6 * d)
    sg_sum = _sum8("sum_small_grads", sg_all).reshape(-1)
    for n, (off, shape) in zip(small_grad_names, sg_spans[1:]):
        g_full = sg_sum[off:off + _size(shape)].reshape(shape)
        if n in SMALL_SHARDED:
            ax = SMALL_SHARDED[n]
            width = shape[ax] // N_CHIP
            g_full = lax.dynamic_slice_in_dim(g_full, chip * width, width, axis=ax)
        grads[n] = g_full
    dmod_cols = lax.dynamic_slice_in_dim(dmod_all, chip * n_mod, n_mod, axis=2)
    grads['mod_w'], gb = _mod_bwd("mod_bwd", c_all, dmod_cols, dmod_all)
    grads['mod_b'] = gb[:, 0, :]

    unit_list, layers_of = [], []
    for n in big_names:
        layers_of.append(list(range(len(unit_list), len(unit_list) + len(units[n]))))
        unit_list += units[n]
    received = _scatter_grads("scatter_big_grads", unit_list)
    reduced = [_sum8(f"sum_big_grads_{u}", r) for u, r in enumerate(received)]
    for n, g in zip(big_names, _swap_halves("swap_big_grad_halves", reduced, layers_of)):
        grads[n] = g.reshape(wts[n].shape)

    deltas, new_m, new_v = {}, {}, {}
    for n in WEIGHTS:
        deltas[n], new_m[n], new_v[n] = _adamw(f"adamw_{n}", wts[n], grads[n], mom1[n], mom2[n])
    return (loss, grad_x, *[grads[n] for n in WEIGHTS], *[deltas[n] for n in WEIGHTS],
            *[new_m[n] for n in WEIGHTS], *[new_v[n] for n in WEIGHTS])
```

```python
import functools

import jax
import jax.numpy as jnp
from jax import lax
from jax.experimental import pallas as pl
from jax.experimental.pallas import tpu as pltpu

F32 = jnp.float32
BF16 = jnp.bfloat16
MESH = pl.DeviceIdType.MESH

N_DEV = 8
N_CHIP = 4
LANE = 128
SUBLANE = 8
VMEM_LIMIT_BYTES = 56 * 2 ** 20
PACK_COLS = 1024

LN_EPS = 1e-5
RMS_EPS = 1e-6
QK_NOPE, QK_ROPE, V_HEAD = 64, 32, 64
ROPE_THETA = 10000.0
HEAD_PAD = 128
POOL_GROUPS = 4
POOL_HALO = 16
CONV_HALO = 8
ADAM_LR, ADAM_B1, ADAM_B2, ADAM_EPS, ADAM_WD, ADAM_STEP = 0.001, 0.9, 0.999, 1e-08, 0.01, 10

WEIGHTS = ['mod_w', 'mod_b', 'ln_g', 'ln_b', 'pool_w', 'pool_scale', 'mla_w_a', 'mla_q_norm', 'mla_w_uq',
           'mla_kv_norm', 'mla_w_ukv', 'mla_w_o', 'sc_w_in', 'sc_conv', 'sc_w_out', 'ffn_w_up', 'ffn_conv',
           'ffn_conv_b', 'ffn_w_down']
BIG = {'pool_w': 2, 'mla_w_a': 2, 'mla_w_uq': 2, 'mla_w_ukv': 2, 'mla_w_o': 1, 'sc_w_in': 2, 'sc_w_out': 1,
       'ffn_w_up': 2, 'ffn_w_down': 1}
SMALL_SHARDED = {'ln_g': 2, 'ln_b': 2, 'pool_scale': 1, 'sc_conv': 2, 'ffn_conv': 2}
REPLICATED = ['mod_b', 'mla_q_norm', 'mla_kv_norm', 'ffn_conv_b']


def _pc(body, **kw):
    return pl.pallas_call(body, **kw)


def _cp(*sem):
    return pltpu.CompilerParams(dimension_semantics=sem, vmem_limit_bytes=VMEM_LIMIT_BYTES)


def _div(n, cap, mult):
    best = None
    for d in range(mult, min(n, cap) + 1, mult):
        if n % d == 0:
            best = d
    return best if best is not None else n


def _sds(shape, dtype):
    return jax.ShapeDtypeStruct(tuple(shape), dtype)


def _flip(v, bit):
    return 1 - v if bit else v


def _all_gather8(name, x_shard, in_vmem):
    m_per, n = x_shard.shape
    space = pltpu.VMEM if in_vmem else pltpu.HBM

    def body(x_ref, out_ref, send_sems, recv_sems, local_sem):
        x, y, c = lax.axis_index("x"), lax.axis_index("y"), lax.axis_index("c")
        me, sibling = (x, y, c), (x, y, 1 - c)
        chips = [(1 - x, y), (x, 1 - y), (1 - x, 1 - y)]

        def rows(px, py, pc_):
            return out_ref.at[pl.ds((4 * px + 2 * py + pc_) * m_per, m_per), :]

        def copy(k, block, to, src=None):
            return pltpu.make_async_remote_copy(
                src_ref=rows(*block) if src is None else src, dst_ref=rows(*block),
                send_sem=send_sems.at[k], recv_sem=recv_sems.at[k], device_id=to, device_id_type=MESH)

        mine = pltpu.make_async_copy(x_ref, rows(*me), local_sem)
        mine.start()
        first = [copy(0, me, sibling, src=x_ref)]
        first += [copy(1 + j, me, (*chip, c), src=x_ref) for j, chip in enumerate(chips)]
        for cp in first:
            cp.start()
        passed = [copy(4 + j, (*chip, c), sibling) for j, chip in enumerate(chips)]
        for j, chip in enumerate(chips):
            copy(1 + j, (*chip, c), me).wait_recv()
            passed[j].start()
        copy(0, sibling, me).wait_recv()
        for j, chip in enumerate(chips):
            copy(4 + j, (*chip, 1 - c), me).wait_recv()
        for cp in first + passed:
            cp.wait_send()
        mine.wait()

    return _pc(
        body, name=name, out_shape=_sds((N_DEV * m_per, n), x_shard.dtype),
        in_specs=[pl.BlockSpec(memory_space=space)], out_specs=pl.BlockSpec(memory_space=space),
        scratch_shapes=[pltpu.SemaphoreType.DMA((7,)), pltpu.SemaphoreType.DMA((7,)), pltpu.SemaphoreType.DMA],
        compiler_params=pltpu.CompilerParams(vmem_limit_bytes=VMEM_LIMIT_BYTES),
    )(x_shard)


def _gather_weights(name, shards):
    n_t = len(shards)
    halves = [s.shape[0] // 2 for s in shards]

    def body(*refs):
        x_refs, o_refs = refs[:n_t], refs[n_t:2 * n_t]
        send_sems, recv_sems, local_sems = refs[2 * n_t:]
        x, y, c = lax.axis_index("x"), lax.axis_index("y"), lax.axis_index("c")
        me, sibling = (x, y, c), (x, y, 1 - c)
        chips = [(1 - x, y), (x, 1 - y), (1 - x, 1 - y)]

        def slot(t, px, py, pc_):
            return o_refs[t].at[4 * px + 2 * py + pc_]

        def my_rows(t):
            return x_refs[t].at[pl.ds(c * halves[t], halves[t]), :]

        def copy(t, k, block, to, src=None):
            return pltpu.make_async_remote_copy(
                src_ref=slot(t, *block) if src is None else src, dst_ref=slot(t, *block),
                send_sem=send_sems.at[t, k], recv_sem=recv_sems.at[t, k], device_id=to, device_id_type=MESH)

        local = [pltpu.make_async_copy(my_rows(t), slot(t, *me), local_sems.at[t]) for t in range(n_t)]
        for cp in local:
            cp.start()
        first = []
        for t in range(n_t):
            first += [copy(t, 1 + j, me, (*chip, c), src=my_rows(t)) for j, chip in enumerate(chips)]
            first.append(copy(t, 0, me, sibling, src=my_rows(t)))
        for cp in first:
            cp.start()
        passed = []
        for j, chip in enumerate(chips):
            for t in range(n_t):
                copy(t, 1 + j, (*chip, c), me).wait_recv()
                passed.append(copy(t, 4 + j, (*chip, c), sibling))
                passed[-1].start()
        for t in range(n_t):
            copy(t, 0, sibling, me).wait_recv()
        for j, chip in enumerate(chips):
            for t in range(n_t):
                copy(t, 4 + j, (*chip, 1 - c), me).wait_recv()
        for cp in first + passed:
            cp.wait_send()
        for cp in local:
            cp.wait()

    hbm = pl.BlockSpec(memory_space=pltpu.HBM)
    return _pc(
        body, name=name, out_shape=tuple(_sds((N_DEV, h, s.shape[1]), s.dtype) for h, s in zip(halves, shards)),
        in_specs=[hbm] * n_t, out_specs=(hbm,) * n_t,
        scratch_shapes=[pltpu.SemaphoreType.DMA((n_t, 7)), pltpu.SemaphoreType.DMA((n_t, 7)),
                        pltpu.SemaphoreType.DMA((n_t,))],
    )(*shards)


def _scatter_grads(name, units):
    n_u = len(units)
    halves = [u.shape[1] // 2 for u in units]

    def body(*refs):
        u_refs, r_refs = refs[:n_u], refs[n_u:2 * n_u]
        send_sems, recv_sems, local_sems = refs[2 * n_u:]
        x, y, c = lax.axis_index("x"), lax.axis_index("y"), lax.axis_index("c")
        me = 4 * x + 2 * y + c

        def piece(t, chip, core):
            return u_refs[t].at[chip, pl.ds(core * halves[t], halves[t]), :]

        local = [pltpu.make_async_copy(piece(t, 2 * x + y, c), r_refs[t].at[me], local_sems.at[t]) for t in range(n_u)]
        for cp in local:
            cp.start()
        sends, recvs = [], []
        for k in range(1, N_DEV):
            px, py, pcc = _flip(x, (k >> 2) & 1), _flip(y, (k >> 1) & 1), _flip(c, k & 1)
            peer = 4 * px + 2 * py + pcc
            for t in range(n_u):
                sends.append(pltpu.make_async_remote_copy(
                    src_ref=piece(t, 2 * px + py, pcc), dst_ref=r_refs[t].at[me], send_sem=send_sems.at[t, k - 1],
                    recv_sem=recv_sems.at[t, k - 1], device_id=(px, py, pcc), device_id_type=MESH))
                recvs.append(pltpu.make_async_remote_copy(
                    src_ref=piece(t, 2 * x + y, c), dst_ref=r_refs[t].at[peer], send_sem=send_sems.at[t, k - 1],
                    recv_sem=recv_sems.at[t, k - 1], device_id=(px, py, pcc), device_id_type=MESH))
        for cp in sends:
            cp.start()
        for cp in recvs:
            cp.wait_recv()
        for cp in sends:
            cp.wait_send()
        for cp in local:
            cp.wait()

    hbm = pl.BlockSpec(memory_space=pltpu.HBM)
    return _pc(
        body, name=name, out_shape=tuple(_sds((N_DEV, h, u.shape[2]), u.dtype) for h, u in zip(halves, units)),
        in_specs=[hbm] * n_u, out_specs=(hbm,) * n_u,
        scratch_shapes=[pltpu.SemaphoreType.DMA((n_u, 7)), pltpu.SemaphoreType.DMA((n_u, 7)),
                        pltpu.SemaphoreType.DMA((n_u,))],
    )(*units)


def _swap_halves(name, bufs):
    n_u = len(bufs)

    def body(*refs):
        o_refs = refs[n_u:2 * n_u]
        send_sems, recv_sems = refs[2 * n_u:]
        x, y, c = lax.axis_index("x"), lax.axis_index("y"), lax.axis_index("c")

        def rows(u, core):
            h = bufs[u].shape[0] // 2
            return o_refs[u].at[pl.ds(core * h, h), :]

        sends = [pltpu.make_async_remote_copy(src_ref=rows(u, c), dst_ref=rows(u, c), send_sem=send_sems.at[u],
                                              recv_sem=recv_sems.at[u], device_id=(x, y, 1 - c), device_id_type=MESH)
                 for u in range(n_u)]
        recvs = [pltpu.make_async_remote_copy(src_ref=rows(u, c), dst_ref=rows(u, 1 - c), send_sem=send_sems.at[u],
                                              recv_sem=recv_sems.at[u], device_id=(x, y, 1 - c), device_id_type=MESH)
                 for u in range(n_u)]
        for cp in sends:
            cp.start()
        for cp in recvs:
            cp.wait_recv()
        for cp in sends:
            cp.wait_send()

    hbm = pl.BlockSpec(memory_space=pltpu.HBM)
    return _pc(
        body, name=name, out_shape=tuple(_sds(b.shape, b.dtype) for b in bufs), in_specs=[hbm] * n_u,
        out_specs=(hbm,) * n_u, input_output_aliases={u: u for u in range(n_u)},
        scratch_shapes=[pltpu.SemaphoreType.DMA((n_u,)), pltpu.SemaphoreType.DMA((n_u,))],
    )(*bufs)


def _sum8_into_half(name, parts, core):
    _, h, n = parts.shape
    tm = _div(h, 256, 16)
    per = h // tm

    def body(core_ref, p_ref, o_ref):
        acc = p_ref[0].astype(F32)
        for s in range(1, N_DEV):
            acc = acc + p_ref[s].astype(F32)
        o_ref[...] = acc

    grid_spec = pltpu.PrefetchScalarGridSpec(
        num_scalar_prefetch=1, grid=(per,),
        in_specs=[pl.BlockSpec((N_DEV, tm, n), lambda i, core_ref: (0, i, 0))],
        out_specs=pl.BlockSpec((tm, n), lambda i, core_ref: (core_ref[0] * per + i, 0)))
    return _pc(body, name=name, grid_spec=grid_spec, out_shape=_sds((2 * h, n), F32),
               compiler_params=_cp("arbitrary"))(core, parts)


def _sum8(name, parts):
    _, m, n = parts.shape
    tm = _div(m, 256, SUBLANE)

    def body(p_ref, o_ref):
        acc = p_ref[0]
        for s in range(1, N_DEV):
            acc = acc + p_ref[s]
        o_ref[...] = acc

    return _pc(body, name=name, grid=(m // tm,), out_shape=_sds((m, n), F32),
               in_specs=[pl.BlockSpec((N_DEV, tm, n), lambda i: (0, i, 0))],
               out_specs=pl.BlockSpec((tm, n), lambda i: (i, 0)), compiler_params=_cp("parallel"))(parts)


def _pack_rows(arrays, dtype, row_mult):
    flat, spans, off = [], [], 0
    for a in arrays:
        flat.append(a.reshape(-1).astype(dtype))
        spans.append((off, a.shape))
        off += a.size
    quantum = row_mult * PACK_COLS
    total = -(-off // quantum) * quantum
    if total > off:
        flat.append(jnp.zeros((total - off,), dtype))
    return jnp.concatenate(flat).reshape(total // PACK_COLS, PACK_COLS), spans


def _size(shape):
    n = 1
    for s in shape:
        n *= s
    return n


def _join_chips(blocks, axis):
    return jnp.concatenate([blocks[j] for j in range(N_CHIP)], axis=axis)


def _cols_by_chip(g):
    k, n = g.shape
    return jnp.transpose(g.reshape(k, N_CHIP, n // N_CHIP), (1, 0, 2))


def _mm_nn(name, pairs, out_dtype, tm_cap=1024, tn_cap=1536):
    m = pairs[0][0].shape[0]
    nb, _, n4 = pairs[0][1][0].shape
    tm, tn = _div(m, tm_cap, 16), _div(n4, tn_cap, LANE)
    per = n4 // tn
    n_pairs = len(pairs)

    def body(*refs):
        o_ref = refs[-1]
        acc = jnp.dot(refs[0][...], refs[1][0], preferred_element_type=F32)
        for i in range(1, n_pairs):
            acc = acc + jnp.dot(refs[2 * i][...], refs[2 * i + 1][0], preferred_element_type=F32)
        o_ref[...] = acc.astype(o_ref.dtype)

    in_specs, args = [], []
    for a, (w, r) in pairs:
        k = a.shape[1]
        assert w.shape[0] == nb and w.shape[2] == n4 and w.shape[1] % k == 0
        in_specs += [pl.BlockSpec((tm, k), lambda j, i: (i, 0)),
                     pl.BlockSpec((1, k, tn), functools.partial(lambda j, i, r_: (j // per, r_, j % per), r_=r))]
        args += [a, w]
    return _pc(body, name=name, grid=(nb * per, m // tm), out_shape=_sds((m, nb * n4), out_dtype), in_specs=in_specs,
               out_specs=pl.BlockSpec((tm, tn), lambda j, i: (i, j)), compiler_params=_cp("parallel", "parallel"))(*args)


def _mm_tn(name, x, ys, n_blocks=1, out_dtype=F32, tt_cap=512):
    t, k = x.shape
    widths = [y.shape[1] for y in ys]
    n4 = sum(widths) // n_blocks
    common = n4
    for w in widths:
        common = _gcd(common, w)
    tk, tn, tt = _div(k, 1024, LANE), _div(common, 1536, LANE), _div(t, tt_cap, 16)
    per = n4 // tn
    starts, acc_w = [], 0
    for w in widths:
        starts.append(acc_w // tn)
        acc_w += w
    counts = [w // tn for w in widths]
    n_y = len(ys)

    def active(i, j):
        return (j >= starts[i]) & (j < starts[i] + counts[i])

    n_t = t // tt

    def body(*refs):
        x_ref, y_refs, o_ref, acc_ref = refs[0], refs[1:1 + n_y], refs[-2], refs[-1]
        j = pl.program_id(1)

        @pl.when(pl.program_id(2) == 0)
        def _():
            acc_ref[...] = jnp.zeros_like(acc_ref)

        for i in range(n_y):
            @pl.when(active(i, j))
            def _():
                acc_ref[...] += lax.dot_general(x_ref[...], y_refs[i][...], (((0,), (0,)), ((), ())),
                                                preferred_element_type=F32)

        @pl.when(pl.program_id(2) == n_t - 1)
        def _():
            o_ref[0] = acc_ref[...].astype(o_ref.dtype)

    def y_spec(i):
        def index(a, j, s):
            on = active(i, j)
            return jnp.where(on, s, 0), jnp.where(on, j - starts[i], 0)
        return pl.BlockSpec((tt, tn), index)

    return _pc(body, name=name, grid=(k // tk, n_blocks * per, n_t), out_shape=_sds((n_blocks, k, n4), out_dtype),
               in_specs=[pl.BlockSpec((tt, tk), lambda a, j, s: (s, a))] + [y_spec(i) for i in range(n_y)],
               out_specs=pl.BlockSpec((1, tk, tn), lambda a, j, s: (j // per, a, j % per)),
               scratch_shapes=[pltpu.VMEM((tk, tn), F32)],
               compiler_params=_cp("parallel", "parallel", "arbitrary"))(x, *ys)


def _gcd(a, b):
    while b:
        a, b = b, a % b
    return a


def _w2(w):
    return (w[None], 0)


def _tok_spec(ts, d):
    return pl.BlockSpec((1, ts, d), lambda b, i: (b, i, 0))


def _seq_spec(d):
    return pl.BlockSpec((1, 1, d), lambda b, i: (b, 0, 0))


def _vec_spec(d):
    return pl.BlockSpec((1, d), lambda b, i: (0, 0))


def _ln_stats(z):
    mu = jnp.mean(z, axis=-1, keepdims=True)
    zc = z - mu
    var = jnp.mean(zc * zc, axis=-1, keepdims=True)
    rstd = lax.rsqrt(var + LN_EPS)
    return zc * rstd, rstd


def _modulate(name, x, sc, sh):
    b, s, d = x.shape
    ts = _div(s, 512, 16)

    def body(x_ref, sc_ref, sh_ref, u_ref):
        u_ref[0] = (x_ref[0] * (1.0 + sc_ref[0]) + sh_ref[0]).astype(BF16)

    return _pc(body, name=name, grid=(b, s // ts), out_shape=_sds(x.shape, BF16),
               in_specs=[_tok_spec(ts, d), _seq_spec(d), _seq_spec(d)], out_specs=_tok_spec(ts, d),
               compiler_params=_cp("parallel", "parallel"))(x, sc, sh)


def _ln_mod_fwd(name, alpha, x, y, g, lng, lnb, sc, sh):
    b, s, d = x.shape
    ts = _div(s, 512, 16)

    def body(x_ref, y_ref, g_ref, lng_ref, lnb_ref, sc_ref, sh_ref, z_ref, xn_ref, u_ref):
        z = alpha * x_ref[0] + (1.0 + g_ref[0]) * y_ref[0]
        xhat, _ = _ln_stats(z)
        xn = xhat * lng_ref[...] + lnb_ref[...]
        z_ref[0] = z
        xn_ref[0] = xn
        u_ref[0] = (xn * (1.0 + sc_ref[0]) + sh_ref[0]).astype(BF16)

    tok, seq, vec = _tok_spec(ts, d), _seq_spec(d), _vec_spec(d)
    return _pc(body, name=name, grid=(b, s // ts),
               out_shape=(_sds(x.shape, F32), _sds(x.shape, F32), _sds(x.shape, BF16)),
               in_specs=[tok, tok, seq, vec, vec, seq, seq], out_specs=(tok, tok, tok),
               compiler_params=_cp("parallel", "parallel"))(x, y, g, lng, lnb, sc, sh)


def _ln_loss_fwd(name, alpha, x, y, g, lng, lnb, target):
    b, s, d = x.shape
    ts = _div(s, 512, 16)

    def body(x_ref, y_ref, g_ref, lng_ref, lnb_ref, t_ref, z_ref, ct_ref, loss_ref):
        @pl.when((pl.program_id(0) == 0) & (pl.program_id(1) == 0))
        def _():
            loss_ref[...] = jnp.zeros_like(loss_ref)
        z = alpha * x_ref[0] + (1.0 + g_ref[0]) * y_ref[0]
        xhat, _ = _ln_stats(z)
        err = xhat * lng_ref[...] + lnb_ref[...] - t_ref[0]
        z_ref[0] = z
        ct_ref[0] = err / d
        part = 0.5 * jnp.sum(jnp.mean(err * err, axis=-1, keepdims=True))
        loss_ref[...] += jnp.full(loss_ref.shape, part, F32)

    tok, seq, vec = _tok_spec(ts, d), _seq_spec(d), _vec_spec(d)
    return _pc(body, name=name, grid=(b, s // ts),
               out_shape=(_sds(x.shape, F32), _sds(x.shape, F32), _sds((SUBLANE, LANE), F32)),
               in_specs=[tok, tok, seq, vec, vec, tok],
               out_specs=(tok, tok, pl.BlockSpec((SUBLANE, LANE), lambda b, i: (0, 0))),
               compiler_params=_cp("arbitrary", "arbitrary"))(x, y, g, lng, lnb, target)


def _sub_bwd(name, alpha, upstream, z, y, g, lng):
    b, s, d = z.shape
    ts = _div(s, 512, 16)
    last = len(upstream) == 1

    def body(*refs):
        if last:
            ct_ref, z_ref, y_ref, g_ref, lng_ref, dz_ref, dy_ref, dg_ref, dlng_ref, dlnb_ref = refs
        else:
            (dzn_ref, dun_ref, xn_ref, scn_ref, z_ref, y_ref, g_ref, lng_ref,
             dz_ref, dy_ref, dg_ref, dlng_ref, dlnb_ref, dsc_ref, dsh_ref) = refs
        first_tile = pl.program_id(1) == 0

        @pl.when(first_tile & (pl.program_id(0) == 0))
        def _():
            dlng_ref[...] = jnp.zeros_like(dlng_ref)
            dlnb_ref[...] = jnp.zeros_like(dlnb_ref)

        @pl.when(first_tile)
        def _():
            dg_ref[...] = jnp.zeros_like(dg_ref)
            if not last:
                dsc_ref[...] = jnp.zeros_like(dsc_ref)
                dsh_ref[...] = jnp.zeros_like(dsh_ref)

        if last:
            ct = ct_ref[0]
        else:
            dun = dun_ref[0]
            ct = alpha * dzn_ref[0] + dun * (1.0 + scn_ref[0])
            dsc_ref[0] += jnp.sum(dun * xn_ref[0], axis=0, keepdims=True)
            dsh_ref[0] += jnp.sum(dun, axis=0, keepdims=True)
        xhat, rstd = _ln_stats(z_ref[0])
        dlng_ref[...] += jnp.sum(ct * xhat, axis=0, keepdims=True)
        dlnb_ref[...] += jnp.sum(ct, axis=0, keepdims=True)
        dxhat = ct * lng_ref[...]
        dz = rstd * (dxhat - jnp.mean(dxhat, axis=-1, keepdims=True)
                     - xhat * jnp.mean(dxhat * xhat, axis=-1, keepdims=True))
        dz_ref[0] = dz
        dy_ref[0] = ((1.0 + g_ref[0]) * dz).astype(BF16)
        dg_ref[0] += jnp.sum(dz * y_ref[0], axis=0, keepdims=True)

    tok, seq, vec = _tok_spec(ts, d), _seq_spec(d), _vec_spec(d)
    seq_out = _sds((b, 1, d), F32)
    out_shape = [_sds(z.shape, F32), _sds(z.shape, BF16), seq_out, _sds((1, d), F32), _sds((1, d), F32)]
    out_specs = [tok, tok, seq, vec, vec]
    if last:
        in_specs = [tok, tok, tok, seq, vec]
    else:
        in_specs = [tok, tok, tok, seq, tok, tok, seq, vec]
        out_shape += [seq_out, seq_out]
        out_specs += [seq, seq]
    return _pc(body, name=name, grid=(b, s // ts), out_shape=tuple(out_shape), in_specs=in_specs,
               out_specs=tuple(out_specs), compiler_params=_cp("arbitrary", "arbitrary"))(*upstream, z, y, g, lng)


def _input_bwd(name, alpha, dz, du, x, sc):
    b, s, d = x.shape
    ts = _div(s, 512, 16)

    def body(dz_ref, du_ref, x_ref, sc_ref, gx_ref, dsc_ref, dsh_ref):
        @pl.when(pl.program_id(1) == 0)
        def _():
            dsc_ref[...] = jnp.zeros_like(dsc_ref)
            dsh_ref[...] = jnp.zeros_like(dsh_ref)
        du_ = du_ref[0]
        gx_ref[0] = alpha * dz_ref[0] + du_ * (1.0 + sc_ref[0])
        dsc_ref[0] += jnp.sum(du_ * x_ref[0], axis=0, keepdims=True)
        dsh_ref[0] += jnp.sum(du_, axis=0, keepdims=True)

    tok, seq = _tok_spec(ts, d), _seq_spec(d)
    seq_out = _sds((b, 1, d), F32)
    return _pc(body, name=name, grid=(b, s // ts), out_shape=(_sds(x.shape, F32), seq_out, seq_out),
               in_specs=[tok, tok, tok, seq], out_specs=(tok, seq, seq),
               compiler_params=_cp("parallel", "arbitrary"))(dz, du, x, sc)


def _rows_iota(shape):
    return lax.broadcasted_iota(jnp.int32, shape, 0)


def _back(v, k):
    return pltpu.roll(v, k, axis=0)


def _ahead(v, k):
    return pltpu.roll(v, v.shape[0] - k, axis=0)


def _conv3(ext, w_ref):
    return w_ref[2:3, :] * ext + w_ref[1:2, :] * _back(ext, 1) + w_ref[0:1, :] * _back(ext, 2)


def _conv3_t(dh_ext, w_ref):
    return w_ref[2:3, :] * dh_ext + w_ref[1:2, :] * _ahead(dh_ext, 1) + w_ref[0:1, :] * _ahead(dh_ext, 2)


def _flag(cond):
    return jnp.where(cond, 1.0, 0.0).astype(F32)


def _sigmoid(v):
    return 1.0 / (1.0 + jnp.exp(-v))


def _halo_specs(ts, tc, halo, n_s, col):
    per = ts // halo
    tile = pl.BlockSpec((1, ts, tc), lambda b, i, j: (b, i, col(j)))
    prev = pl.BlockSpec((1, halo, tc), lambda b, i, j: (b, jnp.maximum(i * per - 1, 0), col(j)))
    nxt = pl.BlockSpec((1, halo, tc), lambda b, i, j: (b, jnp.minimum((i + 1) * per, n_s * per - 1), col(j)))
    return tile, prev, nxt


def _convglu_fwd(name, p, cw, cb):
    b, s, f2 = p.shape
    f = f2 // 2
    ts, tc = _div(s, 512, CONV_HALO), _div(f, 256, LANE)
    n_s, n_c = s // ts, f // tc

    def body(pv_ref, pvh_ref, pg_ref, pgh_ref, wv_ref, wg_ref, bv_ref, bg_ref, a_ref):
        keep = _flag(pl.program_id(1) > 0)

        def conv(t_ref, h_ref, w_ref, b_ref):
            ext = jnp.concatenate([h_ref[0] * keep, t_ref[0]], axis=0)
            return _conv3(ext, w_ref)[CONV_HALO:] + b_ref[...]

        val = conv(pv_ref, pvh_ref, wv_ref, bv_ref)
        gate = conv(pg_ref, pgh_ref, wg_ref, bg_ref)
        a_ref[0] = (gate * _sigmoid(gate) * val).astype(BF16)

    tv, hv, _ = _halo_specs(ts, tc, CONV_HALO, n_s, lambda j: j)
    tg, hg, _ = _halo_specs(ts, tc, CONV_HALO, n_s, lambda j: j + n_c)
    wv = pl.BlockSpec((3, tc), lambda b, i, j: (0, j))
    wg = pl.BlockSpec((3, tc), lambda b, i, j: (0, j + n_c))
    bv = pl.BlockSpec((1, tc), lambda b, i, j: (0, j))
    bg = pl.BlockSpec((1, tc), lambda b, i, j: (0, j + n_c))
    return _pc(body, name=name, grid=(b, n_s, n_c), out_shape=_sds((b, s, f), BF16),
               in_specs=[tv, hv, tg, hg, wv, wg, bv, bg], out_specs=pl.BlockSpec((1, ts, tc), lambda b, i, j: (b, i, j)),
               compiler_params=_cp("parallel", "parallel", "parallel"))(p, p, p, p, cw, cw, cb, cb)


def _convglu_bwd(name, p, da, cw, cb):
    b, s, f2 = p.shape
    f = f2 // 2
    ts, tc = _div(s, 512, CONV_HALO), _div(f, 256, LANE)
    n_s, n_c = s // ts, f // tc

    def body(pv_ref, pvp_ref, pvn_ref, pg_ref, pgp_ref, pgn_ref, da_ref, dan_ref, wv_ref, wg_ref, bv_ref, bg_ref,
             dpv_ref, dpg_ref, dwv_ref, dwg_ref, dbv_ref, dbg_ref):
        bi, i = pl.program_id(1), pl.program_id(2)

        @pl.when((bi == 0) & (i == 0))
        def _():
            for r in (dwv_ref, dwg_ref, dbv_ref, dbg_ref):
                r[...] = jnp.zeros_like(r)

        keep_prev = _flag(i > 0)
        keep_next = _flag(i < n_s - 1)
        pv_ext = jnp.concatenate([pvp_ref[0] * keep_prev, pv_ref[0], pvn_ref[0]], axis=0)
        pg_ext = jnp.concatenate([pgp_ref[0] * keep_prev, pg_ref[0], pgn_ref[0]], axis=0)
        val = _conv3(pv_ext, wv_ref)[CONV_HALO:] + bv_ref[...]
        gate = _conv3(pg_ext, wg_ref)[CONV_HALO:] + bg_ref[...]
        da_ext = jnp.concatenate([da_ref[0], dan_ref[0] * keep_next], axis=0)
        sg = _sigmoid(gate)
        dval = da_ext * gate * sg
        dgate = da_ext * val * (sg * (1.0 + gate * (1.0 - sg)))
        dpv_ref[0] = _conv3_t(dval, wv_ref)[:ts].astype(BF16)
        dpg_ref[0] = _conv3_t(dgate, wg_ref)[:ts].astype(BF16)
        for dh, p_ext, dw_ref, db_ref in ((dval[:ts], pv_ext, dwv_ref, dbv_ref), (dgate[:ts], pg_ext, dwg_ref, dbg_ref)):
            db_ref[...] += jnp.sum(dh, axis=0, keepdims=True)
            for k in range(3):
                shifted = p_ext if k == 2 else _back(p_ext, 2 - k)
                dw_ref[k:k + 1, :] += jnp.sum(dh * shifted[CONV_HALO:CONV_HALO + ts], axis=0, keepdims=True)

    def specs(col):
        per = ts // CONV_HALO
        tile = pl.BlockSpec((1, ts, tc), lambda j, b, i: (b, i, col(j)))
        prev = pl.BlockSpec((1, CONV_HALO, tc), lambda j, b, i: (b, jnp.maximum(i * per - 1, 0), col(j)))
        nxt = pl.BlockSpec((1, CONV_HALO, tc), lambda j, b, i: (b, jnp.minimum((i + 1) * per, n_s * per - 1), col(j)))
        return tile, prev, nxt

    tv, pvp, pvn = specs(lambda j: j)
    tg, pgp, pgn = specs(lambda j: j + n_c)
    wv = pl.BlockSpec((3, tc), lambda j, b, i: (0, j))
    wg = pl.BlockSpec((3, tc), lambda j, b, i: (0, j + n_c))
    bv = pl.BlockSpec((1, tc), lambda j, b, i: (0, j))
    bg = pl.BlockSpec((1, tc), lambda j, b, i: (0, j + n_c))
    out_tile = pl.BlockSpec((1, ts, tc), lambda j, b, i: (b, i, j))
    acc3, acc1 = pl.BlockSpec((3, tc), lambda j, b, i: (0, j)), pl.BlockSpec((1, tc), lambda j, b, i: (0, j))
    dpv, dpg, dwv, dwg, dbv, dbg = _pc(
        body, name=name, grid=(n_c, b, n_s),
        out_shape=(_sds((b, s, f), BF16), _sds((b, s, f), BF16), _sds((3, f), F32), _sds((3, f), F32),
                   _sds((1, f), F32), _sds((1, f), F32)),
        in_specs=[tv, pvp, pvn, tg, pgp, pgn, tv, pvn, wv, wg, bv, bg],
        out_specs=(out_tile, out_tile, acc3, acc3, acc1, acc1),
        compiler_params=_cp("parallel", "arbitrary", "arbitrary"))(p, p, p, p, p, p, da, da, cw, cw, cb, cb)
    return dpv, dpg, jnp.concatenate([dwv, dwg], axis=1), jnp.concatenate([dbv, dbg], axis=1)


def _shortconv_fwd(name, q, cw):
    b, s, d3 = q.shape
    d = d3 // 3
    ts, tc = _div(s, 512, CONV_HALO), _div(d, 256, LANE)
    n_s, n_c = s // ts, d // tc

    def body(gb_ref, gc_ref, gch_ref, h_ref, hh_ref, w_ref, r_ref):
        keep = _flag(pl.program_id(1) > 0)
        m_ext = jnp.concatenate([gch_ref[0] * hh_ref[0] * keep, gc_ref[0] * h_ref[0]], axis=0)
        r_ref[0] = (gb_ref[0] * _conv3(m_ext, w_ref)[CONV_HALO:]).astype(BF16)

    tb, _, _ = _halo_specs(ts, tc, CONV_HALO, n_s, lambda j: j)
    tcc, hc, _ = _halo_specs(ts, tc, CONV_HALO, n_s, lambda j: j + n_c)
    th, hh, _ = _halo_specs(ts, tc, CONV_HALO, n_s, lambda j: j + 2 * n_c)
    w = pl.BlockSpec((3, tc), lambda b, i, j: (0, j))
    return _pc(body, name=name, grid=(b, n_s, n_c), out_shape=_sds((b, s, d), BF16),
               in_specs=[tb, tcc, hc, th, hh, w], out_specs=pl.BlockSpec((1, ts, tc), lambda b, i, j: (b, i, j)),
               compiler_params=_cp("parallel", "parallel", "parallel"))(q, q, q, q, q, cw)


def _shortconv_bwd(name, q, dr, cw):
    b, s, d3 = q.shape
    d = d3 // 3
    ts, tc = _div(s, 512, CONV_HALO), _div(d, 256, LANE)
    n_s, n_c = s // ts, d // tc

    def body(gb_ref, gbn_ref, gc_ref, gcp_ref, h_ref, hp_ref, dr_ref, drn_ref, w_ref,
             dgb_ref, dgc_ref, dh_ref, dw_ref):
        bi, i = pl.program_id(1), pl.program_id(2)

        @pl.when((bi == 0) & (i == 0))
        def _():
            dw_ref[...] = jnp.zeros_like(dw_ref)

        keep_prev = _flag(i > 0)
        keep_next = _flag(i < n_s - 1)
        gc, h = gc_ref[0], h_ref[0]
        m_ext = jnp.concatenate([gcp_ref[0] * hp_ref[0] * keep_prev, gc * h], axis=0)
        cm = _conv3(m_ext, w_ref)[CONV_HALO:]
        dr_ = dr_ref[0]
        dgb_ref[0] = (dr_ * cm).astype(BF16)
        dcv_ext = jnp.concatenate([dr_ * gb_ref[0], drn_ref[0] * gbn_ref[0] * keep_next], axis=0)
        dm = _conv3_t(dcv_ext, w_ref)[:ts]
        dgc_ref[0] = (dm * h).astype(BF16)
        dh_ref[0] = (dm * gc).astype(BF16)
        dcv = dcv_ext[:ts]
        for k in range(3):
            shifted = m_ext if k == 2 else _back(m_ext, 2 - k)
            dw_ref[k:k + 1, :] += jnp.sum(dcv * shifted[CONV_HALO:], axis=0, keepdims=True)

    def specs(col):
        per = ts // CONV_HALO
        tile = pl.BlockSpec((1, ts, tc), lambda j, b, i: (b, i, col(j)))
        prev = pl.BlockSpec((1, CONV_HALO, tc), lambda j, b, i: (b, jnp.maximum(i * per - 1, 0), col(j)))
        nxt = pl.BlockSpec((1, CONV_HALO, tc), lambda j, b, i: (b, jnp.minimum((i + 1) * per, n_s * per - 1), col(j)))
        return tile, prev, nxt

    tb, _, nb = specs(lambda j: j)
    tcc, pc_, _ = specs(lambda j: j + n_c)
    th, ph, _ = specs(lambda j: j + 2 * n_c)
    w = pl.BlockSpec((3, tc), lambda j, b, i: (0, j))
    out_tile = pl.BlockSpec((1, ts, tc), lambda j, b, i: (b, i, j))
    o = _sds((b, s, d), BF16)
    return _pc(body, name=name, grid=(n_c, b, n_s), out_shape=(o, o, o, _sds((3, d), F32)),
               in_specs=[tb, nb, tcc, pc_, th, ph, tb, nb, w], out_specs=(out_tile, out_tile, out_tile, w),
               compiler_params=_cp("parallel", "arbitrary", "arbitrary"))(q, q, q, q, q, q, dr, dr, cw)


def _pick_window(group, cands):
    gid = jnp.full(cands[0].shape, group, jnp.int32)
    out = cands[-1]
    for k in range(len(cands) - 2, -1, -1):
        out = jnp.where(gid == k, cands[k], out)
    return out


def _window_sums(v, shift):
    s1 = v + shift(v, 1)
    s2 = s1 + shift(s1, 2)
    s3 = s2 + shift(s2, 4)
    s4 = s3 + shift(s3, 8)
    return [s1, s2, s3, s4]


def _pool_counts(group, first_row, n_rows, cols):
    t = _rows_iota((n_rows, cols)) + first_row
    window = _pick_window(group, [jnp.full((n_rows, cols), 2 << k, jnp.int32) for k in range(POOL_GROUPS)])
    return jnp.minimum(t + 1, window).astype(F32)


def _pool_fwd(name, x, sc, sh, w, scale):
    b, s, d = x.shape
    tc = d // POOL_GROUPS
    ts = _div(s, 512, POOL_HALO)
    n_s = s // ts

    def body(x_ref, xp_ref, sc_ref, sh_ref, w_ref, scale_ref, y_ref):
        i, grp = pl.program_id(1), pl.program_id(2)
        keep = _flag(i > 0)
        mod = 1.0 + sc_ref[0]
        u = x_ref[0] * mod + sh_ref[0]
        u_ext = jnp.concatenate([(xp_ref[0] * mod + sh_ref[0]) * keep, u], axis=0)
        summed = _pick_window(grp, _window_sums(u_ext, _back))[POOL_HALO:]
        pooled = summed / _pool_counts(grp, i * ts, ts, tc) - u
        y_ref[0] = jnp.dot(pooled.astype(BF16), w_ref[0], preferred_element_type=F32) * scale_ref[...]

    tile, prev, _ = _halo_specs(ts, tc, POOL_HALO, n_s, lambda j: j)
    seq = pl.BlockSpec((1, 1, tc), lambda b, i, j: (b, 0, j))
    return _pc(body, name=name, grid=(b, n_s, POOL_GROUPS), out_shape=_sds(x.shape, F32),
               in_specs=[tile, prev, seq, seq, pl.BlockSpec((1, tc, tc), lambda b, i, j: (j, 0, 0)),
                         pl.BlockSpec((1, tc), lambda b, i, j: (0, j))],
               out_specs=pl.BlockSpec((1, ts, tc), lambda b, i, j: (b, i, j)),
               compiler_params=_cp("parallel", "parallel", "parallel"))(x, x, sc, sh, w, scale)


def _pool_bwd(name, x, sc, sh, dy, w, w_t, scale):
    b, s, d = x.shape
    tc = d // POOL_GROUPS
    ts = _div(s, 512, POOL_HALO)
    n_s = s // ts

    def body(x_ref, xp_ref, sc_ref, sh_ref, dy_ref, dyn_ref, w_ref, wt_ref, scale_ref, du_ref, dw_ref, dscale_ref):
        grp, bi, i = pl.program_id(0), pl.program_id(1), pl.program_id(2)

        @pl.when((bi == 0) & (i == 0))
        def _():
            dw_ref[...] = jnp.zeros_like(dw_ref)
            dscale_ref[...] = jnp.zeros_like(dscale_ref)

        keep_prev = _flag(i > 0)
        keep_next = _flag(i < n_s - 1)
        mod = 1.0 + sc_ref[0]
        u = x_ref[0] * mod + sh_ref[0]
        u_ext = jnp.concatenate([(xp_ref[0] * mod + sh_ref[0]) * keep_prev, u], axis=0)
        summed = _pick_window(grp, _window_sums(u_ext, _back))[POOL_HALO:]
        pooled = (summed / _pool_counts(grp, i * ts, ts, tc) - u).astype(BF16)
        dy_ = dy_ref[0].astype(F32)
        ymat = jnp.dot(pooled, w_ref[0], preferred_element_type=F32)
        dscale_ref[...] += jnp.sum(dy_ * ymat, axis=0, keepdims=True)
        dys_ext = (jnp.concatenate([dy_, dyn_ref[0].astype(F32) * keep_next], axis=0) * scale_ref[...]).astype(BF16)
        dw_ref[0] += lax.dot_general(pooled, dys_ext[:ts], (((0,), (0,)), ((), ())), preferred_element_type=F32)
        dpooled = jnp.dot(dys_ext, wt_ref[0], preferred_element_type=F32)
        e = dpooled / _pool_counts(grp, i * ts, ts + POOL_HALO, tc)
        du_ref[0] = _pick_window(grp, _window_sums(e, _ahead))[:ts] - dpooled[:ts]

    per = ts // POOL_HALO
    tile = pl.BlockSpec((1, ts, tc), lambda j, b, i: (b, i, j))
    prev = pl.BlockSpec((1, POOL_HALO, tc), lambda j, b, i: (b, jnp.maximum(i * per - 1, 0), j))
    nxt = pl.BlockSpec((1, POOL_HALO, tc), lambda j, b, i: (b, jnp.minimum((i + 1) * per, n_s * per - 1), j))
    seq = pl.BlockSpec((1, 1, tc), lambda j, b, i: (b, 0, j))
    wsp = pl.BlockSpec((1, tc, tc), lambda j, b, i: (j, 0, 0))
    vec = pl.BlockSpec((1, tc), lambda j, b, i: (0, j))
    return _pc(body, name=name, grid=(POOL_GROUPS, b, n_s),
               out_shape=(_sds(x.shape, F32), _sds((POOL_GROUPS, tc, tc), F32), _sds((1, d), F32)),
               in_specs=[tile, prev, seq, seq, tile, nxt, wsp, wsp, vec], out_specs=(tile, wsp, vec),
               compiler_params=_cp("parallel", "arbitrary", "arbitrary"))(x, x, sc, sh, dy, dy, w, w_t, scale)


def _rope_swap(v):
    lane = lax.broadcasted_iota(jnp.int32, v.shape, v.ndim - 1)
    lo, hi = QK_NOPE, QK_NOPE + QK_ROPE // 2
    from_above = pltpu.roll(v, HEAD_PAD - QK_ROPE // 2, axis=v.ndim - 1)
    from_below = pltpu.roll(v, QK_ROPE // 2, axis=v.ndim - 1)
    return jnp.where((lane >= lo) & (lane < hi), from_above,
                     jnp.where((lane >= hi) & (lane < hi + QK_ROPE // 2), from_below, 0.0))


def _rope(v, cos_t, sin_t):
    return v * cos_t + _rope_swap(v) * sin_t


def _rope_t(dv, cos_t, sin_t):
    return dv * cos_t + _rope_swap(dv * sin_t)


def _rms(v, g):
    r = lax.rsqrt(jnp.mean(v * v, axis=-1, keepdims=True) + RMS_EPS)
    return v * r, r


def _mla_norm_fwd(name, a, qn, kvn, cos_t, sin_t):
    b, s, wa = a.shape
    ql, kvl = qn.shape[1], kvn.shape[1]
    ts = _div(s, 512, 16)

    def body(aq_ref, akv_ref, ape_ref, qn_ref, kvn_ref, cos_ref, sin_ref, cq_ref, ckv_ref, kpe_ref):
        yq, _ = _rms(aq_ref[0], None)
        cq_ref[0] = (yq * qn_ref[...]).astype(BF16)
        ykv, _ = _rms(akv_ref[0], None)
        ckv_ref[0] = (ykv * kvn_ref[...]).astype(BF16)
        kpe_ref[0] = _rope(ape_ref[0], cos_ref[0], sin_ref[0])

    tok = lambda w, col: pl.BlockSpec((1, ts, w), lambda b, i: (b, i, col))
    return _pc(body, name=name, grid=(b, s // ts),
               out_shape=(_sds((b, s, ql), BF16), _sds((b, s, kvl), BF16), _sds((b, s, HEAD_PAD), F32)),
               in_specs=[tok(ql, 0), tok(kvl, ql // kvl), tok(HEAD_PAD, (ql + kvl) // HEAD_PAD), _vec_spec(ql),
                         _vec_spec(kvl), tok(HEAD_PAD, 0), tok(HEAD_PAD, 0)],
               out_specs=(tok(ql, 0), tok(kvl, 0), tok(HEAD_PAD, 0)),
               compiler_params=_cp("parallel", "parallel"))(a, a, a, qn, kvn, cos_t, sin_t)


def _mla_norm_bwd(name, a, dcq, dckv, dkpe, qn, kvn):
    b, s, wa = a.shape
    ql, kvl = qn.shape[1], kvn.shape[1]
    ts = _div(s, 512, 16)

    def body(a_ref, dcq_ref, dckv_ref, dkpe_ref, qn_ref, kvn_ref, da_ref, dqn_ref, dkvn_ref):
        @pl.when((pl.program_id(0) == 0) & (pl.program_id(1) == 0))
        def _():
            dqn_ref[...] = jnp.zeros_like(dqn_ref)
            dkvn_ref[...] = jnp.zeros_like(dkvn_ref)

        def one(v, dc, g_ref, dg_ref):
            yv, r = _rms(v, None)
            dg_ref[...] += jnp.sum(dc * yv, axis=0, keepdims=True)
            dyv = dc * g_ref[...]
            return r * (dyv - yv * jnp.mean(dyv * yv, axis=-1, keepdims=True))

        av = a_ref[0]
        da_ref[0, :, 0:ql] = one(av[:, 0:ql], dcq_ref[0], qn_ref, dqn_ref).astype(BF16)
        da_ref[0, :, ql:ql + kvl] = one(av[:, ql:ql + kvl], dckv_ref[0], kvn_ref, dkvn_ref).astype(BF16)
        da_ref[0, :, ql + kvl:] = dkpe_ref[0].astype(BF16)

    return _pc(body, name=name, grid=(b, s // ts),
               out_shape=(_sds(a.shape, BF16), _sds((1, ql), F32), _sds((1, kvl), F32)),
               in_specs=[_tok_spec(ts, wa), _tok_spec(ts, ql), _tok_spec(ts, kvl), _tok_spec(ts, HEAD_PAD),
                         _vec_spec(ql), _vec_spec(kvl)],
               out_specs=(_tok_spec(ts, wa), _vec_spec(ql), _vec_spec(kvl)),
               compiler_params=_cp("arbitrary", "arbitrary"))(a, dcq, dckv, dkpe, qn, kvn)


def _mla_prep_fwd(name, q_raw, kv_raw, kpe, cos_t, sin_t, n_heads):
    b, s, wq = q_raw.shape
    ts = _div(s, 256, 16)

    def body(q_ref, k_ref, v_ref, kpe_ref, cos_ref, sin_ref, qo_ref, ko_ref, vo_ref):
        cos_, sin_, kpe_ = cos_ref[0], sin_ref[0], kpe_ref[0]
        for h in range(n_heads):
            lanes = slice(h * HEAD_PAD, (h + 1) * HEAD_PAD)
            qo_ref[0, :, lanes] = _rope(q_ref[0, :, lanes], cos_, sin_).astype(BF16)
            ko_ref[0, :, lanes] = (k_ref[0, :, lanes] + kpe_).astype(BF16)
        vo_ref[0] = v_ref[0].astype(BF16)

    wide = lambda part: pl.BlockSpec((1, ts, wq), lambda b, i: (b, i, part))
    tok = pl.BlockSpec((1, ts, HEAD_PAD), lambda b, i: (b, i, 0))
    o = _sds(q_raw.shape, BF16)
    return _pc(body, name=name, grid=(b, s // ts), out_shape=(o, o, o),
               in_specs=[wide(0), wide(0), wide(1), tok, tok, tok], out_specs=(wide(0), wide(0), wide(0)),
               compiler_params=_cp("parallel", "parallel"))(q_raw, kv_raw, kv_raw, kpe, cos_t, sin_t)


def _mla_prep_bwd(name, dq, dk, dv, cos_t, sin_t, n_heads):
    b, s, wq = dq.shape
    ts = _div(s, 256, 16)

    def body(dq_ref, dk_ref, dv_ref, cos_ref, sin_ref, dqr_ref, dkv_ref, dkpe_ref):
        cos_, sin_ = cos_ref[0], sin_ref[0]
        dk_sum = None
        for h in range(n_heads):
            lanes = slice(h * HEAD_PAD, (h + 1) * HEAD_PAD)
            dqr_ref[0, :, lanes] = _rope_t(dq_ref[0, :, lanes], cos_, sin_).astype(BF16)
            dk_h = dk_ref[0, :, lanes]
            dkv_ref[0, :, lanes] = dk_h.astype(BF16)
            dk_sum = dk_h if dk_sum is None else dk_sum + dk_h
        dkv_ref[0, :, wq:2 * wq] = dv_ref[0]
        dkpe_ref[0] = _rope_t(dk_sum, cos_, sin_)

    wide = pl.BlockSpec((1, ts, wq), lambda b, i: (b, i, 0))
    both = pl.BlockSpec((1, ts, 2 * wq), lambda b, i: (b, i, 0))
    tok = pl.BlockSpec((1, ts, HEAD_PAD), lambda b, i: (b, i, 0))
    return _pc(body, name=name, grid=(b, s // ts),
               out_shape=(_sds(dq.shape, BF16), _sds((b, s, 2 * wq), BF16), _sds((b, s, HEAD_PAD), F32)),
               in_specs=[wide, wide, wide, tok, tok], out_specs=(wide, both, tok),
               compiler_params=_cp("parallel", "parallel"))(dq, dk, dv, cos_t, sin_t)


FLASH_TILE = 1024
LOG2_E = 1.4426950408889634


def _heads_per_step(n_heads):
    return 2 if n_heads % 2 == 0 else 1


def _causal_mask(i, j, tq, tk):
    rows = lax.broadcasted_iota(jnp.int32, (tq, tk), 0) + i * tq
    cols = lax.broadcasted_iota(jnp.int32, (tq, tk), 1) + j * tk
    return cols <= rows


def _nt(a, b):
    return lax.dot_general(a, b, (((1,), (1,)), ((), ())), preferred_element_type=F32)


def _tn(a, b):
    return lax.dot_general(a, b, (((0,), (0,)), ((), ())), preferred_element_type=F32)


def _flash_fwd(name, q, k, v, n_heads, sm_scale):
    b, s, _ = q.shape
    t, hp = _div(s, FLASH_TILE, LANE), _heads_per_step(n_heads)
    n, w = s // t, hp * HEAD_PAD
    neg = float(jnp.finfo(jnp.float32).min)
    c2 = sm_scale * LOG2_E

    def body(q_ref, k_ref, v_ref, o_ref, lse_ref, m_ref, l_ref, acc_ref):
        i, j = pl.program_id(2), pl.program_id(3)

        @pl.when(j == 0)
        def _():
            m_ref[...] = jnp.full(m_ref.shape, neg, F32)
            l_ref[...] = jnp.zeros_like(l_ref)
            acc_ref[...] = jnp.zeros_like(acc_ref)

        def block(on_diagonal):
            for hh in range(hp):
                ln = slice(hh * HEAD_PAD, (hh + 1) * HEAD_PAD)
                sc = _nt(q_ref[0, :, ln], k_ref[0, :, ln])
                if on_diagonal:
                    sc = jnp.where(_causal_mask(i, j, t, t), sc, neg)
                m_old = m_ref[hh]
                m_new = jnp.maximum(m_old, jnp.max(sc, axis=-1, keepdims=True))
                p = jnp.exp2((sc - m_new) * c2)
                corr = jnp.exp2((m_old - m_new) * c2)
                l_ref[hh] = corr * l_ref[hh] + jnp.sum(p, axis=-1, keepdims=True)
                acc_ref[:, ln] = corr * acc_ref[:, ln] + jnp.dot(p.astype(BF16), v_ref[0, :, ln],
                                                                 preferred_element_type=F32)
                m_ref[hh] = m_new

        pl.when(j < i)(functools.partial(block, False))
        pl.when(j == i)(functools.partial(block, True))

        @pl.when(j == n - 1)
        def _():
            for hh in range(hp):
                ln = slice(hh * HEAD_PAD, (hh + 1) * HEAD_PAD)
                o_ref[0, :, ln] = (acc_ref[:, ln] / l_ref[hh]).astype(BF16)
                lse_ref[0, :, ln] = jnp.broadcast_to(m_ref[hh] * sm_scale + jnp.log(l_ref[hh]), (t, HEAD_PAD))

    qs = pl.BlockSpec((1, t, w), lambda b, h, i, j: (b, i, h))
    ks = pl.BlockSpec((1, t, w), lambda b, h, i, j: (b, jnp.minimum(j, i), h))
    return _pc(body, name=name, grid=(b, n_heads // hp, n, n),
               out_shape=(_sds(q.shape, BF16), _sds(q.shape, F32)), in_specs=[qs, ks, ks], out_specs=(qs, qs),
               scratch_shapes=[pltpu.VMEM((hp, t, 1), F32), pltpu.VMEM((hp, t, 1), F32), pltpu.VMEM((t, w), F32)],
               compiler_params=_cp("parallel", "parallel", "parallel", "arbitrary"))(q, k, v)


def _flash_dq(name, q, k, v, o, lse, do, n_heads, sm_scale):
    b, s, _ = q.shape
    t, hp = _div(s, FLASH_TILE, LANE), _heads_per_step(n_heads)
    n, w = s // t, hp * HEAD_PAD
    c2 = sm_scale * LOG2_E

    def body(q_ref, k_ref, v_ref, o_ref, lse_ref, do_ref, dq_ref, acc_ref, delta_ref):
        i, j = pl.program_id(2), pl.program_id(3)

        @pl.when(j == 0)
        def _():
            acc_ref[...] = jnp.zeros_like(acc_ref)
            for hh in range(hp):
                ln = slice(hh * HEAD_PAD, (hh + 1) * HEAD_PAD)
                delta_ref[hh] = jnp.sum(do_ref[0, :, ln].astype(F32) * o_ref[0, :, ln].astype(F32), axis=-1,
                                        keepdims=True)

        def block(on_diagonal):
            for hh in range(hp):
                ln = slice(hh * HEAD_PAD, (hh + 1) * HEAD_PAD)
                sc = _nt(q_ref[0, :, ln], k_ref[0, :, ln])
                p = jnp.exp2(sc * c2 - lse_ref[0, :, hh * HEAD_PAD:hh * HEAD_PAD + 1] * LOG2_E)
                if on_diagonal:
                    p = jnp.where(_causal_mask(i, j, t, t), p, 0.0)
                dp = _nt(do_ref[0, :, ln], v_ref[0, :, ln])
                ds = p * (dp - delta_ref[hh])
                acc_ref[:, ln] += jnp.dot(ds.astype(BF16), k_ref[0, :, ln], preferred_element_type=F32)

        pl.when(j < i)(functools.partial(block, False))
        pl.when(j == i)(functools.partial(block, True))

        @pl.when(j == n - 1)
        def _():
            dq_ref[0] = acc_ref[...] * sm_scale

    qs = pl.BlockSpec((1, t, w), lambda b, h, i, j: (b, i, h))
    ks = pl.BlockSpec((1, t, w), lambda b, h, i, j: (b, jnp.minimum(j, i), h))
    return _pc(body, name=name, grid=(b, n_heads // hp, n, n), out_shape=_sds(q.shape, F32),
               in_specs=[qs, ks, ks, qs, qs, qs], out_specs=qs,
               scratch_shapes=[pltpu.VMEM((t, w), F32), pltpu.VMEM((hp, t, 1), F32)],
               compiler_params=_cp("parallel", "parallel", "parallel", "arbitrary"))(q, k, v, o, lse, do)


def _flash_dkv(name, q, k, v, o, lse, do, n_heads, sm_scale):
    b, s, _ = q.shape
    t, hp = _div(s, FLASH_TILE, LANE), _heads_per_step(n_heads)
    n, w = s // t, hp * HEAD_PAD
    c2 = sm_scale * LOG2_E

    def body(q_ref, k_ref, v_ref, o_ref, lse_ref, do_ref, dk_ref, dv_ref, dk_acc, dv_acc):
        j, i = pl.program_id(2), pl.program_id(3)

        @pl.when(i == 0)
        def _():
            dk_acc[...] = jnp.zeros_like(dk_acc)
            dv_acc[...] = jnp.zeros_like(dv_acc)

        def block(on_diagonal):
            for hh in range(hp):
                ln = slice(hh * HEAD_PAD, (hh + 1) * HEAD_PAD)
                do_ = do_ref[0, :, ln]
                delta = jnp.sum(do_.astype(F32) * o_ref[0, :, ln].astype(F32), axis=-1, keepdims=True)
                sc = _nt(q_ref[0, :, ln], k_ref[0, :, ln])
                p = jnp.exp2(sc * c2 - lse_ref[0, :, hh * HEAD_PAD:hh * HEAD_PAD + 1] * LOG2_E)
                if on_diagonal:
                    p = jnp.where(_causal_mask(i, j, t, t), p, 0.0)
                dv_acc[:, ln] += _tn(p.astype(BF16), do_)
                dp = _nt(do_, v_ref[0, :, ln])
                ds = p * (dp - delta)
                dk_acc[:, ln] += _tn(ds.astype(BF16), q_ref[0, :, ln])

        pl.when(i > j)(functools.partial(block, False))
        pl.when(i == j)(functools.partial(block, True))

        @pl.when(i == n - 1)
        def _():
            dk_ref[0] = dk_acc[...] * sm_scale
            dv_ref[0] = dv_acc[...].astype(BF16)

    qs = pl.BlockSpec((1, t, w), lambda b, h, j, i: (b, jnp.maximum(i, j), h))
    ks = pl.BlockSpec((1, t, w), lambda b, h, j, i: (b, j, h))
    return _pc(body, name=name, grid=(b, n_heads // hp, n, n), out_shape=(_sds(q.shape, F32), _sds(q.shape, BF16)),
               in_specs=[qs, ks, ks, qs, qs, qs], out_specs=(ks, ks),
               scratch_shapes=[pltpu.VMEM((t, w), F32), pltpu.VMEM((t, w), F32)],
               compiler_params=_cp("parallel", "parallel", "parallel", "arbitrary"))(q, k, v, o, lse, do)


def _mod_fwd(name, c_all, w, bias):
    depth, d, n = w.shape
    rows = c_all.shape[0]

    def body(c_ref, w_ref, b_ref, o_ref):
        cv = c_ref[...]
        cond = (cv * _sigmoid(cv)).astype(BF16)
        o_ref[0] = jnp.dot(cond, w_ref[0].astype(BF16), preferred_element_type=F32) + b_ref[0]

    return _pc(body, name=name, grid=(depth,), out_shape=_sds((depth, rows, n), F32),
               in_specs=[pl.BlockSpec((rows, d), lambda l: (0, 0)), pl.BlockSpec((1, d, n), lambda l: (l, 0, 0)),
                         pl.BlockSpec((1, 1, n), lambda l: (l, 0, 0))],
               out_specs=pl.BlockSpec((1, rows, n), lambda l: (l, 0, 0)), compiler_params=_cp("parallel"))(c_all, w, bias)


def _mod_bwd(name, c_all, dmod_cols, dmod_all):
    depth, rows, n = dmod_cols.shape
    d = c_all.shape[1]
    n_all = dmod_all.shape[2]
    tn = _div(n, 512, LANE)

    def body(c_ref, dm_ref, dma_ref, gw_ref, gb_ref):
        cv = c_ref[...]
        cond = (cv * _sigmoid(cv)).astype(BF16)
        gw_ref[0] = _tn(cond, dm_ref[0].astype(BF16))

        @pl.when(pl.program_id(1) == 0)
        def _():
            gb_ref[0] = jnp.sum(dma_ref[0], axis=0, keepdims=True)

    return _pc(body, name=name, grid=(depth, n // tn),
               out_shape=(_sds((depth, d, n), F32), _sds((depth, 1, n_all), F32)),
               in_specs=[pl.BlockSpec((rows, d), lambda l, j: (0, 0)), pl.BlockSpec((1, rows, tn), lambda l, j: (l, 0, j)),
                         pl.BlockSpec((1, rows, n_all), lambda l, j: (l, 0, 0))],
               out_specs=(pl.BlockSpec((1, d, tn), lambda l, j: (l, 0, j)), pl.BlockSpec((1, 1, n_all), lambda l, j: (l, 0, 0))),
               compiler_params=_cp("parallel", "arbitrary"))(c_all, dmod_cols, dmod_all)


def _adamw(name, w, g, m, v):
    shape = w.shape
    cols = shape[-1]
    rows = _size(shape) // cols
    tr = _div(rows, max(SUBLANE, (2 ** 19) // cols // SUBLANE * SUBLANE), SUBLANE)
    c1 = 1.0 - ADAM_B1 ** ADAM_STEP
    c2 = 1.0 - ADAM_B2 ** ADAM_STEP

    def body(w_ref, g_ref, m_ref, v_ref, d_ref, mo_ref, vo_ref):
        gv = g_ref[...]
        m_new = ADAM_B1 * m_ref[...] + (1.0 - ADAM_B1) * gv
        v_new = ADAM_B2 * v_ref[...] + (1.0 - ADAM_B2) * (gv * gv)
        m_hat = m_new / c1
        v_hat = v_new / c2
        d_ref[...] = -ADAM_LR * (m_hat / (jnp.sqrt(v_hat) + ADAM_EPS) + ADAM_WD * w_ref[...])
        mo_ref[...] = m_new
        vo_ref[...] = v_new

    spec = pl.BlockSpec((tr, cols), lambda i: (i, 0))
    o = _sds((rows, cols), F32)
    outs = _pc(body, name=name, grid=(rows // tr,), out_shape=(o, o, o), in_specs=[spec] * 4, out_specs=(spec,) * 3,
               compiler_params=_cp("parallel"))(*[a.reshape(rows, cols) for a in (w, g, m, v)])
    return tuple(a.reshape(shape) for a in outs)


def _rope_tables(positions):
    half = QK_ROPE // 2
    inv_freq = ROPE_THETA ** (-jnp.arange(0, QK_ROPE, 2, dtype=F32) / QK_ROPE)
    ang = positions.astype(F32)[..., None] * inv_freq
    cos, sin = jnp.cos(ang), jnp.sin(ang)
    lead = positions.shape
    ones = jnp.ones(lead + (QK_NOPE,), F32)
    tail_one = jnp.ones(lead + (HEAD_PAD - QK_NOPE - QK_ROPE,), F32)
    cos_t = jnp.concatenate([ones, cos, cos, tail_one], axis=-1)
    sin_t = jnp.concatenate([0 * ones, -sin, sin, 0 * tail_one], axis=-1)
    return cos_t, sin_t


def _pad_heads(w, n_heads, parts, axis):
    w = jnp.moveaxis(w, axis, -1)
    lead = w.shape[:-1]
    per = w.shape[-1] // n_heads
    w = w.reshape(lead + (n_heads, per))
    kept = jnp.concatenate([w[..., a:b_] for a, b_ in parts], axis=-1)
    pad = HEAD_PAD - kept.shape[-1]
    kept = jnp.concatenate([kept, jnp.zeros(lead + (n_heads, pad), w.dtype)], axis=-1)
    return jnp.moveaxis(kept.reshape(lead + (n_heads * HEAD_PAD,)), -1, axis)


def _unpad_heads(g, n_heads, width, axis):
    g = jnp.moveaxis(g, axis, -1)
    lead = g.shape[:-1]
    g = g.reshape(lead + (n_heads, HEAD_PAD))[..., :width]
    return g, lead


def kernel(x, c, positions, mod_w, mod_b, ln_g, ln_b, pool_w, pool_scale, mla_w_a, mla_q_norm, mla_w_uq, mla_kv_norm, mla_w_ukv, mla_w_o, sc_w_in, sc_conv, sc_w_out, ffn_w_up, ffn_conv, ffn_conv_b, ffn_w_down, loss_target, m_mod_w, m_mod_b, m_ln_g, m_ln_b, m_pool_w, m_pool_scale, m_mla_w_a, m_mla_q_norm, m_mla_w_uq, m_mla_kv_norm, m_mla_w_ukv, m_mla_w_o, m_sc_w_in, m_sc_conv, m_sc_w_out, m_ffn_w_up, m_ffn_conv, m_ffn_conv_b, m_ffn_w_down, v_mod_w, v_mod_b, v_ln_g, v_ln_b, v_pool_w, v_pool_scale, v_mla_w_a, v_mla_q_norm, v_mla_w_uq, v_mla_kv_norm, v_mla_w_ukv, v_mla_w_o, v_sc_w_in, v_sc_conv, v_sc_w_out, v_ffn_w_up, v_ffn_conv, v_ffn_conv_b, v_ffn_w_down):
    wts = dict(mod_w=mod_w, mod_b=mod_b, ln_g=ln_g, ln_b=ln_b, pool_w=pool_w, pool_scale=pool_scale, mla_w_a=mla_w_a,
               mla_q_norm=mla_q_norm, mla_w_uq=mla_w_uq, mla_kv_norm=mla_kv_norm, mla_w_ukv=mla_w_ukv, mla_w_o=mla_w_o,
               sc_w_in=sc_w_in, sc_conv=sc_conv, sc_w_out=sc_w_out, ffn_w_up=ffn_w_up, ffn_conv=ffn_conv,
               ffn_conv_b=ffn_conv_b, ffn_w_down=ffn_w_down)
    mom1 = dict(mod_w=m_mod_w, mod_b=m_mod_b, ln_g=m_ln_g, ln_b=m_ln_b, pool_w=m_pool_w, pool_scale=m_pool_scale,
                mla_w_a=m_mla_w_a, mla_q_norm=m_mla_q_norm, mla_w_uq=m_mla_w_uq, mla_kv_norm=m_mla_kv_norm,
                mla_w_ukv=m_mla_w_ukv, mla_w_o=m_mla_w_o, sc_w_in=m_sc_w_in, sc_conv=m_sc_conv, sc_w_out=m_sc_w_out,
                ffn_w_up=m_ffn_w_up, ffn_conv=m_ffn_conv, ffn_conv_b=m_ffn_conv_b, ffn_w_down=m_ffn_w_down)
    mom2 = dict(mod_w=v_mod_w, mod_b=v_mod_b, ln_g=v_ln_g, ln_b=v_ln_b, pool_w=v_pool_w, pool_scale=v_pool_scale,
                mla_w_a=v_mla_w_a, mla_q_norm=v_mla_q_norm, mla_w_uq=v_mla_w_uq, mla_kv_norm=v_mla_kv_norm,
                mla_w_ukv=v_mla_w_ukv, mla_w_o=v_mla_w_o, sc_w_in=v_sc_w_in, sc_conv=v_sc_conv, sc_w_out=v_sc_w_out,
                ffn_w_up=v_ffn_w_up, ffn_conv=v_ffn_conv, ffn_conv_b=v_ffn_conv_b, ffn_w_down=v_ffn_w_down)

    bsz, seq, d = x.shape
    depth = mod_b.shape[0]
    n_tok = bsz * seq
    n_heads = d // V_HEAD
    ql, kvl = mla_q_norm.shape[1], mla_kv_norm.shape[1]
    alpha = float((2 * depth) ** 0.25)
    sm_scale = float((QK_NOPE + QK_ROPE) ** -0.5)
    mx, my, mc = lax.axis_index("x"), lax.axis_index("y"), lax.axis_index("c")
    chip = 2 * mx + my
    dev = 2 * chip + mc

    small_names = list(SMALL_SHARDED)
    small_pack, small_spans = _pack_rows([c] + [wts[n] for n in small_names], F32, SUBLANE)
    rows_small = small_pack.shape[0]
    small_all = _all_gather8("gather_small_params", small_pack, True).reshape(N_DEV, rows_small * PACK_COLS)
    c_all = small_all[:, :c.size].reshape(N_DEV * bsz, d)
    per_chip = small_all[0::2]
    full = dict(wts)
    for n, (off, shape) in zip(small_names, small_spans[1:]):
        blocks = per_chip[:, off:off + _size(shape)].reshape((N_CHIP,) + tuple(shape))
        full[n] = _join_chips(blocks, SMALL_SHARDED[n])

    n_mod = mod_w.shape[2]
    bias_cols = lax.dynamic_slice_in_dim(mod_b, chip * n_mod, n_mod, axis=1)[:, None, :]
    mod_cols = _mod_fwd("mod_fwd", c_all, mod_w, bias_cols)
    half_rows = (N_DEV * bsz) // 2
    mod_half = lax.dynamic_slice_in_dim(mod_cols, mc * half_rows, half_rows, axis=1).reshape(depth * half_rows, n_mod)
    mod_all = _all_gather8("gather_mod", mod_half, True).reshape(N_CHIP, 2, depth, half_rows, n_mod)
    mod_all = jnp.transpose(mod_all, (2, 1, 3, 0, 4)).reshape(depth, N_DEV * bsz, N_CHIP * n_mod)
    mod_mine = lax.dynamic_slice_in_dim(mod_all, dev * bsz, bsz, axis=1)
    mods = [[mod_mine[l, :, k * d:(k + 1) * d][:, None, :] for k in range(6)] for l in range(depth)]

    big_names = list(BIG)
    gathered = _gather_weights("gather_weights", [wts[n].astype(BF16).reshape(-1, wts[n].shape[-1]) for n in big_names])
    by_chip = {n: g.reshape((N_CHIP,) + wts[n].shape) for n, g in zip(big_names, gathered)}
    f_hid = ffn_w_down.shape[1] * N_CHIP
    for n in ('pool_w', 'mla_w_a', 'mla_w_uq', 'mla_w_ukv', 'mla_w_o', 'sc_w_out'):
        full[n] = jnp.concatenate([by_chip[n][j] for j in range(N_CHIP)], axis=BIG[n])
    w_up_cols = by_chip['ffn_w_up'].reshape(N_CHIP, depth * d, -1)
    w_up_rows = jnp.transpose(by_chip['ffn_w_up'], (1, 0, 3, 2)).reshape(1, depth * 2 * f_hid, d)
    w_down_rows = jnp.transpose(by_chip['ffn_w_down'], (1, 0, 2, 3)).reshape(1, depth * f_hid, d)
    w_down_t = jnp.transpose(by_chip['ffn_w_down'], (1, 3, 0, 2)).reshape(1, depth * d, f_hid)
    n_sc = sc_w_in.shape[0]
    w_in_cols = by_chip['sc_w_in'].reshape(N_CHIP, n_sc * d, -1)
    w_in_rows = jnp.transpose(by_chip['sc_w_in'], (1, 0, 3, 2)).reshape(1, n_sc * 3 * d, d)

    nope_rope = [(0, QK_NOPE + QK_ROPE)]
    cos_t, sin_t = _rope_tables(positions)

    def t2(a):
        return a.reshape(n_tok, a.shape[-1])

    def t3(a):
        return a.reshape(bsz, seq, a.shape[-1])

    saved = []
    xin = x
    u = _modulate("modulate_in", x, mods[0][1], mods[0][0])
    loss_acc = None
    for l in range(depth):
        sh1, sc1, g1, sh2, sc2, g2 = mods[l]
        kind, j = l % 3, l // 3
        st = dict(x=xin)
        if kind == 0:
            w = full['pool_w'][j]
            st.update(w=w, w_t=jnp.swapaxes(w, 1, 2), scale=full['pool_scale'][j][None, :])
            y = _pool_fwd(f"pool_fwd_{l}", xin, sc1, sh1, st['w'], st['scale'])
        elif kind == 1:
            wa = full['mla_w_a'][j]
            zeros = jnp.zeros((d, QK_NOPE), BF16)
            w_a = jnp.concatenate([wa[:, :ql + kvl], zeros, wa[:, ql + kvl:], zeros[:, :HEAD_PAD - QK_NOPE - QK_ROPE]], axis=1)
            w_uq = _pad_heads(full['mla_w_uq'][j], n_heads, nope_rope, 1)
            w_kv = jnp.concatenate([_pad_heads(full['mla_w_ukv'][j], n_heads, [(0, QK_NOPE)], 1),
                                    _pad_heads(full['mla_w_ukv'][j], n_heads, [(QK_NOPE, QK_NOPE + V_HEAD)], 1)], axis=1)
            w_o = _pad_heads(full['mla_w_o'][j], n_heads, [(0, V_HEAD)], 0)
            qn, kvn = mla_q_norm[j][None, :], mla_kv_norm[j][None, :]
            a = t3(_mm_nn(f"mla_a_{l}", [(t2(u), _w2(w_a))], F32))
            cq, ckv, kpe = _mla_norm_fwd(f"mla_norm_fwd_{l}", a, qn, kvn, cos_t, sin_t)
            q_raw = t3(_mm_nn(f"mla_q_{l}", [(t2(cq), _w2(w_uq))], F32))
            kv_raw = t3(_mm_nn(f"mla_kv_{l}", [(t2(ckv), _w2(w_kv))], F32))
            qh, kh, vh = _mla_prep_fwd(f"mla_prep_fwd_{l}", q_raw, kv_raw, kpe, cos_t, sin_t, n_heads)
            o, lse = _flash_fwd(f"flash_fwd_{l}", qh, kh, vh, n_heads, sm_scale)
            y = t3(_mm_nn(f"mla_o_{l}", [(t2(o), _w2(w_o))], F32))
            st.update(u=u, w_a=w_a, w_uq=w_uq, w_kv=w_kv, w_o=w_o, qn=qn, kvn=kvn, a=a, cq=cq, ckv=ckv,
                      qh=qh, kh=kh, vh=vh, o=o, lse=lse)
        else:
            w_out, cw = full['sc_w_out'][j], full['sc_conv'][j]
            q = t3(_mm_nn(f"sc_in_{l}", [(t2(u), (w_in_cols, j))], F32))
            r = _shortconv_fwd(f"shortconv_fwd_{l}", q, cw)
            y = t3(_mm_nn(f"sc_out_{l}", [(t2(r), _w2(w_out))], F32))
            st.update(u=u, w_out=w_out, cw=cw, q=q, r=r)
        lng, lnb = full['ln_g'][l], full['ln_b'][l]
        z1, xmid, u2 = _ln_mod_fwd(f"ln_mod_a_{l}", alpha, xin, y, g1, lng[0:1], lnb[0:1], sc2, sh2)
        cwf, cbf = full['ffn_conv'][l], ffn_conv_b[l][None, :]
        p = t3(_mm_nn(f"ffn_up_{l}", [(t2(u2), (w_up_cols, l))], F32))
        act = _convglu_fwd(f"convglu_fwd_{l}", p, cwf, cbf)
        y2 = t3(_mm_nn(f"ffn_down_{l}", [(t2(act), (w_down_rows, l))], F32))
        st.update(y1=y, z1=z1, xmid=xmid, u2=u2, p=p, act=act, y2=y2, cwf=cwf, cbf=cbf, lng=lng, lnb=lnb)
        if l + 1 < depth:
            nsh1, nsc1 = mods[l + 1][0], mods[l + 1][1]
            z2, xin, u = _ln_mod_fwd(f"ln_mod_b_{l}", alpha, xmid, y2, g2, lng[1:2], lnb[1:2], nsc1, nsh1)
        else:
            z2, ct, loss_acc = _ln_loss_fwd("ln_loss", alpha, xmid, y2, g2, lng[1:2], lnb[1:2], loss_target)
        st.update(z2=z2)
        saved.append(st)
    loss = lax.psum(loss_acc[0, 0], ("x", "y", "c"))

    grads = {}
    dmods = [[None] * 6 for _ in range(depth)]
    g_ln_g = [[None, None] for _ in range(depth)]
    g_ln_b = [[None, None] for _ in range(depth)]
    stack = {n: [None] * wts[n].shape[0] for n in ('pool_scale', 'mla_q_norm', 'mla_kv_norm', 'sc_conv', 'ffn_conv',
                                                    'ffn_conv_b')}
    units = {n: [None] * wts[n].shape[0] for n in big_names}
    upstream = (ct,)
    for l in reversed(range(depth)):
        st = saved[l]
        sh1, sc1, g1, sh2, sc2, g2 = mods[l]
        kind, j = l % 3, l // 3
        res = _sub_bwd(f"sub_bwd_b_{l}", alpha, upstream, st['z2'], st['y2'], g2, st['lng'][1:2])
        dz2, dy2, dmods[l][5], g_ln_g[l][1], g_ln_b[l][1] = res[:5]
        if l + 1 < depth:
            dmods[l + 1][1], dmods[l + 1][0] = res[5], res[6]
        dy2f = t2(dy2)
        da = t3(_mm_nn(f"ffn_down_bwd_{l}", [(dy2f, (w_down_t, l))], F32))
        units['ffn_w_down'][l] = _mm_tn(f"ffn_down_dw_{l}", t2(st['act']), [dy2f],
                                        out_dtype=BF16).reshape(N_CHIP, f_hid // N_CHIP, d)
        dpv, dpg, dcw, dcb = _convglu_bwd(f"convglu_bwd_{l}", st['p'], da, st['cwf'], st['cbf'])
        stack['ffn_conv'][l], stack['ffn_conv_b'][l] = dcw, dcb[0]
        du2 = t3(_mm_nn(f"ffn_up_bwd_{l}", [(t2(dpv), (w_up_rows, 2 * l)), (t2(dpg), (w_up_rows, 2 * l + 1))], F32))
        units['ffn_w_up'][l] = _mm_tn(f"ffn_up_dw_{l}", t2(st['u2']), [t2(dpv), t2(dpg)], N_CHIP, out_dtype=BF16)
        res = _sub_bwd(f"sub_bwd_a_{l}", alpha, (dz2, du2, st['xmid'], sc2), st['z1'], st['y1'], g1, st['lng'][0:1])
        dz1, dy1, dmods[l][2], g_ln_g[l][0], g_ln_b[l][0], dmods[l][4], dmods[l][3] = res
        dy1f = t2(dy1)
        if kind == 0:
            du1, dw, dscale = _pool_bwd(f"pool_bwd_{l}", st['x'], sc1, sh1, dy1, st['w'], st['w_t'], st['scale'])
            stack['pool_scale'][j] = dscale[0]
            grp = dw.shape[1] // N_CHIP
            units['pool_w'][j] = jnp.transpose(dw.reshape(POOL_GROUPS, N_CHIP, grp, dw.shape[2]),
                                               (1, 0, 2, 3)).reshape(N_CHIP, POOL_GROUPS * grp, dw.shape[2])
        elif kind == 1:
            do = t3(_mm_nn(f"mla_o_bwd_{l}", [(dy1f, _w2(jnp.swapaxes(st['w_o'], 0, 1)))], BF16))
            gwo, _ = _unpad_heads(_mm_tn(f"mla_o_dw_{l}", t2(st['o']), [dy1f])[0], n_heads, V_HEAD, 0)
            units['mla_w_o'][j] = jnp.moveaxis(gwo.reshape(d, n_heads * V_HEAD), -1, 0).reshape(N_CHIP, -1, d)
            fa = (st['qh'], st['kh'], st['vh'], st['o'], st['lse'], do, n_heads, sm_scale)
            dq = _flash_dq(f"flash_dq_{l}", *fa)
            dk, dv = _flash_dkv(f"flash_dkv_{l}", *fa)
            dq_raw, dkv_raw, dkpe = _mla_prep_bwd(f"mla_prep_bwd_{l}", dq, dk, dv, cos_t, sin_t, n_heads)
            dq_raw, dkv_raw = t2(dq_raw), t2(dkv_raw)
            dcq = t3(_mm_nn(f"mla_q_bwd_{l}", [(dq_raw, _w2(jnp.swapaxes(st['w_uq'], 0, 1)))], F32))
            dckv = t3(_mm_nn(f"mla_kv_bwd_{l}", [(dkv_raw, _w2(jnp.swapaxes(st['w_kv'], 0, 1)))], F32))
            gq, _ = _unpad_heads(_mm_tn(f"mla_q_dw_{l}", t2(st['cq']), [dq_raw])[0], n_heads, QK_NOPE + QK_ROPE, 1)
            units['mla_w_uq'][j] = _cols_by_chip(gq.reshape(ql, n_heads * (QK_NOPE + QK_ROPE)))
            gkv = _mm_tn(f"mla_kv_dw_{l}", t2(st['ckv']), [dkv_raw])[0]
            gk, _ = _unpad_heads(gkv[:, :n_heads * HEAD_PAD], n_heads, QK_NOPE, 1)
            gv, _ = _unpad_heads(gkv[:, n_heads * HEAD_PAD:], n_heads, V_HEAD, 1)
            units['mla_w_ukv'][j] = _cols_by_chip(
                jnp.concatenate([gk, gv], axis=-1).reshape(kvl, n_heads * (QK_NOPE + V_HEAD)))
            da_, dqn, dkvn = _mla_norm_bwd(f"mla_norm_bwd_{l}", st['a'], dcq, dckv, dkpe, st['qn'], st['kvn'])
            stack['mla_q_norm'][j], stack['mla_kv_norm'][j] = dqn[0], dkvn[0]
            du1 = t3(_mm_nn(f"mla_a_bwd_{l}", [(t2(da_), _w2(jnp.swapaxes(st['w_a'], 0, 1)))], F32))
            gwa = _mm_tn(f"mla_a_dw_{l}", t2(st['u']), [t2(da_)])[0]
            units['mla_w_a'][j] = _cols_by_chip(jnp.concatenate(
                [gwa[:, :ql + kvl], gwa[:, ql + kvl + QK_NOPE:ql + kvl + QK_NOPE + QK_ROPE]], axis=1))
        else:
            dr = t3(_mm_nn(f"sc_out_bwd_{l}", [(dy1f, _w2(jnp.swapaxes(st['w_out'], 0, 1)))], F32))
            units['sc_w_out'][j] = _mm_tn(f"sc_out_dw_{l}", t2(st['r']), [dy1f], out_dtype=BF16).reshape(N_CHIP, -1, d)
            dgb, dgc, dh, dcw = _shortconv_bwd(f"shortconv_bwd_{l}", st['q'], dr, st['cw'])
            stack['sc_conv'][j] = dcw
            parts = [t2(dgb), t2(dgc), t2(dh)]
            du1 = t3(_mm_nn(f"sc_in_bwd_{l}", [(parts[k], (w_in_rows, 3 * j + k)) for k in range(3)], F32))
            units['sc_w_in'][j] = _mm_tn(f"sc_in_dw_{l}", t2(st['u']), parts, N_CHIP, out_dtype=BF16)
        upstream = (dz1, du1, st['x'], sc1)
    grad_x, dmods[0][1], dmods[0][0] = _input_bwd("input_bwd", alpha, upstream[0], upstream[1], x, mods[0][1])

    for n, parts in stack.items():
        grads[n] = jnp.stack(parts)
    grads['ln_g'] = jnp.stack([jnp.concatenate(r, axis=0) for r in g_ln_g])
    grads['ln_b'] = jnp.stack([jnp.concatenate(r, axis=0) for r in g_ln_b])
    dmod_mine = jnp.stack([jnp.concatenate([t[:, 0, :] for t in dmods[l]], axis=-1) for l in range(depth)])

    small_grad_names = small_names + ['mla_q_norm', 'mla_kv_norm', 'ffn_conv_b']
    sg_pack, sg_spans = _pack_rows([dmod_mine] + [grads[n] for n in small_grad_names], F32, SUBLANE)
    rows_sg = sg_pack.shape[0]
    sg_all = _all_gather8("gather_small_grads", sg_pack, True).reshape(N_DEV, rows_sg, PACK_COLS)
    dmod_all = sg_all.reshape(N_DEV, -1)[:, :dmod_mine.size].reshape(N_DEV, depth, bsz, 6 * d)
    dmod_all = jnp.transpose(dmod_all, (1, 0, 2, 3)).reshape(depth, N_DEV * bsz, 6 * d)
    sg_sum = _sum8("sum_small_grads", sg_all).reshape(-1)
    for n, (off, shape) in zip(small_grad_names, sg_spans[1:]):
        g_full = sg_sum[off:off + _size(shape)].reshape(shape)
        if n in SMALL_SHARDED:
            ax = SMALL_SHARDED[n]
            width = shape[ax] // N_CHIP
            g_full = lax.dynamic_slice_in_dim(g_full, chip * width, width, axis=ax)
        grads[n] = g_full
    dmod_cols = lax.dynamic_slice_in_dim(dmod_all, chip * n_mod, n_mod, axis=2)
    grads['mod_w'], gb = _mod_bwd("mod_bwd", c_all, dmod_cols, dmod_all)
    grads['mod_b'] = gb[:, 0, :]

    unit_list, layers_of = [], []
    for n in big_names:
        layers_of.append(list(range(len(unit_list), len(unit_list) + len(units[n]))))
        unit_list += units[n]
    received = _scatter_grads("scatter_big_grads", [u.astype(BF16) for u in unit_list])
    core = mc.astype(jnp.int32).reshape(1)
    bufs = _swap_halves("swap_big_grad_halves",
                        [_sum8_into_half(f"sum_big_grads_{u}", r, core) for u, r in enumerate(received)])
    for n, us in zip(big_names, layers_of):
        grads[n] = jnp.stack([bufs[u] for u in us]).reshape(wts[n].shape)

    deltas, new_m, new_v = {}, {}, {}
    for n in WEIGHTS:
        deltas[n], new_m[n], new_v[n] = _adamw(f"adamw_{n}", wts[n], grads[n], mom1[n], mom2[n])
    return (loss, grad_x, *[grads[n] for n in WEIGHTS], *[deltas[n] for n in WEIGHTS],
            *[new_m[n] for n in WEIGHTS], *[new_v[n] for n in WEIGHTS])
```

```python
import functools

import jax
import jax.numpy as jnp
from jax import lax
from jax.experimental import pallas as pl
from jax.experimental.pallas import tpu as pltpu

F32 = jnp.float32
BF16 = jnp.bfloat16
MESH = pl.DeviceIdType.MESH

N_DEV = 8
N_CHIP = 4
LANE = 128
SUBLANE = 8
VMEM_LIMIT_BYTES = 56 * 2 ** 20
PACK_COLS = 1024

LN_EPS = 1e-5
RMS_EPS = 1e-6
QK_NOPE, QK_ROPE, V_HEAD = 64, 32, 64
ROPE_THETA = 10000.0
HEAD_PAD = 128
POOL_GROUPS = 4
POOL_HALO = 16
CONV_HALO = 8
ADAM_LR, ADAM_B1, ADAM_B2, ADAM_EPS, ADAM_WD, ADAM_STEP = 0.001, 0.9, 0.999, 1e-08, 0.01, 10

WEIGHTS = ['mod_w', 'mod_b', 'ln_g', 'ln_b', 'pool_w', 'pool_scale', 'mla_w_a', 'mla_q_norm', 'mla_w_uq',
           'mla_kv_norm', 'mla_w_ukv', 'mla_w_o', 'sc_w_in', 'sc_conv', 'sc_w_out', 'ffn_w_up', 'ffn_conv',
           'ffn_conv_b', 'ffn_w_down']
BIG = {'pool_w': 2, 'mla_w_a': 2, 'mla_w_uq': 2, 'mla_w_ukv': 2, 'mla_w_o': 1, 'sc_w_in': 2, 'sc_w_out': 1,
       'ffn_w_up': 2, 'ffn_w_down': 1}
SMALL_SHARDED = {'ln_g': 2, 'ln_b': 2, 'pool_scale': 1, 'sc_conv': 2, 'ffn_conv': 2}
REPLICATED = ['mod_b', 'mla_q_norm', 'mla_kv_norm', 'ffn_conv_b']


def _pc(body, **kw):
    return pl.pallas_call(body, **kw)


def _cp(*sem):
    return pltpu.CompilerParams(dimension_semantics=sem, vmem_limit_bytes=VMEM_LIMIT_BYTES)


def _div(n, cap, mult):
    best = None
    for d in range(mult, min(n, cap) + 1, mult):
        if n % d == 0:
            best = d
    return best if best is not None else n


def _sds(shape, dtype):
    return jax.ShapeDtypeStruct(tuple(shape), dtype)


def _flip(v, bit):
    return 1 - v if bit else v


def _all_gather8(name, x_shard, in_vmem):
    m_per, n = x_shard.shape
    space = pltpu.VMEM if in_vmem else pltpu.HBM

    def body(x_ref, out_ref, send_sems, recv_sems, local_sem):
        x, y, c = lax.axis_index("x"), lax.axis_index("y"), lax.axis_index("c")
        me, sibling = (x, y, c), (x, y, 1 - c)
        chips = [(1 - x, y), (x, 1 - y), (1 - x, 1 - y)]

        def rows(px, py, pc_):
            return out_ref.at[pl.ds((4 * px + 2 * py + pc_) * m_per, m_per), :]

        def copy(k, block, to, src=None):
            return pltpu.make_async_remote_copy(
                src_ref=rows(*block) if src is None else src, dst_ref=rows(*block),
                send_sem=send_sems.at[k], recv_sem=recv_sems.at[k], device_id=to, device_id_type=MESH)

        mine = pltpu.make_async_copy(x_ref, rows(*me), local_sem)
        mine.start()
        first = [copy(0, me, sibling, src=x_ref)]
        first += [copy(1 + j, me, (*chip, c), src=x_ref) for j, chip in enumerate(chips)]
        for cp in first:
            cp.start()
        passed = [copy(4 + j, (*chip, c), sibling) for j, chip in enumerate(chips)]
        for j, chip in enumerate(chips):
            copy(1 + j, (*chip, c), me).wait_recv()
            passed[j].start()
        copy(0, sibling, me).wait_recv()
        for j, chip in enumerate(chips):
            copy(4 + j, (*chip, 1 - c), me).wait_recv()
        for cp in first + passed:
            cp.wait_send()
        mine.wait()

    return _pc(
        body, name=name, out_shape=_sds((N_DEV * m_per, n), x_shard.dtype),
        in_specs=[pl.BlockSpec(memory_space=space)], out_specs=pl.BlockSpec(memory_space=space),
        scratch_shapes=[pltpu.SemaphoreType.DMA((7,)), pltpu.SemaphoreType.DMA((7,)), pltpu.SemaphoreType.DMA],
        compiler_params=pltpu.CompilerParams(vmem_limit_bytes=VMEM_LIMIT_BYTES),
    )(x_shard)


def _gather_weights(name, shards):
    n_t = len(shards)
    halves = [s.shape[0] // 2 for s in shards]

    def body(*refs):
        x_refs, o_refs = refs[:n_t], refs[n_t:2 * n_t]
        send_sems, recv_sems, local_sems = refs[2 * n_t:]
        x, y, c = lax.axis_index("x"), lax.axis_index("y"), lax.axis_index("c")
        me, sibling = (x, y, c), (x, y, 1 - c)
        chips = [(1 - x, y), (x, 1 - y), (1 - x, 1 - y)]

        def slot(t, px, py, pc_):
            return o_refs[t].at[4 * px + 2 * py + pc_]

        def my_rows(t):
            return x_refs[t].at[pl.ds(c * halves[t], halves[t]), :]

        def copy(t, k, block, to, src=None):
            return pltpu.make_async_remote_copy(
                src_ref=slot(t, *block) if src is None else src, dst_ref=slot(t, *block),
                send_sem=send_sems.at[t, k], recv_sem=recv_sems.at[t, k], device_id=to, device_id_type=MESH)

        local = [pltpu.make_async_copy(my_rows(t), slot(t, *me), local_sems.at[t]) for t in range(n_t)]
        for cp in local:
            cp.start()
        first = []
        for t in range(n_t):
            first += [copy(t, 1 + j, me, (*chip, c), src=my_rows(t)) for j, chip in enumerate(chips)]
            first.append(copy(t, 0, me, sibling, src=my_rows(t)))
        for cp in first:
            cp.start()
        passed = []
        for j, chip in enumerate(chips):
            for t in range(n_t):
                copy(t, 1 + j, (*chip, c), me).wait_recv()
                passed.append(copy(t, 4 + j, (*chip, c), sibling))
                passed[-1].start()
        for t in range(n_t):
            copy(t, 0, sibling, me).wait_recv()
        for j, chip in enumerate(chips):
            for t in range(n_t):
                copy(t, 4 + j, (*chip, 1 - c), me).wait_recv()
        for cp in first + passed:
            cp.wait_send()
        for cp in local:
            cp.wait()

    hbm = pl.BlockSpec(memory_space=pltpu.HBM)
    return _pc(
        body, name=name, out_shape=tuple(_sds((N_DEV, h, s.shape[1]), s.dtype) for h, s in zip(halves, shards)),
        in_specs=[hbm] * n_t, out_specs=(hbm,) * n_t,
        scratch_shapes=[pltpu.SemaphoreType.DMA((n_t, 7)), pltpu.SemaphoreType.DMA((n_t, 7)),
                        pltpu.SemaphoreType.DMA((n_t,))],
    )(*shards)


def _scatter_copies(u_refs, r_refs, send_sems, recv_sems):
    x, y, c = lax.axis_index("x"), lax.axis_index("y"), lax.axis_index("c")
    copies = []
    for k in range(1, N_DEV):
        px, py, pcc = _flip(x, (k >> 2) & 1), _flip(y, (k >> 1) & 1), _flip(c, k & 1)
        for t, (u_ref, r_ref) in enumerate(zip(u_refs, r_refs)):
            h = u_ref.shape[1] // 2
            copies.append(pltpu.make_async_remote_copy(
                src_ref=u_ref.at[2 * px + py, pl.ds(pcc * h, h), :], dst_ref=r_ref.at[k - 1],
                send_sem=send_sems.at[t, k - 1], recv_sem=recv_sems.at[t, k - 1],
                device_id=(px, py, pcc), device_id_type=MESH))
    return copies


def _scatter_shapes(units):
    return tuple(_sds((N_DEV - 1, u.shape[1] // 2, u.shape[2]), u.dtype) for u in units)


def _scatter_grads(name, units):
    n_u = len(units)

    def body(*refs):
        copies = _scatter_copies(refs[:n_u], refs[n_u:2 * n_u], refs[2 * n_u], refs[2 * n_u + 1])
        for cp in copies:
            cp.start()
        for cp in copies:
            cp.wait()

    hbm = pl.BlockSpec(memory_space=pltpu.HBM)
    return _pc(body, name=name, out_shape=_scatter_shapes(units), in_specs=[hbm] * n_u, out_specs=(hbm,) * n_u,
               scratch_shapes=[pltpu.SemaphoreType.DMA((n_u, 7)), pltpu.SemaphoreType.DMA((n_u, 7))])(*units)


def _pc_cargo(body, cargo, *, name, grid, in_specs, out_specs, out_shape, scratch_shapes=()):
    out_specs, out_shape = tuple(out_specs), tuple(out_shape)
    if not cargo:
        return lambda *args: (_pc(body, name=name, grid=grid, in_specs=list(in_specs), out_specs=out_specs,
                                  out_shape=out_shape, scratch_shapes=list(scratch_shapes),
                                  compiler_params=_cp(*["arbitrary"] * len(grid)))(*args), ())
    n_in, n_out, n_u, n_s = len(in_specs), len(out_specs), len(cargo), len(scratch_shapes)

    def wrapped(*refs):
        ins, u_refs = refs[:n_in], refs[n_in:n_in + n_u]
        outs = refs[n_in + n_u:n_in + n_u + n_out]
        r_refs = refs[n_in + n_u + n_out:n_in + 2 * n_u + n_out]
        scratch = refs[n_in + 2 * n_u + n_out:n_in + 2 * n_u + n_out + n_s]
        send_sems, recv_sems = refs[-2:]
        first = last = None
        for axis, extent in enumerate(grid):
            at_start, at_end = pl.program_id(axis) == 0, pl.program_id(axis) == extent - 1
            first = at_start if first is None else first & at_start
            last = at_end if last is None else last & at_end

        @pl.when(first)
        def _():
            for cp in _scatter_copies(u_refs, r_refs, send_sems, recv_sems):
                cp.start()

        body(*ins, *outs, *scratch)

        @pl.when(last)
        def _():
            for cp in _scatter_copies(u_refs, r_refs, send_sems, recv_sems):
                cp.wait()

    hbm = pl.BlockSpec(memory_space=pltpu.HBM)
    call = _pc(wrapped, name=name, grid=grid, in_specs=list(in_specs) + [hbm] * n_u, out_specs=out_specs + (hbm,) * n_u,
               out_shape=out_shape + _scatter_shapes(cargo),
               scratch_shapes=list(scratch_shapes) + [pltpu.SemaphoreType.DMA((n_u, 7)), pltpu.SemaphoreType.DMA((n_u, 7))],
               compiler_params=_cp(*["arbitrary"] * len(grid)))

    def run(*args):
        res = call(*args, *cargo)
        return tuple(res[:n_out]), tuple(res[n_out:])
    return run


def _swap_halves(name, bufs):
    n_u = len(bufs)

    def body(*refs):
        o_refs = refs[n_u:2 * n_u]
        send_sems, recv_sems = refs[2 * n_u:]
        x, y, c = lax.axis_index("x"), lax.axis_index("y"), lax.axis_index("c")

        def rows(u, core):
            h = bufs[u].shape[0] // 2
            return o_refs[u].at[pl.ds(core * h, h), :]

        sends = [pltpu.make_async_remote_copy(src_ref=rows(u, c), dst_ref=rows(u, c), send_sem=send_sems.at[u],
                                              recv_sem=recv_sems.at[u], device_id=(x, y, 1 - c), device_id_type=MESH)
                 for u in range(n_u)]
        recvs = [pltpu.make_async_remote_copy(src_ref=rows(u, c), dst_ref=rows(u, 1 - c), send_sem=send_sems.at[u],
                                              recv_sem=recv_sems.at[u], device_id=(x, y, 1 - c), device_id_type=MESH)
                 for u in range(n_u)]
        for cp in sends:
            cp.start()
        for cp in recvs:
            cp.wait_recv()
        for cp in sends:
            cp.wait_send()

    hbm = pl.BlockSpec(memory_space=pltpu.HBM)
    return _pc(
        body, name=name, out_shape=tuple(_sds(b.shape, b.dtype) for b in bufs), in_specs=[hbm] * n_u,
        out_specs=(hbm,) * n_u, input_output_aliases={u: u for u in range(n_u)},
        scratch_shapes=[pltpu.SemaphoreType.DMA((n_u,)), pltpu.SemaphoreType.DMA((n_u,))],
    )(*bufs)


def _sum8_into_half(name, unit, received, chip_core):
    _, h, n = received.shape
    tm = _div(h, 256, 16)
    per = h // tm

    def body(cc_ref, u_ref, p_ref, o_ref):
        acc = u_ref[0].astype(F32)
        for s in range(N_DEV - 1):
            acc = acc + p_ref[s].astype(F32)
        o_ref[...] = acc

    grid_spec = pltpu.PrefetchScalarGridSpec(
        num_scalar_prefetch=1, grid=(per,),
        in_specs=[pl.BlockSpec((1, tm, n), lambda i, cc_ref: (cc_ref[0], cc_ref[1] * per + i, 0)),
                  pl.BlockSpec((N_DEV - 1, tm, n), lambda i, cc_ref: (0, i, 0))],
        out_specs=pl.BlockSpec((tm, n), lambda i, cc_ref: (cc_ref[1] * per + i, 0)))
    return _pc(body, name=name, grid_spec=grid_spec, out_shape=_sds((2 * h, n), F32),
               compiler_params=_cp("arbitrary"))(chip_core, unit, received)


def _sum8(name, parts):
    _, m, n = parts.shape
    tm = _div(m, 256, SUBLANE)

    def body(p_ref, o_ref):
        acc = p_ref[0]
        for s in range(1, N_DEV):
            acc = acc + p_ref[s]
        o_ref[...] = acc

    return _pc(body, name=name, grid=(m // tm,), out_shape=_sds((m, n), F32),
               in_specs=[pl.BlockSpec((N_DEV, tm, n), lambda i: (0, i, 0))],
               out_specs=pl.BlockSpec((tm, n), lambda i: (i, 0)), compiler_params=_cp("parallel"))(parts)


def _pack_rows(arrays, dtype, row_mult):
    flat, spans, off = [], [], 0
    for a in arrays:
        flat.append(a.reshape(-1).astype(dtype))
        spans.append((off, a.shape))
        off += a.size
    quantum = row_mult * PACK_COLS
    total = -(-off // quantum) * quantum
    if total > off:
        flat.append(jnp.zeros((total - off,), dtype))
    return jnp.concatenate(flat).reshape(total // PACK_COLS, PACK_COLS), spans


def _size(shape):
    n = 1
    for s in shape:
        n *= s
    return n


def _join_chips(blocks, axis):
    return jnp.concatenate([blocks[j] for j in range(N_CHIP)], axis=axis)


def _cols_by_chip(g):
    k, n = g.shape
    return jnp.transpose(g.reshape(k, N_CHIP, n // N_CHIP), (1, 0, 2))


def _mm_nn(name, pairs, out_dtype, tm_cap=1024, tn_cap=1536):
    m = pairs[0][0].shape[0]
    nb, _, n4 = pairs[0][1][0].shape
    tm, tn = _div(m, tm_cap, 16), _div(n4, tn_cap, LANE)
    per = n4 // tn
    n_pairs = len(pairs)

    def body(*refs):
        o_ref = refs[-1]
        acc = jnp.dot(refs[0][...], refs[1][0], preferred_element_type=F32)
        for i in range(1, n_pairs):
            acc = acc + jnp.dot(refs[2 * i][...], refs[2 * i + 1][0], preferred_element_type=F32)
        o_ref[...] = acc.astype(o_ref.dtype)

    in_specs, args = [], []
    for a, (w, r) in pairs:
        k = a.shape[1]
        assert w.shape[0] == nb and w.shape[2] == n4 and w.shape[1] % k == 0
        in_specs += [pl.BlockSpec((tm, k), lambda j, i: (i, 0)),
                     pl.BlockSpec((1, k, tn), functools.partial(lambda j, i, r_: (j // per, r_, j % per), r_=r))]
        args += [a, w]
    return _pc(body, name=name, grid=(nb * per, m // tm), out_shape=_sds((m, nb * n4), out_dtype), in_specs=in_specs,
               out_specs=pl.BlockSpec((tm, tn), lambda j, i: (i, j)), compiler_params=_cp("parallel", "parallel"))(*args)


def _mm_tn(name, x, ys, n_blocks=1, out_dtype=F32, cargo=(), tt_cap=512):
    t, k = x.shape
    widths = [y.shape[1] for y in ys]
    n4 = sum(widths) // n_blocks
    common = n4
    for w in widths:
        common = _gcd(common, w)
    tk, tn, tt = _div(k, 1536, LANE), _div(common, 1536, LANE), _div(t, tt_cap, 16)
    per = n4 // tn
    starts, acc_w = [], 0
    for w in widths:
        starts.append(acc_w // tn)
        acc_w += w
    counts = [w // tn for w in widths]
    n_y = len(ys)

    def active(i, j):
        return (j >= starts[i]) & (j < starts[i] + counts[i])

    n_t = t // tt

    def body(*refs):
        x_ref, y_refs, o_ref, acc_ref = refs[0], refs[1:1 + n_y], refs[-2], refs[-1]
        j = pl.program_id(1)

        @pl.when(pl.program_id(2) == 0)
        def _():
            acc_ref[...] = jnp.zeros_like(acc_ref)

        for i in range(n_y):
            @pl.when(active(i, j))
            def _():
                acc_ref[...] += lax.dot_general(x_ref[...], y_refs[i][...], (((0,), (0,)), ((), ())),
                                                preferred_element_type=F32)

        @pl.when(pl.program_id(2) == n_t - 1)
        def _():
            o_ref[0] = acc_ref[...].astype(o_ref.dtype)

    def y_spec(i):
        def index(a, j, s):
            on = active(i, j)
            return jnp.where(on, s, 0), jnp.where(on, j - starts[i], 0)
        return pl.BlockSpec((tt, tn), index)

    (out,), received = _pc_cargo(
        body, cargo, name=name, grid=(k // tk, n_blocks * per, n_t), out_shape=[_sds((n_blocks, k, n4), out_dtype)],
        in_specs=[pl.BlockSpec((tt, tk), lambda a, j, s: (s, a))] + [y_spec(i) for i in range(n_y)],
        out_specs=[pl.BlockSpec((1, tk, tn), lambda a, j, s: (j // per, a, j % per))],
        scratch_shapes=[pltpu.VMEM((tk, tn), F32)])(x, *ys)
    return (out, received) if cargo else out


def _gcd(a, b):
    while b:
        a, b = b, a % b
    return a


def _w2(w):
    return (w[None], 0)


def _tok_spec(ts, d):
    return pl.BlockSpec((1, ts, d), lambda b, i: (b, i, 0))


def _seq_spec(d):
    return pl.BlockSpec((1, 1, d), lambda b, i: (b, 0, 0))


def _vec_spec(d):
    return pl.BlockSpec((1, d), lambda b, i: (0, 0))


def _ln_stats(z):
    mu = jnp.mean(z, axis=-1, keepdims=True)
    zc = z - mu
    var = jnp.mean(zc * zc, axis=-1, keepdims=True)
    rstd = lax.rsqrt(var + LN_EPS)
    return zc * rstd, rstd


def _modulate(name, x, sc, sh):
    b, s, d = x.shape
    ts = _div(s, 512, 16)

    def body(x_ref, sc_ref, sh_ref, u_ref):
        u_ref[0] = (x_ref[0] * (1.0 + sc_ref[0]) + sh_ref[0]).astype(BF16)

    return _pc(body, name=name, grid=(b, s // ts), out_shape=_sds(x.shape, BF16),
               in_specs=[_tok_spec(ts, d), _seq_spec(d), _seq_spec(d)], out_specs=_tok_spec(ts, d),
               compiler_params=_cp("parallel", "parallel"))(x, sc, sh)


def _ln_mod_fwd(name, alpha, x, y, g, lng, lnb, sc, sh):
    b, s, d = x.shape
    ts = _div(s, 512, 16)

    def body(x_ref, y_ref, g_ref, lng_ref, lnb_ref, sc_ref, sh_ref, z_ref, xn_ref, u_ref):
        z = alpha * x_ref[0] + (1.0 + g_ref[0]) * y_ref[0]
        xhat, _ = _ln_stats(z)
        xn = xhat * lng_ref[...] + lnb_ref[...]
        z_ref[0] = z
        xn_ref[0] = xn
        u_ref[0] = (xn * (1.0 + sc_ref[0]) + sh_ref[0]).astype(BF16)

    tok, seq, vec = _tok_spec(ts, d), _seq_spec(d), _vec_spec(d)
    return _pc(body, name=name, grid=(b, s // ts),
               out_shape=(_sds(x.shape, F32), _sds(x.shape, F32), _sds(x.shape, BF16)),
               in_specs=[tok, tok, seq, vec, vec, seq, seq], out_specs=(tok, tok, tok),
               compiler_params=_cp("parallel", "parallel"))(x, y, g, lng, lnb, sc, sh)


def _ln_loss_fwd(name, alpha, x, y, g, lng, lnb, target):
    b, s, d = x.shape
    ts = _div(s, 512, 16)

    def body(x_ref, y_ref, g_ref, lng_ref, lnb_ref, t_ref, z_ref, ct_ref, loss_ref):
        @pl.when((pl.program_id(0) == 0) & (pl.program_id(1) == 0))
        def _():
            loss_ref[...] = jnp.zeros_like(loss_ref)
        z = alpha * x_ref[0] + (1.0 + g_ref[0]) * y_ref[0]
        xhat, _ = _ln_stats(z)
        err = xhat * lng_ref[...] + lnb_ref[...] - t_ref[0]
        z_ref[0] = z
        ct_ref[0] = err / d
        part = 0.5 * jnp.sum(jnp.mean(err * err, axis=-1, keepdims=True))
        loss_ref[...] += jnp.full(loss_ref.shape, part, F32)

    tok, seq, vec = _tok_spec(ts, d), _seq_spec(d), _vec_spec(d)
    return _pc(body, name=name, grid=(b, s // ts),
               out_shape=(_sds(x.shape, F32), _sds(x.shape, F32), _sds((SUBLANE, LANE), F32)),
               in_specs=[tok, tok, seq, vec, vec, tok],
               out_specs=(tok, tok, pl.BlockSpec((SUBLANE, LANE), lambda b, i: (0, 0))),
               compiler_params=_cp("arbitrary", "arbitrary"))(x, y, g, lng, lnb, target)


def _sub_bwd(name, alpha, upstream, z, y, g, lng):
    b, s, d = z.shape
    ts = _div(s, 512, 16)
    last = len(upstream) == 1

    def body(*refs):
        if last:
            ct_ref, z_ref, y_ref, g_ref, lng_ref, dz_ref, dy_ref, dg_ref, dlng_ref, dlnb_ref = refs
        else:
            (dzn_ref, dun_ref, xn_ref, scn_ref, z_ref, y_ref, g_ref, lng_ref,
             dz_ref, dy_ref, dg_ref, dlng_ref, dlnb_ref, dsc_ref, dsh_ref) = refs
        first_tile = pl.program_id(1) == 0

        @pl.when(first_tile & (pl.program_id(0) == 0))
        def _():
            dlng_ref[...] = jnp.zeros_like(dlng_ref)
            dlnb_ref[...] = jnp.zeros_like(dlnb_ref)

        @pl.when(first_tile)
        def _():
            dg_ref[...] = jnp.zeros_like(dg_ref)
            if not last:
                dsc_ref[...] = jnp.zeros_like(dsc_ref)
                dsh_ref[...] = jnp.zeros_like(dsh_ref)

        if last:
            ct = ct_ref[0]
        else:
            dun = dun_ref[0]
            ct = alpha * dzn_ref[0] + dun * (1.0 + scn_ref[0])
            dsc_ref[0] += jnp.sum(dun * xn_ref[0], axis=0, keepdims=True)
            dsh_ref[0] += jnp.sum(dun, axis=0, keepdims=True)
        xhat, rstd = _ln_stats(z_ref[0])
        dlng_ref[...] += jnp.sum(ct * xhat, axis=0, keepdims=True)
        dlnb_ref[...] += jnp.sum(ct, axis=0, keepdims=True)
        dxhat = ct * lng_ref[...]
        dz = rstd * (dxhat - jnp.mean(dxhat, axis=-1, keepdims=True)
                     - xhat * jnp.mean(dxhat * xhat, axis=-1, keepdims=True))
        dz_ref[0] = dz
        dy_ref[0] = ((1.0 + g_ref[0]) * dz).astype(BF16)
        dg_ref[0] += jnp.sum(dz * y_ref[0], axis=0, keepdims=True)

    tok, seq, vec = _tok_spec(ts, d), _seq_spec(d), _vec_spec(d)
    seq_out = _sds((b, 1, d), F32)
    out_shape = [_sds(z.shape, F32), _sds(z.shape, BF16), seq_out, _sds((1, d), F32), _sds((1, d), F32)]
    out_specs = [tok, tok, seq, vec, vec]
    if last:
        in_specs = [tok, tok, tok, seq, vec]
    else:
        in_specs = [tok, tok, tok, seq, tok, tok, seq, vec]
        out_shape += [seq_out, seq_out]
        out_specs += [seq, seq]
    return _pc(body, name=name, grid=(b, s // ts), out_shape=tuple(out_shape), in_specs=in_specs,
               out_specs=tuple(out_specs), compiler_params=_cp("arbitrary", "arbitrary"))(*upstream, z, y, g, lng)


def _input_bwd(name, alpha, dz, du, x, sc):
    b, s, d = x.shape
    ts = _div(s, 512, 16)

    def body(dz_ref, du_ref, x_ref, sc_ref, gx_ref, dsc_ref, dsh_ref):
        @pl.when(pl.program_id(1) == 0)
        def _():
            dsc_ref[...] = jnp.zeros_like(dsc_ref)
            dsh_ref[...] = jnp.zeros_like(dsh_ref)
        du_ = du_ref[0]
        gx_ref[0] = alpha * dz_ref[0] + du_ * (1.0 + sc_ref[0])
        dsc_ref[0] += jnp.sum(du_ * x_ref[0], axis=0, keepdims=True)
        dsh_ref[0] += jnp.sum(du_, axis=0, keepdims=True)

    tok, seq = _tok_spec(ts, d), _seq_spec(d)
    seq_out = _sds((b, 1, d), F32)
    return _pc(body, name=name, grid=(b, s // ts), out_shape=(_sds(x.shape, F32), seq_out, seq_out),
               in_specs=[tok, tok, tok, seq], out_specs=(tok, seq, seq),
               compiler_params=_cp("parallel", "arbitrary"))(dz, du, x, sc)


def _rows_iota(shape):
    return lax.broadcasted_iota(jnp.int32, shape, 0)


def _back(v, k):
    return pltpu.roll(v, k, axis=0)


def _ahead(v, k):
    return pltpu.roll(v, v.shape[0] - k, axis=0)


def _conv3(ext, w_ref):
    return w_ref[2:3, :] * ext + w_ref[1:2, :] * _back(ext, 1) + w_ref[0:1, :] * _back(ext, 2)


def _conv3_t(dh_ext, w_ref):
    return w_ref[2:3, :] * dh_ext + w_ref[1:2, :] * _ahead(dh_ext, 1) + w_ref[0:1, :] * _ahead(dh_ext, 2)


def _flag(cond):
    return jnp.where(cond, 1.0, 0.0).astype(F32)


def _sigmoid(v):
    return 1.0 / (1.0 + jnp.exp(-v))


def _halo_specs(ts, tc, halo, n_s, col):
    per = ts // halo
    tile = pl.BlockSpec((1, ts, tc), lambda b, i, j: (b, i, col(j)))
    prev = pl.BlockSpec((1, halo, tc), lambda b, i, j: (b, jnp.maximum(i * per - 1, 0), col(j)))
    nxt = pl.BlockSpec((1, halo, tc), lambda b, i, j: (b, jnp.minimum((i + 1) * per, n_s * per - 1), col(j)))
    return tile, prev, nxt


def _convglu_fwd(name, p, cw, cb):
    b, s, f2 = p.shape
    f = f2 // 2
    ts, tc = _div(s, 512, CONV_HALO), _div(f, 256, LANE)
    n_s, n_c = s // ts, f // tc

    def body(pv_ref, pvh_ref, pg_ref, pgh_ref, wv_ref, wg_ref, bv_ref, bg_ref, a_ref):
        keep = _flag(pl.program_id(1) > 0)

        def conv(t_ref, h_ref, w_ref, b_ref):
            ext = jnp.concatenate([h_ref[0] * keep, t_ref[0]], axis=0)
            return _conv3(ext, w_ref)[CONV_HALO:] + b_ref[...]

        val = conv(pv_ref, pvh_ref, wv_ref, bv_ref)
        gate = conv(pg_ref, pgh_ref, wg_ref, bg_ref)
        a_ref[0] = (gate * _sigmoid(gate) * val).astype(BF16)

    tv, hv, _ = _halo_specs(ts, tc, CONV_HALO, n_s, lambda j: j)
    tg, hg, _ = _halo_specs(ts, tc, CONV_HALO, n_s, lambda j: j + n_c)
    wv = pl.BlockSpec((3, tc), lambda b, i, j: (0, j))
    wg = pl.BlockSpec((3, tc), lambda b, i, j: (0, j + n_c))
    bv = pl.BlockSpec((1, tc), lambda b, i, j: (0, j))
    bg = pl.BlockSpec((1, tc), lambda b, i, j: (0, j + n_c))
    return _pc(body, name=name, grid=(b, n_s, n_c), out_shape=_sds((b, s, f), BF16),
               in_specs=[tv, hv, tg, hg, wv, wg, bv, bg], out_specs=pl.BlockSpec((1, ts, tc), lambda b, i, j: (b, i, j)),
               compiler_params=_cp("parallel", "parallel", "parallel"))(p, p, p, p, cw, cw, cb, cb)


def _convglu_bwd(name, p, da, cw, cb, cargo=()):
    b, s, f2 = p.shape
    f = f2 // 2
    ts, tc = _div(s, 512, CONV_HALO), _div(f, 256, LANE)
    n_s, n_c = s // ts, f // tc

    def body(pv_ref, pvp_ref, pvn_ref, pg_ref, pgp_ref, pgn_ref, da_ref, dan_ref, wv_ref, wg_ref, bv_ref, bg_ref,
             dpv_ref, dpg_ref, dwv_ref, dwg_ref, dbv_ref, dbg_ref):
        bi, i = pl.program_id(1), pl.program_id(2)

        @pl.when((bi == 0) & (i == 0))
        def _():
            for r in (dwv_ref, dwg_ref, dbv_ref, dbg_ref):
                r[...] = jnp.zeros_like(r)

        keep_prev = _flag(i > 0)
        keep_next = _flag(i < n_s - 1)
        pv_ext = jnp.concatenate([pvp_ref[0] * keep_prev, pv_ref[0], pvn_ref[0]], axis=0)
        pg_ext = jnp.concatenate([pgp_ref[0] * keep_prev, pg_ref[0], pgn_ref[0]], axis=0)
        val = _conv3(pv_ext, wv_ref)[CONV_HALO:] + bv_ref[...]
        gate = _conv3(pg_ext, wg_ref)[CONV_HALO:] + bg_ref[...]
        da_ext = jnp.concatenate([da_ref[0], dan_ref[0] * keep_next], axis=0)
        sg = _sigmoid(gate)
        dval = da_ext * gate * sg
        dgate = da_ext * val * (sg * (1.0 + gate * (1.0 - sg)))
        dpv_ref[0] = _conv3_t(dval, wv_ref)[:ts].astype(BF16)
        dpg_ref[0] = _conv3_t(dgate, wg_ref)[:ts].astype(BF16)
        for dh, p_ext, dw_ref, db_ref in ((dval[:ts], pv_ext, dwv_ref, dbv_ref), (dgate[:ts], pg_ext, dwg_ref, dbg_ref)):
            db_ref[...] += jnp.sum(dh, axis=0, keepdims=True)
            for k in range(3):
                shifted = p_ext if k == 2 else _back(p_ext, 2 - k)
                dw_ref[k:k + 1, :] += jnp.sum(dh * shifted[CONV_HALO:CONV_HALO + ts], axis=0, keepdims=True)

    def specs(col):
        per = ts // CONV_HALO
        tile = pl.BlockSpec((1, ts, tc), lambda j, b, i: (b, i, col(j)))
        prev = pl.BlockSpec((1, CONV_HALO, tc), lambda j, b, i: (b, jnp.maximum(i * per - 1, 0), col(j)))
        nxt = pl.BlockSpec((1, CONV_HALO, tc), lambda j, b, i: (b, jnp.minimum((i + 1) * per, n_s * per - 1), col(j)))
        return tile, prev, nxt

    tv, pvp, pvn = specs(lambda j: j)
    tg, pgp, pgn = specs(lambda j: j + n_c)
    wv = pl.BlockSpec((3, tc), lambda j, b, i: (0, j))
    wg = pl.BlockSpec((3, tc), lambda j, b, i: (0, j + n_c))
    bv = pl.BlockSpec((1, tc), lambda j, b, i: (0, j))
    bg = pl.BlockSpec((1, tc), lambda j, b, i: (0, j + n_c))
    out_tile = pl.BlockSpec((1, ts, tc), lambda j, b, i: (b, i, j))
    acc3, acc1 = pl.BlockSpec((3, tc), lambda j, b, i: (0, j)), pl.BlockSpec((1, tc), lambda j, b, i: (0, j))
    (dpv, dpg, dwv, dwg, dbv, dbg), received = _pc_cargo(
        body, cargo, name=name, grid=(n_c, b, n_s),
        out_shape=(_sds((b, s, f), BF16), _sds((b, s, f), BF16), _sds((3, f), F32), _sds((3, f), F32),
                   _sds((1, f), F32), _sds((1, f), F32)),
        in_specs=[tv, pvp, pvn, tg, pgp, pgn, tv, pvn, wv, wg, bv, bg],
        out_specs=(out_tile, out_tile, acc3, acc3, acc1, acc1))(p, p, p, p, p, p, da, da, cw, cw, cb, cb)
    return dpv, dpg, jnp.concatenate([dwv, dwg], axis=1), jnp.concatenate([dbv, dbg], axis=1), received


def _shortconv_fwd(name, q, cw):
    b, s, d3 = q.shape
    d = d3 // 3
    ts, tc = _div(s, 512, CONV_HALO), _div(d, 256, LANE)
    n_s, n_c = s // ts, d // tc

    def body(gb_ref, gc_ref, gch_ref, h_ref, hh_ref, w_ref, r_ref):
        keep = _flag(pl.program_id(1) > 0)
        m_ext = jnp.concatenate([gch_ref[0] * hh_ref[0] * keep, gc_ref[0] * h_ref[0]], axis=0)
        r_ref[0] = (gb_ref[0] * _conv3(m_ext, w_ref)[CONV_HALO:]).astype(BF16)

    tb, _, _ = _halo_specs(ts, tc, CONV_HALO, n_s, lambda j: j)
    tcc, hc, _ = _halo_specs(ts, tc, CONV_HALO, n_s, lambda j: j + n_c)
    th, hh, _ = _halo_specs(ts, tc, CONV_HALO, n_s, lambda j: j + 2 * n_c)
    w = pl.BlockSpec((3, tc), lambda b, i, j: (0, j))
    return _pc(body, name=name, grid=(b, n_s, n_c), out_shape=_sds((b, s, d), BF16),
               in_specs=[tb, tcc, hc, th, hh, w], out_specs=pl.BlockSpec((1, ts, tc), lambda b, i, j: (b, i, j)),
               compiler_params=_cp("parallel", "parallel", "parallel"))(q, q, q, q, q, cw)


def _shortconv_bwd(name, q, dr, cw):
    b, s, d3 = q.shape
    d = d3 // 3
    ts, tc = _div(s, 512, CONV_HALO), _div(d, 256, LANE)
    n_s, n_c = s // ts, d // tc

    def body(gb_ref, gbn_ref, gc_ref, gcp_ref, h_ref, hp_ref, dr_ref, drn_ref, w_ref,
             dgb_ref, dgc_ref, dh_ref, dw_ref):
        bi, i = pl.program_id(1), pl.program_id(2)

        @pl.when((bi == 0) & (i == 0))
        def _():
            dw_ref[...] = jnp.zeros_like(dw_ref)

        keep_prev = _flag(i > 0)
        keep_next = _flag(i < n_s - 1)
        gc, h = gc_ref[0], h_ref[0]
        m_ext = jnp.concatenate([gcp_ref[0] * hp_ref[0] * keep_prev, gc * h], axis=0)
        cm = _conv3(m_ext, w_ref)[CONV_HALO:]
        dr_ = dr_ref[0]
        dgb_ref[0] = (dr_ * cm).astype(BF16)
        dcv_ext = jnp.concatenate([dr_ * gb_ref[0], drn_ref[0] * gbn_ref[0] * keep_next], axis=0)
        dm = _conv3_t(dcv_ext, w_ref)[:ts]
        dgc_ref[0] = (dm * h).astype(BF16)
        dh_ref[0] = (dm * gc).astype(BF16)
        dcv = dcv_ext[:ts]
        for k in range(3):
            shifted = m_ext if k == 2 else _back(m_ext, 2 - k)
            dw_ref[k:k + 1, :] += jnp.sum(dcv * shifted[CONV_HALO:], axis=0, keepdims=True)

    def specs(col):
        per = ts // CONV_HALO
        tile = pl.BlockSpec((1, ts, tc), lambda j, b, i: (b, i, col(j)))
        prev = pl.BlockSpec((1, CONV_HALO, tc), lambda j, b, i: (b, jnp.maximum(i * per - 1, 0), col(j)))
        nxt = pl.BlockSpec((1, CONV_HALO, tc), lambda j, b, i: (b, jnp.minimum((i + 1) * per, n_s * per - 1), col(j)))
        return tile, prev, nxt

    tb, _, nb = specs(lambda j: j)
    tcc, pc_, _ = specs(lambda j: j + n_c)
    th, ph, _ = specs(lambda j: j + 2 * n_c)
    w = pl.BlockSpec((3, tc), lambda j, b, i: (0, j))
    out_tile = pl.BlockSpec((1, ts, tc), lambda j, b, i: (b, i, j))
    o = _sds((b, s, d), BF16)
    return _pc(body, name=name, grid=(n_c, b, n_s), out_shape=(o, o, o, _sds((3, d), F32)),
               in_specs=[tb, nb, tcc, pc_, th, ph, tb, nb, w], out_specs=(out_tile, out_tile, out_tile, w),
               compiler_params=_cp("parallel", "arbitrary", "arbitrary"))(q, q, q, q, q, q, dr, dr, cw)


def _pick_window(group, cands):
    gid = jnp.full(cands[0].shape, group, jnp.int32)
    out = cands[-1]
    for k in range(len(cands) - 2, -1, -1):
        out = jnp.where(gid == k, cands[k], out)
    return out


def _window_sums(v, shift):
    s1 = v + shift(v, 1)
    s2 = s1 + shift(s1, 2)
    s3 = s2 + shift(s2, 4)
    s4 = s3 + shift(s3, 8)
    return [s1, s2, s3, s4]


def _pool_counts(group, first_row, n_rows, cols):
    t = _rows_iota((n_rows, cols)) + first_row
    window = _pick_window(group, [jnp.full((n_rows, cols), 2 << k, jnp.int32) for k in range(POOL_GROUPS)])
    return jnp.minimum(t + 1, window).astype(F32)


def _pool_fwd(name, x, sc, sh, w, scale):
    b, s, d = x.shape
    tc = d // POOL_GROUPS
    ts = _div(s, 512, POOL_HALO)
    n_s = s // ts

    def body(x_ref, xp_ref, sc_ref, sh_ref, w_ref, scale_ref, y_ref):
        i, grp = pl.program_id(1), pl.program_id(2)
        keep = _flag(i > 0)
        mod = 1.0 + sc_ref[0]
        u = x_ref[0] * mod + sh_ref[0]
        u_ext = jnp.concatenate([(xp_ref[0] * mod + sh_ref[0]) * keep, u], axis=0)
        summed = _pick_window(grp, _window_sums(u_ext, _back))[POOL_HALO:]
        pooled = summed / _pool_counts(grp, i * ts, ts, tc) - u
        y_ref[0] = jnp.dot(pooled.astype(BF16), w_ref[0], preferred_element_type=F32) * scale_ref[...]

    tile, prev, _ = _halo_specs(ts, tc, POOL_HALO, n_s, lambda j: j)
    seq = pl.BlockSpec((1, 1, tc), lambda b, i, j: (b, 0, j))
    return _pc(body, name=name, grid=(b, n_s, POOL_GROUPS), out_shape=_sds(x.shape, F32),
               in_specs=[tile, prev, seq, seq, pl.BlockSpec((1, tc, tc), lambda b, i, j: (j, 0, 0)),
                         pl.BlockSpec((1, tc), lambda b, i, j: (0, j))],
               out_specs=pl.BlockSpec((1, ts, tc), lambda b, i, j: (b, i, j)),
               compiler_params=_cp("parallel", "parallel", "parallel"))(x, x, sc, sh, w, scale)


def _pool_bwd(name, x, sc, sh, dy, w, w_t, scale):
    b, s, d = x.shape
    tc = d // POOL_GROUPS
    ts = _div(s, 512, POOL_HALO)
    n_s = s // ts

    def body(x_ref, xp_ref, sc_ref, sh_ref, dy_ref, dyn_ref, w_ref, wt_ref, scale_ref, du_ref, dw_ref, dscale_ref):
        grp, bi, i = pl.program_id(0), pl.program_id(1), pl.program_id(2)

        @pl.when((bi == 0) & (i == 0))
        def _():
            dw_ref[...] = jnp.zeros_like(dw_ref)
            dscale_ref[...] = jnp.zeros_like(dscale_ref)

        keep_prev = _flag(i > 0)
        keep_next = _flag(i < n_s - 1)
        mod = 1.0 + sc_ref[0]
        u = x_ref[0] * mod + sh_ref[0]
        u_ext = jnp.concatenate([(xp_ref[0] * mod + sh_ref[0]) * keep_prev, u], axis=0)
        summed = _pick_window(grp, _window_sums(u_ext, _back))[POOL_HALO:]
        pooled = (summed / _pool_counts(grp, i * ts, ts, tc) - u).astype(BF16)
        dy_ = dy_ref[0].astype(F32)
        ymat = jnp.dot(pooled, w_ref[0], preferred_element_type=F32)
        dscale_ref[...] += jnp.sum(dy_ * ymat, axis=0, keepdims=True)
        dys_ext = (jnp.concatenate([dy_, dyn_ref[0].astype(F32) * keep_next], axis=0) * scale_ref[...]).astype(BF16)
        dw_ref[0] += lax.dot_general(pooled, dys_ext[:ts], (((0,), (0,)), ((), ())), preferred_element_type=F32)
        dpooled = jnp.dot(dys_ext, wt_ref[0], preferred_element_type=F32)
        e = dpooled / _pool_counts(grp, i * ts, ts + POOL_HALO, tc)
        du_ref[0] = _pick_window(grp, _window_sums(e, _ahead))[:ts] - dpooled[:ts]

    per = ts // POOL_HALO
    tile = pl.BlockSpec((1, ts, tc), lambda j, b, i: (b, i, j))
    prev = pl.BlockSpec((1, POOL_HALO, tc), lambda j, b, i: (b, jnp.maximum(i * per - 1, 0), j))
    nxt = pl.BlockSpec((1, POOL_HALO, tc), lambda j, b, i: (b, jnp.minimum((i + 1) * per, n_s * per - 1), j))
    seq = pl.BlockSpec((1, 1, tc), lambda j, b, i: (b, 0, j))
    wsp = pl.BlockSpec((1, tc, tc), lambda j, b, i: (j, 0, 0))
    vec = pl.BlockSpec((1, tc), lambda j, b, i: (0, j))
    return _pc(body, name=name, grid=(POOL_GROUPS, b, n_s),
               out_shape=(_sds(x.shape, F32), _sds((POOL_GROUPS, tc, tc), F32), _sds((1, d), F32)),
               in_specs=[tile, prev, seq, seq, tile, nxt, wsp, wsp, vec], out_specs=(tile, wsp, vec),
               compiler_params=_cp("parallel", "arbitrary", "arbitrary"))(x, x, sc, sh, dy, dy, w, w_t, scale)


def _rope_swap(v):
    lane = lax.broadcasted_iota(jnp.int32, v.shape, v.ndim - 1)
    lo, hi = QK_NOPE, QK_NOPE + QK_ROPE // 2
    from_above = pltpu.roll(v, HEAD_PAD - QK_ROPE // 2, axis=v.ndim - 1)
    from_below = pltpu.roll(v, QK_ROPE // 2, axis=v.ndim - 1)
    return jnp.where((lane >= lo) & (lane < hi), from_above,
                     jnp.where((lane >= hi) & (lane < hi + QK_ROPE // 2), from_below, 0.0))


def _rope(v, cos_t, sin_t):
    return v * cos_t + _rope_swap(v) * sin_t


def _rope_t(dv, cos_t, sin_t):
    return dv * cos_t + _rope_swap(dv * sin_t)


def _rms(v, g):
    r = lax.rsqrt(jnp.mean(v * v, axis=-1, keepdims=True) + RMS_EPS)
    return v * r, r


def _mla_norm_fwd(name, a, qn, kvn, cos_t, sin_t):
    b, s, wa = a.shape
    ql, kvl = qn.shape[1], kvn.shape[1]
    ts = _div(s, 512, 16)

    def body(aq_ref, akv_ref, ape_ref, qn_ref, kvn_ref, cos_ref, sin_ref, cq_ref, ckv_ref, kpe_ref):
        yq, _ = _rms(aq_ref[0], None)
        cq_ref[0] = (yq * qn_ref[...]).astype(BF16)
        ykv, _ = _rms(akv_ref[0], None)
        ckv_ref[0] = (ykv * kvn_ref[...]).astype(BF16)
        kpe_ref[0] = _rope(ape_ref[0], cos_ref[0], sin_ref[0])

    tok = lambda w, col: pl.BlockSpec((1, ts, w), lambda b, i: (b, i, col))
    return _pc(body, name=name, grid=(b, s // ts),
               out_shape=(_sds((b, s, ql), BF16), _sds((b, s, kvl), BF16), _sds((b, s, HEAD_PAD), F32)),
               in_specs=[tok(ql, 0), tok(kvl, ql // kvl), tok(HEAD_PAD, (ql + kvl) // HEAD_PAD), _vec_spec(ql),
                         _vec_spec(kvl), tok(HEAD_PAD, 0), tok(HEAD_PAD, 0)],
               out_specs=(tok(ql, 0), tok(kvl, 0), tok(HEAD_PAD, 0)),
               compiler_params=_cp("parallel", "parallel"))(a, a, a, qn, kvn, cos_t, sin_t)


def _mla_norm_bwd(name, a, dcq, dckv, dkpe, qn, kvn):
    b, s, wa = a.shape
    ql, kvl = qn.shape[1], kvn.shape[1]
    ts = _div(s, 512, 16)

    def body(a_ref, dcq_ref, dckv_ref, dkpe_ref, qn_ref, kvn_ref, da_ref, dqn_ref, dkvn_ref):
        @pl.when((pl.program_id(0) == 0) & (pl.program_id(1) == 0))
        def _():
            dqn_ref[...] = jnp.zeros_like(dqn_ref)
            dkvn_ref[...] = jnp.zeros_like(dkvn_ref)

        def one(v, dc, g_ref, dg_ref):
            yv, r = _rms(v, None)
            dg_ref[...] += jnp.sum(dc * yv, axis=0, keepdims=True)
            dyv = dc * g_ref[...]
            return r * (dyv - yv * jnp.mean(dyv * yv, axis=-1, keepdims=True))

        av = a_ref[0]
        da_ref[0, :, 0:ql] = one(av[:, 0:ql], dcq_ref[0], qn_ref, dqn_ref).astype(BF16)
        da_ref[0, :, ql:ql + kvl] = one(av[:, ql:ql + kvl], dckv_ref[0], kvn_ref, dkvn_ref).astype(BF16)
        da_ref[0, :, ql + kvl:] = dkpe_ref[0].astype(BF16)

    return _pc(body, name=name, grid=(b, s // ts),
               out_shape=(_sds(a.shape, BF16), _sds((1, ql), F32), _sds((1, kvl), F32)),
               in_specs=[_tok_spec(ts, wa), _tok_spec(ts, ql), _tok_spec(ts, kvl), _tok_spec(ts, HEAD_PAD),
                         _vec_spec(ql), _vec_spec(kvl)],
               out_specs=(_tok_spec(ts, wa), _vec_spec(ql), _vec_spec(kvl)),
               compiler_params=_cp("arbitrary", "arbitrary"))(a, dcq, dckv, dkpe, qn, kvn)


def _mla_prep_fwd(name, q_raw, kv_raw, kpe, cos_t, sin_t, n_heads):
    b, s, wq = q_raw.shape
    ts = _div(s, 256, 16)

    def body(q_ref, k_ref, v_ref, kpe_ref, cos_ref, sin_ref, qo_ref, ko_ref, vo_ref):
        cos_, sin_, kpe_ = cos_ref[0], sin_ref[0], kpe_ref[0]
        for h in range(n_heads):
            lanes = slice(h * HEAD_PAD, (h + 1) * HEAD_PAD)
            qo_ref[0, :, lanes] = _rope(q_ref[0, :, lanes], cos_, sin_).astype(BF16)
            ko_ref[0, :, lanes] = (k_ref[0, :, lanes] + kpe_).astype(BF16)
        vo_ref[0] = v_ref[0].astype(BF16)

    wide = lambda part: pl.BlockSpec((1, ts, wq), lambda b, i: (b, i, part))
    tok = pl.BlockSpec((1, ts, HEAD_PAD), lambda b, i: (b, i, 0))
    o = _sds(q_raw.shape, BF16)
    return _pc(body, name=name, grid=(b, s // ts), out_shape=(o, o, o),
               in_specs=[wide(0), wide(0), wide(1), tok, tok, tok], out_specs=(wide(0), wide(0), wide(0)),
               compiler_params=_cp("parallel", "parallel"))(q_raw, kv_raw, kv_raw, kpe, cos_t, sin_t)


def _mla_prep_bwd(name, dq, dk, dv, cos_t, sin_t, n_heads):
    b, s, wq = dq.shape
    ts = _div(s, 256, 16)

    def body(dq_ref, dk_ref, dv_ref, cos_ref, sin_ref, dqr_ref, dkv_ref, dkpe_ref):
        cos_, sin_ = cos_ref[0], sin_ref[0]
        dk_sum = None
        for h in range(n_heads):
            lanes = slice(h * HEAD_PAD, (h + 1) * HEAD_PAD)
            dqr_ref[0, :, lanes] = _rope_t(dq_ref[0, :, lanes], cos_, sin_).astype(BF16)
            dk_h = dk_ref[0, :, lanes]
            dkv_ref[0, :, lanes] = dk_h.astype(BF16)
            dk_sum = dk_h if dk_sum is None else dk_sum + dk_h
        dkv_ref[0, :, wq:2 * wq] = dv_ref[0]
        dkpe_ref[0] = _rope_t(dk_sum, cos_, sin_)

    wide = pl.BlockSpec((1, ts, wq), lambda b, i: (b, i, 0))
    both = pl.BlockSpec((1, ts, 2 * wq), lambda b, i: (b, i, 0))
    tok = pl.BlockSpec((1, ts, HEAD_PAD), lambda b, i: (b, i, 0))
    return _pc(body, name=name, grid=(b, s // ts),
               out_shape=(_sds(dq.shape, BF16), _sds((b, s, 2 * wq), BF16), _sds((b, s, HEAD_PAD), F32)),
               in_specs=[wide, wide, wide, tok, tok], out_specs=(wide, both, tok),
               compiler_params=_cp("parallel", "parallel"))(dq, dk, dv, cos_t, sin_t)


FLASH_TILE = 1024
LOG2_E = 1.4426950408889634


def _heads_per_step(n_heads):
    return 2 if n_heads % 2 == 0 else 1


def _causal_mask(i, j, tq, tk):
    rows = lax.broadcasted_iota(jnp.int32, (tq, tk), 0) + i * tq
    cols = lax.broadcasted_iota(jnp.int32, (tq, tk), 1) + j * tk
    return cols <= rows


def _nt(a, b):
    return lax.dot_general(a, b, (((1,), (1,)), ((), ())), preferred_element_type=F32)


def _tn(a, b):
    return lax.dot_general(a, b, (((0,), (0,)), ((), ())), preferred_element_type=F32)


def _flash_fwd(name, q, k, v, n_heads, sm_scale):
    b, s, _ = q.shape
    t, hp = _div(s, FLASH_TILE, LANE), _heads_per_step(n_heads)
    n, w = s // t, hp * HEAD_PAD
    neg = float(jnp.finfo(jnp.float32).min)
    c2 = sm_scale * LOG2_E

    def body(q_ref, k_ref, v_ref, o_ref, lse_ref, m_ref, l_ref, acc_ref):
        i, j = pl.program_id(2), pl.program_id(3)

        @pl.when(j == 0)
        def _():
            m_ref[...] = jnp.full(m_ref.shape, neg, F32)
            l_ref[...] = jnp.zeros_like(l_ref)
            acc_ref[...] = jnp.zeros_like(acc_ref)

        def block(on_diagonal):
            for hh in range(hp):
                ln = slice(hh * HEAD_PAD, (hh + 1) * HEAD_PAD)
                sc = _nt(q_ref[0, :, ln], k_ref[0, :, ln])
                if on_diagonal:
                    sc = jnp.where(_causal_mask(i, j, t, t), sc, neg)
                m_old = m_ref[hh]
                m_new = jnp.maximum(m_old, jnp.max(sc, axis=-1, keepdims=True))
                p = jnp.exp2((sc - m_new) * c2)
                corr = jnp.exp2((m_old - m_new) * c2)
                l_ref[hh] = corr * l_ref[hh] + jnp.sum(p, axis=-1, keepdims=True)
                acc_ref[:, ln] = corr * acc_ref[:, ln] + jnp.dot(p.astype(BF16), v_ref[0, :, ln],
                                                                 preferred_element_type=F32)
                m_ref[hh] = m_new

        pl.when(j < i)(functools.partial(block, False))
        pl.when(j == i)(functools.partial(block, True))

        @pl.when(j == n - 1)
        def _():
            for hh in range(hp):
                ln = slice(hh * HEAD_PAD, (hh + 1) * HEAD_PAD)
                o_ref[0, :, ln] = (acc_ref[:, ln] / l_ref[hh]).astype(BF16)
                lse_ref[0, :, ln] = jnp.broadcast_to(m_ref[hh] * sm_scale + jnp.log(l_ref[hh]), (t, HEAD_PAD))

    qs = pl.BlockSpec((1, t, w), lambda b, h, i, j: (b, i, h))
    ks = pl.BlockSpec((1, t, w), lambda b, h, i, j: (b, jnp.minimum(j, i), h))
    return _pc(body, name=name, grid=(b, n_heads // hp, n, n),
               out_shape=(_sds(q.shape, BF16), _sds(q.shape, F32)), in_specs=[qs, ks, ks], out_specs=(qs, qs),
               scratch_shapes=[pltpu.VMEM((hp, t, 1), F32), pltpu.VMEM((hp, t, 1), F32), pltpu.VMEM((t, w), F32)],
               compiler_params=_cp("parallel", "parallel", "parallel", "arbitrary"))(q, k, v)


def _flash_dq(name, q, k, v, o, lse, do, n_heads, sm_scale):
    b, s, _ = q.shape
    t, hp = _div(s, FLASH_TILE, LANE), _heads_per_step(n_heads)
    n, w = s // t, hp * HEAD_PAD
    c2 = sm_scale * LOG2_E

    def body(q_ref, k_ref, v_ref, o_ref, lse_ref, do_ref, dq_ref, acc_ref, delta_ref):
        i, j = pl.program_id(2), pl.program_id(3)

        @pl.when(j == 0)
        def _():
            acc_ref[...] = jnp.zeros_like(acc_ref)
            for hh in range(hp):
                ln = slice(hh * HEAD_PAD, (hh + 1) * HEAD_PAD)
                delta_ref[hh] = jnp.sum(do_ref[0, :, ln].astype(F32) * o_ref[0, :, ln].astype(F32), axis=-1,
                                        keepdims=True)

        def block(on_diagonal):
            for hh in range(hp):
                ln = slice(hh * HEAD_PAD, (hh + 1) * HEAD_PAD)
                sc = _nt(q_ref[0, :, ln], k_ref[0, :, ln])
                p = jnp.exp2(sc * c2 - lse_ref[0, :, hh * HEAD_PAD:hh * HEAD_PAD + 1] * LOG2_E)
                if on_diagonal:
                    p = jnp.where(_causal_mask(i, j, t, t), p, 0.0)
                dp = _nt(do_ref[0, :, ln], v_ref[0, :, ln])
                ds = p * (dp - delta_ref[hh])
                acc_ref[:, ln] += jnp.dot(ds.astype(BF16), k_ref[0, :, ln], preferred_element_type=F32)

        pl.when(j < i)(functools.partial(block, False))
        pl.when(j == i)(functools.partial(block, True))

        @pl.when(j == n - 1)
        def _():
            dq_ref[0] = acc_ref[...] * sm_scale

    qs = pl.BlockSpec((1, t, w), lambda b, h, i, j: (b, i, h))
    ks = pl.BlockSpec((1, t, w), lambda b, h, i, j: (b, jnp.minimum(j, i), h))
    return _pc(body, name=name, grid=(b, n_heads // hp, n, n), out_shape=_sds(q.shape, F32),
               in_specs=[qs, ks, ks, qs, qs, qs], out_specs=qs,
               scratch_shapes=[pltpu.VMEM((t, w), F32), pltpu.VMEM((hp, t, 1), F32)],
               compiler_params=_cp("parallel", "parallel", "parallel", "arbitrary"))(q, k, v, o, lse, do)


def _flash_dkv(name, q, k, v, o, lse, do, n_heads, sm_scale):
    b, s, _ = q.shape
    t, hp = _div(s, FLASH_TILE, LANE), _heads_per_step(n_heads)
    n, w = s // t, hp * HEAD_PAD
    c2 = sm_scale * LOG2_E

    def body(q_ref, k_ref, v_ref, o_ref, lse_ref, do_ref, dk_ref, dv_ref, dk_acc, dv_acc):
        j, i = pl.program_id(2), pl.program_id(3)

        @pl.when(i == 0)
        def _():
            dk_acc[...] = jnp.zeros_like(dk_acc)
            dv_acc[...] = jnp.zeros_like(dv_acc)

        def block(on_diagonal):
            for hh in range(hp):
                ln = slice(hh * HEAD_PAD, (hh + 1) * HEAD_PAD)
                do_ = do_ref[0, :, ln]
                delta = jnp.sum(do_.astype(F32) * o_ref[0, :, ln].astype(F32), axis=-1, keepdims=True)
                sc = _nt(q_ref[0, :, ln], k_ref[0, :, ln])
                p = jnp.exp2(sc * c2 - lse_ref[0, :, hh * HEAD_PAD:hh * HEAD_PAD + 1] * LOG2_E)
                if on_diagonal:
                    p = jnp.where(_causal_mask(i, j, t, t), p, 0.0)
                dv_acc[:, ln] += _tn(p.astype(BF16), do_)
                dp = _nt(do_, v_ref[0, :, ln])
                ds = p * (dp - delta)
                dk_acc[:, ln] += _tn(ds.astype(BF16), q_ref[0, :, ln])

        pl.when(i > j)(functools.partial(block, False))
        pl.when(i == j)(functools.partial(block, True))

        @pl.when(i == n - 1)
        def _():
            dk_ref[0] = dk_acc[...] * sm_scale
            dv_ref[0] = dv_acc[...].astype(BF16)

    qs = pl.BlockSpec((1, t, w), lambda b, h, j, i: (b, jnp.maximum(i, j), h))
    ks = pl.BlockSpec((1, t, w), lambda b, h, j, i: (b, j, h))
    return _pc(body, name=name, grid=(b, n_heads // hp, n, n), out_shape=(_sds(q.shape, F32), _sds(q.shape, BF16)),
               in_specs=[qs, ks, ks, qs, qs, qs], out_specs=(ks, ks),
               scratch_shapes=[pltpu.VMEM((t, w), F32), pltpu.VMEM((t, w), F32)],
               compiler_params=_cp("parallel", "parallel", "parallel", "arbitrary"))(q, k, v, o, lse, do)


def _mod_fwd(name, c_all, w, bias):
    depth, d, n = w.shape
    rows = c_all.shape[0]

    def body(c_ref, w_ref, b_ref, o_ref):
        cv = c_ref[...]
        cond = (cv * _sigmoid(cv)).astype(BF16)
        o_ref[0] = jnp.dot(cond, w_ref[0].astype(BF16), preferred_element_type=F32) + b_ref[0]

    return _pc(body, name=name, grid=(depth,), out_shape=_sds((depth, rows, n), F32),
               in_specs=[pl.BlockSpec((rows, d), lambda l: (0, 0)), pl.BlockSpec((1, d, n), lambda l: (l, 0, 0)),
                         pl.BlockSpec((1, 1, n), lambda l: (l, 0, 0))],
               out_specs=pl.BlockSpec((1, rows, n), lambda l: (l, 0, 0)), compiler_params=_cp("parallel"))(c_all, w, bias)


def _mod_bwd(name, c_all, dmod_cols, dmod_all):
    depth, rows, n = dmod_cols.shape
    d = c_all.shape[1]
    n_all = dmod_all.shape[2]
    tn = _div(n, 512, LANE)

    def body(c_ref, dm_ref, dma_ref, gw_ref, gb_ref):
        cv = c_ref[...]
        cond = (cv * _sigmoid(cv)).astype(BF16)
        gw_ref[0] = _tn(cond, dm_ref[0].astype(BF16))

        @pl.when(pl.program_id(1) == 0)
        def _():
            gb_ref[0] = jnp.sum(dma_ref[0], axis=0, keepdims=True)

    return _pc(body, name=name, grid=(depth, n // tn),
               out_shape=(_sds((depth, d, n), F32), _sds((depth, 1, n_all), F32)),
               in_specs=[pl.BlockSpec((rows, d), lambda l, j: (0, 0)), pl.BlockSpec((1, rows, tn), lambda l, j: (l, 0, j)),
                         pl.BlockSpec((1, rows, n_all), lambda l, j: (l, 0, 0))],
               out_specs=(pl.BlockSpec((1, d, tn), lambda l, j: (l, 0, j)), pl.BlockSpec((1, 1, n_all), lambda l, j: (l, 0, 0))),
               compiler_params=_cp("parallel", "arbitrary"))(c_all, dmod_cols, dmod_all)


def _adamw(name, w, g, m, v):
    shape = w.shape
    cols = shape[-1]
    rows = _size(shape) // cols
    tr = _div(rows, max(SUBLANE, (2 ** 19) // cols // SUBLANE * SUBLANE), SUBLANE)
    c1 = 1.0 - ADAM_B1 ** ADAM_STEP
    c2 = 1.0 - ADAM_B2 ** ADAM_STEP

    def body(w_ref, g_ref, m_ref, v_ref, d_ref, mo_ref, vo_ref):
        gv = g_ref[...]
        m_new = ADAM_B1 * m_ref[...] + (1.0 - ADAM_B1) * gv
        v_new = ADAM_B2 * v_ref[...] + (1.0 - ADAM_B2) * (gv * gv)
        m_hat = m_new / c1
        v_hat = v_new / c2
        d_ref[...] = -ADAM_LR * (m_hat / (jnp.sqrt(v_hat) + ADAM_EPS) + ADAM_WD * w_ref[...])
        mo_ref[...] = m_new
        vo_ref[...] = v_new

    spec = pl.BlockSpec((tr, cols), lambda i: (i, 0))
    o = _sds((rows, cols), F32)
    outs = _pc(body, name=name, grid=(rows // tr,), out_shape=(o, o, o), in_specs=[spec] * 4, out_specs=(spec,) * 3,
               compiler_params=_cp("parallel"))(*[a.reshape(rows, cols) for a in (w, g, m, v)])
    return tuple(a.reshape(shape) for a in outs)


def _rope_tables(positions):
    half = QK_ROPE // 2
    inv_freq = ROPE_THETA ** (-jnp.arange(0, QK_ROPE, 2, dtype=F32) / QK_ROPE)
    ang = positions.astype(F32)[..., None] * inv_freq
    cos, sin = jnp.cos(ang), jnp.sin(ang)
    lead = positions.shape
    ones = jnp.ones(lead + (QK_NOPE,), F32)
    tail_one = jnp.ones(lead + (HEAD_PAD - QK_NOPE - QK_ROPE,), F32)
    cos_t = jnp.concatenate([ones, cos, cos, tail_one], axis=-1)
    sin_t = jnp.concatenate([0 * ones, -sin, sin, 0 * tail_one], axis=-1)
    return cos_t, sin_t


def _pad_heads(w, n_heads, parts, axis):
    w = jnp.moveaxis(w, axis, -1)
    lead = w.shape[:-1]
    per = w.shape[-1] // n_heads
    w = w.reshape(lead + (n_heads, per))
    kept = jnp.concatenate([w[..., a:b_] for a, b_ in parts], axis=-1)
    pad = HEAD_PAD - kept.shape[-1]
    kept = jnp.concatenate([kept, jnp.zeros(lead + (n_heads, pad), w.dtype)], axis=-1)
    return jnp.moveaxis(kept.reshape(lead + (n_heads * HEAD_PAD,)), -1, axis)


def _unpad_heads(g, n_heads, width, axis):
    g = jnp.moveaxis(g, axis, -1)
    lead = g.shape[:-1]
    g = g.reshape(lead + (n_heads, HEAD_PAD))[..., :width]
    return g, lead


def kernel(x, c, positions, mod_w, mod_b, ln_g, ln_b, pool_w, pool_scale, mla_w_a, mla_q_norm, mla_w_uq, mla_kv_norm, mla_w_ukv, mla_w_o, sc_w_in, sc_conv, sc_w_out, ffn_w_up, ffn_conv, ffn_conv_b, ffn_w_down, loss_target, m_mod_w, m_mod_b, m_ln_g, m_ln_b, m_pool_w, m_pool_scale, m_mla_w_a, m_mla_q_norm, m_mla_w_uq, m_mla_kv_norm, m_mla_w_ukv, m_mla_w_o, m_sc_w_in, m_sc_conv, m_sc_w_out, m_ffn_w_up, m_ffn_conv, m_ffn_conv_b, m_ffn_w_down, v_mod_w, v_mod_b, v_ln_g, v_ln_b, v_pool_w, v_pool_scale, v_mla_w_a, v_mla_q_norm, v_mla_w_uq, v_mla_kv_norm, v_mla_w_ukv, v_mla_w_o, v_sc_w_in, v_sc_conv, v_sc_w_out, v_ffn_w_up, v_ffn_conv, v_ffn_conv_b, v_ffn_w_down):
    wts = dict(mod_w=mod_w, mod_b=mod_b, ln_g=ln_g, ln_b=ln_b, pool_w=pool_w, pool_scale=pool_scale, mla_w_a=mla_w_a,
               mla_q_norm=mla_q_norm, mla_w_uq=mla_w_uq, mla_kv_norm=mla_kv_norm, mla_w_ukv=mla_w_ukv, mla_w_o=mla_w_o,
               sc_w_in=sc_w_in, sc_conv=sc_conv, sc_w_out=sc_w_out, ffn_w_up=ffn_w_up, ffn_conv=ffn_conv,
               ffn_conv_b=ffn_conv_b, ffn_w_down=ffn_w_down)
    mom1 = dict(mod_w=m_mod_w, mod_b=m_mod_b, ln_g=m_ln_g, ln_b=m_ln_b, pool_w=m_pool_w, pool_scale=m_pool_scale,
                mla_w_a=m_mla_w_a, mla_q_norm=m_mla_q_norm, mla_w_uq=m_mla_w_uq, mla_kv_norm=m_mla_kv_norm,
                mla_w_ukv=m_mla_w_ukv, mla_w_o=m_mla_w_o, sc_w_in=m_sc_w_in, sc_conv=m_sc_conv, sc_w_out=m_sc_w_out,
                ffn_w_up=m_ffn_w_up, ffn_conv=m_ffn_conv, ffn_conv_b=m_ffn_conv_b, ffn_w_down=m_ffn_w_down)
    mom2 = dict(mod_w=v_mod_w, mod_b=v_mod_b, ln_g=v_ln_g, ln_b=v_ln_b, pool_w=v_pool_w, pool_scale=v_pool_scale,
                mla_w_a=v_mla_w_a, mla_q_norm=v_mla_q_norm, mla_w_uq=v_mla_w_uq, mla_kv_norm=v_mla_kv_norm,
                mla_w_ukv=v_mla_w_ukv, mla_w_o=v_mla_w_o, sc_w_in=v_sc_w_in, sc_conv=v_sc_conv, sc_w_out=v_sc_w_out,
                ffn_w_up=v_ffn_w_up, ffn_conv=v_ffn_conv, ffn_conv_b=v_ffn_conv_b, ffn_w_down=v_ffn_w_down)

    bsz, seq, d = x.shape
    depth = mod_b.shape[0]
    n_tok = bsz * seq
    n_heads = d // V_HEAD
    ql, kvl = mla_q_norm.shape[1], mla_kv_norm.shape[1]
    alpha = float((2 * depth) ** 0.25)
    sm_scale = float((QK_NOPE + QK_ROPE) ** -0.5)
    mx, my, mc = lax.axis_index("x"), lax.axis_index("y"), lax.axis_index("c")
    chip = 2 * mx + my
    dev = 2 * chip + mc

    small_names = list(SMALL_SHARDED)
    small_pack, small_spans = _pack_rows([c] + [wts[n] for n in small_names], F32, SUBLANE)
    rows_small = small_pack.shape[0]
    small_all = _all_gather8("gather_small_params", small_pack, True).reshape(N_DEV, rows_small * PACK_COLS)
    c_all = small_all[:, :c.size].reshape(N_DEV * bsz, d)
    per_chip = small_all[0::2]
    full = dict(wts)
    for n, (off, shape) in zip(small_names, small_spans[1:]):
        blocks = per_chip[:, off:off + _size(shape)].reshape((N_CHIP,) + tuple(shape))
        full[n] = _join_chips(blocks, SMALL_SHARDED[n])

    n_mod = mod_w.shape[2]
    bias_cols = lax.dynamic_slice_in_dim(mod_b, chip * n_mod, n_mod, axis=1)[:, None, :]
    mod_cols = _mod_fwd("mod_fwd", c_all, mod_w, bias_cols)
    half_rows = (N_DEV * bsz) // 2
    mod_half = lax.dynamic_slice_in_dim(mod_cols, mc * half_rows, half_rows, axis=1).reshape(depth * half_rows, n_mod)
    mod_all = _all_gather8("gather_mod", mod_half, True).reshape(N_CHIP, 2, depth, half_rows, n_mod)
    mod_all = jnp.transpose(mod_all, (2, 1, 3, 0, 4)).reshape(depth, N_DEV * bsz, N_CHIP * n_mod)
    mod_mine = lax.dynamic_slice_in_dim(mod_all, dev * bsz, bsz, axis=1)
    mods = [[mod_mine[l, :, k * d:(k + 1) * d][:, None, :] for k in range(6)] for l in range(depth)]

    big_names = list(BIG)
    gathered = _gather_weights("gather_weights", [wts[n].astype(BF16).reshape(-1, wts[n].shape[-1]) for n in big_names])
    by_chip = {n: g.reshape((N_CHIP,) + wts[n].shape) for n, g in zip(big_names, gathered)}
    f_hid = ffn_w_down.shape[1] * N_CHIP
    for n in ('pool_w', 'mla_w_a', 'mla_w_uq', 'mla_w_ukv', 'mla_w_o', 'sc_w_out'):
        full[n] = jnp.concatenate([by_chip[n][j] for j in range(N_CHIP)], axis=BIG[n])
    w_up_cols = by_chip['ffn_w_up'].reshape(N_CHIP, depth * d, -1)
    w_up_rows = jnp.transpose(by_chip['ffn_w_up'], (1, 0, 3, 2)).reshape(1, depth * 2 * f_hid, d)
    w_down_rows = jnp.transpose(by_chip['ffn_w_down'], (1, 0, 2, 3)).reshape(1, depth * f_hid, d)
    w_down_t = jnp.transpose(by_chip['ffn_w_down'], (1, 3, 0, 2)).reshape(1, depth * d, f_hid)
    n_sc = sc_w_in.shape[0]
    w_in_cols = by_chip['sc_w_in'].reshape(N_CHIP, n_sc * d, -1)
    w_in_rows = jnp.transpose(by_chip['sc_w_in'], (1, 0, 3, 2)).reshape(1, n_sc * 3 * d, d)

    nope_rope = [(0, QK_NOPE + QK_ROPE)]
    cos_t, sin_t = _rope_tables(positions)

    def t2(a):
        return a.reshape(n_tok, a.shape[-1])

    def t3(a):
        return a.reshape(bsz, seq, a.shape[-1])

    saved = []
    xin = x
    u = _modulate("modulate_in", x, mods[0][1], mods[0][0])
    loss_acc = None
    for l in range(depth):
        sh1, sc1, g1, sh2, sc2, g2 = mods[l]
        kind, j = l % 3, l // 3
        st = dict(x=xin)
        if kind == 0:
            w = full['pool_w'][j]
            st.update(w=w, w_t=jnp.swapaxes(w, 1, 2), scale=full['pool_scale'][j][None, :])
            y = _pool_fwd(f"pool_fwd_{l}", xin, sc1, sh1, st['w'], st['scale'])
        elif kind == 1:
            wa = full['mla_w_a'][j]
            zeros = jnp.zeros((d, QK_NOPE), BF16)
            w_a = jnp.concatenate([wa[:, :ql + kvl], zeros, wa[:, ql + kvl:], zeros[:, :HEAD_PAD - QK_NOPE - QK_ROPE]], axis=1)
            w_uq = _pad_heads(full['mla_w_uq'][j], n_heads, nope_rope, 1)
            w_kv = jnp.concatenate([_pad_heads(full['mla_w_ukv'][j], n_heads, [(0, QK_NOPE)], 1),
                                    _pad_heads(full['mla_w_ukv'][j], n_heads, [(QK_NOPE, QK_NOPE + V_HEAD)], 1)], axis=1)
            w_o = _pad_heads(full['mla_w_o'][j], n_heads, [(0, V_HEAD)], 0)
            qn, kvn = mla_q_norm[j][None, :], mla_kv_norm[j][None, :]
            a = t3(_mm_nn(f"mla_a_{l}", [(t2(u), _w2(w_a))], F32))
            cq, ckv, kpe = _mla_norm_fwd(f"mla_norm_fwd_{l}", a, qn, kvn, cos_t, sin_t)
            q_raw = t3(_mm_nn(f"mla_q_{l}", [(t2(cq), _w2(w_uq))], F32))
            kv_raw = t3(_mm_nn(f"mla_kv_{l}", [(t2(ckv), _w2(w_kv))], F32))
            qh, kh, vh = _mla_prep_fwd(f"mla_prep_fwd_{l}", q_raw, kv_raw, kpe, cos_t, sin_t, n_heads)
            o, lse = _flash_fwd(f"flash_fwd_{l}", qh, kh, vh, n_heads, sm_scale)
            y = t3(_mm_nn(f"mla_o_{l}", [(t2(o), _w2(w_o))], F32))
            st.update(u=u, w_a=w_a, w_uq=w_uq, w_kv=w_kv, w_o=w_o, qn=qn, kvn=kvn, a=a, cq=cq, ckv=ckv,
                      qh=qh, kh=kh, vh=vh, o=o, lse=lse)
        else:
            w_out, cw = full['sc_w_out'][j], full['sc_conv'][j]
            q = t3(_mm_nn(f"sc_in_{l}", [(t2(u), (w_in_cols, j))], F32))
            r = _shortconv_fwd(f"shortconv_fwd_{l}", q, cw)
            y = t3(_mm_nn(f"sc_out_{l}", [(t2(r), _w2(w_out))], F32))
            st.update(u=u, w_out=w_out, cw=cw, q=q, r=r)
        lng, lnb = full['ln_g'][l], full['ln_b'][l]
        z1, xmid, u2 = _ln_mod_fwd(f"ln_mod_a_{l}", alpha, xin, y, g1, lng[0:1], lnb[0:1], sc2, sh2)
        cwf, cbf = full['ffn_conv'][l], ffn_conv_b[l][None, :]
        p = t3(_mm_nn(f"ffn_up_{l}", [(t2(u2), (w_up_cols, l))], F32))
        act = _convglu_fwd(f"convglu_fwd_{l}", p, cwf, cbf)
        y2 = t3(_mm_nn(f"ffn_down_{l}", [(t2(act), (w_down_rows, l))], F32))
        st.update(y1=y, z1=z1, xmid=xmid, u2=u2, p=p, act=act, y2=y2, cwf=cwf, cbf=cbf, lng=lng, lnb=lnb)
        if l + 1 < depth:
            nsh1, nsc1 = mods[l + 1][0], mods[l + 1][1]
            z2, xin, u = _ln_mod_fwd(f"ln_mod_b_{l}", alpha, xmid, y2, g2, lng[1:2], lnb[1:2], nsc1, nsh1)
        else:
            z2, ct, loss_acc = _ln_loss_fwd("ln_loss", alpha, xmid, y2, g2, lng[1:2], lnb[1:2], loss_target)
        st.update(z2=z2)
        saved.append(st)
    loss = lax.psum(loss_acc[0, 0], ("x", "y", "c"))

    grads = {}
    dmods = [[None] * 6 for _ in range(depth)]
    g_ln_g = [[None, None] for _ in range(depth)]
    g_ln_b = [[None, None] for _ in range(depth)]
    stack = {n: [None] * wts[n].shape[0] for n in ('pool_scale', 'mla_q_norm', 'mla_kv_norm', 'sc_conv', 'ffn_conv',
                                                    'ffn_conv_b')}
    units = {n: [None] * wts[n].shape[0] for n in big_names}
    cargo_a, cargo_b, received = [], [], {}

    def landed(items, got):
        for (n, i, _), r in zip(items, got):
            received[(n, i)] = r

    upstream = (ct,)
    for l in reversed(range(depth)):
        st = saved[l]
        sh1, sc1, g1, sh2, sc2, g2 = mods[l]
        kind, j = l % 3, l // 3
        res = _sub_bwd(f"sub_bwd_b_{l}", alpha, upstream, st['z2'], st['y2'], g2, st['lng'][1:2])
        dz2, dy2, dmods[l][5], g_ln_g[l][1], g_ln_b[l][1] = res[:5]
        if l + 1 < depth:
            dmods[l + 1][1], dmods[l + 1][0] = res[5], res[6]
        dy2f = t2(dy2)
        da = t3(_mm_nn(f"ffn_down_bwd_{l}", [(dy2f, (w_down_t, l))], F32))
        units['ffn_w_down'][l] = _mm_tn(f"ffn_down_dw_{l}", t2(st['act']), [dy2f],
                                        out_dtype=BF16).reshape(N_CHIP, f_hid // N_CHIP, d)
        dpv, dpg, dcw, dcb, got = _convglu_bwd(f"convglu_bwd_{l}", st['p'], da, st['cwf'], st['cbf'],
                                               cargo=[u for _, _, u in cargo_a])
        landed(cargo_a, got)
        stack['ffn_conv'][l], stack['ffn_conv_b'][l] = dcw, dcb[0]
        du2 = t3(_mm_nn(f"ffn_up_bwd_{l}", [(t2(dpv), (w_up_rows, 2 * l)), (t2(dpg), (w_up_rows, 2 * l + 1))], F32))
        res = _mm_tn(f"ffn_up_dw_{l}", t2(st['u2']), [t2(dpv), t2(dpg)], N_CHIP, out_dtype=BF16,
                     cargo=[u for _, _, u in cargo_b])
        if cargo_b:
            landed(cargo_b, res[1])
            res = res[0]
        units['ffn_w_up'][l] = res
        res = _sub_bwd(f"sub_bwd_a_{l}", alpha, (dz2, du2, st['xmid'], sc2), st['z1'], st['y1'], g1, st['lng'][0:1])
        dz1, dy1, dmods[l][2], g_ln_g[l][0], g_ln_b[l][0], dmods[l][4], dmods[l][3] = res
        dy1f = t2(dy1)
        if kind == 0:
            du1, dw, dscale = _pool_bwd(f"pool_bwd_{l}", st['x'], sc1, sh1, dy1, st['w'], st['w_t'], st['scale'])
            stack['pool_scale'][j] = dscale[0]
            grp = dw.shape[1] // N_CHIP
            units['pool_w'][j] = jnp.transpose(dw.reshape(POOL_GROUPS, N_CHIP, grp, dw.shape[2]),
                                               (1, 0, 2, 3)).reshape(N_CHIP, POOL_GROUPS * grp, dw.shape[2])
        elif kind == 1:
            do = t3(_mm_nn(f"mla_o_bwd_{l}", [(dy1f, _w2(jnp.swapaxes(st['w_o'], 0, 1)))], BF16))
            gwo, _ = _unpad_heads(_mm_tn(f"mla_o_dw_{l}", t2(st['o']), [dy1f])[0], n_heads, V_HEAD, 0)
            units['mla_w_o'][j] = jnp.moveaxis(gwo.reshape(d, n_heads * V_HEAD), -1, 0).reshape(N_CHIP, -1, d)
            fa = (st['qh'], st['kh'], st['vh'], st['o'], st['lse'], do, n_heads, sm_scale)
            dq = _flash_dq(f"flash_dq_{l}", *fa)
            dk, dv = _flash_dkv(f"flash_dkv_{l}", *fa)
            dq_raw, dkv_raw, dkpe = _mla_prep_bwd(f"mla_prep_bwd_{l}", dq, dk, dv, cos_t, sin_t, n_heads)
            dq_raw, dkv_raw = t2(dq_raw), t2(dkv_raw)
            dcq = t3(_mm_nn(f"mla_q_bwd_{l}", [(dq_raw, _w2(jnp.swapaxes(st['w_uq'], 0, 1)))], F32))
            dckv = t3(_mm_nn(f"mla_kv_bwd_{l}", [(dkv_raw, _w2(jnp.swapaxes(st['w_kv'], 0, 1)))], F32))
            gq, _ = _unpad_heads(_mm_tn(f"mla_q_dw_{l}", t2(st['cq']), [dq_raw])[0], n_heads, QK_NOPE + QK_ROPE, 1)
            units['mla_w_uq'][j] = _cols_by_chip(gq.reshape(ql, n_heads * (QK_NOPE + QK_ROPE)))
            gkv = _mm_tn(f"mla_kv_dw_{l}", t2(st['ckv']), [dkv_raw])[0]
            gk, _ = _unpad_heads(gkv[:, :n_heads * HEAD_PAD], n_heads, QK_NOPE, 1)
            gv, _ = _unpad_heads(gkv[:, n_heads * HEAD_PAD:], n_heads, V_HEAD, 1)
            units['mla_w_ukv'][j] = _cols_by_chip(
                jnp.concatenate([gk, gv], axis=-1).reshape(kvl, n_heads * (QK_NOPE + V_HEAD)))
            da_, dqn, dkvn = _mla_norm_bwd(f"mla_norm_bwd_{l}", st['a'], dcq, dckv, dkpe, st['qn'], st['kvn'])
            stack['mla_q_norm'][j], stack['mla_kv_norm'][j] = dqn[0], dkvn[0]
            du1 = t3(_mm_nn(f"mla_a_bwd_{l}", [(t2(da_), _w2(jnp.swapaxes(st['w_a'], 0, 1)))], F32))
            gwa = _mm_tn(f"mla_a_dw_{l}", t2(st['u']), [t2(da_)])[0]
            units['mla_w_a'][j] = _cols_by_chip(jnp.concatenate(
                [gwa[:, :ql + kvl], gwa[:, ql + kvl + QK_NOPE:ql + kvl + QK_NOPE + QK_ROPE]], axis=1))
        else:
            dr = t3(_mm_nn(f"sc_out_bwd_{l}", [(dy1f, _w2(jnp.swapaxes(st['w_out'], 0, 1)))], F32))
            units['sc_w_out'][j] = _mm_tn(f"sc_out_dw_{l}", t2(st['r']), [dy1f], out_dtype=BF16).reshape(N_CHIP, -1, d)
            dgb, dgc, dh, dcw = _shortconv_bwd(f"shortconv_bwd_{l}", st['q'], dr, st['cw'])
            stack['sc_conv'][j] = dcw
            parts = [t2(dgb), t2(dgc), t2(dh)]
            du1 = t3(_mm_nn(f"sc_in_bwd_{l}", [(parts[k], (w_in_rows, 3 * j + k)) for k in range(3)], F32))
            units['sc_w_in'][j] = _mm_tn(f"sc_in_dw_{l}", t2(st['u']), parts, N_CHIP, out_dtype=BF16)
        upstream = (dz1, du1, st['x'], sc1)
        mixer = {0: ['pool_w'], 1: ['mla_w_a', 'mla_w_uq', 'mla_w_ukv', 'mla_w_o'], 2: ['sc_w_in', 'sc_w_out']}[kind]
        for n in mixer:
            units[n][j] = units[n][j].astype(BF16)
        cargo_a = [('ffn_w_up', l, units['ffn_w_up'][l])]
        cargo_b = [('ffn_w_down', l, units['ffn_w_down'][l])] + [(n, j, units[n][j]) for n in mixer]
    grad_x, dmods[0][1], dmods[0][0] = _input_bwd("input_bwd", alpha, upstream[0], upstream[1], x, mods[0][1])

    for n, parts in stack.items():
        grads[n] = jnp.stack(parts)
    grads['ln_g'] = jnp.stack([jnp.concatenate(r, axis=0) for r in g_ln_g])
    grads['ln_b'] = jnp.stack([jnp.concatenate(r, axis=0) for r in g_ln_b])
    dmod_mine = jnp.stack([jnp.concatenate([t[:, 0, :] for t in dmods[l]], axis=-1) for l in range(depth)])

    small_grad_names = small_names + ['mla_q_norm', 'mla_kv_norm', 'ffn_conv_b']
    sg_pack, sg_spans = _pack_rows([dmod_mine] + [grads[n] for n in small_grad_names], F32, SUBLANE)
    rows_sg = sg_pack.shape[0]
    sg_all = _all_gather8("gather_small_grads", sg_pack, True).reshape(N_DEV, rows_sg, PACK_COLS)
    dmod_all = sg_all.reshape(N_DEV, -1)[:, :dmod_mine.size].reshape(N_DEV, depth, bsz, 6 * d)
    dmod_all = jnp.transpose(dmod_all, (1, 0, 2, 3)).reshape(depth, N_DEV * bsz, 6 * d)
    sg_sum = _sum8("sum_small_grads", sg_all).reshape(-1)
    for n, (off, shape) in zip(small_grad_names, sg_spans[1:]):
        g_full = sg_sum[off:off + _size(shape)].reshape(shape)
        if n in SMALL_SHARDED:
            ax = SMALL_SHARDED[n]
            width = shape[ax] // N_CHIP
            g_full = lax.dynamic_slice_in_dim(g_full, chip * width, width, axis=ax)
        grads[n] = g_full
    dmod_cols = lax.dynamic_slice_in_dim(dmod_all, chip * n_mod, n_mod, axis=2)
    grads['mod_w'], gb = _mod_bwd("mod_bwd", c_all, dmod_cols, dmod_all)
    grads['mod_b'] = gb[:, 0, :]

    last = cargo_a + cargo_b
    landed(last, _scatter_grads("scatter_big_grads", [u for _, _, u in last]))
    chip_core = jnp.stack([chip, mc]).astype(jnp.int32)
    keys = [(n, i) for n in big_names for i in range(len(units[n]))]
    bufs = _swap_halves("swap_big_grad_halves",
                        [_sum8_into_half(f"sum_big_grads_{n}_{i}", units[n][i], received[(n, i)], chip_core)
                         for n, i in keys])
    for n in big_names:
        grads[n] = jnp.stack([b for (m, _), b in zip(keys, bufs) if m == n]).reshape(wts[n].shape)

    deltas, new_m, new_v = {}, {}, {}
    for n in WEIGHTS:
        deltas[n], new_m[n], new_v[n] = _adamw(f"adamw_{n}", wts[n], grads[n], mom1[n], mom2[n])
    return (loss, grad_x, *[grads[n] for n in WEIGHTS], *[deltas[n] for n in WEIGHTS],
            *[new_m[n] for n in WEIGHTS], *[new_v[n] for n in WEIGHTS])
```

```python
import functools

import jax
import jax.numpy as jnp
from jax import lax
from jax.experimental import pallas as pl
from jax.experimental.pallas import tpu as pltpu

F32 = jnp.float32
BF16 = jnp.bfloat16
MESH = pl.DeviceIdType.MESH

N_DEV = 8
N_CHIP = 4
LANE = 128
SUBLANE = 8
VMEM_LIMIT_BYTES = 56 * 2 ** 20
PACK_COLS = 1024

LN_EPS = 1e-5
RMS_EPS = 1e-6
QK_NOPE, QK_ROPE, V_HEAD = 64, 32, 64
ROPE_THETA = 10000.0
HEAD_PAD = 128
POOL_GROUPS = 4
POOL_HALO = 16
CONV_HALO = 8
CONV_ROWS = 1024
ADAM_LR, ADAM_B1, ADAM_B2, ADAM_EPS, ADAM_WD, ADAM_STEP = 0.001, 0.9, 0.999, 1e-08, 0.01, 10

WEIGHTS = ['mod_w', 'mod_b', 'ln_g', 'ln_b', 'pool_w', 'pool_scale', 'mla_w_a', 'mla_q_norm', 'mla_w_uq',
           'mla_kv_norm', 'mla_w_ukv', 'mla_w_o', 'sc_w_in', 'sc_conv', 'sc_w_out', 'ffn_w_up', 'ffn_conv',
           'ffn_conv_b', 'ffn_w_down']
BIG = {'pool_w': 2, 'mla_w_a': 2, 'mla_w_uq': 2, 'mla_w_ukv': 2, 'mla_w_o': 1, 'sc_w_in': 2, 'sc_w_out': 1,
       'ffn_w_up': 2, 'ffn_w_down': 1}
SMALL_SHARDED = {'ln_g': 2, 'ln_b': 2, 'pool_scale': 1, 'sc_conv': 2, 'ffn_conv': 2}
REPLICATED = ['mod_b', 'mla_q_norm', 'mla_kv_norm', 'ffn_conv_b']


def _pc(body, **kw):
    return pl.pallas_call(body, **kw)


def _cp(*sem):
    return pltpu.CompilerParams(dimension_semantics=sem, vmem_limit_bytes=VMEM_LIMIT_BYTES)


def _div(n, cap, mult):
    best = None
    for d in range(mult, min(n, cap) + 1, mult):
        if n % d == 0:
            best = d
    return best if best is not None else n


def _sds(shape, dtype):
    return jax.ShapeDtypeStruct(tuple(shape), dtype)


def _flip(v, bit):
    return 1 - v if bit else v


def _all_gather8(name, x_shard, in_vmem):
    m_per, n = x_shard.shape
    space = pltpu.VMEM if in_vmem else pltpu.HBM

    def body(x_ref, out_ref, send_sems, recv_sems, local_sem):
        x, y, c = lax.axis_index("x"), lax.axis_index("y"), lax.axis_index("c")
        me, sibling = (x, y, c), (x, y, 1 - c)
        chips = [(1 - x, y), (x, 1 - y), (1 - x, 1 - y)]

        def rows(px, py, pc_):
            return out_ref.at[pl.ds((4 * px + 2 * py + pc_) * m_per, m_per), :]

        def copy(k, block, to, src=None):
            return pltpu.make_async_remote_copy(
                src_ref=rows(*block) if src is None else src, dst_ref=rows(*block),
                send_sem=send_sems.at[k], recv_sem=recv_sems.at[k], device_id=to, device_id_type=MESH)

        mine = pltpu.make_async_copy(x_ref, rows(*me), local_sem)
        mine.start()
        first = [copy(0, me, sibling, src=x_ref)]
        first += [copy(1 + j, me, (*chip, c), src=x_ref) for j, chip in enumerate(chips)]
        for cp in first:
            cp.start()
        passed = [copy(4 + j, (*chip, c), sibling) for j, chip in enumerate(chips)]
        for j, chip in enumerate(chips):
            copy(1 + j, (*chip, c), me).wait_recv()
            passed[j].start()
        copy(0, sibling, me).wait_recv()
        for j, chip in enumerate(chips):
            copy(4 + j, (*chip, 1 - c), me).wait_recv()
        for cp in first + passed:
            cp.wait_send()
        mine.wait()

    return _pc(
        body, name=name, out_shape=_sds((N_DEV * m_per, n), x_shard.dtype),
        in_specs=[pl.BlockSpec(memory_space=space)], out_specs=pl.BlockSpec(memory_space=space),
        scratch_shapes=[pltpu.SemaphoreType.DMA((7,)), pltpu.SemaphoreType.DMA((7,)), pltpu.SemaphoreType.DMA],
        compiler_params=pltpu.CompilerParams(vmem_limit_bytes=VMEM_LIMIT_BYTES),
    )(x_shard)


def _gather_weights(name, shards):
    n_t = len(shards)
    halves = [s.shape[0] // 2 for s in shards]

    def body(*refs):
        x_refs, o_refs = refs[:n_t], refs[n_t:2 * n_t]
        send_sems, recv_sems, local_sems = refs[2 * n_t:]
        x, y, c = lax.axis_index("x"), lax.axis_index("y"), lax.axis_index("c")
        me, sibling = (x, y, c), (x, y, 1 - c)
        chips = [(1 - x, y), (x, 1 - y), (1 - x, 1 - y)]

        def slot(t, px, py, pc_):
            return o_refs[t].at[4 * px + 2 * py + pc_]

        def my_rows(t):
            return x_refs[t].at[pl.ds(c * halves[t], halves[t]), :]

        def copy(t, k, block, to, src=None):
            return pltpu.make_async_remote_copy(
                src_ref=slot(t, *block) if src is None else src, dst_ref=slot(t, *block),
                send_sem=send_sems.at[t, k], recv_sem=recv_sems.at[t, k], device_id=to, device_id_type=MESH)

        local = [pltpu.make_async_copy(my_rows(t), slot(t, *me), local_sems.at[t]) for t in range(n_t)]
        for cp in local:
            cp.start()
        first = []
        for t in range(n_t):
            first += [copy(t, 1 + j, me, (*chip, c), src=my_rows(t)) for j, chip in enumerate(chips)]
            first.append(copy(t, 0, me, sibling, src=my_rows(t)))
        for cp in first:
            cp.start()
        passed = []
        for j, chip in enumerate(chips):
            for t in range(n_t):
                copy(t, 1 + j, (*chip, c), me).wait_recv()
                passed.append(copy(t, 4 + j, (*chip, c), sibling))
                passed[-1].start()
        for t in range(n_t):
            copy(t, 0, sibling, me).wait_recv()
        for j, chip in enumerate(chips):
            for t in range(n_t):
                copy(t, 4 + j, (*chip, 1 - c), me).wait_recv()
        for cp in first + passed:
            cp.wait_send()
        for cp in local:
            cp.wait()

    hbm = pl.BlockSpec(memory_space=pltpu.HBM)
    return _pc(
        body, name=name, out_shape=tuple(_sds((N_DEV, h, s.shape[1]), s.dtype) for h, s in zip(halves, shards)),
        in_specs=[hbm] * n_t, out_specs=(hbm,) * n_t,
        scratch_shapes=[pltpu.SemaphoreType.DMA((n_t, 7)), pltpu.SemaphoreType.DMA((n_t, 7)),
                        pltpu.SemaphoreType.DMA((n_t,))],
    )(*shards)


def _scatter_copies(u_refs, r_refs, send_sems, recv_sems):
    x, y, c = lax.axis_index("x"), lax.axis_index("y"), lax.axis_index("c")
    copies = []
    for k in range(1, N_DEV):
        px, py, pcc = _flip(x, (k >> 2) & 1), _flip(y, (k >> 1) & 1), _flip(c, k & 1)
        for t, (u_ref, r_ref) in enumerate(zip(u_refs, r_refs)):
            h = u_ref.shape[1] // 2
            copies.append(pltpu.make_async_remote_copy(
                src_ref=u_ref.at[2 * px + py, pl.ds(pcc * h, h), :], dst_ref=r_ref.at[k - 1],
                send_sem=send_sems.at[t, k - 1], recv_sem=recv_sems.at[t, k - 1],
                device_id=(px, py, pcc), device_id_type=MESH))
    return copies


def _scatter_shapes(units):
    return tuple(_sds((N_DEV - 1, u.shape[1] // 2, u.shape[2]), u.dtype) for u in units)


def _scatter_grads(name, units):
    n_u = len(units)

    def body(*refs):
        copies = _scatter_copies(refs[:n_u], refs[n_u:2 * n_u], refs[2 * n_u], refs[2 * n_u + 1])
        for cp in copies:
            cp.start()
        for cp in copies:
            cp.wait()

    hbm = pl.BlockSpec(memory_space=pltpu.HBM)
    return _pc(body, name=name, out_shape=_scatter_shapes(units), in_specs=[hbm] * n_u, out_specs=(hbm,) * n_u,
               scratch_shapes=[pltpu.SemaphoreType.DMA((n_u, 7)), pltpu.SemaphoreType.DMA((n_u, 7))])(*units)


def _pc_cargo(body, cargo, *, name, grid, in_specs, out_specs, out_shape, scratch_shapes=()):
    out_specs, out_shape = tuple(out_specs), tuple(out_shape)
    if not cargo:
        return lambda *args: (_pc(body, name=name, grid=grid, in_specs=list(in_specs), out_specs=out_specs,
                                  out_shape=out_shape, scratch_shapes=list(scratch_shapes),
                                  compiler_params=_cp(*["arbitrary"] * len(grid)))(*args), ())
    n_in, n_out, n_u, n_s = len(in_specs), len(out_specs), len(cargo), len(scratch_shapes)

    def wrapped(*refs):
        ins, u_refs = refs[:n_in], refs[n_in:n_in + n_u]
        outs = refs[n_in + n_u:n_in + n_u + n_out]
        r_refs = refs[n_in + n_u + n_out:n_in + 2 * n_u + n_out]
        scratch = refs[n_in + 2 * n_u + n_out:n_in + 2 * n_u + n_out + n_s]
        send_sems, recv_sems = refs[-2:]
        first = last = None
        for axis, extent in enumerate(grid):
            at_start, at_end = pl.program_id(axis) == 0, pl.program_id(axis) == extent - 1
            first = at_start if first is None else first & at_start
            last = at_end if last is None else last & at_end

        @pl.when(first)
        def _():
            for cp in _scatter_copies(u_refs, r_refs, send_sems, recv_sems):
                cp.start()

        body(*ins, *outs, *scratch)

        @pl.when(last)
        def _():
            for cp in _scatter_copies(u_refs, r_refs, send_sems, recv_sems):
                cp.wait()

    hbm = pl.BlockSpec(memory_space=pltpu.HBM)
    call = _pc(wrapped, name=name, grid=grid, in_specs=list(in_specs) + [hbm] * n_u, out_specs=out_specs + (hbm,) * n_u,
               out_shape=out_shape + _scatter_shapes(cargo),
               scratch_shapes=list(scratch_shapes) + [pltpu.SemaphoreType.DMA((n_u, 7)), pltpu.SemaphoreType.DMA((n_u, 7))],
               compiler_params=_cp(*["arbitrary"] * len(grid)))

    def run(*args):
        res = call(*args, *cargo)
        return tuple(res[:n_out]), tuple(res[n_out:])
    return run


def _swap_halves(name, bufs):
    n_u = len(bufs)

    def body(*refs):
        o_refs = refs[n_u:2 * n_u]
        send_sems, recv_sems = refs[2 * n_u:]
        x, y, c = lax.axis_index("x"), lax.axis_index("y"), lax.axis_index("c")

        def rows(u, core):
            h = bufs[u].shape[0] // 2
            return o_refs[u].at[pl.ds(core * h, h), :]

        sends = [pltpu.make_async_remote_copy(src_ref=rows(u, c), dst_ref=rows(u, c), send_sem=send_sems.at[u],
                                              recv_sem=recv_sems.at[u], device_id=(x, y, 1 - c), device_id_type=MESH)
                 for u in range(n_u)]
        recvs = [pltpu.make_async_remote_copy(src_ref=rows(u, c), dst_ref=rows(u, 1 - c), send_sem=send_sems.at[u],
                                              recv_sem=recv_sems.at[u], device_id=(x, y, 1 - c), device_id_type=MESH)
                 for u in range(n_u)]
        for cp in sends:
            cp.start()
        for cp in recvs:
            cp.wait_recv()
        for cp in sends:
            cp.wait_send()

    hbm = pl.BlockSpec(memory_space=pltpu.HBM)
    return _pc(
        body, name=name, out_shape=tuple(_sds(b.shape, b.dtype) for b in bufs), in_specs=[hbm] * n_u,
        out_specs=(hbm,) * n_u, input_output_aliases={u: u for u in range(n_u)},
        scratch_shapes=[pltpu.SemaphoreType.DMA((n_u,)), pltpu.SemaphoreType.DMA((n_u,))],
    )(*bufs)


def _sum8_into_half(name, unit, received, chip_core):
    _, h, n = received.shape
    tm = _div(h, 256, 16)
    per = h // tm

    def body(cc_ref, u_ref, p_ref, o_ref):
        acc = u_ref[0].astype(F32)
        for s in range(N_DEV - 1):
            acc = acc + p_ref[s].astype(F32)
        o_ref[...] = acc

    grid_spec = pltpu.PrefetchScalarGridSpec(
        num_scalar_prefetch=1, grid=(per,),
        in_specs=[pl.BlockSpec((1, tm, n), lambda i, cc_ref: (cc_ref[0], cc_ref[1] * per + i, 0)),
                  pl.BlockSpec((N_DEV - 1, tm, n), lambda i, cc_ref: (0, i, 0))],
        out_specs=pl.BlockSpec((tm, n), lambda i, cc_ref: (cc_ref[1] * per + i, 0)))
    return _pc(body, name=name, grid_spec=grid_spec, out_shape=_sds((2 * h, n), F32),
               compiler_params=_cp("arbitrary"))(chip_core, unit, received)


def _sum8(name, parts):
    _, m, n = parts.shape
    tm = _div(m, 256, SUBLANE)

    def body(p_ref, o_ref):
        acc = p_ref[0]
        for s in range(1, N_DEV):
            acc = acc + p_ref[s]
        o_ref[...] = acc

    return _pc(body, name=name, grid=(m // tm,), out_shape=_sds((m, n), F32),
               in_specs=[pl.BlockSpec((N_DEV, tm, n), lambda i: (0, i, 0))],
               out_specs=pl.BlockSpec((tm, n), lambda i: (i, 0)), compiler_params=_cp("parallel"))(parts)


def _pack_rows(arrays, dtype, row_mult):
    flat, spans, off = [], [], 0
    for a in arrays:
        flat.append(a.reshape(-1).astype(dtype))
        spans.append((off, a.shape))
        off += a.size
    quantum = row_mult * PACK_COLS
    total = -(-off // quantum) * quantum
    if total > off:
        flat.append(jnp.zeros((total - off,), dtype))
    return jnp.concatenate(flat).reshape(total // PACK_COLS, PACK_COLS), spans


def _size(shape):
    n = 1
    for s in shape:
        n *= s
    return n


def _join_chips(blocks, axis):
    return jnp.concatenate([blocks[j] for j in range(N_CHIP)], axis=axis)


def _cols_by_chip(g):
    k, n = g.shape
    return jnp.transpose(g.reshape(k, N_CHIP, n // N_CHIP), (1, 0, 2))


def _mm_nn(name, pairs, out_dtype, cargo=(), tm_cap=1024, tn_cap=1536):
    m = pairs[0][0].shape[0]
    nb, _, n4 = pairs[0][1][0].shape
    tm, tn = _div(m, tm_cap, 16), _div(n4, tn_cap, LANE)
    per = n4 // tn
    n_pairs = len(pairs)

    def body(*refs):
        o_ref = refs[-1]
        acc = jnp.dot(refs[0][...], refs[1][0], preferred_element_type=F32)
        for i in range(1, n_pairs):
            acc = acc + jnp.dot(refs[2 * i][...], refs[2 * i + 1][0], preferred_element_type=F32)
        o_ref[...] = acc.astype(o_ref.dtype)

    in_specs, args = [], []
    for a, (w, r) in pairs:
        k = a.shape[1]
        assert w.shape[0] == nb and w.shape[2] == n4 and w.shape[1] % k == 0
        in_specs += [pl.BlockSpec((tm, k), lambda j, i: (i, 0)),
                     pl.BlockSpec((1, k, tn), functools.partial(lambda j, i, r_: (j // per, r_, j % per), r_=r))]
        args += [a, w]
    if cargo:
        (out,), received = _pc_cargo(body, cargo, name=name, grid=(nb * per, m // tm), in_specs=in_specs,
                                     out_shape=[_sds((m, nb * n4), out_dtype)],
                                     out_specs=[pl.BlockSpec((tm, tn), lambda j, i: (i, j))])(*args)
        return out, received
    return _pc(body, name=name, grid=(nb * per, m // tm), out_shape=_sds((m, nb * n4), out_dtype), in_specs=in_specs,
               out_specs=pl.BlockSpec((tm, tn), lambda j, i: (i, j)), compiler_params=_cp("parallel", "parallel"))(*args)


def _mm_tn(name, x, ys, n_blocks=1, out_dtype=F32, cargo=(), tt_cap=512):
    t, k = x.shape
    widths = [y.shape[1] for y in ys]
    n4 = sum(widths) // n_blocks
    common = n4
    for w in widths:
        common = _gcd(common, w)
    tk, tn, tt = _div(k, 1536, LANE), _div(common, 1536, LANE), _div(t, tt_cap, 16)
    per = n4 // tn
    starts, acc_w = [], 0
    for w in widths:
        starts.append(acc_w // tn)
        acc_w += w
    counts = [w // tn for w in widths]
    n_y = len(ys)

    def active(i, j):
        return (j >= starts[i]) & (j < starts[i] + counts[i])

    n_t = t // tt

    def body(*refs):
        x_ref, y_refs, o_ref, acc_ref = refs[0], refs[1:1 + n_y], refs[-2], refs[-1]
        j = pl.program_id(1)

        @pl.when(pl.program_id(2) == 0)
        def _():
            acc_ref[...] = jnp.zeros_like(acc_ref)

        for i in range(n_y):
            @pl.when(active(i, j))
            def _():
                acc_ref[...] += lax.dot_general(x_ref[...], y_refs[i][...], (((0,), (0,)), ((), ())),
                                                preferred_element_type=F32)

        @pl.when(pl.program_id(2) == n_t - 1)
        def _():
            o_ref[0] = acc_ref[...].astype(o_ref.dtype)

    def y_spec(i):
        def index(a, j, s):
            on = active(i, j)
            return jnp.where(on, s, 0), jnp.where(on, j - starts[i], 0)
        return pl.BlockSpec((tt, tn), index)

    (out,), received = _pc_cargo(
        body, cargo, name=name, grid=(k // tk, n_blocks * per, n_t), out_shape=[_sds((n_blocks, k, n4), out_dtype)],
        in_specs=[pl.BlockSpec((tt, tk), lambda a, j, s: (s, a))] + [y_spec(i) for i in range(n_y)],
        out_specs=[pl.BlockSpec((1, tk, tn), lambda a, j, s: (j // per, a, j % per))],
        scratch_shapes=[pltpu.VMEM((tk, tn), F32)])(x, *ys)
    return (out, received) if cargo else out


def _gcd(a, b):
    while b:
        a, b = b, a % b
    return a


def _w2(w):
    return (w[None], 0)


def _tok_spec(ts, d):
    return pl.BlockSpec((1, ts, d), lambda b, i: (b, i, 0))


def _seq_spec(d):
    return pl.BlockSpec((1, 1, d), lambda b, i: (b, 0, 0))


def _vec_spec(d):
    return pl.BlockSpec((1, d), lambda b, i: (0, 0))


def _ln_stats(z):
    mu = jnp.mean(z, axis=-1, keepdims=True)
    zc = z - mu
    var = jnp.mean(zc * zc, axis=-1, keepdims=True)
    rstd = lax.rsqrt(var + LN_EPS)
    return zc * rstd, rstd


def _modulate(name, x, sc, sh):
    b, s, d = x.shape
    ts = _div(s, 512, 16)

    def body(x_ref, sc_ref, sh_ref, u_ref):
        u_ref[0] = (x_ref[0] * (1.0 + sc_ref[0]) + sh_ref[0]).astype(BF16)

    return _pc(body, name=name, grid=(b, s // ts), out_shape=_sds(x.shape, BF16),
               in_specs=[_tok_spec(ts, d), _seq_spec(d), _seq_spec(d)], out_specs=_tok_spec(ts, d),
               compiler_params=_cp("parallel", "parallel"))(x, sc, sh)


def _ln_mod_fwd(name, alpha, x, y, g, lng, lnb, sc, sh):
    b, s, d = x.shape
    ts = _div(s, 512, 16)

    def body(x_ref, y_ref, g_ref, lng_ref, lnb_ref, sc_ref, sh_ref, z_ref, xn_ref, u_ref):
        z = alpha * x_ref[0] + (1.0 + g_ref[0]) * y_ref[0]
        xhat, _ = _ln_stats(z)
        xn = xhat * lng_ref[...] + lnb_ref[...]
        z_ref[0] = z
        xn_ref[0] = xn
        u_ref[0] = (xn * (1.0 + sc_ref[0]) + sh_ref[0]).astype(BF16)

    tok, seq, vec = _tok_spec(ts, d), _seq_spec(d), _vec_spec(d)
    return _pc(body, name=name, grid=(b, s // ts),
               out_shape=(_sds(x.shape, F32), _sds(x.shape, F32), _sds(x.shape, BF16)),
               in_specs=[tok, tok, seq, vec, vec, seq, seq], out_specs=(tok, tok, tok),
               compiler_params=_cp("parallel", "parallel"))(x, y, g, lng, lnb, sc, sh)


def _ln_loss_fwd(name, alpha, x, y, g, lng, lnb, target):
    b, s, d = x.shape
    ts = _div(s, 512, 16)

    def body(x_ref, y_ref, g_ref, lng_ref, lnb_ref, t_ref, z_ref, ct_ref, loss_ref):
        @pl.when((pl.program_id(0) == 0) & (pl.program_id(1) == 0))
        def _():
            loss_ref[...] = jnp.zeros_like(loss_ref)
        z = alpha * x_ref[0] + (1.0 + g_ref[0]) * y_ref[0]
        xhat, _ = _ln_stats(z)
        err = xhat * lng_ref[...] + lnb_ref[...] - t_ref[0]
        z_ref[0] = z
        ct_ref[0] = err / d
        part = 0.5 * jnp.sum(jnp.mean(err * err, axis=-1, keepdims=True))
        loss_ref[...] += jnp.full(loss_ref.shape, part, F32)

    tok, seq, vec = _tok_spec(ts, d), _seq_spec(d), _vec_spec(d)
    return _pc(body, name=name, grid=(b, s // ts),
               out_shape=(_sds(x.shape, F32), _sds(x.shape, F32), _sds((SUBLANE, LANE), F32)),
               in_specs=[tok, tok, seq, vec, vec, tok],
               out_specs=(tok, tok, pl.BlockSpec((SUBLANE, LANE), lambda b, i: (0, 0))),
               compiler_params=_cp("arbitrary", "arbitrary"))(x, y, g, lng, lnb, target)


def _sub_bwd(name, alpha, upstream, z, y, g, lng):
    b, s, d = z.shape
    ts = _div(s, 512, 16)
    last = len(upstream) == 1

    def body(*refs):
        if last:
            ct_ref, z_ref, y_ref, g_ref, lng_ref, dz_ref, dy_ref, dg_ref, dlng_ref, dlnb_ref = refs
        else:
            (dzn_ref, dun_ref, xn_ref, scn_ref, z_ref, y_ref, g_ref, lng_ref,
             dz_ref, dy_ref, dg_ref, dlng_ref, dlnb_ref, dsc_ref, dsh_ref) = refs
        first_tile = pl.program_id(1) == 0

        @pl.when(first_tile & (pl.program_id(0) == 0))
        def _():
            dlng_ref[...] = jnp.zeros_like(dlng_ref)
            dlnb_ref[...] = jnp.zeros_like(dlnb_ref)

        @pl.when(first_tile)
        def _():
            dg_ref[...] = jnp.zeros_like(dg_ref)
            if not last:
                dsc_ref[...] = jnp.zeros_like(dsc_ref)
                dsh_ref[...] = jnp.zeros_like(dsh_ref)

        if last:
            ct = ct_ref[0]
        else:
            dun = dun_ref[0]
            ct = alpha * dzn_ref[0] + dun * (1.0 + scn_ref[0])
            dsc_ref[0] += jnp.sum(dun * xn_ref[0], axis=0, keepdims=True)
            dsh_ref[0] += jnp.sum(dun, axis=0, keepdims=True)
        xhat, rstd = _ln_stats(z_ref[0])
        dlng_ref[...] += jnp.sum(ct * xhat, axis=0, keepdims=True)
        dlnb_ref[...] += jnp.sum(ct, axis=0, keepdims=True)
        dxhat = ct * lng_ref[...]
        dz = rstd * (dxhat - jnp.mean(dxhat, axis=-1, keepdims=True)
                     - xhat * jnp.mean(dxhat * xhat, axis=-1, keepdims=True))
        dz_ref[0] = dz
        dy_ref[0] = ((1.0 + g_ref[0]) * dz).astype(BF16)
        dg_ref[0] += jnp.sum(dz * y_ref[0], axis=0, keepdims=True)

    tok, seq, vec = _tok_spec(ts, d), _seq_spec(d), _vec_spec(d)
    seq_out = _sds((b, 1, d), F32)
    out_shape = [_sds(z.shape, F32), _sds(z.shape, BF16), seq_out, _sds((1, d), F32), _sds((1, d), F32)]
    out_specs = [tok, tok, seq, vec, vec]
    if last:
        in_specs = [tok, tok, tok, seq, vec]
    else:
        in_specs = [tok, tok, tok, seq, tok, tok, seq, vec]
        out_shape += [seq_out, seq_out]
        out_specs += [seq, seq]
    return _pc(body, name=name, grid=(b, s // ts), out_shape=tuple(out_shape), in_specs=in_specs,
               out_specs=tuple(out_specs), compiler_params=_cp("arbitrary", "arbitrary"))(*upstream, z, y, g, lng)


def _input_bwd(name, alpha, dz, du, x, sc):
    b, s, d = x.shape
    ts = _div(s, 512, 16)

    def body(dz_ref, du_ref, x_ref, sc_ref, gx_ref, dsc_ref, dsh_ref):
        @pl.when(pl.program_id(1) == 0)
        def _():
            dsc_ref[...] = jnp.zeros_like(dsc_ref)
            dsh_ref[...] = jnp.zeros_like(dsh_ref)
        du_ = du_ref[0]
        gx_ref[0] = alpha * dz_ref[0] + du_ * (1.0 + sc_ref[0])
        dsc_ref[0] += jnp.sum(du_ * x_ref[0], axis=0, keepdims=True)
        dsh_ref[0] += jnp.sum(du_, axis=0, keepdims=True)

    tok, seq = _tok_spec(ts, d), _seq_spec(d)
    seq_out = _sds((b, 1, d), F32)
    return _pc(body, name=name, grid=(b, s // ts), out_shape=(_sds(x.shape, F32), seq_out, seq_out),
               in_specs=[tok, tok, tok, seq], out_specs=(tok, seq, seq),
               compiler_params=_cp("parallel", "arbitrary"))(dz, du, x, sc)


def _rows_iota(shape):
    return lax.broadcasted_iota(jnp.int32, shape, 0)


def _back(v, k):
    return pltpu.roll(v, k, axis=0)


def _ahead(v, k):
    return pltpu.roll(v, v.shape[0] - k, axis=0)


def _conv3(ext, w_ref):
    return w_ref[2:3, :] * ext + w_ref[1:2, :] * _back(ext, 1) + w_ref[0:1, :] * _back(ext, 2)


def _conv3_t(dh_ext, w_ref):
    return w_ref[2:3, :] * dh_ext + w_ref[1:2, :] * _ahead(dh_ext, 1) + w_ref[0:1, :] * _ahead(dh_ext, 2)


def _flag(cond):
    return jnp.where(cond, 1.0, 0.0).astype(F32)


def _sigmoid(v):
    return 1.0 / (1.0 + jnp.exp(-v))


def _halo_specs(ts, tc, halo, n_s, col):
    per = ts // halo
    tile = pl.BlockSpec((1, ts, tc), lambda b, i, j: (b, i, col(j)))
    prev = pl.BlockSpec((1, halo, tc), lambda b, i, j: (b, jnp.maximum(i * per - 1, 0), col(j)))
    nxt = pl.BlockSpec((1, halo, tc), lambda b, i, j: (b, jnp.minimum((i + 1) * per, n_s * per - 1), col(j)))
    return tile, prev, nxt


def _convglu_fwd(name, p, cw, cb):
    b, s, f2 = p.shape
    f = f2 // 2
    ts, tc = _div(s, CONV_ROWS, CONV_HALO), _div(f, 256, LANE)
    n_s, n_c = s // ts, f // tc

    def body(pv_ref, pvh_ref, pg_ref, pgh_ref, wv_ref, wg_ref, bv_ref, bg_ref, a_ref):
        keep = _flag(pl.program_id(1) > 0)

        def conv(t_ref, h_ref, w_ref, b_ref):
            ext = jnp.concatenate([h_ref[0] * keep, t_ref[0]], axis=0)
            return _conv3(ext, w_ref)[CONV_HALO:] + b_ref[...]

        val = conv(pv_ref, pvh_ref, wv_ref, bv_ref)
        gate = conv(pg_ref, pgh_ref, wg_ref, bg_ref)
        a_ref[0] = (gate * _sigmoid(gate) * val).astype(BF16)

    tv, hv, _ = _halo_specs(ts, tc, CONV_HALO, n_s, lambda j: j)
    tg, hg, _ = _halo_specs(ts, tc, CONV_HALO, n_s, lambda j: j + n_c)
    wv = pl.BlockSpec((3, tc), lambda b, i, j: (0, j))
    wg = pl.BlockSpec((3, tc), lambda b, i, j: (0, j + n_c))
    bv = pl.BlockSpec((1, tc), lambda b, i, j: (0, j))
    bg = pl.BlockSpec((1, tc), lambda b, i, j: (0, j + n_c))
    return _pc(body, name=name, grid=(b, n_s, n_c), out_shape=_sds((b, s, f), BF16),
               in_specs=[tv, hv, tg, hg, wv, wg, bv, bg], out_specs=pl.BlockSpec((1, ts, tc), lambda b, i, j: (b, i, j)),
               compiler_params=_cp("parallel", "parallel", "parallel"))(p, p, p, p, cw, cw, cb, cb)


def _convglu_bwd(name, p, da, cw, cb, cargo=()):
    b, s, f2 = p.shape
    f = f2 // 2
    ts, tc = _div(s, CONV_ROWS, CONV_HALO), _div(f, 256, LANE)
    n_s, n_c = s // ts, f // tc

    def body(pv_ref, pvp_ref, pvn_ref, pg_ref, pgp_ref, pgn_ref, da_ref, dan_ref, wv_ref, wg_ref, bv_ref, bg_ref,
             dpv_ref, dpg_ref, dwv_ref, dwg_ref, dbv_ref, dbg_ref):
        bi, i = pl.program_id(1), pl.program_id(2)

        @pl.when((bi == 0) & (i == 0))
        def _():
            for r in (dwv_ref, dwg_ref, dbv_ref, dbg_ref):
                r[...] = jnp.zeros_like(r)

        keep_prev = _flag(i > 0)
        keep_next = _flag(i < n_s - 1)
        pv_ext = jnp.concatenate([pvp_ref[0] * keep_prev, pv_ref[0], pvn_ref[0]], axis=0)
        pg_ext = jnp.concatenate([pgp_ref[0] * keep_prev, pg_ref[0], pgn_ref[0]], axis=0)
        taps_v = (_back(pv_ext, 2), _back(pv_ext, 1), pv_ext)
        taps_g = (_back(pg_ext, 2), _back(pg_ext, 1), pg_ext)

        def conv(taps, w_ref, b_ref):
            return (w_ref[2:3, :] * taps[2] + w_ref[1:2, :] * taps[1] + w_ref[0:1, :] * taps[0])[CONV_HALO:] + b_ref[...]

        val, gate = conv(taps_v, wv_ref, bv_ref), conv(taps_g, wg_ref, bg_ref)
        da_ext = jnp.concatenate([da_ref[0], dan_ref[0] * keep_next], axis=0)
        sg = _sigmoid(gate)
        dval = da_ext * gate * sg
        dgate = da_ext * val * (sg * (1.0 + gate * (1.0 - sg)))
        dpv_ref[0] = _conv3_t(dval, wv_ref)[:ts].astype(BF16)
        dpg_ref[0] = _conv3_t(dgate, wg_ref)[:ts].astype(BF16)
        for dh, taps, dw_ref, db_ref in ((dval[:ts], taps_v, dwv_ref, dbv_ref), (dgate[:ts], taps_g, dwg_ref, dbg_ref)):
            db_ref[...] += jnp.sum(dh, axis=0, keepdims=True)
            for k in range(3):
                dw_ref[k:k + 1, :] += jnp.sum(dh * taps[k][CONV_HALO:CONV_HALO + ts], axis=0, keepdims=True)

    def specs(col):
        per = ts // CONV_HALO
        tile = pl.BlockSpec((1, ts, tc), lambda j, b, i: (b, i, col(j)))
        prev = pl.BlockSpec((1, CONV_HALO, tc), lambda j, b, i: (b, jnp.maximum(i * per - 1, 0), col(j)))
        nxt = pl.BlockSpec((1, CONV_HALO, tc), lambda j, b, i: (b, jnp.minimum((i + 1) * per, n_s * per - 1), col(j)))
        return tile, prev, nxt

    tv, pvp, pvn = specs(lambda j: j)
    tg, pgp, pgn = specs(lambda j: j + n_c)
    wv = pl.BlockSpec((3, tc), lambda j, b, i: (0, j))
    wg = pl.BlockSpec((3, tc), lambda j, b, i: (0, j + n_c))
    bv = pl.BlockSpec((1, tc), lambda j, b, i: (0, j))
    bg = pl.BlockSpec((1, tc), lambda j, b, i: (0, j + n_c))
    out_tile = pl.BlockSpec((1, ts, tc), lambda j, b, i: (b, i, j))
    acc3, acc1 = pl.BlockSpec((3, tc), lambda j, b, i: (0, j)), pl.BlockSpec((1, tc), lambda j, b, i: (0, j))
    (dpv, dpg, dwv, dwg, dbv, dbg), received = _pc_cargo(
        body, cargo, name=name, grid=(n_c, b, n_s),
        out_shape=(_sds((b, s, f), BF16), _sds((b, s, f), BF16), _sds((3, f), F32), _sds((3, f), F32),
                   _sds((1, f), F32), _sds((1, f), F32)),
        in_specs=[tv, pvp, pvn, tg, pgp, pgn, tv, pvn, wv, wg, bv, bg],
        out_specs=(out_tile, out_tile, acc3, acc3, acc1, acc1))(p, p, p, p, p, p, da, da, cw, cw, cb, cb)
    return dpv, dpg, jnp.concatenate([dwv, dwg], axis=1), jnp.concatenate([dbv, dbg], axis=1), received


def _shortconv_fwd(name, q, cw):
    b, s, d3 = q.shape
    d = d3 // 3
    ts, tc = _div(s, CONV_ROWS, CONV_HALO), _div(d, 256, LANE)
    n_s, n_c = s // ts, d // tc

    def body(gb_ref, gc_ref, gch_ref, h_ref, hh_ref, w_ref, r_ref):
        keep = _flag(pl.program_id(1) > 0)
        m_ext = jnp.concatenate([gch_ref[0] * hh_ref[0] * keep, gc_ref[0] * h_ref[0]], axis=0)
        r_ref[0] = (gb_ref[0] * _conv3(m_ext, w_ref)[CONV_HALO:]).astype(BF16)

    tb, _, _ = _halo_specs(ts, tc, CONV_HALO, n_s, lambda j: j)
    tcc, hc, _ = _halo_specs(ts, tc, CONV_HALO, n_s, lambda j: j + n_c)
    th, hh, _ = _halo_specs(ts, tc, CONV_HALO, n_s, lambda j: j + 2 * n_c)
    w = pl.BlockSpec((3, tc), lambda b, i, j: (0, j))
    return _pc(body, name=name, grid=(b, n_s, n_c), out_shape=_sds((b, s, d), BF16),
               in_specs=[tb, tcc, hc, th, hh, w], out_specs=pl.BlockSpec((1, ts, tc), lambda b, i, j: (b, i, j)),
               compiler_params=_cp("parallel", "parallel", "parallel"))(q, q, q, q, q, cw)


def _shortconv_bwd(name, q, dr, cw):
    b, s, d3 = q.shape
    d = d3 // 3
    ts, tc = _div(s, CONV_ROWS, CONV_HALO), _div(d, 256, LANE)
    n_s, n_c = s // ts, d // tc

    def body(gb_ref, gbn_ref, gc_ref, gcp_ref, h_ref, hp_ref, dr_ref, drn_ref, w_ref,
             dgb_ref, dgc_ref, dh_ref, dw_ref):
        bi, i = pl.program_id(1), pl.program_id(2)

        @pl.when((bi == 0) & (i == 0))
        def _():
            dw_ref[...] = jnp.zeros_like(dw_ref)

        keep_prev = _flag(i > 0)
        keep_next = _flag(i < n_s - 1)
        gc, h = gc_ref[0], h_ref[0]
        m_ext = jnp.concatenate([gcp_ref[0] * hp_ref[0] * keep_prev, gc * h], axis=0)
        cm = _conv3(m_ext, w_ref)[CONV_HALO:]
        dr_ = dr_ref[0]
        dgb_ref[0] = (dr_ * cm).astype(BF16)
        dcv_ext = jnp.concatenate([dr_ * gb_ref[0], drn_ref[0] * gbn_ref[0] * keep_next], axis=0)
        dm = _conv3_t(dcv_ext, w_ref)[:ts]
        dgc_ref[0] = (dm * h).astype(BF16)
        dh_ref[0] = (dm * gc).astype(BF16)
        dcv = dcv_ext[:ts]
        for k in range(3):
            shifted = m_ext if k == 2 else _back(m_ext, 2 - k)
            dw_ref[k:k + 1, :] += jnp.sum(dcv * shifted[CONV_HALO:], axis=0, keepdims=True)

    def specs(col):
        per = ts // CONV_HALO
        tile = pl.BlockSpec((1, ts, tc), lambda j, b, i: (b, i, col(j)))
        prev = pl.BlockSpec((1, CONV_HALO, tc), lambda j, b, i: (b, jnp.maximum(i * per - 1, 0), col(j)))
        nxt = pl.BlockSpec((1, CONV_HALO, tc), lambda j, b, i: (b, jnp.minimum((i + 1) * per, n_s * per - 1), col(j)))
        return tile, prev, nxt

    tb, _, nb = specs(lambda j: j)
    tcc, pc_, _ = specs(lambda j: j + n_c)
    th, ph, _ = specs(lambda j: j + 2 * n_c)
    w = pl.BlockSpec((3, tc), lambda j, b, i: (0, j))
    out_tile = pl.BlockSpec((1, ts, tc), lambda j, b, i: (b, i, j))
    o = _sds((b, s, d), BF16)
    return _pc(body, name=name, grid=(n_c, b, n_s), out_shape=(o, o, o, _sds((3, d), F32)),
               in_specs=[tb, nb, tcc, pc_, th, ph, tb, nb, w], out_specs=(out_tile, out_tile, out_tile, w),
               compiler_params=_cp("parallel", "arbitrary", "arbitrary"))(q, q, q, q, q, q, dr, dr, cw)


def _pick_window(group, cands):
    gid = jnp.full(cands[0].shape, group, jnp.int32)
    out = cands[-1]
    for k in range(len(cands) - 2, -1, -1):
        out = jnp.where(gid == k, cands[k], out)
    return out


def _window_sums(v, shift):
    s1 = v + shift(v, 1)
    s2 = s1 + shift(s1, 2)
    s3 = s2 + shift(s2, 4)
    s4 = s3 + shift(s3, 8)
    return [s1, s2, s3, s4]


def _pool_counts(group, first_row, n_rows, cols):
    t = _rows_iota((n_rows, cols)) + first_row
    window = _pick_window(group, [jnp.full((n_rows, cols), 2 << k, jnp.int32) for k in range(POOL_GROUPS)])
    return jnp.minimum(t + 1, window).astype(F32)


def _pool_fwd(name, x, sc, sh, w, scale):
    b, s, d = x.shape
    tc = d // POOL_GROUPS
    ts = _div(s, 512, POOL_HALO)
    n_s = s // ts

    def body(x_ref, xp_ref, sc_ref, sh_ref, w_ref, scale_ref, y_ref):
        i, grp = pl.program_id(1), pl.program_id(2)
        keep = _flag(i > 0)
        mod = 1.0 + sc_ref[0]
        u = x_ref[0] * mod + sh_ref[0]
        u_ext = jnp.concatenate([(xp_ref[0] * mod + sh_ref[0]) * keep, u], axis=0)
        summed = _pick_window(grp, _window_sums(u_ext, _back))[POOL_HALO:]
        pooled = summed / _pool_counts(grp, i * ts, ts, tc) - u
        y_ref[0] = jnp.dot(pooled.astype(BF16), w_ref[0], preferred_element_type=F32) * scale_ref[...]

    tile, prev, _ = _halo_specs(ts, tc, POOL_HALO, n_s, lambda j: j)
    seq = pl.BlockSpec((1, 1, tc), lambda b, i, j: (b, 0, j))
    return _pc(body, name=name, grid=(b, n_s, POOL_GROUPS), out_shape=_sds(x.shape, F32),
               in_specs=[tile, prev, seq, seq, pl.BlockSpec((1, tc, tc), lambda b, i, j: (j, 0, 0)),
                         pl.BlockSpec((1, tc), lambda b, i, j: (0, j))],
               out_specs=pl.BlockSpec((1, ts, tc), lambda b, i, j: (b, i, j)),
               compiler_params=_cp("parallel", "parallel", "parallel"))(x, x, sc, sh, w, scale)


def _pool_bwd(name, x, sc, sh, dy, w, w_t, scale, cargo=()):
    b, s, d = x.shape
    tc = d // POOL_GROUPS
    ts = _div(s, 512, POOL_HALO)
    n_s = s // ts

    def body(x_ref, xp_ref, sc_ref, sh_ref, dy_ref, dyn_ref, w_ref, wt_ref, scale_ref, du_ref, dw_ref, dscale_ref):
        grp, bi, i = pl.program_id(0), pl.program_id(1), pl.program_id(2)

        @pl.when((bi == 0) & (i == 0))
        def _():
            dw_ref[...] = jnp.zeros_like(dw_ref)
            dscale_ref[...] = jnp.zeros_like(dscale_ref)

        keep_prev = _flag(i > 0)
        keep_next = _flag(i < n_s - 1)
        mod = 1.0 + sc_ref[0]
        u = x_ref[0] * mod + sh_ref[0]
        u_ext = jnp.concatenate([(xp_ref[0] * mod + sh_ref[0]) * keep_prev, u], axis=0)
        summed = _pick_window(grp, _window_sums(u_ext, _back))[POOL_HALO:]
        pooled = (summed / _pool_counts(grp, i * ts, ts, tc) - u).astype(BF16)
        dy_ = dy_ref[0].astype(F32)
        ymat = jnp.dot(pooled, w_ref[0], preferred_element_type=F32)
        dscale_ref[...] += jnp.sum(dy_ * ymat, axis=0, keepdims=True)
        dys_ext = (jnp.concatenate([dy_, dyn_ref[0].astype(F32) * keep_next], axis=0) * scale_ref[...]).astype(BF16)
        dw_ref[0] += lax.dot_general(pooled, dys_ext[:ts], (((0,), (0,)), ((), ())), preferred_element_type=F32)
        dpooled = jnp.dot(dys_ext, wt_ref[0], preferred_element_type=F32)
        e = dpooled / _pool_counts(grp, i * ts, ts + POOL_HALO, tc)
        du_ref[0] = _pick_window(grp, _window_sums(e, _ahead))[:ts] - dpooled[:ts]

    per = ts // POOL_HALO
    tile = pl.BlockSpec((1, ts, tc), lambda j, b, i: (b, i, j))
    prev = pl.BlockSpec((1, POOL_HALO, tc), lambda j, b, i: (b, jnp.maximum(i * per - 1, 0), j))
    nxt = pl.BlockSpec((1, POOL_HALO, tc), lambda j, b, i: (b, jnp.minimum((i + 1) * per, n_s * per - 1), j))
    seq = pl.BlockSpec((1, 1, tc), lambda j, b, i: (b, 0, j))
    wsp = pl.BlockSpec((1, tc, tc), lambda j, b, i: (j, 0, 0))
    vec = pl.BlockSpec((1, tc), lambda j, b, i: (0, j))
    (du, dw, dscale), received = _pc_cargo(
        body, cargo, name=name, grid=(POOL_GROUPS, b, n_s),
        out_shape=(_sds(x.shape, F32), _sds((POOL_GROUPS, tc, tc), F32), _sds((1, d), F32)),
        in_specs=[tile, prev, seq, seq, tile, nxt, wsp, wsp, vec],
        out_specs=(tile, wsp, vec))(x, x, sc, sh, dy, dy, w, w_t, scale)
    return du, dw, dscale, received


def _rope_swap(v):
    lane = lax.broadcasted_iota(jnp.int32, v.shape, v.ndim - 1)
    lo, hi = QK_NOPE, QK_NOPE + QK_ROPE // 2
    from_above = pltpu.roll(v, HEAD_PAD - QK_ROPE // 2, axis=v.ndim - 1)
    from_below = pltpu.roll(v, QK_ROPE // 2, axis=v.ndim - 1)
    return jnp.where((lane >= lo) & (lane < hi), from_above,
                     jnp.where((lane >= hi) & (lane < hi + QK_ROPE // 2), from_below, 0.0))


def _rope(v, cos_t, sin_t):
    return v * cos_t + _rope_swap(v) * sin_t


def _rope_t(dv, cos_t, sin_t):
    return dv * cos_t + _rope_swap(dv * sin_t)


def _rms(v, g):
    r = lax.rsqrt(jnp.mean(v * v, axis=-1, keepdims=True) + RMS_EPS)
    return v * r, r


def _mla_norm_fwd(name, a, qn, kvn, cos_t, sin_t):
    b, s, wa = a.shape
    ql, kvl = qn.shape[1], kvn.shape[1]
    ts = _div(s, 512, 16)

    def body(aq_ref, akv_ref, ape_ref, qn_ref, kvn_ref, cos_ref, sin_ref, cq_ref, ckv_ref, kpe_ref):
        yq, _ = _rms(aq_ref[0], None)
        cq_ref[0] = (yq * qn_ref[...]).astype(BF16)
        ykv, _ = _rms(akv_ref[0], None)
        ckv_ref[0] = (ykv * kvn_ref[...]).astype(BF16)
        kpe_ref[0] = _rope(ape_ref[0], cos_ref[0], sin_ref[0])

    tok = lambda w, col: pl.BlockSpec((1, ts, w), lambda b, i: (b, i, col))
    return _pc(body, name=name, grid=(b, s // ts),
               out_shape=(_sds((b, s, ql), BF16), _sds((b, s, kvl), BF16), _sds((b, s, HEAD_PAD), F32)),
               in_specs=[tok(ql, 0), tok(kvl, ql // kvl), tok(HEAD_PAD, (ql + kvl) // HEAD_PAD), _vec_spec(ql),
                         _vec_spec(kvl), tok(HEAD_PAD, 0), tok(HEAD_PAD, 0)],
               out_specs=(tok(ql, 0), tok(kvl, 0), tok(HEAD_PAD, 0)),
               compiler_params=_cp("parallel", "parallel"))(a, a, a, qn, kvn, cos_t, sin_t)


def _mla_norm_bwd(name, a, dcq, dckv, dkpe, qn, kvn):
    b, s, wa = a.shape
    ql, kvl = qn.shape[1], kvn.shape[1]
    ts = _div(s, 512, 16)

    def body(a_ref, dcq_ref, dckv_ref, dkpe_ref, qn_ref, kvn_ref, da_ref, dqn_ref, dkvn_ref):
        @pl.when((pl.program_id(0) == 0) & (pl.program_id(1) == 0))
        def _():
            dqn_ref[...] = jnp.zeros_like(dqn_ref)
            dkvn_ref[...] = jnp.zeros_like(dkvn_ref)

        def one(v, dc, g_ref, dg_ref):
            yv, r = _rms(v, None)
            dg_ref[...] += jnp.sum(dc * yv, axis=0, keepdims=True)
            dyv = dc * g_ref[...]
            return r * (dyv - yv * jnp.mean(dyv * yv, axis=-1, keepdims=True))

        av = a_ref[0]
        da_ref[0, :, 0:ql] = one(av[:, 0:ql], dcq_ref[0], qn_ref, dqn_ref).astype(BF16)
        da_ref[0, :, ql:ql + kvl] = one(av[:, ql:ql + kvl], dckv_ref[0], kvn_ref, dkvn_ref).astype(BF16)
        da_ref[0, :, ql + kvl:] = dkpe_ref[0].astype(BF16)

    return _pc(body, name=name, grid=(b, s // ts),
               out_shape=(_sds(a.shape, BF16), _sds((1, ql), F32), _sds((1, kvl), F32)),
               in_specs=[_tok_spec(ts, wa), _tok_spec(ts, ql), _tok_spec(ts, kvl), _tok_spec(ts, HEAD_PAD),
                         _vec_spec(ql), _vec_spec(kvl)],
               out_specs=(_tok_spec(ts, wa), _vec_spec(ql), _vec_spec(kvl)),
               compiler_params=_cp("arbitrary", "arbitrary"))(a, dcq, dckv, dkpe, qn, kvn)


def _mla_prep_fwd(name, q_raw, kv_raw, kpe, cos_t, sin_t, n_heads):
    b, s, wq = q_raw.shape
    ts = _div(s, 256, 16)

    def body(q_ref, k_ref, v_ref, kpe_ref, cos_ref, sin_ref, qo_ref, ko_ref, vo_ref):
        cos_, sin_, kpe_ = cos_ref[0], sin_ref[0], kpe_ref[0]
        for h in range(n_heads):
            lanes = slice(h * HEAD_PAD, (h + 1) * HEAD_PAD)
            qo_ref[0, :, lanes] = _rope(q_ref[0, :, lanes], cos_, sin_).astype(BF16)
            ko_ref[0, :, lanes] = (k_ref[0, :, lanes] + kpe_).astype(BF16)
        vo_ref[0] = v_ref[0].astype(BF16)

    wide = lambda part: pl.BlockSpec((1, ts, wq), lambda b, i: (b, i, part))
    tok = pl.BlockSpec((1, ts, HEAD_PAD), lambda b, i: (b, i, 0))
    o = _sds(q_raw.shape, BF16)
    return _pc(body, name=name, grid=(b, s // ts), out_shape=(o, o, o),
               in_specs=[wide(0), wide(0), wide(1), tok, tok, tok], out_specs=(wide(0), wide(0), wide(0)),
               compiler_params=_cp("parallel", "parallel"))(q_raw, kv_raw, kv_raw, kpe, cos_t, sin_t)


def _mla_prep_bwd(name, dq, dk, dv, cos_t, sin_t, n_heads):
    b, s, wq = dq.shape
    ts = _div(s, 256, 16)

    def body(dq_ref, dk_ref, dv_ref, cos_ref, sin_ref, dqr_ref, dkv_ref, dkpe_ref):
        cos_, sin_ = cos_ref[0], sin_ref[0]
        dk_sum = None
        for h in range(n_heads):
            lanes = slice(h * HEAD_PAD, (h + 1) * HEAD_PAD)
            dqr_ref[0, :, lanes] = _rope_t(dq_ref[0, :, lanes], cos_, sin_).astype(BF16)
            dk_h = dk_ref[0, :, lanes]
            dkv_ref[0, :, lanes] = dk_h.astype(BF16)
            dk_sum = dk_h if dk_sum is None else dk_sum + dk_h
        dkv_ref[0, :, wq:2 * wq] = dv_ref[0]
        dkpe_ref[0] = _rope_t(dk_sum, cos_, sin_)

    wide = pl.BlockSpec((1, ts, wq), lambda b, i: (b, i, 0))
    both = pl.BlockSpec((1, ts, 2 * wq), lambda b, i: (b, i, 0))
    tok = pl.BlockSpec((1, ts, HEAD_PAD), lambda b, i: (b, i, 0))
    return _pc(body, name=name, grid=(b, s // ts),
               out_shape=(_sds(dq.shape, BF16), _sds((b, s, 2 * wq), BF16), _sds((b, s, HEAD_PAD), F32)),
               in_specs=[wide, wide, wide, tok, tok], out_specs=(wide, both, tok),
               compiler_params=_cp("parallel", "parallel"))(dq, dk, dv, cos_t, sin_t)


FLASH_TILE = 1024
LOG2_E = 1.4426950408889634


def _heads_per_step(n_heads):
    return 2 if n_heads % 2 == 0 else 1


def _causal_mask(i, j, tq, tk):
    rows = lax.broadcasted_iota(jnp.int32, (tq, tk), 0) + i * tq
    cols = lax.broadcasted_iota(jnp.int32, (tq, tk), 1) + j * tk
    return cols <= rows


def _nt(a, b):
    return lax.dot_general(a, b, (((1,), (1,)), ((), ())), preferred_element_type=F32)


def _tn(a, b):
    return lax.dot_general(a, b, (((0,), (0,)), ((), ())), preferred_element_type=F32)


def _flash_fwd(name, q, k, v, n_heads, sm_scale):
    b, s, _ = q.shape
    t, hp = _div(s, FLASH_TILE, LANE), _heads_per_step(n_heads)
    n, w = s // t, hp * HEAD_PAD
    neg = float(jnp.finfo(jnp.float32).min)
    c2 = sm_scale * LOG2_E

    def body(q_ref, k_ref, v_ref, o_ref, lse_ref, m_ref, l_ref, acc_ref):
        i, j = pl.program_id(2), pl.program_id(3)

        @pl.when(j == 0)
        def _():
            m_ref[...] = jnp.full(m_ref.shape, neg, F32)
            l_ref[...] = jnp.zeros_like(l_ref)
            acc_ref[...] = jnp.zeros_like(acc_ref)

        def block(on_diagonal):
            for hh in range(hp):
                ln = slice(hh * HEAD_PAD, (hh + 1) * HEAD_PAD)
                sc = _nt(q_ref[0, :, ln], k_ref[0, :, ln])
                if on_diagonal:
                    sc = jnp.where(_causal_mask(i, j, t, t), sc, neg)
                m_old = m_ref[hh]
                m_new = jnp.maximum(m_old, jnp.max(sc, axis=-1, keepdims=True))
                p = jnp.exp2((sc - m_new) * c2)
                corr = jnp.exp2((m_old - m_new) * c2)
                l_ref[hh] = corr * l_ref[hh] + jnp.sum(p, axis=-1, keepdims=True)
                acc_ref[:, ln] = corr * acc_ref[:, ln] + jnp.dot(p.astype(BF16), v_ref[0, :, ln],
                                                                 preferred_element_type=F32)
                m_ref[hh] = m_new

        pl.when(j < i)(functools.partial(block, False))
        pl.when(j == i)(functools.partial(block, True))

        @pl.when(j == n - 1)
        def _():
            for hh in range(hp):
                ln = slice(hh * HEAD_PAD, (hh + 1) * HEAD_PAD)
                o_ref[0, :, ln] = (acc_ref[:, ln] / l_ref[hh]).astype(BF16)
                lse_ref[0, :, ln] = jnp.broadcast_to(m_ref[hh] * sm_scale + jnp.log(l_ref[hh]), (t, HEAD_PAD))

    qs = pl.BlockSpec((1, t, w), lambda b, h, i, j: (b, i, h))
    ks = pl.BlockSpec((1, t, w), lambda b, h, i, j: (b, jnp.minimum(j, i), h))
    return _pc(body, name=name, grid=(b, n_heads // hp, n, n),
               out_shape=(_sds(q.shape, BF16), _sds(q.shape, F32)), in_specs=[qs, ks, ks], out_specs=(qs, qs),
               scratch_shapes=[pltpu.VMEM((hp, t, 1), F32), pltpu.VMEM((hp, t, 1), F32), pltpu.VMEM((t, w), F32)],
               compiler_params=_cp("parallel", "parallel", "parallel", "arbitrary"))(q, k, v)


def _flash_dq(name, q, k, v, o, lse, do, n_heads, sm_scale):
    b, s, _ = q.shape
    t, hp = _div(s, FLASH_TILE, LANE), _heads_per_step(n_heads)
    n, w = s // t, hp * HEAD_PAD
    c2 = sm_scale * LOG2_E

    def body(q_ref, k_ref, v_ref, o_ref, lse_ref, do_ref, dq_ref, acc_ref, delta_ref):
        i, j = pl.program_id(2), pl.program_id(3)

        @pl.when(j == 0)
        def _():
            acc_ref[...] = jnp.zeros_like(acc_ref)
            for hh in range(hp):
                ln = slice(hh * HEAD_PAD, (hh + 1) * HEAD_PAD)
                delta_ref[hh] = jnp.sum(do_ref[0, :, ln].astype(F32) * o_ref[0, :, ln].astype(F32), axis=-1,
                                        keepdims=True)

        def block(on_diagonal):
            for hh in range(hp):
                ln = slice(hh * HEAD_PAD, (hh + 1) * HEAD_PAD)
                sc = _nt(q_ref[0, :, ln], k_ref[0, :, ln])
                p = jnp.exp2(sc * c2 - lse_ref[0, :, hh * HEAD_PAD:hh * HEAD_PAD + 1] * LOG2_E)
                if on_diagonal:
                    p = jnp.where(_causal_mask(i, j, t, t), p, 0.0)
                dp = _nt(do_ref[0, :, ln], v_ref[0, :, ln])
                ds = p * (dp - delta_ref[hh])
                acc_ref[:, ln] += jnp.dot(ds.astype(BF16), k_ref[0, :, ln], preferred_element_type=F32)

        pl.when(j < i)(functools.partial(block, False))
        pl.when(j == i)(functools.partial(block, True))

        @pl.when(j == n - 1)
        def _():
            dq_ref[0] = acc_ref[...] * sm_scale

    qs = pl.BlockSpec((1, t, w), lambda b, h, i, j: (b, i, h))
    ks = pl.BlockSpec((1, t, w), lambda b, h, i, j: (b, jnp.minimum(j, i), h))
    return _pc(body, name=name, grid=(b, n_heads // hp, n, n), out_shape=_sds(q.shape, F32),
               in_specs=[qs, ks, ks, qs, qs, qs], out_specs=qs,
               scratch_shapes=[pltpu.VMEM((t, w), F32), pltpu.VMEM((hp, t, 1), F32)],
               compiler_params=_cp("parallel", "parallel", "parallel", "arbitrary"))(q, k, v, o, lse, do)


def _flash_dkv(name, q, k, v, o, lse, do, n_heads, sm_scale):
    b, s, _ = q.shape
    t, hp = _div(s, FLASH_TILE, LANE), _heads_per_step(n_heads)
    n, w = s // t, hp * HEAD_PAD
    c2 = sm_scale * LOG2_E

    def body(q_ref, k_ref, v_ref, o_ref, lse_ref, do_ref, dk_ref, dv_ref, dk_acc, dv_acc):
        j, i = pl.program_id(2), pl.program_id(3)

        @pl.when(i == 0)
        def _():
            dk_acc[...] = jnp.zeros_like(dk_acc)
            dv_acc[...] = jnp.zeros_like(dv_acc)

        def block(on_diagonal):
            for hh in range(hp):
                ln = slice(hh * HEAD_PAD, (hh + 1) * HEAD_PAD)
                do_ = do_ref[0, :, ln]
                delta = jnp.sum(do_.astype(F32) * o_ref[0, :, ln].astype(F32), axis=-1, keepdims=True)
                sc = _nt(q_ref[0, :, ln], k_ref[0, :, ln])
                p = jnp.exp2(sc * c2 - lse_ref[0, :, hh * HEAD_PAD:hh * HEAD_PAD + 1] * LOG2_E)
                if on_diagonal:
                    p = jnp.where(_causal_mask(i, j, t, t), p, 0.0)
                dv_acc[:, ln] += _tn(p.astype(BF16), do_)
                dp = _nt(do_, v_ref[0, :, ln])
                ds = p * (dp - delta)
                dk_acc[:, ln] += _tn(ds.astype(BF16), q_ref[0, :, ln])

        pl.when(i > j)(functools.partial(block, False))
        pl.when(i == j)(functools.partial(block, True))

        @pl.when(i == n - 1)
        def _():
            dk_ref[0] = dk_acc[...] * sm_scale
            dv_ref[0] = dv_acc[...].astype(BF16)

    qs = pl.BlockSpec((1, t, w), lambda b, h, j, i: (b, jnp.maximum(i, j), h))
    ks = pl.BlockSpec((1, t, w), lambda b, h, j, i: (b, j, h))
    return _pc(body, name=name, grid=(b, n_heads // hp, n, n), out_shape=(_sds(q.shape, F32), _sds(q.shape, BF16)),
               in_specs=[qs, ks, ks, qs, qs, qs], out_specs=(ks, ks),
               scratch_shapes=[pltpu.VMEM((t, w), F32), pltpu.VMEM((t, w), F32)],
               compiler_params=_cp("parallel", "parallel", "parallel", "arbitrary"))(q, k, v, o, lse, do)


def _mod_fwd(name, c_all, w, bias):
    depth, d, n = w.shape
    rows = c_all.shape[0]

    def body(c_ref, w_ref, b_ref, o_ref):
        cv = c_ref[...]
        cond = (cv * _sigmoid(cv)).astype(BF16)
        o_ref[0] = jnp.dot(cond, w_ref[0].astype(BF16), preferred_element_type=F32) + b_ref[0]

    return _pc(body, name=name, grid=(depth,), out_shape=_sds((depth, rows, n), F32),
               in_specs=[pl.BlockSpec((rows, d), lambda l: (0, 0)), pl.BlockSpec((1, d, n), lambda l: (l, 0, 0)),
                         pl.BlockSpec((1, 1, n), lambda l: (l, 0, 0))],
               out_specs=pl.BlockSpec((1, rows, n), lambda l: (l, 0, 0)), compiler_params=_cp("parallel"))(c_all, w, bias)


def _mod_bwd(name, c_all, dmod_cols, dmod_all):
    depth, rows, n = dmod_cols.shape
    d = c_all.shape[1]
    n_all = dmod_all.shape[2]
    tn = _div(n, 512, LANE)

    def body(c_ref, dm_ref, dma_ref, gw_ref, gb_ref):
        cv = c_ref[...]
        cond = (cv * _sigmoid(cv)).astype(BF16)
        gw_ref[0] = _tn(cond, dm_ref[0].astype(BF16))

        @pl.when(pl.program_id(1) == 0)
        def _():
            gb_ref[0] = jnp.sum(dma_ref[0], axis=0, keepdims=True)

    return _pc(body, name=name, grid=(depth, n // tn),
               out_shape=(_sds((depth, d, n), F32), _sds((depth, 1, n_all), F32)),
               in_specs=[pl.BlockSpec((rows, d), lambda l, j: (0, 0)), pl.BlockSpec((1, rows, tn), lambda l, j: (l, 0, j)),
                         pl.BlockSpec((1, rows, n_all), lambda l, j: (l, 0, 0))],
               out_specs=(pl.BlockSpec((1, d, tn), lambda l, j: (l, 0, j)), pl.BlockSpec((1, 1, n_all), lambda l, j: (l, 0, 0))),
               compiler_params=_cp("parallel", "arbitrary"))(c_all, dmod_cols, dmod_all)


def _adamw(name, w, g, m, v):
    shape = w.shape
    cols = shape[-1]
    rows = _size(shape) // cols
    tr = _div(rows, max(SUBLANE, (2 ** 19) // cols // SUBLANE * SUBLANE), SUBLANE)
    c1 = 1.0 - ADAM_B1 ** ADAM_STEP
    c2 = 1.0 - ADAM_B2 ** ADAM_STEP

    def body(w_ref, g_ref, m_ref, v_ref, d_ref, mo_ref, vo_ref):
        gv = g_ref[...]
        m_new = ADAM_B1 * m_ref[...] + (1.0 - ADAM_B1) * gv
        v_new = ADAM_B2 * v_ref[...] + (1.0 - ADAM_B2) * (gv * gv)
        m_hat = m_new / c1
        v_hat = v_new / c2
        d_ref[...] = -ADAM_LR * (m_hat / (jnp.sqrt(v_hat) + ADAM_EPS) + ADAM_WD * w_ref[...])
        mo_ref[...] = m_new
        vo_ref[...] = v_new

    spec = pl.BlockSpec((tr, cols), lambda i: (i, 0))
    o = _sds((rows, cols), F32)
    outs = _pc(body, name=name, grid=(rows // tr,), out_shape=(o, o, o), in_specs=[spec] * 4, out_specs=(spec,) * 3,
               compiler_params=_cp("parallel"))(*[a.reshape(rows, cols) for a in (w, g, m, v)])
    return tuple(a.reshape(shape) for a in outs)


def _rope_tables(positions):
    half = QK_ROPE // 2
    inv_freq = ROPE_THETA ** (-jnp.arange(0, QK_ROPE, 2, dtype=F32) / QK_ROPE)
    ang = positions.astype(F32)[..., None] * inv_freq
    cos, sin = jnp.cos(ang), jnp.sin(ang)
    lead = positions.shape
    ones = jnp.ones(lead + (QK_NOPE,), F32)
    tail_one = jnp.ones(lead + (HEAD_PAD - QK_NOPE - QK_ROPE,), F32)
    cos_t = jnp.concatenate([ones, cos, cos, tail_one], axis=-1)
    sin_t = jnp.concatenate([0 * ones, -sin, sin, 0 * tail_one], axis=-1)
    return cos_t, sin_t


def _pad_heads(w, n_heads, parts, axis):
    w = jnp.moveaxis(w, axis, -1)
    lead = w.shape[:-1]
    per = w.shape[-1] // n_heads
    w = w.reshape(lead + (n_heads, per))
    kept = jnp.concatenate([w[..., a:b_] for a, b_ in parts], axis=-1)
    pad = HEAD_PAD - kept.shape[-1]
    kept = jnp.concatenate([kept, jnp.zeros(lead + (n_heads, pad), w.dtype)], axis=-1)
    return jnp.moveaxis(kept.reshape(lead + (n_heads * HEAD_PAD,)), -1, axis)


def _unpad_heads(g, n_heads, width, axis):
    g = jnp.moveaxis(g, axis, -1)
    lead = g.shape[:-1]
    g = g.reshape(lead + (n_heads, HEAD_PAD))[..., :width]
    return g, lead


def kernel(x, c, positions, mod_w, mod_b, ln_g, ln_b, pool_w, pool_scale, mla_w_a, mla_q_norm, mla_w_uq, mla_kv_norm, mla_w_ukv, mla_w_o, sc_w_in, sc_conv, sc_w_out, ffn_w_up, ffn_conv, ffn_conv_b, ffn_w_down, loss_target, m_mod_w, m_mod_b, m_ln_g, m_ln_b, m_pool_w, m_pool_scale, m_mla_w_a, m_mla_q_norm, m_mla_w_uq, m_mla_kv_norm, m_mla_w_ukv, m_mla_w_o, m_sc_w_in, m_sc_conv, m_sc_w_out, m_ffn_w_up, m_ffn_conv, m_ffn_conv_b, m_ffn_w_down, v_mod_w, v_mod_b, v_ln_g, v_ln_b, v_pool_w, v_pool_scale, v_mla_w_a, v_mla_q_norm, v_mla_w_uq, v_mla_kv_norm, v_mla_w_ukv, v_mla_w_o, v_sc_w_in, v_sc_conv, v_sc_w_out, v_ffn_w_up, v_ffn_conv, v_ffn_conv_b, v_ffn_w_down):
    wts = dict(mod_w=mod_w, mod_b=mod_b, ln_g=ln_g, ln_b=ln_b, pool_w=pool_w, pool_scale=pool_scale, mla_w_a=mla_w_a,
               mla_q_norm=mla_q_norm, mla_w_uq=mla_w_uq, mla_kv_norm=mla_kv_norm, mla_w_ukv=mla_w_ukv, mla_w_o=mla_w_o,
               sc_w_in=sc_w_in, sc_conv=sc_conv, sc_w_out=sc_w_out, ffn_w_up=ffn_w_up, ffn_conv=ffn_conv,
               ffn_conv_b=ffn_conv_b, ffn_w_down=ffn_w_down)
    mom1 = dict(mod_w=m_mod_w, mod_b=m_mod_b, ln_g=m_ln_g, ln_b=m_ln_b, pool_w=m_pool_w, pool_scale=m_pool_scale,
                mla_w_a=m_mla_w_a, mla_q_norm=m_mla_q_norm, mla_w_uq=m_mla_w_uq, mla_kv_norm=m_mla_kv_norm,
                mla_w_ukv=m_mla_w_ukv, mla_w_o=m_mla_w_o, sc_w_in=m_sc_w_in, sc_conv=m_sc_conv, sc_w_out=m_sc_w_out,
                ffn_w_up=m_ffn_w_up, ffn_conv=m_ffn_conv, ffn_conv_b=m_ffn_conv_b, ffn_w_down=m_ffn_w_down)
    mom2 = dict(mod_w=v_mod_w, mod_b=v_mod_b, ln_g=v_ln_g, ln_b=v_ln_b, pool_w=v_pool_w, pool_scale=v_pool_scale,
                mla_w_a=v_mla_w_a, mla_q_norm=v_mla_q_norm, mla_w_uq=v_mla_w_uq, mla_kv_norm=v_mla_kv_norm,
                mla_w_ukv=v_mla_w_ukv, mla_w_o=v_mla_w_o, sc_w_in=v_sc_w_in, sc_conv=v_sc_conv, sc_w_out=v_sc_w_out,
                ffn_w_up=v_ffn_w_up, ffn_conv=v_ffn_conv, ffn_conv_b=v_ffn_conv_b, ffn_w_down=v_ffn_w_down)

    bsz, seq, d = x.shape
    depth = mod_b.shape[0]
    n_tok = bsz * seq
    n_heads = d // V_HEAD
    ql, kvl = mla_q_norm.shape[1], mla_kv_norm.shape[1]
    alpha = float((2 * depth) ** 0.25)
    sm_scale = float((QK_NOPE + QK_ROPE) ** -0.5)
    mx, my, mc = lax.axis_index("x"), lax.axis_index("y"), lax.axis_index("c")
    chip = 2 * mx + my
    dev = 2 * chip + mc

    small_names = list(SMALL_SHARDED)
    small_pack, small_spans = _pack_rows([c] + [wts[n] for n in small_names], F32, SUBLANE)
    rows_small = small_pack.shape[0]
    small_all = _all_gather8("gather_small_params", small_pack, True).reshape(N_DEV, rows_small * PACK_COLS)
    c_all = small_all[:, :c.size].reshape(N_DEV * bsz, d)
    per_chip = small_all[0::2]
    full = dict(wts)
    for n, (off, shape) in zip(small_names, small_spans[1:]):
        blocks = per_chip[:, off:off + _size(shape)].reshape((N_CHIP,) + tuple(shape))
        full[n] = _join_chips(blocks, SMALL_SHARDED[n])

    n_mod = mod_w.shape[2]
    bias_cols = lax.dynamic_slice_in_dim(mod_b, chip * n_mod, n_mod, axis=1)[:, None, :]
    mod_cols = _mod_fwd("mod_fwd", c_all, mod_w, bias_cols)
    half_rows = (N_DEV * bsz) // 2
    mod_half = lax.dynamic_slice_in_dim(mod_cols, mc * half_rows, half_rows, axis=1).reshape(depth * half_rows, n_mod)
    mod_all = _all_gather8("gather_mod", mod_half, True).reshape(N_CHIP, 2, depth, half_rows, n_mod)
    mod_all = jnp.transpose(mod_all, (2, 1, 3, 0, 4)).reshape(depth, N_DEV * bsz, N_CHIP * n_mod)
    mod_mine = lax.dynamic_slice_in_dim(mod_all, dev * bsz, bsz, axis=1)
    mods = [[mod_mine[l, :, k * d:(k + 1) * d][:, None, :] for k in range(6)] for l in range(depth)]

    big_names = list(BIG)
    gathered = _gather_weights("gather_weights", [wts[n].astype(BF16).reshape(-1, wts[n].shape[-1]) for n in big_names])
    by_chip = {n: g.reshape((N_CHIP,) + wts[n].shape) for n, g in zip(big_names, gathered)}
    f_hid = ffn_w_down.shape[1] * N_CHIP
    for n in ('pool_w', 'mla_w_a', 'mla_w_uq', 'mla_w_ukv', 'mla_w_o', 'sc_w_out'):
        full[n] = jnp.concatenate([by_chip[n][j] for j in range(N_CHIP)], axis=BIG[n])
    w_up_cols = by_chip['ffn_w_up'].reshape(N_CHIP, depth * d, -1)
    w_up_rows = jnp.transpose(by_chip['ffn_w_up'], (1, 0, 3, 2)).reshape(1, depth * 2 * f_hid, d)
    w_down_rows = jnp.transpose(by_chip['ffn_w_down'], (1, 0, 2, 3)).reshape(1, depth * f_hid, d)
    w_down_t = jnp.transpose(by_chip['ffn_w_down'], (1, 3, 0, 2)).reshape(1, depth * d, f_hid)
    n_sc = sc_w_in.shape[0]
    w_in_cols = by_chip['sc_w_in'].reshape(N_CHIP, n_sc * d, -1)
    w_in_rows = jnp.transpose(by_chip['sc_w_in'], (1, 0, 3, 2)).reshape(1, n_sc * 3 * d, d)

    nope_rope = [(0, QK_NOPE + QK_ROPE)]
    cos_t, sin_t = _rope_tables(positions)

    def t2(a):
        return a.reshape(n_tok, a.shape[-1])

    def t3(a):
        return a.reshape(bsz, seq, a.shape[-1])

    saved = []
    xin = x
    u = _modulate("modulate_in", x, mods[0][1], mods[0][0])
    loss_acc = None
    for l in range(depth):
        sh1, sc1, g1, sh2, sc2, g2 = mods[l]
        kind, j = l % 3, l // 3
        st = dict(x=xin)
        if kind == 0:
            w = full['pool_w'][j]
            st.update(w=w, w_t=jnp.swapaxes(w, 1, 2), scale=full['pool_scale'][j][None, :])
            y = _pool_fwd(f"pool_fwd_{l}", xin, sc1, sh1, st['w'], st['scale'])
        elif kind == 1:
            wa = full['mla_w_a'][j]
            zeros = jnp.zeros((d, QK_NOPE), BF16)
            w_a = jnp.concatenate([wa[:, :ql + kvl], zeros, wa[:, ql + kvl:], zeros[:, :HEAD_PAD - QK_NOPE - QK_ROPE]], axis=1)
            w_uq = _pad_heads(full['mla_w_uq'][j], n_heads, nope_rope, 1)
            w_kv = jnp.concatenate([_pad_heads(full['mla_w_ukv'][j], n_heads, [(0, QK_NOPE)], 1),
                                    _pad_heads(full['mla_w_ukv'][j], n_heads, [(QK_NOPE, QK_NOPE + V_HEAD)], 1)], axis=1)
            w_o = _pad_heads(full['mla_w_o'][j], n_heads, [(0, V_HEAD)], 0)
            qn, kvn = mla_q_norm[j][None, :], mla_kv_norm[j][None, :]
            a = t3(_mm_nn(f"mla_a_{l}", [(t2(u), _w2(w_a))], F32))
            cq, ckv, kpe = _mla_norm_fwd(f"mla_norm_fwd_{l}", a, qn, kvn, cos_t, sin_t)
            q_raw = t3(_mm_nn(f"mla_q_{l}", [(t2(cq), _w2(w_uq))], F32))
            kv_raw = t3(_mm_nn(f"mla_kv_{l}", [(t2(ckv), _w2(w_kv))], F32))
            qh, kh, vh = _mla_prep_fwd(f"mla_prep_fwd_{l}", q_raw, kv_raw, kpe, cos_t, sin_t, n_heads)
            o, lse = _flash_fwd(f"flash_fwd_{l}", qh, kh, vh, n_heads, sm_scale)
            y = t3(_mm_nn(f"mla_o_{l}", [(t2(o), _w2(w_o))], F32))
            st.update(u=u, w_a=w_a, w_uq=w_uq, w_kv=w_kv, w_o=w_o, qn=qn, kvn=kvn, a=a, cq=cq, ckv=ckv,
                      qh=qh, kh=kh, vh=vh, o=o, lse=lse)
        else:
            w_out, cw = full['sc_w_out'][j], full['sc_conv'][j]
            q = t3(_mm_nn(f"sc_in_{l}", [(t2(u), (w_in_cols, j))], F32))
            r = _shortconv_fwd(f"shortconv_fwd_{l}", q, cw)
            y = t3(_mm_nn(f"sc_out_{l}", [(t2(r), _w2(w_out))], F32))
            st.update(u=u, w_out=w_out, cw=cw, q=q, r=r)
        lng, lnb = full['ln_g'][l], full['ln_b'][l]
        z1, xmid, u2 = _ln_mod_fwd(f"ln_mod_a_{l}", alpha, xin, y, g1, lng[0:1], lnb[0:1], sc2, sh2)
        cwf, cbf = full['ffn_conv'][l], ffn_conv_b[l][None, :]
        p = t3(_mm_nn(f"ffn_up_{l}", [(t2(u2), (w_up_cols, l))], F32))
        act = _convglu_fwd(f"convglu_fwd_{l}", p, cwf, cbf)
        y2 = t3(_mm_nn(f"ffn_down_{l}", [(t2(act), (w_down_rows, l))], F32))
        st.update(y1=y, z1=z1, xmid=xmid, u2=u2, p=p, act=act, y2=y2, cwf=cwf, cbf=cbf, lng=lng, lnb=lnb)
        if l + 1 < depth:
            nsh1, nsc1 = mods[l + 1][0], mods[l + 1][1]
            z2, xin, u = _ln_mod_fwd(f"ln_mod_b_{l}", alpha, xmid, y2, g2, lng[1:2], lnb[1:2], nsc1, nsh1)
        else:
            z2, ct, loss_acc = _ln_loss_fwd("ln_loss", alpha, xmid, y2, g2, lng[1:2], lnb[1:2], loss_target)
        st.update(z2=z2)
        saved.append(st)
    loss = lax.psum(loss_acc[0, 0], ("x", "y", "c"))

    grads = {}
    dmods = [[None] * 6 for _ in range(depth)]
    g_ln_g = [[None, None] for _ in range(depth)]
    g_ln_b = [[None, None] for _ in range(depth)]
    stack = {n: [None] * wts[n].shape[0] for n in ('pool_scale', 'mla_q_norm', 'mla_kv_norm', 'sc_conv', 'ffn_conv',
                                                    'ffn_conv_b')}
    units = {n: [None] * wts[n].shape[0] for n in big_names}
    cargo_a, cargo_b, received = [], [], {}

    def landed(items, got):
        for (n, i, _), r in zip(items, got):
            received[(n, i)] = r

    upstream = (ct,)
    for l in reversed(range(depth)):
        st = saved[l]
        sh1, sc1, g1, sh2, sc2, g2 = mods[l]
        kind, j = l % 3, l // 3
        res = _sub_bwd(f"sub_bwd_b_{l}", alpha, upstream, st['z2'], st['y2'], g2, st['lng'][1:2])
        dz2, dy2, dmods[l][5], g_ln_g[l][1], g_ln_b[l][1] = res[:5]
        if l + 1 < depth:
            dmods[l + 1][1], dmods[l + 1][0] = res[5], res[6]
        dy2f = t2(dy2)
        da = t3(_mm_nn(f"ffn_down_bwd_{l}", [(dy2f, (w_down_t, l))], F32))
        units['ffn_w_down'][l] = _mm_tn(f"ffn_down_dw_{l}", t2(st['act']), [dy2f],
                                        out_dtype=BF16).reshape(N_CHIP, f_hid // N_CHIP, d)
        dpv, dpg, dcw, dcb, got = _convglu_bwd(f"convglu_bwd_{l}", st['p'], da, st['cwf'], st['cbf'],
                                               cargo=[u for _, _, u in cargo_a])
        landed(cargo_a, got)
        stack['ffn_conv'][l], stack['ffn_conv_b'][l] = dcw, dcb[0]
        down_unit = [('ffn_w_down', l, units['ffn_w_down'][l])]
        du2, got = _mm_nn(f"ffn_up_bwd_{l}", [(t2(dpv), (w_up_rows, 2 * l)), (t2(dpg), (w_up_rows, 2 * l + 1))], F32,
                          cargo=[units['ffn_w_down'][l]])
        landed(down_unit, got)
        du2 = t3(du2)
        res = _mm_tn(f"ffn_up_dw_{l}", t2(st['u2']), [t2(dpv), t2(dpg)], N_CHIP, out_dtype=BF16,
                     cargo=[u for _, _, u in cargo_b])
        if cargo_b:
            landed(cargo_b, res[1])
            res = res[0]
        units['ffn_w_up'][l] = res
        res = _sub_bwd(f"sub_bwd_a_{l}", alpha, (dz2, du2, st['xmid'], sc2), st['z1'], st['y1'], g1, st['lng'][0:1])
        dz1, dy1, dmods[l][2], g_ln_g[l][0], g_ln_b[l][0], dmods[l][4], dmods[l][3] = res
        dy1f = t2(dy1)
        if kind == 0:
            up_unit = [('ffn_w_up', l, units['ffn_w_up'][l])] if l == 0 else []
            du1, dw, dscale, got = _pool_bwd(f"pool_bwd_{l}", st['x'], sc1, sh1, dy1, st['w'], st['w_t'], st['scale'],
                                             cargo=[u for _, _, u in up_unit])
            landed(up_unit, got)
            stack['pool_scale'][j] = dscale[0]
            grp = dw.shape[1] // N_CHIP
            units['pool_w'][j] = jnp.transpose(dw.reshape(POOL_GROUPS, N_CHIP, grp, dw.shape[2]),
                                               (1, 0, 2, 3)).reshape(N_CHIP, POOL_GROUPS * grp, dw.shape[2])
        elif kind == 1:
            do = t3(_mm_nn(f"mla_o_bwd_{l}", [(dy1f, _w2(jnp.swapaxes(st['w_o'], 0, 1)))], BF16))
            gwo, _ = _unpad_heads(_mm_tn(f"mla_o_dw_{l}", t2(st['o']), [dy1f])[0], n_heads, V_HEAD, 0)
            units['mla_w_o'][j] = jnp.moveaxis(gwo.reshape(d, n_heads * V_HEAD), -1, 0).reshape(N_CHIP, -1, d)
            fa = (st['qh'], st['kh'], st['vh'], st['o'], st['lse'], do, n_heads, sm_scale)
            dq = _flash_dq(f"flash_dq_{l}", *fa)
            dk, dv = _flash_dkv(f"flash_dkv_{l}", *fa)
            dq_raw, dkv_raw, dkpe = _mla_prep_bwd(f"mla_prep_bwd_{l}", dq, dk, dv, cos_t, sin_t, n_heads)
            dq_raw, dkv_raw = t2(dq_raw), t2(dkv_raw)
            dcq = t3(_mm_nn(f"mla_q_bwd_{l}", [(dq_raw, _w2(jnp.swapaxes(st['w_uq'], 0, 1)))], F32))
            dckv = t3(_mm_nn(f"mla_kv_bwd_{l}", [(dkv_raw, _w2(jnp.swapaxes(st['w_kv'], 0, 1)))], F32))
            gq, _ = _unpad_heads(_mm_tn(f"mla_q_dw_{l}", t2(st['cq']), [dq_raw])[0], n_heads, QK_NOPE + QK_ROPE, 1)
            units['mla_w_uq'][j] = _cols_by_chip(gq.reshape(ql, n_heads * (QK_NOPE + QK_ROPE)))
            gkv = _mm_tn(f"mla_kv_dw_{l}", t2(st['ckv']), [dkv_raw])[0]
            gk, _ = _unpad_heads(gkv[:, :n_heads * HEAD_PAD], n_heads, QK_NOPE, 1)
            gv, _ = _unpad_heads(gkv[:, n_heads * HEAD_PAD:], n_heads, V_HEAD, 1)
            units['mla_w_ukv'][j] = _cols_by_chip(
                jnp.concatenate([gk, gv], axis=-1).reshape(kvl, n_heads * (QK_NOPE + V_HEAD)))
            da_, dqn, dkvn = _mla_norm_bwd(f"mla_norm_bwd_{l}", st['a'], dcq, dckv, dkpe, st['qn'], st['kvn'])
            stack['mla_q_norm'][j], stack['mla_kv_norm'][j] = dqn[0], dkvn[0]
            du1 = t3(_mm_nn(f"mla_a_bwd_{l}", [(t2(da_), _w2(jnp.swapaxes(st['w_a'], 0, 1)))], F32))
            gwa = _mm_tn(f"mla_a_dw_{l}", t2(st['u']), [t2(da_)])[0]
            units['mla_w_a'][j] = _cols_by_chip(jnp.concatenate(
                [gwa[:, :ql + kvl], gwa[:, ql + kvl + QK_NOPE:ql + kvl + QK_NOPE + QK_ROPE]], axis=1))
        else:
            dr = t3(_mm_nn(f"sc_out_bwd_{l}", [(dy1f, _w2(jnp.swapaxes(st['w_out'], 0, 1)))], F32))
            units['sc_w_out'][j] = _mm_tn(f"sc_out_dw_{l}", t2(st['r']), [dy1f], out_dtype=BF16).reshape(N_CHIP, -1, d)
            dgb, dgc, dh, dcw = _shortconv_bwd(f"shortconv_bwd_{l}", st['q'], dr, st['cw'])
            stack['sc_conv'][j] = dcw
            parts = [t2(dgb), t2(dgc), t2(dh)]
            du1 = t3(_mm_nn(f"sc_in_bwd_{l}", [(parts[k], (w_in_rows, 3 * j + k)) for k in range(3)], F32))
            units['sc_w_in'][j] = _cols_by_chip(jnp.concatenate(
                [_mm_tn(f"sc_in_dw_{k}_{l}", t2(st['u']), [parts[k]])[0] for k in range(3)], axis=1))
        upstream = (dz1, du1, st['x'], sc1)
        mixer = {0: ['pool_w'], 1: ['mla_w_a', 'mla_w_uq', 'mla_w_ukv', 'mla_w_o'], 2: ['sc_w_in', 'sc_w_out']}[kind]
        for n in mixer:
            units[n][j] = units[n][j].astype(BF16)
        cargo_a = [('ffn_w_up', l, units['ffn_w_up'][l])] if l > 0 else []
        cargo_b = [(n, j, units[n][j]) for n in mixer]
    grad_x, dmods[0][1], dmods[0][0] = _input_bwd("input_bwd", alpha, upstream[0], upstream[1], x, mods[0][1])

    for n, parts in stack.items():
        grads[n] = jnp.stack(parts)
    grads['ln_g'] = jnp.stack([jnp.concatenate(r, axis=0) for r in g_ln_g])
    grads['ln_b'] = jnp.stack([jnp.concatenate(r, axis=0) for r in g_ln_b])
    dmod_mine = jnp.stack([jnp.concatenate([t[:, 0, :] for t in dmods[l]], axis=-1) for l in range(depth)])

    small_grad_names = small_names + ['mla_q_norm', 'mla_kv_norm', 'ffn_conv_b']
    sg_pack, sg_spans = _pack_rows([dmod_mine] + [grads[n] for n in small_grad_names], F32, SUBLANE)
    rows_sg = sg_pack.shape[0]
    sg_all = _all_gather8("gather_small_grads", sg_pack, True).reshape(N_DEV, rows_sg, PACK_COLS)
    dmod_all = sg_all.reshape(N_DEV, -1)[:, :dmod_mine.size].reshape(N_DEV, depth, bsz, 6 * d)
    dmod_all = jnp.transpose(dmod_all, (1, 0, 2, 3)).reshape(depth, N_DEV * bsz, 6 * d)
    sg_sum = _sum8("sum_small_grads", sg_all).reshape(-1)
    for n, (off, shape) in zip(small_grad_names, sg_spans[1:]):
        g_full = sg_sum[off:off + _size(shape)].reshape(shape)
        if n in SMALL_SHARDED:
            ax = SMALL_SHARDED[n]
            width = shape[ax] // N_CHIP
            g_full = lax.dynamic_slice_in_dim(g_full, chip * width, width, axis=ax)
        grads[n] = g_full
    dmod_cols = lax.dynamic_slice_in_dim(dmod_all, chip * n_mod, n_mod, axis=2)
    grads['mod_w'], gb = _mod_bwd("mod_bwd", c_all, dmod_cols, dmod_all)
    grads['mod_b'] = gb[:, 0, :]

    keys = [(n, i) for n in big_names for i in range(len(units[n]))]
    last = [(n, i, units[n][i]) for n, i in keys if (n, i) not in received]
    landed(last, _scatter_grads("scatter_big_grads", [u for _, _, u in last]))
    chip_core = jnp.stack([chip, mc]).astype(jnp.int32)
    bufs = _swap_halves("swap_big_grad_halves",
                        [_sum8_into_half(f"sum_big_grads_{n}_{i}", units[n][i], received[(n, i)], chip_core)
                         for n, i in keys])
    for n in big_names:
        grads[n] = jnp.stack([b for (m, _), b in zip(keys, bufs) if m == n]).reshape(wts[n].shape)

    deltas, new_m, new_v = {}, {}, {}
    for n in WEIGHTS:
        deltas[n], new_m[n], new_v[n] = _adamw(f"adamw_{n}", wts[n], grads[n], mom1[n], mom2[n])
    return (loss, grad_x, *[grads[n] for n in WEIGHTS], *[deltas[n] for n in WEIGHTS],
            *[new_m[n] for n in WEIGHTS], *[new_v[n] for n in WEIGHTS])
```

```python
import functools

import jax
import jax.numpy as jnp
from jax import lax
from jax.experimental import pallas as pl
from jax.experimental.pallas import tpu as pltpu

F32 = jnp.float32
BF16 = jnp.bfloat16
MESH = pl.DeviceIdType.MESH

N_DEV = 8
N_CHIP = 4
LANE = 128
SUBLANE = 8
VMEM_LIMIT_BYTES = 56 * 2 ** 20
PACK_COLS = 1024

LN_EPS = 1e-5
RMS_EPS = 1e-6
QK_NOPE, QK_ROPE, V_HEAD = 64, 32, 64
ROPE_THETA = 10000.0
HEAD_PAD = 128
POOL_GROUPS = 4
POOL_HALO = 16
CONV_HALO = 8
CONV_ROWS = 1024
ADAM_LR, ADAM_B1, ADAM_B2, ADAM_EPS, ADAM_WD, ADAM_STEP = 0.001, 0.9, 0.999, 1e-08, 0.01, 10

WEIGHTS = ['mod_w', 'mod_b', 'ln_g', 'ln_b', 'pool_w', 'pool_scale', 'mla_w_a', 'mla_q_norm', 'mla_w_uq',
           'mla_kv_norm', 'mla_w_ukv', 'mla_w_o', 'sc_w_in', 'sc_conv', 'sc_w_out', 'ffn_w_up', 'ffn_conv',
           'ffn_conv_b', 'ffn_w_down']
BIG = {'pool_w': 2, 'mla_w_a': 2, 'mla_w_uq': 2, 'mla_w_ukv': 2, 'mla_w_o': 1, 'sc_w_in': 2, 'sc_w_out': 1,
       'ffn_w_up': 2, 'ffn_w_down': 1}
SMALL_SHARDED = {'ln_g': 2, 'ln_b': 2, 'pool_scale': 1, 'sc_conv': 2, 'ffn_conv': 2}
REPLICATED = ['mod_b', 'mla_q_norm', 'mla_kv_norm', 'ffn_conv_b']


def _pc(body, **kw):
    return pl.pallas_call(body, **kw)


def _cp(*sem):
    return pltpu.CompilerParams(dimension_semantics=sem, vmem_limit_bytes=VMEM_LIMIT_BYTES)


def _div(n, cap, mult):
    best = None
    for d in range(mult, min(n, cap) + 1, mult):
        if n % d == 0:
            best = d
    return best if best is not None else n


def _sds(shape, dtype):
    return jax.ShapeDtypeStruct(tuple(shape), dtype)


def _flip(v, bit):
    return 1 - v if bit else v


def _all_gather8(name, x_shard, in_vmem):
    m_per, n = x_shard.shape
    space = pltpu.VMEM if in_vmem else pltpu.HBM

    def body(x_ref, out_ref, send_sems, recv_sems, local_sem):
        x, y, c = lax.axis_index("x"), lax.axis_index("y"), lax.axis_index("c")
        me, sibling = (x, y, c), (x, y, 1 - c)
        chips = [(1 - x, y), (x, 1 - y), (1 - x, 1 - y)]

        def rows(px, py, pc_):
            return out_ref.at[pl.ds((4 * px + 2 * py + pc_) * m_per, m_per), :]

        def copy(k, block, to, src=None):
            return pltpu.make_async_remote_copy(
                src_ref=rows(*block) if src is None else src, dst_ref=rows(*block),
                send_sem=send_sems.at[k], recv_sem=recv_sems.at[k], device_id=to, device_id_type=MESH)

        mine = pltpu.make_async_copy(x_ref, rows(*me), local_sem)
        mine.start()
        first = [copy(0, me, sibling, src=x_ref)]
        first += [copy(1 + j, me, (*chip, c), src=x_ref) for j, chip in enumerate(chips)]
        for cp in first:
            cp.start()
        passed = [copy(4 + j, (*chip, c), sibling) for j, chip in enumerate(chips)]
        for j, chip in enumerate(chips):
            copy(1 + j, (*chip, c), me).wait_recv()
            passed[j].start()
        copy(0, sibling, me).wait_recv()
        for j, chip in enumerate(chips):
            copy(4 + j, (*chip, 1 - c), me).wait_recv()
        for cp in first + passed:
            cp.wait_send()
        mine.wait()

    return _pc(
        body, name=name, out_shape=_sds((N_DEV * m_per, n), x_shard.dtype),
        in_specs=[pl.BlockSpec(memory_space=space)], out_specs=pl.BlockSpec(memory_space=space),
        scratch_shapes=[pltpu.SemaphoreType.DMA((7,)), pltpu.SemaphoreType.DMA((7,)), pltpu.SemaphoreType.DMA],
        compiler_params=pltpu.CompilerParams(vmem_limit_bytes=VMEM_LIMIT_BYTES),
    )(x_shard)


def _gather_weights(name, shards):
    n_t = len(shards)
    halves = [s.shape[0] // 2 for s in shards]

    def body(*refs):
        x_refs, o_refs = refs[:n_t], refs[n_t:2 * n_t]
        send_sems, recv_sems, local_sems = refs[2 * n_t:]
        x, y, c = lax.axis_index("x"), lax.axis_index("y"), lax.axis_index("c")
        me, sibling = (x, y, c), (x, y, 1 - c)
        chips = [(1 - x, y), (x, 1 - y), (1 - x, 1 - y)]

        def slot(t, px, py, pc_):
            return o_refs[t].at[4 * px + 2 * py + pc_]

        def my_rows(t):
            return x_refs[t].at[pl.ds(c * halves[t], halves[t]), :]

        def copy(t, k, block, to, src=None):
            return pltpu.make_async_remote_copy(
                src_ref=slot(t, *block) if src is None else src, dst_ref=slot(t, *block),
                send_sem=send_sems.at[t, k], recv_sem=recv_sems.at[t, k], device_id=to, device_id_type=MESH)

        local = [pltpu.make_async_copy(my_rows(t), slot(t, *me), local_sems.at[t]) for t in range(n_t)]
        for cp in local:
            cp.start()
        first = []
        for t in range(n_t):
            first += [copy(t, 1 + j, me, (*chip, c), src=my_rows(t)) for j, chip in enumerate(chips)]
            first.append(copy(t, 0, me, sibling, src=my_rows(t)))
        for cp in first:
            cp.start()
        passed = []
        for j, chip in enumerate(chips):
            for t in range(n_t):
                copy(t, 1 + j, (*chip, c), me).wait_recv()
                passed.append(copy(t, 4 + j, (*chip, c), sibling))
                passed[-1].start()
        for t in range(n_t):
            copy(t, 0, sibling, me).wait_recv()
        for j, chip in enumerate(chips):
            for t in range(n_t):
                copy(t, 4 + j, (*chip, 1 - c), me).wait_recv()
        for cp in first + passed:
            cp.wait_send()
        for cp in local:
            cp.wait()

    hbm = pl.BlockSpec(memory_space=pltpu.HBM)
    return _pc(
        body, name=name, out_shape=tuple(_sds((N_DEV, h, s.shape[1]), s.dtype) for h, s in zip(halves, shards)),
        in_specs=[hbm] * n_t, out_specs=(hbm,) * n_t,
        scratch_shapes=[pltpu.SemaphoreType.DMA((n_t, 7)), pltpu.SemaphoreType.DMA((n_t, 7)),
                        pltpu.SemaphoreType.DMA((n_t,))],
    )(*shards)


def _scatter_copies(u_refs, r_refs, send_sems, recv_sems):
    x, y, c = lax.axis_index("x"), lax.axis_index("y"), lax.axis_index("c")
    copies = []
    for k in range(1, N_DEV):
        px, py, pcc = _flip(x, (k >> 2) & 1), _flip(y, (k >> 1) & 1), _flip(c, k & 1)
        for t, (u_ref, r_ref) in enumerate(zip(u_refs, r_refs)):
            h = u_ref.shape[1] // 2
            copies.append(pltpu.make_async_remote_copy(
                src_ref=u_ref.at[2 * px + py, pl.ds(pcc * h, h), :], dst_ref=r_ref.at[k - 1],
                send_sem=send_sems.at[t, k - 1], recv_sem=recv_sems.at[t, k - 1],
                device_id=(px, py, pcc), device_id_type=MESH))
    return copies


def _scatter_shapes(units):
    return tuple(_sds((N_DEV - 1, u.shape[1] // 2, u.shape[2]), u.dtype) for u in units)


def _scatter_grads(name, units):
    n_u = len(units)

    def body(*refs):
        copies = _scatter_copies(refs[:n_u], refs[n_u:2 * n_u], refs[2 * n_u], refs[2 * n_u + 1])
        for cp in copies:
            cp.start()
        for cp in copies:
            cp.wait()

    hbm = pl.BlockSpec(memory_space=pltpu.HBM)
    return _pc(body, name=name, out_shape=_scatter_shapes(units), in_specs=[hbm] * n_u, out_specs=(hbm,) * n_u,
               scratch_shapes=[pltpu.SemaphoreType.DMA((n_u, 7)), pltpu.SemaphoreType.DMA((n_u, 7))])(*units)


class _Scatter:
    peers = N_DEV - 1
    shapes = staticmethod(_scatter_shapes)

    @staticmethod
    def copies(u_refs, r_refs, send_sems, recv_sems):
        both = _scatter_copies(u_refs, r_refs, send_sems, recv_sems)
        return both, both


class _Fetch:
    peers = N_CHIP - 1

    @staticmethod
    def shapes(units):
        return tuple(_sds((N_CHIP,) + u.shape, u.dtype) for u in units)

    @staticmethod
    def copies(u_refs, r_refs, send_sems, recv_sems):
        x, y, c = lax.axis_index("x"), lax.axis_index("y"), lax.axis_index("c")
        sends, recvs = [], []
        for k in range(1, N_CHIP):
            px, py = _flip(x, (k >> 1) & 1), _flip(y, k & 1)
            for t, (u_ref, r_ref) in enumerate(zip(u_refs, r_refs)):
                sends.append(pltpu.make_async_remote_copy(
                    src_ref=u_ref, dst_ref=r_ref.at[2 * x + y], send_sem=send_sems.at[t, k - 1],
                    recv_sem=recv_sems.at[t, k - 1], device_id=(px, py, c), device_id_type=MESH))
                recvs.append(pltpu.make_async_remote_copy(
                    src_ref=u_ref, dst_ref=r_ref.at[2 * px + py], send_sem=send_sems.at[t, k - 1],
                    recv_sem=recv_sems.at[t, k - 1], device_id=(px, py, c), device_id_type=MESH))
        return sends, recvs


def _pc_cargo(body, cargo, *, name, grid, in_specs, out_specs, out_shape, scratch_shapes=(), route=_Scatter):
    out_specs, out_shape = tuple(out_specs), tuple(out_shape)
    if not cargo:
        return lambda *args: (_pc(body, name=name, grid=grid, in_specs=list(in_specs), out_specs=out_specs,
                                  out_shape=out_shape, scratch_shapes=list(scratch_shapes),
                                  compiler_params=_cp(*["arbitrary"] * len(grid)))(*args), ())
    n_in, n_out, n_u, n_s = len(in_specs), len(out_specs), len(cargo), len(scratch_shapes)

    def wrapped(*refs):
        ins, u_refs = refs[:n_in], refs[n_in:n_in + n_u]
        outs = refs[n_in + n_u:n_in + n_u + n_out]
        r_refs = refs[n_in + n_u + n_out:n_in + 2 * n_u + n_out]
        scratch = refs[n_in + 2 * n_u + n_out:n_in + 2 * n_u + n_out + n_s]
        send_sems, recv_sems = refs[-2:]
        first = last = None
        for axis, extent in enumerate(grid):
            at_start, at_end = pl.program_id(axis) == 0, pl.program_id(axis) == extent - 1
            first = at_start if first is None else first & at_start
            last = at_end if last is None else last & at_end

        @pl.when(first)
        def _():
            sends, _ = route.copies(u_refs, r_refs, send_sems, recv_sems)
            for cp in sends:
                cp.start()

        body(*ins, *outs, *scratch)

        @pl.when(last)
        def _():
            sends, recvs = route.copies(u_refs, r_refs, send_sems, recv_sems)
            for cp in recvs:
                cp.wait_recv()
            for cp in sends:
                cp.wait_send()

    hbm = pl.BlockSpec(memory_space=pltpu.HBM)
    sems = pltpu.SemaphoreType.DMA((n_u, route.peers))
    call = _pc(wrapped, name=name, grid=grid, in_specs=list(in_specs) + [hbm] * n_u, out_specs=out_specs + (hbm,) * n_u,
               out_shape=out_shape + route.shapes(cargo), scratch_shapes=list(scratch_shapes) + [sems, sems],
               compiler_params=_cp(*["arbitrary"] * len(grid)))

    def run(*args):
        res = call(*args, *cargo)
        return tuple(res[:n_out]), tuple(res[n_out:])
    return run


def _swap_halves(name, bufs):
    n_u = len(bufs)

    def body(*refs):
        o_refs = refs[n_u:2 * n_u]
        send_sems, recv_sems = refs[2 * n_u:]
        x, y, c = lax.axis_index("x"), lax.axis_index("y"), lax.axis_index("c")

        def rows(u, core):
            h = bufs[u].shape[0] // 2
            return o_refs[u].at[pl.ds(core * h, h), :]

        sends = [pltpu.make_async_remote_copy(src_ref=rows(u, c), dst_ref=rows(u, c), send_sem=send_sems.at[u],
                                              recv_sem=recv_sems.at[u], device_id=(x, y, 1 - c), device_id_type=MESH)
                 for u in range(n_u)]
        recvs = [pltpu.make_async_remote_copy(src_ref=rows(u, c), dst_ref=rows(u, 1 - c), send_sem=send_sems.at[u],
                                              recv_sem=recv_sems.at[u], device_id=(x, y, 1 - c), device_id_type=MESH)
                 for u in range(n_u)]
        for cp in sends:
            cp.start()
        for cp in recvs:
            cp.wait_recv()
        for cp in sends:
            cp.wait_send()

    hbm = pl.BlockSpec(memory_space=pltpu.HBM)
    return _pc(
        body, name=name, out_shape=tuple(_sds(b.shape, b.dtype) for b in bufs), in_specs=[hbm] * n_u,
        out_specs=(hbm,) * n_u, input_output_aliases={u: u for u in range(n_u)},
        scratch_shapes=[pltpu.SemaphoreType.DMA((n_u,)), pltpu.SemaphoreType.DMA((n_u,))],
    )(*bufs)


def _sum8_into_half(name, unit, received, chip_core):
    _, h, n = received.shape
    tm = _div(h, 256, 16)
    per = h // tm

    def body(cc_ref, u_ref, p_ref, o_ref):
        acc = u_ref[0].astype(F32)
        for s in range(N_DEV - 1):
            acc = acc + p_ref[s].astype(F32)
        o_ref[...] = acc

    grid_spec = pltpu.PrefetchScalarGridSpec(
        num_scalar_prefetch=1, grid=(per,),
        in_specs=[pl.BlockSpec((1, tm, n), lambda i, cc_ref: (cc_ref[0], cc_ref[1] * per + i, 0)),
                  pl.BlockSpec((N_DEV - 1, tm, n), lambda i, cc_ref: (0, i, 0))],
        out_specs=pl.BlockSpec((tm, n), lambda i, cc_ref: (cc_ref[1] * per + i, 0)))
    return _pc(body, name=name, grid_spec=grid_spec, out_shape=_sds((2 * h, n), F32),
               compiler_params=_cp("arbitrary"))(chip_core, unit, received)


def _sum8(name, parts):
    _, m, n = parts.shape
    tm = _div(m, 256, SUBLANE)

    def body(p_ref, o_ref):
        acc = p_ref[0]
        for s in range(1, N_DEV):
            acc = acc + p_ref[s]
        o_ref[...] = acc

    return _pc(body, name=name, grid=(m // tm,), out_shape=_sds((m, n), F32),
               in_specs=[pl.BlockSpec((N_DEV, tm, n), lambda i: (0, i, 0))],
               out_specs=pl.BlockSpec((tm, n), lambda i: (i, 0)), compiler_params=_cp("parallel"))(parts)


def _pack_rows(arrays, dtype, row_mult):
    flat, spans, off = [], [], 0
    for a in arrays:
        flat.append(a.reshape(-1).astype(dtype))
        spans.append((off, a.shape))
        off += a.size
    quantum = row_mult * PACK_COLS
    total = -(-off // quantum) * quantum
    if total > off:
        flat.append(jnp.zeros((total - off,), dtype))
    return jnp.concatenate(flat).reshape(total // PACK_COLS, PACK_COLS), spans


def _size(shape):
    n = 1
    for s in shape:
        n *= s
    return n


def _join_chips(blocks, axis):
    return jnp.concatenate([blocks[j] for j in range(N_CHIP)], axis=axis)


def _cols_by_chip(g):
    k, n = g.shape
    return jnp.transpose(g.reshape(k, N_CHIP, n // N_CHIP), (1, 0, 2))


def _mm_nn(name, pairs, out_dtype, cargo=(), tm_cap=1024, tn_cap=1536):
    m = pairs[0][0].shape[0]
    nb, _, n4 = pairs[0][1][0].shape
    tm, tn = _div(m, tm_cap, 16), _div(n4, tn_cap, LANE)
    per = n4 // tn
    n_pairs = len(pairs)

    def body(*refs):
        o_ref = refs[-1]
        acc = jnp.dot(refs[0][...], refs[1][0], preferred_element_type=F32)
        for i in range(1, n_pairs):
            acc = acc + jnp.dot(refs[2 * i][...], refs[2 * i + 1][0], preferred_element_type=F32)
        o_ref[...] = acc.astype(o_ref.dtype)

    in_specs, args = [], []
    for a, (w, r) in pairs:
        k = a.shape[1]
        assert w.shape[0] == nb and w.shape[2] == n4 and w.shape[1] % k == 0
        in_specs += [pl.BlockSpec((tm, k), lambda j, i: (i, 0)),
                     pl.BlockSpec((1, k, tn), functools.partial(lambda j, i, r_: (j // per, r_, j % per), r_=r))]
        args += [a, w]
    if cargo:
        (out,), received = _pc_cargo(body, cargo, name=name, grid=(nb * per, m // tm), in_specs=in_specs,
                                     out_shape=[_sds((m, nb * n4), out_dtype)],
                                     out_specs=[pl.BlockSpec((tm, tn), lambda j, i: (i, j))])(*args)
        return out, received
    return _pc(body, name=name, grid=(nb * per, m // tm), out_shape=_sds((m, nb * n4), out_dtype), in_specs=in_specs,
               out_specs=pl.BlockSpec((tm, tn), lambda j, i: (i, j)), compiler_params=_cp("parallel", "parallel"))(*args)


def _mm_tn(name, x, ys, n_blocks=1, out_dtype=F32, cargo=(), tt_cap=512):
    t, k = x.shape
    widths = [y.shape[1] for y in ys]
    n4 = sum(widths) // n_blocks
    common = n4
    for w in widths:
        common = _gcd(common, w)
    tk, tn, tt = _div(k, 1536, LANE), _div(common, 1536, LANE), _div(t, tt_cap, 16)
    per = n4 // tn
    starts, acc_w = [], 0
    for w in widths:
        starts.append(acc_w // tn)
        acc_w += w
    counts = [w // tn for w in widths]
    n_y = len(ys)

    def active(i, j):
        return (j >= starts[i]) & (j < starts[i] + counts[i])

    n_t = t // tt

    def body(*refs):
        x_ref, y_refs, o_ref, acc_ref = refs[0], refs[1:1 + n_y], refs[-2], refs[-1]
        j = pl.program_id(1)

        @pl.when(pl.program_id(2) == 0)
        def _():
            acc_ref[...] = jnp.zeros_like(acc_ref)

        for i in range(n_y):
            @pl.when(active(i, j))
            def _():
                acc_ref[...] += lax.dot_general(x_ref[...], y_refs[i][...], (((0,), (0,)), ((), ())),
                                                preferred_element_type=F32)

        @pl.when(pl.program_id(2) == n_t - 1)
        def _():
            o_ref[0] = acc_ref[...].astype(o_ref.dtype)

    def y_spec(i):
        def index(a, j, s):
            on = active(i, j)
            return jnp.where(on, s, 0), jnp.where(on, j - starts[i], 0)
        return pl.BlockSpec((tt, tn), index)

    (out,), received = _pc_cargo(
        body, cargo, name=name, grid=(k // tk, n_blocks * per, n_t), out_shape=[_sds((n_blocks, k, n4), out_dtype)],
        in_specs=[pl.BlockSpec((tt, tk), lambda a, j, s: (s, a))] + [y_spec(i) for i in range(n_y)],
        out_specs=[pl.BlockSpec((1, tk, tn), lambda a, j, s: (j // per, a, j % per))],
        scratch_shapes=[pltpu.VMEM((tk, tn), F32)])(x, *ys)
    return (out, received) if cargo else out


def _gcd(a, b):
    while b:
        a, b = b, a % b
    return a


def _w2(w):
    return (w[None], 0)


def _tok_spec(ts, d):
    return pl.BlockSpec((1, ts, d), lambda b, i: (b, i, 0))


def _seq_spec(d):
    return pl.BlockSpec((1, 1, d), lambda b, i: (b, 0, 0))


def _vec_spec(d):
    return pl.BlockSpec((1, d), lambda b, i: (0, 0))


def _ln_stats(z):
    mu = jnp.mean(z, axis=-1, keepdims=True)
    zc = z - mu
    var = jnp.mean(zc * zc, axis=-1, keepdims=True)
    rstd = lax.rsqrt(var + LN_EPS)
    return zc * rstd, rstd


def _modulate(name, x, sc, sh):
    b, s, d = x.shape
    ts = _div(s, 512, 16)

    def body(x_ref, sc_ref, sh_ref, u_ref):
        u_ref[0] = (x_ref[0] * (1.0 + sc_ref[0]) + sh_ref[0]).astype(BF16)

    return _pc(body, name=name, grid=(b, s // ts), out_shape=_sds(x.shape, BF16),
               in_specs=[_tok_spec(ts, d), _seq_spec(d), _seq_spec(d)], out_specs=_tok_spec(ts, d),
               compiler_params=_cp("parallel", "parallel"))(x, sc, sh)


def _ln_mod_fwd(name, alpha, x, y, g, lng, lnb, sc, sh):
    b, s, d = x.shape
    ts = _div(s, 512, 16)

    def body(x_ref, y_ref, g_ref, lng_ref, lnb_ref, sc_ref, sh_ref, z_ref, xn_ref, u_ref):
        z = alpha * x_ref[0] + (1.0 + g_ref[0]) * y_ref[0]
        xhat, _ = _ln_stats(z)
        xn = xhat * lng_ref[...] + lnb_ref[...]
        z_ref[0] = z
        xn_ref[0] = xn
        u_ref[0] = (xn * (1.0 + sc_ref[0]) + sh_ref[0]).astype(BF16)

    tok, seq, vec = _tok_spec(ts, d), _seq_spec(d), _vec_spec(d)
    return _pc(body, name=name, grid=(b, s // ts),
               out_shape=(_sds(x.shape, F32), _sds(x.shape, F32), _sds(x.shape, BF16)),
               in_specs=[tok, tok, seq, vec, vec, seq, seq], out_specs=(tok, tok, tok),
               compiler_params=_cp("parallel", "parallel"))(x, y, g, lng, lnb, sc, sh)


def _ln_loss_fwd(name, alpha, x, y, g, lng, lnb, target):
    b, s, d = x.shape
    ts = _div(s, 512, 16)

    def body(x_ref, y_ref, g_ref, lng_ref, lnb_ref, t_ref, z_ref, ct_ref, loss_ref):
        @pl.when((pl.program_id(0) == 0) & (pl.program_id(1) == 0))
        def _():
            loss_ref[...] = jnp.zeros_like(loss_ref)
        z = alpha * x_ref[0] + (1.0 + g_ref[0]) * y_ref[0]
        xhat, _ = _ln_stats(z)
        err = xhat * lng_ref[...] + lnb_ref[...] - t_ref[0]
        z_ref[0] = z
        ct_ref[0] = err / d
        part = 0.5 * jnp.sum(jnp.mean(err * err, axis=-1, keepdims=True))
        loss_ref[...] += jnp.full(loss_ref.shape, part, F32)

    tok, seq, vec = _tok_spec(ts, d), _seq_spec(d), _vec_spec(d)
    return _pc(body, name=name, grid=(b, s // ts),
               out_shape=(_sds(x.shape, F32), _sds(x.shape, F32), _sds((SUBLANE, LANE), F32)),
               in_specs=[tok, tok, seq, vec, vec, tok],
               out_specs=(tok, tok, pl.BlockSpec((SUBLANE, LANE), lambda b, i: (0, 0))),
               compiler_params=_cp("arbitrary", "arbitrary"))(x, y, g, lng, lnb, target)


def _sub_bwd(name, alpha, upstream, z, y, g, lng):
    b, s, d = z.shape
    ts = _div(s, 512, 16)
    last = len(upstream) == 1

    def body(*refs):
        if last:
            ct_ref, z_ref, y_ref, g_ref, lng_ref, dz_ref, dy_ref, dg_ref, dlng_ref, dlnb_ref = refs
        else:
            (dzn_ref, dun_ref, xn_ref, scn_ref, z_ref, y_ref, g_ref, lng_ref,
             dz_ref, dy_ref, dg_ref, dlng_ref, dlnb_ref, dsc_ref, dsh_ref) = refs
        first_tile = pl.program_id(1) == 0

        @pl.when(first_tile & (pl.program_id(0) == 0))
        def _():
            dlng_ref[...] = jnp.zeros_like(dlng_ref)
            dlnb_ref[...] = jnp.zeros_like(dlnb_ref)

        @pl.when(first_tile)
        def _():
            dg_ref[...] = jnp.zeros_like(dg_ref)
            if not last:
                dsc_ref[...] = jnp.zeros_like(dsc_ref)
                dsh_ref[...] = jnp.zeros_like(dsh_ref)

        if last:
            ct = ct_ref[0]
        else:
            dun = dun_ref[0]
            ct = alpha * dzn_ref[0] + dun * (1.0 + scn_ref[0])
            dsc_ref[0] += jnp.sum(dun * xn_ref[0], axis=0, keepdims=True)
            dsh_ref[0] += jnp.sum(dun, axis=0, keepdims=True)
        xhat, rstd = _ln_stats(z_ref[0])
        dlng_ref[...] += jnp.sum(ct * xhat, axis=0, keepdims=True)
        dlnb_ref[...] += jnp.sum(ct, axis=0, keepdims=True)
        dxhat = ct * lng_ref[...]
        dz = rstd * (dxhat - jnp.mean(dxhat, axis=-1, keepdims=True)
                     - xhat * jnp.mean(dxhat * xhat, axis=-1, keepdims=True))
        dz_ref[0] = dz
        dy_ref[0] = ((1.0 + g_ref[0]) * dz).astype(BF16)
        dg_ref[0] += jnp.sum(dz * y_ref[0], axis=0, keepdims=True)

    tok, seq, vec = _tok_spec(ts, d), _seq_spec(d), _vec_spec(d)
    seq_out = _sds((b, 1, d), F32)
    out_shape = [_sds(z.shape, F32), _sds(z.shape, BF16), seq_out, _sds((1, d), F32), _sds((1, d), F32)]
    out_specs = [tok, tok, seq, vec, vec]
    if last:
        in_specs = [tok, tok, tok, seq, vec]
    else:
        in_specs = [tok, tok, tok, seq, tok, tok, seq, vec]
        out_shape += [seq_out, seq_out]
        out_specs += [seq, seq]
    return _pc(body, name=name, grid=(b, s // ts), out_shape=tuple(out_shape), in_specs=in_specs,
               out_specs=tuple(out_specs), compiler_params=_cp("arbitrary", "arbitrary"))(*upstream, z, y, g, lng)


def _input_bwd(name, alpha, dz, du, x, sc):
    b, s, d = x.shape
    ts = _div(s, 512, 16)

    def body(dz_ref, du_ref, x_ref, sc_ref, gx_ref, dsc_ref, dsh_ref):
        @pl.when(pl.program_id(1) == 0)
        def _():
            dsc_ref[...] = jnp.zeros_like(dsc_ref)
            dsh_ref[...] = jnp.zeros_like(dsh_ref)
        du_ = du_ref[0]
        gx_ref[0] = alpha * dz_ref[0] + du_ * (1.0 + sc_ref[0])
        dsc_ref[0] += jnp.sum(du_ * x_ref[0], axis=0, keepdims=True)
        dsh_ref[0] += jnp.sum(du_, axis=0, keepdims=True)

    tok, seq = _tok_spec(ts, d), _seq_spec(d)
    seq_out = _sds((b, 1, d), F32)
    return _pc(body, name=name, grid=(b, s // ts), out_shape=(_sds(x.shape, F32), seq_out, seq_out),
               in_specs=[tok, tok, tok, seq], out_specs=(tok, seq, seq),
               compiler_params=_cp("parallel", "arbitrary"))(dz, du, x, sc)


def _rows_iota(shape):
    return lax.broadcasted_iota(jnp.int32, shape, 0)


def _back(v, k):
    return pltpu.roll(v, k, axis=0)


def _ahead(v, k):
    return pltpu.roll(v, v.shape[0] - k, axis=0)


def _conv3(ext, w_ref):
    return w_ref[2:3, :] * ext + w_ref[1:2, :] * _back(ext, 1) + w_ref[0:1, :] * _back(ext, 2)


def _conv3_t(dh_ext, w_ref):
    return w_ref[2:3, :] * dh_ext + w_ref[1:2, :] * _ahead(dh_ext, 1) + w_ref[0:1, :] * _ahead(dh_ext, 2)


def _flag(cond):
    return jnp.where(cond, 1.0, 0.0).astype(F32)


def _sigmoid(v):
    return 1.0 / (1.0 + jnp.exp(-v))


def _halo_specs(ts, tc, halo, n_s, col):
    per = ts // halo
    tile = pl.BlockSpec((1, ts, tc), lambda b, i, j: (b, i, col(j)))
    prev = pl.BlockSpec((1, halo, tc), lambda b, i, j: (b, jnp.maximum(i * per - 1, 0), col(j)))
    nxt = pl.BlockSpec((1, halo, tc), lambda b, i, j: (b, jnp.minimum((i + 1) * per, n_s * per - 1), col(j)))
    return tile, prev, nxt


def _convglu_fwd(name, p, cw, cb):
    b, s, f2 = p.shape
    f = f2 // 2
    ts, tc = _div(s, CONV_ROWS, CONV_HALO), _div(f, 256, LANE)
    n_s, n_c = s // ts, f // tc

    def body(pv_ref, pvh_ref, pg_ref, pgh_ref, wv_ref, wg_ref, bv_ref, bg_ref, a_ref):
        keep = _flag(pl.program_id(1) > 0)

        def conv(t_ref, h_ref, w_ref, b_ref):
            ext = jnp.concatenate([h_ref[0] * keep, t_ref[0]], axis=0)
            return _conv3(ext, w_ref)[CONV_HALO:] + b_ref[...]

        val = conv(pv_ref, pvh_ref, wv_ref, bv_ref)
        gate = conv(pg_ref, pgh_ref, wg_ref, bg_ref)
        a_ref[0] = (gate * _sigmoid(gate) * val).astype(BF16)

    tv, hv, _ = _halo_specs(ts, tc, CONV_HALO, n_s, lambda j: j)
    tg, hg, _ = _halo_specs(ts, tc, CONV_HALO, n_s, lambda j: j + n_c)
    wv = pl.BlockSpec((3, tc), lambda b, i, j: (0, j))
    wg = pl.BlockSpec((3, tc), lambda b, i, j: (0, j + n_c))
    bv = pl.BlockSpec((1, tc), lambda b, i, j: (0, j))
    bg = pl.BlockSpec((1, tc), lambda b, i, j: (0, j + n_c))
    return _pc(body, name=name, grid=(b, n_s, n_c), out_shape=_sds((b, s, f), BF16),
               in_specs=[tv, hv, tg, hg, wv, wg, bv, bg], out_specs=pl.BlockSpec((1, ts, tc), lambda b, i, j: (b, i, j)),
               compiler_params=_cp("parallel", "parallel", "parallel"))(p, p, p, p, cw, cw, cb, cb)


def _convglu_bwd(name, p, da, cw, cb, cargo=()):
    b, s, f2 = p.shape
    f = f2 // 2
    ts, tc = _div(s, CONV_ROWS, CONV_HALO), _div(f, 256, LANE)
    n_s, n_c = s // ts, f // tc

    def body(pv_ref, pvp_ref, pvn_ref, pg_ref, pgp_ref, pgn_ref, da_ref, dan_ref, wv_ref, wg_ref, bv_ref, bg_ref,
             dpv_ref, dpg_ref, dwv_ref, dwg_ref, dbv_ref, dbg_ref):
        bi, i = pl.program_id(1), pl.program_id(2)

        @pl.when((bi == 0) & (i == 0))
        def _():
            for r in (dwv_ref, dwg_ref, dbv_ref, dbg_ref):
                r[...] = jnp.zeros_like(r)

        keep_prev = _flag(i > 0)
        keep_next = _flag(i < n_s - 1)
        pv_ext = jnp.concatenate([pvp_ref[0] * keep_prev, pv_ref[0], pvn_ref[0]], axis=0)
        pg_ext = jnp.concatenate([pgp_ref[0] * keep_prev, pg_ref[0], pgn_ref[0]], axis=0)
        taps_v = (_back(pv_ext, 2), _back(pv_ext, 1), pv_ext)
        taps_g = (_back(pg_ext, 2), _back(pg_ext, 1), pg_ext)

        def conv(taps, w_ref, b_ref):
            return (w_ref[2:3, :] * taps[2] + w_ref[1:2, :] * taps[1] + w_ref[0:1, :] * taps[0])[CONV_HALO:] + b_ref[...]

        val, gate = conv(taps_v, wv_ref, bv_ref), conv(taps_g, wg_ref, bg_ref)
        da_ext = jnp.concatenate([da_ref[0], dan_ref[0] * keep_next], axis=0)
        sg = _sigmoid(gate)
        dval = da_ext * gate * sg
        dgate = da_ext * val * (sg * (1.0 + gate * (1.0 - sg)))
        dpv_ref[0] = _conv3_t(dval, wv_ref)[:ts].astype(BF16)
        dpg_ref[0] = _conv3_t(dgate, wg_ref)[:ts].astype(BF16)
        for dh, taps, dw_ref, db_ref in ((dval[:ts], taps_v, dwv_ref, dbv_ref), (dgate[:ts], taps_g, dwg_ref, dbg_ref)):
            db_ref[...] += jnp.sum(dh, axis=0, keepdims=True)
            for k in range(3):
                dw_ref[k:k + 1, :] += jnp.sum(dh * taps[k][CONV_HALO:CONV_HALO + ts], axis=0, keepdims=True)

    def specs(col):
        per = ts // CONV_HALO
        tile = pl.BlockSpec((1, ts, tc), lambda j, b, i: (b, i, col(j)))
        prev = pl.BlockSpec((1, CONV_HALO, tc), lambda j, b, i: (b, jnp.maximum(i * per - 1, 0), col(j)))
        nxt = pl.BlockSpec((1, CONV_HALO, tc), lambda j, b, i: (b, jnp.minimum((i + 1) * per, n_s * per - 1), col(j)))
        return tile, prev, nxt

    tv, pvp, pvn = specs(lambda j: j)
    tg, pgp, pgn = specs(lambda j: j + n_c)
    wv = pl.BlockSpec((3, tc), lambda j, b, i: (0, j))
    wg = pl.BlockSpec((3, tc), lambda j, b, i: (0, j + n_c))
    bv = pl.BlockSpec((1, tc), lambda j, b, i: (0, j))
    bg = pl.BlockSpec((1, tc), lambda j, b, i: (0, j + n_c))
    out_tile = pl.BlockSpec((1, ts, tc), lambda j, b, i: (b, i, j))
    acc3, acc1 = pl.BlockSpec((3, tc), lambda j, b, i: (0, j)), pl.BlockSpec((1, tc), lambda j, b, i: (0, j))
    (dpv, dpg, dwv, dwg, dbv, dbg), received = _pc_cargo(
        body, cargo, name=name, grid=(n_c, b, n_s),
        out_shape=(_sds((b, s, f), BF16), _sds((b, s, f), BF16), _sds((3, f), F32), _sds((3, f), F32),
                   _sds((1, f), F32), _sds((1, f), F32)),
        in_specs=[tv, pvp, pvn, tg, pgp, pgn, tv, pvn, wv, wg, bv, bg],
        out_specs=(out_tile, out_tile, acc3, acc3, acc1, acc1))(p, p, p, p, p, p, da, da, cw, cw, cb, cb)
    return dpv, dpg, jnp.concatenate([dwv, dwg], axis=1), jnp.concatenate([dbv, dbg], axis=1), received


def _shortconv_fwd(name, q, cw):
    b, s, d3 = q.shape
    d = d3 // 3
    ts, tc = _div(s, CONV_ROWS, CONV_HALO), _div(d, 256, LANE)
    n_s, n_c = s // ts, d // tc

    def body(gb_ref, gc_ref, gch_ref, h_ref, hh_ref, w_ref, r_ref):
        keep = _flag(pl.program_id(1) > 0)
        m_ext = jnp.concatenate([gch_ref[0] * hh_ref[0] * keep, gc_ref[0] * h_ref[0]], axis=0)
        r_ref[0] = (gb_ref[0] * _conv3(m_ext, w_ref)[CONV_HALO:]).astype(BF16)

    tb, _, _ = _halo_specs(ts, tc, CONV_HALO, n_s, lambda j: j)
    tcc, hc, _ = _halo_specs(ts, tc, CONV_HALO, n_s, lambda j: j + n_c)
    th, hh, _ = _halo_specs(ts, tc, CONV_HALO, n_s, lambda j: j + 2 * n_c)
    w = pl.BlockSpec((3, tc), lambda b, i, j: (0, j))
    return _pc(body, name=name, grid=(b, n_s, n_c), out_shape=_sds((b, s, d), BF16),
               in_specs=[tb, tcc, hc, th, hh, w], out_specs=pl.BlockSpec((1, ts, tc), lambda b, i, j: (b, i, j)),
               compiler_params=_cp("parallel", "parallel", "parallel"))(q, q, q, q, q, cw)


def _shortconv_bwd(name, q, dr, cw):
    b, s, d3 = q.shape
    d = d3 // 3
    ts, tc = _div(s, CONV_ROWS, CONV_HALO), _div(d, 256, LANE)
    n_s, n_c = s // ts, d // tc

    def body(gb_ref, gbn_ref, gc_ref, gcp_ref, h_ref, hp_ref, dr_ref, drn_ref, w_ref,
             dgb_ref, dgc_ref, dh_ref, dw_ref):
        bi, i = pl.program_id(1), pl.program_id(2)

        @pl.when((bi == 0) & (i == 0))
        def _():
            dw_ref[...] = jnp.zeros_like(dw_ref)

        keep_prev = _flag(i > 0)
        keep_next = _flag(i < n_s - 1)
        gc, h = gc_ref[0], h_ref[0]
        m_ext = jnp.concatenate([gcp_ref[0] * hp_ref[0] * keep_prev, gc * h], axis=0)
        cm = _conv3(m_ext, w_ref)[CONV_HALO:]
        dr_ = dr_ref[0]
        dgb_ref[0] = (dr_ * cm).astype(BF16)
        dcv_ext = jnp.concatenate([dr_ * gb_ref[0], drn_ref[0] * gbn_ref[0] * keep_next], axis=0)
        dm = _conv3_t(dcv_ext, w_ref)[:ts]
        dgc_ref[0] = (dm * h).astype(BF16)
        dh_ref[0] = (dm * gc).astype(BF16)
        dcv = dcv_ext[:ts]
        for k in range(3):
            shifted = m_ext if k == 2 else _back(m_ext, 2 - k)
            dw_ref[k:k + 1, :] += jnp.sum(dcv * shifted[CONV_HALO:], axis=0, keepdims=True)

    def specs(col):
        per = ts // CONV_HALO
        tile = pl.BlockSpec((1, ts, tc), lambda j, b, i: (b, i, col(j)))
        prev = pl.BlockSpec((1, CONV_HALO, tc), lambda j, b, i: (b, jnp.maximum(i * per - 1, 0), col(j)))
        nxt = pl.BlockSpec((1, CONV_HALO, tc), lambda j, b, i: (b, jnp.minimum((i + 1) * per, n_s * per - 1), col(j)))
        return tile, prev, nxt

    tb, _, nb = specs(lambda j: j)
    tcc, pc_, _ = specs(lambda j: j + n_c)
    th, ph, _ = specs(lambda j: j + 2 * n_c)
    w = pl.BlockSpec((3, tc), lambda j, b, i: (0, j))
    out_tile = pl.BlockSpec((1, ts, tc), lambda j, b, i: (b, i, j))
    o = _sds((b, s, d), BF16)
    return _pc(body, name=name, grid=(n_c, b, n_s), out_shape=(o, o, o, _sds((3, d), F32)),
               in_specs=[tb, nb, tcc, pc_, th, ph, tb, nb, w], out_specs=(out_tile, out_tile, out_tile, w),
               compiler_params=_cp("parallel", "arbitrary", "arbitrary"))(q, q, q, q, q, q, dr, dr, cw)


def _pick_window(group, cands):
    gid = jnp.full(cands[0].shape, group, jnp.int32)
    out = cands[-1]
    for k in range(len(cands) - 2, -1, -1):
        out = jnp.where(gid == k, cands[k], out)
    return out


def _window_sums(v, shift):
    s1 = v + shift(v, 1)
    s2 = s1 + shift(s1, 2)
    s3 = s2 + shift(s2, 4)
    s4 = s3 + shift(s3, 8)
    return [s1, s2, s3, s4]


def _pool_counts(group, first_row, n_rows, cols):
    t = _rows_iota((n_rows, cols)) + first_row
    window = _pick_window(group, [jnp.full((n_rows, cols), 2 << k, jnp.int32) for k in range(POOL_GROUPS)])
    return jnp.minimum(t + 1, window).astype(F32)


def _pool_fwd(name, x, sc, sh, w, scale):
    b, s, d = x.shape
    tc = d // POOL_GROUPS
    ts = _div(s, 512, POOL_HALO)
    n_s = s // ts

    def body(x_ref, xp_ref, sc_ref, sh_ref, w_ref, scale_ref, y_ref):
        i, grp = pl.program_id(1), pl.program_id(2)
        keep = _flag(i > 0)
        mod = 1.0 + sc_ref[0]
        u = x_ref[0] * mod + sh_ref[0]
        u_ext = jnp.concatenate([(xp_ref[0] * mod + sh_ref[0]) * keep, u], axis=0)
        summed = _pick_window(grp, _window_sums(u_ext, _back))[POOL_HALO:]
        pooled = summed / _pool_counts(grp, i * ts, ts, tc) - u
        y_ref[0] = jnp.dot(pooled.astype(BF16), w_ref[0], preferred_element_type=F32) * scale_ref[...]

    tile, prev, _ = _halo_specs(ts, tc, POOL_HALO, n_s, lambda j: j)
    seq = pl.BlockSpec((1, 1, tc), lambda b, i, j: (b, 0, j))
    return _pc(body, name=name, grid=(b, n_s, POOL_GROUPS), out_shape=_sds(x.shape, F32),
               in_specs=[tile, prev, seq, seq, pl.BlockSpec((1, tc, tc), lambda b, i, j: (j, 0, 0)),
                         pl.BlockSpec((1, tc), lambda b, i, j: (0, j))],
               out_specs=pl.BlockSpec((1, ts, tc), lambda b, i, j: (b, i, j)),
               compiler_params=_cp("parallel", "parallel", "parallel"))(x, x, sc, sh, w, scale)


def _pool_bwd(name, x, sc, sh, dy, w, w_t, scale, cargo=()):
    b, s, d = x.shape
    tc = d // POOL_GROUPS
    ts = _div(s, 512, POOL_HALO)
    n_s = s // ts

    def body(x_ref, xp_ref, sc_ref, sh_ref, dy_ref, dyn_ref, w_ref, wt_ref, scale_ref, du_ref, dw_ref, dscale_ref):
        grp, bi, i = pl.program_id(0), pl.program_id(1), pl.program_id(2)

        @pl.when((bi == 0) & (i == 0))
        def _():
            dw_ref[...] = jnp.zeros_like(dw_ref)
            dscale_ref[...] = jnp.zeros_like(dscale_ref)

        keep_prev = _flag(i > 0)
        keep_next = _flag(i < n_s - 1)
        mod = 1.0 + sc_ref[0]
        u = x_ref[0] * mod + sh_ref[0]
        u_ext = jnp.concatenate([(xp_ref[0] * mod + sh_ref[0]) * keep_prev, u], axis=0)
        summed = _pick_window(grp, _window_sums(u_ext, _back))[POOL_HALO:]
        pooled = (summed / _pool_counts(grp, i * ts, ts, tc) - u).astype(BF16)
        dy_ = dy_ref[0].astype(F32)
        ymat = jnp.dot(pooled, w_ref[0], preferred_element_type=F32)
        dscale_ref[...] += jnp.sum(dy_ * ymat, axis=0, keepdims=True)
        dys_ext = (jnp.concatenate([dy_, dyn_ref[0].astype(F32) * keep_next], axis=0) * scale_ref[...]).astype(BF16)
        dw_ref[0] += lax.dot_general(pooled, dys_ext[:ts], (((0,), (0,)), ((), ())), preferred_element_type=F32)
        dpooled = jnp.dot(dys_ext, wt_ref[0], preferred_element_type=F32)
        e = dpooled / _pool_counts(grp, i * ts, ts + POOL_HALO, tc)
        du_ref[0] = _pick_window(grp, _window_sums(e, _ahead))[:ts] - dpooled[:ts]

    per = ts // POOL_HALO
    tile = pl.BlockSpec((1, ts, tc), lambda j, b, i: (b, i, j))
    prev = pl.BlockSpec((1, POOL_HALO, tc), lambda j, b, i: (b, jnp.maximum(i * per - 1, 0), j))
    nxt = pl.BlockSpec((1, POOL_HALO, tc), lambda j, b, i: (b, jnp.minimum((i + 1) * per, n_s * per - 1), j))
    seq = pl.BlockSpec((1, 1, tc), lambda j, b, i: (b, 0, j))
    wsp = pl.BlockSpec((1, tc, tc), lambda j, b, i: (j, 0, 0))
    vec = pl.BlockSpec((1, tc), lambda j, b, i: (0, j))
    (du, dw, dscale), received = _pc_cargo(
        body, cargo, name=name, grid=(POOL_GROUPS, b, n_s),
        out_shape=(_sds(x.shape, F32), _sds((POOL_GROUPS, tc, tc), F32), _sds((1, d), F32)),
        in_specs=[tile, prev, seq, seq, tile, nxt, wsp, wsp, vec],
        out_specs=(tile, wsp, vec))(x, x, sc, sh, dy, dy, w, w_t, scale)
    return du, dw, dscale, received


def _rope_swap(v):
    lane = lax.broadcasted_iota(jnp.int32, v.shape, v.ndim - 1)
    lo, hi = QK_NOPE, QK_NOPE + QK_ROPE // 2
    from_above = pltpu.roll(v, HEAD_PAD - QK_ROPE // 2, axis=v.ndim - 1)
    from_below = pltpu.roll(v, QK_ROPE // 2, axis=v.ndim - 1)
    return jnp.where((lane >= lo) & (lane < hi), from_above,
                     jnp.where((lane >= hi) & (lane < hi + QK_ROPE // 2), from_below, 0.0))


def _rope(v, cos_t, sin_t):
    return v * cos_t + _rope_swap(v) * sin_t


def _rope_t(dv, cos_t, sin_t):
    return dv * cos_t + _rope_swap(dv * sin_t)


def _rms(v, g):
    r = lax.rsqrt(jnp.mean(v * v, axis=-1, keepdims=True) + RMS_EPS)
    return v * r, r


def _mla_norm_fwd(name, a, qn, kvn, cos_t, sin_t):
    b, s, wa = a.shape
    ql, kvl = qn.shape[1], kvn.shape[1]
    ts = _div(s, 512, 16)

    def body(aq_ref, akv_ref, ape_ref, qn_ref, kvn_ref, cos_ref, sin_ref, cq_ref, ckv_ref, kpe_ref):
        yq, _ = _rms(aq_ref[0], None)
        cq_ref[0] = (yq * qn_ref[...]).astype(BF16)
        ykv, _ = _rms(akv_ref[0], None)
        ckv_ref[0] = (ykv * kvn_ref[...]).astype(BF16)
        kpe_ref[0] = _rope(ape_ref[0], cos_ref[0], sin_ref[0])

    tok = lambda w, col: pl.BlockSpec((1, ts, w), lambda b, i: (b, i, col))
    return _pc(body, name=name, grid=(b, s // ts),
               out_shape=(_sds((b, s, ql), BF16), _sds((b, s, kvl), BF16), _sds((b, s, HEAD_PAD), F32)),
               in_specs=[tok(ql, 0), tok(kvl, ql // kvl), tok(HEAD_PAD, (ql + kvl) // HEAD_PAD), _vec_spec(ql),
                         _vec_spec(kvl), tok(HEAD_PAD, 0), tok(HEAD_PAD, 0)],
               out_specs=(tok(ql, 0), tok(kvl, 0), tok(HEAD_PAD, 0)),
               compiler_params=_cp("parallel", "parallel"))(a, a, a, qn, kvn, cos_t, sin_t)


def _mla_norm_bwd(name, a, dcq, dckv, dkpe, qn, kvn):
    b, s, wa = a.shape
    ql, kvl = qn.shape[1], kvn.shape[1]
    ts = _div(s, 512, 16)

    def body(a_ref, dcq_ref, dckv_ref, dkpe_ref, qn_ref, kvn_ref, da_ref, dqn_ref, dkvn_ref):
        @pl.when((pl.program_id(0) == 0) & (pl.program_id(1) == 0))
        def _():
            dqn_ref[...] = jnp.zeros_like(dqn_ref)
            dkvn_ref[...] = jnp.zeros_like(dkvn_ref)

        def one(v, dc, g_ref, dg_ref):
            yv, r = _rms(v, None)
            dg_ref[...] += jnp.sum(dc * yv, axis=0, keepdims=True)
            dyv = dc * g_ref[...]
            return r * (dyv - yv * jnp.mean(dyv * yv, axis=-1, keepdims=True))

        av = a_ref[0]
        da_ref[0, :, 0:ql] = one(av[:, 0:ql], dcq_ref[0], qn_ref, dqn_ref).astype(BF16)
        da_ref[0, :, ql:ql + kvl] = one(av[:, ql:ql + kvl], dckv_ref[0], kvn_ref, dkvn_ref).astype(BF16)
        da_ref[0, :, ql + kvl:] = dkpe_ref[0].astype(BF16)

    return _pc(body, name=name, grid=(b, s // ts),
               out_shape=(_sds(a.shape, BF16), _sds((1, ql), F32), _sds((1, kvl), F32)),
               in_specs=[_tok_spec(ts, wa), _tok_spec(ts, ql), _tok_spec(ts, kvl), _tok_spec(ts, HEAD_PAD),
                         _vec_spec(ql), _vec_spec(kvl)],
               out_specs=(_tok_spec(ts, wa), _vec_spec(ql), _vec_spec(kvl)),
               compiler_params=_cp("arbitrary", "arbitrary"))(a, dcq, dckv, dkpe, qn, kvn)


def _mla_prep_fwd(name, q_raw, kv_raw, kpe, cos_t, sin_t, n_heads):
    b, s, wq = q_raw.shape
    ts = _div(s, 256, 16)

    def body(q_ref, k_ref, v_ref, kpe_ref, cos_ref, sin_ref, qo_ref, ko_ref, vo_ref):
        cos_, sin_, kpe_ = cos_ref[0], sin_ref[0], kpe_ref[0]
        for h in range(n_heads):
            lanes = slice(h * HEAD_PAD, (h + 1) * HEAD_PAD)
            qo_ref[0, :, lanes] = _rope(q_ref[0, :, lanes], cos_, sin_).astype(BF16)
            ko_ref[0, :, lanes] = (k_ref[0, :, lanes] + kpe_).astype(BF16)
        vo_ref[0] = v_ref[0].astype(BF16)

    wide = lambda part: pl.BlockSpec((1, ts, wq), lambda b, i: (b, i, part))
    tok = pl.BlockSpec((1, ts, HEAD_PAD), lambda b, i: (b, i, 0))
    o = _sds(q_raw.shape, BF16)
    return _pc(body, name=name, grid=(b, s // ts), out_shape=(o, o, o),
               in_specs=[wide(0), wide(0), wide(1), tok, tok, tok], out_specs=(wide(0), wide(0), wide(0)),
               compiler_params=_cp("parallel", "parallel"))(q_raw, kv_raw, kv_raw, kpe, cos_t, sin_t)


def _mla_prep_bwd(name, dq, dk, dv, cos_t, sin_t, n_heads):
    b, s, wq = dq.shape
    ts = _div(s, 256, 16)

    def body(dq_ref, dk_ref, dv_ref, cos_ref, sin_ref, dqr_ref, dkv_ref, dkpe_ref):
        cos_, sin_ = cos_ref[0], sin_ref[0]
        dk_sum = None
        for h in range(n_heads):
            lanes = slice(h * HEAD_PAD, (h + 1) * HEAD_PAD)
            dqr_ref[0, :, lanes] = _rope_t(dq_ref[0, :, lanes], cos_, sin_).astype(BF16)
            dk_h = dk_ref[0, :, lanes]
            dkv_ref[0, :, lanes] = dk_h.astype(BF16)
            dk_sum = dk_h if dk_sum is None else dk_sum + dk_h
        dkv_ref[0, :, wq:2 * wq] = dv_ref[0]
        dkpe_ref[0] = _rope_t(dk_sum, cos_, sin_)

    wide = pl.BlockSpec((1, ts, wq), lambda b, i: (b, i, 0))
    both = pl.BlockSpec((1, ts, 2 * wq), lambda b, i: (b, i, 0))
    tok = pl.BlockSpec((1, ts, HEAD_PAD), lambda b, i: (b, i, 0))
    return _pc(body, name=name, grid=(b, s // ts),
               out_shape=(_sds(dq.shape, BF16), _sds((b, s, 2 * wq), BF16), _sds((b, s, HEAD_PAD), F32)),
               in_specs=[wide, wide, wide, tok, tok], out_specs=(wide, both, tok),
               compiler_params=_cp("parallel", "parallel"))(dq, dk, dv, cos_t, sin_t)


FLASH_TILE = 1024
LOG2_E = 1.4426950408889634


def _heads_per_step(n_heads):
    return 2 if n_heads % 2 == 0 else 1


def _causal_mask(i, j, tq, tk):
    rows = lax.broadcasted_iota(jnp.int32, (tq, tk), 0) + i * tq
    cols = lax.broadcasted_iota(jnp.int32, (tq, tk), 1) + j * tk
    return cols <= rows


def _nt(a, b):
    return lax.dot_general(a, b, (((1,), (1,)), ((), ())), preferred_element_type=F32)


def _tn(a, b):
    return lax.dot_general(a, b, (((0,), (0,)), ((), ())), preferred_element_type=F32)


def _flash_fwd(name, q, k, v, n_heads, sm_scale, cargo=()):
    b, s, _ = q.shape
    t, hp = _div(s, FLASH_TILE, LANE), _heads_per_step(n_heads)
    n, w = s // t, hp * HEAD_PAD
    neg = float(jnp.finfo(jnp.float32).min)
    c2 = sm_scale * LOG2_E

    def body(q_ref, k_ref, v_ref, o_ref, lse_ref, m_ref, l_ref, acc_ref):
        i, j = pl.program_id(2), pl.program_id(3)

        @pl.when(j == 0)
        def _():
            m_ref[...] = jnp.full(m_ref.shape, neg, F32)
            l_ref[...] = jnp.zeros_like(l_ref)
            acc_ref[...] = jnp.zeros_like(acc_ref)

        def block(on_diagonal):
            for hh in range(hp):
                ln = slice(hh * HEAD_PAD, (hh + 1) * HEAD_PAD)
                sc = _nt(q_ref[0, :, ln], k_ref[0, :, ln])
                if on_diagonal:
                    sc = jnp.where(_causal_mask(i, j, t, t), sc, neg)
                m_old = m_ref[hh]
                m_new = jnp.maximum(m_old, jnp.max(sc, axis=-1, keepdims=True))
                p = jnp.exp2((sc - m_new) * c2)
                corr = jnp.exp2((m_old - m_new) * c2)
                l_ref[hh] = corr * l_ref[hh] + jnp.sum(p, axis=-1, keepdims=True)
                acc_ref[:, ln] = corr * acc_ref[:, ln] + jnp.dot(p.astype(BF16), v_ref[0, :, ln],
                                                                 preferred_element_type=F32)
                m_ref[hh] = m_new

        pl.when(j < i)(functools.partial(block, False))
        pl.when(j == i)(functools.partial(block, True))

        @pl.when(j == n - 1)
        def _():
            for hh in range(hp):
                ln = slice(hh * HEAD_PAD, (hh + 1) * HEAD_PAD)
                o_ref[0, :, ln] = (acc_ref[:, ln] / l_ref[hh]).astype(BF16)
                lse_ref[0, :, ln] = jnp.broadcast_to(m_ref[hh] * sm_scale + jnp.log(l_ref[hh]), (t, HEAD_PAD))

    qs = pl.BlockSpec((1, t, w), lambda b, h, i, j: (b, i, h))
    ks = pl.BlockSpec((1, t, w), lambda b, h, i, j: (b, jnp.minimum(j, i), h))
    (o, lse), fetched = _pc_cargo(
        body, cargo, name=name, grid=(b, n_heads // hp, n, n), route=_Fetch,
        out_shape=(_sds(q.shape, BF16), _sds(q.shape, F32)), in_specs=[qs, ks, ks], out_specs=(qs, qs),
        scratch_shapes=[pltpu.VMEM((hp, t, 1), F32), pltpu.VMEM((hp, t, 1), F32), pltpu.VMEM((t, w), F32)])(q, k, v)
    return o, lse, fetched


def _flash_dq(name, q, k, v, o, lse, do, n_heads, sm_scale):
    b, s, _ = q.shape
    t, hp = _div(s, FLASH_TILE, LANE), _heads_per_step(n_heads)
    n, w = s // t, hp * HEAD_PAD
    c2 = sm_scale * LOG2_E

    def body(q_ref, k_ref, v_ref, o_ref, lse_ref, do_ref, dq_ref, acc_ref, delta_ref):
        i, j = pl.program_id(2), pl.program_id(3)

        @pl.when(j == 0)
        def _():
            acc_ref[...] = jnp.zeros_like(acc_ref)
            for hh in range(hp):
                ln = slice(hh * HEAD_PAD, (hh + 1) * HEAD_PAD)
                delta_ref[hh] = jnp.sum(do_ref[0, :, ln].astype(F32) * o_ref[0, :, ln].astype(F32), axis=-1,
                                        keepdims=True)

        def block(on_diagonal):
            for hh in range(hp):
                ln = slice(hh * HEAD_PAD, (hh + 1) * HEAD_PAD)
                sc = _nt(q_ref[0, :, ln], k_ref[0, :, ln])
                p = jnp.exp2(sc * c2 - lse_ref[0, :, hh * HEAD_PAD:hh * HEAD_PAD + 1] * LOG2_E)
                if on_diagonal:
                    p = jnp.where(_causal_mask(i, j, t, t), p, 0.0)
                dp = _nt(do_ref[0, :, ln], v_ref[0, :, ln])
                ds = p * (dp - delta_ref[hh])
                acc_ref[:, ln] += jnp.dot(ds.astype(BF16), k_ref[0, :, ln], preferred_element_type=F32)

        pl.when(j < i)(functools.partial(block, False))
        pl.when(j == i)(functools.partial(block, True))

        @pl.when(j == n - 1)
        def _():
            dq_ref[0] = acc_ref[...] * sm_scale

    qs = pl.BlockSpec((1, t, w), lambda b, h, i, j: (b, i, h))
    ks = pl.BlockSpec((1, t, w), lambda b, h, i, j: (b, jnp.minimum(j, i), h))
    return _pc(body, name=name, grid=(b, n_heads // hp, n, n), out_shape=_sds(q.shape, F32),
               in_specs=[qs, ks, ks, qs, qs, qs], out_specs=qs,
               scratch_shapes=[pltpu.VMEM((t, w), F32), pltpu.VMEM((hp, t, 1), F32)],
               compiler_params=_cp("parallel", "parallel", "parallel", "arbitrary"))(q, k, v, o, lse, do)


def _flash_dkv(name, q, k, v, o, lse, do, n_heads, sm_scale):
    b, s, _ = q.shape
    t, hp = _div(s, FLASH_TILE, LANE), _heads_per_step(n_heads)
    n, w = s // t, hp * HEAD_PAD
    c2 = sm_scale * LOG2_E

    def body(q_ref, k_ref, v_ref, o_ref, lse_ref, do_ref, dk_ref, dv_ref, dk_acc, dv_acc):
        j, i = pl.program_id(2), pl.program_id(3)

        @pl.when(i == 0)
        def _():
            dk_acc[...] = jnp.zeros_like(dk_acc)
            dv_acc[...] = jnp.zeros_like(dv_acc)

        def block(on_diagonal):
            for hh in range(hp):
                ln = slice(hh * HEAD_PAD, (hh + 1) * HEAD_PAD)
                do_ = do_ref[0, :, ln]
                delta = jnp.sum(do_.astype(F32) * o_ref[0, :, ln].astype(F32), axis=-1, keepdims=True)
                sc = _nt(q_ref[0, :, ln], k_ref[0, :, ln])
                p = jnp.exp2(sc * c2 - lse_ref[0, :, hh * HEAD_PAD:hh * HEAD_PAD + 1] * LOG2_E)
                if on_diagonal:
                    p = jnp.where(_causal_mask(i, j, t, t), p, 0.0)
                dv_acc[:, ln] += _tn(p.astype(BF16), do_)
                dp = _nt(do_, v_ref[0, :, ln])
                ds = p * (dp - delta)
                dk_acc[:, ln] += _tn(ds.astype(BF16), q_ref[0, :, ln])

        pl.when(i > j)(functools.partial(block, False))
        pl.when(i == j)(functools.partial(block, True))

        @pl.when(i == n - 1)
        def _():
            dk_ref[0] = dk_acc[...] * sm_scale
            dv_ref[0] = dv_acc[...].astype(BF16)

    qs = pl.BlockSpec((1, t, w), lambda b, h, j, i: (b, jnp.maximum(i, j), h))
    ks = pl.BlockSpec((1, t, w), lambda b, h, j, i: (b, j, h))
    return _pc(body, name=name, grid=(b, n_heads // hp, n, n), out_shape=(_sds(q.shape, F32), _sds(q.shape, BF16)),
               in_specs=[qs, ks, ks, qs, qs, qs], out_specs=(ks, ks),
               scratch_shapes=[pltpu.VMEM((t, w), F32), pltpu.VMEM((t, w), F32)],
               compiler_params=_cp("parallel", "parallel", "parallel", "arbitrary"))(q, k, v, o, lse, do)


def _mod_fwd(name, c_all, w, bias):
    depth, d, n = w.shape
    rows = c_all.shape[0]

    def body(c_ref, w_ref, b_ref, o_ref):
        cv = c_ref[...]
        cond = (cv * _sigmoid(cv)).astype(BF16)
        o_ref[0] = jnp.dot(cond, w_ref[0].astype(BF16), preferred_element_type=F32) + b_ref[0]

    return _pc(body, name=name, grid=(depth,), out_shape=_sds((depth, rows, n), F32),
               in_specs=[pl.BlockSpec((rows, d), lambda l: (0, 0)), pl.BlockSpec((1, d, n), lambda l: (l, 0, 0)),
                         pl.BlockSpec((1, 1, n), lambda l: (l, 0, 0))],
               out_specs=pl.BlockSpec((1, rows, n), lambda l: (l, 0, 0)), compiler_params=_cp("parallel"))(c_all, w, bias)


def _mod_bwd(name, c_all, dmod_cols, dmod_all):
    depth, rows, n = dmod_cols.shape
    d = c_all.shape[1]
    n_all = dmod_all.shape[2]
    tn = _div(n, 512, LANE)

    def body(c_ref, dm_ref, dma_ref, gw_ref, gb_ref):
        cv = c_ref[...]
        cond = (cv * _sigmoid(cv)).astype(BF16)
        gw_ref[0] = _tn(cond, dm_ref[0].astype(BF16))

        @pl.when(pl.program_id(1) == 0)
        def _():
            gb_ref[0] = jnp.sum(dma_ref[0], axis=0, keepdims=True)

    return _pc(body, name=name, grid=(depth, n // tn),
               out_shape=(_sds((depth, d, n), F32), _sds((depth, 1, n_all), F32)),
               in_specs=[pl.BlockSpec((rows, d), lambda l, j: (0, 0)), pl.BlockSpec((1, rows, tn), lambda l, j: (l, 0, j)),
                         pl.BlockSpec((1, rows, n_all), lambda l, j: (l, 0, 0))],
               out_specs=(pl.BlockSpec((1, d, tn), lambda l, j: (l, 0, j)), pl.BlockSpec((1, 1, n_all), lambda l, j: (l, 0, 0))),
               compiler_params=_cp("parallel", "arbitrary"))(c_all, dmod_cols, dmod_all)


def _adamw(name, w, g, m, v):
    shape = w.shape
    cols = shape[-1]
    rows = _size(shape) // cols
    tr = _div(rows, max(SUBLANE, (2 ** 19) // cols // SUBLANE * SUBLANE), SUBLANE)
    c1 = 1.0 - ADAM_B1 ** ADAM_STEP
    c2 = 1.0 - ADAM_B2 ** ADAM_STEP

    def body(w_ref, g_ref, m_ref, v_ref, d_ref, mo_ref, vo_ref):
        gv = g_ref[...]
        m_new = ADAM_B1 * m_ref[...] + (1.0 - ADAM_B1) * gv
        v_new = ADAM_B2 * v_ref[...] + (1.0 - ADAM_B2) * (gv * gv)
        m_hat = m_new / c1
        v_hat = v_new / c2
        d_ref[...] = -ADAM_LR * (m_hat / (jnp.sqrt(v_hat) + ADAM_EPS) + ADAM_WD * w_ref[...])
        mo_ref[...] = m_new
        vo_ref[...] = v_new

    spec = pl.BlockSpec((tr, cols), lambda i: (i, 0))
    o = _sds((rows, cols), F32)
    outs = _pc(body, name=name, grid=(rows // tr,), out_shape=(o, o, o), in_specs=[spec] * 4, out_specs=(spec,) * 3,
               compiler_params=_cp("parallel"))(*[a.reshape(rows, cols) for a in (w, g, m, v)])
    return tuple(a.reshape(shape) for a in outs)


def _rope_tables(positions):
    half = QK_ROPE // 2
    inv_freq = ROPE_THETA ** (-jnp.arange(0, QK_ROPE, 2, dtype=F32) / QK_ROPE)
    ang = positions.astype(F32)[..., None] * inv_freq
    cos, sin = jnp.cos(ang), jnp.sin(ang)
    lead = positions.shape
    ones = jnp.ones(lead + (QK_NOPE,), F32)
    tail_one = jnp.ones(lead + (HEAD_PAD - QK_NOPE - QK_ROPE,), F32)
    cos_t = jnp.concatenate([ones, cos, cos, tail_one], axis=-1)
    sin_t = jnp.concatenate([0 * ones, -sin, sin, 0 * tail_one], axis=-1)
    return cos_t, sin_t


def _pad_heads(w, n_heads, parts, axis):
    w = jnp.moveaxis(w, axis, -1)
    lead = w.shape[:-1]
    per = w.shape[-1] // n_heads
    w = w.reshape(lead + (n_heads, per))
    kept = jnp.concatenate([w[..., a:b_] for a, b_ in parts], axis=-1)
    pad = HEAD_PAD - kept.shape[-1]
    kept = jnp.concatenate([kept, jnp.zeros(lead + (n_heads, pad), w.dtype)], axis=-1)
    return jnp.moveaxis(kept.reshape(lead + (n_heads * HEAD_PAD,)), -1, axis)


def _unpad_heads(g, n_heads, width, axis):
    g = jnp.moveaxis(g, axis, -1)
    lead = g.shape[:-1]
    g = g.reshape(lead + (n_heads, HEAD_PAD))[..., :width]
    return g, lead


def kernel(x, c, positions, mod_w, mod_b, ln_g, ln_b, pool_w, pool_scale, mla_w_a, mla_q_norm, mla_w_uq, mla_kv_norm, mla_w_ukv, mla_w_o, sc_w_in, sc_conv, sc_w_out, ffn_w_up, ffn_conv, ffn_conv_b, ffn_w_down, loss_target, m_mod_w, m_mod_b, m_ln_g, m_ln_b, m_pool_w, m_pool_scale, m_mla_w_a, m_mla_q_norm, m_mla_w_uq, m_mla_kv_norm, m_mla_w_ukv, m_mla_w_o, m_sc_w_in, m_sc_conv, m_sc_w_out, m_ffn_w_up, m_ffn_conv, m_ffn_conv_b, m_ffn_w_down, v_mod_w, v_mod_b, v_ln_g, v_ln_b, v_pool_w, v_pool_scale, v_mla_w_a, v_mla_q_norm, v_mla_w_uq, v_mla_kv_norm, v_mla_w_ukv, v_mla_w_o, v_sc_w_in, v_sc_conv, v_sc_w_out, v_ffn_w_up, v_ffn_conv, v_ffn_conv_b, v_ffn_w_down):
    wts = dict(mod_w=mod_w, mod_b=mod_b, ln_g=ln_g, ln_b=ln_b, pool_w=pool_w, pool_scale=pool_scale, mla_w_a=mla_w_a,
               mla_q_norm=mla_q_norm, mla_w_uq=mla_w_uq, mla_kv_norm=mla_kv_norm, mla_w_ukv=mla_w_ukv, mla_w_o=mla_w_o,
               sc_w_in=sc_w_in, sc_conv=sc_conv, sc_w_out=sc_w_out, ffn_w_up=ffn_w_up, ffn_conv=ffn_conv,
               ffn_conv_b=ffn_conv_b, ffn_w_down=ffn_w_down)
    mom1 = dict(mod_w=m_mod_w, mod_b=m_mod_b, ln_g=m_ln_g, ln_b=m_ln_b, pool_w=m_pool_w, pool_scale=m_pool_scale,
                mla_w_a=m_mla_w_a, mla_q_norm=m_mla_q_norm, mla_w_uq=m_mla_w_uq, mla_kv_norm=m_mla_kv_norm,
                mla_w_ukv=m_mla_w_ukv, mla_w_o=m_mla_w_o, sc_w_in=m_sc_w_in, sc_conv=m_sc_conv, sc_w_out=m_sc_w_out,
                ffn_w_up=m_ffn_w_up, ffn_conv=m_ffn_conv, ffn_conv_b=m_ffn_conv_b, ffn_w_down=m_ffn_w_down)
    mom2 = dict(mod_w=v_mod_w, mod_b=v_mod_b, ln_g=v_ln_g, ln_b=v_ln_b, pool_w=v_pool_w, pool_scale=v_pool_scale,
                mla_w_a=v_mla_w_a, mla_q_norm=v_mla_q_norm, mla_w_uq=v_mla_w_uq, mla_kv_norm=v_mla_kv_norm,
                mla_w_ukv=v_mla_w_ukv, mla_w_o=v_mla_w_o, sc_w_in=v_sc_w_in, sc_conv=v_sc_conv, sc_w_out=v_sc_w_out,
                ffn_w_up=v_ffn_w_up, ffn_conv=v_ffn_conv, ffn_conv_b=v_ffn_conv_b, ffn_w_down=v_ffn_w_down)

    bsz, seq, d = x.shape
    depth = mod_b.shape[0]
    n_tok = bsz * seq
    n_heads = d // V_HEAD
    ql, kvl = mla_q_norm.shape[1], mla_kv_norm.shape[1]
    alpha = float((2 * depth) ** 0.25)
    sm_scale = float((QK_NOPE + QK_ROPE) ** -0.5)
    mx, my, mc = lax.axis_index("x"), lax.axis_index("y"), lax.axis_index("c")
    chip = 2 * mx + my
    dev = 2 * chip + mc

    small_names = list(SMALL_SHARDED)
    small_pack, small_spans = _pack_rows([c] + [wts[n] for n in small_names], F32, SUBLANE)
    rows_small = small_pack.shape[0]
    small_all = _all_gather8("gather_small_params", small_pack, True).reshape(N_DEV, rows_small * PACK_COLS)
    c_all = small_all[:, :c.size].reshape(N_DEV * bsz, d)
    per_chip = small_all[0::2]
    full = dict(wts)
    for n, (off, shape) in zip(small_names, small_spans[1:]):
        blocks = per_chip[:, off:off + _size(shape)].reshape((N_CHIP,) + tuple(shape))
        full[n] = _join_chips(blocks, SMALL_SHARDED[n])

    n_mod = mod_w.shape[2]
    bias_cols = lax.dynamic_slice_in_dim(mod_b, chip * n_mod, n_mod, axis=1)[:, None, :]
    mod_cols = _mod_fwd("mod_fwd", c_all, mod_w, bias_cols)
    half_rows = (N_DEV * bsz) // 2
    mod_half = lax.dynamic_slice_in_dim(mod_cols, mc * half_rows, half_rows, axis=1).reshape(depth * half_rows, n_mod)
    mod_all = _all_gather8("gather_mod", mod_half, True).reshape(N_CHIP, 2, depth, half_rows, n_mod)
    mod_all = jnp.transpose(mod_all, (2, 1, 3, 0, 4)).reshape(depth, N_DEV * bsz, N_CHIP * n_mod)
    mod_mine = lax.dynamic_slice_in_dim(mod_all, dev * bsz, bsz, axis=1)
    mods = [[mod_mine[l, :, k * d:(k + 1) * d][:, None, :] for k in range(6)] for l in range(depth)]

    big_names = list(BIG)
    f_hid = ffn_w_down.shape[1] * N_CHIP
    host_layer = 1
    assert depth > host_layer

    def layer_of(n, i):
        if n == 'pool_w':
            return 3 * i
        if n.startswith('mla_'):
            return 3 * i + 1
        if n.startswith('sc_'):
            return 3 * i + 2
        return i

    n_early = {n: sum(layer_of(n, i) <= host_layer for i in range(wts[n].shape[0])) for n in big_names}
    early = [n for n in big_names if n_early[n] > 0]
    late = [n for n in big_names if n_early[n] < wts[n].shape[0]]

    def rows2d(a):
        return a.astype(BF16).reshape(-1, a.shape[-1])

    def layouts(bc):
        out = {}
        for n in ('pool_w', 'mla_w_a', 'mla_w_uq', 'mla_w_ukv', 'mla_w_o', 'sc_w_out'):
            if n in bc:
                out[n] = jnp.concatenate([bc[n][j] for j in range(N_CHIP)], axis=BIG[n])
        if 'ffn_w_up' in bc:
            nl = bc['ffn_w_up'].shape[1]
            out['up_cols'] = bc['ffn_w_up'].reshape(N_CHIP, nl * d, -1)
            out['up_rows'] = jnp.transpose(bc['ffn_w_up'], (1, 0, 3, 2)).reshape(1, nl * 2 * f_hid, d)
            out['down_rows'] = jnp.transpose(bc['ffn_w_down'], (1, 0, 2, 3)).reshape(1, nl * f_hid, d)
            out['down_t'] = jnp.transpose(bc['ffn_w_down'], (1, 3, 0, 2)).reshape(1, nl * d, f_hid)
        if 'sc_w_in' in bc:
            ns = bc['sc_w_in'].shape[1]
            out['in_cols'] = bc['sc_w_in'].reshape(N_CHIP, ns * d, -1)
            out['in_rows'] = jnp.transpose(bc['sc_w_in'], (1, 0, 3, 2)).reshape(1, ns * 3 * d, d)
        return out

    gathered = _gather_weights("gather_weights", [rows2d(wts[n][:n_early[n]]) for n in early])
    lay = [layouts({n: g.reshape((N_CHIP, n_early[n]) + wts[n].shape[1:]) for n, g in zip(early, gathered)}), None]
    late_shards = [rows2d(wts[n][n_early[n]:]) for n in late]

    def grp(n, i):
        return (lay[0], i) if i < n_early[n] else (lay[1], i - n_early[n])

    nope_rope = [(0, QK_NOPE + QK_ROPE)]
    cos_t, sin_t = _rope_tables(positions)

    def t2(a):
        return a.reshape(n_tok, a.shape[-1])

    def t3(a):
        return a.reshape(bsz, seq, a.shape[-1])

    saved = []
    xin = x
    u = _modulate("modulate_in", x, mods[0][1], mods[0][0])
    loss_acc = None
    for l in range(depth):
        sh1, sc1, g1, sh2, sc2, g2 = mods[l]
        kind, j = l % 3, l // 3
        st = dict(x=xin)
        if kind == 0:
            grp_l, jj = grp('pool_w', j)
            w = grp_l['pool_w'][jj]
            st.update(w=w, w_t=jnp.swapaxes(w, 1, 2), scale=full['pool_scale'][j][None, :])
            y = _pool_fwd(f"pool_fwd_{l}", xin, sc1, sh1, st['w'], st['scale'])
        elif kind == 1:
            grp_l, jj = grp('mla_w_a', j)
            wa, wuq, wukv = grp_l['mla_w_a'][jj], grp_l['mla_w_uq'][jj], grp_l['mla_w_ukv'][jj]
            zeros = jnp.zeros((d, QK_NOPE), BF16)
            w_a = jnp.concatenate([wa[:, :ql + kvl], zeros, wa[:, ql + kvl:], zeros[:, :HEAD_PAD - QK_NOPE - QK_ROPE]], axis=1)
            w_uq = _pad_heads(wuq, n_heads, nope_rope, 1)
            w_kv = jnp.concatenate([_pad_heads(wukv, n_heads, [(0, QK_NOPE)], 1),
                                    _pad_heads(wukv, n_heads, [(QK_NOPE, QK_NOPE + V_HEAD)], 1)], axis=1)
            w_o = _pad_heads(grp_l['mla_w_o'][jj], n_heads, [(0, V_HEAD)], 0)
            qn, kvn = mla_q_norm[j][None, :], mla_kv_norm[j][None, :]
            a = t3(_mm_nn(f"mla_a_{l}", [(t2(u), _w2(w_a))], F32))
            cq, ckv, kpe = _mla_norm_fwd(f"mla_norm_fwd_{l}", a, qn, kvn, cos_t, sin_t)
            q_raw = t3(_mm_nn(f"mla_q_{l}", [(t2(cq), _w2(w_uq))], F32))
            kv_raw = t3(_mm_nn(f"mla_kv_{l}", [(t2(ckv), _w2(w_kv))], F32))
            qh, kh, vh = _mla_prep_fwd(f"mla_prep_fwd_{l}", q_raw, kv_raw, kpe, cos_t, sin_t, n_heads)
            o, lse, fetched = _flash_fwd(f"flash_fwd_{l}", qh, kh, vh, n_heads, sm_scale,
                                         cargo=late_shards if l == host_layer else ())
            if l == host_layer:
                lay[1] = layouts({n: lax.dynamic_update_index_in_dim(got, mine, chip, 0).reshape(
                    (N_CHIP, wts[n].shape[0] - n_early[n]) + wts[n].shape[1:])
                    for n, got, mine in zip(late, fetched, late_shards)})
            y = t3(_mm_nn(f"mla_o_{l}", [(t2(o), _w2(w_o))], F32))
            st.update(u=u, w_a=w_a, w_uq=w_uq, w_kv=w_kv, w_o=w_o, qn=qn, kvn=kvn, a=a, cq=cq, ckv=ckv,
                      qh=qh, kh=kh, vh=vh, o=o, lse=lse)
        else:
            grp_l, jj = grp('sc_w_in', j)
            w_out, cw = grp_l['sc_w_out'][jj], full['sc_conv'][j]
            q = t3(_mm_nn(f"sc_in_{l}", [(t2(u), (grp_l['in_cols'], jj))], F32))
            r = _shortconv_fwd(f"shortconv_fwd_{l}", q, cw)
            y = t3(_mm_nn(f"sc_out_{l}", [(t2(r), _w2(w_out))], F32))
            st.update(u=u, w_out=w_out, cw=cw, q=q, r=r, in_rows=(grp_l['in_rows'], jj))
        lng, lnb = full['ln_g'][l], full['ln_b'][l]
        z1, xmid, u2 = _ln_mod_fwd(f"ln_mod_a_{l}", alpha, xin, y, g1, lng[0:1], lnb[0:1], sc2, sh2)
        cwf, cbf = full['ffn_conv'][l], ffn_conv_b[l][None, :]
        ffn_w, ll = grp('ffn_w_up', l)
        p = t3(_mm_nn(f"ffn_up_{l}", [(t2(u2), (ffn_w['up_cols'], ll))], F32))
        act = _convglu_fwd(f"convglu_fwd_{l}", p, cwf, cbf)
        y2 = t3(_mm_nn(f"ffn_down_{l}", [(t2(act), (ffn_w['down_rows'], ll))], F32))
        st.update(y1=y, z1=z1, xmid=xmid, u2=u2, p=p, act=act, y2=y2, cwf=cwf, cbf=cbf, lng=lng, lnb=lnb,
                  ffn_w=ffn_w, ll=ll)
        if l + 1 < depth:
            nsh1, nsc1 = mods[l + 1][0], mods[l + 1][1]
            z2, xin, u = _ln_mod_fwd(f"ln_mod_b_{l}", alpha, xmid, y2, g2, lng[1:2], lnb[1:2], nsc1, nsh1)
        else:
            z2, ct, loss_acc = _ln_loss_fwd("ln_loss", alpha, xmid, y2, g2, lng[1:2], lnb[1:2], loss_target)
        st.update(z2=z2)
        saved.append(st)
    loss = lax.psum(loss_acc[0, 0], ("x", "y", "c"))

    grads = {}
    dmods = [[None] * 6 for _ in range(depth)]
    g_ln_g = [[None, None] for _ in range(depth)]
    g_ln_b = [[None, None] for _ in range(depth)]
    stack = {n: [None] * wts[n].shape[0] for n in ('pool_scale', 'mla_q_norm', 'mla_kv_norm', 'sc_conv', 'ffn_conv',
                                                    'ffn_conv_b')}
    units = {n: [None] * wts[n].shape[0] for n in big_names}
    cargo_a, cargo_b, received = [], [], {}

    def landed(items, got):
        for (n, i, _), r in zip(items, got):
            received[(n, i)] = r

    upstream = (ct,)
    for l in reversed(range(depth)):
        st = saved[l]
        sh1, sc1, g1, sh2, sc2, g2 = mods[l]
        kind, j = l % 3, l // 3
        res = _sub_bwd(f"sub_bwd_b_{l}", alpha, upstream, st['z2'], st['y2'], g2, st['lng'][1:2])
        dz2, dy2, dmods[l][5], g_ln_g[l][1], g_ln_b[l][1] = res[:5]
        if l + 1 < depth:
            dmods[l + 1][1], dmods[l + 1][0] = res[5], res[6]
        dy2f = t2(dy2)
        ffn_w, ll = st['ffn_w'], st['ll']
        da = t3(_mm_nn(f"ffn_down_bwd_{l}", [(dy2f, (ffn_w['down_t'], ll))], F32))
        units['ffn_w_down'][l] = _mm_tn(f"ffn_down_dw_{l}", t2(st['act']), [dy2f],
                                        out_dtype=BF16).reshape(N_CHIP, f_hid // N_CHIP, d)
        dpv, dpg, dcw, dcb, got = _convglu_bwd(f"convglu_bwd_{l}", st['p'], da, st['cwf'], st['cbf'],
                                               cargo=[u for _, _, u in cargo_a])
        landed(cargo_a, got)
        stack['ffn_conv'][l], stack['ffn_conv_b'][l] = dcw, dcb[0]
        down_unit = [('ffn_w_down', l, units['ffn_w_down'][l])]
        du2, got = _mm_nn(f"ffn_up_bwd_{l}", [(t2(dpv), (ffn_w['up_rows'], 2 * ll)),
                                              (t2(dpg), (ffn_w['up_rows'], 2 * ll + 1))], F32,
                          cargo=[units['ffn_w_down'][l]])
        landed(down_unit, got)
        du2 = t3(du2)
        res = _mm_tn(f"ffn_up_dw_{l}", t2(st['u2']), [t2(dpv), t2(dpg)], N_CHIP, out_dtype=BF16,
                     cargo=[u for _, _, u in cargo_b])
        if cargo_b:
            landed(cargo_b, res[1])
            res = res[0]
        units['ffn_w_up'][l] = res
        res = _sub_bwd(f"sub_bwd_a_{l}", alpha, (dz2, du2, st['xmid'], sc2), st['z1'], st['y1'], g1, st['lng'][0:1])
        dz1, dy1, dmods[l][2], g_ln_g[l][0], g_ln_b[l][0], dmods[l][4], dmods[l][3] = res
        dy1f = t2(dy1)
        if kind == 0:
            up_unit = [('ffn_w_up', l, units['ffn_w_up'][l])] if l == 0 else []
            du1, dw, dscale, got = _pool_bwd(f"pool_bwd_{l}", st['x'], sc1, sh1, dy1, st['w'], st['w_t'], st['scale'],
                                             cargo=[u for _, _, u in up_unit])
            landed(up_unit, got)
            stack['pool_scale'][j] = dscale[0]
            grp = dw.shape[1] // N_CHIP
            units['pool_w'][j] = jnp.transpose(dw.reshape(POOL_GROUPS, N_CHIP, grp, dw.shape[2]),
                                               (1, 0, 2, 3)).reshape(N_CHIP, POOL_GROUPS * grp, dw.shape[2])
        elif kind == 1:
            do = t3(_mm_nn(f"mla_o_bwd_{l}", [(dy1f, _w2(jnp.swapaxes(st['w_o'], 0, 1)))], BF16))
            gwo, _ = _unpad_heads(_mm_tn(f"mla_o_dw_{l}", t2(st['o']), [dy1f])[0], n_heads, V_HEAD, 0)
            units['mla_w_o'][j] = jnp.moveaxis(gwo.reshape(d, n_heads * V_HEAD), -1, 0).reshape(N_CHIP, -1, d)
            fa = (st['qh'], st['kh'], st['vh'], st['o'], st['lse'], do, n_heads, sm_scale)
            dq = _flash_dq(f"flash_dq_{l}", *fa)
            dk, dv = _flash_dkv(f"flash_dkv_{l}", *fa)
            dq_raw, dkv_raw, dkpe = _mla_prep_bwd(f"mla_prep_bwd_{l}", dq, dk, dv, cos_t, sin_t, n_heads)
            dq_raw, dkv_raw = t2(dq_raw), t2(dkv_raw)
            dcq = t3(_mm_nn(f"mla_q_bwd_{l}", [(dq_raw, _w2(jnp.swapaxes(st['w_uq'], 0, 1)))], F32))
            dckv = t3(_mm_nn(f"mla_kv_bwd_{l}", [(dkv_raw, _w2(jnp.swapaxes(st['w_kv'], 0, 1)))], F32))
            gq, _ = _unpad_heads(_mm_tn(f"mla_q_dw_{l}", t2(st['cq']), [dq_raw])[0], n_heads, QK_NOPE + QK_ROPE, 1)
            units['mla_w_uq'][j] = _cols_by_chip(gq.reshape(ql, n_heads * (QK_NOPE + QK_ROPE)))
            gkv = _mm_tn(f"mla_kv_dw_{l}", t2(st['ckv']), [dkv_raw])[0]
            gk, _ = _unpad_heads(gkv[:, :n_heads * HEAD_PAD], n_heads, QK_NOPE, 1)
            gv, _ = _unpad_heads(gkv[:, n_heads * HEAD_PAD:], n_heads, V_HEAD, 1)
            units['mla_w_ukv'][j] = _cols_by_chip(
                jnp.concatenate([gk, gv], axis=-1).reshape(kvl, n_heads * (QK_NOPE + V_HEAD)))
            da_, dqn, dkvn = _mla_norm_bwd(f"mla_norm_bwd_{l}", st['a'], dcq, dckv, dkpe, st['qn'], st['kvn'])
            stack['mla_q_norm'][j], stack['mla_kv_norm'][j] = dqn[0], dkvn[0]
            du1 = t3(_mm_nn(f"mla_a_bwd_{l}", [(t2(da_), _w2(jnp.swapaxes(st['w_a'], 0, 1)))], F32))
            gwa = _mm_tn(f"mla_a_dw_{l}", t2(st['u']), [t2(da_)])[0]
            units['mla_w_a'][j] = _cols_by_chip(jnp.concatenate(
                [gwa[:, :ql + kvl], gwa[:, ql + kvl + QK_NOPE:ql + kvl + QK_NOPE + QK_ROPE]], axis=1))
        else:
            dr = t3(_mm_nn(f"sc_out_bwd_{l}", [(dy1f, _w2(jnp.swapaxes(st['w_out'], 0, 1)))], F32))
            units['sc_w_out'][j] = _mm_tn(f"sc_out_dw_{l}", t2(st['r']), [dy1f], out_dtype=BF16).reshape(N_CHIP, -1, d)
            dgb, dgc, dh, dcw = _shortconv_bwd(f"shortconv_bwd_{l}", st['q'], dr, st['cw'])
            stack['sc_conv'][j] = dcw
            parts = [t2(dgb), t2(dgc), t2(dh)]
            in_rows, jj = st['in_rows']
            du1 = t3(_mm_nn(f"sc_in_bwd_{l}", [(parts[k], (in_rows, 3 * jj + k)) for k in range(3)], F32))
            units['sc_w_in'][j] = _cols_by_chip(jnp.concatenate(
                [_mm_tn(f"sc_in_dw_{k}_{l}", t2(st['u']), [parts[k]])[0] for k in range(3)], axis=1))
        upstream = (dz1, du1, st['x'], sc1)
        mixer = {0: ['pool_w'], 1: ['mla_w_a', 'mla_w_uq', 'mla_w_ukv', 'mla_w_o'], 2: ['sc_w_in', 'sc_w_out']}[kind]
        for n in mixer:
            units[n][j] = units[n][j].astype(BF16)
        cargo_a = [('ffn_w_up', l, units['ffn_w_up'][l])] if l > 0 else []
        cargo_b = [(n, j, units[n][j]) for n in mixer]
    grad_x, dmods[0][1], dmods[0][0] = _input_bwd("input_bwd", alpha, upstream[0], upstream[1], x, mods[0][1])

    for n, parts in stack.items():
        grads[n] = jnp.stack(parts)
    grads['ln_g'] = jnp.stack([jnp.concatenate(r, axis=0) for r in g_ln_g])
    grads['ln_b'] = jnp.stack([jnp.concatenate(r, axis=0) for r in g_ln_b])
    dmod_mine = jnp.stack([jnp.concatenate([t[:, 0, :] for t in dmods[l]], axis=-1) for l in range(depth)])

    small_grad_names = small_names + ['mla_q_norm', 'mla_kv_norm', 'ffn_conv_b']
    sg_pack, sg_spans = _pack_rows([dmod_mine] + [grads[n] for n in small_grad_names], F32, SUBLANE)
    rows_sg = sg_pack.shape[0]
    sg_all = _all_gather8("gather_small_grads", sg_pack, True).reshape(N_DEV, rows_sg, PACK_COLS)
    dmod_all = sg_all.reshape(N_DEV, -1)[:, :dmod_mine.size].reshape(N_DEV, depth, bsz, 6 * d)
    dmod_all = jnp.transpose(dmod_all, (1, 0, 2, 3)).reshape(depth, N_DEV * bsz, 6 * d)
    sg_sum = _sum8("sum_small_grads", sg_all).reshape(-1)
    for n, (off, shape) in zip(small_grad_names, sg_spans[1:]):
        g_full = sg_sum[off:off + _size(shape)].reshape(shape)
        if n in SMALL_SHARDED:
            ax = SMALL_SHARDED[n]
            width = shape[ax] // N_CHIP
            g_full = lax.dynamic_slice_in_dim(g_full, chip * width, width, axis=ax)
        grads[n] = g_full
    dmod_cols = lax.dynamic_slice_in_dim(dmod_all, chip * n_mod, n_mod, axis=2)
    grads['mod_w'], gb = _mod_bwd("mod_bwd", c_all, dmod_cols, dmod_all)
    grads['mod_b'] = gb[:, 0, :]

    keys = [(n, i) for n in big_names for i in range(len(units[n]))]
    last = [(n, i, units[n][i]) for n, i in keys if (n, i) not in received]
    landed(last, _scatter_grads("scatter_big_grads", [u for _, _, u in last]))
    chip_core = jnp.stack([chip, mc]).astype(jnp.int32)
    bufs = _swap_halves("swap_big_grad_halves",
                        [_sum8_into_half(f"sum_big_grads_{n}_{i}", units[n][i], received[(n, i)], chip_core)
                         for n, i in keys])
    for n in big_names:
        grads[n] = jnp.stack([b for (m, _), b in zip(keys, bufs) if m == n]).reshape(wts[n].shape)

    deltas, new_m, new_v = {}, {}, {}
    for n in WEIGHTS:
        deltas[n], new_m[n], new_v[n] = _adamw(f"adamw_{n}", wts[n], grads[n], mom1[n], mom2[n])
    return (loss, grad_x, *[grads[n] for n in WEIGHTS], *[deltas[n] for n in WEIGHTS],
            *[new_m[n] for n in WEIGHTS], *[new_v[n] for n in WEIGHTS])
```

```python
import functools

import jax
import jax.numpy as jnp
from jax import lax
from jax.experimental import pallas as pl
from jax.experimental.pallas import tpu as pltpu

F32 = jnp.float32
BF16 = jnp.bfloat16
MESH = pl.DeviceIdType.MESH

N_DEV = 8
N_CHIP = 4
LANE = 128
SUBLANE = 8
VMEM_LIMIT_BYTES = 56 * 2 ** 20
PACK_COLS = 1024

LN_EPS = 1e-5
RMS_EPS = 1e-6
QK_NOPE, QK_ROPE, V_HEAD = 64, 32, 64
ROPE_THETA = 10000.0
HEAD_PAD = 128
POOL_GROUPS = 4
POOL_HALO = 16
CONV_HALO = 8
CONV_ROWS = 1024
ADAM_LR, ADAM_B1, ADAM_B2, ADAM_EPS, ADAM_WD, ADAM_STEP = 0.001, 0.9, 0.999, 1e-08, 0.01, 10

WEIGHTS = ['mod_w', 'mod_b', 'ln_g', 'ln_b', 'pool_w', 'pool_scale', 'mla_w_a', 'mla_q_norm', 'mla_w_uq',
           'mla_kv_norm', 'mla_w_ukv', 'mla_w_o', 'sc_w_in', 'sc_conv', 'sc_w_out', 'ffn_w_up', 'ffn_conv',
           'ffn_conv_b', 'ffn_w_down']
BIG = {'pool_w': 2, 'mla_w_a': 2, 'mla_w_uq': 2, 'mla_w_ukv': 2, 'mla_w_o': 1, 'sc_w_in': 2, 'sc_w_out': 1,
       'ffn_w_up': 2, 'ffn_w_down': 1}
SMALL_SHARDED = {'ln_g': 2, 'ln_b': 2, 'pool_scale': 1, 'sc_conv': 2, 'ffn_conv': 2}
REPLICATED = ['mod_b', 'mla_q_norm', 'mla_kv_norm', 'ffn_conv_b']


def _pc(body, **kw):
    return pl.pallas_call(body, **kw)


def _cp(*sem):
    return pltpu.CompilerParams(dimension_semantics=sem, vmem_limit_bytes=VMEM_LIMIT_BYTES)


def _div(n, cap, mult):
    best = None
    for d in range(mult, min(n, cap) + 1, mult):
        if n % d == 0:
            best = d
    return best if best is not None else n


def _sds(shape, dtype):
    return jax.ShapeDtypeStruct(tuple(shape), dtype)


def _flip(v, bit):
    return 1 - v if bit else v


def _all_gather8(name, x_shard, in_vmem):
    m_per, n = x_shard.shape
    space = pltpu.VMEM if in_vmem else pltpu.HBM

    def body(x_ref, out_ref, send_sems, recv_sems, local_sem):
        x, y, c = lax.axis_index("x"), lax.axis_index("y"), lax.axis_index("c")
        me, sibling = (x, y, c), (x, y, 1 - c)
        chips = [(1 - x, y), (x, 1 - y), (1 - x, 1 - y)]

        def rows(px, py, pc_):
            return out_ref.at[pl.ds((4 * px + 2 * py + pc_) * m_per, m_per), :]

        def copy(k, block, to, src=None):
            return pltpu.make_async_remote_copy(
                src_ref=rows(*block) if src is None else src, dst_ref=rows(*block),
                send_sem=send_sems.at[k], recv_sem=recv_sems.at[k], device_id=to, device_id_type=MESH)

        mine = pltpu.make_async_copy(x_ref, rows(*me), local_sem)
        mine.start()
        first = [copy(0, me, sibling, src=x_ref)]
        first += [copy(1 + j, me, (*chip, c), src=x_ref) for j, chip in enumerate(chips)]
        for cp in first:
            cp.start()
        passed = [copy(4 + j, (*chip, c), sibling) for j, chip in enumerate(chips)]
        for j, chip in enumerate(chips):
            copy(1 + j, (*chip, c), me).wait_recv()
            passed[j].start()
        copy(0, sibling, me).wait_recv()
        for j, chip in enumerate(chips):
            copy(4 + j, (*chip, 1 - c), me).wait_recv()
        for cp in first + passed:
            cp.wait_send()
        mine.wait()

    return _pc(
        body, name=name, out_shape=_sds((N_DEV * m_per, n), x_shard.dtype),
        in_specs=[pl.BlockSpec(memory_space=space)], out_specs=pl.BlockSpec(memory_space=space),
        scratch_shapes=[pltpu.SemaphoreType.DMA((7,)), pltpu.SemaphoreType.DMA((7,)), pltpu.SemaphoreType.DMA],
        compiler_params=pltpu.CompilerParams(vmem_limit_bytes=VMEM_LIMIT_BYTES),
    )(x_shard)


def _gather_weights(name, shards):
    n_t = len(shards)
    halves = [s.shape[0] // 2 for s in shards]

    def body(*refs):
        x_refs, o_refs = refs[:n_t], refs[n_t:2 * n_t]
        send_sems, recv_sems, local_sems = refs[2 * n_t:]
        x, y, c = lax.axis_index("x"), lax.axis_index("y"), lax.axis_index("c")
        me, sibling = (x, y, c), (x, y, 1 - c)
        chips = [(1 - x, y), (x, 1 - y), (1 - x, 1 - y)]

        def slot(t, px, py, pc_):
            return o_refs[t].at[4 * px + 2 * py + pc_]

        def my_rows(t):
            return x_refs[t].at[pl.ds(c * halves[t], halves[t]), :]

        def copy(t, k, block, to, src=None):
            return pltpu.make_async_remote_copy(
                src_ref=slot(t, *block) if src is None else src, dst_ref=slot(t, *block),
                send_sem=send_sems.at[t, k], recv_sem=recv_sems.at[t, k], device_id=to, device_id_type=MESH)

        local = [pltpu.make_async_copy(my_rows(t), slot(t, *me), local_sems.at[t]) for t in range(n_t)]
        for cp in local:
            cp.start()
        first = []
        for t in range(n_t):
            first += [copy(t, 1 + j, me, (*chip, c), src=my_rows(t)) for j, chip in enumerate(chips)]
            first.append(copy(t, 0, me, sibling, src=my_rows(t)))
        for cp in first:
            cp.start()
        passed = []
        for j, chip in enumerate(chips):
            for t in range(n_t):
                copy(t, 1 + j, (*chip, c), me).wait_recv()
                passed.append(copy(t, 4 + j, (*chip, c), sibling))
                passed[-1].start()
        for t in range(n_t):
            copy(t, 0, sibling, me).wait_recv()
        for j, chip in enumerate(chips):
            for t in range(n_t):
                copy(t, 4 + j, (*chip, 1 - c), me).wait_recv()
        for cp in first + passed:
            cp.wait_send()
        for cp in local:
            cp.wait()

    hbm = pl.BlockSpec(memory_space=pltpu.HBM)
    return _pc(
        body, name=name, out_shape=tuple(_sds((N_DEV, h, s.shape[1]), s.dtype) for h, s in zip(halves, shards)),
        in_specs=[hbm] * n_t, out_specs=(hbm,) * n_t,
        scratch_shapes=[pltpu.SemaphoreType.DMA((n_t, 7)), pltpu.SemaphoreType.DMA((n_t, 7)),
                        pltpu.SemaphoreType.DMA((n_t,))],
    )(*shards)


def _scatter_copies(u_refs, r_refs, send_sems, recv_sems):
    x, y, c = lax.axis_index("x"), lax.axis_index("y"), lax.axis_index("c")
    copies = []
    for k in range(1, N_DEV):
        px, py, pcc = _flip(x, (k >> 2) & 1), _flip(y, (k >> 1) & 1), _flip(c, k & 1)
        for t, (u_ref, r_ref) in enumerate(zip(u_refs, r_refs)):
            h = u_ref.shape[1] // 2
            copies.append(pltpu.make_async_remote_copy(
                src_ref=u_ref.at[2 * px + py, pl.ds(pcc * h, h), :], dst_ref=r_ref.at[k - 1],
                send_sem=send_sems.at[t, k - 1], recv_sem=recv_sems.at[t, k - 1],
                device_id=(px, py, pcc), device_id_type=MESH))
    return copies


def _scatter_shapes(units):
    return tuple(_sds((N_DEV - 1, u.shape[1] // 2, u.shape[2]), u.dtype) for u in units)


def _scatter_grads(name, units):
    n_u = len(units)

    def body(*refs):
        copies = _scatter_copies(refs[:n_u], refs[n_u:2 * n_u], refs[2 * n_u], refs[2 * n_u + 1])
        for cp in copies:
            cp.start()
        for cp in copies:
            cp.wait()

    hbm = pl.BlockSpec(memory_space=pltpu.HBM)
    return _pc(body, name=name, out_shape=_scatter_shapes(units), in_specs=[hbm] * n_u, out_specs=(hbm,) * n_u,
               scratch_shapes=[pltpu.SemaphoreType.DMA((n_u, 7)), pltpu.SemaphoreType.DMA((n_u, 7))])(*units)


class _Scatter:
    peers = N_DEV - 1
    shapes = staticmethod(_scatter_shapes)

    @staticmethod
    def copies(u_refs, r_refs, send_sems, recv_sems):
        both = _scatter_copies(u_refs, r_refs, send_sems, recv_sems)
        return both, both


class _Fetch:
    peers = N_CHIP - 1

    @staticmethod
    def shapes(units):
        return tuple(_sds((N_CHIP,) + u.shape, u.dtype) for u in units)

    @staticmethod
    def copies(u_refs, r_refs, send_sems, recv_sems):
        x, y, c = lax.axis_index("x"), lax.axis_index("y"), lax.axis_index("c")
        sends, recvs = [], []
        for k in range(1, N_CHIP):
            px, py = _flip(x, (k >> 1) & 1), _flip(y, k & 1)
            for t, (u_ref, r_ref) in enumerate(zip(u_refs, r_refs)):
                sends.append(pltpu.make_async_remote_copy(
                    src_ref=u_ref, dst_ref=r_ref.at[2 * x + y], send_sem=send_sems.at[t, k - 1],
                    recv_sem=recv_sems.at[t, k - 1], device_id=(px, py, c), device_id_type=MESH))
                recvs.append(pltpu.make_async_remote_copy(
                    src_ref=u_ref, dst_ref=r_ref.at[2 * px + py], send_sem=send_sems.at[t, k - 1],
                    recv_sem=recv_sems.at[t, k - 1], device_id=(px, py, c), device_id_type=MESH))
        return sends, recvs


def _pc_cargo(body, cargo, *, name, grid, in_specs, out_specs, out_shape, scratch_shapes=(), route=_Scatter):
    out_specs, out_shape = tuple(out_specs), tuple(out_shape)
    if not cargo:
        return lambda *args: (_pc(body, name=name, grid=grid, in_specs=list(in_specs), out_specs=out_specs,
                                  out_shape=out_shape, scratch_shapes=list(scratch_shapes),
                                  compiler_params=_cp(*["arbitrary"] * len(grid)))(*args), ())
    n_in, n_out, n_u, n_s = len(in_specs), len(out_specs), len(cargo), len(scratch_shapes)

    def wrapped(*refs):
        ins, u_refs = refs[:n_in], refs[n_in:n_in + n_u]
        outs = refs[n_in + n_u:n_in + n_u + n_out]
        r_refs = refs[n_in + n_u + n_out:n_in + 2 * n_u + n_out]
        scratch = refs[n_in + 2 * n_u + n_out:n_in + 2 * n_u + n_out + n_s]
        send_sems, recv_sems = refs[-2:]
        first = last = None
        for axis, extent in enumerate(grid):
            at_start, at_end = pl.program_id(axis) == 0, pl.program_id(axis) == extent - 1
            first = at_start if first is None else first & at_start
            last = at_end if last is None else last & at_end

        @pl.when(first)
        def _():
            sends, _ = route.copies(u_refs, r_refs, send_sems, recv_sems)
            for cp in sends:
                cp.start()

        body(*ins, *outs, *scratch)

        @pl.when(last)
        def _():
            sends, recvs = route.copies(u_refs, r_refs, send_sems, recv_sems)
            for cp in recvs:
                cp.wait_recv()
            for cp in sends:
                cp.wait_send()

    hbm = pl.BlockSpec(memory_space=pltpu.HBM)
    sems = pltpu.SemaphoreType.DMA((n_u, route.peers))
    call = _pc(wrapped, name=name, grid=grid, in_specs=list(in_specs) + [hbm] * n_u, out_specs=out_specs + (hbm,) * n_u,
               out_shape=out_shape + route.shapes(cargo), scratch_shapes=list(scratch_shapes) + [sems, sems],
               compiler_params=_cp(*["arbitrary"] * len(grid)))

    def run(*args):
        res = call(*args, *cargo)
        return tuple(res[:n_out]), tuple(res[n_out:])
    return run


def _swap_halves(name, bufs):
    n_u = len(bufs)

    def body(*refs):
        o_refs = refs[n_u:2 * n_u]
        send_sems, recv_sems = refs[2 * n_u:]
        x, y, c = lax.axis_index("x"), lax.axis_index("y"), lax.axis_index("c")

        def rows(u, core):
            h = bufs[u].shape[0] // 2
            return o_refs[u].at[pl.ds(core * h, h), :]

        sends = [pltpu.make_async_remote_copy(src_ref=rows(u, c), dst_ref=rows(u, c), send_sem=send_sems.at[u],
                                              recv_sem=recv_sems.at[u], device_id=(x, y, 1 - c), device_id_type=MESH)
                 for u in range(n_u)]
        recvs = [pltpu.make_async_remote_copy(src_ref=rows(u, c), dst_ref=rows(u, 1 - c), send_sem=send_sems.at[u],
                                              recv_sem=recv_sems.at[u], device_id=(x, y, 1 - c), device_id_type=MESH)
                 for u in range(n_u)]
        for cp in sends:
            cp.start()
        for cp in recvs:
            cp.wait_recv()
        for cp in sends:
            cp.wait_send()

    hbm = pl.BlockSpec(memory_space=pltpu.HBM)
    return _pc(
        body, name=name, out_shape=tuple(_sds(b.shape, b.dtype) for b in bufs), in_specs=[hbm] * n_u,
        out_specs=(hbm,) * n_u, input_output_aliases={u: u for u in range(n_u)},
        scratch_shapes=[pltpu.SemaphoreType.DMA((n_u,)), pltpu.SemaphoreType.DMA((n_u,))],
    )(*bufs)


def _sum8_into_half(name, unit, received, chip_core):
    _, h, n = received.shape
    tm = _div(h, 256, 16)
    per = h // tm

    def body(cc_ref, u_ref, p_ref, o_ref):
        acc = u_ref[0].astype(F32)
        for s in range(N_DEV - 1):
            acc = acc + p_ref[s].astype(F32)
        o_ref[...] = acc

    grid_spec = pltpu.PrefetchScalarGridSpec(
        num_scalar_prefetch=1, grid=(per,),
        in_specs=[pl.BlockSpec((1, tm, n), lambda i, cc_ref: (cc_ref[0], cc_ref[1] * per + i, 0)),
                  pl.BlockSpec((N_DEV - 1, tm, n), lambda i, cc_ref: (0, i, 0))],
        out_specs=pl.BlockSpec((tm, n), lambda i, cc_ref: (cc_ref[1] * per + i, 0)))
    return _pc(body, name=name, grid_spec=grid_spec, out_shape=_sds((2 * h, n), F32),
               compiler_params=_cp("arbitrary"))(chip_core, unit, received)


def _sum8(name, parts):
    _, m, n = parts.shape
    tm = _div(m, 256, SUBLANE)

    def body(p_ref, o_ref):
        acc = p_ref[0]
        for s in range(1, N_DEV):
            acc = acc + p_ref[s]
        o_ref[...] = acc

    return _pc(body, name=name, grid=(m // tm,), out_shape=_sds((m, n), F32),
               in_specs=[pl.BlockSpec((N_DEV, tm, n), lambda i: (0, i, 0))],
               out_specs=pl.BlockSpec((tm, n), lambda i: (i, 0)), compiler_params=_cp("parallel"))(parts)


def _pack_rows(arrays, dtype, row_mult):
    flat, spans, off = [], [], 0
    for a in arrays:
        flat.append(a.reshape(-1).astype(dtype))
        spans.append((off, a.shape))
        off += a.size
    quantum = row_mult * PACK_COLS
    total = -(-off // quantum) * quantum
    if total > off:
        flat.append(jnp.zeros((total - off,), dtype))
    return jnp.concatenate(flat).reshape(total // PACK_COLS, PACK_COLS), spans


def _size(shape):
    n = 1
    for s in shape:
        n *= s
    return n


def _join_chips(blocks, axis):
    return jnp.concatenate([blocks[j] for j in range(N_CHIP)], axis=axis)


def _cols_by_chip(g):
    k, n = g.shape
    return jnp.transpose(g.reshape(k, N_CHIP, n // N_CHIP), (1, 0, 2))


def _mm_nn(name, pairs, out_dtype, cargo=(), route=_Scatter, tm_cap=1024, tn_cap=1536):
    m = pairs[0][0].shape[0]
    nb, _, n4 = pairs[0][1][0].shape
    tm, tn = _div(m, tm_cap, 16), _div(n4, tn_cap, LANE)
    per = n4 // tn
    n_pairs = len(pairs)

    def body(*refs):
        o_ref = refs[-1]
        acc = jnp.dot(refs[0][...], refs[1][0], preferred_element_type=F32)
        for i in range(1, n_pairs):
            acc = acc + jnp.dot(refs[2 * i][...], refs[2 * i + 1][0], preferred_element_type=F32)
        o_ref[...] = acc.astype(o_ref.dtype)

    in_specs, args = [], []
    for a, (w, r) in pairs:
        k = a.shape[1]
        assert w.shape[0] == nb and w.shape[2] == n4 and w.shape[1] % k == 0
        in_specs += [pl.BlockSpec((tm, k), lambda j, i: (i, 0)),
                     pl.BlockSpec((1, k, tn), functools.partial(lambda j, i, r_: (j // per, r_, j % per), r_=r))]
        args += [a, w]
    if cargo:
        (out,), received = _pc_cargo(body, cargo, name=name, grid=(nb * per, m // tm), in_specs=in_specs, route=route,
                                     out_shape=[_sds((m, nb * n4), out_dtype)],
                                     out_specs=[pl.BlockSpec((tm, tn), lambda j, i: (i, j))])(*args)
        return out, received
    return _pc(body, name=name, grid=(nb * per, m // tm), out_shape=_sds((m, nb * n4), out_dtype), in_specs=in_specs,
               out_specs=pl.BlockSpec((tm, tn), lambda j, i: (i, j)), compiler_params=_cp("parallel", "parallel"))(*args)


def _mm_tn(name, x, ys, n_blocks=1, out_dtype=F32, cargo=(), tt_cap=512):
    t, k = x.shape
    widths = [y.shape[1] for y in ys]
    n4 = sum(widths) // n_blocks
    common = n4
    for w in widths:
        common = _gcd(common, w)
    tk, tn, tt = _div(k, 1536, LANE), _div(common, 1536, LANE), _div(t, tt_cap, 16)
    per = n4 // tn
    starts, acc_w = [], 0
    for w in widths:
        starts.append(acc_w // tn)
        acc_w += w
    counts = [w // tn for w in widths]
    n_y = len(ys)

    def active(i, j):
        return (j >= starts[i]) & (j < starts[i] + counts[i])

    n_t = t // tt

    def body(*refs):
        x_ref, y_refs, o_ref, acc_ref = refs[0], refs[1:1 + n_y], refs[-2], refs[-1]
        j = pl.program_id(1)

        @pl.when(pl.program_id(2) == 0)
        def _():
            acc_ref[...] = jnp.zeros_like(acc_ref)

        for i in range(n_y):
            @pl.when(active(i, j))
            def _():
                acc_ref[...] += lax.dot_general(x_ref[...], y_refs[i][...], (((0,), (0,)), ((), ())),
                                                preferred_element_type=F32)

        @pl.when(pl.program_id(2) == n_t - 1)
        def _():
            o_ref[0] = acc_ref[...].astype(o_ref.dtype)

    def y_spec(i):
        def index(a, j, s):
            on = active(i, j)
            return jnp.where(on, s, 0), jnp.where(on, j - starts[i], 0)
        return pl.BlockSpec((tt, tn), index)

    (out,), received = _pc_cargo(
        body, cargo, name=name, grid=(k // tk, n_blocks * per, n_t), out_shape=[_sds((n_blocks, k, n4), out_dtype)],
        in_specs=[pl.BlockSpec((tt, tk), lambda a, j, s: (s, a))] + [y_spec(i) for i in range(n_y)],
        out_specs=[pl.BlockSpec((1, tk, tn), lambda a, j, s: (j // per, a, j % per))],
        scratch_shapes=[pltpu.VMEM((tk, tn), F32)])(x, *ys)
    return (out, received) if cargo else out


def _gcd(a, b):
    while b:
        a, b = b, a % b
    return a


def _w2(w):
    return (w[None], 0)


def _tok_spec(ts, d):
    return pl.BlockSpec((1, ts, d), lambda b, i: (b, i, 0))


def _seq_spec(d):
    return pl.BlockSpec((1, 1, d), lambda b, i: (b, 0, 0))


def _vec_spec(d):
    return pl.BlockSpec((1, d), lambda b, i: (0, 0))


def _ln_stats(z):
    mu = jnp.mean(z, axis=-1, keepdims=True)
    zc = z - mu
    var = jnp.mean(zc * zc, axis=-1, keepdims=True)
    rstd = lax.rsqrt(var + LN_EPS)
    return zc * rstd, rstd


def _modulate(name, x, sc, sh):
    b, s, d = x.shape
    ts = _div(s, 512, 16)

    def body(x_ref, sc_ref, sh_ref, u_ref):
        u_ref[0] = (x_ref[0] * (1.0 + sc_ref[0]) + sh_ref[0]).astype(BF16)

    return _pc(body, name=name, grid=(b, s // ts), out_shape=_sds(x.shape, BF16),
               in_specs=[_tok_spec(ts, d), _seq_spec(d), _seq_spec(d)], out_specs=_tok_spec(ts, d),
               compiler_params=_cp("parallel", "parallel"))(x, sc, sh)


def _ln_mod_fwd(name, alpha, x, y, g, lng, lnb, sc, sh):
    b, s, d = x.shape
    ts = _div(s, 512, 16)

    def body(x_ref, y_ref, g_ref, lng_ref, lnb_ref, sc_ref, sh_ref, z_ref, xn_ref, u_ref):
        z = alpha * x_ref[0] + (1.0 + g_ref[0]) * y_ref[0]
        xhat, _ = _ln_stats(z)
        xn = xhat * lng_ref[...] + lnb_ref[...]
        z_ref[0] = z
        xn_ref[0] = xn
        u_ref[0] = (xn * (1.0 + sc_ref[0]) + sh_ref[0]).astype(BF16)

    tok, seq, vec = _tok_spec(ts, d), _seq_spec(d), _vec_spec(d)
    return _pc(body, name=name, grid=(b, s // ts),
               out_shape=(_sds(x.shape, F32), _sds(x.shape, F32), _sds(x.shape, BF16)),
               in_specs=[tok, tok, seq, vec, vec, seq, seq], out_specs=(tok, tok, tok),
               compiler_params=_cp("parallel", "parallel"))(x, y, g, lng, lnb, sc, sh)


def _ln_loss_fwd(name, alpha, x, y, g, lng, lnb, target):
    b, s, d = x.shape
    ts = _div(s, 512, 16)

    def body(x_ref, y_ref, g_ref, lng_ref, lnb_ref, t_ref, z_ref, ct_ref, loss_ref):
        @pl.when((pl.program_id(0) == 0) & (pl.program_id(1) == 0))
        def _():
            loss_ref[...] = jnp.zeros_like(loss_ref)
        z = alpha * x_ref[0] + (1.0 + g_ref[0]) * y_ref[0]
        xhat, _ = _ln_stats(z)
        err = xhat * lng_ref[...] + lnb_ref[...] - t_ref[0]
        z_ref[0] = z
        ct_ref[0] = err / d
        part = 0.5 * jnp.sum(jnp.mean(err * err, axis=-1, keepdims=True))
        loss_ref[...] += jnp.full(loss_ref.shape, part, F32)

    tok, seq, vec = _tok_spec(ts, d), _seq_spec(d), _vec_spec(d)
    return _pc(body, name=name, grid=(b, s // ts),
               out_shape=(_sds(x.shape, F32), _sds(x.shape, F32), _sds((SUBLANE, LANE), F32)),
               in_specs=[tok, tok, seq, vec, vec, tok],
               out_specs=(tok, tok, pl.BlockSpec((SUBLANE, LANE), lambda b, i: (0, 0))),
               compiler_params=_cp("arbitrary", "arbitrary"))(x, y, g, lng, lnb, target)


def _sub_bwd(name, alpha, upstream, z, y, g, lng):
    b, s, d = z.shape
    ts = _div(s, 512, 16)
    last = len(upstream) == 1

    def body(*refs):
        if last:
            ct_ref, z_ref, y_ref, g_ref, lng_ref, dz_ref, dy_ref, dg_ref, dlng_ref, dlnb_ref = refs
        else:
            (dzn_ref, dun_ref, lnb_ref, scn_ref, z_ref, y_ref, g_ref, lng_ref,
             dz_ref, dy_ref, dg_ref, dlng_ref, dlnb_ref, dsc_ref, dsh_ref) = refs
        first_tile = pl.program_id(1) == 0

        @pl.when(first_tile & (pl.program_id(0) == 0))
        def _():
            dlng_ref[...] = jnp.zeros_like(dlng_ref)
            dlnb_ref[...] = jnp.zeros_like(dlnb_ref)

        @pl.when(first_tile)
        def _():
            dg_ref[...] = jnp.zeros_like(dg_ref)
            if not last:
                dsc_ref[...] = jnp.zeros_like(dsc_ref)
                dsh_ref[...] = jnp.zeros_like(dsh_ref)

        xhat, rstd = _ln_stats(z_ref[0])
        if last:
            ct = ct_ref[0]
        else:
            dun = dun_ref[0]
            ct = alpha * dzn_ref[0] + dun * (1.0 + scn_ref[0])
            xn = xhat * lng_ref[...] + lnb_ref[...]
            dsc_ref[0] += jnp.sum(dun * xn, axis=0, keepdims=True)
            dsh_ref[0] += jnp.sum(dun, axis=0, keepdims=True)
        dlng_ref[...] += jnp.sum(ct * xhat, axis=0, keepdims=True)
        dlnb_ref[...] += jnp.sum(ct, axis=0, keepdims=True)
        dxhat = ct * lng_ref[...]
        dz = rstd * (dxhat - jnp.mean(dxhat, axis=-1, keepdims=True)
                     - xhat * jnp.mean(dxhat * xhat, axis=-1, keepdims=True))
        dz_ref[0] = dz
        dy_ref[0] = ((1.0 + g_ref[0]) * dz).astype(BF16)
        dg_ref[0] += jnp.sum(dz * y_ref[0], axis=0, keepdims=True)

    tok, seq, vec = _tok_spec(ts, d), _seq_spec(d), _vec_spec(d)
    seq_out = _sds((b, 1, d), F32)
    out_shape = [_sds(z.shape, F32), _sds(z.shape, BF16), seq_out, _sds((1, d), F32), _sds((1, d), F32)]
    out_specs = [tok, tok, seq, vec, vec]
    if last:
        in_specs = [tok, tok, tok, seq, vec]
    else:
        in_specs = [tok, tok, vec, seq, tok, tok, seq, vec]
        out_shape += [seq_out, seq_out]
        out_specs += [seq, seq]
    return _pc(body, name=name, grid=(b, s // ts), out_shape=tuple(out_shape), in_specs=in_specs,
               out_specs=tuple(out_specs), compiler_params=_cp("arbitrary", "arbitrary"))(*upstream, z, y, g, lng)


def _input_bwd(name, alpha, dz, du, x, sc):
    b, s, d = x.shape
    ts = _div(s, 512, 16)

    def body(dz_ref, du_ref, x_ref, sc_ref, gx_ref, dsc_ref, dsh_ref):
        @pl.when(pl.program_id(1) == 0)
        def _():
            dsc_ref[...] = jnp.zeros_like(dsc_ref)
            dsh_ref[...] = jnp.zeros_like(dsh_ref)
        du_ = du_ref[0]
        gx_ref[0] = alpha * dz_ref[0] + du_ * (1.0 + sc_ref[0])
        dsc_ref[0] += jnp.sum(du_ * x_ref[0], axis=0, keepdims=True)
        dsh_ref[0] += jnp.sum(du_, axis=0, keepdims=True)

    tok, seq = _tok_spec(ts, d), _seq_spec(d)
    seq_out = _sds((b, 1, d), F32)
    return _pc(body, name=name, grid=(b, s // ts), out_shape=(_sds(x.shape, F32), seq_out, seq_out),
               in_specs=[tok, tok, tok, seq], out_specs=(tok, seq, seq),
               compiler_params=_cp("parallel", "arbitrary"))(dz, du, x, sc)


def _rows_iota(shape):
    return lax.broadcasted_iota(jnp.int32, shape, 0)


def _back(v, k):
    return pltpu.roll(v, k, axis=0)


def _ahead(v, k):
    return pltpu.roll(v, v.shape[0] - k, axis=0)


def _conv3(ext, w_ref):
    return w_ref[2:3, :] * ext + w_ref[1:2, :] * _back(ext, 1) + w_ref[0:1, :] * _back(ext, 2)


def _conv3_t(dh_ext, w_ref):
    return w_ref[2:3, :] * dh_ext + w_ref[1:2, :] * _ahead(dh_ext, 1) + w_ref[0:1, :] * _ahead(dh_ext, 2)


def _flag(cond):
    return jnp.where(cond, 1.0, 0.0).astype(F32)


def _sigmoid(v):
    return 1.0 / (1.0 + jnp.exp(-v))


def _halo_specs(ts, tc, halo, n_s, col):
    per = ts // halo
    tile = pl.BlockSpec((1, ts, tc), lambda b, i, j: (b, i, col(j)))
    prev = pl.BlockSpec((1, halo, tc), lambda b, i, j: (b, jnp.maximum(i * per - 1, 0), col(j)))
    nxt = pl.BlockSpec((1, halo, tc), lambda b, i, j: (b, jnp.minimum((i + 1) * per, n_s * per - 1), col(j)))
    return tile, prev, nxt


def _convglu_fwd(name, p, cw, cb, cargo=()):
    b, s, f2 = p.shape
    f = f2 // 2
    ts, tc = _div(s, CONV_ROWS, CONV_HALO), _div(f, 256, LANE)
    n_s, n_c = s // ts, f // tc

    def body(pv_ref, pvh_ref, pg_ref, pgh_ref, wv_ref, wg_ref, bv_ref, bg_ref, a_ref):
        keep = _flag(pl.program_id(1) > 0)

        def conv(t_ref, h_ref, w_ref, b_ref):
            ext = jnp.concatenate([h_ref[0] * keep, t_ref[0]], axis=0)
            return _conv3(ext, w_ref)[CONV_HALO:] + b_ref[...]

        val = conv(pv_ref, pvh_ref, wv_ref, bv_ref)
        gate = conv(pg_ref, pgh_ref, wg_ref, bg_ref)
        a_ref[0] = (gate * _sigmoid(gate) * val).astype(BF16)

    tv, hv, _ = _halo_specs(ts, tc, CONV_HALO, n_s, lambda j: j)
    tg, hg, _ = _halo_specs(ts, tc, CONV_HALO, n_s, lambda j: j + n_c)
    wv = pl.BlockSpec((3, tc), lambda b, i, j: (0, j))
    wg = pl.BlockSpec((3, tc), lambda b, i, j: (0, j + n_c))
    bv = pl.BlockSpec((1, tc), lambda b, i, j: (0, j))
    bg = pl.BlockSpec((1, tc), lambda b, i, j: (0, j + n_c))
    (act,), fetched = _pc_cargo(
        body, cargo, name=name, grid=(b, n_s, n_c), route=_Fetch, out_shape=[_sds((b, s, f), BF16)],
        in_specs=[tv, hv, tg, hg, wv, wg, bv, bg],
        out_specs=[pl.BlockSpec((1, ts, tc), lambda b, i, j: (b, i, j))])(p, p, p, p, cw, cw, cb, cb)
    return act, fetched


def _convglu_bwd(name, p, da, cw, cb, cargo=()):
    b, s, f2 = p.shape
    f = f2 // 2
    ts, tc = _div(s, CONV_ROWS, CONV_HALO), _div(f, 256, LANE)
    n_s, n_c = s // ts, f // tc

    def body(pv_ref, pvp_ref, pvn_ref, pg_ref, pgp_ref, pgn_ref, da_ref, dan_ref, wv_ref, wg_ref, bv_ref, bg_ref,
             dpv_ref, dpg_ref, dwv_ref, dwg_ref, dbv_ref, dbg_ref):
        bi, i = pl.program_id(1), pl.program_id(2)

        @pl.when((bi == 0) & (i == 0))
        def _():
            for r in (dwv_ref, dwg_ref, dbv_ref, dbg_ref):
                r[...] = jnp.zeros_like(r)

        keep_prev = _flag(i > 0)
        keep_next = _flag(i < n_s - 1)
        pv_ext = jnp.concatenate([pvp_ref[0] * keep_prev, pv_ref[0], pvn_ref[0]], axis=0)
        pg_ext = jnp.concatenate([pgp_ref[0] * keep_prev, pg_ref[0], pgn_ref[0]], axis=0)
        taps_v = (_back(pv_ext, 2), _back(pv_ext, 1), pv_ext)
        taps_g = (_back(pg_ext, 2), _back(pg_ext, 1), pg_ext)

        def conv(taps, w_ref, b_ref):
            return (w_ref[2:3, :] * taps[2] + w_ref[1:2, :] * taps[1] + w_ref[0:1, :] * taps[0])[CONV_HALO:] + b_ref[...]

        val, gate = conv(taps_v, wv_ref, bv_ref), conv(taps_g, wg_ref, bg_ref)
        da_ext = jnp.concatenate([da_ref[0], dan_ref[0] * keep_next], axis=0)
        sg = _sigmoid(gate)
        dval = da_ext * gate * sg
        dgate = da_ext * val * (sg * (1.0 + gate * (1.0 - sg)))
        dpv_ref[0] = _conv3_t(dval, wv_ref)[:ts].astype(BF16)
        dpg_ref[0] = _conv3_t(dgate, wg_ref)[:ts].astype(BF16)
        for dh, taps, dw_ref, db_ref in ((dval[:ts], taps_v, dwv_ref, dbv_ref), (dgate[:ts], taps_g, dwg_ref, dbg_ref)):
            db_ref[...] += jnp.sum(dh, axis=0, keepdims=True)
            for k in range(3):
                dw_ref[k:k + 1, :] += jnp.sum(dh * taps[k][CONV_HALO:CONV_HALO + ts], axis=0, keepdims=True)

    def specs(col):
        per = ts // CONV_HALO
        tile = pl.BlockSpec((1, ts, tc), lambda j, b, i: (b, i, col(j)))
        prev = pl.BlockSpec((1, CONV_HALO, tc), lambda j, b, i: (b, jnp.maximum(i * per - 1, 0), col(j)))
        nxt = pl.BlockSpec((1, CONV_HALO, tc), lambda j, b, i: (b, jnp.minimum((i + 1) * per, n_s * per - 1), col(j)))
        return tile, prev, nxt

    tv, pvp, pvn = specs(lambda j: j)
    tg, pgp, pgn = specs(lambda j: j + n_c)
    wv = pl.BlockSpec((3, tc), lambda j, b, i: (0, j))
    wg = pl.BlockSpec((3, tc), lambda j, b, i: (0, j + n_c))
    bv = pl.BlockSpec((1, tc), lambda j, b, i: (0, j))
    bg = pl.BlockSpec((1, tc), lambda j, b, i: (0, j + n_c))
    out_tile = pl.BlockSpec((1, ts, tc), lambda j, b, i: (b, i, j))
    acc3, acc1 = pl.BlockSpec((3, tc), lambda j, b, i: (0, j)), pl.BlockSpec((1, tc), lambda j, b, i: (0, j))
    (dpv, dpg, dwv, dwg, dbv, dbg), received = _pc_cargo(
        body, cargo, name=name, grid=(n_c, b, n_s),
        out_shape=(_sds((b, s, f), BF16), _sds((b, s, f), BF16), _sds((3, f), F32), _sds((3, f), F32),
                   _sds((1, f), F32), _sds((1, f), F32)),
        in_specs=[tv, pvp, pvn, tg, pgp, pgn, tv, pvn, wv, wg, bv, bg],
        out_specs=(out_tile, out_tile, acc3, acc3, acc1, acc1))(p, p, p, p, p, p, da, da, cw, cw, cb, cb)
    return dpv, dpg, jnp.concatenate([dwv, dwg], axis=1), jnp.concatenate([dbv, dbg], axis=1), received


def _shortconv_fwd(name, q, cw):
    b, s, d3 = q.shape
    d = d3 // 3
    ts, tc = _div(s, CONV_ROWS, CONV_HALO), _div(d, 256, LANE)
    n_s, n_c = s // ts, d // tc

    def body(gb_ref, gc_ref, gch_ref, h_ref, hh_ref, w_ref, r_ref):
        keep = _flag(pl.program_id(1) > 0)
        m_ext = jnp.concatenate([gch_ref[0] * hh_ref[0] * keep, gc_ref[0] * h_ref[0]], axis=0)
        r_ref[0] = (gb_ref[0] * _conv3(m_ext, w_ref)[CONV_HALO:]).astype(BF16)

    tb, _, _ = _halo_specs(ts, tc, CONV_HALO, n_s, lambda j: j)
    tcc, hc, _ = _halo_specs(ts, tc, CONV_HALO, n_s, lambda j: j + n_c)
    th, hh, _ = _halo_specs(ts, tc, CONV_HALO, n_s, lambda j: j + 2 * n_c)
    w = pl.BlockSpec((3, tc), lambda b, i, j: (0, j))
    return _pc(body, name=name, grid=(b, n_s, n_c), out_shape=_sds((b, s, d), BF16),
               in_specs=[tb, tcc, hc, th, hh, w], out_specs=pl.BlockSpec((1, ts, tc), lambda b, i, j: (b, i, j)),
               compiler_params=_cp("parallel", "parallel", "parallel"))(q, q, q, q, q, cw)


def _shortconv_bwd(name, q, dr, cw):
    b, s, d3 = q.shape
    d = d3 // 3
    ts, tc = _div(s, CONV_ROWS, CONV_HALO), _div(d, 256, LANE)
    n_s, n_c = s // ts, d // tc

    def body(gb_ref, gbn_ref, gc_ref, gcp_ref, h_ref, hp_ref, dr_ref, drn_ref, w_ref,
             dgb_ref, dgc_ref, dh_ref, dw_ref):
        bi, i = pl.program_id(1), pl.program_id(2)

        @pl.when((bi == 0) & (i == 0))
        def _():
            dw_ref[...] = jnp.zeros_like(dw_ref)

        keep_prev = _flag(i > 0)
        keep_next = _flag(i < n_s - 1)
        gc, h = gc_ref[0], h_ref[0]
        m_ext = jnp.concatenate([gcp_ref[0] * hp_ref[0] * keep_prev, gc * h], axis=0)
        cm = _conv3(m_ext, w_ref)[CONV_HALO:]
        dr_ = dr_ref[0]
        dgb_ref[0] = (dr_ * cm).astype(BF16)
        dcv_ext = jnp.concatenate([dr_ * gb_ref[0], drn_ref[0] * gbn_ref[0] * keep_next], axis=0)
        dm = _conv3_t(dcv_ext, w_ref)[:ts]
        dgc_ref[0] = (dm * h).astype(BF16)
        dh_ref[0] = (dm * gc).astype(BF16)
        dcv = dcv_ext[:ts]
        for k in range(3):
            shifted = m_ext if k == 2 else _back(m_ext, 2 - k)
            dw_ref[k:k + 1, :] += jnp.sum(dcv * shifted[CONV_HALO:], axis=0, keepdims=True)

    def specs(col):
        per = ts // CONV_HALO
        tile = pl.BlockSpec((1, ts, tc), lambda j, b, i: (b, i, col(j)))
        prev = pl.BlockSpec((1, CONV_HALO, tc), lambda j, b, i: (b, jnp.maximum(i * per - 1, 0), col(j)))
        nxt = pl.BlockSpec((1, CONV_HALO, tc), lambda j, b, i: (b, jnp.minimum((i + 1) * per, n_s * per - 1), col(j)))
        return tile, prev, nxt

    tb, _, nb = specs(lambda j: j)
    tcc, pc_, _ = specs(lambda j: j + n_c)
    th, ph, _ = specs(lambda j: j + 2 * n_c)
    w = pl.BlockSpec((3, tc), lambda j, b, i: (0, j))
    out_tile = pl.BlockSpec((1, ts, tc), lambda j, b, i: (b, i, j))
    o = _sds((b, s, d), BF16)
    return _pc(body, name=name, grid=(n_c, b, n_s), out_shape=(o, o, o, _sds((3, d), F32)),
               in_specs=[tb, nb, tcc, pc_, th, ph, tb, nb, w], out_specs=(out_tile, out_tile, out_tile, w),
               compiler_params=_cp("parallel", "arbitrary", "arbitrary"))(q, q, q, q, q, q, dr, dr, cw)


def _pick_window(group, cands):
    gid = jnp.full(cands[0].shape, group, jnp.int32)
    out = cands[-1]
    for k in range(len(cands) - 2, -1, -1):
        out = jnp.where(gid == k, cands[k], out)
    return out


def _window_sums(v, shift):
    s1 = v + shift(v, 1)
    s2 = s1 + shift(s1, 2)
    s3 = s2 + shift(s2, 4)
    s4 = s3 + shift(s3, 8)
    return [s1, s2, s3, s4]


def _pool_counts(group, first_row, n_rows, cols):
    t = _rows_iota((n_rows, cols)) + first_row
    window = _pick_window(group, [jnp.full((n_rows, cols), 2 << k, jnp.int32) for k in range(POOL_GROUPS)])
    return jnp.minimum(t + 1, window).astype(F32)


def _pool_fwd(name, x, sc, sh, w, scale):
    b, s, d = x.shape
    tc = d // POOL_GROUPS
    ts = _div(s, 512, POOL_HALO)
    n_s = s // ts

    def body(x_ref, xp_ref, sc_ref, sh_ref, w_ref, scale_ref, y_ref):
        i, grp = pl.program_id(1), pl.program_id(2)
        keep = _flag(i > 0)
        mod = 1.0 + sc_ref[0]
        u = x_ref[0] * mod + sh_ref[0]
        u_ext = jnp.concatenate([(xp_ref[0] * mod + sh_ref[0]) * keep, u], axis=0)
        summed = _pick_window(grp, _window_sums(u_ext, _back))[POOL_HALO:]
        pooled = summed / _pool_counts(grp, i * ts, ts, tc) - u
        y_ref[0] = jnp.dot(pooled.astype(BF16), w_ref[0], preferred_element_type=F32) * scale_ref[...]

    tile, prev, _ = _halo_specs(ts, tc, POOL_HALO, n_s, lambda j: j)
    seq = pl.BlockSpec((1, 1, tc), lambda b, i, j: (b, 0, j))
    return _pc(body, name=name, grid=(b, n_s, POOL_GROUPS), out_shape=_sds(x.shape, F32),
               in_specs=[tile, prev, seq, seq, pl.BlockSpec((1, tc, tc), lambda b, i, j: (j, 0, 0)),
                         pl.BlockSpec((1, tc), lambda b, i, j: (0, j))],
               out_specs=pl.BlockSpec((1, ts, tc), lambda b, i, j: (b, i, j)),
               compiler_params=_cp("parallel", "parallel", "parallel"))(x, x, sc, sh, w, scale)


def _pool_bwd(name, x, sc, sh, dy, w, w_t, scale, cargo=()):
    b, s, d = x.shape
    tc = d // POOL_GROUPS
    ts = _div(s, 512, POOL_HALO)
    n_s = s // ts

    def body(x_ref, xp_ref, sc_ref, sh_ref, dy_ref, dyn_ref, w_ref, wt_ref, scale_ref, du_ref, dw_ref, dscale_ref):
        grp, bi, i = pl.program_id(0), pl.program_id(1), pl.program_id(2)

        @pl.when((bi == 0) & (i == 0))
        def _():
            dw_ref[...] = jnp.zeros_like(dw_ref)
            dscale_ref[...] = jnp.zeros_like(dscale_ref)

        keep_prev = _flag(i > 0)
        keep_next = _flag(i < n_s - 1)
        mod = 1.0 + sc_ref[0]
        u = x_ref[0] * mod + sh_ref[0]
        u_ext = jnp.concatenate([(xp_ref[0] * mod + sh_ref[0]) * keep_prev, u], axis=0)
        summed = _pick_window(grp, _window_sums(u_ext, _back))[POOL_HALO:]
        pooled = (summed / _pool_counts(grp, i * ts, ts, tc) - u).astype(BF16)
        dy_ = dy_ref[0].astype(F32)
        ymat = jnp.dot(pooled, w_ref[0], preferred_element_type=F32)
        dscale_ref[...] += jnp.sum(dy_ * ymat, axis=0, keepdims=True)
        dys_ext = (jnp.concatenate([dy_, dyn_ref[0].astype(F32) * keep_next], axis=0) * scale_ref[...]).astype(BF16)
        dw_ref[0] += lax.dot_general(pooled, dys_ext[:ts], (((0,), (0,)), ((), ())), preferred_element_type=F32)
        dpooled = jnp.dot(dys_ext, wt_ref[0], preferred_element_type=F32)
        e = dpooled / _pool_counts(grp, i * ts, ts + POOL_HALO, tc)
        du_ref[0] = _pick_window(grp, _window_sums(e, _ahead))[:ts] - dpooled[:ts]

    per = ts // POOL_HALO
    tile = pl.BlockSpec((1, ts, tc), lambda j, b, i: (b, i, j))
    prev = pl.BlockSpec((1, POOL_HALO, tc), lambda j, b, i: (b, jnp.maximum(i * per - 1, 0), j))
    nxt = pl.BlockSpec((1, POOL_HALO, tc), lambda j, b, i: (b, jnp.minimum((i + 1) * per, n_s * per - 1), j))
    seq = pl.BlockSpec((1, 1, tc), lambda j, b, i: (b, 0, j))
    wsp = pl.BlockSpec((1, tc, tc), lambda j, b, i: (j, 0, 0))
    vec = pl.BlockSpec((1, tc), lambda j, b, i: (0, j))
    (du, dw, dscale), received = _pc_cargo(
        body, cargo, name=name, grid=(POOL_GROUPS, b, n_s),
        out_shape=(_sds(x.shape, F32), _sds((POOL_GROUPS, tc, tc), F32), _sds((1, d), F32)),
        in_specs=[tile, prev, seq, seq, tile, nxt, wsp, wsp, vec],
        out_specs=(tile, wsp, vec))(x, x, sc, sh, dy, dy, w, w_t, scale)
    return du, dw, dscale, received


def _rope_swap(v):
    lane = lax.broadcasted_iota(jnp.int32, v.shape, v.ndim - 1)
    lo, hi = QK_NOPE, QK_NOPE + QK_ROPE // 2
    from_above = pltpu.roll(v, HEAD_PAD - QK_ROPE // 2, axis=v.ndim - 1)
    from_below = pltpu.roll(v, QK_ROPE // 2, axis=v.ndim - 1)
    return jnp.where((lane >= lo) & (lane < hi), from_above,
                     jnp.where((lane >= hi) & (lane < hi + QK_ROPE // 2), from_below, 0.0))


def _rope(v, cos_t, sin_t):
    return v * cos_t + _rope_swap(v) * sin_t


def _rope_t(dv, cos_t, sin_t):
    return dv * cos_t + _rope_swap(dv * sin_t)


def _rms(v, g):
    r = lax.rsqrt(jnp.mean(v * v, axis=-1, keepdims=True) + RMS_EPS)
    return v * r, r


def _mla_norm_fwd(name, a, qn, kvn, cos_t, sin_t):
    b, s, wa = a.shape
    ql, kvl = qn.shape[1], kvn.shape[1]
    ts = _div(s, 512, 16)

    def body(aq_ref, akv_ref, ape_ref, qn_ref, kvn_ref, cos_ref, sin_ref, cq_ref, ckv_ref, kpe_ref):
        yq, _ = _rms(aq_ref[0], None)
        cq_ref[0] = (yq * qn_ref[...]).astype(BF16)
        ykv, _ = _rms(akv_ref[0], None)
        ckv_ref[0] = (ykv * kvn_ref[...]).astype(BF16)
        kpe_ref[0] = _rope(ape_ref[0], cos_ref[0], sin_ref[0])

    tok = lambda w, col: pl.BlockSpec((1, ts, w), lambda b, i: (b, i, col))
    return _pc(body, name=name, grid=(b, s // ts),
               out_shape=(_sds((b, s, ql), BF16), _sds((b, s, kvl), BF16), _sds((b, s, HEAD_PAD), F32)),
               in_specs=[tok(ql, 0), tok(kvl, ql // kvl), tok(HEAD_PAD, (ql + kvl) // HEAD_PAD), _vec_spec(ql),
                         _vec_spec(kvl), tok(HEAD_PAD, 0), tok(HEAD_PAD, 0)],
               out_specs=(tok(ql, 0), tok(kvl, 0), tok(HEAD_PAD, 0)),
               compiler_params=_cp("parallel", "parallel"))(a, a, a, qn, kvn, cos_t, sin_t)


def _mla_norm_bwd(name, a, dcq, dckv, dkpe, qn, kvn):
    b, s, wa = a.shape
    ql, kvl = qn.shape[1], kvn.shape[1]
    ts = _div(s, 512, 16)

    def body(a_ref, dcq_ref, dckv_ref, dkpe_ref, qn_ref, kvn_ref, da_ref, dqn_ref, dkvn_ref):
        @pl.when((pl.program_id(0) == 0) & (pl.program_id(1) == 0))
        def _():
            dqn_ref[...] = jnp.zeros_like(dqn_ref)
            dkvn_ref[...] = jnp.zeros_like(dkvn_ref)

        def one(v, dc, g_ref, dg_ref):
            yv, r = _rms(v, None)
            dg_ref[...] += jnp.sum(dc * yv, axis=0, keepdims=True)
            dyv = dc * g_ref[...]
            return r * (dyv - yv * jnp.mean(dyv * yv, axis=-1, keepdims=True))

        av = a_ref[0]
        da_ref[0, :, 0:ql] = one(av[:, 0:ql], dcq_ref[0], qn_ref, dqn_ref).astype(BF16)
        da_ref[0, :, ql:ql + kvl] = one(av[:, ql:ql + kvl], dckv_ref[0], kvn_ref, dkvn_ref).astype(BF16)
        da_ref[0, :, ql + kvl:] = dkpe_ref[0].astype(BF16)

    return _pc(body, name=name, grid=(b, s // ts),
               out_shape=(_sds(a.shape, BF16), _sds((1, ql), F32), _sds((1, kvl), F32)),
               in_specs=[_tok_spec(ts, wa), _tok_spec(ts, ql), _tok_spec(ts, kvl), _tok_spec(ts, HEAD_PAD),
                         _vec_spec(ql), _vec_spec(kvl)],
               out_specs=(_tok_spec(ts, wa), _vec_spec(ql), _vec_spec(kvl)),
               compiler_params=_cp("arbitrary", "arbitrary"))(a, dcq, dckv, dkpe, qn, kvn)


def _mla_prep_fwd(name, q_raw, kv_raw, kpe, cos_t, sin_t, n_heads):
    b, s, wq = q_raw.shape
    ts = _div(s, 256, 16)

    def body(q_ref, k_ref, v_ref, kpe_ref, cos_ref, sin_ref, qo_ref, ko_ref, vo_ref):
        cos_, sin_, kpe_ = cos_ref[0], sin_ref[0], kpe_ref[0]
        for h in range(n_heads):
            lanes = slice(h * HEAD_PAD, (h + 1) * HEAD_PAD)
            qo_ref[0, :, lanes] = _rope(q_ref[0, :, lanes], cos_, sin_).astype(BF16)
            ko_ref[0, :, lanes] = (k_ref[0, :, lanes] + kpe_).astype(BF16)
        vo_ref[0] = v_ref[0].astype(BF16)

    wide = lambda part: pl.BlockSpec((1, ts, wq), lambda b, i: (b, i, part))
    tok = pl.BlockSpec((1, ts, HEAD_PAD), lambda b, i: (b, i, 0))
    o = _sds(q_raw.shape, BF16)
    return _pc(body, name=name, grid=(b, s // ts), out_shape=(o, o, o),
               in_specs=[wide(0), wide(0), wide(1), tok, tok, tok], out_specs=(wide(0), wide(0), wide(0)),
               compiler_params=_cp("parallel", "parallel"))(q_raw, kv_raw, kv_raw, kpe, cos_t, sin_t)


def _mla_prep_bwd(name, dq, dk, dv, cos_t, sin_t, n_heads):
    b, s, wq = dq.shape
    ts = _div(s, 256, 16)

    def body(dq_ref, dk_ref, dv_ref, cos_ref, sin_ref, dqr_ref, dkv_ref, dkpe_ref):
        cos_, sin_ = cos_ref[0], sin_ref[0]
        dk_sum = None
        for h in range(n_heads):
            lanes = slice(h * HEAD_PAD, (h + 1) * HEAD_PAD)
            dqr_ref[0, :, lanes] = _rope_t(dq_ref[0, :, lanes], cos_, sin_).astype(BF16)
            dk_h = dk_ref[0, :, lanes]
            dkv_ref[0, :, lanes] = dk_h.astype(BF16)
            dk_sum = dk_h if dk_sum is None else dk_sum + dk_h
        dkv_ref[0, :, wq:2 * wq] = dv_ref[0]
        dkpe_ref[0] = _rope_t(dk_sum, cos_, sin_)

    wide = pl.BlockSpec((1, ts, wq), lambda b, i: (b, i, 0))
    both = pl.BlockSpec((1, ts, 2 * wq), lambda b, i: (b, i, 0))
    tok = pl.BlockSpec((1, ts, HEAD_PAD), lambda b, i: (b, i, 0))
    return _pc(body, name=name, grid=(b, s // ts),
               out_shape=(_sds(dq.shape, BF16), _sds((b, s, 2 * wq), BF16), _sds((b, s, HEAD_PAD), F32)),
               in_specs=[wide, wide, wide, tok, tok], out_specs=(wide, both, tok),
               compiler_params=_cp("parallel", "parallel"))(dq, dk, dv, cos_t, sin_t)


FLASH_TILE = 1024
LOG2_E = 1.4426950408889634


def _heads_per_step(n_heads):
    return 2 if n_heads % 2 == 0 else 1


def _causal_mask(i, j, tq, tk):
    rows = lax.broadcasted_iota(jnp.int32, (tq, tk), 0) + i * tq
    cols = lax.broadcasted_iota(jnp.int32, (tq, tk), 1) + j * tk
    return cols <= rows


def _nt(a, b):
    return lax.dot_general(a, b, (((1,), (1,)), ((), ())), preferred_element_type=F32)


def _tn(a, b):
    return lax.dot_general(a, b, (((0,), (0,)), ((), ())), preferred_element_type=F32)


def _flash_fwd(name, q, k, v, n_heads, sm_scale, cargo=()):
    b, s, _ = q.shape
    t, hp = _div(s, FLASH_TILE, LANE), _heads_per_step(n_heads)
    n, w = s // t, hp * HEAD_PAD
    neg = float(jnp.finfo(jnp.float32).min)
    c2 = sm_scale * LOG2_E

    def body(q_ref, k_ref, v_ref, o_ref, lse_ref, m_ref, l_ref, acc_ref):
        i, j = pl.program_id(2), pl.program_id(3)

        @pl.when(j == 0)
        def _():
            m_ref[...] = jnp.full(m_ref.shape, neg, F32)
            l_ref[...] = jnp.zeros_like(l_ref)
            acc_ref[...] = jnp.zeros_like(acc_ref)

        def block(on_diagonal):
            for hh in range(hp):
                ln = slice(hh * HEAD_PAD, (hh + 1) * HEAD_PAD)
                sc = _nt(q_ref[0, :, ln], k_ref[0, :, ln])
                if on_diagonal:
                    sc = jnp.where(_causal_mask(i, j, t, t), sc, neg)
                m_old = m_ref[hh]
                m_new = jnp.maximum(m_old, jnp.max(sc, axis=-1, keepdims=True))
                p = jnp.exp2((sc - m_new) * c2)
                corr = jnp.exp2((m_old - m_new) * c2)
                l_ref[hh] = corr * l_ref[hh] + jnp.sum(p, axis=-1, keepdims=True)
                acc_ref[:, ln] = corr * acc_ref[:, ln] + jnp.dot(p.astype(BF16), v_ref[0, :, ln],
                                                                 preferred_element_type=F32)
                m_ref[hh] = m_new

        pl.when(j < i)(functools.partial(block, False))
        pl.when(j == i)(functools.partial(block, True))

        @pl.when(j == n - 1)
        def _():
            for hh in range(hp):
                ln = slice(hh * HEAD_PAD, (hh + 1) * HEAD_PAD)
                o_ref[0, :, ln] = (acc_ref[:, ln] / l_ref[hh]).astype(BF16)
                lse_ref[0, :, ln] = jnp.broadcast_to(m_ref[hh] * sm_scale + jnp.log(l_ref[hh]), (t, HEAD_PAD))

    qs = pl.BlockSpec((1, t, w), lambda b, h, i, j: (b, i, h))
    ks = pl.BlockSpec((1, t, w), lambda b, h, i, j: (b, jnp.minimum(j, i), h))
    (o, lse), fetched = _pc_cargo(
        body, cargo, name=name, grid=(b, n_heads // hp, n, n), route=_Fetch,
        out_shape=(_sds(q.shape, BF16), _sds(q.shape, F32)), in_specs=[qs, ks, ks], out_specs=(qs, qs),
        scratch_shapes=[pltpu.VMEM((hp, t, 1), F32), pltpu.VMEM((hp, t, 1), F32), pltpu.VMEM((t, w), F32)])(q, k, v)
    return o, lse, fetched


def _flash_dq(name, q, k, v, o, lse, do, n_heads, sm_scale):
    b, s, _ = q.shape
    t, hp = _div(s, FLASH_TILE, LANE), _heads_per_step(n_heads)
    n, w = s // t, hp * HEAD_PAD
    c2 = sm_scale * LOG2_E

    def body(q_ref, k_ref, v_ref, o_ref, lse_ref, do_ref, dq_ref, acc_ref, delta_ref):
        i, j = pl.program_id(2), pl.program_id(3)

        @pl.when(j == 0)
        def _():
            acc_ref[...] = jnp.zeros_like(acc_ref)
            for hh in range(hp):
                ln = slice(hh * HEAD_PAD, (hh + 1) * HEAD_PAD)
                delta_ref[hh] = jnp.sum(do_ref[0, :, ln].astype(F32) * o_ref[0, :, ln].astype(F32), axis=-1,
                                        keepdims=True)

        def block(on_diagonal):
            for hh in range(hp):
                ln = slice(hh * HEAD_PAD, (hh + 1) * HEAD_PAD)
                sc = _nt(q_ref[0, :, ln], k_ref[0, :, ln])
                p = jnp.exp2(sc * c2 - lse_ref[0, :, hh * HEAD_PAD:hh * HEAD_PAD + 1] * LOG2_E)
                if on_diagonal:
                    p = jnp.where(_causal_mask(i, j, t, t), p, 0.0)
                dp = _nt(do_ref[0, :, ln], v_ref[0, :, ln])
                ds = p * (dp - delta_ref[hh])
                acc_ref[:, ln] += jnp.dot(ds.astype(BF16), k_ref[0, :, ln], preferred_element_type=F32)

        pl.when(j < i)(functools.partial(block, False))
        pl.when(j == i)(functools.partial(block, True))

        @pl.when(j == n - 1)
        def _():
            dq_ref[0] = acc_ref[...] * sm_scale

    qs = pl.BlockSpec((1, t, w), lambda b, h, i, j: (b, i, h))
    ks = pl.BlockSpec((1, t, w), lambda b, h, i, j: (b, jnp.minimum(j, i), h))
    return _pc(body, name=name, grid=(b, n_heads // hp, n, n), out_shape=_sds(q.shape, F32),
               in_specs=[qs, ks, ks, qs, qs, qs], out_specs=qs,
               scratch_shapes=[pltpu.VMEM((t, w), F32), pltpu.VMEM((hp, t, 1), F32)],
               compiler_params=_cp("parallel", "parallel", "parallel", "arbitrary"))(q, k, v, o, lse, do)


def _flash_dkv(name, q, k, v, o, lse, do, n_heads, sm_scale):
    b, s, _ = q.shape
    t, hp = _div(s, FLASH_TILE, LANE), _heads_per_step(n_heads)
    n, w = s // t, hp * HEAD_PAD
    c2 = sm_scale * LOG2_E

    def body(q_ref, k_ref, v_ref, o_ref, lse_ref, do_ref, dk_ref, dv_ref, dk_acc, dv_acc):
        j, i = pl.program_id(2), pl.program_id(3)

        @pl.when(i == 0)
        def _():
            dk_acc[...] = jnp.zeros_like(dk_acc)
            dv_acc[...] = jnp.zeros_like(dv_acc)

        def block(on_diagonal):
            for hh in range(hp):
                ln = slice(hh * HEAD_PAD, (hh + 1) * HEAD_PAD)
                do_ = do_ref[0, :, ln]
                delta = jnp.sum(do_.astype(F32) * o_ref[0, :, ln].astype(F32), axis=-1, keepdims=True)
                sc = _nt(q_ref[0, :, ln], k_ref[0, :, ln])
                p = jnp.exp2(sc * c2 - lse_ref[0, :, hh * HEAD_PAD:hh * HEAD_PAD + 1] * LOG2_E)
                if on_diagonal:
                    p = jnp.where(_causal_mask(i, j, t, t), p, 0.0)
                dv_acc[:, ln] += _tn(p.astype(BF16), do_)
                dp = _nt(do_, v_ref[0, :, ln])
                ds = p * (dp - delta)
                dk_acc[:, ln] += _tn(ds.astype(BF16), q_ref[0, :, ln])

        pl.when(i > j)(functools.partial(block, False))
        pl.when(i == j)(functools.partial(block, True))

        @pl.when(i == n - 1)
        def _():
            dk_ref[0] = dk_acc[...] * sm_scale
            dv_ref[0] = dv_acc[...].astype(BF16)

    qs = pl.BlockSpec((1, t, w), lambda b, h, j, i: (b, jnp.maximum(i, j), h))
    ks = pl.BlockSpec((1, t, w), lambda b, h, j, i: (b, j, h))
    return _pc(body, name=name, grid=(b, n_heads // hp, n, n), out_shape=(_sds(q.shape, F32), _sds(q.shape, BF16)),
               in_specs=[qs, ks, ks, qs, qs, qs], out_specs=(ks, ks),
               scratch_shapes=[pltpu.VMEM((t, w), F32), pltpu.VMEM((t, w), F32)],
               compiler_params=_cp("parallel", "parallel", "parallel", "arbitrary"))(q, k, v, o, lse, do)


def _mod_fwd(name, c_all, w, bias):
    depth, d, n = w.shape
    rows = c_all.shape[0]

    def body(c_ref, w_ref, b_ref, o_ref):
        cv = c_ref[...]
        cond = (cv * _sigmoid(cv)).astype(BF16)
        o_ref[0] = jnp.dot(cond, w_ref[0].astype(BF16), preferred_element_type=F32) + b_ref[0]

    return _pc(body, name=name, grid=(depth,), out_shape=_sds((depth, rows, n), F32),
               in_specs=[pl.BlockSpec((rows, d), lambda l: (0, 0)), pl.BlockSpec((1, d, n), lambda l: (l, 0, 0)),
                         pl.BlockSpec((1, 1, n), lambda l: (l, 0, 0))],
               out_specs=pl.BlockSpec((1, rows, n), lambda l: (l, 0, 0)), compiler_params=_cp("parallel"))(c_all, w, bias)


def _mod_bwd(name, c_all, dmod_cols, dmod_all):
    depth, rows, n = dmod_cols.shape
    d = c_all.shape[1]
    n_all = dmod_all.shape[2]
    tn = _div(n, 512, LANE)

    def body(c_ref, dm_ref, dma_ref, gw_ref, gb_ref):
        cv = c_ref[...]
        cond = (cv * _sigmoid(cv)).astype(BF16)
        gw_ref[0] = _tn(cond, dm_ref[0].astype(BF16))

        @pl.when(pl.program_id(1) == 0)
        def _():
            gb_ref[0] = jnp.sum(dma_ref[0], axis=0, keepdims=True)

    return _pc(body, name=name, grid=(depth, n // tn),
               out_shape=(_sds((depth, d, n), F32), _sds((depth, 1, n_all), F32)),
               in_specs=[pl.BlockSpec((rows, d), lambda l, j: (0, 0)), pl.BlockSpec((1, rows, tn), lambda l, j: (l, 0, j)),
                         pl.BlockSpec((1, rows, n_all), lambda l, j: (l, 0, 0))],
               out_specs=(pl.BlockSpec((1, d, tn), lambda l, j: (l, 0, j)), pl.BlockSpec((1, 1, n_all), lambda l, j: (l, 0, 0))),
               compiler_params=_cp("parallel", "arbitrary"))(c_all, dmod_cols, dmod_all)


def _adamw(name, w, g, m, v):
    shape = w.shape
    cols = shape[-1]
    rows = _size(shape) // cols
    tr = _div(rows, max(SUBLANE, (2 ** 19) // cols // SUBLANE * SUBLANE), SUBLANE)
    c1 = 1.0 - ADAM_B1 ** ADAM_STEP
    c2 = 1.0 - ADAM_B2 ** ADAM_STEP

    def body(w_ref, g_ref, m_ref, v_ref, d_ref, mo_ref, vo_ref):
        gv = g_ref[...]
        m_new = ADAM_B1 * m_ref[...] + (1.0 - ADAM_B1) * gv
        v_new = ADAM_B2 * v_ref[...] + (1.0 - ADAM_B2) * (gv * gv)
        m_hat = m_new / c1
        v_hat = v_new / c2
        d_ref[...] = -ADAM_LR * (m_hat / (jnp.sqrt(v_hat) + ADAM_EPS) + ADAM_WD * w_ref[...])
        mo_ref[...] = m_new
        vo_ref[...] = v_new

    spec = pl.BlockSpec((tr, cols), lambda i: (i, 0))
    o = _sds((rows, cols), F32)
    outs = _pc(body, name=name, grid=(rows // tr,), out_shape=(o, o, o), in_specs=[spec] * 4, out_specs=(spec,) * 3,
               compiler_params=_cp("parallel"))(*[a.reshape(rows, cols) for a in (w, g, m, v)])
    return tuple(a.reshape(shape) for a in outs)


def _rope_tables(positions):
    half = QK_ROPE // 2
    inv_freq = ROPE_THETA ** (-jnp.arange(0, QK_ROPE, 2, dtype=F32) / QK_ROPE)
    ang = positions.astype(F32)[..., None] * inv_freq
    cos, sin = jnp.cos(ang), jnp.sin(ang)
    lead = positions.shape
    ones = jnp.ones(lead + (QK_NOPE,), F32)
    tail_one = jnp.ones(lead + (HEAD_PAD - QK_NOPE - QK_ROPE,), F32)
    cos_t = jnp.concatenate([ones, cos, cos, tail_one], axis=-1)
    sin_t = jnp.concatenate([0 * ones, -sin, sin, 0 * tail_one], axis=-1)
    return cos_t, sin_t


def _pad_heads(w, n_heads, parts, axis):
    w = jnp.moveaxis(w, axis, -1)
    lead = w.shape[:-1]
    per = w.shape[-1] // n_heads
    w = w.reshape(lead + (n_heads, per))
    kept = jnp.concatenate([w[..., a:b_] for a, b_ in parts], axis=-1)
    pad = HEAD_PAD - kept.shape[-1]
    kept = jnp.concatenate([kept, jnp.zeros(lead + (n_heads, pad), w.dtype)], axis=-1)
    return jnp.moveaxis(kept.reshape(lead + (n_heads * HEAD_PAD,)), -1, axis)


def _unpad_heads(g, n_heads, width, axis):
    g = jnp.moveaxis(g, axis, -1)
    lead = g.shape[:-1]
    g = g.reshape(lead + (n_heads, HEAD_PAD))[..., :width]
    return g, lead


def kernel(x, c, positions, mod_w, mod_b, ln_g, ln_b, pool_w, pool_scale, mla_w_a, mla_q_norm, mla_w_uq, mla_kv_norm, mla_w_ukv, mla_w_o, sc_w_in, sc_conv, sc_w_out, ffn_w_up, ffn_conv, ffn_conv_b, ffn_w_down, loss_target, m_mod_w, m_mod_b, m_ln_g, m_ln_b, m_pool_w, m_pool_scale, m_mla_w_a, m_mla_q_norm, m_mla_w_uq, m_mla_kv_norm, m_mla_w_ukv, m_mla_w_o, m_sc_w_in, m_sc_conv, m_sc_w_out, m_ffn_w_up, m_ffn_conv, m_ffn_conv_b, m_ffn_w_down, v_mod_w, v_mod_b, v_ln_g, v_ln_b, v_pool_w, v_pool_scale, v_mla_w_a, v_mla_q_norm, v_mla_w_uq, v_mla_kv_norm, v_mla_w_ukv, v_mla_w_o, v_sc_w_in, v_sc_conv, v_sc_w_out, v_ffn_w_up, v_ffn_conv, v_ffn_conv_b, v_ffn_w_down):
    wts = dict(mod_w=mod_w, mod_b=mod_b, ln_g=ln_g, ln_b=ln_b, pool_w=pool_w, pool_scale=pool_scale, mla_w_a=mla_w_a,
               mla_q_norm=mla_q_norm, mla_w_uq=mla_w_uq, mla_kv_norm=mla_kv_norm, mla_w_ukv=mla_w_ukv, mla_w_o=mla_w_o,
               sc_w_in=sc_w_in, sc_conv=sc_conv, sc_w_out=sc_w_out, ffn_w_up=ffn_w_up, ffn_conv=ffn_conv,
               ffn_conv_b=ffn_conv_b, ffn_w_down=ffn_w_down)
    mom1 = dict(mod_w=m_mod_w, mod_b=m_mod_b, ln_g=m_ln_g, ln_b=m_ln_b, pool_w=m_pool_w, pool_scale=m_pool_scale,
                mla_w_a=m_mla_w_a, mla_q_norm=m_mla_q_norm, mla_w_uq=m_mla_w_uq, mla_kv_norm=m_mla_kv_norm,
                mla_w_ukv=m_mla_w_ukv, mla_w_o=m_mla_w_o, sc_w_in=m_sc_w_in, sc_conv=m_sc_conv, sc_w_out=m_sc_w_out,
                ffn_w_up=m_ffn_w_up, ffn_conv=m_ffn_conv, ffn_conv_b=m_ffn_conv_b, ffn_w_down=m_ffn_w_down)
    mom2 = dict(mod_w=v_mod_w, mod_b=v_mod_b, ln_g=v_ln_g, ln_b=v_ln_b, pool_w=v_pool_w, pool_scale=v_pool_scale,
                mla_w_a=v_mla_w_a, mla_q_norm=v_mla_q_norm, mla_w_uq=v_mla_w_uq, mla_kv_norm=v_mla_kv_norm,
                mla_w_ukv=v_mla_w_ukv, mla_w_o=v_mla_w_o, sc_w_in=v_sc_w_in, sc_conv=v_sc_conv, sc_w_out=v_sc_w_out,
                ffn_w_up=v_ffn_w_up, ffn_conv=v_ffn_conv, ffn_conv_b=v_ffn_conv_b, ffn_w_down=v_ffn_w_down)

    bsz, seq, d = x.shape
    depth = mod_b.shape[0]
    n_tok = bsz * seq
    n_heads = d // V_HEAD
    ql, kvl = mla_q_norm.shape[1], mla_kv_norm.shape[1]
    alpha = float((2 * depth) ** 0.25)
    sm_scale = float((QK_NOPE + QK_ROPE) ** -0.5)
    mx, my, mc = lax.axis_index("x"), lax.axis_index("y"), lax.axis_index("c")
    chip = 2 * mx + my
    dev = 2 * chip + mc

    small_names = list(SMALL_SHARDED)
    small_pack, small_spans = _pack_rows([c] + [wts[n] for n in small_names], F32, SUBLANE)
    rows_small = small_pack.shape[0]
    small_all = _all_gather8("gather_small_params", small_pack, True).reshape(N_DEV, rows_small * PACK_COLS)
    c_all = small_all[:, :c.size].reshape(N_DEV * bsz, d)
    per_chip = small_all[0::2]
    full = dict(wts)
    for n, (off, shape) in zip(small_names, small_spans[1:]):
        blocks = per_chip[:, off:off + _size(shape)].reshape((N_CHIP,) + tuple(shape))
        full[n] = _join_chips(blocks, SMALL_SHARDED[n])

    n_mod = mod_w.shape[2]
    bias_cols = lax.dynamic_slice_in_dim(mod_b, chip * n_mod, n_mod, axis=1)[:, None, :]
    mod_cols = _mod_fwd("mod_fwd", c_all, mod_w, bias_cols)
    half_rows = (N_DEV * bsz) // 2
    mod_half = lax.dynamic_slice_in_dim(mod_cols, mc * half_rows, half_rows, axis=1).reshape(depth * half_rows, n_mod)
    mod_all = _all_gather8("gather_mod", mod_half, True).reshape(N_CHIP, 2, depth, half_rows, n_mod)
    mod_all = jnp.transpose(mod_all, (2, 1, 3, 0, 4)).reshape(depth, N_DEV * bsz, N_CHIP * n_mod)
    mod_mine = lax.dynamic_slice_in_dim(mod_all, dev * bsz, bsz, axis=1)
    mods = [[mod_mine[l, :, k * d:(k + 1) * d][:, None, :] for k in range(6)] for l in range(depth)]

    big_names = list(BIG)
    f_hid = ffn_w_down.shape[1] * N_CHIP
    host_layer = 1
    assert depth > host_layer

    def layer_of(n, i):
        if n == 'pool_w':
            return 3 * i
        if n.startswith('mla_'):
            return 3 * i + 1
        if n.startswith('sc_'):
            return 3 * i + 2
        return i

    last_layer_of_group = (0, host_layer)

    def group_of(n, i):
        return sum(layer_of(n, i) > top for top in last_layer_of_group)

    n_groups = len(last_layer_of_group) + 1
    span = {n: [[i for i in range(wts[n].shape[0]) if group_of(n, i) == g] for g in range(n_groups)] for n in big_names}
    members = [[n for n in big_names if span[n][g]] for g in range(n_groups)]

    def rows2d(a):
        return a.astype(BF16).reshape(-1, a.shape[-1])

    def layouts(bc):
        out = {}
        for n in ('pool_w', 'mla_w_a', 'mla_w_uq', 'mla_w_ukv', 'mla_w_o', 'sc_w_out'):
            if n in bc:
                out[n] = jnp.concatenate([bc[n][j] for j in range(N_CHIP)], axis=BIG[n])
        if 'ffn_w_up' in bc:
            nl = bc['ffn_w_up'].shape[1]
            out['up_cols'] = bc['ffn_w_up'].reshape(N_CHIP, nl * d, -1)
            out['up_rows'] = jnp.transpose(bc['ffn_w_up'], (1, 0, 3, 2)).reshape(1, nl * 2 * f_hid, d)
            out['down_rows'] = jnp.transpose(bc['ffn_w_down'], (1, 0, 2, 3)).reshape(1, nl * f_hid, d)
            out['down_t'] = jnp.transpose(bc['ffn_w_down'], (1, 3, 0, 2)).reshape(1, nl * d, f_hid)
        if 'sc_w_in' in bc:
            ns = bc['sc_w_in'].shape[1]
            out['in_cols'] = bc['sc_w_in'].reshape(N_CHIP, ns * d, -1)
            out['in_rows'] = jnp.transpose(bc['sc_w_in'], (1, 0, 3, 2)).reshape(1, ns * 3 * d, d)
        return out

    shards = [{n: rows2d(wts[n][span[n][g][0]:span[n][g][-1] + 1]) for n in members[g]} for g in range(n_groups)]
    lay = [None] * n_groups

    def by_chip(g, n, blocks):
        return blocks.reshape((N_CHIP, len(span[n][g])) + wts[n].shape[1:])

    def fetched_group(g, got):
        lay[g] = layouts({n: by_chip(g, n, lax.dynamic_update_index_in_dim(blocks, shards[g][n], chip, 0))
                          for n, blocks in got.items()})

    gathered = _gather_weights("gather_weights", [shards[0][n] for n in members[0]])
    lay[0] = layouts({n: by_chip(0, n, blocks) for n, blocks in zip(members[0], gathered)})

    def grp(n, i):
        g = group_of(n, i)
        return lay[g], i - span[n][g][0]

    nope_rope = [(0, QK_NOPE + QK_ROPE)]
    cos_t, sin_t = _rope_tables(positions)

    def t2(a):
        return a.reshape(n_tok, a.shape[-1])

    def t3(a):
        return a.reshape(bsz, seq, a.shape[-1])

    saved = []
    xin = x
    u = _modulate("modulate_in", x, mods[0][1], mods[0][0])
    loss_acc = None
    for l in range(depth):
        sh1, sc1, g1, sh2, sc2, g2 = mods[l]
        kind, j = l % 3, l // 3
        st = dict(x=xin)
        if kind == 0:
            grp_l, jj = grp('pool_w', j)
            w = grp_l['pool_w'][jj]
            st.update(w=w, w_t=jnp.swapaxes(w, 1, 2), scale=full['pool_scale'][j][None, :])
            y = _pool_fwd(f"pool_fwd_{l}", xin, sc1, sh1, st['w'], st['scale'])
        elif kind == 1:
            grp_l, jj = grp('mla_w_a', j)
            wa, wuq, wukv = grp_l['mla_w_a'][jj], grp_l['mla_w_uq'][jj], grp_l['mla_w_ukv'][jj]
            zeros = jnp.zeros((d, QK_NOPE), BF16)
            w_a = jnp.concatenate([wa[:, :ql + kvl], zeros, wa[:, ql + kvl:], zeros[:, :HEAD_PAD - QK_NOPE - QK_ROPE]], axis=1)
            w_uq = _pad_heads(wuq, n_heads, nope_rope, 1)
            w_kv = jnp.concatenate([_pad_heads(wukv, n_heads, [(0, QK_NOPE)], 1),
                                    _pad_heads(wukv, n_heads, [(QK_NOPE, QK_NOPE + V_HEAD)], 1)], axis=1)
            w_o = _pad_heads(grp_l['mla_w_o'][jj], n_heads, [(0, V_HEAD)], 0)
            qn, kvn = mla_q_norm[j][None, :], mla_kv_norm[j][None, :]
            a = t3(_mm_nn(f"mla_a_{l}", [(t2(u), _w2(w_a))], F32))
            cq, ckv, kpe = _mla_norm_fwd(f"mla_norm_fwd_{l}", a, qn, kvn, cos_t, sin_t)
            q_raw = t3(_mm_nn(f"mla_q_{l}", [(t2(cq), _w2(w_uq))], F32))
            kv_raw = t3(_mm_nn(f"mla_kv_{l}", [(t2(ckv), _w2(w_kv))], F32))
            qh, kh, vh = _mla_prep_fwd(f"mla_prep_fwd_{l}", q_raw, kv_raw, kpe, cos_t, sin_t, n_heads)
            o, lse, fetched = _flash_fwd(f"flash_fwd_{l}", qh, kh, vh, n_heads, sm_scale,
                                         cargo=[shards[2][n] for n in members[2]] if l == host_layer else ())
            if l == host_layer:
                fetched_group(2, dict(zip(members[2], fetched)))
            y = t3(_mm_nn(f"mla_o_{l}", [(t2(o), _w2(w_o))], F32))
            st.update(u=u, w_a=w_a, w_uq=w_uq, w_kv=w_kv, w_o=w_o, qn=qn, kvn=kvn, a=a, cq=cq, ckv=ckv,
                      qh=qh, kh=kh, vh=vh, o=o, lse=lse)
        else:
            grp_l, jj = grp('sc_w_in', j)
            w_out, cw = grp_l['sc_w_out'][jj], full['sc_conv'][j]
            q = t3(_mm_nn(f"sc_in_{l}", [(t2(u), (grp_l['in_cols'], jj))], F32))
            r = _shortconv_fwd(f"shortconv_fwd_{l}", q, cw)
            y = t3(_mm_nn(f"sc_out_{l}", [(t2(r), _w2(w_out))], F32))
            st.update(u=u, w_out=w_out, cw=cw, q=q, r=r, in_rows=(grp_l['in_rows'], jj))
        lng, lnb = full['ln_g'][l], full['ln_b'][l]
        z1, xmid, u2 = _ln_mod_fwd(f"ln_mod_a_{l}", alpha, xin, y, g1, lng[0:1], lnb[0:1], sc2, sh2)
        cwf, cbf = full['ffn_conv'][l], ffn_conv_b[l][None, :]
        ffn_w, ll = grp('ffn_w_up', l)
        ride_mm = [n for n in members[1] if n != 'ffn_w_up'] if l == 0 else []
        ride_conv = [n for n in members[1] if n == 'ffn_w_up'] if l == 0 else []
        p = _mm_nn(f"ffn_up_{l}", [(t2(u2), (ffn_w['up_cols'], ll))], F32, cargo=[shards[1][n] for n in ride_mm],
                   route=_Fetch)
        got_mm = ()
        if ride_mm:
            p, got_mm = p
        p = t3(p)
        act, got_conv = _convglu_fwd(f"convglu_fwd_{l}", p, cwf, cbf, cargo=[shards[1][n] for n in ride_conv])
        if l == 0:
            fetched_group(1, {**dict(zip(ride_mm, got_mm)), **dict(zip(ride_conv, got_conv))})
        y2 = t3(_mm_nn(f"ffn_down_{l}", [(t2(act), (ffn_w['down_rows'], ll))], F32))
        st.update(y1=y, z1=z1, xmid=xmid, u2=u2, p=p, act=act, y2=y2, cwf=cwf, cbf=cbf, lng=lng, lnb=lnb,
                  ffn_w=ffn_w, ll=ll)
        if l + 1 < depth:
            nsh1, nsc1 = mods[l + 1][0], mods[l + 1][1]
            z2, xin, u = _ln_mod_fwd(f"ln_mod_b_{l}", alpha, xmid, y2, g2, lng[1:2], lnb[1:2], nsc1, nsh1)
        else:
            z2, ct, loss_acc = _ln_loss_fwd("ln_loss", alpha, xmid, y2, g2, lng[1:2], lnb[1:2], loss_target)
        st.update(z2=z2)
        saved.append(st)
    loss = lax.psum(loss_acc[0, 0], ("x", "y", "c"))

    grads = {}
    dmods = [[None] * 6 for _ in range(depth)]
    g_ln_g = [[None, None] for _ in range(depth)]
    g_ln_b = [[None, None] for _ in range(depth)]
    stack = {n: [None] * wts[n].shape[0] for n in ('pool_scale', 'mla_q_norm', 'mla_kv_norm', 'sc_conv', 'ffn_conv',
                                                    'ffn_conv_b')}
    units = {n: [None] * wts[n].shape[0] for n in big_names}
    cargo_a, cargo_b, received = [], [], {}

    def landed(items, got):
        for (n, i, _), r in zip(items, got):
            received[(n, i)] = r

    upstream = (ct,)
    for l in reversed(range(depth)):
        st = saved[l]
        sh1, sc1, g1, sh2, sc2, g2 = mods[l]
        kind, j = l % 3, l // 3
        if len(upstream) > 1:
            upstream = (upstream[0], upstream[1], st['lnb'][1:2], upstream[2])
        res = _sub_bwd(f"sub_bwd_b_{l}", alpha, upstream, st['z2'], st['y2'], g2, st['lng'][1:2])
        dz2, dy2, dmods[l][5], g_ln_g[l][1], g_ln_b[l][1] = res[:5]
        if l + 1 < depth:
            dmods[l + 1][1], dmods[l + 1][0] = res[5], res[6]
        dy2f = t2(dy2)
        ffn_w, ll = st['ffn_w'], st['ll']
        da = t3(_mm_nn(f"ffn_down_bwd_{l}", [(dy2f, (ffn_w['down_t'], ll))], F32))
        units['ffn_w_down'][l] = _mm_tn(f"ffn_down_dw_{l}", t2(st['act']), [dy2f],
                                        out_dtype=BF16).reshape(N_CHIP, f_hid // N_CHIP, d)
        dpv, dpg, dcw, dcb, got = _convglu_bwd(f"convglu_bwd_{l}", st['p'], da, st['cwf'], st['cbf'],
                                               cargo=[u for _, _, u in cargo_a])
        landed(cargo_a, got)
        stack['ffn_conv'][l], stack['ffn_conv_b'][l] = dcw, dcb[0]
        down_unit = [('ffn_w_down', l, units['ffn_w_down'][l])]
        du2, got = _mm_nn(f"ffn_up_bwd_{l}", [(t2(dpv), (ffn_w['up_rows'], 2 * ll)),
                                              (t2(dpg), (ffn_w['up_rows'], 2 * ll + 1))], F32,
                          cargo=[units['ffn_w_down'][l]])
        landed(down_unit, got)
        du2 = t3(du2)
        res = _mm_tn(f"ffn_up_dw_{l}", t2(st['u2']), [t2(dpv), t2(dpg)], N_CHIP, out_dtype=BF16,
                     cargo=[u for _, _, u in cargo_b])
        if cargo_b:
            landed(cargo_b, res[1])
            res = res[0]
        units['ffn_w_up'][l] = res
        res = _sub_bwd(f"sub_bwd_a_{l}", alpha, (dz2, du2, st['lnb'][0:1], sc2), st['z1'], st['y1'], g1, st['lng'][0:1])
        dz1, dy1, dmods[l][2], g_ln_g[l][0], g_ln_b[l][0], dmods[l][4], dmods[l][3] = res
        dy1f = t2(dy1)
        if kind == 0:
            up_unit = [('ffn_w_up', l, units['ffn_w_up'][l])] if l == 0 else []
            du1, dw, dscale, got = _pool_bwd(f"pool_bwd_{l}", st['x'], sc1, sh1, dy1, st['w'], st['w_t'], st['scale'],
                                             cargo=[u for _, _, u in up_unit])
            landed(up_unit, got)
            stack['pool_scale'][j] = dscale[0]
            grp = dw.shape[1] // N_CHIP
            units['pool_w'][j] = jnp.transpose(dw.reshape(POOL_GROUPS, N_CHIP, grp, dw.shape[2]),
                                               (1, 0, 2, 3)).reshape(N_CHIP, POOL_GROUPS * grp, dw.shape[2])
        elif kind == 1:
            do = t3(_mm_nn(f"mla_o_bwd_{l}", [(dy1f, _w2(jnp.swapaxes(st['w_o'], 0, 1)))], BF16))
            gwo, _ = _unpad_heads(_mm_tn(f"mla_o_dw_{l}", t2(st['o']), [dy1f])[0], n_heads, V_HEAD, 0)
            units['mla_w_o'][j] = jnp.moveaxis(gwo.reshape(d, n_heads * V_HEAD), -1, 0).reshape(N_CHIP, -1, d)
            fa = (st['qh'], st['kh'], st['vh'], st['o'], st['lse'], do, n_heads, sm_scale)
            dq = _flash_dq(f"flash_dq_{l}", *fa)
            dk, dv = _flash_dkv(f"flash_dkv_{l}", *fa)
            dq_raw, dkv_raw, dkpe = _mla_prep_bwd(f"mla_prep_bwd_{l}", dq, dk, dv, cos_t, sin_t, n_heads)
            dq_raw, dkv_raw = t2(dq_raw), t2(dkv_raw)
            dcq = t3(_mm_nn(f"mla_q_bwd_{l}", [(dq_raw, _w2(jnp.swapaxes(st['w_uq'], 0, 1)))], F32))
            dckv = t3(_mm_nn(f"mla_kv_bwd_{l}", [(dkv_raw, _w2(jnp.swapaxes(st['w_kv'], 0, 1)))], F32))
            gq, _ = _unpad_heads(_mm_tn(f"mla_q_dw_{l}", t2(st['cq']), [dq_raw])[0], n_heads, QK_NOPE + QK_ROPE, 1)
            units['mla_w_uq'][j] = _cols_by_chip(gq.reshape(ql, n_heads * (QK_NOPE + QK_ROPE)))
            gkv = _mm_tn(f"mla_kv_dw_{l}", t2(st['ckv']), [dkv_raw])[0]
            gk, _ = _unpad_heads(gkv[:, :n_heads * HEAD_PAD], n_heads, QK_NOPE, 1)
            gv, _ = _unpad_heads(gkv[:, n_heads * HEAD_PAD:], n_heads, V_HEAD, 1)
            units['mla_w_ukv'][j] = _cols_by_chip(
                jnp.concatenate([gk, gv], axis=-1).reshape(kvl, n_heads * (QK_NOPE + V_HEAD)))
            da_, dqn, dkvn = _mla_norm_bwd(f"mla_norm_bwd_{l}", st['a'], dcq, dckv, dkpe, st['qn'], st['kvn'])
            stack['mla_q_norm'][j], stack['mla_kv_norm'][j] = dqn[0], dkvn[0]
            du1 = t3(_mm_nn(f"mla_a_bwd_{l}", [(t2(da_), _w2(jnp.swapaxes(st['w_a'], 0, 1)))], F32))
            gwa = _mm_tn(f"mla_a_dw_{l}", t2(st['u']), [t2(da_)])[0]
            units['mla_w_a'][j] = _cols_by_chip(jnp.concatenate(
                [gwa[:, :ql + kvl], gwa[:, ql + kvl + QK_NOPE:ql + kvl + QK_NOPE + QK_ROPE]], axis=1))
        else:
            dr = t3(_mm_nn(f"sc_out_bwd_{l}", [(dy1f, _w2(jnp.swapaxes(st['w_out'], 0, 1)))], F32))
            units['sc_w_out'][j] = _mm_tn(f"sc_out_dw_{l}", t2(st['r']), [dy1f], out_dtype=BF16).reshape(N_CHIP, -1, d)
            dgb, dgc, dh, dcw = _shortconv_bwd(f"shortconv_bwd_{l}", st['q'], dr, st['cw'])
            stack['sc_conv'][j] = dcw
            parts = [t2(dgb), t2(dgc), t2(dh)]
            in_rows, jj = st['in_rows']
            du1 = t3(_mm_nn(f"sc_in_bwd_{l}", [(parts[k], (in_rows, 3 * jj + k)) for k in range(3)], F32))
            units['sc_w_in'][j] = _cols_by_chip(jnp.concatenate(
                [_mm_tn(f"sc_in_dw_{k}_{l}", t2(st['u']), [parts[k]])[0] for k in range(3)], axis=1))
        upstream = (dz1, du1, sc1)
        mixer = {0: ['pool_w'], 1: ['mla_w_a', 'mla_w_uq', 'mla_w_ukv', 'mla_w_o'], 2: ['sc_w_in', 'sc_w_out']}[kind]
        for n in mixer:
            units[n][j] = units[n][j].astype(BF16)
        cargo_a = [('ffn_w_up', l, units['ffn_w_up'][l])] if l > 0 else []
        cargo_b = [(n, j, units[n][j]) for n in mixer]
    grad_x, dmods[0][1], dmods[0][0] = _input_bwd("input_bwd", alpha, upstream[0], upstream[1], x, mods[0][1])

    for n, parts in stack.items():
        grads[n] = jnp.stack(parts)
    grads['ln_g'] = jnp.stack([jnp.concatenate(r, axis=0) for r in g_ln_g])
    grads['ln_b'] = jnp.stack([jnp.concatenate(r, axis=0) for r in g_ln_b])
    dmod_mine = jnp.stack([jnp.concatenate([t[:, 0, :] for t in dmods[l]], axis=-1) for l in range(depth)])

    small_grad_names = small_names + ['mla_q_norm', 'mla_kv_norm', 'ffn_conv_b']
    sg_pack, sg_spans = _pack_rows([dmod_mine] + [grads[n] for n in small_grad_names], F32, SUBLANE)
    rows_sg = sg_pack.shape[0]
    sg_all = _all_gather8("gather_small_grads", sg_pack, True).reshape(N_DEV, rows_sg, PACK_COLS)
    dmod_all = sg_all.reshape(N_DEV, -1)[:, :dmod_mine.size].reshape(N_DEV, depth, bsz, 6 * d)
    dmod_all = jnp.transpose(dmod_all, (1, 0, 2, 3)).reshape(depth, N_DEV * bsz, 6 * d)
    sg_sum = _sum8("sum_small_grads", sg_all).reshape(-1)
    for n, (off, shape) in zip(small_grad_names, sg_spans[1:]):
        g_full = sg_sum[off:off + _size(shape)].reshape(shape)
        if n in SMALL_SHARDED:
            ax = SMALL_SHARDED[n]
            width = shape[ax] // N_CHIP
            g_full = lax.dynamic_slice_in_dim(g_full, chip * width, width, axis=ax)
        grads[n] = g_full
    dmod_cols = lax.dynamic_slice_in_dim(dmod_all, chip * n_mod, n_mod, axis=2)
    grads['mod_w'], gb = _mod_bwd("mod_bwd", c_all, dmod_cols, dmod_all)
    grads['mod_b'] = gb[:, 0, :]

    keys = [(n, i) for n in big_names for i in range(len(units[n]))]
    last = [(n, i, units[n][i]) for n, i in keys if (n, i) not in received]
    landed(last, _scatter_grads("scatter_big_grads", [u for _, _, u in last]))
    chip_core = jnp.stack([chip, mc]).astype(jnp.int32)
    bufs = _swap_halves("swap_big_grad_halves",
                        [_sum8_into_half(f"sum_big_grads_{n}_{i}", units[n][i], received[(n, i)], chip_core)
                         for n, i in keys])
    for n in big_names:
        grads[n] = jnp.stack([b for (m, _), b in zip(keys, bufs) if m == n]).reshape(wts[n].shape)

    deltas, new_m, new_v = {}, {}, {}
    for n in WEIGHTS:
        deltas[n], new_m[n], new_v[n] = _adamw(f"adamw_{n}", wts[n], grads[n], mom1[n], mom2[n])
    return (loss, grad_x, *[grads[n] for n in WEIGHTS], *[deltas[n] for n in WEIGHTS],
            *[new_m[n] for n in WEIGHTS], *[new_v[n] for n in WEIGHTS])
```

```python
import functools

import jax
import jax.numpy as jnp
from jax import lax
from jax.experimental import pallas as pl
from jax.experimental.pallas import tpu as pltpu

F32 = jnp.float32
BF16 = jnp.bfloat16
MESH = pl.DeviceIdType.MESH

N_DEV = 8
N_CHIP = 4
LANE = 128
SUBLANE = 8
VMEM_LIMIT_BYTES = 56 * 2 ** 20
PACK_COLS = 1024

LN_EPS = 1e-5
RMS_EPS = 1e-6
QK_NOPE, QK_ROPE, V_HEAD = 64, 32, 64
ROPE_THETA = 10000.0
HEAD_PAD = 128
POOL_GROUPS = 4
POOL_HALO = 16
CONV_HALO = 8
CONV_ROWS = 1024
ADAM_LR, ADAM_B1, ADAM_B2, ADAM_EPS, ADAM_WD, ADAM_STEP = 0.001, 0.9, 0.999, 1e-08, 0.01, 10

WEIGHTS = ['mod_w', 'mod_b', 'ln_g', 'ln_b', 'pool_w', 'pool_scale', 'mla_w_a', 'mla_q_norm', 'mla_w_uq',
           'mla_kv_norm', 'mla_w_ukv', 'mla_w_o', 'sc_w_in', 'sc_conv', 'sc_w_out', 'ffn_w_up', 'ffn_conv',
           'ffn_conv_b', 'ffn_w_down']
BIG = {'pool_w': 2, 'mla_w_a': 2, 'mla_w_uq': 2, 'mla_w_ukv': 2, 'mla_w_o': 1, 'sc_w_in': 2, 'sc_w_out': 1,
       'ffn_w_up': 2, 'ffn_w_down': 1}
SMALL_SHARDED = {'ln_g': 2, 'ln_b': 2, 'pool_scale': 1, 'sc_conv': 2, 'ffn_conv': 2}
REPLICATED = ['mod_b', 'mla_q_norm', 'mla_kv_norm', 'ffn_conv_b']


def _pc(body, **kw):
    return pl.pallas_call(body, **kw)


def _cp(*sem):
    return pltpu.CompilerParams(dimension_semantics=sem, vmem_limit_bytes=VMEM_LIMIT_BYTES)


def _div(n, cap, mult):
    best = None
    for d in range(mult, min(n, cap) + 1, mult):
        if n % d == 0:
            best = d
    return best if best is not None else n


def _sds(shape, dtype):
    return jax.ShapeDtypeStruct(tuple(shape), dtype)


def _flip(v, bit):
    return 1 - v if bit else v


def _all_gather8(name, x_shard, in_vmem):
    m_per, n = x_shard.shape
    space = pltpu.VMEM if in_vmem else pltpu.HBM

    def body(x_ref, out_ref, send_sems, recv_sems, local_sem):
        x, y, c = lax.axis_index("x"), lax.axis_index("y"), lax.axis_index("c")
        me, sibling = (x, y, c), (x, y, 1 - c)
        chips = [(1 - x, y), (x, 1 - y), (1 - x, 1 - y)]

        def rows(px, py, pc_):
            return out_ref.at[pl.ds((4 * px + 2 * py + pc_) * m_per, m_per), :]

        def copy(k, block, to, src=None):
            return pltpu.make_async_remote_copy(
                src_ref=rows(*block) if src is None else src, dst_ref=rows(*block),
                send_sem=send_sems.at[k], recv_sem=recv_sems.at[k], device_id=to, device_id_type=MESH)

        mine = pltpu.make_async_copy(x_ref, rows(*me), local_sem)
        mine.start()
        first = [copy(0, me, sibling, src=x_ref)]
        first += [copy(1 + j, me, (*chip, c), src=x_ref) for j, chip in enumerate(chips)]
        for cp in first:
            cp.start()
        passed = [copy(4 + j, (*chip, c), sibling) for j, chip in enumerate(chips)]
        for j, chip in enumerate(chips):
            copy(1 + j, (*chip, c), me).wait_recv()
            passed[j].start()
        copy(0, sibling, me).wait_recv()
        for j, chip in enumerate(chips):
            copy(4 + j, (*chip, 1 - c), me).wait_recv()
        for cp in first + passed:
            cp.wait_send()
        mine.wait()

    return _pc(
        body, name=name, out_shape=_sds((N_DEV * m_per, n), x_shard.dtype),
        in_specs=[pl.BlockSpec(memory_space=space)], out_specs=pl.BlockSpec(memory_space=space),
        scratch_shapes=[pltpu.SemaphoreType.DMA((7,)), pltpu.SemaphoreType.DMA((7,)), pltpu.SemaphoreType.DMA],
        compiler_params=pltpu.CompilerParams(vmem_limit_bytes=VMEM_LIMIT_BYTES),
    )(x_shard)


def _gather_weights(name, shards):
    n_t = len(shards)
    halves = [s.shape[0] // 2 for s in shards]

    def body(*refs):
        x_refs, o_refs = refs[:n_t], refs[n_t:2 * n_t]
        send_sems, recv_sems, local_sems = refs[2 * n_t:]
        x, y, c = lax.axis_index("x"), lax.axis_index("y"), lax.axis_index("c")
        me, sibling = (x, y, c), (x, y, 1 - c)
        chips = [(1 - x, y), (x, 1 - y), (1 - x, 1 - y)]

        def slot(t, px, py, pc_):
            return o_refs[t].at[4 * px + 2 * py + pc_]

        def my_rows(t):
            return x_refs[t].at[pl.ds(c * halves[t], halves[t]), :]

        def copy(t, k, block, to, src=None):
            return pltpu.make_async_remote_copy(
                src_ref=slot(t, *block) if src is None else src, dst_ref=slot(t, *block),
                send_sem=send_sems.at[t, k], recv_sem=recv_sems.at[t, k], device_id=to, device_id_type=MESH)

        local = [pltpu.make_async_copy(my_rows(t), slot(t, *me), local_sems.at[t]) for t in range(n_t)]
        for cp in local:
            cp.start()
        first = []
        for t in range(n_t):
            first += [copy(t, 1 + j, me, (*chip, c), src=my_rows(t)) for j, chip in enumerate(chips)]
            first.append(copy(t, 0, me, sibling, src=my_rows(t)))
        for cp in first:
            cp.start()
        passed = []
        for j, chip in enumerate(chips):
            for t in range(n_t):
                copy(t, 1 + j, (*chip, c), me).wait_recv()
                passed.append(copy(t, 4 + j, (*chip, c), sibling))
                passed[-1].start()
        for t in range(n_t):
            copy(t, 0, sibling, me).wait_recv()
        for j, chip in enumerate(chips):
            for t in range(n_t):
                copy(t, 4 + j, (*chip, 1 - c), me).wait_recv()
        for cp in first + passed:
            cp.wait_send()
        for cp in local:
            cp.wait()

    hbm = pl.BlockSpec(memory_space=pltpu.HBM)
    return _pc(
        body, name=name, out_shape=tuple(_sds((N_DEV, h, s.shape[1]), s.dtype) for h, s in zip(halves, shards)),
        in_specs=[hbm] * n_t, out_specs=(hbm,) * n_t,
        scratch_shapes=[pltpu.SemaphoreType.DMA((n_t, 7)), pltpu.SemaphoreType.DMA((n_t, 7)),
                        pltpu.SemaphoreType.DMA((n_t,))],
    )(*shards)


def _scatter_copies(u_refs, r_refs, send_sems, recv_sems):
    x, y, c = lax.axis_index("x"), lax.axis_index("y"), lax.axis_index("c")
    copies = []
    for k in range(1, N_DEV):
        px, py, pcc = _flip(x, (k >> 2) & 1), _flip(y, (k >> 1) & 1), _flip(c, k & 1)
        for t, (u_ref, r_ref) in enumerate(zip(u_refs, r_refs)):
            h = u_ref.shape[1] // 2
            copies.append(pltpu.make_async_remote_copy(
                src_ref=u_ref.at[2 * px + py, pl.ds(pcc * h, h), :], dst_ref=r_ref.at[k - 1],
                send_sem=send_sems.at[t, k - 1], recv_sem=recv_sems.at[t, k - 1],
                device_id=(px, py, pcc), device_id_type=MESH))
    return copies


def _scatter_shapes(units):
    return tuple(_sds((N_DEV - 1, u.shape[1] // 2, u.shape[2]), u.dtype) for u in units)


def _scatter_grads(name, units):
    n_u = len(units)

    def body(*refs):
        copies = _scatter_copies(refs[:n_u], refs[n_u:2 * n_u], refs[2 * n_u], refs[2 * n_u + 1])
        for cp in copies:
            cp.start()
        for cp in copies:
            cp.wait()

    hbm = pl.BlockSpec(memory_space=pltpu.HBM)
    return _pc(body, name=name, out_shape=_scatter_shapes(units), in_specs=[hbm] * n_u, out_specs=(hbm,) * n_u,
               scratch_shapes=[pltpu.SemaphoreType.DMA((n_u, 7)), pltpu.SemaphoreType.DMA((n_u, 7))])(*units)


class _Scatter:
    peers = N_DEV - 1
    shapes = staticmethod(_scatter_shapes)

    @staticmethod
    def copies(u_refs, r_refs, send_sems, recv_sems):
        both = _scatter_copies(u_refs, r_refs, send_sems, recv_sems)
        return both, both


class _Fetch:
    peers = N_CHIP - 1

    @staticmethod
    def shapes(units):
        return tuple(_sds((N_CHIP,) + u.shape, u.dtype) for u in units)

    @staticmethod
    def copies(u_refs, r_refs, send_sems, recv_sems):
        x, y, c = lax.axis_index("x"), lax.axis_index("y"), lax.axis_index("c")
        sends, recvs = [], []
        for k in range(1, N_CHIP):
            px, py = _flip(x, (k >> 1) & 1), _flip(y, k & 1)
            for t, (u_ref, r_ref) in enumerate(zip(u_refs, r_refs)):
                sends.append(pltpu.make_async_remote_copy(
                    src_ref=u_ref, dst_ref=r_ref.at[2 * x + y], send_sem=send_sems.at[t, k - 1],
                    recv_sem=recv_sems.at[t, k - 1], device_id=(px, py, c), device_id_type=MESH))
                recvs.append(pltpu.make_async_remote_copy(
                    src_ref=u_ref, dst_ref=r_ref.at[2 * px + py], send_sem=send_sems.at[t, k - 1],
                    recv_sem=recv_sems.at[t, k - 1], device_id=(px, py, c), device_id_type=MESH))
        return sends, recvs


def _pc_cargo(body, cargo, *, name, grid, in_specs, out_specs, out_shape, scratch_shapes=(), route=_Scatter):
    out_specs, out_shape = tuple(out_specs), tuple(out_shape)
    if not cargo:
        return lambda *args: (_pc(body, name=name, grid=grid, in_specs=list(in_specs), out_specs=out_specs,
                                  out_shape=out_shape, scratch_shapes=list(scratch_shapes),
                                  compiler_params=_cp(*["arbitrary"] * len(grid)))(*args), ())
    n_in, n_out, n_u, n_s = len(in_specs), len(out_specs), len(cargo), len(scratch_shapes)

    def wrapped(*refs):
        ins, u_refs = refs[:n_in], refs[n_in:n_in + n_u]
        outs = refs[n_in + n_u:n_in + n_u + n_out]
        r_refs = refs[n_in + n_u + n_out:n_in + 2 * n_u + n_out]
        scratch = refs[n_in + 2 * n_u + n_out:n_in + 2 * n_u + n_out + n_s]
        send_sems, recv_sems = refs[-2:]
        first = last = None
        for axis, extent in enumerate(grid):
            at_start, at_end = pl.program_id(axis) == 0, pl.program_id(axis) == extent - 1
            first = at_start if first is None else first & at_start
            last = at_end if last is None else last & at_end

        @pl.when(first)
        def _():
            sends, _ = route.copies(u_refs, r_refs, send_sems, recv_sems)
            for cp in sends:
                cp.start()

        body(*ins, *outs, *scratch)

        @pl.when(last)
        def _():
            sends, recvs = route.copies(u_refs, r_refs, send_sems, recv_sems)
            for cp in recvs:
                cp.wait_recv()
            for cp in sends:
                cp.wait_send()

    hbm = pl.BlockSpec(memory_space=pltpu.HBM)
    sems = pltpu.SemaphoreType.DMA((n_u, route.peers))
    call = _pc(wrapped, name=name, grid=grid, in_specs=list(in_specs) + [hbm] * n_u, out_specs=out_specs + (hbm,) * n_u,
               out_shape=out_shape + route.shapes(cargo), scratch_shapes=list(scratch_shapes) + [sems, sems],
               compiler_params=_cp(*["arbitrary"] * len(grid)))

    def run(*args):
        res = call(*args, *cargo)
        return tuple(res[:n_out]), tuple(res[n_out:])
    return run


def _swap_halves(name, bufs):
    n_u = len(bufs)

    def body(*refs):
        o_refs = refs[n_u:2 * n_u]
        send_sems, recv_sems = refs[2 * n_u:]
        x, y, c = lax.axis_index("x"), lax.axis_index("y"), lax.axis_index("c")

        def rows(u, core):
            h = bufs[u].shape[0] // 2
            return o_refs[u].at[pl.ds(core * h, h), :]

        sends = [pltpu.make_async_remote_copy(src_ref=rows(u, c), dst_ref=rows(u, c), send_sem=send_sems.at[u],
                                              recv_sem=recv_sems.at[u], device_id=(x, y, 1 - c), device_id_type=MESH)
                 for u in range(n_u)]
        recvs = [pltpu.make_async_remote_copy(src_ref=rows(u, c), dst_ref=rows(u, 1 - c), send_sem=send_sems.at[u],
                                              recv_sem=recv_sems.at[u], device_id=(x, y, 1 - c), device_id_type=MESH)
                 for u in range(n_u)]
        for cp in sends:
            cp.start()
        for cp in recvs:
            cp.wait_recv()
        for cp in sends:
            cp.wait_send()

    hbm = pl.BlockSpec(memory_space=pltpu.HBM)
    return _pc(
        body, name=name, out_shape=tuple(_sds(b.shape, b.dtype) for b in bufs), in_specs=[hbm] * n_u,
        out_specs=(hbm,) * n_u, input_output_aliases={u: u for u in range(n_u)},
        scratch_shapes=[pltpu.SemaphoreType.DMA((n_u,)), pltpu.SemaphoreType.DMA((n_u,))],
    )(*bufs)


def _sum8_into_half(name, unit, received, chip_core):
    _, h, n = received.shape
    tm = _div(h, 256, 16)
    per = h // tm

    def body(cc_ref, u_ref, p_ref, o_ref):
        acc = u_ref[0].astype(F32)
        for s in range(N_DEV - 1):
            acc = acc + p_ref[s].astype(F32)
        o_ref[...] = acc

    grid_spec = pltpu.PrefetchScalarGridSpec(
        num_scalar_prefetch=1, grid=(per,),
        in_specs=[pl.BlockSpec((1, tm, n), lambda i, cc_ref: (cc_ref[0], cc_ref[1] * per + i, 0)),
                  pl.BlockSpec((N_DEV - 1, tm, n), lambda i, cc_ref: (0, i, 0))],
        out_specs=pl.BlockSpec((tm, n), lambda i, cc_ref: (cc_ref[1] * per + i, 0)))
    return _pc(body, name=name, grid_spec=grid_spec, out_shape=_sds((2 * h, n), F32),
               compiler_params=_cp("arbitrary"))(chip_core, unit, received)


def _sum8(name, parts):
    _, m, n = parts.shape
    tm = _div(m, 256, SUBLANE)

    def body(p_ref, o_ref):
        acc = p_ref[0]
        for s in range(1, N_DEV):
            acc = acc + p_ref[s]
        o_ref[...] = acc

    return _pc(body, name=name, grid=(m // tm,), out_shape=_sds((m, n), F32),
               in_specs=[pl.BlockSpec((N_DEV, tm, n), lambda i: (0, i, 0))],
               out_specs=pl.BlockSpec((tm, n), lambda i: (i, 0)), compiler_params=_cp("parallel"))(parts)


def _pack_rows(arrays, dtype, row_mult):
    flat, spans, off = [], [], 0
    for a in arrays:
        flat.append(a.reshape(-1).astype(dtype))
        spans.append((off, a.shape))
        off += a.size
    quantum = row_mult * PACK_COLS
    total = -(-off // quantum) * quantum
    if total > off:
        flat.append(jnp.zeros((total - off,), dtype))
    return jnp.concatenate(flat).reshape(total // PACK_COLS, PACK_COLS), spans


def _size(shape):
    n = 1
    for s in shape:
        n *= s
    return n


def _join_chips(blocks, axis):
    return jnp.concatenate([blocks[j] for j in range(N_CHIP)], axis=axis)


def _cols_by_chip(g):
    k, n = g.shape
    return jnp.transpose(g.reshape(k, N_CHIP, n // N_CHIP), (1, 0, 2))


def _mm_nn(name, pairs, out_dtype, cargo=(), route=_Scatter, tm_cap=1024, tn_cap=1536):
    m = pairs[0][0].shape[0]
    nb, _, n4 = pairs[0][1][0].shape
    tm, tn = _div(m, tm_cap, 16), _div(n4, tn_cap, LANE)
    per = n4 // tn
    n_pairs = len(pairs)

    def body(*refs):
        o_ref = refs[-1]
        acc = jnp.dot(refs[0][...], refs[1][0], preferred_element_type=F32)
        for i in range(1, n_pairs):
            acc = acc + jnp.dot(refs[2 * i][...], refs[2 * i + 1][0], preferred_element_type=F32)
        o_ref[...] = acc.astype(o_ref.dtype)

    in_specs, args = [], []
    for a, (w, r) in pairs:
        k = a.shape[1]
        assert w.shape[0] == nb and w.shape[2] == n4 and w.shape[1] % k == 0
        in_specs += [pl.BlockSpec((tm, k), lambda j, i: (i, 0)),
                     pl.BlockSpec((1, k, tn), functools.partial(lambda j, i, r_: (j // per, r_, j % per), r_=r))]
        args += [a, w]
    if cargo:
        (out,), received = _pc_cargo(body, cargo, name=name, grid=(nb * per, m // tm), in_specs=in_specs, route=route,
                                     out_shape=[_sds((m, nb * n4), out_dtype)],
                                     out_specs=[pl.BlockSpec((tm, tn), lambda j, i: (i, j))])(*args)
        return out, received
    return _pc(body, name=name, grid=(nb * per, m // tm), out_shape=_sds((m, nb * n4), out_dtype), in_specs=in_specs,
               out_specs=pl.BlockSpec((tm, tn), lambda j, i: (i, j)), compiler_params=_cp("parallel", "parallel"))(*args)


def _mm_tn(name, x, ys, n_blocks=1, out_dtype=F32, cargo=(), tt_cap=512):
    t, k = x.shape
    widths = [y.shape[1] for y in ys]
    n4 = sum(widths) // n_blocks
    common = n4
    for w in widths:
        common = _gcd(common, w)
    tk, tn, tt = _div(k, 1536, LANE), _div(common, 1536, LANE), _div(t, tt_cap, 16)
    per = n4 // tn
    starts, acc_w = [], 0
    for w in widths:
        starts.append(acc_w // tn)
        acc_w += w
    counts = [w // tn for w in widths]
    n_y = len(ys)

    def active(i, j):
        return (j >= starts[i]) & (j < starts[i] + counts[i])

    n_t = t // tt

    def body(*refs):
        x_ref, y_refs, o_ref, acc_ref = refs[0], refs[1:1 + n_y], refs[-2], refs[-1]
        j = pl.program_id(1)

        @pl.when(pl.program_id(2) == 0)
        def _():
            acc_ref[...] = jnp.zeros_like(acc_ref)

        for i in range(n_y):
            @pl.when(active(i, j))
            def _():
                acc_ref[...] += lax.dot_general(x_ref[...], y_refs[i][...], (((0,), (0,)), ((), ())),
                                                preferred_element_type=F32)

        @pl.when(pl.program_id(2) == n_t - 1)
        def _():
            o_ref[0] = acc_ref[...].astype(o_ref.dtype)

    def y_spec(i):
        def index(a, j, s):
            on = active(i, j)
            return jnp.where(on, s, 0), jnp.where(on, j - starts[i], 0)
        return pl.BlockSpec((tt, tn), index)

    (out,), received = _pc_cargo(
        body, cargo, name=name, grid=(k // tk, n_blocks * per, n_t), out_shape=[_sds((n_blocks, k, n4), out_dtype)],
        in_specs=[pl.BlockSpec((tt, tk), lambda a, j, s: (s, a))] + [y_spec(i) for i in range(n_y)],
        out_specs=[pl.BlockSpec((1, tk, tn), lambda a, j, s: (j // per, a, j % per))],
        scratch_shapes=[pltpu.VMEM((tk, tn), F32)])(x, *ys)
    return (out, received) if cargo else out


def _gcd(a, b):
    while b:
        a, b = b, a % b
    return a


def _w2(w):
    return (w[None], 0)


def _tok_spec(ts, d):
    return pl.BlockSpec((1, ts, d), lambda b, i: (b, i, 0))


def _seq_spec(d):
    return pl.BlockSpec((1, 1, d), lambda b, i: (b, 0, 0))


def _vec_spec(d):
    return pl.BlockSpec((1, d), lambda b, i: (0, 0))


def _ln_stats(z):
    mu = jnp.mean(z, axis=-1, keepdims=True)
    zc = z - mu
    var = jnp.mean(zc * zc, axis=-1, keepdims=True)
    rstd = lax.rsqrt(var + LN_EPS)
    return zc * rstd, rstd


def _modulate(name, x, sc, sh):
    b, s, d = x.shape
    ts = _div(s, 512, 16)

    def body(x_ref, sc_ref, sh_ref, u_ref):
        u_ref[0] = (x_ref[0] * (1.0 + sc_ref[0]) + sh_ref[0]).astype(BF16)

    return _pc(body, name=name, grid=(b, s // ts), out_shape=_sds(x.shape, BF16),
               in_specs=[_tok_spec(ts, d), _seq_spec(d), _seq_spec(d)], out_specs=_tok_spec(ts, d),
               compiler_params=_cp("parallel", "parallel"))(x, sc, sh)


def _ln_mod_fwd(name, alpha, x, y, g, lng, lnb, sc, sh):
    b, s, d = x.shape
    ts = _div(s, 512, 16)

    def body(x_ref, y_ref, g_ref, lng_ref, lnb_ref, sc_ref, sh_ref, z_ref, xn_ref, u_ref):
        z = alpha * x_ref[0] + (1.0 + g_ref[0]) * y_ref[0]
        xhat, _ = _ln_stats(z)
        xn = xhat * lng_ref[...] + lnb_ref[...]
        z_ref[0] = z
        xn_ref[0] = xn
        u_ref[0] = (xn * (1.0 + sc_ref[0]) + sh_ref[0]).astype(BF16)

    tok, seq, vec = _tok_spec(ts, d), _seq_spec(d), _vec_spec(d)
    return _pc(body, name=name, grid=(b, s // ts),
               out_shape=(_sds(x.shape, F32), _sds(x.shape, F32), _sds(x.shape, BF16)),
               in_specs=[tok, tok, seq, vec, vec, seq, seq], out_specs=(tok, tok, tok),
               compiler_params=_cp("parallel", "parallel"))(x, y, g, lng, lnb, sc, sh)


def _ln_loss_fwd(name, alpha, x, y, g, lng, lnb, target):
    b, s, d = x.shape
    ts = _div(s, 512, 16)

    def body(x_ref, y_ref, g_ref, lng_ref, lnb_ref, t_ref, z_ref, ct_ref, loss_ref):
        @pl.when((pl.program_id(0) == 0) & (pl.program_id(1) == 0))
        def _():
            loss_ref[...] = jnp.zeros_like(loss_ref)
        z = alpha * x_ref[0] + (1.0 + g_ref[0]) * y_ref[0]
        xhat, _ = _ln_stats(z)
        err = xhat * lng_ref[...] + lnb_ref[...] - t_ref[0]
        z_ref[0] = z
        ct_ref[0] = err / d
        part = 0.5 * jnp.sum(jnp.mean(err * err, axis=-1, keepdims=True))
        loss_ref[...] += jnp.full(loss_ref.shape, part, F32)

    tok, seq, vec = _tok_spec(ts, d), _seq_spec(d), _vec_spec(d)
    return _pc(body, name=name, grid=(b, s // ts),
               out_shape=(_sds(x.shape, F32), _sds(x.shape, F32), _sds((SUBLANE, LANE), F32)),
               in_specs=[tok, tok, seq, vec, vec, tok],
               out_specs=(tok, tok, pl.BlockSpec((SUBLANE, LANE), lambda b, i: (0, 0))),
               compiler_params=_cp("arbitrary", "arbitrary"))(x, y, g, lng, lnb, target)


def _sub_bwd(name, alpha, upstream, z, y, g, lng):
    b, s, d = z.shape
    ts = _div(s, 512, 16)
    last = len(upstream) == 1

    def body(*refs):
        if last:
            ct_ref, z_ref, y_ref, g_ref, lng_ref, dz_ref, dy_ref, dg_ref, dlng_ref, dlnb_ref = refs
        else:
            (dzn_ref, dun_ref, lnb_ref, scn_ref, z_ref, y_ref, g_ref, lng_ref,
             dz_ref, dy_ref, dg_ref, dlng_ref, dlnb_ref, dsc_ref, dsh_ref) = refs
        first_tile = pl.program_id(1) == 0

        @pl.when(first_tile & (pl.program_id(0) == 0))
        def _():
            dlng_ref[...] = jnp.zeros_like(dlng_ref)
            dlnb_ref[...] = jnp.zeros_like(dlnb_ref)

        @pl.when(first_tile)
        def _():
            dg_ref[...] = jnp.zeros_like(dg_ref)
            if not last:
                dsc_ref[...] = jnp.zeros_like(dsc_ref)
                dsh_ref[...] = jnp.zeros_like(dsh_ref)

        xhat, rstd = _ln_stats(z_ref[0])
        if last:
            ct = ct_ref[0]
        else:
            dun = dun_ref[0]
            ct = alpha * dzn_ref[0] + dun * (1.0 + scn_ref[0])
            xn = xhat * lng_ref[...] + lnb_ref[...]
            dsc_ref[0] += jnp.sum(dun * xn, axis=0, keepdims=True)
            dsh_ref[0] += jnp.sum(dun, axis=0, keepdims=True)
        dlng_ref[...] += jnp.sum(ct * xhat, axis=0, keepdims=True)
        dlnb_ref[...] += jnp.sum(ct, axis=0, keepdims=True)
        dxhat = ct * lng_ref[...]
        dz = rstd * (dxhat - jnp.mean(dxhat, axis=-1, keepdims=True)
                     - xhat * jnp.mean(dxhat * xhat, axis=-1, keepdims=True))
        dz_ref[0] = dz
        dy_ref[0] = ((1.0 + g_ref[0]) * dz).astype(BF16)
        dg_ref[0] += jnp.sum(dz * y_ref[0], axis=0, keepdims=True)

    tok, seq, vec = _tok_spec(ts, d), _seq_spec(d), _vec_spec(d)
    seq_out = _sds((b, 1, d), F32)
    out_shape = [_sds(z.shape, F32), _sds(z.shape, BF16), seq_out, _sds((1, d), F32), _sds((1, d), F32)]
    out_specs = [tok, tok, seq, vec, vec]
    if last:
        in_specs = [tok, tok, tok, seq, vec]
    else:
        in_specs = [tok, tok, vec, seq, tok, tok, seq, vec]
        out_shape += [seq_out, seq_out]
        out_specs += [seq, seq]
    return _pc(body, name=name, grid=(b, s // ts), out_shape=tuple(out_shape), in_specs=in_specs,
               out_specs=tuple(out_specs), compiler_params=_cp("arbitrary", "arbitrary"))(*upstream, z, y, g, lng)


def _input_bwd(name, alpha, dz, du, x, sc):
    b, s, d = x.shape
    ts = _div(s, 512, 16)

    def body(dz_ref, du_ref, x_ref, sc_ref, gx_ref, dsc_ref, dsh_ref):
        @pl.when(pl.program_id(1) == 0)
        def _():
            dsc_ref[...] = jnp.zeros_like(dsc_ref)
            dsh_ref[...] = jnp.zeros_like(dsh_ref)
        du_ = du_ref[0]
        gx_ref[0] = alpha * dz_ref[0] + du_ * (1.0 + sc_ref[0])
        dsc_ref[0] += jnp.sum(du_ * x_ref[0], axis=0, keepdims=True)
        dsh_ref[0] += jnp.sum(du_, axis=0, keepdims=True)

    tok, seq = _tok_spec(ts, d), _seq_spec(d)
    seq_out = _sds((b, 1, d), F32)
    return _pc(body, name=name, grid=(b, s // ts), out_shape=(_sds(x.shape, F32), seq_out, seq_out),
               in_specs=[tok, tok, tok, seq], out_specs=(tok, seq, seq),
               compiler_params=_cp("parallel", "arbitrary"))(dz, du, x, sc)


def _rows_iota(shape):
    return lax.broadcasted_iota(jnp.int32, shape, 0)


def _back(v, k):
    return pltpu.roll(v, k, axis=0)


def _ahead(v, k):
    return pltpu.roll(v, v.shape[0] - k, axis=0)


def _conv3(ext, w_ref):
    return w_ref[2:3, :] * ext + w_ref[1:2, :] * _back(ext, 1) + w_ref[0:1, :] * _back(ext, 2)


def _conv3_t(dh_ext, w_ref):
    return w_ref[2:3, :] * dh_ext + w_ref[1:2, :] * _ahead(dh_ext, 1) + w_ref[0:1, :] * _ahead(dh_ext, 2)


def _flag(cond):
    return jnp.where(cond, 1.0, 0.0).astype(F32)


def _sigmoid(v):
    return 1.0 / (1.0 + jnp.exp(-v))


def _halo_specs(ts, tc, halo, n_s, col):
    per = ts // halo
    tile = pl.BlockSpec((1, ts, tc), lambda b, i, j: (b, i, col(j)))
    prev = pl.BlockSpec((1, halo, tc), lambda b, i, j: (b, jnp.maximum(i * per - 1, 0), col(j)))
    nxt = pl.BlockSpec((1, halo, tc), lambda b, i, j: (b, jnp.minimum((i + 1) * per, n_s * per - 1), col(j)))
    return tile, prev, nxt


def _convglu_fwd(name, p, cw, cb, cargo=()):
    b, s, f2 = p.shape
    f = f2 // 2
    ts, tc = _div(s, CONV_ROWS, CONV_HALO), _div(f, 256, LANE)
    n_s, n_c = s // ts, f // tc

    def body(pv_ref, pvh_ref, pg_ref, pgh_ref, wv_ref, wg_ref, bv_ref, bg_ref, a_ref):
        keep = _flag(pl.program_id(1) > 0)

        def conv(t_ref, h_ref, w_ref, b_ref):
            ext = jnp.concatenate([h_ref[0] * keep, t_ref[0]], axis=0)
            return _conv3(ext, w_ref)[CONV_HALO:] + b_ref[...]

        val = conv(pv_ref, pvh_ref, wv_ref, bv_ref)
        gate = conv(pg_ref, pgh_ref, wg_ref, bg_ref)
        a_ref[0] = (gate * _sigmoid(gate) * val).astype(BF16)

    tv, hv, _ = _halo_specs(ts, tc, CONV_HALO, n_s, lambda j: j)
    tg, hg, _ = _halo_specs(ts, tc, CONV_HALO, n_s, lambda j: j + n_c)
    wv = pl.BlockSpec((3, tc), lambda b, i, j: (0, j))
    wg = pl.BlockSpec((3, tc), lambda b, i, j: (0, j + n_c))
    bv = pl.BlockSpec((1, tc), lambda b, i, j: (0, j))
    bg = pl.BlockSpec((1, tc), lambda b, i, j: (0, j + n_c))
    (act,), fetched = _pc_cargo(
        body, cargo, name=name, grid=(b, n_s, n_c), route=_Fetch, out_shape=[_sds((b, s, f), BF16)],
        in_specs=[tv, hv, tg, hg, wv, wg, bv, bg],
        out_specs=[pl.BlockSpec((1, ts, tc), lambda b, i, j: (b, i, j))])(p, p, p, p, cw, cw, cb, cb)
    return act, fetched


def _convglu_bwd(name, p, da, cw, cb, cargo=()):
    b, s, f2 = p.shape
    f = f2 // 2
    ts, tc = _div(s, CONV_ROWS, CONV_HALO), _div(f, 256, LANE)
    n_s, n_c = s // ts, f // tc

    def body(pv_ref, pvp_ref, pvn_ref, pg_ref, pgp_ref, pgn_ref, da_ref, dan_ref, wv_ref, wg_ref, bv_ref, bg_ref,
             dpv_ref, dpg_ref, dwv_ref, dwg_ref, dbv_ref, dbg_ref):
        bi, i = pl.program_id(1), pl.program_id(2)

        @pl.when((bi == 0) & (i == 0))
        def _():
            for r in (dwv_ref, dwg_ref, dbv_ref, dbg_ref):
                r[...] = jnp.zeros_like(r)

        keep_prev = _flag(i > 0)
        keep_next = _flag(i < n_s - 1)
        pv_ext = jnp.concatenate([pvp_ref[0] * keep_prev, pv_ref[0], pvn_ref[0]], axis=0)
        pg_ext = jnp.concatenate([pgp_ref[0] * keep_prev, pg_ref[0], pgn_ref[0]], axis=0)
        taps_v = (_back(pv_ext, 2), _back(pv_ext, 1), pv_ext)
        taps_g = (_back(pg_ext, 2), _back(pg_ext, 1), pg_ext)

        def conv(taps, w_ref, b_ref):
            return (w_ref[2:3, :] * taps[2] + w_ref[1:2, :] * taps[1] + w_ref[0:1, :] * taps[0])[CONV_HALO:] + b_ref[...]

        val, gate = conv(taps_v, wv_ref, bv_ref), conv(taps_g, wg_ref, bg_ref)
        da_ext = jnp.concatenate([da_ref[0], dan_ref[0] * keep_next], axis=0)
        sg = _sigmoid(gate)
        dval = da_ext * gate * sg
        dgate = da_ext * val * (sg * (1.0 + gate * (1.0 - sg)))
        dpv_ref[0] = _conv3_t(dval, wv_ref)[:ts].astype(BF16)
        dpg_ref[0] = _conv3_t(dgate, wg_ref)[:ts].astype(BF16)
        for dh, taps, dw_ref, db_ref in ((dval[:ts], taps_v, dwv_ref, dbv_ref), (dgate[:ts], taps_g, dwg_ref, dbg_ref)):
            db_ref[...] += jnp.sum(dh, axis=0, keepdims=True)
            for k in range(3):
                dw_ref[k:k + 1, :] += jnp.sum(dh * taps[k][CONV_HALO:CONV_HALO + ts], axis=0, keepdims=True)

    def specs(col):
        per = ts // CONV_HALO
        tile = pl.BlockSpec((1, ts, tc), lambda j, b, i: (b, i, col(j)))
        prev = pl.BlockSpec((1, CONV_HALO, tc), lambda j, b, i: (b, jnp.maximum(i * per - 1, 0), col(j)))
        nxt = pl.BlockSpec((1, CONV_HALO, tc), lambda j, b, i: (b, jnp.minimum((i + 1) * per, n_s * per - 1), col(j)))
        return tile, prev, nxt

    tv, pvp, pvn = specs(lambda j: j)
    tg, pgp, pgn = specs(lambda j: j + n_c)
    wv = pl.BlockSpec((3, tc), lambda j, b, i: (0, j))
    wg = pl.BlockSpec((3, tc), lambda j, b, i: (0, j + n_c))
    bv = pl.BlockSpec((1, tc), lambda j, b, i: (0, j))
    bg = pl.BlockSpec((1, tc), lambda j, b, i: (0, j + n_c))
    out_tile = pl.BlockSpec((1, ts, tc), lambda j, b, i: (b, i, j))
    acc3, acc1 = pl.BlockSpec((3, tc), lambda j, b, i: (0, j)), pl.BlockSpec((1, tc), lambda j, b, i: (0, j))
    (dpv, dpg, dwv, dwg, dbv, dbg), received = _pc_cargo(
        body, cargo, name=name, grid=(n_c, b, n_s),
        out_shape=(_sds((b, s, f), BF16), _sds((b, s, f), BF16), _sds((3, f), F32), _sds((3, f), F32),
                   _sds((1, f), F32), _sds((1, f), F32)),
        in_specs=[tv, pvp, pvn, tg, pgp, pgn, tv, pvn, wv, wg, bv, bg],
        out_specs=(out_tile, out_tile, acc3, acc3, acc1, acc1))(p, p, p, p, p, p, da, da, cw, cw, cb, cb)
    return dpv, dpg, jnp.concatenate([dwv, dwg], axis=1), jnp.concatenate([dbv, dbg], axis=1), received


def _shortconv_fwd(name, q, cw):
    b, s, d3 = q.shape
    d = d3 // 3
    ts, tc = _div(s, CONV_ROWS, CONV_HALO), _div(d, 256, LANE)
    n_s, n_c = s // ts, d // tc

    def body(gb_ref, gc_ref, gch_ref, h_ref, hh_ref, w_ref, r_ref):
        keep = _flag(pl.program_id(1) > 0)
        m_ext = jnp.concatenate([gch_ref[0] * hh_ref[0] * keep, gc_ref[0] * h_ref[0]], axis=0)
        r_ref[0] = (gb_ref[0] * _conv3(m_ext, w_ref)[CONV_HALO:]).astype(BF16)

    tb, _, _ = _halo_specs(ts, tc, CONV_HALO, n_s, lambda j: j)
    tcc, hc, _ = _halo_specs(ts, tc, CONV_HALO, n_s, lambda j: j + n_c)
    th, hh, _ = _halo_specs(ts, tc, CONV_HALO, n_s, lambda j: j + 2 * n_c)
    w = pl.BlockSpec((3, tc), lambda b, i, j: (0, j))
    return _pc(body, name=name, grid=(b, n_s, n_c), out_shape=_sds((b, s, d), BF16),
               in_specs=[tb, tcc, hc, th, hh, w], out_specs=pl.BlockSpec((1, ts, tc), lambda b, i, j: (b, i, j)),
               compiler_params=_cp("parallel", "parallel", "parallel"))(q, q, q, q, q, cw)


def _shortconv_bwd(name, q, dr, cw):
    b, s, d3 = q.shape
    d = d3 // 3
    ts, tc = _div(s, CONV_ROWS, CONV_HALO), _div(d, 256, LANE)
    n_s, n_c = s // ts, d // tc

    def body(gb_ref, gbn_ref, gc_ref, gcp_ref, h_ref, hp_ref, dr_ref, drn_ref, w_ref,
             dgb_ref, dgc_ref, dh_ref, dw_ref):
        bi, i = pl.program_id(1), pl.program_id(2)

        @pl.when((bi == 0) & (i == 0))
        def _():
            dw_ref[...] = jnp.zeros_like(dw_ref)

        keep_prev = _flag(i > 0)
        keep_next = _flag(i < n_s - 1)
        gc, h = gc_ref[0], h_ref[0]
        m_ext = jnp.concatenate([gcp_ref[0] * hp_ref[0] * keep_prev, gc * h], axis=0)
        cm = _conv3(m_ext, w_ref)[CONV_HALO:]
        dr_ = dr_ref[0]
        dgb_ref[0] = (dr_ * cm).astype(BF16)
        dcv_ext = jnp.concatenate([dr_ * gb_ref[0], drn_ref[0] * gbn_ref[0] * keep_next], axis=0)
        dm = _conv3_t(dcv_ext, w_ref)[:ts]
        dgc_ref[0] = (dm * h).astype(BF16)
        dh_ref[0] = (dm * gc).astype(BF16)
        dcv = dcv_ext[:ts]
        for k in range(3):
            shifted = m_ext if k == 2 else _back(m_ext, 2 - k)
            dw_ref[k:k + 1, :] += jnp.sum(dcv * shifted[CONV_HALO:], axis=0, keepdims=True)

    def specs(col):
        per = ts // CONV_HALO
        tile = pl.BlockSpec((1, ts, tc), lambda j, b, i: (b, i, col(j)))
        prev = pl.BlockSpec((1, CONV_HALO, tc), lambda j, b, i: (b, jnp.maximum(i * per - 1, 0), col(j)))
        nxt = pl.BlockSpec((1, CONV_HALO, tc), lambda j, b, i: (b, jnp.minimum((i + 1) * per, n_s * per - 1), col(j)))
        return tile, prev, nxt

    tb, _, nb = specs(lambda j: j)
    tcc, pc_, _ = specs(lambda j: j + n_c)
    th, ph, _ = specs(lambda j: j + 2 * n_c)
    w = pl.BlockSpec((3, tc), lambda j, b, i: (0, j))
    out_tile = pl.BlockSpec((1, ts, tc), lambda j, b, i: (b, i, j))
    o = _sds((b, s, d), BF16)
    return _pc(body, name=name, grid=(n_c, b, n_s), out_shape=(o, o, o, _sds((3, d), F32)),
               in_specs=[tb, nb, tcc, pc_, th, ph, tb, nb, w], out_specs=(out_tile, out_tile, out_tile, w),
               compiler_params=_cp("parallel", "arbitrary", "arbitrary"))(q, q, q, q, q, q, dr, dr, cw)


def _pick_window(group, cands):
    gid = jnp.full(cands[0].shape, group, jnp.int32)
    out = cands[-1]
    for k in range(len(cands) - 2, -1, -1):
        out = jnp.where(gid == k, cands[k], out)
    return out


def _window_sums(v, shift):
    s1 = v + shift(v, 1)
    s2 = s1 + shift(s1, 2)
    s3 = s2 + shift(s2, 4)
    s4 = s3 + shift(s3, 8)
    return [s1, s2, s3, s4]


def _pool_counts(group, first_row, n_rows, cols):
    t = _rows_iota((n_rows, cols)) + first_row
    window = _pick_window(group, [jnp.full((n_rows, cols), 2 << k, jnp.int32) for k in range(POOL_GROUPS)])
    return jnp.minimum(t + 1, window).astype(F32)


def _pool_fwd(name, x, sc, sh, w, scale):
    b, s, d = x.shape
    tc = d // POOL_GROUPS
    ts = _div(s, 512, POOL_HALO)
    n_s = s // ts

    def body(x_ref, xp_ref, sc_ref, sh_ref, w_ref, scale_ref, y_ref):
        i, grp = pl.program_id(1), pl.program_id(2)
        keep = _flag(i > 0)
        mod = 1.0 + sc_ref[0]
        u = x_ref[0] * mod + sh_ref[0]
        u_ext = jnp.concatenate([(xp_ref[0] * mod + sh_ref[0]) * keep, u], axis=0)
        summed = _pick_window(grp, _window_sums(u_ext, _back))[POOL_HALO:]
        pooled = summed / _pool_counts(grp, i * ts, ts, tc) - u
        y_ref[0] = jnp.dot(pooled.astype(BF16), w_ref[0], preferred_element_type=F32) * scale_ref[...]

    tile, prev, _ = _halo_specs(ts, tc, POOL_HALO, n_s, lambda j: j)
    seq = pl.BlockSpec((1, 1, tc), lambda b, i, j: (b, 0, j))
    return _pc(body, name=name, grid=(b, n_s, POOL_GROUPS), out_shape=_sds(x.shape, F32),
               in_specs=[tile, prev, seq, seq, pl.BlockSpec((1, tc, tc), lambda b, i, j: (j, 0, 0)),
                         pl.BlockSpec((1, tc), lambda b, i, j: (0, j))],
               out_specs=pl.BlockSpec((1, ts, tc), lambda b, i, j: (b, i, j)),
               compiler_params=_cp("parallel", "parallel", "parallel"))(x, x, sc, sh, w, scale)


def _pool_bwd(name, x, sc, sh, dy, w, w_t, scale, cargo=()):
    b, s, d = x.shape
    tc = d // POOL_GROUPS
    ts = _div(s, 512, POOL_HALO)
    n_s = s // ts

    def body(x_ref, xp_ref, sc_ref, sh_ref, dy_ref, dyn_ref, w_ref, wt_ref, scale_ref, du_ref, dw_ref, dscale_ref):
        grp, bi, i = pl.program_id(0), pl.program_id(1), pl.program_id(2)

        @pl.when((bi == 0) & (i == 0))
        def _():
            dw_ref[...] = jnp.zeros_like(dw_ref)
            dscale_ref[...] = jnp.zeros_like(dscale_ref)

        keep_prev = _flag(i > 0)
        keep_next = _flag(i < n_s - 1)
        mod = 1.0 + sc_ref[0]
        u = x_ref[0] * mod + sh_ref[0]
        u_ext = jnp.concatenate([(xp_ref[0] * mod + sh_ref[0]) * keep_prev, u], axis=0)
        summed = _pick_window(grp, _window_sums(u_ext, _back))[POOL_HALO:]
        pooled = (summed / _pool_counts(grp, i * ts, ts, tc) - u).astype(BF16)
        dy_ = dy_ref[0].astype(F32)
        ymat = jnp.dot(pooled, w_ref[0], preferred_element_type=F32)
        dscale_ref[...] += jnp.sum(dy_ * ymat, axis=0, keepdims=True)
        dys_ext = (jnp.concatenate([dy_, dyn_ref[0].astype(F32) * keep_next], axis=0) * scale_ref[...]).astype(BF16)
        dw_ref[0] += lax.dot_general(pooled, dys_ext[:ts], (((0,), (0,)), ((), ())), preferred_element_type=F32)
        dpooled = jnp.dot(dys_ext, wt_ref[0], preferred_element_type=F32)
        e = dpooled / _pool_counts(grp, i * ts, ts + POOL_HALO, tc)
        du_ref[0] = _pick_window(grp, _window_sums(e, _ahead))[:ts] - dpooled[:ts]

    per = ts // POOL_HALO
    tile = pl.BlockSpec((1, ts, tc), lambda j, b, i: (b, i, j))
    prev = pl.BlockSpec((1, POOL_HALO, tc), lambda j, b, i: (b, jnp.maximum(i * per - 1, 0), j))
    nxt = pl.BlockSpec((1, POOL_HALO, tc), lambda j, b, i: (b, jnp.minimum((i + 1) * per, n_s * per - 1), j))
    seq = pl.BlockSpec((1, 1, tc), lambda j, b, i: (b, 0, j))
    wsp = pl.BlockSpec((1, tc, tc), lambda j, b, i: (j, 0, 0))
    vec = pl.BlockSpec((1, tc), lambda j, b, i: (0, j))
    (du, dw, dscale), received = _pc_cargo(
        body, cargo, name=name, grid=(POOL_GROUPS, b, n_s),
        out_shape=(_sds(x.shape, F32), _sds((POOL_GROUPS, tc, tc), F32), _sds((1, d), F32)),
        in_specs=[tile, prev, seq, seq, tile, nxt, wsp, wsp, vec],
        out_specs=(tile, wsp, vec))(x, x, sc, sh, dy, dy, w, w_t, scale)
    return du, dw, dscale, received


def _rope_swap(v):
    lane = lax.broadcasted_iota(jnp.int32, v.shape, v.ndim - 1)
    lo, hi = QK_NOPE, QK_NOPE + QK_ROPE // 2
    from_above = pltpu.roll(v, HEAD_PAD - QK_ROPE // 2, axis=v.ndim - 1)
    from_below = pltpu.roll(v, QK_ROPE // 2, axis=v.ndim - 1)
    return jnp.where((lane >= lo) & (lane < hi), from_above,
                     jnp.where((lane >= hi) & (lane < hi + QK_ROPE // 2), from_below, 0.0))


def _rope(v, cos_t, sin_t):
    return v * cos_t + _rope_swap(v) * sin_t


def _rope_t(dv, cos_t, sin_t):
    return dv * cos_t + _rope_swap(dv * sin_t)


def _rms(v, g):
    r = lax.rsqrt(jnp.mean(v * v, axis=-1, keepdims=True) + RMS_EPS)
    return v * r, r


def _mla_norm_fwd(name, a, qn, kvn, cos_t, sin_t):
    b, s, wa = a.shape
    ql, kvl = qn.shape[1], kvn.shape[1]
    ts = _div(s, 512, 16)

    def body(aq_ref, akv_ref, ape_ref, qn_ref, kvn_ref, cos_ref, sin_ref, cq_ref, ckv_ref, kpe_ref):
        yq, _ = _rms(aq_ref[0], None)
        cq_ref[0] = (yq * qn_ref[...]).astype(BF16)
        ykv, _ = _rms(akv_ref[0], None)
        ckv_ref[0] = (ykv * kvn_ref[...]).astype(BF16)
        kpe_ref[0] = _rope(ape_ref[0], cos_ref[0], sin_ref[0])

    tok = lambda w, col: pl.BlockSpec((1, ts, w), lambda b, i: (b, i, col))
    return _pc(body, name=name, grid=(b, s // ts),
               out_shape=(_sds((b, s, ql), BF16), _sds((b, s, kvl), BF16), _sds((b, s, HEAD_PAD), F32)),
               in_specs=[tok(ql, 0), tok(kvl, ql // kvl), tok(HEAD_PAD, (ql + kvl) // HEAD_PAD), _vec_spec(ql),
                         _vec_spec(kvl), tok(HEAD_PAD, 0), tok(HEAD_PAD, 0)],
               out_specs=(tok(ql, 0), tok(kvl, 0), tok(HEAD_PAD, 0)),
               compiler_params=_cp("parallel", "parallel"))(a, a, a, qn, kvn, cos_t, sin_t)


def _mla_norm_bwd(name, a, dcq, dckv, dkpe, qn, kvn):
    b, s, wa = a.shape
    ql, kvl = qn.shape[1], kvn.shape[1]
    ts = _div(s, 512, 16)

    def body(a_ref, dcq_ref, dckv_ref, dkpe_ref, qn_ref, kvn_ref, da_ref, dqn_ref, dkvn_ref):
        @pl.when((pl.program_id(0) == 0) & (pl.program_id(1) == 0))
        def _():
            dqn_ref[...] = jnp.zeros_like(dqn_ref)
            dkvn_ref[...] = jnp.zeros_like(dkvn_ref)

        def one(v, dc, g_ref, dg_ref):
            yv, r = _rms(v, None)
            dg_ref[...] += jnp.sum(dc * yv, axis=0, keepdims=True)
            dyv = dc * g_ref[...]
            return r * (dyv - yv * jnp.mean(dyv * yv, axis=-1, keepdims=True))

        av = a_ref[0]
        da_ref[0, :, 0:ql] = one(av[:, 0:ql], dcq_ref[0], qn_ref, dqn_ref).astype(BF16)
        da_ref[0, :, ql:ql + kvl] = one(av[:, ql:ql + kvl], dckv_ref[0], kvn_ref, dkvn_ref).astype(BF16)
        da_ref[0, :, ql + kvl:] = dkpe_ref[0].astype(BF16)

    return _pc(body, name=name, grid=(b, s // ts),
               out_shape=(_sds(a.shape, BF16), _sds((1, ql), F32), _sds((1, kvl), F32)),
               in_specs=[_tok_spec(ts, wa), _tok_spec(ts, ql), _tok_spec(ts, kvl), _tok_spec(ts, HEAD_PAD),
                         _vec_spec(ql), _vec_spec(kvl)],
               out_specs=(_tok_spec(ts, wa), _vec_spec(ql), _vec_spec(kvl)),
               compiler_params=_cp("arbitrary", "arbitrary"))(a, dcq, dckv, dkpe, qn, kvn)


def _mla_prep_fwd(name, q_raw, kv_raw, kpe, cos_t, sin_t, n_heads):
    b, s, wq = q_raw.shape
    ts = _div(s, 256, 16)

    def body(q_ref, k_ref, v_ref, kpe_ref, cos_ref, sin_ref, qo_ref, ko_ref, vo_ref):
        cos_, sin_, kpe_ = cos_ref[0], sin_ref[0], kpe_ref[0]
        for h in range(n_heads):
            lanes = slice(h * HEAD_PAD, (h + 1) * HEAD_PAD)
            qo_ref[0, :, lanes] = _rope(q_ref[0, :, lanes], cos_, sin_).astype(BF16)
            ko_ref[0, :, lanes] = (k_ref[0, :, lanes] + kpe_).astype(BF16)
        vo_ref[0] = v_ref[0].astype(BF16)

    wide = lambda part: pl.BlockSpec((1, ts, wq), lambda b, i: (b, i, part))
    tok = pl.BlockSpec((1, ts, HEAD_PAD), lambda b, i: (b, i, 0))
    o = _sds(q_raw.shape, BF16)
    return _pc(body, name=name, grid=(b, s // ts), out_shape=(o, o, o),
               in_specs=[wide(0), wide(0), wide(1), tok, tok, tok], out_specs=(wide(0), wide(0), wide(0)),
               compiler_params=_cp("parallel", "parallel"))(q_raw, kv_raw, kv_raw, kpe, cos_t, sin_t)


def _mla_prep_bwd(name, dq, dk, dv, cos_t, sin_t, n_heads):
    b, s, wq = dq.shape
    ts = _div(s, 256, 16)

    def body(dq_ref, dk_ref, dv_ref, cos_ref, sin_ref, dqr_ref, dkv_ref, dkpe_ref):
        cos_, sin_ = cos_ref[0], sin_ref[0]
        dk_sum = None
        for h in range(n_heads):
            lanes = slice(h * HEAD_PAD, (h + 1) * HEAD_PAD)
            dqr_ref[0, :, lanes] = _rope_t(dq_ref[0, :, lanes], cos_, sin_).astype(BF16)
            dk_h = dk_ref[0, :, lanes]
            dkv_ref[0, :, lanes] = dk_h.astype(BF16)
            dk_sum = dk_h if dk_sum is None else dk_sum + dk_h
        dkv_ref[0, :, wq:2 * wq] = dv_ref[0]
        dkpe_ref[0] = _rope_t(dk_sum, cos_, sin_)

    wide = pl.BlockSpec((1, ts, wq), lambda b, i: (b, i, 0))
    both = pl.BlockSpec((1, ts, 2 * wq), lambda b, i: (b, i, 0))
    tok = pl.BlockSpec((1, ts, HEAD_PAD), lambda b, i: (b, i, 0))
    return _pc(body, name=name, grid=(b, s // ts),
               out_shape=(_sds(dq.shape, BF16), _sds((b, s, 2 * wq), BF16), _sds((b, s, HEAD_PAD), F32)),
               in_specs=[wide, wide, wide, tok, tok], out_specs=(wide, both, tok),
               compiler_params=_cp("parallel", "parallel"))(dq, dk, dv, cos_t, sin_t)


FLASH_TILE = 1024
LOG2_E = 1.4426950408889634


def _heads_per_step(n_heads):
    return 2 if n_heads % 2 == 0 else 1


def _causal_mask(i, j, tq, tk):
    rows = lax.broadcasted_iota(jnp.int32, (tq, tk), 0) + i * tq
    cols = lax.broadcasted_iota(jnp.int32, (tq, tk), 1) + j * tk
    return cols <= rows


def _nt(a, b):
    return lax.dot_general(a, b, (((1,), (1,)), ((), ())), preferred_element_type=F32)


def _tn(a, b):
    return lax.dot_general(a, b, (((0,), (0,)), ((), ())), preferred_element_type=F32)


def _flash_fwd(name, q, k, v, n_heads, sm_scale, cargo=()):
    b, s, _ = q.shape
    t, hp = _div(s, FLASH_TILE, LANE), _heads_per_step(n_heads)
    n, w = s // t, hp * HEAD_PAD
    neg = float(jnp.finfo(jnp.float32).min)
    c2 = sm_scale * LOG2_E

    def body(q_ref, k_ref, v_ref, o_ref, lse_ref, m_ref, l_ref, acc_ref):
        i, j = pl.program_id(2), pl.program_id(3)

        @pl.when(j == 0)
        def _():
            m_ref[...] = jnp.full(m_ref.shape, neg, F32)
            l_ref[...] = jnp.zeros_like(l_ref)
            acc_ref[...] = jnp.zeros_like(acc_ref)

        def block(on_diagonal):
            for hh in range(hp):
                ln = slice(hh * HEAD_PAD, (hh + 1) * HEAD_PAD)
                sc = _nt(q_ref[0, :, ln], k_ref[0, :, ln])
                if on_diagonal:
                    sc = jnp.where(_causal_mask(i, j, t, t), sc, neg)
                m_old = m_ref[hh]
                m_new = jnp.maximum(m_old, jnp.max(sc, axis=-1, keepdims=True))
                p = jnp.exp2((sc - m_new) * c2)
                corr = jnp.exp2((m_old - m_new) * c2)
                l_ref[hh] = corr * l_ref[hh] + jnp.sum(p, axis=-1, keepdims=True)
                acc_ref[:, ln] = corr * acc_ref[:, ln] + jnp.dot(p.astype(BF16), v_ref[0, :, ln],
                                                                 preferred_element_type=F32)
                m_ref[hh] = m_new

        pl.when(j < i)(functools.partial(block, False))
        pl.when(j == i)(functools.partial(block, True))

        @pl.when(j == n - 1)
        def _():
            for hh in range(hp):
                ln = slice(hh * HEAD_PAD, (hh + 1) * HEAD_PAD)
                o_ref[0, :, ln] = (acc_ref[:, ln] / l_ref[hh]).astype(BF16)
                lse_ref[0, :, ln] = jnp.broadcast_to(m_ref[hh] * sm_scale + jnp.log(l_ref[hh]), (t, HEAD_PAD))

    qs = pl.BlockSpec((1, t, w), lambda b, h, i, j: (b, i, h))
    ks = pl.BlockSpec((1, t, w), lambda b, h, i, j: (b, jnp.minimum(j, i), h))
    (o, lse), fetched = _pc_cargo(
        body, cargo, name=name, grid=(b, n_heads // hp, n, n), route=_Fetch,
        out_shape=(_sds(q.shape, BF16), _sds(q.shape, F32)), in_specs=[qs, ks, ks], out_specs=(qs, qs),
        scratch_shapes=[pltpu.VMEM((hp, t, 1), F32), pltpu.VMEM((hp, t, 1), F32), pltpu.VMEM((t, w), F32)])(q, k, v)
    return o, lse, fetched


def _flash_bwd(name, q, k, v, o, lse, do, n_heads, sm_scale):
    b, s, _ = q.shape
    t, hp = _div(s, FLASH_TILE, LANE), _heads_per_step(n_heads)
    n, w = s // t, hp * HEAD_PAD
    c2 = sm_scale * LOG2_E

    def body(q_ref, k_ref, v_ref, o_ref, lse_ref, do_ref, dq_hbm, dk_ref, dv_ref, dq_acc, dk_acc, dv_acc, dq_sem):
        bi, hi, j, i = pl.program_id(0), pl.program_id(1), pl.program_id(2), pl.program_id(3)

        @pl.when(i == 0)
        def _():
            dk_acc[...] = jnp.zeros_like(dk_acc)
            dv_acc[...] = jnp.zeros_like(dv_acc)

        rows = pl.ds(pl.multiple_of(i * t, t), t)

        def block(on_diagonal):
            for hh in range(hp):
                ln = slice(hh * HEAD_PAD, (hh + 1) * HEAD_PAD)
                do_ = do_ref[0, :, ln]
                delta = jnp.sum(do_.astype(F32) * o_ref[0, :, ln].astype(F32), axis=-1, keepdims=True)
                sc = _nt(q_ref[0, :, ln], k_ref[0, :, ln])
                p = jnp.exp2(sc * c2 - lse_ref[0, :, hh * HEAD_PAD:hh * HEAD_PAD + 1] * LOG2_E)
                if on_diagonal:
                    p = jnp.where(_causal_mask(i, j, t, t), p, 0.0)
                dv_acc[:, ln] += _tn(p.astype(BF16), do_)
                dp = _nt(do_, v_ref[0, :, ln])
                ds = (p * (dp - delta)).astype(BF16)
                dk_acc[:, ln] += _tn(ds, q_ref[0, :, ln])
                dq_part = jnp.dot(ds, k_ref[0, :, ln], preferred_element_type=F32)

                @pl.when(j == 0)
                def _():
                    dq_acc[rows, ln] = dq_part

                @pl.when(j > 0)
                def _():
                    dq_acc[rows, ln] += dq_part

        pl.when(i > j)(functools.partial(block, False))
        pl.when(i == j)(functools.partial(block, True))

        @pl.when(i == j)
        def _():
            dq_acc[rows, :] = dq_acc[rows, :] * sm_scale
            done = pltpu.make_async_copy(dq_acc.at[rows, :], dq_hbm.at[bi, rows, pl.ds(pl.multiple_of(hi * w, w), w)],
                                         dq_sem)
            done.start()
            done.wait()

        @pl.when(i == n - 1)
        def _():
            dk_ref[0] = dk_acc[...] * sm_scale
            dv_ref[0] = dv_acc[...].astype(BF16)

    qs = pl.BlockSpec((1, t, w), lambda b, h, j, i: (b, jnp.maximum(i, j), h))
    ks = pl.BlockSpec((1, t, w), lambda b, h, j, i: (b, j, h))
    return _pc(body, name=name, grid=(b, n_heads // hp, n, n),
               out_shape=(_sds(q.shape, F32), _sds(q.shape, F32), _sds(q.shape, BF16)),
               in_specs=[qs, ks, ks, qs, qs, qs], out_specs=(pl.BlockSpec(memory_space=pltpu.HBM), ks, ks),
               scratch_shapes=[pltpu.VMEM((s, w), F32), pltpu.VMEM((t, w), F32), pltpu.VMEM((t, w), F32),
                               pltpu.SemaphoreType.DMA],
               compiler_params=_cp("arbitrary", "arbitrary", "arbitrary", "arbitrary"))(q, k, v, o, lse, do)


def _mod_fwd(name, c_all, w, bias):
    depth, d, n = w.shape
    rows = c_all.shape[0]

    def body(c_ref, w_ref, b_ref, o_ref):
        cv = c_ref[...]
        cond = (cv * _sigmoid(cv)).astype(BF16)
        o_ref[0] = jnp.dot(cond, w_ref[0].astype(BF16), preferred_element_type=F32) + b_ref[0]

    return _pc(body, name=name, grid=(depth,), out_shape=_sds((depth, rows, n), F32),
               in_specs=[pl.BlockSpec((rows, d), lambda l: (0, 0)), pl.BlockSpec((1, d, n), lambda l: (l, 0, 0)),
                         pl.BlockSpec((1, 1, n), lambda l: (l, 0, 0))],
               out_specs=pl.BlockSpec((1, rows, n), lambda l: (l, 0, 0)), compiler_params=_cp("parallel"))(c_all, w, bias)


def _mod_bwd(name, c_all, dmod_cols, dmod_all):
    depth, rows, n = dmod_cols.shape
    d = c_all.shape[1]
    n_all = dmod_all.shape[2]
    tn = _div(n, 512, LANE)

    def body(c_ref, dm_ref, dma_ref, gw_ref, gb_ref):
        cv = c_ref[...]
        cond = (cv * _sigmoid(cv)).astype(BF16)
        gw_ref[0] = _tn(cond, dm_ref[0].astype(BF16))

        @pl.when(pl.program_id(1) == 0)
        def _():
            gb_ref[0] = jnp.sum(dma_ref[0], axis=0, keepdims=True)

    return _pc(body, name=name, grid=(depth, n // tn),
               out_shape=(_sds((depth, d, n), F32), _sds((depth, 1, n_all), F32)),
               in_specs=[pl.BlockSpec((rows, d), lambda l, j: (0, 0)), pl.BlockSpec((1, rows, tn), lambda l, j: (l, 0, j)),
                         pl.BlockSpec((1, rows, n_all), lambda l, j: (l, 0, 0))],
               out_specs=(pl.BlockSpec((1, d, tn), lambda l, j: (l, 0, j)), pl.BlockSpec((1, 1, n_all), lambda l, j: (l, 0, 0))),
               compiler_params=_cp("parallel", "arbitrary"))(c_all, dmod_cols, dmod_all)


def _adamw(name, w, g, m, v):
    shape = w.shape
    cols = shape[-1]
    rows = _size(shape) // cols
    tr = _div(rows, max(SUBLANE, (2 ** 19) // cols // SUBLANE * SUBLANE), SUBLANE)
    c1 = 1.0 - ADAM_B1 ** ADAM_STEP
    c2 = 1.0 - ADAM_B2 ** ADAM_STEP

    def body(w_ref, g_ref, m_ref, v_ref, d_ref, mo_ref, vo_ref):
        gv = g_ref[...]
        m_new = ADAM_B1 * m_ref[...] + (1.0 - ADAM_B1) * gv
        v_new = ADAM_B2 * v_ref[...] + (1.0 - ADAM_B2) * (gv * gv)
        m_hat = m_new / c1
        v_hat = v_new / c2
        d_ref[...] = -ADAM_LR * (m_hat / (jnp.sqrt(v_hat) + ADAM_EPS) + ADAM_WD * w_ref[...])
        mo_ref[...] = m_new
        vo_ref[...] = v_new

    spec = pl.BlockSpec((tr, cols), lambda i: (i, 0))
    o = _sds((rows, cols), F32)
    outs = _pc(body, name=name, grid=(rows // tr,), out_shape=(o, o, o), in_specs=[spec] * 4, out_specs=(spec,) * 3,
               compiler_params=_cp("parallel"))(*[a.reshape(rows, cols) for a in (w, g, m, v)])
    return tuple(a.reshape(shape) for a in outs)


def _rope_tables(positions):
    half = QK_ROPE // 2
    inv_freq = ROPE_THETA ** (-jnp.arange(0, QK_ROPE, 2, dtype=F32) / QK_ROPE)
    ang = positions.astype(F32)[..., None] * inv_freq
    cos, sin = jnp.cos(ang), jnp.sin(ang)
    lead = positions.shape
    ones = jnp.ones(lead + (QK_NOPE,), F32)
    tail_one = jnp.ones(lead + (HEAD_PAD - QK_NOPE - QK_ROPE,), F32)
    cos_t = jnp.concatenate([ones, cos, cos, tail_one], axis=-1)
    sin_t = jnp.concatenate([0 * ones, -sin, sin, 0 * tail_one], axis=-1)
    return cos_t, sin_t


def _pad_heads(w, n_heads, parts, axis):
    w = jnp.moveaxis(w, axis, -1)
    lead = w.shape[:-1]
    per = w.shape[-1] // n_heads
    w = w.reshape(lead + (n_heads, per))
    kept = jnp.concatenate([w[..., a:b_] for a, b_ in parts], axis=-1)
    pad = HEAD_PAD - kept.shape[-1]
    kept = jnp.concatenate([kept, jnp.zeros(lead + (n_heads, pad), w.dtype)], axis=-1)
    return jnp.moveaxis(kept.reshape(lead + (n_heads * HEAD_PAD,)), -1, axis)


def _unpad_heads(g, n_heads, width, axis):
    g = jnp.moveaxis(g, axis, -1)
    lead = g.shape[:-1]
    g = g.reshape(lead + (n_heads, HEAD_PAD))[..., :width]
    return g, lead


def kernel(x, c, positions, mod_w, mod_b, ln_g, ln_b, pool_w, pool_scale, mla_w_a, mla_q_norm, mla_w_uq, mla_kv_norm, mla_w_ukv, mla_w_o, sc_w_in, sc_conv, sc_w_out, ffn_w_up, ffn_conv, ffn_conv_b, ffn_w_down, loss_target, m_mod_w, m_mod_b, m_ln_g, m_ln_b, m_pool_w, m_pool_scale, m_mla_w_a, m_mla_q_norm, m_mla_w_uq, m_mla_kv_norm, m_mla_w_ukv, m_mla_w_o, m_sc_w_in, m_sc_conv, m_sc_w_out, m_ffn_w_up, m_ffn_conv, m_ffn_conv_b, m_ffn_w_down, v_mod_w, v_mod_b, v_ln_g, v_ln_b, v_pool_w, v_pool_scale, v_mla_w_a, v_mla_q_norm, v_mla_w_uq, v_mla_kv_norm, v_mla_w_ukv, v_mla_w_o, v_sc_w_in, v_sc_conv, v_sc_w_out, v_ffn_w_up, v_ffn_conv, v_ffn_conv_b, v_ffn_w_down):
    wts = dict(mod_w=mod_w, mod_b=mod_b, ln_g=ln_g, ln_b=ln_b, pool_w=pool_w, pool_scale=pool_scale, mla_w_a=mla_w_a,
               mla_q_norm=mla_q_norm, mla_w_uq=mla_w_uq, mla_kv_norm=mla_kv_norm, mla_w_ukv=mla_w_ukv, mla_w_o=mla_w_o,
               sc_w_in=sc_w_in, sc_conv=sc_conv, sc_w_out=sc_w_out, ffn_w_up=ffn_w_up, ffn_conv=ffn_conv,
               ffn_conv_b=ffn_conv_b, ffn_w_down=ffn_w_down)
    mom1 = dict(mod_w=m_mod_w, mod_b=m_mod_b, ln_g=m_ln_g, ln_b=m_ln_b, pool_w=m_pool_w, pool_scale=m_pool_scale,
                mla_w_a=m_mla_w_a, mla_q_norm=m_mla_q_norm, mla_w_uq=m_mla_w_uq, mla_kv_norm=m_mla_kv_norm,
                mla_w_ukv=m_mla_w_ukv, mla_w_o=m_mla_w_o, sc_w_in=m_sc_w_in, sc_conv=m_sc_conv, sc_w_out=m_sc_w_out,
                ffn_w_up=m_ffn_w_up, ffn_conv=m_ffn_conv, ffn_conv_b=m_ffn_conv_b, ffn_w_down=m_ffn_w_down)
    mom2 = dict(mod_w=v_mod_w, mod_b=v_mod_b, ln_g=v_ln_g, ln_b=v_ln_b, pool_w=v_pool_w, pool_scale=v_pool_scale,
                mla_w_a=v_mla_w_a, mla_q_norm=v_mla_q_norm, mla_w_uq=v_mla_w_uq, mla_kv_norm=v_mla_kv_norm,
                mla_w_ukv=v_mla_w_ukv, mla_w_o=v_mla_w_o, sc_w_in=v_sc_w_in, sc_conv=v_sc_conv, sc_w_out=v_sc_w_out,
                ffn_w_up=v_ffn_w_up, ffn_conv=v_ffn_conv, ffn_conv_b=v_ffn_conv_b, ffn_w_down=v_ffn_w_down)

    bsz, seq, d = x.shape
    depth = mod_b.shape[0]
    n_tok = bsz * seq
    n_heads = d // V_HEAD
    ql, kvl = mla_q_norm.shape[1], mla_kv_norm.shape[1]
    alpha = float((2 * depth) ** 0.25)
    sm_scale = float((QK_NOPE + QK_ROPE) ** -0.5)
    mx, my, mc = lax.axis_index("x"), lax.axis_index("y"), lax.axis_index("c")
    chip = 2 * mx + my
    dev = 2 * chip + mc

    small_names = list(SMALL_SHARDED)
    small_pack, small_spans = _pack_rows([c] + [wts[n] for n in small_names], F32, SUBLANE)
    rows_small = small_pack.shape[0]
    small_all = _all_gather8("gather_small_params", small_pack, True).reshape(N_DEV, rows_small * PACK_COLS)
    c_all = small_all[:, :c.size].reshape(N_DEV * bsz, d)
    per_chip = small_all[0::2]
    full = dict(wts)
    for n, (off, shape) in zip(small_names, small_spans[1:]):
        blocks = per_chip[:, off:off + _size(shape)].reshape((N_CHIP,) + tuple(shape))
        full[n] = _join_chips(blocks, SMALL_SHARDED[n])

    n_mod = mod_w.shape[2]
    bias_cols = lax.dynamic_slice_in_dim(mod_b, chip * n_mod, n_mod, axis=1)[:, None, :]
    mod_cols = _mod_fwd("mod_fwd", c_all, mod_w, bias_cols)
    half_rows = (N_DEV * bsz) // 2
    mod_half = lax.dynamic_slice_in_dim(mod_cols, mc * half_rows, half_rows, axis=1).reshape(depth * half_rows, n_mod)
    mod_all = _all_gather8("gather_mod", mod_half, True).reshape(N_CHIP, 2, depth, half_rows, n_mod)
    mod_all = jnp.transpose(mod_all, (2, 1, 3, 0, 4)).reshape(depth, N_DEV * bsz, N_CHIP * n_mod)
    mod_mine = lax.dynamic_slice_in_dim(mod_all, dev * bsz, bsz, axis=1)
    mods = [[mod_mine[l, :, k * d:(k + 1) * d][:, None, :] for k in range(6)] for l in range(depth)]

    big_names = list(BIG)
    f_hid = ffn_w_down.shape[1] * N_CHIP
    host_layer = 1
    assert depth > host_layer

    def layer_of(n, i):
        if n == 'pool_w':
            return 3 * i
        if n.startswith('mla_'):
            return 3 * i + 1
        if n.startswith('sc_'):
            return 3 * i + 2
        return i

    last_layer_of_group = (0, host_layer)

    def group_of(n, i):
        return sum(layer_of(n, i) > top for top in last_layer_of_group)

    n_groups = len(last_layer_of_group) + 1
    span = {n: [[i for i in range(wts[n].shape[0]) if group_of(n, i) == g] for g in range(n_groups)] for n in big_names}
    members = [[n for n in big_names if span[n][g]] for g in range(n_groups)]

    def rows2d(a):
        return a.astype(BF16).reshape(-1, a.shape[-1])

    def layouts(bc):
        out = {}
        for n in ('pool_w', 'mla_w_a', 'mla_w_uq', 'mla_w_ukv', 'mla_w_o', 'sc_w_out'):
            if n in bc:
                out[n] = jnp.concatenate([bc[n][j] for j in range(N_CHIP)], axis=BIG[n])
        if 'ffn_w_up' in bc:
            nl = bc['ffn_w_up'].shape[1]
            out['up_cols'] = bc['ffn_w_up'].reshape(N_CHIP, nl * d, -1)
            out['up_rows'] = jnp.transpose(bc['ffn_w_up'], (1, 0, 3, 2)).reshape(1, nl * 2 * f_hid, d)
            out['down_rows'] = jnp.transpose(bc['ffn_w_down'], (1, 0, 2, 3)).reshape(1, nl * f_hid, d)
            out['down_t'] = jnp.transpose(bc['ffn_w_down'], (1, 3, 0, 2)).reshape(1, nl * d, f_hid)
        if 'sc_w_in' in bc:
            ns = bc['sc_w_in'].shape[1]
            out['in_cols'] = bc['sc_w_in'].reshape(N_CHIP, ns * d, -1)
            out['in_rows'] = jnp.transpose(bc['sc_w_in'], (1, 0, 3, 2)).reshape(1, ns * 3 * d, d)
        return out

    shards = [{n: rows2d(wts[n][span[n][g][0]:span[n][g][-1] + 1]) for n in members[g]} for g in range(n_groups)]
    lay = [None] * n_groups

    def by_chip(g, n, blocks):
        return blocks.reshape((N_CHIP, len(span[n][g])) + wts[n].shape[1:])

    def fetched_group(g, got):
        lay[g] = layouts({n: by_chip(g, n, lax.dynamic_update_index_in_dim(blocks, shards[g][n], chip, 0))
                          for n, blocks in got.items()})

    gathered = _gather_weights("gather_weights", [shards[0][n] for n in members[0]])
    lay[0] = layouts({n: by_chip(0, n, blocks) for n, blocks in zip(members[0], gathered)})

    def grp(n, i):
        g = group_of(n, i)
        return lay[g], i - span[n][g][0]

    nope_rope = [(0, QK_NOPE + QK_ROPE)]
    cos_t, sin_t = _rope_tables(positions)

    def t2(a):
        return a.reshape(n_tok, a.shape[-1])

    def t3(a):
        return a.reshape(bsz, seq, a.shape[-1])

    saved = []
    xin = x
    u = _modulate("modulate_in", x, mods[0][1], mods[0][0])
    loss_acc = None
    for l in range(depth):
        sh1, sc1, g1, sh2, sc2, g2 = mods[l]
        kind, j = l % 3, l // 3
        st = dict(x=xin)
        if kind == 0:
            grp_l, jj = grp('pool_w', j)
            w = grp_l['pool_w'][jj]
            st.update(w=w, w_t=jnp.swapaxes(w, 1, 2), scale=full['pool_scale'][j][None, :])
            y = _pool_fwd(f"pool_fwd_{l}", xin, sc1, sh1, st['w'], st['scale'])
        elif kind == 1:
            grp_l, jj = grp('mla_w_a', j)
            wa, wuq, wukv = grp_l['mla_w_a'][jj], grp_l['mla_w_uq'][jj], grp_l['mla_w_ukv'][jj]
            zeros = jnp.zeros((d, QK_NOPE), BF16)
            w_a = jnp.concatenate([wa[:, :ql + kvl], zeros, wa[:, ql + kvl:], zeros[:, :HEAD_PAD - QK_NOPE - QK_ROPE]], axis=1)
            w_uq = _pad_heads(wuq, n_heads, nope_rope, 1)
            w_kv = jnp.concatenate([_pad_heads(wukv, n_heads, [(0, QK_NOPE)], 1),
                                    _pad_heads(wukv, n_heads, [(QK_NOPE, QK_NOPE + V_HEAD)], 1)], axis=1)
            w_o = _pad_heads(grp_l['mla_w_o'][jj], n_heads, [(0, V_HEAD)], 0)
            qn, kvn = mla_q_norm[j][None, :], mla_kv_norm[j][None, :]
            a = t3(_mm_nn(f"mla_a_{l}", [(t2(u), _w2(w_a))], F32))
            cq, ckv, kpe = _mla_norm_fwd(f"mla_norm_fwd_{l}", a, qn, kvn, cos_t, sin_t)
            q_raw = t3(_mm_nn(f"mla_q_{l}", [(t2(cq), _w2(w_uq))], F32))
            kv_raw = t3(_mm_nn(f"mla_kv_{l}", [(t2(ckv), _w2(w_kv))], F32))
            qh, kh, vh = _mla_prep_fwd(f"mla_prep_fwd_{l}", q_raw, kv_raw, kpe, cos_t, sin_t, n_heads)
            o, lse, fetched = _flash_fwd(f"flash_fwd_{l}", qh, kh, vh, n_heads, sm_scale,
                                         cargo=[shards[2][n] for n in members[2]] if l == host_layer else ())
            if l == host_layer:
                fetched_group(2, dict(zip(members[2], fetched)))
            y = t3(_mm_nn(f"mla_o_{l}", [(t2(o), _w2(w_o))], F32))
            st.update(u=u, w_a=w_a, w_uq=w_uq, w_kv=w_kv, w_o=w_o, qn=qn, kvn=kvn, a=a, cq=cq, ckv=ckv,
                      qh=qh, kh=kh, vh=vh, o=o, lse=lse)
        else:
            grp_l, jj = grp('sc_w_in', j)
            w_out, cw = grp_l['sc_w_out'][jj], full['sc_conv'][j]
            q = t3(_mm_nn(f"sc_in_{l}", [(t2(u), (grp_l['in_cols'], jj))], F32))
            r = _shortconv_fwd(f"shortconv_fwd_{l}", q, cw)
            y = t3(_mm_nn(f"sc_out_{l}", [(t2(r), _w2(w_out))], F32))
            st.update(u=u, w_out=w_out, cw=cw, q=q, r=r, in_rows=(grp_l['in_rows'], jj))
        lng, lnb = full['ln_g'][l], full['ln_b'][l]
        z1, xmid, u2 = _ln_mod_fwd(f"ln_mod_a_{l}", alpha, xin, y, g1, lng[0:1], lnb[0:1], sc2, sh2)
        cwf, cbf = full['ffn_conv'][l], ffn_conv_b[l][None, :]
        ffn_w, ll = grp('ffn_w_up', l)
        ffn_names = ('ffn_w_up', 'ffn_w_down')
        ride_mm = [n for n in members[1] if n not in ffn_names] if l == 0 else []
        ride_conv = [n for n in members[1] if n == 'ffn_w_up'] if l == 0 else []
        ride_down = [n for n in members[1] if n == 'ffn_w_down'] if l == 0 else []
        p = _mm_nn(f"ffn_up_{l}", [(t2(u2), (ffn_w['up_cols'], ll))], F32, cargo=[shards[1][n] for n in ride_mm],
                   route=_Fetch)
        got_mm = ()
        if ride_mm:
            p, got_mm = p
        p = t3(p)
        act, got_conv = _convglu_fwd(f"convglu_fwd_{l}", p, cwf, cbf, cargo=[shards[1][n] for n in ride_conv])
        y2 = _mm_nn(f"ffn_down_{l}", [(t2(act), (ffn_w['down_rows'], ll))], F32, cargo=[shards[1][n] for n in ride_down],
                    route=_Fetch)
        got_down = ()
        if ride_down:
            y2, got_down = y2
        y2 = t3(y2)
        if l == 0:
            fetched_group(1, {**dict(zip(ride_mm, got_mm)), **dict(zip(ride_conv, got_conv)),
                              **dict(zip(ride_down, got_down))})
        st.update(y1=y, z1=z1, xmid=xmid, u2=u2, p=p, act=act, y2=y2, cwf=cwf, cbf=cbf, lng=lng, lnb=lnb,
                  ffn_w=ffn_w, ll=ll)
        if l + 1 < depth:
            nsh1, nsc1 = mods[l + 1][0], mods[l + 1][1]
            z2, xin, u = _ln_mod_fwd(f"ln_mod_b_{l}", alpha, xmid, y2, g2, lng[1:2], lnb[1:2], nsc1, nsh1)
        else:
            z2, ct, loss_acc = _ln_loss_fwd("ln_loss", alpha, xmid, y2, g2, lng[1:2], lnb[1:2], loss_target)
        st.update(z2=z2)
        saved.append(st)
    loss = lax.psum(loss_acc[0, 0], ("x", "y", "c"))

    grads = {}
    dmods = [[None] * 6 for _ in range(depth)]
    g_ln_g = [[None, None] for _ in range(depth)]
    g_ln_b = [[None, None] for _ in range(depth)]
    stack = {n: [None] * wts[n].shape[0] for n in ('pool_scale', 'mla_q_norm', 'mla_kv_norm', 'sc_conv', 'ffn_conv',
                                                    'ffn_conv_b')}
    units = {n: [None] * wts[n].shape[0] for n in big_names}
    cargo_a, cargo_b, received = [], [], {}

    def landed(items, got):
        for (n, i, _), r in zip(items, got):
            received[(n, i)] = r

    upstream = (ct,)
    for l in reversed(range(depth)):
        st = saved[l]
        sh1, sc1, g1, sh2, sc2, g2 = mods[l]
        kind, j = l % 3, l // 3
        if len(upstream) > 1:
            upstream = (upstream[0], upstream[1], st['lnb'][1:2], upstream[2])
        res = _sub_bwd(f"sub_bwd_b_{l}", alpha, upstream, st['z2'], st['y2'], g2, st['lng'][1:2])
        dz2, dy2, dmods[l][5], g_ln_g[l][1], g_ln_b[l][1] = res[:5]
        if l + 1 < depth:
            dmods[l + 1][1], dmods[l + 1][0] = res[5], res[6]
        dy2f = t2(dy2)
        ffn_w, ll = st['ffn_w'], st['ll']
        da = t3(_mm_nn(f"ffn_down_bwd_{l}", [(dy2f, (ffn_w['down_t'], ll))], F32))
        units['ffn_w_down'][l] = _mm_tn(f"ffn_down_dw_{l}", t2(st['act']), [dy2f],
                                        out_dtype=BF16).reshape(N_CHIP, f_hid // N_CHIP, d)
        dpv, dpg, dcw, dcb, got = _convglu_bwd(f"convglu_bwd_{l}", st['p'], da, st['cwf'], st['cbf'],
                                               cargo=[u for _, _, u in cargo_a])
        landed(cargo_a, got)
        stack['ffn_conv'][l], stack['ffn_conv_b'][l] = dcw, dcb[0]
        down_unit = [('ffn_w_down', l, units['ffn_w_down'][l])]
        du2, got = _mm_nn(f"ffn_up_bwd_{l}", [(t2(dpv), (ffn_w['up_rows'], 2 * ll)),
                                              (t2(dpg), (ffn_w['up_rows'], 2 * ll + 1))], F32,
                          cargo=[units['ffn_w_down'][l]])
        landed(down_unit, got)
        du2 = t3(du2)
        res = _mm_tn(f"ffn_up_dw_{l}", t2(st['u2']), [t2(dpv), t2(dpg)], N_CHIP, out_dtype=BF16,
                     cargo=[u for _, _, u in cargo_b])
        if cargo_b:
            landed(cargo_b, res[1])
            res = res[0]
        units['ffn_w_up'][l] = res
        res = _sub_bwd(f"sub_bwd_a_{l}", alpha, (dz2, du2, st['lnb'][0:1], sc2), st['z1'], st['y1'], g1, st['lng'][0:1])
        dz1, dy1, dmods[l][2], g_ln_g[l][0], g_ln_b[l][0], dmods[l][4], dmods[l][3] = res
        dy1f = t2(dy1)
        if kind == 0:
            up_unit = [('ffn_w_up', l, units['ffn_w_up'][l])] if l == 0 else []
            du1, dw, dscale, got = _pool_bwd(f"pool_bwd_{l}", st['x'], sc1, sh1, dy1, st['w'], st['w_t'], st['scale'],
                                             cargo=[u for _, _, u in up_unit])
            landed(up_unit, got)
            stack['pool_scale'][j] = dscale[0]
            grp = dw.shape[1] // N_CHIP
            units['pool_w'][j] = jnp.transpose(dw.reshape(POOL_GROUPS, N_CHIP, grp, dw.shape[2]),
                                               (1, 0, 2, 3)).reshape(N_CHIP, POOL_GROUPS * grp, dw.shape[2])
        elif kind == 1:
            do = t3(_mm_nn(f"mla_o_bwd_{l}", [(dy1f, _w2(jnp.swapaxes(st['w_o'], 0, 1)))], BF16))
            gwo, _ = _unpad_heads(_mm_tn(f"mla_o_dw_{l}", t2(st['o']), [dy1f])[0], n_heads, V_HEAD, 0)
            units['mla_w_o'][j] = jnp.moveaxis(gwo.reshape(d, n_heads * V_HEAD), -1, 0).reshape(N_CHIP, -1, d)
            fa = (st['qh'], st['kh'], st['vh'], st['o'], st['lse'], do, n_heads, sm_scale)
            dq, dk, dv = _flash_bwd(f"flash_bwd_{l}", *fa)
            dq_raw, dkv_raw, dkpe = _mla_prep_bwd(f"mla_prep_bwd_{l}", dq, dk, dv, cos_t, sin_t, n_heads)
            dq_raw, dkv_raw = t2(dq_raw), t2(dkv_raw)
            dcq = t3(_mm_nn(f"mla_q_bwd_{l}", [(dq_raw, _w2(jnp.swapaxes(st['w_uq'], 0, 1)))], F32))
            dckv = t3(_mm_nn(f"mla_kv_bwd_{l}", [(dkv_raw, _w2(jnp.swapaxes(st['w_kv'], 0, 1)))], F32))
            gq, _ = _unpad_heads(_mm_tn(f"mla_q_dw_{l}", t2(st['cq']), [dq_raw])[0], n_heads, QK_NOPE + QK_ROPE, 1)
            units['mla_w_uq'][j] = _cols_by_chip(gq.reshape(ql, n_heads * (QK_NOPE + QK_ROPE)))
            gkv = _mm_tn(f"mla_kv_dw_{l}", t2(st['ckv']), [dkv_raw])[0]
            gk, _ = _unpad_heads(gkv[:, :n_heads * HEAD_PAD], n_heads, QK_NOPE, 1)
            gv, _ = _unpad_heads(gkv[:, n_heads * HEAD_PAD:], n_heads, V_HEAD, 1)
            units['mla_w_ukv'][j] = _cols_by_chip(
                jnp.concatenate([gk, gv], axis=-1).reshape(kvl, n_heads * (QK_NOPE + V_HEAD)))
            da_, dqn, dkvn = _mla_norm_bwd(f"mla_norm_bwd_{l}", st['a'], dcq, dckv, dkpe, st['qn'], st['kvn'])
            stack['mla_q_norm'][j], stack['mla_kv_norm'][j] = dqn[0], dkvn[0]
            du1 = t3(_mm_nn(f"mla_a_bwd_{l}", [(t2(da_), _w2(jnp.swapaxes(st['w_a'], 0, 1)))], F32))
            gwa = _mm_tn(f"mla_a_dw_{l}", t2(st['u']), [t2(da_)])[0]
            units['mla_w_a'][j] = _cols_by_chip(jnp.concatenate(
                [gwa[:, :ql + kvl], gwa[:, ql + kvl + QK_NOPE:ql + kvl + QK_NOPE + QK_ROPE]], axis=1))
        else:
            dr = t3(_mm_nn(f"sc_out_bwd_{l}", [(dy1f, _w2(jnp.swapaxes(st['w_out'], 0, 1)))], F32))
            units['sc_w_out'][j] = _mm_tn(f"sc_out_dw_{l}", t2(st['r']), [dy1f], out_dtype=BF16).reshape(N_CHIP, -1, d)
            dgb, dgc, dh, dcw = _shortconv_bwd(f"shortconv_bwd_{l}", st['q'], dr, st['cw'])
            stack['sc_conv'][j] = dcw
            parts = [t2(dgb), t2(dgc), t2(dh)]
            in_rows, jj = st['in_rows']
            du1 = t3(_mm_nn(f"sc_in_bwd_{l}", [(parts[k], (in_rows, 3 * jj + k)) for k in range(3)], F32))
            units['sc_w_in'][j] = _cols_by_chip(jnp.concatenate(
                [_mm_tn(f"sc_in_dw_{k}_{l}", t2(st['u']), [parts[k]])[0] for k in range(3)], axis=1))
        upstream = (dz1, du1, sc1)
        mixer = {0: ['pool_w'], 1: ['mla_w_a', 'mla_w_uq', 'mla_w_ukv', 'mla_w_o'], 2: ['sc_w_in', 'sc_w_out']}[kind]
        for n in mixer:
            units[n][j] = units[n][j].astype(BF16)
        cargo_a = [('ffn_w_up', l, units['ffn_w_up'][l])] if l > 0 else []
        cargo_b = [(n, j, units[n][j]) for n in mixer]
    grad_x, dmods[0][1], dmods[0][0] = _input_bwd("input_bwd", alpha, upstream[0], upstream[1], x, mods[0][1])

    for n, parts in stack.items():
        grads[n] = jnp.stack(parts)
    grads['ln_g'] = jnp.stack([jnp.concatenate(r, axis=0) for r in g_ln_g])
    grads['ln_b'] = jnp.stack([jnp.concatenate(r, axis=0) for r in g_ln_b])
    dmod_mine = jnp.stack([jnp.concatenate([t[:, 0, :] for t in dmods[l]], axis=-1) for l in range(depth)])

    small_grad_names = small_names + ['mla_q_norm', 'mla_kv_norm', 'ffn_conv_b']
    sg_pack, sg_spans = _pack_rows([dmod_mine] + [grads[n] for n in small_grad_names], F32, SUBLANE)
    rows_sg = sg_pack.shape[0]
    sg_all = _all_gather8("gather_small_grads", sg_pack, True).reshape(N_DEV, rows_sg, PACK_COLS)
    dmod_all = sg_all.reshape(N_DEV, -1)[:, :dmod_mine.size].reshape(N_DEV, depth, bsz, 6 * d)
    dmod_all = jnp.transpose(dmod_all, (1, 0, 2, 3)).reshape(depth, N_DEV * bsz, 6 * d)
    sg_sum = _sum8("sum_small_grads", sg_all).reshape(-1)
    for n, (off, shape) in zip(small_grad_names, sg_spans[1:]):
        g_full = sg_sum[off:off + _size(shape)].reshape(shape)
        if n in SMALL_SHARDED:
            ax = SMALL_SHARDED[n]
            width = shape[ax] // N_CHIP
            g_full = lax.dynamic_slice_in_dim(g_full, chip * width, width, axis=ax)
        grads[n] = g_full
    dmod_cols = lax.dynamic_slice_in_dim(dmod_all, chip * n_mod, n_mod, axis=2)
    grads['mod_w'], gb = _mod_bwd("mod_bwd", c_all, dmod_cols, dmod_all)
    grads['mod_b'] = gb[:, 0, :]

    keys = [(n, i) for n in big_names for i in range(len(units[n]))]
    last = [(n, i, units[n][i]) for n, i in keys if (n, i) not in received]
    landed(last, _scatter_grads("scatter_big_grads", [u for _, _, u in last]))
    chip_core = jnp.stack([chip, mc]).astype(jnp.int32)
    bufs = _swap_halves("swap_big_grad_halves",
                        [_sum8_into_half(f"sum_big_grads_{n}_{i}", units[n][i], received[(n, i)], chip_core)
                         for n, i in keys])
    for n in big_names:
        grads[n] = jnp.stack([b for (m, _), b in zip(keys, bufs) if m == n]).reshape(wts[n].shape)

    deltas, new_m, new_v = {}, {}, {}
    for n in WEIGHTS:
        deltas[n], new_m[n], new_v[n] = _adamw(f"adamw_{n}", wts[n], grads[n], mom1[n], mom2[n])
    return (loss, grad_x, *[grads[n] for n in WEIGHTS], *[deltas[n] for n in WEIGHTS],
            *[new_m[n] for n in WEIGHTS], *[new_v[n] for n in WEIGHTS])
```

```python
import functools

import jax
import jax.numpy as jnp
from jax import lax
from jax.experimental import pallas as pl
from jax.experimental.pallas import tpu as pltpu

F32 = jnp.float32
BF16 = jnp.bfloat16
MESH = pl.DeviceIdType.MESH

N_DEV = 8
N_CHIP = 4
LANE = 128
SUBLANE = 8
VMEM_LIMIT_BYTES = 56 * 2 ** 20
PACK_COLS = 1024

LN_EPS = 1e-5
RMS_EPS = 1e-6
QK_NOPE, QK_ROPE, V_HEAD = 64, 32, 64
ROPE_THETA = 10000.0
HEAD_PAD = 128
POOL_GROUPS = 4
POOL_HALO = 16
CONV_HALO = 8
CONV_ROWS = 1024
ADAM_LR, ADAM_B1, ADAM_B2, ADAM_EPS, ADAM_WD, ADAM_STEP = 0.001, 0.9, 0.999, 1e-08, 0.01, 10

WEIGHTS = ['mod_w', 'mod_b', 'ln_g', 'ln_b', 'pool_w', 'pool_scale', 'mla_w_a', 'mla_q_norm', 'mla_w_uq',
           'mla_kv_norm', 'mla_w_ukv', 'mla_w_o', 'sc_w_in', 'sc_conv', 'sc_w_out', 'ffn_w_up', 'ffn_conv',
           'ffn_conv_b', 'ffn_w_down']
BIG = {'pool_w': 2, 'mla_w_a': 2, 'mla_w_uq': 2, 'mla_w_ukv': 2, 'mla_w_o': 1, 'sc_w_in': 2, 'sc_w_out': 1,
       'ffn_w_up': 2, 'ffn_w_down': 1}
SMALL_SHARDED = {'ln_g': 2, 'ln_b': 2, 'pool_scale': 1, 'sc_conv': 2, 'ffn_conv': 2}
REPLICATED = ['mod_b', 'mla_q_norm', 'mla_kv_norm', 'ffn_conv_b']


def _pc(body, **kw):
    return pl.pallas_call(body, **kw)


def _cp(*sem):
    return pltpu.CompilerParams(dimension_semantics=sem, vmem_limit_bytes=VMEM_LIMIT_BYTES)


def _div(n, cap, mult):
    best = None
    for d in range(mult, min(n, cap) + 1, mult):
        if n % d == 0:
            best = d
    return best if best is not None else n


def _sds(shape, dtype):
    return jax.ShapeDtypeStruct(tuple(shape), dtype)


def _flip(v, bit):
    return 1 - v if bit else v


def _all_gather8(name, x_shard, in_vmem):
    m_per, n = x_shard.shape
    space = pltpu.VMEM if in_vmem else pltpu.HBM

    def body(x_ref, out_ref, send_sems, recv_sems, local_sem):
        x, y, c = lax.axis_index("x"), lax.axis_index("y"), lax.axis_index("c")
        me, sibling = (x, y, c), (x, y, 1 - c)
        chips = [(1 - x, y), (x, 1 - y), (1 - x, 1 - y)]

        def rows(px, py, pc_):
            return out_ref.at[pl.ds((4 * px + 2 * py + pc_) * m_per, m_per), :]

        def copy(k, block, to, src=None):
            return pltpu.make_async_remote_copy(
                src_ref=rows(*block) if src is None else src, dst_ref=rows(*block),
                send_sem=send_sems.at[k], recv_sem=recv_sems.at[k], device_id=to, device_id_type=MESH)

        mine = pltpu.make_async_copy(x_ref, rows(*me), local_sem)
        mine.start()
        first = [copy(0, me, sibling, src=x_ref)]
        first += [copy(1 + j, me, (*chip, c), src=x_ref) for j, chip in enumerate(chips)]
        for cp in first:
            cp.start()
        passed = [copy(4 + j, (*chip, c), sibling) for j, chip in enumerate(chips)]
        for j, chip in enumerate(chips):
            copy(1 + j, (*chip, c), me).wait_recv()
            passed[j].start()
        copy(0, sibling, me).wait_recv()
        for j, chip in enumerate(chips):
            copy(4 + j, (*chip, 1 - c), me).wait_recv()
        for cp in first + passed:
            cp.wait_send()
        mine.wait()

    return _pc(
        body, name=name, out_shape=_sds((N_DEV * m_per, n), x_shard.dtype),
        in_specs=[pl.BlockSpec(memory_space=space)], out_specs=pl.BlockSpec(memory_space=space),
        scratch_shapes=[pltpu.SemaphoreType.DMA((7,)), pltpu.SemaphoreType.DMA((7,)), pltpu.SemaphoreType.DMA],
        compiler_params=pltpu.CompilerParams(vmem_limit_bytes=VMEM_LIMIT_BYTES),
    )(x_shard)


def _gather_weights(name, shards):
    n_t = len(shards)
    halves = [s.shape[0] // 2 for s in shards]

    def body(*refs):
        x_refs, o_refs = refs[:n_t], refs[n_t:2 * n_t]
        send_sems, recv_sems, local_sems = refs[2 * n_t:]
        x, y, c = lax.axis_index("x"), lax.axis_index("y"), lax.axis_index("c")
        me, sibling = (x, y, c), (x, y, 1 - c)
        chips = [(1 - x, y), (x, 1 - y), (1 - x, 1 - y)]

        def slot(t, px, py, pc_):
            return o_refs[t].at[4 * px + 2 * py + pc_]

        def my_rows(t):
            return x_refs[t].at[pl.ds(c * halves[t], halves[t]), :]

        def copy(t, k, block, to, src=None):
            return pltpu.make_async_remote_copy(
                src_ref=slot(t, *block) if src is None else src, dst_ref=slot(t, *block),
                send_sem=send_sems.at[t, k], recv_sem=recv_sems.at[t, k], device_id=to, device_id_type=MESH)

        local = [pltpu.make_async_copy(my_rows(t), slot(t, *me), local_sems.at[t]) for t in range(n_t)]
        for cp in local:
            cp.start()
        first = []
        for t in range(n_t):
            first += [copy(t, 1 + j, me, (*chip, c), src=my_rows(t)) for j, chip in enumerate(chips)]
            first.append(copy(t, 0, me, sibling, src=my_rows(t)))
        for cp in first:
            cp.start()
        passed = []
        for j, chip in enumerate(chips):
            for t in range(n_t):
                copy(t, 1 + j, (*chip, c), me).wait_recv()
                passed.append(copy(t, 4 + j, (*chip, c), sibling))
                passed[-1].start()
        for t in range(n_t):
            copy(t, 0, sibling, me).wait_recv()
        for j, chip in enumerate(chips):
            for t in range(n_t):
                copy(t, 4 + j, (*chip, 1 - c), me).wait_recv()
        for cp in first + passed:
            cp.wait_send()
        for cp in local:
            cp.wait()

    hbm = pl.BlockSpec(memory_space=pltpu.HBM)
    return _pc(
        body, name=name, out_shape=tuple(_sds((N_DEV, h, s.shape[1]), s.dtype) for h, s in zip(halves, shards)),
        in_specs=[hbm] * n_t, out_specs=(hbm,) * n_t,
        scratch_shapes=[pltpu.SemaphoreType.DMA((n_t, 7)), pltpu.SemaphoreType.DMA((n_t, 7)),
                        pltpu.SemaphoreType.DMA((n_t,))],
    )(*shards)


def _scatter_copies(u_refs, r_refs, send_sems, recv_sems):
    x, y, c = lax.axis_index("x"), lax.axis_index("y"), lax.axis_index("c")
    copies = []
    for k in range(1, N_DEV):
        px, py, pcc = _flip(x, (k >> 2) & 1), _flip(y, (k >> 1) & 1), _flip(c, k & 1)
        for t, (u_ref, r_ref) in enumerate(zip(u_refs, r_refs)):
            h = u_ref.shape[1] // 2
            copies.append(pltpu.make_async_remote_copy(
                src_ref=u_ref.at[2 * px + py, pl.ds(pcc * h, h), :], dst_ref=r_ref.at[k - 1],
                send_sem=send_sems.at[t, k - 1], recv_sem=recv_sems.at[t, k - 1],
                device_id=(px, py, pcc), device_id_type=MESH))
    return copies


def _scatter_shapes(units):
    return tuple(_sds((N_DEV - 1, u.shape[1] // 2, u.shape[2]), u.dtype) for u in units)


def _scatter_grads(name, units):
    n_u = len(units)

    def body(*refs):
        copies = _scatter_copies(refs[:n_u], refs[n_u:2 * n_u], refs[2 * n_u], refs[2 * n_u + 1])
        for cp in copies:
            cp.start()
        for cp in copies:
            cp.wait()

    hbm = pl.BlockSpec(memory_space=pltpu.HBM)
    return _pc(body, name=name, out_shape=_scatter_shapes(units), in_specs=[hbm] * n_u, out_specs=(hbm,) * n_u,
               scratch_shapes=[pltpu.SemaphoreType.DMA((n_u, 7)), pltpu.SemaphoreType.DMA((n_u, 7))])(*units)


class _Scatter:
    peers = N_DEV - 1
    shapes = staticmethod(_scatter_shapes)

    @staticmethod
    def copies(u_refs, r_refs, send_sems, recv_sems):
        both = _scatter_copies(u_refs, r_refs, send_sems, recv_sems)
        return both, both


class _Fetch:
    peers = N_CHIP - 1

    @staticmethod
    def shapes(units):
        return tuple(_sds((N_CHIP,) + u.shape, u.dtype) for u in units)

    @staticmethod
    def copies(u_refs, r_refs, send_sems, recv_sems):
        x, y, c = lax.axis_index("x"), lax.axis_index("y"), lax.axis_index("c")
        sends, recvs = [], []
        for k in range(1, N_CHIP):
            px, py = _flip(x, (k >> 1) & 1), _flip(y, k & 1)
            for t, (u_ref, r_ref) in enumerate(zip(u_refs, r_refs)):
                sends.append(pltpu.make_async_remote_copy(
                    src_ref=u_ref, dst_ref=r_ref.at[2 * x + y], send_sem=send_sems.at[t, k - 1],
                    recv_sem=recv_sems.at[t, k - 1], device_id=(px, py, c), device_id_type=MESH))
                recvs.append(pltpu.make_async_remote_copy(
                    src_ref=u_ref, dst_ref=r_ref.at[2 * px + py], send_sem=send_sems.at[t, k - 1],
                    recv_sem=recv_sems.at[t, k - 1], device_id=(px, py, c), device_id_type=MESH))
        return sends, recvs


def _pc_cargo(body, cargo, *, name, grid, in_specs, out_specs, out_shape, scratch_shapes=(), route=_Scatter):
    out_specs, out_shape = tuple(out_specs), tuple(out_shape)
    if not cargo:
        return lambda *args: (_pc(body, name=name, grid=grid, in_specs=list(in_specs), out_specs=out_specs,
                                  out_shape=out_shape, scratch_shapes=list(scratch_shapes),
                                  compiler_params=_cp(*["arbitrary"] * len(grid)))(*args), ())
    n_in, n_out, n_u, n_s = len(in_specs), len(out_specs), len(cargo), len(scratch_shapes)

    def wrapped(*refs):
        ins, u_refs = refs[:n_in], refs[n_in:n_in + n_u]
        outs = refs[n_in + n_u:n_in + n_u + n_out]
        r_refs = refs[n_in + n_u + n_out:n_in + 2 * n_u + n_out]
        scratch = refs[n_in + 2 * n_u + n_out:n_in + 2 * n_u + n_out + n_s]
        send_sems, recv_sems = refs[-2:]
        first = last = None
        for axis, extent in enumerate(grid):
            at_start, at_end = pl.program_id(axis) == 0, pl.program_id(axis) == extent - 1
            first = at_start if first is None else first & at_start
            last = at_end if last is None else last & at_end

        @pl.when(first)
        def _():
            sends, _ = route.copies(u_refs, r_refs, send_sems, recv_sems)
            for cp in sends:
                cp.start()

        body(*ins, *outs, *scratch)

        @pl.when(last)
        def _():
            sends, recvs = route.copies(u_refs, r_refs, send_sems, recv_sems)
            for cp in recvs:
                cp.wait_recv()
            for cp in sends:
                cp.wait_send()

    hbm = pl.BlockSpec(memory_space=pltpu.HBM)
    sems = pltpu.SemaphoreType.DMA((n_u, route.peers))
    call = _pc(wrapped, name=name, grid=grid, in_specs=list(in_specs) + [hbm] * n_u, out_specs=out_specs + (hbm,) * n_u,
               out_shape=out_shape + route.shapes(cargo), scratch_shapes=list(scratch_shapes) + [sems, sems],
               compiler_params=_cp(*["arbitrary"] * len(grid)))

    def run(*args):
        res = call(*args, *cargo)
        return tuple(res[:n_out]), tuple(res[n_out:])
    return run


def _swap_halves(name, bufs):
    n_u = len(bufs)

    def body(*refs):
        o_refs = refs[n_u:2 * n_u]
        send_sems, recv_sems = refs[2 * n_u:]
        x, y, c = lax.axis_index("x"), lax.axis_index("y"), lax.axis_index("c")

        def rows(u, core):
            h = bufs[u].shape[0] // 2
            return o_refs[u].at[pl.ds(core * h, h), :]

        sends = [pltpu.make_async_remote_copy(src_ref=rows(u, c), dst_ref=rows(u, c), send_sem=send_sems.at[u],
                                              recv_sem=recv_sems.at[u], device_id=(x, y, 1 - c), device_id_type=MESH)
                 for u in range(n_u)]
        recvs = [pltpu.make_async_remote_copy(src_ref=rows(u, c), dst_ref=rows(u, 1 - c), send_sem=send_sems.at[u],
                                              recv_sem=recv_sems.at[u], device_id=(x, y, 1 - c), device_id_type=MESH)
                 for u in range(n_u)]
        for cp in sends:
            cp.start()
        for cp in recvs:
            cp.wait_recv()
        for cp in sends:
            cp.wait_send()

    hbm = pl.BlockSpec(memory_space=pltpu.HBM)
    return _pc(
        body, name=name, out_shape=tuple(_sds(b.shape, b.dtype) for b in bufs), in_specs=[hbm] * n_u,
        out_specs=(hbm,) * n_u, input_output_aliases={u: u for u in range(n_u)},
        scratch_shapes=[pltpu.SemaphoreType.DMA((n_u,)), pltpu.SemaphoreType.DMA((n_u,))],
    )(*bufs)


def _sum8_into_half(name, unit, received, chip_core):
    _, h, n = received.shape
    tm = _div(h, 256, 16)
    per = h // tm

    def body(cc_ref, u_ref, p_ref, o_ref):
        acc = u_ref[0].astype(F32)
        for s in range(N_DEV - 1):
            acc = acc + p_ref[s].astype(F32)
        o_ref[...] = acc

    grid_spec = pltpu.PrefetchScalarGridSpec(
        num_scalar_prefetch=1, grid=(per,),
        in_specs=[pl.BlockSpec((1, tm, n), lambda i, cc_ref: (cc_ref[0], cc_ref[1] * per + i, 0)),
                  pl.BlockSpec((N_DEV - 1, tm, n), lambda i, cc_ref: (0, i, 0))],
        out_specs=pl.BlockSpec((tm, n), lambda i, cc_ref: (cc_ref[1] * per + i, 0)))
    return _pc(body, name=name, grid_spec=grid_spec, out_shape=_sds((2 * h, n), F32),
               compiler_params=_cp("arbitrary"))(chip_core, unit, received)


def _sum8(name, parts):
    _, m, n = parts.shape
    tm = _div(m, 256, SUBLANE)

    def body(p_ref, o_ref):
        acc = p_ref[0]
        for s in range(1, N_DEV):
            acc = acc + p_ref[s]
        o_ref[...] = acc

    return _pc(body, name=name, grid=(m // tm,), out_shape=_sds((m, n), F32),
               in_specs=[pl.BlockSpec((N_DEV, tm, n), lambda i: (0, i, 0))],
               out_specs=pl.BlockSpec((tm, n), lambda i: (i, 0)), compiler_params=_cp("parallel"))(parts)


def _pack_rows(arrays, dtype, row_mult):
    flat, spans, off = [], [], 0
    for a in arrays:
        flat.append(a.reshape(-1).astype(dtype))
        spans.append((off, a.shape))
        off += a.size
    quantum = row_mult * PACK_COLS
    total = -(-off // quantum) * quantum
    if total > off:
        flat.append(jnp.zeros((total - off,), dtype))
    return jnp.concatenate(flat).reshape(total // PACK_COLS, PACK_COLS), spans


def _size(shape):
    n = 1
    for s in shape:
        n *= s
    return n


def _join_chips(blocks, axis):
    return jnp.concatenate([blocks[j] for j in range(N_CHIP)], axis=axis)


def _cols_by_chip(g):
    k, n = g.shape
    return jnp.transpose(g.reshape(k, N_CHIP, n // N_CHIP), (1, 0, 2))


def _mm_nn(name, pairs, out_dtype, cargo=(), route=_Scatter, tm_cap=1024, tn_cap=1536):
    m = pairs[0][0].shape[0]
    nb, _, n4 = pairs[0][1][0].shape
    tm, tn = _div(m, tm_cap, 16), _div(n4, tn_cap, LANE)
    per = n4 // tn
    n_pairs = len(pairs)

    def body(*refs):
        o_ref = refs[-1]
        acc = jnp.dot(refs[0][...], refs[1][0], preferred_element_type=F32)
        for i in range(1, n_pairs):
            acc = acc + jnp.dot(refs[2 * i][...], refs[2 * i + 1][0], preferred_element_type=F32)
        o_ref[...] = acc.astype(o_ref.dtype)

    in_specs, args = [], []
    for a, (w, r) in pairs:
        k = a.shape[1]
        assert w.shape[0] == nb and w.shape[2] == n4 and w.shape[1] % k == 0
        in_specs += [pl.BlockSpec((tm, k), lambda j, i: (i, 0)),
                     pl.BlockSpec((1, k, tn), functools.partial(lambda j, i, r_: (j // per, r_, j % per), r_=r))]
        args += [a, w]
    if cargo:
        (out,), received = _pc_cargo(body, cargo, name=name, grid=(nb * per, m // tm), in_specs=in_specs, route=route,
                                     out_shape=[_sds((m, nb * n4), out_dtype)],
                                     out_specs=[pl.BlockSpec((tm, tn), lambda j, i: (i, j))])(*args)
        return out, received
    return _pc(body, name=name, grid=(nb * per, m // tm), out_shape=_sds((m, nb * n4), out_dtype), in_specs=in_specs,
               out_specs=pl.BlockSpec((tm, tn), lambda j, i: (i, j)), compiler_params=_cp("parallel", "parallel"))(*args)


def _mm_tn(name, x, ys, n_blocks=1, out_dtype=F32, cargo=(), tt_cap=1024):
    t, k = x.shape
    widths = [y.shape[1] for y in ys]
    n4 = sum(widths) // n_blocks
    common = n4
    for w in widths:
        common = _gcd(common, w)
    tk, tn, tt = _div(k, 1536, LANE), _div(common, 1536, LANE), _div(t, tt_cap, 16)
    per = n4 // tn
    starts, acc_w = [], 0
    for w in widths:
        starts.append(acc_w // tn)
        acc_w += w
    counts = [w // tn for w in widths]
    n_y = len(ys)

    def active(i, j):
        return (j >= starts[i]) & (j < starts[i] + counts[i])

    n_t = t // tt

    def body(*refs):
        x_ref, y_refs, o_ref, acc_ref = refs[0], refs[1:1 + n_y], refs[-2], refs[-1]
        j = pl.program_id(1)

        @pl.when(pl.program_id(2) == 0)
        def _():
            acc_ref[...] = jnp.zeros_like(acc_ref)

        for i in range(n_y):
            @pl.when(active(i, j))
            def _():
                acc_ref[...] += lax.dot_general(x_ref[...], y_refs[i][...], (((0,), (0,)), ((), ())),
                                                preferred_element_type=F32)

        @pl.when(pl.program_id(2) == n_t - 1)
        def _():
            o_ref[0] = acc_ref[...].astype(o_ref.dtype)

    def y_spec(i):
        def index(a, j, s):
            on = active(i, j)
            return jnp.where(on, s, 0), jnp.where(on, j - starts[i], 0)
        return pl.BlockSpec((tt, tn), index)

    (out,), received = _pc_cargo(
        body, cargo, name=name, grid=(k // tk, n_blocks * per, n_t), out_shape=[_sds((n_blocks, k, n4), out_dtype)],
        in_specs=[pl.BlockSpec((tt, tk), lambda a, j, s: (s, a))] + [y_spec(i) for i in range(n_y)],
        out_specs=[pl.BlockSpec((1, tk, tn), lambda a, j, s: (j // per, a, j % per))],
        scratch_shapes=[pltpu.VMEM((tk, tn), F32)])(x, *ys)
    return (out, received) if cargo else out


def _gcd(a, b):
    while b:
        a, b = b, a % b
    return a


def _w2(w):
    return (w[None], 0)


def _tok_spec(ts, d):
    return pl.BlockSpec((1, ts, d), lambda b, i: (b, i, 0))


def _seq_spec(d):
    return pl.BlockSpec((1, 1, d), lambda b, i: (b, 0, 0))


def _vec_spec(d):
    return pl.BlockSpec((1, d), lambda b, i: (0, 0))


def _ln_stats(z):
    mu = jnp.mean(z, axis=-1, keepdims=True)
    zc = z - mu
    var = jnp.mean(zc * zc, axis=-1, keepdims=True)
    rstd = lax.rsqrt(var + LN_EPS)
    return zc * rstd, rstd


def _modulate(name, x, sc, sh):
    b, s, d = x.shape
    ts = _div(s, 512, 16)

    def body(x_ref, sc_ref, sh_ref, u_ref):
        u_ref[0] = (x_ref[0] * (1.0 + sc_ref[0]) + sh_ref[0]).astype(BF16)

    return _pc(body, name=name, grid=(b, s // ts), out_shape=_sds(x.shape, BF16),
               in_specs=[_tok_spec(ts, d), _seq_spec(d), _seq_spec(d)], out_specs=_tok_spec(ts, d),
               compiler_params=_cp("parallel", "parallel"))(x, sc, sh)


def _ln_mod_fwd(name, alpha, x, y, g, lng, lnb, sc, sh):
    b, s, d = x.shape
    ts = _div(s, 512, 16)

    def body(x_ref, y_ref, g_ref, lng_ref, lnb_ref, sc_ref, sh_ref, z_ref, xn_ref, u_ref):
        z = alpha * x_ref[0] + (1.0 + g_ref[0]) * y_ref[0]
        xhat, _ = _ln_stats(z)
        xn = xhat * lng_ref[...] + lnb_ref[...]
        z_ref[0] = z
        xn_ref[0] = xn
        u_ref[0] = (xn * (1.0 + sc_ref[0]) + sh_ref[0]).astype(BF16)

    tok, seq, vec = _tok_spec(ts, d), _seq_spec(d), _vec_spec(d)
    return _pc(body, name=name, grid=(b, s // ts),
               out_shape=(_sds(x.shape, F32), _sds(x.shape, F32), _sds(x.shape, BF16)),
               in_specs=[tok, tok, seq, vec, vec, seq, seq], out_specs=(tok, tok, tok),
               compiler_params=_cp("parallel", "parallel"))(x, y, g, lng, lnb, sc, sh)


def _ln_loss_fwd(name, alpha, x, y, g, lng, lnb, target):
    b, s, d = x.shape
    ts = _div(s, 512, 16)

    def body(x_ref, y_ref, g_ref, lng_ref, lnb_ref, t_ref, z_ref, ct_ref, loss_ref):
        @pl.when((pl.program_id(0) == 0) & (pl.program_id(1) == 0))
        def _():
            loss_ref[...] = jnp.zeros_like(loss_ref)
        z = alpha * x_ref[0] + (1.0 + g_ref[0]) * y_ref[0]
        xhat, _ = _ln_stats(z)
        err = xhat * lng_ref[...] + lnb_ref[...] - t_ref[0]
        z_ref[0] = z
        ct_ref[0] = err / d
        part = 0.5 * jnp.sum(jnp.mean(err * err, axis=-1, keepdims=True))
        loss_ref[...] += jnp.full(loss_ref.shape, part, F32)

    tok, seq, vec = _tok_spec(ts, d), _seq_spec(d), _vec_spec(d)
    return _pc(body, name=name, grid=(b, s // ts),
               out_shape=(_sds(x.shape, F32), _sds(x.shape, F32), _sds((SUBLANE, LANE), F32)),
               in_specs=[tok, tok, seq, vec, vec, tok],
               out_specs=(tok, tok, pl.BlockSpec((SUBLANE, LANE), lambda b, i: (0, 0))),
               compiler_params=_cp("arbitrary", "arbitrary"))(x, y, g, lng, lnb, target)


def _sub_bwd(name, alpha, upstream, z, y, g, lng):
    b, s, d = z.shape
    ts = _div(s, 512, 16)
    last = len(upstream) == 1

    def body(*refs):
        if last:
            ct_ref, z_ref, y_ref, g_ref, lng_ref, dz_ref, dy_ref, dg_ref, dlng_ref, dlnb_ref = refs
        else:
            (dzn_ref, dun_ref, lnb_ref, scn_ref, z_ref, y_ref, g_ref, lng_ref,
             dz_ref, dy_ref, dg_ref, dlng_ref, dlnb_ref, dsc_ref, dsh_ref) = refs
        first_tile = pl.program_id(1) == 0

        @pl.when(first_tile & (pl.program_id(0) == 0))
        def _():
            dlng_ref[...] = jnp.zeros_like(dlng_ref)
            dlnb_ref[...] = jnp.zeros_like(dlnb_ref)

        @pl.when(first_tile)
        def _():
            dg_ref[...] = jnp.zeros_like(dg_ref)
            if not last:
                dsc_ref[...] = jnp.zeros_like(dsc_ref)
                dsh_ref[...] = jnp.zeros_like(dsh_ref)

        xhat, rstd = _ln_stats(z_ref[0])
        if last:
            ct = ct_ref[0]
        else:
            dun = dun_ref[0]
            ct = alpha * dzn_ref[0] + dun * (1.0 + scn_ref[0])
            xn = xhat * lng_ref[...] + lnb_ref[...]
            dsc_ref[0] += jnp.sum(dun * xn, axis=0, keepdims=True)
            dsh_ref[0] += jnp.sum(dun, axis=0, keepdims=True)
        dlng_ref[...] += jnp.sum(ct * xhat, axis=0, keepdims=True)
        dlnb_ref[...] += jnp.sum(ct, axis=0, keepdims=True)
        dxhat = ct * lng_ref[...]
        dz = rstd * (dxhat - jnp.mean(dxhat, axis=-1, keepdims=True)
                     - xhat * jnp.mean(dxhat * xhat, axis=-1, keepdims=True))
        dz_ref[0] = dz
        dy_ref[0] = ((1.0 + g_ref[0]) * dz).astype(BF16)
        dg_ref[0] += jnp.sum(dz * y_ref[0], axis=0, keepdims=True)

    tok, seq, vec = _tok_spec(ts, d), _seq_spec(d), _vec_spec(d)
    seq_out = _sds((b, 1, d), F32)
    out_shape = [_sds(z.shape, F32), _sds(z.shape, BF16), seq_out, _sds((1, d), F32), _sds((1, d), F32)]
    out_specs = [tok, tok, seq, vec, vec]
    if last:
        in_specs = [tok, tok, tok, seq, vec]
    else:
        in_specs = [tok, tok, vec, seq, tok, tok, seq, vec]
        out_shape += [seq_out, seq_out]
        out_specs += [seq, seq]
    return _pc(body, name=name, grid=(b, s // ts), out_shape=tuple(out_shape), in_specs=in_specs,
               out_specs=tuple(out_specs), compiler_params=_cp("arbitrary", "arbitrary"))(*upstream, z, y, g, lng)


def _input_bwd(name, alpha, dz, du, x, sc):
    b, s, d = x.shape
    ts = _div(s, 512, 16)

    def body(dz_ref, du_ref, x_ref, sc_ref, gx_ref, dsc_ref, dsh_ref):
        @pl.when(pl.program_id(1) == 0)
        def _():
            dsc_ref[...] = jnp.zeros_like(dsc_ref)
            dsh_ref[...] = jnp.zeros_like(dsh_ref)
        du_ = du_ref[0]
        gx_ref[0] = alpha * dz_ref[0] + du_ * (1.0 + sc_ref[0])
        dsc_ref[0] += jnp.sum(du_ * x_ref[0], axis=0, keepdims=True)
        dsh_ref[0] += jnp.sum(du_, axis=0, keepdims=True)

    tok, seq = _tok_spec(ts, d), _seq_spec(d)
    seq_out = _sds((b, 1, d), F32)
    return _pc(body, name=name, grid=(b, s // ts), out_shape=(_sds(x.shape, F32), seq_out, seq_out),
               in_specs=[tok, tok, tok, seq], out_specs=(tok, seq, seq),
               compiler_params=_cp("parallel", "arbitrary"))(dz, du, x, sc)


def _rows_iota(shape):
    return lax.broadcasted_iota(jnp.int32, shape, 0)


def _back(v, k):
    return pltpu.roll(v, k, axis=0)


def _ahead(v, k):
    return pltpu.roll(v, v.shape[0] - k, axis=0)


def _conv3(ext, w_ref):
    return w_ref[2:3, :] * ext + w_ref[1:2, :] * _back(ext, 1) + w_ref[0:1, :] * _back(ext, 2)


def _conv3_t(dh_ext, w_ref):
    return w_ref[2:3, :] * dh_ext + w_ref[1:2, :] * _ahead(dh_ext, 1) + w_ref[0:1, :] * _ahead(dh_ext, 2)


def _flag(cond):
    return jnp.where(cond, 1.0, 0.0).astype(F32)


def _sigmoid(v):
    return 1.0 / (1.0 + jnp.exp(-v))


def _halo_specs(ts, tc, halo, n_s, col):
    per = ts // halo
    tile = pl.BlockSpec((1, ts, tc), lambda b, i, j: (b, i, col(j)))
    prev = pl.BlockSpec((1, halo, tc), lambda b, i, j: (b, jnp.maximum(i * per - 1, 0), col(j)))
    nxt = pl.BlockSpec((1, halo, tc), lambda b, i, j: (b, jnp.minimum((i + 1) * per, n_s * per - 1), col(j)))
    return tile, prev, nxt


def _convglu_fwd(name, p, cw, cb, cargo=()):
    b, s, f2 = p.shape
    f = f2 // 2
    ts, tc = _div(s, CONV_ROWS, CONV_HALO), _div(f, 256, LANE)
    n_s, n_c = s // ts, f // tc

    def body(pv_ref, pvh_ref, pg_ref, pgh_ref, wv_ref, wg_ref, bv_ref, bg_ref, a_ref):
        keep = _flag(pl.program_id(1) > 0)

        def conv(t_ref, h_ref, w_ref, b_ref):
            ext = jnp.concatenate([h_ref[0] * keep, t_ref[0]], axis=0)
            return _conv3(ext, w_ref)[CONV_HALO:] + b_ref[...]

        val = conv(pv_ref, pvh_ref, wv_ref, bv_ref)
        gate = conv(pg_ref, pgh_ref, wg_ref, bg_ref)
        a_ref[0] = (gate * _sigmoid(gate) * val).astype(BF16)

    tv, hv, _ = _halo_specs(ts, tc, CONV_HALO, n_s, lambda j: j)
    tg, hg, _ = _halo_specs(ts, tc, CONV_HALO, n_s, lambda j: j + n_c)
    wv = pl.BlockSpec((3, tc), lambda b, i, j: (0, j))
    wg = pl.BlockSpec((3, tc), lambda b, i, j: (0, j + n_c))
    bv = pl.BlockSpec((1, tc), lambda b, i, j: (0, j))
    bg = pl.BlockSpec((1, tc), lambda b, i, j: (0, j + n_c))
    (act,), fetched = _pc_cargo(
        body, cargo, name=name, grid=(b, n_s, n_c), route=_Fetch, out_shape=[_sds((b, s, f), BF16)],
        in_specs=[tv, hv, tg, hg, wv, wg, bv, bg],
        out_specs=[pl.BlockSpec((1, ts, tc), lambda b, i, j: (b, i, j))])(p, p, p, p, cw, cw, cb, cb)
    return act, fetched


def _convglu_bwd(name, p, da, cw, cb, cargo=()):
    b, s, f2 = p.shape
    f = f2 // 2
    ts, tc = _div(s, CONV_ROWS, CONV_HALO), _div(f, 256, LANE)
    n_s, n_c = s // ts, f // tc

    def body(pv_ref, pvp_ref, pvn_ref, pg_ref, pgp_ref, pgn_ref, da_ref, dan_ref, wv_ref, wg_ref, bv_ref, bg_ref,
             dpv_ref, dpg_ref, dwv_ref, dwg_ref, dbv_ref, dbg_ref):
        bi, i = pl.program_id(1), pl.program_id(2)

        @pl.when((bi == 0) & (i == 0))
        def _():
            for r in (dwv_ref, dwg_ref, dbv_ref, dbg_ref):
                r[...] = jnp.zeros_like(r)

        keep_prev = _flag(i > 0)
        keep_next = _flag(i < n_s - 1)
        pv_ext = jnp.concatenate([pvp_ref[0] * keep_prev, pv_ref[0], pvn_ref[0]], axis=0)
        pg_ext = jnp.concatenate([pgp_ref[0] * keep_prev, pg_ref[0], pgn_ref[0]], axis=0)
        taps_v = (_back(pv_ext, 2), _back(pv_ext, 1), pv_ext)
        taps_g = (_back(pg_ext, 2), _back(pg_ext, 1), pg_ext)

        def conv(taps, w_ref, b_ref):
            return (w_ref[2:3, :] * taps[2] + w_ref[1:2, :] * taps[1] + w_ref[0:1, :] * taps[0])[CONV_HALO:] + b_ref[...]

        val, gate = conv(taps_v, wv_ref, bv_ref), conv(taps_g, wg_ref, bg_ref)
        da_ext = jnp.concatenate([da_ref[0], dan_ref[0] * keep_next], axis=0)
        sg = _sigmoid(gate)
        dval = da_ext * gate * sg
        dgate = da_ext * val * (sg * (1.0 + gate * (1.0 - sg)))
        dpv_ref[0] = _conv3_t(dval, wv_ref)[:ts].astype(BF16)
        dpg_ref[0] = _conv3_t(dgate, wg_ref)[:ts].astype(BF16)
        for dh, taps, dw_ref, db_ref in ((dval[:ts], taps_v, dwv_ref, dbv_ref), (dgate[:ts], taps_g, dwg_ref, dbg_ref)):
            db_ref[...] += jnp.sum(dh, axis=0, keepdims=True)
            for k in range(3):
                dw_ref[k:k + 1, :] += jnp.sum(dh * taps[k][CONV_HALO:CONV_HALO + ts], axis=0, keepdims=True)

    def specs(col):
        per = ts // CONV_HALO
        tile = pl.BlockSpec((1, ts, tc), lambda j, b, i: (b, i, col(j)))
        prev = pl.BlockSpec((1, CONV_HALO, tc), lambda j, b, i: (b, jnp.maximum(i * per - 1, 0), col(j)))
        nxt = pl.BlockSpec((1, CONV_HALO, tc), lambda j, b, i: (b, jnp.minimum((i + 1) * per, n_s * per - 1), col(j)))
        return tile, prev, nxt

    tv, pvp, pvn = specs(lambda j: j)
    tg, pgp, pgn = specs(lambda j: j + n_c)
    wv = pl.BlockSpec((3, tc), lambda j, b, i: (0, j))
    wg = pl.BlockSpec((3, tc), lambda j, b, i: (0, j + n_c))
    bv = pl.BlockSpec((1, tc), lambda j, b, i: (0, j))
    bg = pl.BlockSpec((1, tc), lambda j, b, i: (0, j + n_c))
    out_tile = pl.BlockSpec((1, ts, tc), lambda j, b, i: (b, i, j))
    acc3, acc1 = pl.BlockSpec((3, tc), lambda j, b, i: (0, j)), pl.BlockSpec((1, tc), lambda j, b, i: (0, j))
    (dpv, dpg, dwv, dwg, dbv, dbg), received = _pc_cargo(
        body, cargo, name=name, grid=(n_c, b, n_s),
        out_shape=(_sds((b, s, f), BF16), _sds((b, s, f), BF16), _sds((3, f), F32), _sds((3, f), F32),
                   _sds((1, f), F32), _sds((1, f), F32)),
        in_specs=[tv, pvp, pvn, tg, pgp, pgn, tv, pvn, wv, wg, bv, bg],
        out_specs=(out_tile, out_tile, acc3, acc3, acc1, acc1))(p, p, p, p, p, p, da, da, cw, cw, cb, cb)
    return dpv, dpg, jnp.concatenate([dwv, dwg], axis=1), jnp.concatenate([dbv, dbg], axis=1), received


def _shortconv_fwd(name, q, cw):
    b, s, d3 = q.shape
    d = d3 // 3
    ts, tc = _div(s, CONV_ROWS, CONV_HALO), _div(d, 256, LANE)
    n_s, n_c = s // ts, d // tc

    def body(gb_ref, gc_ref, gch_ref, h_ref, hh_ref, w_ref, r_ref):
        keep = _flag(pl.program_id(1) > 0)
        m_ext = jnp.concatenate([gch_ref[0] * hh_ref[0] * keep, gc_ref[0] * h_ref[0]], axis=0)
        r_ref[0] = (gb_ref[0] * _conv3(m_ext, w_ref)[CONV_HALO:]).astype(BF16)

    tb, _, _ = _halo_specs(ts, tc, CONV_HALO, n_s, lambda j: j)
    tcc, hc, _ = _halo_specs(ts, tc, CONV_HALO, n_s, lambda j: j + n_c)
    th, hh, _ = _halo_specs(ts, tc, CONV_HALO, n_s, lambda j: j + 2 * n_c)
    w = pl.BlockSpec((3, tc), lambda b, i, j: (0, j))
    return _pc(body, name=name, grid=(b, n_s, n_c), out_shape=_sds((b, s, d), BF16),
               in_specs=[tb, tcc, hc, th, hh, w], out_specs=pl.BlockSpec((1, ts, tc), lambda b, i, j: (b, i, j)),
               compiler_params=_cp("parallel", "parallel", "parallel"))(q, q, q, q, q, cw)


def _shortconv_bwd(name, q, dr, cw):
    b, s, d3 = q.shape
    d = d3 // 3
    ts, tc = _div(s, CONV_ROWS, CONV_HALO), _div(d, 256, LANE)
    n_s, n_c = s // ts, d // tc

    def body(gb_ref, gbn_ref, gc_ref, gcp_ref, h_ref, hp_ref, dr_ref, drn_ref, w_ref,
             dgb_ref, dgc_ref, dh_ref, dw_ref):
        bi, i = pl.program_id(1), pl.program_id(2)

        @pl.when((bi == 0) & (i == 0))
        def _():
            dw_ref[...] = jnp.zeros_like(dw_ref)

        keep_prev = _flag(i > 0)
        keep_next = _flag(i < n_s - 1)
        gc, h = gc_ref[0], h_ref[0]
        m_ext = jnp.concatenate([gcp_ref[0] * hp_ref[0] * keep_prev, gc * h], axis=0)
        cm = _conv3(m_ext, w_ref)[CONV_HALO:]
        dr_ = dr_ref[0]
        dgb_ref[0] = (dr_ * cm).astype(BF16)
        dcv_ext = jnp.concatenate([dr_ * gb_ref[0], drn_ref[0] * gbn_ref[0] * keep_next], axis=0)
        dm = _conv3_t(dcv_ext, w_ref)[:ts]
        dgc_ref[0] = (dm * h).astype(BF16)
        dh_ref[0] = (dm * gc).astype(BF16)
        dcv = dcv_ext[:ts]
        for k in range(3):
            shifted = m_ext if k == 2 else _back(m_ext, 2 - k)
            dw_ref[k:k + 1, :] += jnp.sum(dcv * shifted[CONV_HALO:], axis=0, keepdims=True)

    def specs(col):
        per = ts // CONV_HALO
        tile = pl.BlockSpec((1, ts, tc), lambda j, b, i: (b, i, col(j)))
        prev = pl.BlockSpec((1, CONV_HALO, tc), lambda j, b, i: (b, jnp.maximum(i * per - 1, 0), col(j)))
        nxt = pl.BlockSpec((1, CONV_HALO, tc), lambda j, b, i: (b, jnp.minimum((i + 1) * per, n_s * per - 1), col(j)))
        return tile, prev, nxt

    tb, _, nb = specs(lambda j: j)
    tcc, pc_, _ = specs(lambda j: j + n_c)
    th, ph, _ = specs(lambda j: j + 2 * n_c)
    w = pl.BlockSpec((3, tc), lambda j, b, i: (0, j))
    out_tile = pl.BlockSpec((1, ts, tc), lambda j, b, i: (b, i, j))
    o = _sds((b, s, d), BF16)
    return _pc(body, name=name, grid=(n_c, b, n_s), out_shape=(o, o, o, _sds((3, d), F32)),
               in_specs=[tb, nb, tcc, pc_, th, ph, tb, nb, w], out_specs=(out_tile, out_tile, out_tile, w),
               compiler_params=_cp("parallel", "arbitrary", "arbitrary"))(q, q, q, q, q, q, dr, dr, cw)


def _pick_window(group, cands):
    gid = jnp.full(cands[0].shape, group, jnp.int32)
    out = cands[-1]
    for k in range(len(cands) - 2, -1, -1):
        out = jnp.where(gid == k, cands[k], out)
    return out


def _window_sums(v, shift):
    s1 = v + shift(v, 1)
    s2 = s1 + shift(s1, 2)
    s3 = s2 + shift(s2, 4)
    s4 = s3 + shift(s3, 8)
    return [s1, s2, s3, s4]


def _pool_counts(group, first_row, n_rows, cols):
    t = _rows_iota((n_rows, cols)) + first_row
    window = _pick_window(group, [jnp.full((n_rows, cols), 2 << k, jnp.int32) for k in range(POOL_GROUPS)])
    return jnp.minimum(t + 1, window).astype(F32)


def _pool_fwd(name, x, sc, sh, w, scale):
    b, s, d = x.shape
    tc = d // POOL_GROUPS
    ts = _div(s, 512, POOL_HALO)
    n_s = s // ts

    def body(x_ref, xp_ref, sc_ref, sh_ref, w_ref, scale_ref, y_ref):
        i, grp = pl.program_id(1), pl.program_id(2)
        keep = _flag(i > 0)
        mod = 1.0 + sc_ref[0]
        u = x_ref[0] * mod + sh_ref[0]
        u_ext = jnp.concatenate([(xp_ref[0] * mod + sh_ref[0]) * keep, u], axis=0)
        summed = _pick_window(grp, _window_sums(u_ext, _back))[POOL_HALO:]
        pooled = summed / _pool_counts(grp, i * ts, ts, tc) - u
        y_ref[0] = jnp.dot(pooled.astype(BF16), w_ref[0], preferred_element_type=F32) * scale_ref[...]

    tile, prev, _ = _halo_specs(ts, tc, POOL_HALO, n_s, lambda j: j)
    seq = pl.BlockSpec((1, 1, tc), lambda b, i, j: (b, 0, j))
    return _pc(body, name=name, grid=(b, n_s, POOL_GROUPS), out_shape=_sds(x.shape, F32),
               in_specs=[tile, prev, seq, seq, pl.BlockSpec((1, tc, tc), lambda b, i, j: (j, 0, 0)),
                         pl.BlockSpec((1, tc), lambda b, i, j: (0, j))],
               out_specs=pl.BlockSpec((1, ts, tc), lambda b, i, j: (b, i, j)),
               compiler_params=_cp("parallel", "parallel", "parallel"))(x, x, sc, sh, w, scale)


def _pool_bwd(name, x, sc, sh, dy, w, w_t, scale, cargo=()):
    b, s, d = x.shape
    tc = d // POOL_GROUPS
    ts = _div(s, 512, POOL_HALO)
    n_s = s // ts

    def body(x_ref, xp_ref, sc_ref, sh_ref, dy_ref, dyn_ref, w_ref, wt_ref, scale_ref, du_ref, dw_ref, dscale_ref):
        grp, bi, i = pl.program_id(0), pl.program_id(1), pl.program_id(2)

        @pl.when((bi == 0) & (i == 0))
        def _():
            dw_ref[...] = jnp.zeros_like(dw_ref)
            dscale_ref[...] = jnp.zeros_like(dscale_ref)

        keep_prev = _flag(i > 0)
        keep_next = _flag(i < n_s - 1)
        mod = 1.0 + sc_ref[0]
        u = x_ref[0] * mod + sh_ref[0]
        u_ext = jnp.concatenate([(xp_ref[0] * mod + sh_ref[0]) * keep_prev, u], axis=0)
        summed = _pick_window(grp, _window_sums(u_ext, _back))[POOL_HALO:]
        pooled = (summed / _pool_counts(grp, i * ts, ts, tc) - u).astype(BF16)
        dy_ = dy_ref[0].astype(F32)
        ymat = jnp.dot(pooled, w_ref[0], preferred_element_type=F32)
        dscale_ref[...] += jnp.sum(dy_ * ymat, axis=0, keepdims=True)
        dys_ext = (jnp.concatenate([dy_, dyn_ref[0].astype(F32) * keep_next], axis=0) * scale_ref[...]).astype(BF16)
        dw_ref[0] += lax.dot_general(pooled, dys_ext[:ts], (((0,), (0,)), ((), ())), preferred_element_type=F32)
        dpooled = jnp.dot(dys_ext, wt_ref[0], preferred_element_type=F32)
        e = dpooled / _pool_counts(grp, i * ts, ts + POOL_HALO, tc)
        du_ref[0] = _pick_window(grp, _window_sums(e, _ahead))[:ts] - dpooled[:ts]

    per = ts // POOL_HALO
    tile = pl.BlockSpec((1, ts, tc), lambda j, b, i: (b, i, j))
    prev = pl.BlockSpec((1, POOL_HALO, tc), lambda j, b, i: (b, jnp.maximum(i * per - 1, 0), j))
    nxt = pl.BlockSpec((1, POOL_HALO, tc), lambda j, b, i: (b, jnp.minimum((i + 1) * per, n_s * per - 1), j))
    seq = pl.BlockSpec((1, 1, tc), lambda j, b, i: (b, 0, j))
    wsp = pl.BlockSpec((1, tc, tc), lambda j, b, i: (j, 0, 0))
    vec = pl.BlockSpec((1, tc), lambda j, b, i: (0, j))
    (du, dw, dscale), received = _pc_cargo(
        body, cargo, name=name, grid=(POOL_GROUPS, b, n_s),
        out_shape=(_sds(x.shape, F32), _sds((POOL_GROUPS, tc, tc), F32), _sds((1, d), F32)),
        in_specs=[tile, prev, seq, seq, tile, nxt, wsp, wsp, vec],
        out_specs=(tile, wsp, vec))(x, x, sc, sh, dy, dy, w, w_t, scale)
    return du, dw, dscale, received


def _rope_swap(v):
    lane = lax.broadcasted_iota(jnp.int32, v.shape, v.ndim - 1)
    lo, hi = QK_NOPE, QK_NOPE + QK_ROPE // 2
    from_above = pltpu.roll(v, HEAD_PAD - QK_ROPE // 2, axis=v.ndim - 1)
    from_below = pltpu.roll(v, QK_ROPE // 2, axis=v.ndim - 1)
    return jnp.where((lane >= lo) & (lane < hi), from_above,
                     jnp.where((lane >= hi) & (lane < hi + QK_ROPE // 2), from_below, 0.0))


def _rope(v, cos_t, sin_t):
    return v * cos_t + _rope_swap(v) * sin_t


def _rope_t(dv, cos_t, sin_t):
    return dv * cos_t + _rope_swap(dv * sin_t)


def _rms(v, g):
    r = lax.rsqrt(jnp.mean(v * v, axis=-1, keepdims=True) + RMS_EPS)
    return v * r, r


def _mla_norm_fwd(name, a, qn, kvn, cos_t, sin_t):
    b, s, wa = a.shape
    ql, kvl = qn.shape[1], kvn.shape[1]
    ts = _div(s, 512, 16)

    def body(aq_ref, akv_ref, ape_ref, qn_ref, kvn_ref, cos_ref, sin_ref, cq_ref, ckv_ref, kpe_ref):
        yq, _ = _rms(aq_ref[0], None)
        cq_ref[0] = (yq * qn_ref[...]).astype(BF16)
        ykv, _ = _rms(akv_ref[0], None)
        ckv_ref[0] = (ykv * kvn_ref[...]).astype(BF16)
        kpe_ref[0] = _rope(ape_ref[0], cos_ref[0], sin_ref[0])

    tok = lambda w, col: pl.BlockSpec((1, ts, w), lambda b, i: (b, i, col))
    return _pc(body, name=name, grid=(b, s // ts),
               out_shape=(_sds((b, s, ql), BF16), _sds((b, s, kvl), BF16), _sds((b, s, HEAD_PAD), F32)),
               in_specs=[tok(ql, 0), tok(kvl, ql // kvl), tok(HEAD_PAD, (ql + kvl) // HEAD_PAD), _vec_spec(ql),
                         _vec_spec(kvl), tok(HEAD_PAD, 0), tok(HEAD_PAD, 0)],
               out_specs=(tok(ql, 0), tok(kvl, 0), tok(HEAD_PAD, 0)),
               compiler_params=_cp("parallel", "parallel"))(a, a, a, qn, kvn, cos_t, sin_t)


def _mla_norm_bwd(name, a, dcq, dckv, dkpe, qn, kvn):
    b, s, wa = a.shape
    ql, kvl = qn.shape[1], kvn.shape[1]
    ts = _div(s, 512, 16)

    def body(a_ref, dcq_ref, dckv_ref, dkpe_ref, qn_ref, kvn_ref, da_ref, dqn_ref, dkvn_ref):
        @pl.when((pl.program_id(0) == 0) & (pl.program_id(1) == 0))
        def _():
            dqn_ref[...] = jnp.zeros_like(dqn_ref)
            dkvn_ref[...] = jnp.zeros_like(dkvn_ref)

        def one(v, dc, g_ref, dg_ref):
            yv, r = _rms(v, None)
            dg_ref[...] += jnp.sum(dc * yv, axis=0, keepdims=True)
            dyv = dc * g_ref[...]
            return r * (dyv - yv * jnp.mean(dyv * yv, axis=-1, keepdims=True))

        av = a_ref[0]
        da_ref[0, :, 0:ql] = one(av[:, 0:ql], dcq_ref[0], qn_ref, dqn_ref).astype(BF16)
        da_ref[0, :, ql:ql + kvl] = one(av[:, ql:ql + kvl], dckv_ref[0], kvn_ref, dkvn_ref).astype(BF16)
        da_ref[0, :, ql + kvl:] = dkpe_ref[0].astype(BF16)

    return _pc(body, name=name, grid=(b, s // ts),
               out_shape=(_sds(a.shape, BF16), _sds((1, ql), F32), _sds((1, kvl), F32)),
               in_specs=[_tok_spec(ts, wa), _tok_spec(ts, ql), _tok_spec(ts, kvl), _tok_spec(ts, HEAD_PAD),
                         _vec_spec(ql), _vec_spec(kvl)],
               out_specs=(_tok_spec(ts, wa), _vec_spec(ql), _vec_spec(kvl)),
               compiler_params=_cp("arbitrary", "arbitrary"))(a, dcq, dckv, dkpe, qn, kvn)


def _mla_prep_fwd(name, q_raw, k_raw, kpe, cos_t, sin_t, n_heads):
    b, s, wq = q_raw.shape
    ts = _div(s, 256, 16)

    def body(q_ref, k_ref, kpe_ref, cos_ref, sin_ref, qo_ref, ko_ref):
        cos_, sin_, kpe_ = cos_ref[0], sin_ref[0], kpe_ref[0]
        for h in range(n_heads):
            lanes = slice(h * HEAD_PAD, (h + 1) * HEAD_PAD)
            qo_ref[0, :, lanes] = _rope(q_ref[0, :, lanes], cos_, sin_).astype(BF16)
            ko_ref[0, :, lanes] = (k_ref[0, :, lanes] + kpe_).astype(BF16)

    wide = pl.BlockSpec((1, ts, wq), lambda b, i: (b, i, 0))
    tok = pl.BlockSpec((1, ts, HEAD_PAD), lambda b, i: (b, i, 0))
    o = _sds(q_raw.shape, BF16)
    return _pc(body, name=name, grid=(b, s // ts), out_shape=(o, o),
               in_specs=[wide, wide, tok, tok, tok], out_specs=(wide, wide),
               compiler_params=_cp("parallel", "parallel"))(q_raw, k_raw, kpe, cos_t, sin_t)


def _mla_prep_bwd(name, dq, dk, cos_t, sin_t, n_heads):
    b, s, wq = dq.shape
    ts = _div(s, 256, 16)

    def body(dq_ref, dk_ref, cos_ref, sin_ref, dqr_ref, dkr_ref, dkpe_ref):
        cos_, sin_ = cos_ref[0], sin_ref[0]
        dk_sum = None
        for h in range(n_heads):
            lanes = slice(h * HEAD_PAD, (h + 1) * HEAD_PAD)
            dqr_ref[0, :, lanes] = _rope_t(dq_ref[0, :, lanes], cos_, sin_).astype(BF16)
            dk_h = dk_ref[0, :, lanes]
            dkr_ref[0, :, lanes] = dk_h.astype(BF16)
            dk_sum = dk_h if dk_sum is None else dk_sum + dk_h
        dkpe_ref[0] = _rope_t(dk_sum, cos_, sin_)

    wide = pl.BlockSpec((1, ts, wq), lambda b, i: (b, i, 0))
    tok = pl.BlockSpec((1, ts, HEAD_PAD), lambda b, i: (b, i, 0))
    return _pc(body, name=name, grid=(b, s // ts),
               out_shape=(_sds(dq.shape, BF16), _sds(dq.shape, BF16), _sds((b, s, HEAD_PAD), F32)),
               in_specs=[wide, wide, tok, tok], out_specs=(wide, wide, tok),
               compiler_params=_cp("parallel", "parallel"))(dq, dk, cos_t, sin_t)


FLASH_TILE = 1024
LOG2_E = 1.4426950408889634


def _heads_per_step(n_heads):
    return 2 if n_heads % 2 == 0 else 1


def _causal_mask(i, j, tq, tk):
    rows = lax.broadcasted_iota(jnp.int32, (tq, tk), 0) + i * tq
    cols = lax.broadcasted_iota(jnp.int32, (tq, tk), 1) + j * tk
    return cols <= rows


def _nt(a, b):
    return lax.dot_general(a, b, (((1,), (1,)), ((), ())), preferred_element_type=F32)


def _tn(a, b):
    return lax.dot_general(a, b, (((0,), (0,)), ((), ())), preferred_element_type=F32)


def _flash_fwd(name, q, k, v, n_heads, sm_scale, cargo=()):
    b, s, _ = q.shape
    t, hp = _div(s, FLASH_TILE, LANE), _heads_per_step(n_heads)
    n, w = s // t, hp * HEAD_PAD
    neg = float(jnp.finfo(jnp.float32).min)
    c2 = sm_scale * LOG2_E

    def body(q_ref, k_ref, v_ref, o_ref, lse_ref, m_ref, l_ref, acc_ref):
        i, j = pl.program_id(2), pl.program_id(3)

        @pl.when(j == 0)
        def _():
            m_ref[...] = jnp.full(m_ref.shape, neg, F32)
            l_ref[...] = jnp.zeros_like(l_ref)
            acc_ref[...] = jnp.zeros_like(acc_ref)

        def block(on_diagonal):
            for hh in range(hp):
                ln = slice(hh * HEAD_PAD, (hh + 1) * HEAD_PAD)
                sc = _nt(q_ref[0, :, ln], k_ref[0, :, ln])
                if on_diagonal:
                    sc = jnp.where(_causal_mask(i, j, t, t), sc, neg)
                m_old = m_ref[hh]
                m_new = jnp.maximum(m_old, jnp.max(sc, axis=-1, keepdims=True))
                p = jnp.exp2((sc - m_new) * c2)
                corr = jnp.exp2((m_old - m_new) * c2)
                l_ref[hh] = corr * l_ref[hh] + jnp.sum(p, axis=-1, keepdims=True)
                acc_ref[:, ln] = corr * acc_ref[:, ln] + jnp.dot(p.astype(BF16), v_ref[0, :, ln],
                                                                 preferred_element_type=F32)
                m_ref[hh] = m_new

        pl.when(j < i)(functools.partial(block, False))
        pl.when(j == i)(functools.partial(block, True))

        @pl.when(j == n - 1)
        def _():
            for hh in range(hp):
                ln = slice(hh * HEAD_PAD, (hh + 1) * HEAD_PAD)
                o_ref[0, :, ln] = (acc_ref[:, ln] / l_ref[hh]).astype(BF16)
                lse_ref[0, :, ln] = jnp.broadcast_to(m_ref[hh] * sm_scale + jnp.log(l_ref[hh]), (t, HEAD_PAD))

    qs = pl.BlockSpec((1, t, w), lambda b, h, i, j: (b, i, h))
    ks = pl.BlockSpec((1, t, w), lambda b, h, i, j: (b, jnp.minimum(j, i), h))
    (o, lse), fetched = _pc_cargo(
        body, cargo, name=name, grid=(b, n_heads // hp, n, n), route=_Fetch,
        out_shape=(_sds(q.shape, BF16), _sds(q.shape, F32)), in_specs=[qs, ks, ks], out_specs=(qs, qs),
        scratch_shapes=[pltpu.VMEM((hp, t, 1), F32), pltpu.VMEM((hp, t, 1), F32), pltpu.VMEM((t, w), F32)])(q, k, v)
    return o, lse, fetched


def _flash_bwd(name, q, k, v, o, lse, do, n_heads, sm_scale):
    b, s, _ = q.shape
    t, hp = _div(s, FLASH_TILE, LANE), _heads_per_step(n_heads)
    n, w = s // t, hp * HEAD_PAD
    c2 = sm_scale * LOG2_E

    def body(q_ref, k_ref, v_ref, o_ref, lse_ref, do_ref, dq_hbm, dk_ref, dv_ref, dq_acc, dk_acc, dv_acc, dq_sem):
        bi, hi, j, i = pl.program_id(0), pl.program_id(1), pl.program_id(2), pl.program_id(3)

        @pl.when(i == 0)
        def _():
            dk_acc[...] = jnp.zeros_like(dk_acc)
            dv_acc[...] = jnp.zeros_like(dv_acc)

        rows = pl.ds(pl.multiple_of(i * t, t), t)

        def block(on_diagonal):
            for hh in range(hp):
                ln = slice(hh * HEAD_PAD, (hh + 1) * HEAD_PAD)
                do_ = do_ref[0, :, ln]
                delta = jnp.sum(do_.astype(F32) * o_ref[0, :, ln].astype(F32), axis=-1, keepdims=True)
                sc = _nt(q_ref[0, :, ln], k_ref[0, :, ln])
                p = jnp.exp2(sc * c2 - lse_ref[0, :, hh * HEAD_PAD:hh * HEAD_PAD + 1] * LOG2_E)
                if on_diagonal:
                    p = jnp.where(_causal_mask(i, j, t, t), p, 0.0)
                dv_acc[:, ln] += _tn(p.astype(BF16), do_)
                dp = _nt(do_, v_ref[0, :, ln])
                ds = (p * (dp - delta)).astype(BF16)
                dk_acc[:, ln] += _tn(ds, q_ref[0, :, ln])
                dq_part = jnp.dot(ds, k_ref[0, :, ln], preferred_element_type=F32)

                @pl.when(j == 0)
                def _():
                    dq_acc[rows, ln] = dq_part

                @pl.when(j > 0)
                def _():
                    dq_acc[rows, ln] += dq_part

        pl.when(i > j)(functools.partial(block, False))
        pl.when(i == j)(functools.partial(block, True))

        @pl.when(i == j)
        def _():
            dq_acc[rows, :] = dq_acc[rows, :] * sm_scale
            done = pltpu.make_async_copy(dq_acc.at[rows, :], dq_hbm.at[bi, rows, pl.ds(pl.multiple_of(hi * w, w), w)],
                                         dq_sem)
            done.start()
            done.wait()

        @pl.when(i == n - 1)
        def _():
            dk_ref[0] = dk_acc[...] * sm_scale
            dv_ref[0] = dv_acc[...].astype(BF16)

    qs = pl.BlockSpec((1, t, w), lambda b, h, j, i: (b, jnp.maximum(i, j), h))
    ks = pl.BlockSpec((1, t, w), lambda b, h, j, i: (b, j, h))
    return _pc(body, name=name, grid=(b, n_heads // hp, n, n),
               out_shape=(_sds(q.shape, F32), _sds(q.shape, F32), _sds(q.shape, BF16)),
               in_specs=[qs, ks, ks, qs, qs, qs], out_specs=(pl.BlockSpec(memory_space=pltpu.HBM), ks, ks),
               scratch_shapes=[pltpu.VMEM((s, w), F32), pltpu.VMEM((t, w), F32), pltpu.VMEM((t, w), F32),
                               pltpu.SemaphoreType.DMA],
               compiler_params=_cp("arbitrary", "arbitrary", "arbitrary", "arbitrary"))(q, k, v, o, lse, do)


def _mod_fwd(name, c_all, w, bias):
    depth, d, n = w.shape
    rows = c_all.shape[0]

    def body(c_ref, w_ref, b_ref, o_ref):
        cv = c_ref[...]
        cond = (cv * _sigmoid(cv)).astype(BF16)
        o_ref[0] = jnp.dot(cond, w_ref[0].astype(BF16), preferred_element_type=F32) + b_ref[0]

    return _pc(body, name=name, grid=(depth,), out_shape=_sds((depth, rows, n), F32),
               in_specs=[pl.BlockSpec((rows, d), lambda l: (0, 0)), pl.BlockSpec((1, d, n), lambda l: (l, 0, 0)),
                         pl.BlockSpec((1, 1, n), lambda l: (l, 0, 0))],
               out_specs=pl.BlockSpec((1, rows, n), lambda l: (l, 0, 0)), compiler_params=_cp("parallel"))(c_all, w, bias)


def _mod_bwd(name, c_all, dmod_cols, dmod_all):
    depth, rows, n = dmod_cols.shape
    d = c_all.shape[1]
    n_all = dmod_all.shape[2]
    tn = _div(n, 512, LANE)

    def body(c_ref, dm_ref, dma_ref, gw_ref, gb_ref):
        cv = c_ref[...]
        cond = (cv * _sigmoid(cv)).astype(BF16)
        gw_ref[0] = _tn(cond, dm_ref[0].astype(BF16))

        @pl.when(pl.program_id(1) == 0)
        def _():
            gb_ref[0] = jnp.sum(dma_ref[0], axis=0, keepdims=True)

    return _pc(body, name=name, grid=(depth, n // tn),
               out_shape=(_sds((depth, d, n), F32), _sds((depth, 1, n_all), F32)),
               in_specs=[pl.BlockSpec((rows, d), lambda l, j: (0, 0)), pl.BlockSpec((1, rows, tn), lambda l, j: (l, 0, j)),
                         pl.BlockSpec((1, rows, n_all), lambda l, j: (l, 0, 0))],
               out_specs=(pl.BlockSpec((1, d, tn), lambda l, j: (l, 0, j)), pl.BlockSpec((1, 1, n_all), lambda l, j: (l, 0, 0))),
               compiler_params=_cp("parallel", "arbitrary"))(c_all, dmod_cols, dmod_all)


def _adamw(name, w, g, m, v):
    shape = w.shape
    cols = shape[-1]
    rows = _size(shape) // cols
    tr = _div(rows, max(SUBLANE, (2 ** 19) // cols // SUBLANE * SUBLANE), SUBLANE)
    c1 = 1.0 - ADAM_B1 ** ADAM_STEP
    c2 = 1.0 - ADAM_B2 ** ADAM_STEP

    def body(w_ref, g_ref, m_ref, v_ref, d_ref, mo_ref, vo_ref):
        gv = g_ref[...]
        m_new = ADAM_B1 * m_ref[...] + (1.0 - ADAM_B1) * gv
        v_new = ADAM_B2 * v_ref[...] + (1.0 - ADAM_B2) * (gv * gv)
        m_hat = m_new / c1
        v_hat = v_new / c2
        d_ref[...] = -ADAM_LR * (m_hat / (jnp.sqrt(v_hat) + ADAM_EPS) + ADAM_WD * w_ref[...])
        mo_ref[...] = m_new
        vo_ref[...] = v_new

    spec = pl.BlockSpec((tr, cols), lambda i: (i, 0))
    o = _sds((rows, cols), F32)
    outs = _pc(body, name=name, grid=(rows // tr,), out_shape=(o, o, o), in_specs=[spec] * 4, out_specs=(spec,) * 3,
               compiler_params=_cp("parallel"))(*[a.reshape(rows, cols) for a in (w, g, m, v)])
    return tuple(a.reshape(shape) for a in outs)


def _rope_tables(positions):
    half = QK_ROPE // 2
    inv_freq = ROPE_THETA ** (-jnp.arange(0, QK_ROPE, 2, dtype=F32) / QK_ROPE)
    ang = positions.astype(F32)[..., None] * inv_freq
    cos, sin = jnp.cos(ang), jnp.sin(ang)
    lead = positions.shape
    ones = jnp.ones(lead + (QK_NOPE,), F32)
    tail_one = jnp.ones(lead + (HEAD_PAD - QK_NOPE - QK_ROPE,), F32)
    cos_t = jnp.concatenate([ones, cos, cos, tail_one], axis=-1)
    sin_t = jnp.concatenate([0 * ones, -sin, sin, 0 * tail_one], axis=-1)
    return cos_t, sin_t


def _pad_heads(w, n_heads, parts, axis):
    w = jnp.moveaxis(w, axis, -1)
    lead = w.shape[:-1]
    per = w.shape[-1] // n_heads
    w = w.reshape(lead + (n_heads, per))
    kept = jnp.concatenate([w[..., a:b_] for a, b_ in parts], axis=-1)
    pad = HEAD_PAD - kept.shape[-1]
    kept = jnp.concatenate([kept, jnp.zeros(lead + (n_heads, pad), w.dtype)], axis=-1)
    return jnp.moveaxis(kept.reshape(lead + (n_heads * HEAD_PAD,)), -1, axis)


def _unpad_heads(g, n_heads, width, axis):
    g = jnp.moveaxis(g, axis, -1)
    lead = g.shape[:-1]
    g = g.reshape(lead + (n_heads, HEAD_PAD))[..., :width]
    return g, lead


def kernel(x, c, positions, mod_w, mod_b, ln_g, ln_b, pool_w, pool_scale, mla_w_a, mla_q_norm, mla_w_uq, mla_kv_norm, mla_w_ukv, mla_w_o, sc_w_in, sc_conv, sc_w_out, ffn_w_up, ffn_conv, ffn_conv_b, ffn_w_down, loss_target, m_mod_w, m_mod_b, m_ln_g, m_ln_b, m_pool_w, m_pool_scale, m_mla_w_a, m_mla_q_norm, m_mla_w_uq, m_mla_kv_norm, m_mla_w_ukv, m_mla_w_o, m_sc_w_in, m_sc_conv, m_sc_w_out, m_ffn_w_up, m_ffn_conv, m_ffn_conv_b, m_ffn_w_down, v_mod_w, v_mod_b, v_ln_g, v_ln_b, v_pool_w, v_pool_scale, v_mla_w_a, v_mla_q_norm, v_mla_w_uq, v_mla_kv_norm, v_mla_w_ukv, v_mla_w_o, v_sc_w_in, v_sc_conv, v_sc_w_out, v_ffn_w_up, v_ffn_conv, v_ffn_conv_b, v_ffn_w_down):
    wts = dict(mod_w=mod_w, mod_b=mod_b, ln_g=ln_g, ln_b=ln_b, pool_w=pool_w, pool_scale=pool_scale, mla_w_a=mla_w_a,
               mla_q_norm=mla_q_norm, mla_w_uq=mla_w_uq, mla_kv_norm=mla_kv_norm, mla_w_ukv=mla_w_ukv, mla_w_o=mla_w_o,
               sc_w_in=sc_w_in, sc_conv=sc_conv, sc_w_out=sc_w_out, ffn_w_up=ffn_w_up, ffn_conv=ffn_conv,
               ffn_conv_b=ffn_conv_b, ffn_w_down=ffn_w_down)
    mom1 = dict(mod_w=m_mod_w, mod_b=m_mod_b, ln_g=m_ln_g, ln_b=m_ln_b, pool_w=m_pool_w, pool_scale=m_pool_scale,
                mla_w_a=m_mla_w_a, mla_q_norm=m_mla_q_norm, mla_w_uq=m_mla_w_uq, mla_kv_norm=m_mla_kv_norm,
                mla_w_ukv=m_mla_w_ukv, mla_w_o=m_mla_w_o, sc_w_in=m_sc_w_in, sc_conv=m_sc_conv, sc_w_out=m_sc_w_out,
                ffn_w_up=m_ffn_w_up, ffn_conv=m_ffn_conv, ffn_conv_b=m_ffn_conv_b, ffn_w_down=m_ffn_w_down)
    mom2 = dict(mod_w=v_mod_w, mod_b=v_mod_b, ln_g=v_ln_g, ln_b=v_ln_b, pool_w=v_pool_w, pool_scale=v_pool_scale,
                mla_w_a=v_mla_w_a, mla_q_norm=v_mla_q_norm, mla_w_uq=v_mla_w_uq, mla_kv_norm=v_mla_kv_norm,
                mla_w_ukv=v_mla_w_ukv, mla_w_o=v_mla_w_o, sc_w_in=v_sc_w_in, sc_conv=v_sc_conv, sc_w_out=v_sc_w_out,
                ffn_w_up=v_ffn_w_up, ffn_conv=v_ffn_conv, ffn_conv_b=v_ffn_conv_b, ffn_w_down=v_ffn_w_down)

    bsz, seq, d = x.shape
    depth = mod_b.shape[0]
    n_tok = bsz * seq
    n_heads = d // V_HEAD
    ql, kvl = mla_q_norm.shape[1], mla_kv_norm.shape[1]
    alpha = float((2 * depth) ** 0.25)
    sm_scale = float((QK_NOPE + QK_ROPE) ** -0.5)
    mx, my, mc = lax.axis_index("x"), lax.axis_index("y"), lax.axis_index("c")
    chip = 2 * mx + my
    dev = 2 * chip + mc

    small_names = list(SMALL_SHARDED)
    small_pack, small_spans = _pack_rows([c] + [wts[n] for n in small_names], F32, SUBLANE)
    rows_small = small_pack.shape[0]
    small_all = _all_gather8("gather_small_params", small_pack, True).reshape(N_DEV, rows_small * PACK_COLS)
    c_all = small_all[:, :c.size].reshape(N_DEV * bsz, d)
    per_chip = small_all[0::2]
    full = dict(wts)
    for n, (off, shape) in zip(small_names, small_spans[1:]):
        blocks = per_chip[:, off:off + _size(shape)].reshape((N_CHIP,) + tuple(shape))
        full[n] = _join_chips(blocks, SMALL_SHARDED[n])

    n_mod = mod_w.shape[2]
    bias_cols = lax.dynamic_slice_in_dim(mod_b, chip * n_mod, n_mod, axis=1)[:, None, :]
    mod_cols = _mod_fwd("mod_fwd", c_all, mod_w, bias_cols)
    half_rows = (N_DEV * bsz) // 2
    mod_half = lax.dynamic_slice_in_dim(mod_cols, mc * half_rows, half_rows, axis=1).reshape(depth * half_rows, n_mod)
    mod_all = _all_gather8("gather_mod", mod_half, True).reshape(N_CHIP, 2, depth, half_rows, n_mod)
    mod_all = jnp.transpose(mod_all, (2, 1, 3, 0, 4)).reshape(depth, N_DEV * bsz, N_CHIP * n_mod)
    mod_mine = lax.dynamic_slice_in_dim(mod_all, dev * bsz, bsz, axis=1)
    mods = [[mod_mine[l, :, k * d:(k + 1) * d][:, None, :] for k in range(6)] for l in range(depth)]

    big_names = list(BIG)
    f_hid = ffn_w_down.shape[1] * N_CHIP
    host_layer = 1
    assert depth > host_layer

    def layer_of(n, i):
        if n == 'pool_w':
            return 3 * i
        if n.startswith('mla_'):
            return 3 * i + 1
        if n.startswith('sc_'):
            return 3 * i + 2
        return i

    last_layer_of_group = (0, host_layer)

    def group_of(n, i):
        return sum(layer_of(n, i) > top for top in last_layer_of_group)

    n_groups = len(last_layer_of_group) + 1
    span = {n: [[i for i in range(wts[n].shape[0]) if group_of(n, i) == g] for g in range(n_groups)] for n in big_names}
    members = [[n for n in big_names if span[n][g]] for g in range(n_groups)]

    def rows2d(a):
        return a.astype(BF16).reshape(-1, a.shape[-1])

    def layouts(bc):
        out = {}
        for n in ('pool_w', 'mla_w_a', 'mla_w_uq', 'mla_w_ukv', 'mla_w_o', 'sc_w_out'):
            if n in bc:
                out[n] = jnp.concatenate([bc[n][j] for j in range(N_CHIP)], axis=BIG[n])
        if 'ffn_w_up' in bc:
            nl = bc['ffn_w_up'].shape[1]
            out['up_cols'] = bc['ffn_w_up'].reshape(N_CHIP, nl * d, -1)
            out['up_rows'] = jnp.transpose(bc['ffn_w_up'], (1, 0, 3, 2)).reshape(1, nl * 2 * f_hid, d)
            out['down_rows'] = jnp.transpose(bc['ffn_w_down'], (1, 0, 2, 3)).reshape(1, nl * f_hid, d)
            out['down_t'] = jnp.transpose(bc['ffn_w_down'], (1, 3, 0, 2)).reshape(1, nl * d, f_hid)
        if 'sc_w_in' in bc:
            ns = bc['sc_w_in'].shape[1]
            out['in_cols'] = bc['sc_w_in'].reshape(N_CHIP, ns * d, -1)
            out['in_rows'] = jnp.transpose(bc['sc_w_in'], (1, 0, 3, 2)).reshape(1, ns * 3 * d, d)
        return out

    shards = [{n: rows2d(wts[n][span[n][g][0]:span[n][g][-1] + 1]) for n in members[g]} for g in range(n_groups)]
    lay = [None] * n_groups

    def by_chip(g, n, blocks):
        return blocks.reshape((N_CHIP, len(span[n][g])) + wts[n].shape[1:])

    def fetched_group(g, got):
        lay[g] = layouts({n: by_chip(g, n, lax.dynamic_update_index_in_dim(blocks, shards[g][n], chip, 0))
                          for n, blocks in got.items()})

    gathered = _gather_weights("gather_weights", [shards[0][n] for n in members[0]])
    lay[0] = layouts({n: by_chip(0, n, blocks) for n, blocks in zip(members[0], gathered)})

    def grp(n, i):
        g = group_of(n, i)
        return lay[g], i - span[n][g][0]

    nope_rope = [(0, QK_NOPE + QK_ROPE)]
    cos_t, sin_t = _rope_tables(positions)

    def t2(a):
        return a.reshape(n_tok, a.shape[-1])

    def t3(a):
        return a.reshape(bsz, seq, a.shape[-1])

    saved = []
    xin = x
    u = _modulate("modulate_in", x, mods[0][1], mods[0][0])
    loss_acc = None
    for l in range(depth):
        sh1, sc1, g1, sh2, sc2, g2 = mods[l]
        kind, j = l % 3, l // 3
        st = dict(x=xin)
        if kind == 0:
            grp_l, jj = grp('pool_w', j)
            w = grp_l['pool_w'][jj]
            st.update(w=w, w_t=jnp.swapaxes(w, 1, 2), scale=full['pool_scale'][j][None, :])
            y = _pool_fwd(f"pool_fwd_{l}", xin, sc1, sh1, st['w'], st['scale'])
        elif kind == 1:
            grp_l, jj = grp('mla_w_a', j)
            wa, wuq, wukv = grp_l['mla_w_a'][jj], grp_l['mla_w_uq'][jj], grp_l['mla_w_ukv'][jj]
            zeros = jnp.zeros((d, QK_NOPE), BF16)
            w_a = jnp.concatenate([wa[:, :ql + kvl], zeros, wa[:, ql + kvl:], zeros[:, :HEAD_PAD - QK_NOPE - QK_ROPE]], axis=1)
            w_uq = _pad_heads(wuq, n_heads, nope_rope, 1)
            w_k = _pad_heads(wukv, n_heads, [(0, QK_NOPE)], 1)
            w_v = _pad_heads(wukv, n_heads, [(QK_NOPE, QK_NOPE + V_HEAD)], 1)
            w_o = _pad_heads(grp_l['mla_w_o'][jj], n_heads, [(0, V_HEAD)], 0)
            qn, kvn = mla_q_norm[j][None, :], mla_kv_norm[j][None, :]
            a = t3(_mm_nn(f"mla_a_{l}", [(t2(u), _w2(w_a))], F32))
            cq, ckv, kpe = _mla_norm_fwd(f"mla_norm_fwd_{l}", a, qn, kvn, cos_t, sin_t)
            q_raw = t3(_mm_nn(f"mla_q_{l}", [(t2(cq), _w2(w_uq))], F32))
            k_raw = t3(_mm_nn(f"mla_k_{l}", [(t2(ckv), _w2(w_k))], F32))
            vh = t3(_mm_nn(f"mla_v_{l}", [(t2(ckv), _w2(w_v))], BF16))
            qh, kh = _mla_prep_fwd(f"mla_prep_fwd_{l}", q_raw, k_raw, kpe, cos_t, sin_t, n_heads)
            o, lse, fetched = _flash_fwd(f"flash_fwd_{l}", qh, kh, vh, n_heads, sm_scale,
                                         cargo=[shards[2][n] for n in members[2]] if l == host_layer else ())
            if l == host_layer:
                fetched_group(2, dict(zip(members[2], fetched)))
            y = t3(_mm_nn(f"mla_o_{l}", [(t2(o), _w2(w_o))], F32))
            st.update(u=u, w_a=w_a, w_uq=w_uq, w_k=w_k, w_v=w_v, w_o=w_o, qn=qn, kvn=kvn, a=a, cq=cq, ckv=ckv,
                      qh=qh, kh=kh, vh=vh, o=o, lse=lse)
        else:
            grp_l, jj = grp('sc_w_in', j)
            w_out, cw = grp_l['sc_w_out'][jj], full['sc_conv'][j]
            q = t3(_mm_nn(f"sc_in_{l}", [(t2(u), (grp_l['in_cols'], jj))], F32))
            r = _shortconv_fwd(f"shortconv_fwd_{l}", q, cw)
            y = t3(_mm_nn(f"sc_out_{l}", [(t2(r), _w2(w_out))], F32))
            st.update(u=u, w_out=w_out, cw=cw, q=q, r=r, in_rows=(grp_l['in_rows'], jj))
        lng, lnb = full['ln_g'][l], full['ln_b'][l]
        z1, xmid, u2 = _ln_mod_fwd(f"ln_mod_a_{l}", alpha, xin, y, g1, lng[0:1], lnb[0:1], sc2, sh2)
        cwf, cbf = full['ffn_conv'][l], ffn_conv_b[l][None, :]
        ffn_w, ll = grp('ffn_w_up', l)
        ffn_names = ('ffn_w_up', 'ffn_w_down')
        ride_mm = [n for n in members[1] if n not in ffn_names] if l == 0 else []
        ride_conv = [n for n in members[1] if n == 'ffn_w_up'] if l == 0 else []
        ride_down = [n for n in members[1] if n == 'ffn_w_down'] if l == 0 else []
        p = _mm_nn(f"ffn_up_{l}", [(t2(u2), (ffn_w['up_cols'], ll))], F32, cargo=[shards[1][n] for n in ride_mm],
                   route=_Fetch)
        got_mm = ()
        if ride_mm:
            p, got_mm = p
        p = t3(p)
        act, got_conv = _convglu_fwd(f"convglu_fwd_{l}", p, cwf, cbf, cargo=[shards[1][n] for n in ride_conv])
        y2 = _mm_nn(f"ffn_down_{l}", [(t2(act), (ffn_w['down_rows'], ll))], F32, cargo=[shards[1][n] for n in ride_down],
                    route=_Fetch)
        got_down = ()
        if ride_down:
            y2, got_down = y2
        y2 = t3(y2)
        if l == 0:
            fetched_group(1, {**dict(zip(ride_mm, got_mm)), **dict(zip(ride_conv, got_conv)),
                              **dict(zip(ride_down, got_down))})
        st.update(y1=y, z1=z1, xmid=xmid, u2=u2, p=p, act=act, y2=y2, cwf=cwf, cbf=cbf, lng=lng, lnb=lnb,
                  ffn_w=ffn_w, ll=ll)
        if l + 1 < depth:
            nsh1, nsc1 = mods[l + 1][0], mods[l + 1][1]
            z2, xin, u = _ln_mod_fwd(f"ln_mod_b_{l}", alpha, xmid, y2, g2, lng[1:2], lnb[1:2], nsc1, nsh1)
        else:
            z2, ct, loss_acc = _ln_loss_fwd("ln_loss", alpha, xmid, y2, g2, lng[1:2], lnb[1:2], loss_target)
        st.update(z2=z2)
        saved.append(st)
    loss = lax.psum(loss_acc[0, 0], ("x", "y", "c"))

    grads = {}
    dmods = [[None] * 6 for _ in range(depth)]
    g_ln_g = [[None, None] for _ in range(depth)]
    g_ln_b = [[None, None] for _ in range(depth)]
    stack = {n: [None] * wts[n].shape[0] for n in ('pool_scale', 'mla_q_norm', 'mla_kv_norm', 'sc_conv', 'ffn_conv',
                                                    'ffn_conv_b')}
    units = {n: [None] * wts[n].shape[0] for n in big_names}
    cargo_a, cargo_b, received = [], [], {}

    def landed(items, got):
        for (n, i, _), r in zip(items, got):
            received[(n, i)] = r

    upstream = (ct,)
    for l in reversed(range(depth)):
        st = saved[l]
        sh1, sc1, g1, sh2, sc2, g2 = mods[l]
        kind, j = l % 3, l // 3
        if len(upstream) > 1:
            upstream = (upstream[0], upstream[1], st['lnb'][1:2], upstream[2])
        res = _sub_bwd(f"sub_bwd_b_{l}", alpha, upstream, st['z2'], st['y2'], g2, st['lng'][1:2])
        dz2, dy2, dmods[l][5], g_ln_g[l][1], g_ln_b[l][1] = res[:5]
        if l + 1 < depth:
            dmods[l + 1][1], dmods[l + 1][0] = res[5], res[6]
        dy2f = t2(dy2)
        ffn_w, ll = st['ffn_w'], st['ll']
        da = t3(_mm_nn(f"ffn_down_bwd_{l}", [(dy2f, (ffn_w['down_t'], ll))], F32))
        units['ffn_w_down'][l] = _mm_tn(f"ffn_down_dw_{l}", t2(st['act']), [dy2f],
                                        out_dtype=BF16).reshape(N_CHIP, f_hid // N_CHIP, d)
        dpv, dpg, dcw, dcb, got = _convglu_bwd(f"convglu_bwd_{l}", st['p'], da, st['cwf'], st['cbf'],
                                               cargo=[u for _, _, u in cargo_a])
        landed(cargo_a, got)
        stack['ffn_conv'][l], stack['ffn_conv_b'][l] = dcw, dcb[0]
        down_unit = [('ffn_w_down', l, units['ffn_w_down'][l])]
        du2, got = _mm_nn(f"ffn_up_bwd_{l}", [(t2(dpv), (ffn_w['up_rows'], 2 * ll)),
                                              (t2(dpg), (ffn_w['up_rows'], 2 * ll + 1))], F32,
                          cargo=[units['ffn_w_down'][l]])
        landed(down_unit, got)
        du2 = t3(du2)
        res = _mm_tn(f"ffn_up_dw_{l}", t2(st['u2']), [t2(dpv), t2(dpg)], N_CHIP, out_dtype=BF16,
                     cargo=[u for _, _, u in cargo_b])
        if cargo_b:
            landed(cargo_b, res[1])
            res = res[0]
        units['ffn_w_up'][l] = res
        res = _sub_bwd(f"sub_bwd_a_{l}", alpha, (dz2, du2, st['lnb'][0:1], sc2), st['z1'], st['y1'], g1, st['lng'][0:1])
        dz1, dy1, dmods[l][2], g_ln_g[l][0], g_ln_b[l][0], dmods[l][4], dmods[l][3] = res
        dy1f = t2(dy1)
        if kind == 0:
            up_unit = [('ffn_w_up', l, units['ffn_w_up'][l])] if l == 0 else []
            du1, dw, dscale, got = _pool_bwd(f"pool_bwd_{l}", st['x'], sc1, sh1, dy1, st['w'], st['w_t'], st['scale'],
                                             cargo=[u for _, _, u in up_unit])
            landed(up_unit, got)
            stack['pool_scale'][j] = dscale[0]
            grp = dw.shape[1] // N_CHIP
            units['pool_w'][j] = jnp.transpose(dw.reshape(POOL_GROUPS, N_CHIP, grp, dw.shape[2]),
                                               (1, 0, 2, 3)).reshape(N_CHIP, POOL_GROUPS * grp, dw.shape[2])
        elif kind == 1:
            do = t3(_mm_nn(f"mla_o_bwd_{l}", [(dy1f, _w2(jnp.swapaxes(st['w_o'], 0, 1)))], BF16))
            gwo, _ = _unpad_heads(_mm_tn(f"mla_o_dw_{l}", t2(st['o']), [dy1f])[0], n_heads, V_HEAD, 0)
            units['mla_w_o'][j] = jnp.moveaxis(gwo.reshape(d, n_heads * V_HEAD), -1, 0).reshape(N_CHIP, -1, d)
            fa = (st['qh'], st['kh'], st['vh'], st['o'], st['lse'], do, n_heads, sm_scale)
            dq, dk, dv = _flash_bwd(f"flash_bwd_{l}", *fa)
            dq_raw, dk_raw, dkpe = _mla_prep_bwd(f"mla_prep_bwd_{l}", dq, dk, cos_t, sin_t, n_heads)
            dq_raw, dk_raw, dv_raw = t2(dq_raw), t2(dk_raw), t2(dv)
            dcq = t3(_mm_nn(f"mla_q_bwd_{l}", [(dq_raw, _w2(jnp.swapaxes(st['w_uq'], 0, 1)))], F32))
            dckv = t3(_mm_nn(f"mla_kv_bwd_{l}", [(dk_raw, _w2(jnp.swapaxes(st['w_k'], 0, 1))),
                                                   (dv_raw, _w2(jnp.swapaxes(st['w_v'], 0, 1)))], F32))
            gq, _ = _unpad_heads(_mm_tn(f"mla_q_dw_{l}", t2(st['cq']), [dq_raw])[0], n_heads, QK_NOPE + QK_ROPE, 1)
            units['mla_w_uq'][j] = _cols_by_chip(gq.reshape(ql, n_heads * (QK_NOPE + QK_ROPE)))
            gkv = _mm_tn(f"mla_kv_dw_{l}", t2(st['ckv']), [dk_raw, dv_raw])[0]
            gk, _ = _unpad_heads(gkv[:, :n_heads * HEAD_PAD], n_heads, QK_NOPE, 1)
            gv, _ = _unpad_heads(gkv[:, n_heads * HEAD_PAD:], n_heads, V_HEAD, 1)
            units['mla_w_ukv'][j] = _cols_by_chip(
                jnp.concatenate([gk, gv], axis=-1).reshape(kvl, n_heads * (QK_NOPE + V_HEAD)))
            da_, dqn, dkvn = _mla_norm_bwd(f"mla_norm_bwd_{l}", st['a'], dcq, dckv, dkpe, st['qn'], st['kvn'])
            stack['mla_q_norm'][j], stack['mla_kv_norm'][j] = dqn[0], dkvn[0]
            du1 = t3(_mm_nn(f"mla_a_bwd_{l}", [(t2(da_), _w2(jnp.swapaxes(st['w_a'], 0, 1)))], F32))
            gwa = _mm_tn(f"mla_a_dw_{l}", t2(st['u']), [t2(da_)])[0]
            units['mla_w_a'][j] = _cols_by_chip(jnp.concatenate(
                [gwa[:, :ql + kvl], gwa[:, ql + kvl + QK_NOPE:ql + kvl + QK_NOPE + QK_ROPE]], axis=1))
        else:
            dr = t3(_mm_nn(f"sc_out_bwd_{l}", [(dy1f, _w2(jnp.swapaxes(st['w_out'], 0, 1)))], F32))
            units['sc_w_out'][j] = _mm_tn(f"sc_out_dw_{l}", t2(st['r']), [dy1f], out_dtype=BF16).reshape(N_CHIP, -1, d)
            dgb, dgc, dh, dcw = _shortconv_bwd(f"shortconv_bwd_{l}", st['q'], dr, st['cw'])
            stack['sc_conv'][j] = dcw
            parts = [t2(dgb), t2(dgc), t2(dh)]
            in_rows, jj = st['in_rows']
            du1 = t3(_mm_nn(f"sc_in_bwd_{l}", [(parts[k], (in_rows, 3 * jj + k)) for k in range(3)], F32))
            units['sc_w_in'][j] = _cols_by_chip(jnp.concatenate(
                [_mm_tn(f"sc_in_dw_{k}_{l}", t2(st['u']), [parts[k]])[0] for k in range(3)], axis=1))
        upstream = (dz1, du1, sc1)
        mixer = {0: ['pool_w'], 1: ['mla_w_a', 'mla_w_uq', 'mla_w_ukv', 'mla_w_o'], 2: ['sc_w_in', 'sc_w_out']}[kind]
        for n in mixer:
            units[n][j] = units[n][j].astype(BF16)
        cargo_a = [('ffn_w_up', l, units['ffn_w_up'][l])] if l > 0 else []
        cargo_b = [(n, j, units[n][j]) for n in mixer]
    grad_x, dmods[0][1], dmods[0][0] = _input_bwd("input_bwd", alpha, upstream[0], upstream[1], x, mods[0][1])

    for n, parts in stack.items():
        grads[n] = jnp.stack(parts)
    grads['ln_g'] = jnp.stack([jnp.concatenate(r, axis=0) for r in g_ln_g])
    grads['ln_b'] = jnp.stack([jnp.concatenate(r, axis=0) for r in g_ln_b])
    dmod_mine = jnp.stack([jnp.concatenate([t[:, 0, :] for t in dmods[l]], axis=-1) for l in range(depth)])

    small_grad_names = small_names + ['mla_q_norm', 'mla_kv_norm', 'ffn_conv_b']
    sg_pack, sg_spans = _pack_rows([dmod_mine] + [grads[n] for n in small_grad_names], F32, SUBLANE)
    rows_sg = sg_pack.shape[0]
    sg_all = _all_gather8("gather_small_grads", sg_pack, True).reshape(N_DEV, rows_sg, PACK_COLS)
    dmod_all = sg_all.reshape(N_DEV, -1)[:, :dmod_mine.size].reshape(N_DEV, depth, bsz, 6 * d)
    dmod_all = jnp.transpose(dmod_all, (1, 0, 2, 3)).reshape(depth, N_DEV * bsz, 6 * d)
    sg_sum = _sum8("sum_small_grads", sg_all).reshape(-1)
    for n, (off, shape) in zip(small_grad_names, sg_spans[1:]):
        g_full = sg_sum[off:off + _size(shape)].reshape(shape)
        if n in SMALL_SHARDED:
            ax = SMALL_SHARDED[n]
            width = shape[ax] // N_CHIP
            g_full = lax.dynamic_slice_in_dim(g_full, chip * width, width, axis=ax)
        grads[n] = g_full
    dmod_cols = lax.dynamic_slice_in_dim(dmod_all, chip * n_mod, n_mod, axis=2)
    grads['mod_w'], gb = _mod_bwd("mod_bwd", c_all, dmod_cols, dmod_all)
    grads['mod_b'] = gb[:, 0, :]

    keys = [(n, i) for n in big_names for i in range(len(units[n]))]
    last = [(n, i, units[n][i]) for n, i in keys if (n, i) not in received]
    landed(last, _scatter_grads("scatter_big_grads", [u for _, _, u in last]))
    chip_core = jnp.stack([chip, mc]).astype(jnp.int32)
    bufs = _swap_halves("swap_big_grad_halves",
                        [_sum8_into_half(f"sum_big_grads_{n}_{i}", units[n][i], received[(n, i)], chip_core)
                         for n, i in keys])
    for n in big_names:
        grads[n] = jnp.stack([b for (m, _), b in zip(keys, bufs) if m == n]).reshape(wts[n].shape)

    deltas, new_m, new_v = {}, {}, {}
    for n in WEIGHTS:
        deltas[n], new_m[n], new_v[n] = _adamw(f"adamw_{n}", wts[n], grads[n], mom1[n], mom2[n])
    return (loss, grad_x, *[grads[n] for n in WEIGHTS], *[deltas[n] for n in WEIGHTS],
            *[new_m[n] for n in WEIGHTS], *[new_v[n] for n in WEIGHTS])
```

```python
import functools

import jax
import jax.numpy as jnp
from jax import lax
from jax.experimental import pallas as pl
from jax.experimental.pallas import tpu as pltpu

F32 = jnp.float32
BF16 = jnp.bfloat16
MESH = pl.DeviceIdType.MESH

N_DEV = 8
N_CHIP = 4
LANE = 128
SUBLANE = 8
VMEM_LIMIT_BYTES = 56 * 2 ** 20
PACK_COLS = 1024

LN_EPS = 1e-5
RMS_EPS = 1e-6
QK_NOPE, QK_ROPE, V_HEAD = 64, 32, 64
ROPE_THETA = 10000.0
HEAD_PAD = 128
POOL_GROUPS = 4
POOL_HALO = 16
CONV_HALO = 8
CONV_ROWS = 1024
ADAM_LR, ADAM_B1, ADAM_B2, ADAM_EPS, ADAM_WD, ADAM_STEP = 0.001, 0.9, 0.999, 1e-08, 0.01, 10

WEIGHTS = ['mod_w', 'mod_b', 'ln_g', 'ln_b', 'pool_w', 'pool_scale', 'mla_w_a', 'mla_q_norm', 'mla_w_uq',
           'mla_kv_norm', 'mla_w_ukv', 'mla_w_o', 'sc_w_in', 'sc_conv', 'sc_w_out', 'ffn_w_up', 'ffn_conv',
           'ffn_conv_b', 'ffn_w_down']
BIG = {'pool_w': 2, 'mla_w_a': 2, 'mla_w_uq': 2, 'mla_w_ukv': 2, 'mla_w_o': 1, 'sc_w_in': 2, 'sc_w_out': 1,
       'ffn_w_up': 2, 'ffn_w_down': 1}
SMALL_SHARDED = {'ln_g': 2, 'ln_b': 2, 'pool_scale': 1, 'sc_conv': 2, 'ffn_conv': 2}
REPLICATED = ['mod_b', 'mla_q_norm', 'mla_kv_norm', 'ffn_conv_b']


def _pc(body, **kw):
    return pl.pallas_call(body, **kw)


def _cp(*sem):
    return pltpu.CompilerParams(dimension_semantics=sem, vmem_limit_bytes=VMEM_LIMIT_BYTES)


def _div(n, cap, mult):
    best = None
    for d in range(mult, min(n, cap) + 1, mult):
        if n % d == 0:
            best = d
    return best if best is not None else n


def _sds(shape, dtype):
    return jax.ShapeDtypeStruct(tuple(shape), dtype)


def _flip(v, bit):
    return 1 - v if bit else v


def _all_gather8(name, x_shard, in_vmem):
    m_per, n = x_shard.shape
    space = pltpu.VMEM if in_vmem else pltpu.HBM

    def body(x_ref, out_ref, send_sems, recv_sems, local_sem):
        x, y, c = lax.axis_index("x"), lax.axis_index("y"), lax.axis_index("c")
        me, sibling = (x, y, c), (x, y, 1 - c)
        chips = [(1 - x, y), (x, 1 - y), (1 - x, 1 - y)]

        def rows(px, py, pc_):
            return out_ref.at[pl.ds((4 * px + 2 * py + pc_) * m_per, m_per), :]

        def copy(k, block, to, src=None):
            return pltpu.make_async_remote_copy(
                src_ref=rows(*block) if src is None else src, dst_ref=rows(*block),
                send_sem=send_sems.at[k], recv_sem=recv_sems.at[k], device_id=to, device_id_type=MESH)

        mine = pltpu.make_async_copy(x_ref, rows(*me), local_sem)
        mine.start()
        first = [copy(0, me, sibling, src=x_ref)]
        first += [copy(1 + j, me, (*chip, c), src=x_ref) for j, chip in enumerate(chips)]
        for cp in first:
            cp.start()
        passed = [copy(4 + j, (*chip, c), sibling) for j, chip in enumerate(chips)]
        for j, chip in enumerate(chips):
            copy(1 + j, (*chip, c), me).wait_recv()
            passed[j].start()
        copy(0, sibling, me).wait_recv()
        for j, chip in enumerate(chips):
            copy(4 + j, (*chip, 1 - c), me).wait_recv()
        for cp in first + passed:
            cp.wait_send()
        mine.wait()

    return _pc(
        body, name=name, out_shape=_sds((N_DEV * m_per, n), x_shard.dtype),
        in_specs=[pl.BlockSpec(memory_space=space)], out_specs=pl.BlockSpec(memory_space=space),
        scratch_shapes=[pltpu.SemaphoreType.DMA((7,)), pltpu.SemaphoreType.DMA((7,)), pltpu.SemaphoreType.DMA],
        compiler_params=pltpu.CompilerParams(vmem_limit_bytes=VMEM_LIMIT_BYTES),
    )(x_shard)


def _gather_weights(name, shards):
    n_t = len(shards)
    halves = [s.shape[0] // 2 for s in shards]

    def body(*refs):
        x_refs, o_refs = refs[:n_t], refs[n_t:2 * n_t]
        send_sems, recv_sems, local_sems = refs[2 * n_t:]
        x, y, c = lax.axis_index("x"), lax.axis_index("y"), lax.axis_index("c")
        me, sibling = (x, y, c), (x, y, 1 - c)
        chips = [(1 - x, y), (x, 1 - y), (1 - x, 1 - y)]

        def slot(t, px, py, pc_):
            return o_refs[t].at[4 * px + 2 * py + pc_]

        def my_rows(t):
            return x_refs[t].at[pl.ds(c * halves[t], halves[t]), :]

        def copy(t, k, block, to, src=None):
            return pltpu.make_async_remote_copy(
                src_ref=slot(t, *block) if src is None else src, dst_ref=slot(t, *block),
                send_sem=send_sems.at[t, k], recv_sem=recv_sems.at[t, k], device_id=to, device_id_type=MESH)

        local = [pltpu.make_async_copy(my_rows(t), slot(t, *me), local_sems.at[t]) for t in range(n_t)]
        for cp in local:
            cp.start()
        first = []
        for t in range(n_t):
            first += [copy(t, 1 + j, me, (*chip, c), src=my_rows(t)) for j, chip in enumerate(chips)]
            first.append(copy(t, 0, me, sibling, src=my_rows(t)))
        for cp in first:
            cp.start()
        passed = []
        for j, chip in enumerate(chips):
            for t in range(n_t):
                copy(t, 1 + j, (*chip, c), me).wait_recv()
                passed.append(copy(t, 4 + j, (*chip, c), sibling))
                passed[-1].start()
        for t in range(n_t):
            copy(t, 0, sibling, me).wait_recv()
        for j, chip in enumerate(chips):
            for t in range(n_t):
                copy(t, 4 + j, (*chip, 1 - c), me).wait_recv()
        for cp in first + passed:
            cp.wait_send()
        for cp in local:
            cp.wait()

    hbm = pl.BlockSpec(memory_space=pltpu.HBM)
    return _pc(
        body, name=name, out_shape=tuple(_sds((N_DEV, h, s.shape[1]), s.dtype) for h, s in zip(halves, shards)),
        in_specs=[hbm] * n_t, out_specs=(hbm,) * n_t,
        scratch_shapes=[pltpu.SemaphoreType.DMA((n_t, 7)), pltpu.SemaphoreType.DMA((n_t, 7)),
                        pltpu.SemaphoreType.DMA((n_t,))],
    )(*shards)


def _scatter_copies(u_refs, r_refs, send_sems, recv_sems):
    x, y, c = lax.axis_index("x"), lax.axis_index("y"), lax.axis_index("c")
    copies = []
    for k in range(1, N_DEV):
        px, py, pcc = _flip(x, (k >> 2) & 1), _flip(y, (k >> 1) & 1), _flip(c, k & 1)
        for t, (u_ref, r_ref) in enumerate(zip(u_refs, r_refs)):
            h = u_ref.shape[1] // 2
            copies.append(pltpu.make_async_remote_copy(
                src_ref=u_ref.at[2 * px + py, pl.ds(pcc * h, h), :], dst_ref=r_ref.at[k - 1],
                send_sem=send_sems.at[t, k - 1], recv_sem=recv_sems.at[t, k - 1],
                device_id=(px, py, pcc), device_id_type=MESH))
    return copies


def _scatter_shapes(units):
    return tuple(_sds((N_DEV - 1, u.shape[1] // 2, u.shape[2]), u.dtype) for u in units)


def _scatter_grads(name, units):
    n_u = len(units)

    def body(*refs):
        copies = _scatter_copies(refs[:n_u], refs[n_u:2 * n_u], refs[2 * n_u], refs[2 * n_u + 1])
        for cp in copies:
            cp.start()
        for cp in copies:
            cp.wait()

    hbm = pl.BlockSpec(memory_space=pltpu.HBM)
    return _pc(body, name=name, out_shape=_scatter_shapes(units), in_specs=[hbm] * n_u, out_specs=(hbm,) * n_u,
               scratch_shapes=[pltpu.SemaphoreType.DMA((n_u, 7)), pltpu.SemaphoreType.DMA((n_u, 7))])(*units)


class _Scatter:
    peers = N_DEV - 1
    shapes = staticmethod(_scatter_shapes)

    @staticmethod
    def copies(u_refs, r_refs, send_sems, recv_sems):
        both = _scatter_copies(u_refs, r_refs, send_sems, recv_sems)
        return both, both


class _Fetch:
    peers = N_CHIP - 1

    @staticmethod
    def shapes(units):
        return tuple(_sds((N_CHIP,) + u.shape, u.dtype) for u in units)

    @staticmethod
    def copies(u_refs, r_refs, send_sems, recv_sems):
        x, y, c = lax.axis_index("x"), lax.axis_index("y"), lax.axis_index("c")
        sends, recvs = [], []
        for k in range(1, N_CHIP):
            px, py = _flip(x, (k >> 1) & 1), _flip(y, k & 1)
            for t, (u_ref, r_ref) in enumerate(zip(u_refs, r_refs)):
                sends.append(pltpu.make_async_remote_copy(
                    src_ref=u_ref, dst_ref=r_ref.at[2 * x + y], send_sem=send_sems.at[t, k - 1],
                    recv_sem=recv_sems.at[t, k - 1], device_id=(px, py, c), device_id_type=MESH))
                recvs.append(pltpu.make_async_remote_copy(
                    src_ref=u_ref, dst_ref=r_ref.at[2 * px + py], send_sem=send_sems.at[t, k - 1],
                    recv_sem=recv_sems.at[t, k - 1], device_id=(px, py, c), device_id_type=MESH))
        return sends, recvs


def _pc_cargo(body, cargo, *, name, grid, in_specs, out_specs, out_shape, scratch_shapes=(), route=_Scatter):
    out_specs, out_shape = tuple(out_specs), tuple(out_shape)
    if not cargo:
        return lambda *args: (_pc(body, name=name, grid=grid, in_specs=list(in_specs), out_specs=out_specs,
                                  out_shape=out_shape, scratch_shapes=list(scratch_shapes),
                                  compiler_params=_cp(*["arbitrary"] * len(grid)))(*args), ())
    n_in, n_out, n_u, n_s = len(in_specs), len(out_specs), len(cargo), len(scratch_shapes)

    def wrapped(*refs):
        ins, u_refs = refs[:n_in], refs[n_in:n_in + n_u]
        outs = refs[n_in + n_u:n_in + n_u + n_out]
        r_refs = refs[n_in + n_u + n_out:n_in + 2 * n_u + n_out]
        scratch = refs[n_in + 2 * n_u + n_out:n_in + 2 * n_u + n_out + n_s]
        send_sems, recv_sems = refs[-2:]
        first = last = None
        for axis, extent in enumerate(grid):
            at_start, at_end = pl.program_id(axis) == 0, pl.program_id(axis) == extent - 1
            first = at_start if first is None else first & at_start
            last = at_end if last is None else last & at_end

        @pl.when(first)
        def _():
            sends, _ = route.copies(u_refs, r_refs, send_sems, recv_sems)
            for cp in sends:
                cp.start()

        body(*ins, *outs, *scratch)

        @pl.when(last)
        def _():
            sends, recvs = route.copies(u_refs, r_refs, send_sems, recv_sems)
            for cp in recvs:
                cp.wait_recv()
            for cp in sends:
                cp.wait_send()

    hbm = pl.BlockSpec(memory_space=pltpu.HBM)
    sems = pltpu.SemaphoreType.DMA((n_u, route.peers))
    call = _pc(wrapped, name=name, grid=grid, in_specs=list(in_specs) + [hbm] * n_u, out_specs=out_specs + (hbm,) * n_u,
               out_shape=out_shape + route.shapes(cargo), scratch_shapes=list(scratch_shapes) + [sems, sems],
               compiler_params=_cp(*["arbitrary"] * len(grid)))

    def run(*args):
        res = call(*args, *cargo)
        return tuple(res[:n_out]), tuple(res[n_out:])
    return run


def _swap_halves(name, bufs):
    n_u = len(bufs)

    def body(*refs):
        o_refs = refs[n_u:2 * n_u]
        send_sems, recv_sems = refs[2 * n_u:]
        x, y, c = lax.axis_index("x"), lax.axis_index("y"), lax.axis_index("c")

        def rows(u, core):
            h = bufs[u].shape[0] // 2
            return o_refs[u].at[pl.ds(core * h, h), :]

        sends = [pltpu.make_async_remote_copy(src_ref=rows(u, c), dst_ref=rows(u, c), send_sem=send_sems.at[u],
                                              recv_sem=recv_sems.at[u], device_id=(x, y, 1 - c), device_id_type=MESH)
                 for u in range(n_u)]
        recvs = [pltpu.make_async_remote_copy(src_ref=rows(u, c), dst_ref=rows(u, 1 - c), send_sem=send_sems.at[u],
                                              recv_sem=recv_sems.at[u], device_id=(x, y, 1 - c), device_id_type=MESH)
                 for u in range(n_u)]
        for cp in sends:
            cp.start()
        for cp in recvs:
            cp.wait_recv()
        for cp in sends:
            cp.wait_send()

    hbm = pl.BlockSpec(memory_space=pltpu.HBM)
    return _pc(
        body, name=name, out_shape=tuple(_sds(b.shape, b.dtype) for b in bufs), in_specs=[hbm] * n_u,
        out_specs=(hbm,) * n_u, input_output_aliases={u: u for u in range(n_u)},
        scratch_shapes=[pltpu.SemaphoreType.DMA((n_u,)), pltpu.SemaphoreType.DMA((n_u,))],
    )(*bufs)


def _sum8_into_half(name, unit, received, chip_core):
    _, h, n = received.shape
    tm = _div(h, 256, 16)
    per = h // tm

    def body(cc_ref, u_ref, p_ref, o_ref):
        acc = u_ref[0].astype(F32)
        for s in range(N_DEV - 1):
            acc = acc + p_ref[s].astype(F32)
        o_ref[...] = acc

    grid_spec = pltpu.PrefetchScalarGridSpec(
        num_scalar_prefetch=1, grid=(per,),
        in_specs=[pl.BlockSpec((1, tm, n), lambda i, cc_ref: (cc_ref[0], cc_ref[1] * per + i, 0)),
                  pl.BlockSpec((N_DEV - 1, tm, n), lambda i, cc_ref: (0, i, 0))],
        out_specs=pl.BlockSpec((tm, n), lambda i, cc_ref: (cc_ref[1] * per + i, 0)))
    return _pc(body, name=name, grid_spec=grid_spec, out_shape=_sds((2 * h, n), F32),
               compiler_params=_cp("arbitrary"))(chip_core, unit, received)


def _sum8(name, parts):
    _, m, n = parts.shape
    tm = _div(m, 256, SUBLANE)

    def body(p_ref, o_ref):
        acc = p_ref[0]
        for s in range(1, N_DEV):
            acc = acc + p_ref[s]
        o_ref[...] = acc

    return _pc(body, name=name, grid=(m // tm,), out_shape=_sds((m, n), F32),
               in_specs=[pl.BlockSpec((N_DEV, tm, n), lambda i: (0, i, 0))],
               out_specs=pl.BlockSpec((tm, n), lambda i: (i, 0)), compiler_params=_cp("parallel"))(parts)


def _pack_rows(arrays, dtype, row_mult):
    flat, spans, off = [], [], 0
    for a in arrays:
        flat.append(a.reshape(-1).astype(dtype))
        spans.append((off, a.shape))
        off += a.size
    quantum = row_mult * PACK_COLS
    total = -(-off // quantum) * quantum
    if total > off:
        flat.append(jnp.zeros((total - off,), dtype))
    return jnp.concatenate(flat).reshape(total // PACK_COLS, PACK_COLS), spans


def _size(shape):
    n = 1
    for s in shape:
        n *= s
    return n


def _join_chips(blocks, axis):
    return jnp.concatenate([blocks[j] for j in range(N_CHIP)], axis=axis)


def _cols_by_chip(g):
    k, n = g.shape
    return jnp.transpose(g.reshape(k, N_CHIP, n // N_CHIP), (1, 0, 2))


def _mm_nn(name, pairs, out_dtype, cargo=(), route=_Scatter, tm_cap=1024, tn_cap=1536):
    m = pairs[0][0].shape[0]
    nb, _, n4 = pairs[0][1][0].shape
    tm, tn = _div(m, tm_cap, 16), _div(n4, tn_cap, LANE)
    per = n4 // tn
    n_pairs = len(pairs)

    def body(*refs):
        o_ref = refs[-1]
        acc = jnp.dot(refs[0][...], refs[1][0], preferred_element_type=F32)
        for i in range(1, n_pairs):
            acc = acc + jnp.dot(refs[2 * i][...], refs[2 * i + 1][0], preferred_element_type=F32)
        o_ref[...] = acc.astype(o_ref.dtype)

    in_specs, args = [], []
    for a, (w, r) in pairs:
        k = a.shape[1]
        assert w.shape[0] == nb and w.shape[2] == n4 and w.shape[1] % k == 0
        in_specs += [pl.BlockSpec((tm, k), lambda j, i: (i, 0)),
                     pl.BlockSpec((1, k, tn), functools.partial(lambda j, i, r_: (j // per, r_, j % per), r_=r))]
        args += [a, w]
    if cargo:
        (out,), received = _pc_cargo(body, cargo, name=name, grid=(nb * per, m // tm), in_specs=in_specs, route=route,
                                     out_shape=[_sds((m, nb * n4), out_dtype)],
                                     out_specs=[pl.BlockSpec((tm, tn), lambda j, i: (i, j))])(*args)
        return out, received
    return _pc(body, name=name, grid=(nb * per, m // tm), out_shape=_sds((m, nb * n4), out_dtype), in_specs=in_specs,
               out_specs=pl.BlockSpec((tm, tn), lambda j, i: (i, j)), compiler_params=_cp("parallel", "parallel"))(*args)


def _mm_tn(name, x, ys, n_blocks=1, out_dtype=F32, cargo=(), tt_cap=1024):
    t, k = x.shape
    widths = [y.shape[1] for y in ys]
    n4 = sum(widths) // n_blocks
    common = n4
    for w in widths:
        common = _gcd(common, w)
    tk, tn, tt = _div(k, 1536, LANE), _div(common, 1536, LANE), _div(t, tt_cap, 16)
    per = n4 // tn
    starts, acc_w = [], 0
    for w in widths:
        starts.append(acc_w // tn)
        acc_w += w
    counts = [w // tn for w in widths]
    n_y = len(ys)

    def active(i, j):
        return (j >= starts[i]) & (j < starts[i] + counts[i])

    n_t = t // tt

    def body(*refs):
        x_ref, y_refs, o_ref, acc_ref = refs[0], refs[1:1 + n_y], refs[-2], refs[-1]
        j = pl.program_id(1)

        @pl.when(pl.program_id(2) == 0)
        def _():
            acc_ref[...] = jnp.zeros_like(acc_ref)

        for i in range(n_y):
            @pl.when(active(i, j))
            def _():
                acc_ref[...] += lax.dot_general(x_ref[...], y_refs[i][...], (((0,), (0,)), ((), ())),
                                                preferred_element_type=F32)

        @pl.when(pl.program_id(2) == n_t - 1)
        def _():
            o_ref[0] = acc_ref[...].astype(o_ref.dtype)

    def y_spec(i):
        def index(a, j, s):
            on = active(i, j)
            return jnp.where(on, s, 0), jnp.where(on, j - starts[i], 0)
        return pl.BlockSpec((tt, tn), index)

    (out,), received = _pc_cargo(
        body, cargo, name=name, grid=(k // tk, n_blocks * per, n_t), out_shape=[_sds((n_blocks, k, n4), out_dtype)],
        in_specs=[pl.BlockSpec((tt, tk), lambda a, j, s: (s, a))] + [y_spec(i) for i in range(n_y)],
        out_specs=[pl.BlockSpec((1, tk, tn), lambda a, j, s: (j // per, a, j % per))],
        scratch_shapes=[pltpu.VMEM((tk, tn), F32)])(x, *ys)
    return (out, received) if cargo else out


def _gcd(a, b):
    while b:
        a, b = b, a % b
    return a


def _w2(w):
    return (w[None], 0)


def _tok_spec(ts, d):
    return pl.BlockSpec((1, ts, d), lambda b, i: (b, i, 0))


def _seq_spec(d):
    return pl.BlockSpec((1, 1, d), lambda b, i: (b, 0, 0))


def _vec_spec(d):
    return pl.BlockSpec((1, d), lambda b, i: (0, 0))


def _ln_stats(z):
    mu = jnp.mean(z, axis=-1, keepdims=True)
    zc = z - mu
    var = jnp.mean(zc * zc, axis=-1, keepdims=True)
    rstd = lax.rsqrt(var + LN_EPS)
    return zc * rstd, rstd


def _modulate(name, x, sc, sh):
    b, s, d = x.shape
    ts = _div(s, 512, 16)

    def body(x_ref, sc_ref, sh_ref, u_ref):
        u_ref[0] = (x_ref[0] * (1.0 + sc_ref[0]) + sh_ref[0]).astype(BF16)

    return _pc(body, name=name, grid=(b, s // ts), out_shape=_sds(x.shape, BF16),
               in_specs=[_tok_spec(ts, d), _seq_spec(d), _seq_spec(d)], out_specs=_tok_spec(ts, d),
               compiler_params=_cp("parallel", "parallel"))(x, sc, sh)


def _ln_mod_fwd(name, alpha, x, y, g, lng, lnb, sc, sh):
    b, s, d = x.shape
    ts = _div(s, 512, 16)

    def body(x_ref, y_ref, g_ref, lng_ref, lnb_ref, sc_ref, sh_ref, z_ref, xn_ref, u_ref):
        z = alpha * x_ref[0] + (1.0 + g_ref[0]) * y_ref[0]
        xhat, _ = _ln_stats(z)
        xn = xhat * lng_ref[...] + lnb_ref[...]
        z_ref[0] = z
        xn_ref[0] = xn
        u_ref[0] = (xn * (1.0 + sc_ref[0]) + sh_ref[0]).astype(BF16)

    tok, seq, vec = _tok_spec(ts, d), _seq_spec(d), _vec_spec(d)
    return _pc(body, name=name, grid=(b, s // ts),
               out_shape=(_sds(x.shape, F32), _sds(x.shape, F32), _sds(x.shape, BF16)),
               in_specs=[tok, tok, seq, vec, vec, seq, seq], out_specs=(tok, tok, tok),
               compiler_params=_cp("parallel", "parallel"))(x, y, g, lng, lnb, sc, sh)


def _ln_loss_fwd(name, alpha, x, y, g, lng, lnb, target):
    b, s, d = x.shape
    ts = _div(s, 512, 16)

    def body(x_ref, y_ref, g_ref, lng_ref, lnb_ref, t_ref, z_ref, ct_ref, loss_ref):
        @pl.when((pl.program_id(0) == 0) & (pl.program_id(1) == 0))
        def _():
            loss_ref[...] = jnp.zeros_like(loss_ref)
        z = alpha * x_ref[0] + (1.0 + g_ref[0]) * y_ref[0]
        xhat, _ = _ln_stats(z)
        err = xhat * lng_ref[...] + lnb_ref[...] - t_ref[0]
        z_ref[0] = z
        ct_ref[0] = err / d
        part = 0.5 * jnp.sum(jnp.mean(err * err, axis=-1, keepdims=True))
        loss_ref[...] += jnp.full(loss_ref.shape, part, F32)

    tok, seq, vec = _tok_spec(ts, d), _seq_spec(d), _vec_spec(d)
    return _pc(body, name=name, grid=(b, s // ts),
               out_shape=(_sds(x.shape, F32), _sds(x.shape, F32), _sds((SUBLANE, LANE), F32)),
               in_specs=[tok, tok, seq, vec, vec, tok],
               out_specs=(tok, tok, pl.BlockSpec((SUBLANE, LANE), lambda b, i: (0, 0))),
               compiler_params=_cp("arbitrary", "arbitrary"))(x, y, g, lng, lnb, target)


def _sub_bwd(name, alpha, upstream, z, y, g, lng):
    b, s, d = z.shape
    ts = _div(s, 512, 16)
    last = len(upstream) == 1

    def body(*refs):
        if last:
            ct_ref, z_ref, y_ref, g_ref, lng_ref, dz_ref, dy_ref, dg_ref, dlng_ref, dlnb_ref = refs
        else:
            (dzn_ref, dun_ref, lnb_ref, scn_ref, z_ref, y_ref, g_ref, lng_ref,
             dz_ref, dy_ref, dg_ref, dlng_ref, dlnb_ref, dsc_ref, dsh_ref) = refs
        first_tile = pl.program_id(1) == 0

        @pl.when(first_tile & (pl.program_id(0) == 0))
        def _():
            dlng_ref[...] = jnp.zeros_like(dlng_ref)
            dlnb_ref[...] = jnp.zeros_like(dlnb_ref)

        @pl.when(first_tile)
        def _():
            dg_ref[...] = jnp.zeros_like(dg_ref)
            if not last:
                dsc_ref[...] = jnp.zeros_like(dsc_ref)
                dsh_ref[...] = jnp.zeros_like(dsh_ref)

        xhat, rstd = _ln_stats(z_ref[0])
        if last:
            ct = ct_ref[0]
        else:
            dun = dun_ref[0]
            ct = alpha * dzn_ref[0] + dun * (1.0 + scn_ref[0])
            xn = xhat * lng_ref[...] + lnb_ref[...]
            dsc_ref[0] += jnp.sum(dun * xn, axis=0, keepdims=True)
            dsh_ref[0] += jnp.sum(dun, axis=0, keepdims=True)
        dlng_ref[...] += jnp.sum(ct * xhat, axis=0, keepdims=True)
        dlnb_ref[...] += jnp.sum(ct, axis=0, keepdims=True)
        dxhat = ct * lng_ref[...]
        dz = rstd * (dxhat - jnp.mean(dxhat, axis=-1, keepdims=True)
                     - xhat * jnp.mean(dxhat * xhat, axis=-1, keepdims=True))
        dz_ref[0] = dz
        dy_ref[0] = ((1.0 + g_ref[0]) * dz).astype(BF16)
        dg_ref[0] += jnp.sum(dz * y_ref[0], axis=0, keepdims=True)

    tok, seq, vec = _tok_spec(ts, d), _seq_spec(d), _vec_spec(d)
    seq_out = _sds((b, 1, d), F32)
    out_shape = [_sds(z.shape, F32), _sds(z.shape, BF16), seq_out, _sds((1, d), F32), _sds((1, d), F32)]
    out_specs = [tok, tok, seq, vec, vec]
    if last:
        in_specs = [tok, tok, tok, seq, vec]
    else:
        in_specs = [tok, tok, vec, seq, tok, tok, seq, vec]
        out_shape += [seq_out, seq_out]
        out_specs += [seq, seq]
    return _pc(body, name=name, grid=(b, s // ts), out_shape=tuple(out_shape), in_specs=in_specs,
               out_specs=tuple(out_specs), compiler_params=_cp("arbitrary", "arbitrary"))(*upstream, z, y, g, lng)


def _input_bwd(name, alpha, dz, du, x, sc):
    b, s, d = x.shape
    ts = _div(s, 512, 16)

    def body(dz_ref, du_ref, x_ref, sc_ref, gx_ref, dsc_ref, dsh_ref):
        @pl.when(pl.program_id(1) == 0)
        def _():
            dsc_ref[...] = jnp.zeros_like(dsc_ref)
            dsh_ref[...] = jnp.zeros_like(dsh_ref)
        du_ = du_ref[0]
        gx_ref[0] = alpha * dz_ref[0] + du_ * (1.0 + sc_ref[0])
        dsc_ref[0] += jnp.sum(du_ * x_ref[0], axis=0, keepdims=True)
        dsh_ref[0] += jnp.sum(du_, axis=0, keepdims=True)

    tok, seq = _tok_spec(ts, d), _seq_spec(d)
    seq_out = _sds((b, 1, d), F32)
    return _pc(body, name=name, grid=(b, s // ts), out_shape=(_sds(x.shape, F32), seq_out, seq_out),
               in_specs=[tok, tok, tok, seq], out_specs=(tok, seq, seq),
               compiler_params=_cp("parallel", "arbitrary"))(dz, du, x, sc)


def _rows_iota(shape):
    return lax.broadcasted_iota(jnp.int32, shape, 0)


def _back(v, k):
    return pltpu.roll(v, k, axis=0)


def _ahead(v, k):
    return pltpu.roll(v, v.shape[0] - k, axis=0)


def _conv3(ext, w_ref):
    return w_ref[2:3, :] * ext + w_ref[1:2, :] * _back(ext, 1) + w_ref[0:1, :] * _back(ext, 2)


def _conv3_t(dh_ext, w_ref):
    return w_ref[2:3, :] * dh_ext + w_ref[1:2, :] * _ahead(dh_ext, 1) + w_ref[0:1, :] * _ahead(dh_ext, 2)


def _flag(cond):
    return jnp.where(cond, 1.0, 0.0).astype(F32)


def _sigmoid(v):
    return 1.0 / (1.0 + jnp.exp(-v))


def _halo_specs(ts, tc, halo, n_s, col):
    per = ts // halo
    tile = pl.BlockSpec((1, ts, tc), lambda b, i, j: (b, i, col(j)))
    prev = pl.BlockSpec((1, halo, tc), lambda b, i, j: (b, jnp.maximum(i * per - 1, 0), col(j)))
    nxt = pl.BlockSpec((1, halo, tc), lambda b, i, j: (b, jnp.minimum((i + 1) * per, n_s * per - 1), col(j)))
    return tile, prev, nxt


def _convglu_fwd(name, p, cw, cb, cargo=()):
    b, s, f2 = p.shape
    f = f2 // 2
    ts, tc = _div(s, CONV_ROWS, CONV_HALO), _div(f, 256, LANE)
    n_s, n_c = s // ts, f // tc

    def body(pv_ref, pvh_ref, pg_ref, pgh_ref, wv_ref, wg_ref, bv_ref, bg_ref, a_ref):
        keep = _flag(pl.program_id(1) > 0)

        def conv(t_ref, h_ref, w_ref, b_ref):
            ext = jnp.concatenate([h_ref[0] * keep, t_ref[0]], axis=0)
            return _conv3(ext, w_ref)[CONV_HALO:] + b_ref[...]

        val = conv(pv_ref, pvh_ref, wv_ref, bv_ref)
        gate = conv(pg_ref, pgh_ref, wg_ref, bg_ref)
        a_ref[0] = (gate * _sigmoid(gate) * val).astype(BF16)

    tv, hv, _ = _halo_specs(ts, tc, CONV_HALO, n_s, lambda j: j)
    tg, hg, _ = _halo_specs(ts, tc, CONV_HALO, n_s, lambda j: j + n_c)
    wv = pl.BlockSpec((3, tc), lambda b, i, j: (0, j))
    wg = pl.BlockSpec((3, tc), lambda b, i, j: (0, j + n_c))
    bv = pl.BlockSpec((1, tc), lambda b, i, j: (0, j))
    bg = pl.BlockSpec((1, tc), lambda b, i, j: (0, j + n_c))
    (act,), fetched = _pc_cargo(
        body, cargo, name=name, grid=(b, n_s, n_c), route=_Fetch, out_shape=[_sds((b, s, f), BF16)],
        in_specs=[tv, hv, tg, hg, wv, wg, bv, bg],
        out_specs=[pl.BlockSpec((1, ts, tc), lambda b, i, j: (b, i, j))])(p, p, p, p, cw, cw, cb, cb)
    return act, fetched


def _convglu_bwd(name, p, da, cw, cb, cargo=()):
    b, s, f2 = p.shape
    f = f2 // 2
    ts, tc = _div(s, CONV_ROWS, CONV_HALO), _div(f, 256, LANE)
    n_s, n_c = s // ts, f // tc

    def body(pv_ref, pvp_ref, pvn_ref, pg_ref, pgp_ref, pgn_ref, da_ref, dan_ref, wv_ref, wg_ref, bv_ref, bg_ref,
             dpv_ref, dpg_ref, dwv_ref, dwg_ref, dbv_ref, dbg_ref):
        bi, i = pl.program_id(1), pl.program_id(2)

        @pl.when((bi == 0) & (i == 0))
        def _():
            for r in (dwv_ref, dwg_ref, dbv_ref, dbg_ref):
                r[...] = jnp.zeros_like(r)

        keep_prev = _flag(i > 0)
        keep_next = _flag(i < n_s - 1)
        pv_ext = jnp.concatenate([pvp_ref[0] * keep_prev, pv_ref[0], pvn_ref[0]], axis=0)
        pg_ext = jnp.concatenate([pgp_ref[0] * keep_prev, pg_ref[0], pgn_ref[0]], axis=0)
        taps_v = (_back(pv_ext, 2), _back(pv_ext, 1), pv_ext)
        taps_g = (_back(pg_ext, 2), _back(pg_ext, 1), pg_ext)

        def conv(taps, w_ref, b_ref):
            return (w_ref[2:3, :] * taps[2] + w_ref[1:2, :] * taps[1] + w_ref[0:1, :] * taps[0])[CONV_HALO:] + b_ref[...]

        val, gate = conv(taps_v, wv_ref, bv_ref), conv(taps_g, wg_ref, bg_ref)
        da_ext = jnp.concatenate([da_ref[0], dan_ref[0] * keep_next], axis=0)
        sg = _sigmoid(gate)
        dval = da_ext * gate * sg
        dgate = da_ext * val * (sg * (1.0 + gate * (1.0 - sg)))
        dpv_ref[0] = _conv3_t(dval, wv_ref)[:ts].astype(BF16)
        dpg_ref[0] = _conv3_t(dgate, wg_ref)[:ts].astype(BF16)
        for dh, taps, dw_ref, db_ref in ((dval[:ts], taps_v, dwv_ref, dbv_ref), (dgate[:ts], taps_g, dwg_ref, dbg_ref)):
            db_ref[...] += jnp.sum(dh, axis=0, keepdims=True)
            for k in range(3):
                dw_ref[k:k + 1, :] += jnp.sum(dh * taps[k][CONV_HALO:CONV_HALO + ts], axis=0, keepdims=True)

    def specs(col):
        per = ts // CONV_HALO
        tile = pl.BlockSpec((1, ts, tc), lambda j, b, i: (b, i, col(j)))
        prev = pl.BlockSpec((1, CONV_HALO, tc), lambda j, b, i: (b, jnp.maximum(i * per - 1, 0), col(j)))
        nxt = pl.BlockSpec((1, CONV_HALO, tc), lambda j, b, i: (b, jnp.minimum((i + 1) * per, n_s * per - 1), col(j)))
        return tile, prev, nxt

    tv, pvp, pvn = specs(lambda j: j)
    tg, pgp, pgn = specs(lambda j: j + n_c)
    wv = pl.BlockSpec((3, tc), lambda j, b, i: (0, j))
    wg = pl.BlockSpec((3, tc), lambda j, b, i: (0, j + n_c))
    bv = pl.BlockSpec((1, tc), lambda j, b, i: (0, j))
    bg = pl.BlockSpec((1, tc), lambda j, b, i: (0, j + n_c))
    out_tile = pl.BlockSpec((1, ts, tc), lambda j, b, i: (b, i, j))
    acc3, acc1 = pl.BlockSpec((3, tc), lambda j, b, i: (0, j)), pl.BlockSpec((1, tc), lambda j, b, i: (0, j))
    (dpv, dpg, dwv, dwg, dbv, dbg), received = _pc_cargo(
        body, cargo, name=name, grid=(n_c, b, n_s),
        out_shape=(_sds((b, s, f), BF16), _sds((b, s, f), BF16), _sds((3, f), F32), _sds((3, f), F32),
                   _sds((1, f), F32), _sds((1, f), F32)),
        in_specs=[tv, pvp, pvn, tg, pgp, pgn, tv, pvn, wv, wg, bv, bg],
        out_specs=(out_tile, out_tile, acc3, acc3, acc1, acc1))(p, p, p, p, p, p, da, da, cw, cw, cb, cb)
    return dpv, dpg, jnp.concatenate([dwv, dwg], axis=1), jnp.concatenate([dbv, dbg], axis=1), received


def _shortconv_fwd(name, q, cw):
    b, s, d3 = q.shape
    d = d3 // 3
    ts, tc = _div(s, CONV_ROWS, CONV_HALO), _div(d, 256, LANE)
    n_s, n_c = s // ts, d // tc

    def body(gb_ref, gc_ref, gch_ref, h_ref, hh_ref, w_ref, r_ref):
        keep = _flag(pl.program_id(1) > 0)
        m_ext = jnp.concatenate([gch_ref[0] * hh_ref[0] * keep, gc_ref[0] * h_ref[0]], axis=0)
        r_ref[0] = (gb_ref[0] * _conv3(m_ext, w_ref)[CONV_HALO:]).astype(BF16)

    tb, _, _ = _halo_specs(ts, tc, CONV_HALO, n_s, lambda j: j)
    tcc, hc, _ = _halo_specs(ts, tc, CONV_HALO, n_s, lambda j: j + n_c)
    th, hh, _ = _halo_specs(ts, tc, CONV_HALO, n_s, lambda j: j + 2 * n_c)
    w = pl.BlockSpec((3, tc), lambda b, i, j: (0, j))
    return _pc(body, name=name, grid=(b, n_s, n_c), out_shape=_sds((b, s, d), BF16),
               in_specs=[tb, tcc, hc, th, hh, w], out_specs=pl.BlockSpec((1, ts, tc), lambda b, i, j: (b, i, j)),
               compiler_params=_cp("parallel", "parallel", "parallel"))(q, q, q, q, q, cw)


def _shortconv_bwd(name, q, dr, cw):
    b, s, d3 = q.shape
    d = d3 // 3
    ts, tc = _div(s, CONV_ROWS, CONV_HALO), _div(d, 256, LANE)
    n_s, n_c = s // ts, d // tc

    def body(gb_ref, gbn_ref, gc_ref, gcp_ref, h_ref, hp_ref, dr_ref, drn_ref, w_ref,
             dgb_ref, dgc_ref, dh_ref, dw_ref):
        bi, i = pl.program_id(1), pl.program_id(2)

        @pl.when((bi == 0) & (i == 0))
        def _():
            dw_ref[...] = jnp.zeros_like(dw_ref)

        keep_prev = _flag(i > 0)
        keep_next = _flag(i < n_s - 1)
        gc, h = gc_ref[0], h_ref[0]
        m_ext = jnp.concatenate([gcp_ref[0] * hp_ref[0] * keep_prev, gc * h], axis=0)
        cm = _conv3(m_ext, w_ref)[CONV_HALO:]
        dr_ = dr_ref[0]
        dgb_ref[0] = (dr_ * cm).astype(BF16)
        dcv_ext = jnp.concatenate([dr_ * gb_ref[0], drn_ref[0] * gbn_ref[0] * keep_next], axis=0)
        dm = _conv3_t(dcv_ext, w_ref)[:ts]
        dgc_ref[0] = (dm * h).astype(BF16)
        dh_ref[0] = (dm * gc).astype(BF16)
        dcv = dcv_ext[:ts]
        for k in range(3):
            shifted = m_ext if k == 2 else _back(m_ext, 2 - k)
            dw_ref[k:k + 1, :] += jnp.sum(dcv * shifted[CONV_HALO:], axis=0, keepdims=True)

    def specs(col):
        per = ts // CONV_HALO
        tile = pl.BlockSpec((1, ts, tc), lambda j, b, i: (b, i, col(j)))
        prev = pl.BlockSpec((1, CONV_HALO, tc), lambda j, b, i: (b, jnp.maximum(i * per - 1, 0), col(j)))
        nxt = pl.BlockSpec((1, CONV_HALO, tc), lambda j, b, i: (b, jnp.minimum((i + 1) * per, n_s * per - 1), col(j)))
        return tile, prev, nxt

    tb, _, nb = specs(lambda j: j)
    tcc, pc_, _ = specs(lambda j: j + n_c)
    th, ph, _ = specs(lambda j: j + 2 * n_c)
    w = pl.BlockSpec((3, tc), lambda j, b, i: (0, j))
    out_tile = pl.BlockSpec((1, ts, tc), lambda j, b, i: (b, i, j))
    o = _sds((b, s, d), BF16)
    return _pc(body, name=name, grid=(n_c, b, n_s), out_shape=(o, o, o, _sds((3, d), F32)),
               in_specs=[tb, nb, tcc, pc_, th, ph, tb, nb, w], out_specs=(out_tile, out_tile, out_tile, w),
               compiler_params=_cp("parallel", "arbitrary", "arbitrary"))(q, q, q, q, q, q, dr, dr, cw)


def _pick_window(group, cands):
    gid = jnp.full(cands[0].shape, group, jnp.int32)
    out = cands[-1]
    for k in range(len(cands) - 2, -1, -1):
        out = jnp.where(gid == k, cands[k], out)
    return out


def _window_sums(v, shift):
    s1 = v + shift(v, 1)
    s2 = s1 + shift(s1, 2)
    s3 = s2 + shift(s2, 4)
    s4 = s3 + shift(s3, 8)
    return [s1, s2, s3, s4]


def _pool_counts(group, first_row, n_rows, cols):
    t = _rows_iota((n_rows, cols)) + first_row
    window = _pick_window(group, [jnp.full((n_rows, cols), 2 << k, jnp.int32) for k in range(POOL_GROUPS)])
    return jnp.minimum(t + 1, window).astype(F32)


def _pool_fwd(name, x, sc, sh, w, scale):
    b, s, d = x.shape
    tc = d // POOL_GROUPS
    ts = _div(s, 512, POOL_HALO)
    n_s = s // ts

    def body(x_ref, xp_ref, sc_ref, sh_ref, w_ref, scale_ref, y_ref):
        i, grp = pl.program_id(1), pl.program_id(2)
        keep = _flag(i > 0)
        mod = 1.0 + sc_ref[0]
        u = x_ref[0] * mod + sh_ref[0]
        u_ext = jnp.concatenate([(xp_ref[0] * mod + sh_ref[0]) * keep, u], axis=0)
        summed = _pick_window(grp, _window_sums(u_ext, _back))[POOL_HALO:]
        pooled = summed / _pool_counts(grp, i * ts, ts, tc) - u
        y_ref[0] = jnp.dot(pooled.astype(BF16), w_ref[0], preferred_element_type=F32) * scale_ref[...]

    tile, prev, _ = _halo_specs(ts, tc, POOL_HALO, n_s, lambda j: j)
    seq = pl.BlockSpec((1, 1, tc), lambda b, i, j: (b, 0, j))
    return _pc(body, name=name, grid=(b, n_s, POOL_GROUPS), out_shape=_sds(x.shape, F32),
               in_specs=[tile, prev, seq, seq, pl.BlockSpec((1, tc, tc), lambda b, i, j: (j, 0, 0)),
                         pl.BlockSpec((1, tc), lambda b, i, j: (0, j))],
               out_specs=pl.BlockSpec((1, ts, tc), lambda b, i, j: (b, i, j)),
               compiler_params=_cp("parallel", "parallel", "parallel"))(x, x, sc, sh, w, scale)


def _pool_bwd(name, x, sc, sh, dy, w, w_t, scale, cargo=()):
    b, s, d = x.shape
    tc = d // POOL_GROUPS
    ts = _div(s, 512, POOL_HALO)
    n_s = s // ts

    def body(x_ref, xp_ref, sc_ref, sh_ref, dy_ref, dyn_ref, w_ref, wt_ref, scale_ref, du_ref, dw_ref, dscale_ref):
        grp, bi, i = pl.program_id(0), pl.program_id(1), pl.program_id(2)

        @pl.when((bi == 0) & (i == 0))
        def _():
            dw_ref[...] = jnp.zeros_like(dw_ref)
            dscale_ref[...] = jnp.zeros_like(dscale_ref)

        keep_prev = _flag(i > 0)
        keep_next = _flag(i < n_s - 1)
        mod = 1.0 + sc_ref[0]
        u = x_ref[0] * mod + sh_ref[0]
        u_ext = jnp.concatenate([(xp_ref[0] * mod + sh_ref[0]) * keep_prev, u], axis=0)
        summed = _pick_window(grp, _window_sums(u_ext, _back))[POOL_HALO:]
        pooled = (summed / _pool_counts(grp, i * ts, ts, tc) - u).astype(BF16)
        dy_ = dy_ref[0].astype(F32)
        ymat = jnp.dot(pooled, w_ref[0], preferred_element_type=F32)
        dscale_ref[...] += jnp.sum(dy_ * ymat, axis=0, keepdims=True)
        dys_ext = (jnp.concatenate([dy_, dyn_ref[0].astype(F32) * keep_next], axis=0) * scale_ref[...]).astype(BF16)
        dw_ref[0] += lax.dot_general(pooled, dys_ext[:ts], (((0,), (0,)), ((), ())), preferred_element_type=F32)
        dpooled = jnp.dot(dys_ext, wt_ref[0], preferred_element_type=F32)
        e = dpooled / _pool_counts(grp, i * ts, ts + POOL_HALO, tc)
        du_ref[0] = _pick_window(grp, _window_sums(e, _ahead))[:ts] - dpooled[:ts]

    per = ts // POOL_HALO
    tile = pl.BlockSpec((1, ts, tc), lambda j, b, i: (b, i, j))
    prev = pl.BlockSpec((1, POOL_HALO, tc), lambda j, b, i: (b, jnp.maximum(i * per - 1, 0), j))
    nxt = pl.BlockSpec((1, POOL_HALO, tc), lambda j, b, i: (b, jnp.minimum((i + 1) * per, n_s * per - 1), j))
    seq = pl.BlockSpec((1, 1, tc), lambda j, b, i: (b, 0, j))
    wsp = pl.BlockSpec((1, tc, tc), lambda j, b, i: (j, 0, 0))
    vec = pl.BlockSpec((1, tc), lambda j, b, i: (0, j))
    (du, dw, dscale), received = _pc_cargo(
        body, cargo, name=name, grid=(POOL_GROUPS, b, n_s),
        out_shape=(_sds(x.shape, F32), _sds((POOL_GROUPS, tc, tc), F32), _sds((1, d), F32)),
        in_specs=[tile, prev, seq, seq, tile, nxt, wsp, wsp, vec],
        out_specs=(tile, wsp, vec))(x, x, sc, sh, dy, dy, w, w_t, scale)
    return du, dw, dscale, received


def _rope_swap(v):
    lane = lax.broadcasted_iota(jnp.int32, v.shape, v.ndim - 1)
    lo, hi = QK_NOPE, QK_NOPE + QK_ROPE // 2
    from_above = pltpu.roll(v, HEAD_PAD - QK_ROPE // 2, axis=v.ndim - 1)
    from_below = pltpu.roll(v, QK_ROPE // 2, axis=v.ndim - 1)
    return jnp.where((lane >= lo) & (lane < hi), from_above,
                     jnp.where((lane >= hi) & (lane < hi + QK_ROPE // 2), from_below, 0.0))


def _rope(v, cos_t, sin_t):
    return v * cos_t + _rope_swap(v) * sin_t


def _rope_t(dv, cos_t, sin_t):
    return dv * cos_t + _rope_swap(dv * sin_t)


def _rms(v, g):
    r = lax.rsqrt(jnp.mean(v * v, axis=-1, keepdims=True) + RMS_EPS)
    return v * r, r


def _mla_norm_fwd(name, a, qn, kvn, cos_t, sin_t):
    b, s, wa = a.shape
    ql, kvl = qn.shape[1], kvn.shape[1]
    ts = _div(s, 512, 16)

    def body(aq_ref, akv_ref, ape_ref, qn_ref, kvn_ref, cos_ref, sin_ref, cq_ref, ckv_ref, kpe_ref):
        yq, _ = _rms(aq_ref[0], None)
        cq_ref[0] = (yq * qn_ref[...]).astype(BF16)
        ykv, _ = _rms(akv_ref[0], None)
        ckv_ref[0] = (ykv * kvn_ref[...]).astype(BF16)
        kpe_ref[0] = _rope(ape_ref[0], cos_ref[0], sin_ref[0])

    tok = lambda w, col: pl.BlockSpec((1, ts, w), lambda b, i: (b, i, col))
    return _pc(body, name=name, grid=(b, s // ts),
               out_shape=(_sds((b, s, ql), BF16), _sds((b, s, kvl), BF16), _sds((b, s, HEAD_PAD), F32)),
               in_specs=[tok(ql, 0), tok(kvl, ql // kvl), tok(HEAD_PAD, (ql + kvl) // HEAD_PAD), _vec_spec(ql),
                         _vec_spec(kvl), tok(HEAD_PAD, 0), tok(HEAD_PAD, 0)],
               out_specs=(tok(ql, 0), tok(kvl, 0), tok(HEAD_PAD, 0)),
               compiler_params=_cp("parallel", "parallel"))(a, a, a, qn, kvn, cos_t, sin_t)


def _mla_norm_bwd(name, a, dcq, dckv, dkpe, qn, kvn):
    b, s, wa = a.shape
    ql, kvl = qn.shape[1], kvn.shape[1]
    ts = _div(s, 512, 16)

    def body(a_ref, dcq_ref, dckv_ref, dkpe_ref, qn_ref, kvn_ref, da_ref, dqn_ref, dkvn_ref):
        @pl.when((pl.program_id(0) == 0) & (pl.program_id(1) == 0))
        def _():
            dqn_ref[...] = jnp.zeros_like(dqn_ref)
            dkvn_ref[...] = jnp.zeros_like(dkvn_ref)

        def one(v, dc, g_ref, dg_ref):
            yv, r = _rms(v, None)
            dg_ref[...] += jnp.sum(dc * yv, axis=0, keepdims=True)
            dyv = dc * g_ref[...]
            return r * (dyv - yv * jnp.mean(dyv * yv, axis=-1, keepdims=True))

        av = a_ref[0]
        da_ref[0, :, 0:ql] = one(av[:, 0:ql], dcq_ref[0], qn_ref, dqn_ref).astype(BF16)
        da_ref[0, :, ql:ql + kvl] = one(av[:, ql:ql + kvl], dckv_ref[0], kvn_ref, dkvn_ref).astype(BF16)
        da_ref[0, :, ql + kvl:] = dkpe_ref[0].astype(BF16)

    return _pc(body, name=name, grid=(b, s // ts),
               out_shape=(_sds(a.shape, BF16), _sds((1, ql), F32), _sds((1, kvl), F32)),
               in_specs=[_tok_spec(ts, wa), _tok_spec(ts, ql), _tok_spec(ts, kvl), _tok_spec(ts, HEAD_PAD),
                         _vec_spec(ql), _vec_spec(kvl)],
               out_specs=(_tok_spec(ts, wa), _vec_spec(ql), _vec_spec(kvl)),
               compiler_params=_cp("arbitrary", "arbitrary"))(a, dcq, dckv, dkpe, qn, kvn)


def _mla_prep_fwd(name, q_raw, k_raw, kpe, cos_t, sin_t, n_heads):
    b, s, wq = q_raw.shape
    ts = _div(s, 256, 16)

    def body(q_ref, k_ref, kpe_ref, cos_ref, sin_ref, qo_ref, ko_ref):
        cos_, sin_, kpe_ = cos_ref[0], sin_ref[0], kpe_ref[0]
        for h in range(n_heads):
            lanes = slice(h * HEAD_PAD, (h + 1) * HEAD_PAD)
            qo_ref[0, :, lanes] = _rope(q_ref[0, :, lanes], cos_, sin_).astype(BF16)
            ko_ref[0, :, lanes] = (k_ref[0, :, lanes] + kpe_).astype(BF16)

    wide = pl.BlockSpec((1, ts, wq), lambda b, i: (b, i, 0))
    tok = pl.BlockSpec((1, ts, HEAD_PAD), lambda b, i: (b, i, 0))
    o = _sds(q_raw.shape, BF16)
    return _pc(body, name=name, grid=(b, s // ts), out_shape=(o, o),
               in_specs=[wide, wide, tok, tok, tok], out_specs=(wide, wide),
               compiler_params=_cp("parallel", "parallel"))(q_raw, k_raw, kpe, cos_t, sin_t)


def _mla_prep_bwd(name, dq, dk, cos_t, sin_t, n_heads):
    b, s, wq = dq.shape
    ts = _div(s, 256, 16)

    def body(dq_ref, dk_ref, cos_ref, sin_ref, dqr_ref, dkr_ref, dkpe_ref):
        cos_, sin_ = cos_ref[0], sin_ref[0]
        dk_sum = None
        for h in range(n_heads):
            lanes = slice(h * HEAD_PAD, (h + 1) * HEAD_PAD)
            dqr_ref[0, :, lanes] = _rope_t(dq_ref[0, :, lanes], cos_, sin_).astype(BF16)
            dk_h = dk_ref[0, :, lanes]
            dkr_ref[0, :, lanes] = dk_h.astype(BF16)
            dk_sum = dk_h if dk_sum is None else dk_sum + dk_h
        dkpe_ref[0] = _rope_t(dk_sum, cos_, sin_)

    wide = pl.BlockSpec((1, ts, wq), lambda b, i: (b, i, 0))
    tok = pl.BlockSpec((1, ts, HEAD_PAD), lambda b, i: (b, i, 0))
    return _pc(body, name=name, grid=(b, s // ts),
               out_shape=(_sds(dq.shape, BF16), _sds(dq.shape, BF16), _sds((b, s, HEAD_PAD), F32)),
               in_specs=[wide, wide, tok, tok], out_specs=(wide, wide, tok),
               compiler_params=_cp("parallel", "parallel"))(dq, dk, cos_t, sin_t)


FLASH_TILE = 1024
LOG2_E = 1.4426950408889634


def _heads_per_step(n_heads):
    return 2 if n_heads % 2 == 0 else 1


def _causal_mask(i, j, tq, tk):
    rows = lax.broadcasted_iota(jnp.int32, (tq, tk), 0) + i * tq
    cols = lax.broadcasted_iota(jnp.int32, (tq, tk), 1) + j * tk
    return cols <= rows


def _nt(a, b):
    return lax.dot_general(a, b, (((1,), (1,)), ((), ())), preferred_element_type=F32)


def _tn(a, b):
    return lax.dot_general(a, b, (((0,), (0,)), ((), ())), preferred_element_type=F32)


def _flash_fwd(name, q, k, v, n_heads, sm_scale, cargo=()):
    b, s, _ = q.shape
    t, hp = _div(s, FLASH_TILE, LANE), _heads_per_step(n_heads)
    n, w = s // t, hp * HEAD_PAD
    neg = float(jnp.finfo(jnp.float32).min)
    c2 = sm_scale * LOG2_E

    def body(q_ref, k_ref, v_ref, o_ref, lse_ref, m_ref, l_ref, acc_ref):
        i, j = pl.program_id(2), pl.program_id(3)

        @pl.when(j == 0)
        def _():
            m_ref[...] = jnp.full(m_ref.shape, neg, F32)
            l_ref[...] = jnp.zeros_like(l_ref)
            acc_ref[...] = jnp.zeros_like(acc_ref)

        def block(on_diagonal):
            for hh in range(hp):
                ln = slice(hh * HEAD_PAD, (hh + 1) * HEAD_PAD)
                sc = _nt(q_ref[0, :, ln], k_ref[0, :, ln])
                if on_diagonal:
                    sc = jnp.where(_causal_mask(i, j, t, t), sc, neg)
                m_old = m_ref[hh]
                m_new = jnp.maximum(m_old, jnp.max(sc, axis=-1, keepdims=True))
                p = jnp.exp2((sc - m_new) * c2)
                corr = jnp.exp2((m_old - m_new) * c2)
                l_ref[hh] = corr * l_ref[hh] + jnp.sum(p, axis=-1, keepdims=True)
                acc_ref[:, ln] = corr * acc_ref[:, ln] + jnp.dot(p.astype(BF16), v_ref[0, :, ln],
                                                                 preferred_element_type=F32)
                m_ref[hh] = m_new

        pl.when(j < i)(functools.partial(block, False))
        pl.when(j == i)(functools.partial(block, True))

        @pl.when(j == n - 1)
        def _():
            for hh in range(hp):
                ln = slice(hh * HEAD_PAD, (hh + 1) * HEAD_PAD)
                o_ref[0, :, ln] = (acc_ref[:, ln] / l_ref[hh]).astype(BF16)
                lse_ref[0, :, ln] = jnp.broadcast_to(m_ref[hh] * sm_scale + jnp.log(l_ref[hh]), (t, HEAD_PAD))

    qs = pl.BlockSpec((1, t, w), lambda b, h, i, j: (b, i, h))
    ks = pl.BlockSpec((1, t, w), lambda b, h, i, j: (b, jnp.minimum(j, i), h))
    (o, lse), fetched = _pc_cargo(
        body, cargo, name=name, grid=(b, n_heads // hp, n, n), route=_Fetch,
        out_shape=(_sds(q.shape, BF16), _sds(q.shape, F32)), in_specs=[qs, ks, ks], out_specs=(qs, qs),
        scratch_shapes=[pltpu.VMEM((hp, t, 1), F32), pltpu.VMEM((hp, t, 1), F32), pltpu.VMEM((t, w), F32)])(q, k, v)
    return o, lse, fetched


def _flash_bwd(name, q, k, v, o, lse, do, n_heads, sm_scale):
    b, s, _ = q.shape
    t, hp = _div(s, FLASH_TILE, LANE), _heads_per_step(n_heads)
    n, w = s // t, hp * HEAD_PAD
    c2 = sm_scale * LOG2_E

    def body(q_ref, k_ref, v_ref, o_ref, lse_ref, do_ref, dq_hbm, dk_ref, dv_ref, dq_acc, dk_acc, dv_acc, dq_sem):
        bi, hi, j, i = pl.program_id(0), pl.program_id(1), pl.program_id(2), pl.program_id(3)

        @pl.when(i == 0)
        def _():
            dk_acc[...] = jnp.zeros_like(dk_acc)
            dv_acc[...] = jnp.zeros_like(dv_acc)

        rows = pl.ds(pl.multiple_of(i * t, t), t)

        def block(on_diagonal):
            for hh in range(hp):
                ln = slice(hh * HEAD_PAD, (hh + 1) * HEAD_PAD)
                do_ = do_ref[0, :, ln]
                delta = jnp.sum(do_.astype(F32) * o_ref[0, :, ln].astype(F32), axis=-1, keepdims=True)
                sc = _nt(q_ref[0, :, ln], k_ref[0, :, ln])
                p = jnp.exp2(sc * c2 - lse_ref[0, :, hh * HEAD_PAD:hh * HEAD_PAD + 1] * LOG2_E)
                if on_diagonal:
                    p = jnp.where(_causal_mask(i, j, t, t), p, 0.0)
                dv_acc[:, ln] += _tn(p.astype(BF16), do_)
                dp = _nt(do_, v_ref[0, :, ln])
                ds = (p * (dp - delta)).astype(BF16)
                dk_acc[:, ln] += _tn(ds, q_ref[0, :, ln])
                dq_part = jnp.dot(ds, k_ref[0, :, ln], preferred_element_type=F32)

                @pl.when(j == 0)
                def _():
                    dq_acc[rows, ln] = dq_part

                @pl.when(j > 0)
                def _():
                    dq_acc[rows, ln] += dq_part

        pl.when(i > j)(functools.partial(block, False))
        pl.when(i == j)(functools.partial(block, True))

        @pl.when(i == j)
        def _():
            dq_acc[rows, :] = dq_acc[rows, :] * sm_scale
            done = pltpu.make_async_copy(dq_acc.at[rows, :], dq_hbm.at[bi, rows, pl.ds(pl.multiple_of(hi * w, w), w)],
                                         dq_sem)
            done.start()
            done.wait()

        @pl.when(i == n - 1)
        def _():
            dk_ref[0] = dk_acc[...] * sm_scale
            dv_ref[0] = dv_acc[...].astype(BF16)

    qs = pl.BlockSpec((1, t, w), lambda b, h, j, i: (b, jnp.maximum(i, j), h))
    ks = pl.BlockSpec((1, t, w), lambda b, h, j, i: (b, j, h))
    return _pc(body, name=name, grid=(b, n_heads // hp, n, n),
               out_shape=(_sds(q.shape, F32), _sds(q.shape, F32), _sds(q.shape, BF16)),
               in_specs=[qs, ks, ks, qs, qs, qs], out_specs=(pl.BlockSpec(memory_space=pltpu.HBM), ks, ks),
               scratch_shapes=[pltpu.VMEM((s, w), F32), pltpu.VMEM((t, w), F32), pltpu.VMEM((t, w), F32),
                               pltpu.SemaphoreType.DMA],
               compiler_params=_cp("arbitrary", "arbitrary", "arbitrary", "arbitrary"))(q, k, v, o, lse, do)


def _mod_fwd(name, c_all, w, bias):
    depth, d, n = w.shape
    rows = c_all.shape[0]

    def body(c_ref, w_ref, b_ref, o_ref):
        cv = c_ref[...]
        cond = (cv * _sigmoid(cv)).astype(BF16)
        o_ref[0] = jnp.dot(cond, w_ref[0].astype(BF16), preferred_element_type=F32) + b_ref[0]

    return _pc(body, name=name, grid=(depth,), out_shape=_sds((depth, rows, n), F32),
               in_specs=[pl.BlockSpec((rows, d), lambda l: (0, 0)), pl.BlockSpec((1, d, n), lambda l: (l, 0, 0)),
                         pl.BlockSpec((1, 1, n), lambda l: (l, 0, 0))],
               out_specs=pl.BlockSpec((1, rows, n), lambda l: (l, 0, 0)), compiler_params=_cp("parallel"))(c_all, w, bias)


def _mod_bwd(name, c_all, dmod_cols, dmod_all):
    depth, rows, n = dmod_cols.shape
    d = c_all.shape[1]
    n_all = dmod_all.shape[2]
    tn = _div(n, 512, LANE)

    def body(c_ref, dm_ref, dma_ref, gw_ref, gb_ref):
        cv = c_ref[...]
        cond = (cv * _sigmoid(cv)).astype(BF16)
        gw_ref[0] = _tn(cond, dm_ref[0].astype(BF16))

        @pl.when(pl.program_id(1) == 0)
        def _():
            gb_ref[0] = jnp.sum(dma_ref[0], axis=0, keepdims=True)

    return _pc(body, name=name, grid=(depth, n // tn),
               out_shape=(_sds((depth, d, n), F32), _sds((depth, 1, n_all), F32)),
               in_specs=[pl.BlockSpec((rows, d), lambda l, j: (0, 0)), pl.BlockSpec((1, rows, tn), lambda l, j: (l, 0, j)),
                         pl.BlockSpec((1, rows, n_all), lambda l, j: (l, 0, 0))],
               out_specs=(pl.BlockSpec((1, d, tn), lambda l, j: (l, 0, j)), pl.BlockSpec((1, 1, n_all), lambda l, j: (l, 0, 0))),
               compiler_params=_cp("parallel", "arbitrary"))(c_all, dmod_cols, dmod_all)


def _adamw(name, w, g, m, v):
    shape = w.shape
    cols = shape[-1]
    rows = _size(shape) // cols
    tr = _div(rows, max(SUBLANE, (2 ** 19) // cols // SUBLANE * SUBLANE), SUBLANE)
    c1 = 1.0 - ADAM_B1 ** ADAM_STEP
    c2 = 1.0 - ADAM_B2 ** ADAM_STEP

    def body(w_ref, g_ref, m_ref, v_ref, d_ref, mo_ref, vo_ref):
        gv = g_ref[...]
        m_new = ADAM_B1 * m_ref[...] + (1.0 - ADAM_B1) * gv
        v_new = ADAM_B2 * v_ref[...] + (1.0 - ADAM_B2) * (gv * gv)
        m_hat = m_new / c1
        v_hat = v_new / c2
        d_ref[...] = -ADAM_LR * (m_hat / (jnp.sqrt(v_hat) + ADAM_EPS) + ADAM_WD * w_ref[...])
        mo_ref[...] = m_new
        vo_ref[...] = v_new

    spec = pl.BlockSpec((tr, cols), lambda i: (i, 0))
    o = _sds((rows, cols), F32)
    outs = _pc(body, name=name, grid=(rows // tr,), out_shape=(o, o, o), in_specs=[spec] * 4, out_specs=(spec,) * 3,
               compiler_params=_cp("parallel"))(*[a.reshape(rows, cols) for a in (w, g, m, v)])
    return tuple(a.reshape(shape) for a in outs)


def _rope_tables(positions):
    half = QK_ROPE // 2
    inv_freq = ROPE_THETA ** (-jnp.arange(0, QK_ROPE, 2, dtype=F32) / QK_ROPE)
    ang = positions.astype(F32)[..., None] * inv_freq
    cos, sin = jnp.cos(ang), jnp.sin(ang)
    lead = positions.shape
    ones = jnp.ones(lead + (QK_NOPE,), F32)
    tail_one = jnp.ones(lead + (HEAD_PAD - QK_NOPE - QK_ROPE,), F32)
    cos_t = jnp.concatenate([ones, cos, cos, tail_one], axis=-1)
    sin_t = jnp.concatenate([0 * ones, -sin, sin, 0 * tail_one], axis=-1)
    return cos_t, sin_t


def _pad_heads(w, n_heads, parts, axis):
    w = jnp.moveaxis(w, axis, -1)
    lead = w.shape[:-1]
    per = w.shape[-1] // n_heads
    w = w.reshape(lead + (n_heads, per))
    kept = jnp.concatenate([w[..., a:b_] for a, b_ in parts], axis=-1)
    pad = HEAD_PAD - kept.shape[-1]
    kept = jnp.concatenate([kept, jnp.zeros(lead + (n_heads, pad), w.dtype)], axis=-1)
    return jnp.moveaxis(kept.reshape(lead + (n_heads * HEAD_PAD,)), -1, axis)


def _unpad_heads(g, n_heads, width, axis):
    g = jnp.moveaxis(g, axis, -1)
    lead = g.shape[:-1]
    g = g.reshape(lead + (n_heads, HEAD_PAD))[..., :width]
    return g, lead


def kernel(x, c, positions, mod_w, mod_b, ln_g, ln_b, pool_w, pool_scale, mla_w_a, mla_q_norm, mla_w_uq, mla_kv_norm, mla_w_ukv, mla_w_o, sc_w_in, sc_conv, sc_w_out, ffn_w_up, ffn_conv, ffn_conv_b, ffn_w_down, loss_target, m_mod_w, m_mod_b, m_ln_g, m_ln_b, m_pool_w, m_pool_scale, m_mla_w_a, m_mla_q_norm, m_mla_w_uq, m_mla_kv_norm, m_mla_w_ukv, m_mla_w_o, m_sc_w_in, m_sc_conv, m_sc_w_out, m_ffn_w_up, m_ffn_conv, m_ffn_conv_b, m_ffn_w_down, v_mod_w, v_mod_b, v_ln_g, v_ln_b, v_pool_w, v_pool_scale, v_mla_w_a, v_mla_q_norm, v_mla_w_uq, v_mla_kv_norm, v_mla_w_ukv, v_mla_w_o, v_sc_w_in, v_sc_conv, v_sc_w_out, v_ffn_w_up, v_ffn_conv, v_ffn_conv_b, v_ffn_w_down):
    wts = dict(mod_w=mod_w, mod_b=mod_b, ln_g=ln_g, ln_b=ln_b, pool_w=pool_w, pool_scale=pool_scale, mla_w_a=mla_w_a,
               mla_q_norm=mla_q_norm, mla_w_uq=mla_w_uq, mla_kv_norm=mla_kv_norm, mla_w_ukv=mla_w_ukv, mla_w_o=mla_w_o,
               sc_w_in=sc_w_in, sc_conv=sc_conv, sc_w_out=sc_w_out, ffn_w_up=ffn_w_up, ffn_conv=ffn_conv,
               ffn_conv_b=ffn_conv_b, ffn_w_down=ffn_w_down)
    mom1 = dict(mod_w=m_mod_w, mod_b=m_mod_b, ln_g=m_ln_g, ln_b=m_ln_b, pool_w=m_pool_w, pool_scale=m_pool_scale,
                mla_w_a=m_mla_w_a, mla_q_norm=m_mla_q_norm, mla_w_uq=m_mla_w_uq, mla_kv_norm=m_mla_kv_norm,
                mla_w_ukv=m_mla_w_ukv, mla_w_o=m_mla_w_o, sc_w_in=m_sc_w_in, sc_conv=m_sc_conv, sc_w_out=m_sc_w_out,
                ffn_w_up=m_ffn_w_up, ffn_conv=m_ffn_conv, ffn_conv_b=m_ffn_conv_b, ffn_w_down=m_ffn_w_down)
    mom2 = dict(mod_w=v_mod_w, mod_b=v_mod_b, ln_g=v_ln_g, ln_b=v_ln_b, pool_w=v_pool_w, pool_scale=v_pool_scale,
                mla_w_a=v_mla_w_a, mla_q_norm=v_mla_q_norm, mla_w_uq=v_mla_w_uq, mla_kv_norm=v_mla_kv_norm,
                mla_w_ukv=v_mla_w_ukv, mla_w_o=v_mla_w_o, sc_w_in=v_sc_w_in, sc_conv=v_sc_conv, sc_w_out=v_sc_w_out,
                ffn_w_up=v_ffn_w_up, ffn_conv=v_ffn_conv, ffn_conv_b=v_ffn_conv_b, ffn_w_down=v_ffn_w_down)

    bsz, seq, d = x.shape
    depth = mod_b.shape[0]
    n_tok = bsz * seq
    n_heads = d // V_HEAD
    ql, kvl = mla_q_norm.shape[1], mla_kv_norm.shape[1]
    alpha = float((2 * depth) ** 0.25)
    sm_scale = float((QK_NOPE + QK_ROPE) ** -0.5)
    mx, my, mc = lax.axis_index("x"), lax.axis_index("y"), lax.axis_index("c")
    chip = 2 * mx + my
    dev = 2 * chip + mc

    small_names = list(SMALL_SHARDED)
    small_pack, small_spans = _pack_rows([c] + [wts[n] for n in small_names], F32, SUBLANE)
    rows_small = small_pack.shape[0]
    small_all = _all_gather8("gather_small_params", small_pack, True).reshape(N_DEV, rows_small * PACK_COLS)
    c_all = small_all[:, :c.size].reshape(N_DEV * bsz, d)
    per_chip = small_all[0::2]
    full = dict(wts)
    for n, (off, shape) in zip(small_names, small_spans[1:]):
        blocks = per_chip[:, off:off + _size(shape)].reshape((N_CHIP,) + tuple(shape))
        full[n] = _join_chips(blocks, SMALL_SHARDED[n])

    n_mod = mod_w.shape[2]
    bias_cols = lax.dynamic_slice_in_dim(mod_b, chip * n_mod, n_mod, axis=1)[:, None, :]
    mod_cols = _mod_fwd("mod_fwd", c_all, mod_w, bias_cols)
    half_rows = (N_DEV * bsz) // 2
    mod_half = lax.dynamic_slice_in_dim(mod_cols, mc * half_rows, half_rows, axis=1).reshape(depth * half_rows, n_mod)
    mod_all = _all_gather8("gather_mod", mod_half, True).reshape(N_CHIP, 2, depth, half_rows, n_mod)
    mod_all = jnp.transpose(mod_all, (2, 1, 3, 0, 4)).reshape(depth, N_DEV * bsz, N_CHIP * n_mod)
    mod_mine = lax.dynamic_slice_in_dim(mod_all, dev * bsz, bsz, axis=1)
    mods = [[mod_mine[l, :, k * d:(k + 1) * d][:, None, :] for k in range(6)] for l in range(depth)]

    big_names = list(BIG)
    f_hid = ffn_w_down.shape[1] * N_CHIP
    host_layer = 1
    assert depth > host_layer

    def layer_of(n, i):
        if n == 'pool_w':
            return 3 * i
        if n.startswith('mla_'):
            return 3 * i + 1
        if n.startswith('sc_'):
            return 3 * i + 2
        return i

    last_layer_of_group = (0, host_layer)

    def group_of(n, i):
        return sum(layer_of(n, i) > top for top in last_layer_of_group)

    n_groups = len(last_layer_of_group) + 1
    span = {n: [[i for i in range(wts[n].shape[0]) if group_of(n, i) == g] for g in range(n_groups)] for n in big_names}
    members = [[n for n in big_names if span[n][g]] for g in range(n_groups)]

    def rows2d(a):
        return a.astype(BF16).reshape(-1, a.shape[-1])

    def layouts(bc):
        out = {}
        for n in ('pool_w', 'mla_w_a', 'mla_w_uq', 'mla_w_ukv', 'mla_w_o', 'sc_w_out'):
            if n in bc:
                out[n] = jnp.concatenate([bc[n][j] for j in range(N_CHIP)], axis=BIG[n])
        if 'ffn_w_up' in bc:
            nl = bc['ffn_w_up'].shape[1]
            out['up_cols'] = bc['ffn_w_up'].reshape(N_CHIP, nl * d, -1)
            out['up_rows'] = jnp.transpose(bc['ffn_w_up'], (1, 0, 3, 2)).reshape(1, nl * 2 * f_hid, d)
            out['down_rows'] = jnp.transpose(bc['ffn_w_down'], (1, 0, 2, 3)).reshape(1, nl * f_hid, d)
            out['down_t'] = jnp.transpose(bc['ffn_w_down'], (1, 3, 0, 2)).reshape(1, nl * d, f_hid)
        if 'sc_w_in' in bc:
            ns = bc['sc_w_in'].shape[1]
            out['in_cols'] = bc['sc_w_in'].reshape(N_CHIP, ns * d, -1)
            out['in_rows'] = jnp.transpose(bc['sc_w_in'], (1, 0, 3, 2)).reshape(1, ns * 3 * d, d)
        return out

    shards = [{n: rows2d(wts[n][span[n][g][0]:span[n][g][-1] + 1]) for n in members[g]} for g in range(n_groups)]
    lay = [None] * n_groups

    def by_chip(g, n, blocks):
        return blocks.reshape((N_CHIP, len(span[n][g])) + wts[n].shape[1:])

    def fetched_group(g, got):
        lay[g] = layouts({n: by_chip(g, n, lax.dynamic_update_index_in_dim(blocks, shards[g][n], chip, 0))
                          for n, blocks in got.items()})

    gathered = _gather_weights("gather_weights", [shards[0][n] for n in members[0]])
    lay[0] = layouts({n: by_chip(0, n, blocks) for n, blocks in zip(members[0], gathered)})

    def grp(n, i):
        g = group_of(n, i)
        return lay[g], i - span[n][g][0]

    nope_rope = [(0, QK_NOPE + QK_ROPE)]
    cos_t, sin_t = _rope_tables(positions)

    def t2(a):
        return a.reshape(n_tok, a.shape[-1])

    def t3(a):
        return a.reshape(bsz, seq, a.shape[-1])

    saved = []
    xin = x
    u = _modulate("modulate_in", x, mods[0][1], mods[0][0])
    loss_acc = None
    for l in range(depth):
        sh1, sc1, g1, sh2, sc2, g2 = mods[l]
        kind, j = l % 3, l // 3
        st = dict(x=xin)
        if kind == 0:
            grp_l, jj = grp('pool_w', j)
            w = grp_l['pool_w'][jj]
            st.update(w=w, w_t=jnp.swapaxes(w, 1, 2), scale=full['pool_scale'][j][None, :])
            y = _pool_fwd(f"pool_fwd_{l}", xin, sc1, sh1, st['w'], st['scale'])
        elif kind == 1:
            grp_l, jj = grp('mla_w_a', j)
            wa, wuq, wukv = grp_l['mla_w_a'][jj], grp_l['mla_w_uq'][jj], grp_l['mla_w_ukv'][jj]
            zeros = jnp.zeros((d, QK_NOPE), BF16)
            w_a = jnp.concatenate([wa[:, :ql + kvl], zeros, wa[:, ql + kvl:], zeros[:, :HEAD_PAD - QK_NOPE - QK_ROPE]], axis=1)
            w_uq = _pad_heads(wuq, n_heads, nope_rope, 1)
            w_k = _pad_heads(wukv, n_heads, [(0, QK_NOPE)], 1)
            w_v = _pad_heads(wukv, n_heads, [(QK_NOPE, QK_NOPE + V_HEAD)], 1)
            w_o = _pad_heads(grp_l['mla_w_o'][jj], n_heads, [(0, V_HEAD)], 0)
            qn, kvn = mla_q_norm[j][None, :], mla_kv_norm[j][None, :]
            a = t3(_mm_nn(f"mla_a_{l}", [(t2(u), _w2(w_a))], F32))
            cq, ckv, kpe = _mla_norm_fwd(f"mla_norm_fwd_{l}", a, qn, kvn, cos_t, sin_t)
            q_raw = t3(_mm_nn(f"mla_q_{l}", [(t2(cq), _w2(w_uq))], F32))
            k_raw = t3(_mm_nn(f"mla_k_{l}", [(t2(ckv), _w2(w_k))], F32))
            vh = t3(_mm_nn(f"mla_v_{l}", [(t2(ckv), _w2(w_v))], BF16))
            qh, kh = _mla_prep_fwd(f"mla_prep_fwd_{l}", q_raw, k_raw, kpe, cos_t, sin_t, n_heads)
            o, lse, fetched = _flash_fwd(f"flash_fwd_{l}", qh, kh, vh, n_heads, sm_scale,
                                         cargo=[shards[2][n] for n in members[2]] if l == host_layer else ())
            if l == host_layer:
                fetched_group(2, dict(zip(members[2], fetched)))
            y = t3(_mm_nn(f"mla_o_{l}", [(t2(o), _w2(w_o))], F32))
            st.update(u=u, w_a=w_a, w_uq=w_uq, w_k=w_k, w_v=w_v, w_o=w_o, qn=qn, kvn=kvn, a=a, cq=cq, ckv=ckv,
                      qh=qh, kh=kh, vh=vh, o=o, lse=lse)
        else:
            grp_l, jj = grp('sc_w_in', j)
            w_out, cw = grp_l['sc_w_out'][jj], full['sc_conv'][j]
            q = t3(_mm_nn(f"sc_in_{l}", [(t2(u), (grp_l['in_cols'], jj))], F32))
            r = _shortconv_fwd(f"shortconv_fwd_{l}", q, cw)
            y = t3(_mm_nn(f"sc_out_{l}", [(t2(r), _w2(w_out))], F32))
            st.update(u=u, w_out=w_out, cw=cw, q=q, r=r, in_rows=(grp_l['in_rows'], jj))
        lng, lnb = full['ln_g'][l], full['ln_b'][l]
        z1, xmid, u2 = _ln_mod_fwd(f"ln_mod_a_{l}", alpha, xin, y, g1, lng[0:1], lnb[0:1], sc2, sh2)
        cwf, cbf = full['ffn_conv'][l], ffn_conv_b[l][None, :]
        ffn_w, ll = grp('ffn_w_up', l)
        ffn_names = ('ffn_w_up', 'ffn_w_down')
        ride_mm = [n for n in members[1] if n not in ffn_names] if l == 0 else []
        ride_conv = [n for n in members[1] if n == 'ffn_w_up'] if l == 0 else []
        ride_down = [n for n in members[1] if n == 'ffn_w_down'] if l == 0 else []
        p = _mm_nn(f"ffn_up_{l}", [(t2(u2), (ffn_w['up_cols'], ll))], F32, cargo=[shards[1][n] for n in ride_mm],
                   route=_Fetch, tm_cap=2048)
        got_mm = ()
        if ride_mm:
            p, got_mm = p
        p = t3(p)
        act, got_conv = _convglu_fwd(f"convglu_fwd_{l}", p, cwf, cbf, cargo=[shards[1][n] for n in ride_conv])
        y2 = _mm_nn(f"ffn_down_{l}", [(t2(act), (ffn_w['down_rows'], ll))], F32, cargo=[shards[1][n] for n in ride_down],
                    route=_Fetch)
        got_down = ()
        if ride_down:
            y2, got_down = y2
        y2 = t3(y2)
        if l == 0:
            fetched_group(1, {**dict(zip(ride_mm, got_mm)), **dict(zip(ride_conv, got_conv)),
                              **dict(zip(ride_down, got_down))})
        st.update(y1=y, z1=z1, xmid=xmid, u2=u2, p=p, act=act, y2=y2, cwf=cwf, cbf=cbf, lng=lng, lnb=lnb,
                  ffn_w=ffn_w, ll=ll)
        if l + 1 < depth:
            nsh1, nsc1 = mods[l + 1][0], mods[l + 1][1]
            z2, xin, u = _ln_mod_fwd(f"ln_mod_b_{l}", alpha, xmid, y2, g2, lng[1:2], lnb[1:2], nsc1, nsh1)
        else:
            z2, ct, loss_acc = _ln_loss_fwd("ln_loss", alpha, xmid, y2, g2, lng[1:2], lnb[1:2], loss_target)
        st.update(z2=z2)
        saved.append(st)
    loss = lax.psum(loss_acc[0, 0], ("x", "y", "c"))

    grads = {}
    dmods = [[None] * 6 for _ in range(depth)]
    g_ln_g = [[None, None] for _ in range(depth)]
    g_ln_b = [[None, None] for _ in range(depth)]
    stack = {n: [None] * wts[n].shape[0] for n in ('pool_scale', 'mla_q_norm', 'mla_kv_norm', 'sc_conv', 'ffn_conv',
                                                    'ffn_conv_b')}
    units = {n: [None] * wts[n].shape[0] for n in big_names}
    cargo_a, cargo_b, received = [], [], {}

    def landed(items, got):
        for (n, i, _), r in zip(items, got):
            received[(n, i)] = r

    upstream = (ct,)
    for l in reversed(range(depth)):
        st = saved[l]
        sh1, sc1, g1, sh2, sc2, g2 = mods[l]
        kind, j = l % 3, l // 3
        if len(upstream) > 1:
            upstream = (upstream[0], upstream[1], st['lnb'][1:2], upstream[2])
        res = _sub_bwd(f"sub_bwd_b_{l}", alpha, upstream, st['z2'], st['y2'], g2, st['lng'][1:2])
        dz2, dy2, dmods[l][5], g_ln_g[l][1], g_ln_b[l][1] = res[:5]
        if l + 1 < depth:
            dmods[l + 1][1], dmods[l + 1][0] = res[5], res[6]
        dy2f = t2(dy2)
        ffn_w, ll = st['ffn_w'], st['ll']
        da = t3(_mm_nn(f"ffn_down_bwd_{l}", [(dy2f, (ffn_w['down_t'], ll))], F32))
        units['ffn_w_down'][l] = _mm_tn(f"ffn_down_dw_{l}", t2(st['act']), [dy2f],
                                        out_dtype=BF16).reshape(N_CHIP, f_hid // N_CHIP, d)
        dpv, dpg, dcw, dcb, got = _convglu_bwd(f"convglu_bwd_{l}", st['p'], da, st['cwf'], st['cbf'],
                                               cargo=[u for _, _, u in cargo_a])
        landed(cargo_a, got)
        stack['ffn_conv'][l], stack['ffn_conv_b'][l] = dcw, dcb[0]
        down_unit = [('ffn_w_down', l, units['ffn_w_down'][l])]
        du2, got = _mm_nn(f"ffn_up_bwd_{l}", [(t2(dpv), (ffn_w['up_rows'], 2 * ll)),
                                              (t2(dpg), (ffn_w['up_rows'], 2 * ll + 1))], F32,
                          cargo=[units['ffn_w_down'][l]])
        landed(down_unit, got)
        du2 = t3(du2)
        res = _mm_tn(f"ffn_up_dw_{l}", t2(st['u2']), [t2(dpv), t2(dpg)], N_CHIP, out_dtype=BF16,
                     cargo=[u for _, _, u in cargo_b])
        if cargo_b:
            landed(cargo_b, res[1])
            res = res[0]
        units['ffn_w_up'][l] = res
        res = _sub_bwd(f"sub_bwd_a_{l}", alpha, (dz2, du2, st['lnb'][0:1], sc2), st['z1'], st['y1'], g1, st['lng'][0:1])
        dz1, dy1, dmods[l][2], g_ln_g[l][0], g_ln_b[l][0], dmods[l][4], dmods[l][3] = res
        dy1f = t2(dy1)
        if kind == 0:
            up_unit = [('ffn_w_up', l, units['ffn_w_up'][l])] if l == 0 else []
            du1, dw, dscale, got = _pool_bwd(f"pool_bwd_{l}", st['x'], sc1, sh1, dy1, st['w'], st['w_t'], st['scale'],
                                             cargo=[u for _, _, u in up_unit])
            landed(up_unit, got)
            stack['pool_scale'][j] = dscale[0]
            grp = dw.shape[1] // N_CHIP
            units['pool_w'][j] = jnp.transpose(dw.reshape(POOL_GROUPS, N_CHIP, grp, dw.shape[2]),
                                               (1, 0, 2, 3)).reshape(N_CHIP, POOL_GROUPS * grp, dw.shape[2])
        elif kind == 1:
            do = t3(_mm_nn(f"mla_o_bwd_{l}", [(dy1f, _w2(jnp.swapaxes(st['w_o'], 0, 1)))], BF16))
            gwo, _ = _unpad_heads(_mm_tn(f"mla_o_dw_{l}", t2(st['o']), [dy1f])[0], n_heads, V_HEAD, 0)
            units['mla_w_o'][j] = jnp.moveaxis(gwo.reshape(d, n_heads * V_HEAD), -1, 0).reshape(N_CHIP, -1, d)
            fa = (st['qh'], st['kh'], st['vh'], st['o'], st['lse'], do, n_heads, sm_scale)
            dq, dk, dv = _flash_bwd(f"flash_bwd_{l}", *fa)
            dq_raw, dk_raw, dkpe = _mla_prep_bwd(f"mla_prep_bwd_{l}", dq, dk, cos_t, sin_t, n_heads)
            dq_raw, dk_raw, dv_raw = t2(dq_raw), t2(dk_raw), t2(dv)
            dcq = t3(_mm_nn(f"mla_q_bwd_{l}", [(dq_raw, _w2(jnp.swapaxes(st['w_uq'], 0, 1)))], F32))
            dckv = t3(_mm_nn(f"mla_kv_bwd_{l}", [(dk_raw, _w2(jnp.swapaxes(st['w_k'], 0, 1))),
                                                   (dv_raw, _w2(jnp.swapaxes(st['w_v'], 0, 1)))], F32))
            gq, _ = _unpad_heads(_mm_tn(f"mla_q_dw_{l}", t2(st['cq']), [dq_raw])[0], n_heads, QK_NOPE + QK_ROPE, 1)
            units['mla_w_uq'][j] = _cols_by_chip(gq.reshape(ql, n_heads * (QK_NOPE + QK_ROPE)))
            gkv = _mm_tn(f"mla_kv_dw_{l}", t2(st['ckv']), [dk_raw, dv_raw])[0]
            gk, _ = _unpad_heads(gkv[:, :n_heads * HEAD_PAD], n_heads, QK_NOPE, 1)
            gv, _ = _unpad_heads(gkv[:, n_heads * HEAD_PAD:], n_heads, V_HEAD, 1)
            units['mla_w_ukv'][j] = _cols_by_chip(
                jnp.concatenate([gk, gv], axis=-1).reshape(kvl, n_heads * (QK_NOPE + V_HEAD)))
            da_, dqn, dkvn = _mla_norm_bwd(f"mla_norm_bwd_{l}", st['a'], dcq, dckv, dkpe, st['qn'], st['kvn'])
            stack['mla_q_norm'][j], stack['mla_kv_norm'][j] = dqn[0], dkvn[0]
            du1 = t3(_mm_nn(f"mla_a_bwd_{l}", [(t2(da_), _w2(jnp.swapaxes(st['w_a'], 0, 1)))], F32))
            gwa = _mm_tn(f"mla_a_dw_{l}", t2(st['u']), [t2(da_)])[0]
            units['mla_w_a'][j] = _cols_by_chip(jnp.concatenate(
                [gwa[:, :ql + kvl], gwa[:, ql + kvl + QK_NOPE:ql + kvl + QK_NOPE + QK_ROPE]], axis=1))
        else:
            dr = t3(_mm_nn(f"sc_out_bwd_{l}", [(dy1f, _w2(jnp.swapaxes(st['w_out'], 0, 1)))], F32))
            units['sc_w_out'][j] = _mm_tn(f"sc_out_dw_{l}", t2(st['r']), [dy1f], out_dtype=BF16).reshape(N_CHIP, -1, d)
            dgb, dgc, dh, dcw = _shortconv_bwd(f"shortconv_bwd_{l}", st['q'], dr, st['cw'])
            stack['sc_conv'][j] = dcw
            parts = [t2(dgb), t2(dgc), t2(dh)]
            in_rows, jj = st['in_rows']
            du1 = t3(_mm_nn(f"sc_in_bwd_{l}", [(parts[k], (in_rows, 3 * jj + k)) for k in range(3)], F32))
            units['sc_w_in'][j] = _cols_by_chip(jnp.concatenate(
                [_mm_tn(f"sc_in_dw_{k}_{l}", t2(st['u']), [parts[k]])[0] for k in range(3)], axis=1))
        upstream = (dz1, du1, sc1)
        mixer = {0: ['pool_w'], 1: ['mla_w_a', 'mla_w_uq', 'mla_w_ukv', 'mla_w_o'], 2: ['sc_w_in', 'sc_w_out']}[kind]
        for n in mixer:
            units[n][j] = units[n][j].astype(BF16)
        cargo_a = [('ffn_w_up', l, units['ffn_w_up'][l])] if l > 0 else []
        cargo_b = [(n, j, units[n][j]) for n in mixer]
    grad_x, dmods[0][1], dmods[0][0] = _input_bwd("input_bwd", alpha, upstream[0], upstream[1], x, mods[0][1])

    for n, parts in stack.items():
        grads[n] = jnp.stack(parts)
    grads['ln_g'] = jnp.stack([jnp.concatenate(r, axis=0) for r in g_ln_g])
    grads['ln_b'] = jnp.stack([jnp.concatenate(r, axis=0) for r in g_ln_b])
    dmod_mine = jnp.stack([jnp.concatenate([t[:, 0, :] for t in dmods[l]], axis=-1) for l in range(depth)])

    small_grad_names = small_names + ['mla_q_norm', 'mla_kv_norm', 'ffn_conv_b']
    sg_pack, sg_spans = _pack_rows([dmod_mine] + [grads[n] for n in small_grad_names], F32, SUBLANE)
    rows_sg = sg_pack.shape[0]
    sg_all = _all_gather8("gather_small_grads", sg_pack, True).reshape(N_DEV, rows_sg, PACK_COLS)
    dmod_all = sg_all.reshape(N_DEV, -1)[:, :dmod_mine.size].reshape(N_DEV, depth, bsz, 6 * d)
    dmod_all = jnp.transpose(dmod_all, (1, 0, 2, 3)).reshape(depth, N_DEV * bsz, 6 * d)
    sg_sum = _sum8("sum_small_grads", sg_all).reshape(-1)
    for n, (off, shape) in zip(small_grad_names, sg_spans[1:]):
        g_full = sg_sum[off:off + _size(shape)].reshape(shape)
        if n in SMALL_SHARDED:
            ax = SMALL_SHARDED[n]
            width = shape[ax] // N_CHIP
            g_full = lax.dynamic_slice_in_dim(g_full, chip * width, width, axis=ax)
        grads[n] = g_full
    dmod_cols = lax.dynamic_slice_in_dim(dmod_all, chip * n_mod, n_mod, axis=2)
    grads['mod_w'], gb = _mod_bwd("mod_bwd", c_all, dmod_cols, dmod_all)
    grads['mod_b'] = gb[:, 0, :]

    keys = [(n, i) for n in big_names for i in range(len(units[n]))]
    last = [(n, i, units[n][i]) for n, i in keys if (n, i) not in received]
    landed(last, _scatter_grads("scatter_big_grads", [u for _, _, u in last]))
    chip_core = jnp.stack([chip, mc]).astype(jnp.int32)
    bufs = _swap_halves("swap_big_grad_halves",
                        [_sum8_into_half(f"sum_big_grads_{n}_{i}", units[n][i], received[(n, i)], chip_core)
                         for n, i in keys])
    for n in big_names:
        grads[n] = jnp.stack([b for (m, _), b in zip(keys, bufs) if m == n]).reshape(wts[n].shape)

    deltas, new_m, new_v = {}, {}, {}
    for n in WEIGHTS:
        deltas[n], new_m[n], new_v[n] = _adamw(f"adamw_{n}", wts[n], grads[n], mom1[n], mom2[n])
    return (loss, grad_x, *[grads[n] for n in WEIGHTS], *[deltas[n] for n in WEIGHTS],
            *[new_m[n] for n in WEIGHTS], *[new_v[n] for n in WEIGHTS])
```

```python
import functools

import jax
import jax.numpy as jnp
from jax import lax
from jax.experimental import pallas as pl
from jax.experimental.pallas import tpu as pltpu

F32 = jnp.float32
BF16 = jnp.bfloat16
MESH = pl.DeviceIdType.MESH

N_DEV = 8
N_CHIP = 4
LANE = 128
SUBLANE = 8
VMEM_LIMIT_BYTES = 56 * 2 ** 20
PACK_COLS = 1024

LN_EPS = 1e-5
RMS_EPS = 1e-6
QK_NOPE, QK_ROPE, V_HEAD = 64, 32, 64
ROPE_THETA = 10000.0
HEAD_PAD = 128
POOL_GROUPS = 4
POOL_HALO = 16
CONV_HALO = 8
CONV_ROWS = 1024
WIDE_K_ROWS = 1024
ADAM_LR, ADAM_B1, ADAM_B2, ADAM_EPS, ADAM_WD, ADAM_STEP = 0.001, 0.9, 0.999, 1e-08, 0.01, 10

WEIGHTS = ['mod_w', 'mod_b', 'ln_g', 'ln_b', 'pool_w', 'pool_scale', 'mla_w_a', 'mla_q_norm', 'mla_w_uq',
           'mla_kv_norm', 'mla_w_ukv', 'mla_w_o', 'sc_w_in', 'sc_conv', 'sc_w_out', 'ffn_w_up', 'ffn_conv',
           'ffn_conv_b', 'ffn_w_down']
BIG = {'pool_w': 2, 'mla_w_a': 2, 'mla_w_uq': 2, 'mla_w_ukv': 2, 'mla_w_o': 1, 'sc_w_in': 2, 'sc_w_out': 1,
       'ffn_w_up': 2, 'ffn_w_down': 1}
SMALL_SHARDED = {'ln_g': 2, 'ln_b': 2, 'pool_scale': 1, 'sc_conv': 2, 'ffn_conv': 2}
REPLICATED = ['mod_b', 'mla_q_norm', 'mla_kv_norm', 'ffn_conv_b']


def _pc(body, **kw):
    return pl.pallas_call(body, **kw)


def _cp(*sem):
    return pltpu.CompilerParams(dimension_semantics=sem, vmem_limit_bytes=VMEM_LIMIT_BYTES)


def _div(n, cap, mult):
    best = None
    for d in range(mult, min(n, cap) + 1, mult):
        if n % d == 0:
            best = d
    return best if best is not None else n


def _sds(shape, dtype):
    return jax.ShapeDtypeStruct(tuple(shape), dtype)


def _flip(v, bit):
    return 1 - v if bit else v


def _all_gather8(name, x_shard, in_vmem):
    m_per, n = x_shard.shape
    space = pltpu.VMEM if in_vmem else pltpu.HBM

    def body(x_ref, out_ref, send_sems, recv_sems, local_sem):
        x, y, c = lax.axis_index("x"), lax.axis_index("y"), lax.axis_index("c")
        me, sibling = (x, y, c), (x, y, 1 - c)
        chips = [(1 - x, y), (x, 1 - y), (1 - x, 1 - y)]

        def rows(px, py, pc_):
            return out_ref.at[pl.ds((4 * px + 2 * py + pc_) * m_per, m_per), :]

        def copy(k, block, to, src=None):
            return pltpu.make_async_remote_copy(
                src_ref=rows(*block) if src is None else src, dst_ref=rows(*block),
                send_sem=send_sems.at[k], recv_sem=recv_sems.at[k], device_id=to, device_id_type=MESH)

        mine = pltpu.make_async_copy(x_ref, rows(*me), local_sem)
        mine.start()
        first = [copy(0, me, sibling, src=x_ref)]
        first += [copy(1 + j, me, (*chip, c), src=x_ref) for j, chip in enumerate(chips)]
        for cp in first:
            cp.start()
        passed = [copy(4 + j, (*chip, c), sibling) for j, chip in enumerate(chips)]
        for j, chip in enumerate(chips):
            copy(1 + j, (*chip, c), me).wait_recv()
            passed[j].start()
        copy(0, sibling, me).wait_recv()
        for j, chip in enumerate(chips):
            copy(4 + j, (*chip, 1 - c), me).wait_recv()
        for cp in first + passed:
            cp.wait_send()
        mine.wait()

    return _pc(
        body, name=name, out_shape=_sds((N_DEV * m_per, n), x_shard.dtype),
        in_specs=[pl.BlockSpec(memory_space=space)], out_specs=pl.BlockSpec(memory_space=space),
        scratch_shapes=[pltpu.SemaphoreType.DMA((7,)), pltpu.SemaphoreType.DMA((7,)), pltpu.SemaphoreType.DMA],
        compiler_params=pltpu.CompilerParams(vmem_limit_bytes=VMEM_LIMIT_BYTES),
    )(x_shard)


def _gather_weights(name, shards):
    n_t = len(shards)
    halves = [s.shape[0] // 2 for s in shards]

    def body(*refs):
        x_refs, o_refs = refs[:n_t], refs[n_t:2 * n_t]
        send_sems, recv_sems, local_sems = refs[2 * n_t:]
        x, y, c = lax.axis_index("x"), lax.axis_index("y"), lax.axis_index("c")
        me, sibling = (x, y, c), (x, y, 1 - c)
        chips = [(1 - x, y), (x, 1 - y), (1 - x, 1 - y)]

        def slot(t, px, py, pc_):
            return o_refs[t].at[4 * px + 2 * py + pc_]

        def my_rows(t):
            return x_refs[t].at[pl.ds(c * halves[t], halves[t]), :]

        def copy(t, k, block, to, src=None):
            return pltpu.make_async_remote_copy(
                src_ref=slot(t, *block) if src is None else src, dst_ref=slot(t, *block),
                send_sem=send_sems.at[t, k], recv_sem=recv_sems.at[t, k], device_id=to, device_id_type=MESH)

        local = [pltpu.make_async_copy(my_rows(t), slot(t, *me), local_sems.at[t]) for t in range(n_t)]
        for cp in local:
            cp.start()
        first = []
        for t in range(n_t):
            first += [copy(t, 1 + j, me, (*chip, c), src=my_rows(t)) for j, chip in enumerate(chips)]
            first.append(copy(t, 0, me, sibling, src=my_rows(t)))
        for cp in first:
            cp.start()
        passed = []
        for j, chip in enumerate(chips):
            for t in range(n_t):
                copy(t, 1 + j, (*chip, c), me).wait_recv()
                passed.append(copy(t, 4 + j, (*chip, c), sibling))
                passed[-1].start()
        for t in range(n_t):
            copy(t, 0, sibling, me).wait_recv()
        for j, chip in enumerate(chips):
            for t in range(n_t):
                copy(t, 4 + j, (*chip, 1 - c), me).wait_recv()
        for cp in first + passed:
            cp.wait_send()
        for cp in local:
            cp.wait()

    hbm = pl.BlockSpec(memory_space=pltpu.HBM)
    return _pc(
        body, name=name, out_shape=tuple(_sds((N_DEV, h, s.shape[1]), s.dtype) for h, s in zip(halves, shards)),
        in_specs=[hbm] * n_t, out_specs=(hbm,) * n_t,
        scratch_shapes=[pltpu.SemaphoreType.DMA((n_t, 7)), pltpu.SemaphoreType.DMA((n_t, 7)),
                        pltpu.SemaphoreType.DMA((n_t,))],
    )(*shards)


def _scatter_copies(u_refs, r_refs, send_sems, recv_sems):
    x, y, c = lax.axis_index("x"), lax.axis_index("y"), lax.axis_index("c")
    copies = []
    for k in range(1, N_DEV):
        px, py, pcc = _flip(x, (k >> 2) & 1), _flip(y, (k >> 1) & 1), _flip(c, k & 1)
        for t, (u_ref, r_ref) in enumerate(zip(u_refs, r_refs)):
            h = u_ref.shape[1] // 2
            copies.append(pltpu.make_async_remote_copy(
                src_ref=u_ref.at[2 * px + py, pl.ds(pcc * h, h), :], dst_ref=r_ref.at[k - 1],
                send_sem=send_sems.at[t, k - 1], recv_sem=recv_sems.at[t, k - 1],
                device_id=(px, py, pcc), device_id_type=MESH))
    return copies


def _scatter_shapes(units):
    return tuple(_sds((N_DEV - 1, u.shape[1] // 2, u.shape[2]), u.dtype) for u in units)


def _scatter_grads(name, units):
    n_u = len(units)

    def body(*refs):
        copies = _scatter_copies(refs[:n_u], refs[n_u:2 * n_u], refs[2 * n_u], refs[2 * n_u + 1])
        for cp in copies:
            cp.start()
        for cp in copies:
            cp.wait()

    hbm = pl.BlockSpec(memory_space=pltpu.HBM)
    return _pc(body, name=name, out_shape=_scatter_shapes(units), in_specs=[hbm] * n_u, out_specs=(hbm,) * n_u,
               scratch_shapes=[pltpu.SemaphoreType.DMA((n_u, 7)), pltpu.SemaphoreType.DMA((n_u, 7))])(*units)


class _Scatter:
    peers = N_DEV - 1
    shapes = staticmethod(_scatter_shapes)

    @staticmethod
    def copies(u_refs, r_refs, send_sems, recv_sems):
        both = _scatter_copies(u_refs, r_refs, send_sems, recv_sems)
        return both, both


class _Fetch:
    peers = N_CHIP - 1

    @staticmethod
    def shapes(units):
        return tuple(_sds((N_CHIP,) + u.shape, u.dtype) for u in units)

    @staticmethod
    def copies(u_refs, r_refs, send_sems, recv_sems):
        x, y, c = lax.axis_index("x"), lax.axis_index("y"), lax.axis_index("c")
        sends, recvs = [], []
        for k in range(1, N_CHIP):
            px, py = _flip(x, (k >> 1) & 1), _flip(y, k & 1)
            for t, (u_ref, r_ref) in enumerate(zip(u_refs, r_refs)):
                sends.append(pltpu.make_async_remote_copy(
                    src_ref=u_ref, dst_ref=r_ref.at[2 * x + y], send_sem=send_sems.at[t, k - 1],
                    recv_sem=recv_sems.at[t, k - 1], device_id=(px, py, c), device_id_type=MESH))
                recvs.append(pltpu.make_async_remote_copy(
                    src_ref=u_ref, dst_ref=r_ref.at[2 * px + py], send_sem=send_sems.at[t, k - 1],
                    recv_sem=recv_sems.at[t, k - 1], device_id=(px, py, c), device_id_type=MESH))
        return sends, recvs


def _pc_cargo(body, cargo, *, name, grid, in_specs, out_specs, out_shape, scratch_shapes=(), route=_Scatter):
    out_specs, out_shape = tuple(out_specs), tuple(out_shape)
    if not cargo:
        return lambda *args: (_pc(body, name=name, grid=grid, in_specs=list(in_specs), out_specs=out_specs,
                                  out_shape=out_shape, scratch_shapes=list(scratch_shapes),
                                  compiler_params=_cp(*["arbitrary"] * len(grid)))(*args), ())
    n_in, n_out, n_u, n_s = len(in_specs), len(out_specs), len(cargo), len(scratch_shapes)

    def wrapped(*refs):
        ins, u_refs = refs[:n_in], refs[n_in:n_in + n_u]
        outs = refs[n_in + n_u:n_in + n_u + n_out]
        r_refs = refs[n_in + n_u + n_out:n_in + 2 * n_u + n_out]
        scratch = refs[n_in + 2 * n_u + n_out:n_in + 2 * n_u + n_out + n_s]
        send_sems, recv_sems = refs[-2:]
        first = last = None
        for axis, extent in enumerate(grid):
            at_start, at_end = pl.program_id(axis) == 0, pl.program_id(axis) == extent - 1
            first = at_start if first is None else first & at_start
            last = at_end if last is None else last & at_end

        @pl.when(first)
        def _():
            sends, _ = route.copies(u_refs, r_refs, send_sems, recv_sems)
            for cp in sends:
                cp.start()

        body(*ins, *outs, *scratch)

        @pl.when(last)
        def _():
            sends, recvs = route.copies(u_refs, r_refs, send_sems, recv_sems)
            for cp in recvs:
                cp.wait_recv()
            for cp in sends:
                cp.wait_send()

    hbm = pl.BlockSpec(memory_space=pltpu.HBM)
    sems = pltpu.SemaphoreType.DMA((n_u, route.peers))
    call = _pc(wrapped, name=name, grid=grid, in_specs=list(in_specs) + [hbm] * n_u, out_specs=out_specs + (hbm,) * n_u,
               out_shape=out_shape + route.shapes(cargo), scratch_shapes=list(scratch_shapes) + [sems, sems],
               compiler_params=_cp(*["arbitrary"] * len(grid)))

    def run(*args):
        res = call(*args, *cargo)
        return tuple(res[:n_out]), tuple(res[n_out:])
    return run


def _swap_halves(name, bufs):
    n_u = len(bufs)

    def body(*refs):
        o_refs = refs[n_u:2 * n_u]
        send_sems, recv_sems = refs[2 * n_u:]
        x, y, c = lax.axis_index("x"), lax.axis_index("y"), lax.axis_index("c")

        def rows(u, core):
            h = bufs[u].shape[0] // 2
            return o_refs[u].at[pl.ds(core * h, h), :]

        sends = [pltpu.make_async_remote_copy(src_ref=rows(u, c), dst_ref=rows(u, c), send_sem=send_sems.at[u],
                                              recv_sem=recv_sems.at[u], device_id=(x, y, 1 - c), device_id_type=MESH)
                 for u in range(n_u)]
        recvs = [pltpu.make_async_remote_copy(src_ref=rows(u, c), dst_ref=rows(u, 1 - c), send_sem=send_sems.at[u],
                                              recv_sem=recv_sems.at[u], device_id=(x, y, 1 - c), device_id_type=MESH)
                 for u in range(n_u)]
        for cp in sends:
            cp.start()
        for cp in recvs:
            cp.wait_recv()
        for cp in sends:
            cp.wait_send()

    hbm = pl.BlockSpec(memory_space=pltpu.HBM)
    return _pc(
        body, name=name, out_shape=tuple(_sds(b.shape, b.dtype) for b in bufs), in_specs=[hbm] * n_u,
        out_specs=(hbm,) * n_u, input_output_aliases={u: u for u in range(n_u)},
        scratch_shapes=[pltpu.SemaphoreType.DMA((n_u,)), pltpu.SemaphoreType.DMA((n_u,))],
    )(*bufs)


def _sum8_into_half(name, unit, received, chip_core):
    _, h, n = received.shape
    tm = _div(h, 256, 16)
    per = h // tm

    def body(cc_ref, u_ref, p_ref, o_ref):
        acc = u_ref[0].astype(F32)
        for s in range(N_DEV - 1):
            acc = acc + p_ref[s].astype(F32)
        o_ref[...] = acc

    grid_spec = pltpu.PrefetchScalarGridSpec(
        num_scalar_prefetch=1, grid=(per,),
        in_specs=[pl.BlockSpec((1, tm, n), lambda i, cc_ref: (cc_ref[0], cc_ref[1] * per + i, 0)),
                  pl.BlockSpec((N_DEV - 1, tm, n), lambda i, cc_ref: (0, i, 0))],
        out_specs=pl.BlockSpec((tm, n), lambda i, cc_ref: (cc_ref[1] * per + i, 0)))
    return _pc(body, name=name, grid_spec=grid_spec, out_shape=_sds((2 * h, n), F32),
               compiler_params=_cp("arbitrary"))(chip_core, unit, received)


def _sum8(name, parts):
    _, m, n = parts.shape
    tm = _div(m, 256, SUBLANE)

    def body(p_ref, o_ref):
        acc = p_ref[0]
        for s in range(1, N_DEV):
            acc = acc + p_ref[s]
        o_ref[...] = acc

    return _pc(body, name=name, grid=(m // tm,), out_shape=_sds((m, n), F32),
               in_specs=[pl.BlockSpec((N_DEV, tm, n), lambda i: (0, i, 0))],
               out_specs=pl.BlockSpec((tm, n), lambda i: (i, 0)), compiler_params=_cp("parallel"))(parts)


def _pack_rows(arrays, dtype, row_mult):
    flat, spans, off = [], [], 0
    for a in arrays:
        flat.append(a.reshape(-1).astype(dtype))
        spans.append((off, a.shape))
        off += a.size
    quantum = row_mult * PACK_COLS
    total = -(-off // quantum) * quantum
    if total > off:
        flat.append(jnp.zeros((total - off,), dtype))
    return jnp.concatenate(flat).reshape(total // PACK_COLS, PACK_COLS), spans


def _size(shape):
    n = 1
    for s in shape:
        n *= s
    return n


def _join_chips(blocks, axis):
    return jnp.concatenate([blocks[j] for j in range(N_CHIP)], axis=axis)


def _cols_by_chip(g):
    k, n = g.shape
    return jnp.transpose(g.reshape(k, N_CHIP, n // N_CHIP), (1, 0, 2))


def _mm_nn(name, pairs, out_dtype, cargo=(), route=_Scatter, tm_cap=2048, tn_cap=1536):
    m = pairs[0][0].shape[0]
    nb, _, n4 = pairs[0][1][0].shape
    tm, tn = _div(m, tm_cap, 16), _div(n4, tn_cap, LANE)
    per = n4 // tn
    n_pairs = len(pairs)

    def body(*refs):
        o_ref = refs[-1]
        acc = jnp.dot(refs[0][...], refs[1][0], preferred_element_type=F32)
        for i in range(1, n_pairs):
            acc = acc + jnp.dot(refs[2 * i][...], refs[2 * i + 1][0], preferred_element_type=F32)
        o_ref[...] = acc.astype(o_ref.dtype)

    in_specs, args = [], []
    for a, (w, r) in pairs:
        k = a.shape[1]
        assert w.shape[0] == nb and w.shape[2] == n4 and w.shape[1] % k == 0
        in_specs += [pl.BlockSpec((tm, k), lambda j, i: (i, 0)),
                     pl.BlockSpec((1, k, tn), functools.partial(lambda j, i, r_: (j // per, r_, j % per), r_=r))]
        args += [a, w]
    if cargo:
        (out,), received = _pc_cargo(body, cargo, name=name, grid=(nb * per, m // tm), in_specs=in_specs, route=route,
                                     out_shape=[_sds((m, nb * n4), out_dtype)],
                                     out_specs=[pl.BlockSpec((tm, tn), lambda j, i: (i, j))])(*args)
        return out, received
    return _pc(body, name=name, grid=(nb * per, m // tm), out_shape=_sds((m, nb * n4), out_dtype), in_specs=in_specs,
               out_specs=pl.BlockSpec((tm, tn), lambda j, i: (i, j)), compiler_params=_cp("parallel", "parallel"))(*args)


def _mm_tn(name, x, ys, n_blocks=1, out_dtype=F32, cargo=(), tt_cap=1024):
    t, k = x.shape
    widths = [y.shape[1] for y in ys]
    n4 = sum(widths) // n_blocks
    common = n4
    for w in widths:
        common = _gcd(common, w)
    tk, tn, tt = _div(k, 1536, LANE), _div(common, 1536, LANE), _div(t, tt_cap, 16)
    per = n4 // tn
    starts, acc_w = [], 0
    for w in widths:
        starts.append(acc_w // tn)
        acc_w += w
    counts = [w // tn for w in widths]
    n_y = len(ys)

    def active(i, j):
        return (j >= starts[i]) & (j < starts[i] + counts[i])

    n_t = t // tt

    def body(*refs):
        x_ref, y_refs, o_ref, acc_ref = refs[0], refs[1:1 + n_y], refs[-2], refs[-1]
        j = pl.program_id(1)

        @pl.when(pl.program_id(2) == 0)
        def _():
            acc_ref[...] = jnp.zeros_like(acc_ref)

        for i in range(n_y):
            @pl.when(active(i, j))
            def _():
                acc_ref[...] += lax.dot_general(x_ref[...], y_refs[i][...], (((0,), (0,)), ((), ())),
                                                preferred_element_type=F32)

        @pl.when(pl.program_id(2) == n_t - 1)
        def _():
            o_ref[0] = acc_ref[...].astype(o_ref.dtype)

    def y_spec(i):
        def index(a, j, s):
            on = active(i, j)
            return jnp.where(on, s, 0), jnp.where(on, j - starts[i], 0)
        return pl.BlockSpec((tt, tn), index)

    (out,), received = _pc_cargo(
        body, cargo, name=name, grid=(k // tk, n_blocks * per, n_t), out_shape=[_sds((n_blocks, k, n4), out_dtype)],
        in_specs=[pl.BlockSpec((tt, tk), lambda a, j, s: (s, a))] + [y_spec(i) for i in range(n_y)],
        out_specs=[pl.BlockSpec((1, tk, tn), lambda a, j, s: (j // per, a, j % per))],
        scratch_shapes=[pltpu.VMEM((tk, tn), F32)])(x, *ys)
    return (out, received) if cargo else out


def _gcd(a, b):
    while b:
        a, b = b, a % b
    return a


def _w2(w):
    return (w[None], 0)


def _tok_spec(ts, d):
    return pl.BlockSpec((1, ts, d), lambda b, i: (b, i, 0))


def _seq_spec(d):
    return pl.BlockSpec((1, 1, d), lambda b, i: (b, 0, 0))


def _vec_spec(d):
    return pl.BlockSpec((1, d), lambda b, i: (0, 0))


def _ln_stats(z):
    mu = jnp.mean(z, axis=-1, keepdims=True)
    zc = z - mu
    var = jnp.mean(zc * zc, axis=-1, keepdims=True)
    rstd = lax.rsqrt(var + LN_EPS)
    return zc * rstd, rstd


def _modulate(name, x, sc, sh):
    b, s, d = x.shape
    ts = _div(s, 512, 16)

    def body(x_ref, sc_ref, sh_ref, u_ref):
        u_ref[0] = (x_ref[0] * (1.0 + sc_ref[0]) + sh_ref[0]).astype(BF16)

    return _pc(body, name=name, grid=(b, s // ts), out_shape=_sds(x.shape, BF16),
               in_specs=[_tok_spec(ts, d), _seq_spec(d), _seq_spec(d)], out_specs=_tok_spec(ts, d),
               compiler_params=_cp("parallel", "parallel"))(x, sc, sh)


def _ln_mod_fwd(name, alpha, x, y, g, lng, lnb, sc, sh):
    b, s, d = x.shape
    ts = _div(s, 512, 16)

    def body(x_ref, y_ref, g_ref, lng_ref, lnb_ref, sc_ref, sh_ref, z_ref, xn_ref, u_ref):
        z = alpha * x_ref[0] + (1.0 + g_ref[0]) * y_ref[0]
        xhat, _ = _ln_stats(z)
        xn = xhat * lng_ref[...] + lnb_ref[...]
        z_ref[0] = z
        xn_ref[0] = xn
        u_ref[0] = (xn * (1.0 + sc_ref[0]) + sh_ref[0]).astype(BF16)

    tok, seq, vec = _tok_spec(ts, d), _seq_spec(d), _vec_spec(d)
    return _pc(body, name=name, grid=(b, s // ts),
               out_shape=(_sds(x.shape, F32), _sds(x.shape, F32), _sds(x.shape, BF16)),
               in_specs=[tok, tok, seq, vec, vec, seq, seq], out_specs=(tok, tok, tok),
               compiler_params=_cp("parallel", "parallel"))(x, y, g, lng, lnb, sc, sh)


def _ln_loss_fwd(name, alpha, x, y, g, lng, lnb, target):
    b, s, d = x.shape
    ts = _div(s, 512, 16)

    def body(x_ref, y_ref, g_ref, lng_ref, lnb_ref, t_ref, z_ref, ct_ref, loss_ref):
        @pl.when((pl.program_id(0) == 0) & (pl.program_id(1) == 0))
        def _():
            loss_ref[...] = jnp.zeros_like(loss_ref)
        z = alpha * x_ref[0] + (1.0 + g_ref[0]) * y_ref[0]
        xhat, _ = _ln_stats(z)
        err = xhat * lng_ref[...] + lnb_ref[...] - t_ref[0]
        z_ref[0] = z
        ct_ref[0] = err / d
        part = 0.5 * jnp.sum(jnp.mean(err * err, axis=-1, keepdims=True))
        loss_ref[...] += jnp.full(loss_ref.shape, part, F32)

    tok, seq, vec = _tok_spec(ts, d), _seq_spec(d), _vec_spec(d)
    return _pc(body, name=name, grid=(b, s // ts),
               out_shape=(_sds(x.shape, F32), _sds(x.shape, F32), _sds((SUBLANE, LANE), F32)),
               in_specs=[tok, tok, seq, vec, vec, tok],
               out_specs=(tok, tok, pl.BlockSpec((SUBLANE, LANE), lambda b, i: (0, 0))),
               compiler_params=_cp("arbitrary", "arbitrary"))(x, y, g, lng, lnb, target)


def _sub_bwd(name, alpha, upstream, z, y, g, lng):
    b, s, d = z.shape
    ts = _div(s, 512, 16)
    last = len(upstream) == 1

    def body(*refs):
        if last:
            ct_ref, z_ref, y_ref, g_ref, lng_ref, dz_ref, dy_ref, dg_ref, dlng_ref, dlnb_ref = refs
        else:
            (dzn_ref, dun_ref, lnb_ref, scn_ref, z_ref, y_ref, g_ref, lng_ref,
             dz_ref, dy_ref, dg_ref, dlng_ref, dlnb_ref, dsc_ref, dsh_ref) = refs
        first_tile = pl.program_id(1) == 0

        @pl.when(first_tile & (pl.program_id(0) == 0))
        def _():
            dlng_ref[...] = jnp.zeros_like(dlng_ref)
            dlnb_ref[...] = jnp.zeros_like(dlnb_ref)

        @pl.when(first_tile)
        def _():
            dg_ref[...] = jnp.zeros_like(dg_ref)
            if not last:
                dsc_ref[...] = jnp.zeros_like(dsc_ref)
                dsh_ref[...] = jnp.zeros_like(dsh_ref)

        xhat, rstd = _ln_stats(z_ref[0])
        if last:
            ct = ct_ref[0]
        else:
            dun = dun_ref[0]
            ct = alpha * dzn_ref[0] + dun * (1.0 + scn_ref[0])
            xn = xhat * lng_ref[...] + lnb_ref[...]
            dsc_ref[0] += jnp.sum(dun * xn, axis=0, keepdims=True)
            dsh_ref[0] += jnp.sum(dun, axis=0, keepdims=True)
        dlng_ref[...] += jnp.sum(ct * xhat, axis=0, keepdims=True)
        dlnb_ref[...] += jnp.sum(ct, axis=0, keepdims=True)
        dxhat = ct * lng_ref[...]
        dz = rstd * (dxhat - jnp.mean(dxhat, axis=-1, keepdims=True)
                     - xhat * jnp.mean(dxhat * xhat, axis=-1, keepdims=True))
        dz_ref[0] = dz
        dy_ref[0] = ((1.0 + g_ref[0]) * dz).astype(BF16)
        dg_ref[0] += jnp.sum(dz * y_ref[0], axis=0, keepdims=True)

    tok, seq, vec = _tok_spec(ts, d), _seq_spec(d), _vec_spec(d)
    seq_out = _sds((b, 1, d), F32)
    out_shape = [_sds(z.shape, F32), _sds(z.shape, BF16), seq_out, _sds((1, d), F32), _sds((1, d), F32)]
    out_specs = [tok, tok, seq, vec, vec]
    if last:
        in_specs = [tok, tok, tok, seq, vec]
    else:
        in_specs = [tok, tok, vec, seq, tok, tok, seq, vec]
        out_shape += [seq_out, seq_out]
        out_specs += [seq, seq]
    return _pc(body, name=name, grid=(b, s // ts), out_shape=tuple(out_shape), in_specs=in_specs,
               out_specs=tuple(out_specs), compiler_params=_cp("arbitrary", "arbitrary"))(*upstream, z, y, g, lng)


def _input_bwd(name, alpha, dz, du, x, sc):
    b, s, d = x.shape
    ts = _div(s, 512, 16)

    def body(dz_ref, du_ref, x_ref, sc_ref, gx_ref, dsc_ref, dsh_ref):
        @pl.when(pl.program_id(1) == 0)
        def _():
            dsc_ref[...] = jnp.zeros_like(dsc_ref)
            dsh_ref[...] = jnp.zeros_like(dsh_ref)
        du_ = du_ref[0]
        gx_ref[0] = alpha * dz_ref[0] + du_ * (1.0 + sc_ref[0])
        dsc_ref[0] += jnp.sum(du_ * x_ref[0], axis=0, keepdims=True)
        dsh_ref[0] += jnp.sum(du_, axis=0, keepdims=True)

    tok, seq = _tok_spec(ts, d), _seq_spec(d)
    seq_out = _sds((b, 1, d), F32)
    return _pc(body, name=name, grid=(b, s // ts), out_shape=(_sds(x.shape, F32), seq_out, seq_out),
               in_specs=[tok, tok, tok, seq], out_specs=(tok, seq, seq),
               compiler_params=_cp("parallel", "arbitrary"))(dz, du, x, sc)


def _rows_iota(shape):
    return lax.broadcasted_iota(jnp.int32, shape, 0)


def _back(v, k):
    return pltpu.roll(v, k, axis=0)


def _ahead(v, k):
    return pltpu.roll(v, v.shape[0] - k, axis=0)


def _conv3(ext, w_ref):
    return w_ref[2:3, :] * ext + w_ref[1:2, :] * _back(ext, 1) + w_ref[0:1, :] * _back(ext, 2)


def _conv3_t(dh_ext, w_ref):
    return w_ref[2:3, :] * dh_ext + w_ref[1:2, :] * _ahead(dh_ext, 1) + w_ref[0:1, :] * _ahead(dh_ext, 2)


def _flag(cond):
    return jnp.where(cond, 1.0, 0.0).astype(F32)


def _sigmoid(v):
    return 1.0 / (1.0 + jnp.exp(-v))


def _halo_specs(ts, tc, halo, n_s, col):
    per = ts // halo
    tile = pl.BlockSpec((1, ts, tc), lambda b, i, j: (b, i, col(j)))
    prev = pl.BlockSpec((1, halo, tc), lambda b, i, j: (b, jnp.maximum(i * per - 1, 0), col(j)))
    nxt = pl.BlockSpec((1, halo, tc), lambda b, i, j: (b, jnp.minimum((i + 1) * per, n_s * per - 1), col(j)))
    return tile, prev, nxt


def _convglu_fwd(name, p, cw, cb, cargo=()):
    b, s, f2 = p.shape
    f = f2 // 2
    ts, tc = _div(s, CONV_ROWS, CONV_HALO), _div(f, 256, LANE)
    n_s, n_c = s // ts, f // tc

    def body(pv_ref, pvh_ref, pg_ref, pgh_ref, wv_ref, wg_ref, bv_ref, bg_ref, a_ref):
        keep = _flag(pl.program_id(1) > 0)

        def conv(t_ref, h_ref, w_ref, b_ref):
            ext = jnp.concatenate([h_ref[0] * keep, t_ref[0]], axis=0)
            return _conv3(ext, w_ref)[CONV_HALO:] + b_ref[...]

        val = conv(pv_ref, pvh_ref, wv_ref, bv_ref)
        gate = conv(pg_ref, pgh_ref, wg_ref, bg_ref)
        a_ref[0] = (gate * _sigmoid(gate) * val).astype(BF16)

    tv, hv, _ = _halo_specs(ts, tc, CONV_HALO, n_s, lambda j: j)
    tg, hg, _ = _halo_specs(ts, tc, CONV_HALO, n_s, lambda j: j + n_c)
    wv = pl.BlockSpec((3, tc), lambda b, i, j: (0, j))
    wg = pl.BlockSpec((3, tc), lambda b, i, j: (0, j + n_c))
    bv = pl.BlockSpec((1, tc), lambda b, i, j: (0, j))
    bg = pl.BlockSpec((1, tc), lambda b, i, j: (0, j + n_c))
    (act,), fetched = _pc_cargo(
        body, cargo, name=name, grid=(b, n_s, n_c), route=_Fetch, out_shape=[_sds((b, s, f), BF16)],
        in_specs=[tv, hv, tg, hg, wv, wg, bv, bg],
        out_specs=[pl.BlockSpec((1, ts, tc), lambda b, i, j: (b, i, j))])(p, p, p, p, cw, cw, cb, cb)
    return act, fetched


def _convglu_bwd(name, p, da, cw, cb, cargo=()):
    b, s, f2 = p.shape
    f = f2 // 2
    ts, tc = _div(s, CONV_ROWS, CONV_HALO), _div(f, 256, LANE)
    n_s, n_c = s // ts, f // tc

    def body(pv_ref, pvp_ref, pvn_ref, pg_ref, pgp_ref, pgn_ref, da_ref, dan_ref, wv_ref, wg_ref, bv_ref, bg_ref,
             dpv_ref, dpg_ref, dwv_ref, dwg_ref, dbv_ref, dbg_ref):
        bi, i = pl.program_id(1), pl.program_id(2)

        @pl.when((bi == 0) & (i == 0))
        def _():
            for r in (dwv_ref, dwg_ref, dbv_ref, dbg_ref):
                r[...] = jnp.zeros_like(r)

        keep_prev = _flag(i > 0)
        keep_next = _flag(i < n_s - 1)
        pv_ext = jnp.concatenate([pvp_ref[0] * keep_prev, pv_ref[0], pvn_ref[0]], axis=0)
        pg_ext = jnp.concatenate([pgp_ref[0] * keep_prev, pg_ref[0], pgn_ref[0]], axis=0)
        taps_v = (_back(pv_ext, 2), _back(pv_ext, 1), pv_ext)
        taps_g = (_back(pg_ext, 2), _back(pg_ext, 1), pg_ext)

        def conv(taps, w_ref, b_ref):
            return (w_ref[2:3, :] * taps[2] + w_ref[1:2, :] * taps[1] + w_ref[0:1, :] * taps[0])[CONV_HALO:] + b_ref[...]

        val, gate = conv(taps_v, wv_ref, bv_ref), conv(taps_g, wg_ref, bg_ref)
        da_ext = jnp.concatenate([da_ref[0], dan_ref[0] * keep_next], axis=0)
        sg = _sigmoid(gate)
        dval = da_ext * gate * sg
        dgate = da_ext * val * (sg * (1.0 + gate * (1.0 - sg)))
        dpv_ref[0] = _conv3_t(dval, wv_ref)[:ts].astype(BF16)
        dpg_ref[0] = _conv3_t(dgate, wg_ref)[:ts].astype(BF16)
        for dh, taps, dw_ref, db_ref in ((dval[:ts], taps_v, dwv_ref, dbv_ref), (dgate[:ts], taps_g, dwg_ref, dbg_ref)):
            db_ref[...] += jnp.sum(dh, axis=0, keepdims=True)
            for k in range(3):
                dw_ref[k:k + 1, :] += jnp.sum(dh * taps[k][CONV_HALO:CONV_HALO + ts], axis=0, keepdims=True)

    def specs(col):
        per = ts // CONV_HALO
        tile = pl.BlockSpec((1, ts, tc), lambda j, b, i: (b, i, col(j)))
        prev = pl.BlockSpec((1, CONV_HALO, tc), lambda j, b, i: (b, jnp.maximum(i * per - 1, 0), col(j)))
        nxt = pl.BlockSpec((1, CONV_HALO, tc), lambda j, b, i: (b, jnp.minimum((i + 1) * per, n_s * per - 1), col(j)))
        return tile, prev, nxt

    tv, pvp, pvn = specs(lambda j: j)
    tg, pgp, pgn = specs(lambda j: j + n_c)
    wv = pl.BlockSpec((3, tc), lambda j, b, i: (0, j))
    wg = pl.BlockSpec((3, tc), lambda j, b, i: (0, j + n_c))
    bv = pl.BlockSpec((1, tc), lambda j, b, i: (0, j))
    bg = pl.BlockSpec((1, tc), lambda j, b, i: (0, j + n_c))
    out_tile = pl.BlockSpec((1, ts, tc), lambda j, b, i: (b, i, j))
    acc3, acc1 = pl.BlockSpec((3, tc), lambda j, b, i: (0, j)), pl.BlockSpec((1, tc), lambda j, b, i: (0, j))
    (dpv, dpg, dwv, dwg, dbv, dbg), received = _pc_cargo(
        body, cargo, name=name, grid=(n_c, b, n_s),
        out_shape=(_sds((b, s, f), BF16), _sds((b, s, f), BF16), _sds((3, f), F32), _sds((3, f), F32),
                   _sds((1, f), F32), _sds((1, f), F32)),
        in_specs=[tv, pvp, pvn, tg, pgp, pgn, tv, pvn, wv, wg, bv, bg],
        out_specs=(out_tile, out_tile, acc3, acc3, acc1, acc1))(p, p, p, p, p, p, da, da, cw, cw, cb, cb)
    return dpv, dpg, jnp.concatenate([dwv, dwg], axis=1), jnp.concatenate([dbv, dbg], axis=1), received


def _shortconv_fwd(name, q, cw):
    b, s, d3 = q.shape
    d = d3 // 3
    ts, tc = _div(s, CONV_ROWS, CONV_HALO), _div(d, 256, LANE)
    n_s, n_c = s // ts, d // tc

    def body(gb_ref, gc_ref, gch_ref, h_ref, hh_ref, w_ref, r_ref):
        keep = _flag(pl.program_id(1) > 0)
        m_ext = jnp.concatenate([gch_ref[0] * hh_ref[0] * keep, gc_ref[0] * h_ref[0]], axis=0)
        r_ref[0] = (gb_ref[0] * _conv3(m_ext, w_ref)[CONV_HALO:]).astype(BF16)

    tb, _, _ = _halo_specs(ts, tc, CONV_HALO, n_s, lambda j: j)
    tcc, hc, _ = _halo_specs(ts, tc, CONV_HALO, n_s, lambda j: j + n_c)
    th, hh, _ = _halo_specs(ts, tc, CONV_HALO, n_s, lambda j: j + 2 * n_c)
    w = pl.BlockSpec((3, tc), lambda b, i, j: (0, j))
    return _pc(body, name=name, grid=(b, n_s, n_c), out_shape=_sds((b, s, d), BF16),
               in_specs=[tb, tcc, hc, th, hh, w], out_specs=pl.BlockSpec((1, ts, tc), lambda b, i, j: (b, i, j)),
               compiler_params=_cp("parallel", "parallel", "parallel"))(q, q, q, q, q, cw)


def _shortconv_bwd(name, q, dr, cw):
    b, s, d3 = q.shape
    d = d3 // 3
    ts, tc = _div(s, CONV_ROWS, CONV_HALO), _div(d, 256, LANE)
    n_s, n_c = s // ts, d // tc

    def body(gb_ref, gbn_ref, gc_ref, gcp_ref, h_ref, hp_ref, dr_ref, drn_ref, w_ref,
             dgb_ref, dgc_ref, dh_ref, dw_ref):
        bi, i = pl.program_id(1), pl.program_id(2)

        @pl.when((bi == 0) & (i == 0))
        def _():
            dw_ref[...] = jnp.zeros_like(dw_ref)

        keep_prev = _flag(i > 0)
        keep_next = _flag(i < n_s - 1)
        gc, h = gc_ref[0], h_ref[0]
        m_ext = jnp.concatenate([gcp_ref[0] * hp_ref[0] * keep_prev, gc * h], axis=0)
        cm = _conv3(m_ext, w_ref)[CONV_HALO:]
        dr_ = dr_ref[0]
        dgb_ref[0] = (dr_ * cm).astype(BF16)
        dcv_ext = jnp.concatenate([dr_ * gb_ref[0], drn_ref[0] * gbn_ref[0] * keep_next], axis=0)
        dm = _conv3_t(dcv_ext, w_ref)[:ts]
        dgc_ref[0] = (dm * h).astype(BF16)
        dh_ref[0] = (dm * gc).astype(BF16)
        dcv = dcv_ext[:ts]
        for k in range(3):
            shifted = m_ext if k == 2 else _back(m_ext, 2 - k)
            dw_ref[k:k + 1, :] += jnp.sum(dcv * shifted[CONV_HALO:], axis=0, keepdims=True)

    def specs(col):
        per = ts // CONV_HALO
        tile = pl.BlockSpec((1, ts, tc), lambda j, b, i: (b, i, col(j)))
        prev = pl.BlockSpec((1, CONV_HALO, tc), lambda j, b, i: (b, jnp.maximum(i * per - 1, 0), col(j)))
        nxt = pl.BlockSpec((1, CONV_HALO, tc), lambda j, b, i: (b, jnp.minimum((i + 1) * per, n_s * per - 1), col(j)))
        return tile, prev, nxt

    tb, _, nb = specs(lambda j: j)
    tcc, pc_, _ = specs(lambda j: j + n_c)
    th, ph, _ = specs(lambda j: j + 2 * n_c)
    w = pl.BlockSpec((3, tc), lambda j, b, i: (0, j))
    out_tile = pl.BlockSpec((1, ts, tc), lambda j, b, i: (b, i, j))
    o = _sds((b, s, d), BF16)
    return _pc(body, name=name, grid=(n_c, b, n_s), out_shape=(o, o, o, _sds((3, d), F32)),
               in_specs=[tb, nb, tcc, pc_, th, ph, tb, nb, w], out_specs=(out_tile, out_tile, out_tile, w),
               compiler_params=_cp("parallel", "arbitrary", "arbitrary"))(q, q, q, q, q, q, dr, dr, cw)


def _pick_window(group, cands):
    gid = jnp.full(cands[0].shape, group, jnp.int32)
    out = cands[-1]
    for k in range(len(cands) - 2, -1, -1):
        out = jnp.where(gid == k, cands[k], out)
    return out


def _window_sums(v, shift):
    s1 = v + shift(v, 1)
    s2 = s1 + shift(s1, 2)
    s3 = s2 + shift(s2, 4)
    s4 = s3 + shift(s3, 8)
    return [s1, s2, s3, s4]


def _pool_counts(group, first_row, n_rows, cols):
    t = _rows_iota((n_rows, cols)) + first_row
    window = _pick_window(group, [jnp.full((n_rows, cols), 2 << k, jnp.int32) for k in range(POOL_GROUPS)])
    return jnp.minimum(t + 1, window).astype(F32)


def _pool_fwd(name, x, sc, sh, w, scale):
    b, s, d = x.shape
    tc = d // POOL_GROUPS
    ts = _div(s, 512, POOL_HALO)
    n_s = s // ts

    def body(x_ref, xp_ref, sc_ref, sh_ref, w_ref, scale_ref, y_ref):
        i, grp = pl.program_id(1), pl.program_id(2)
        keep = _flag(i > 0)
        mod = 1.0 + sc_ref[0]
        u = x_ref[0] * mod + sh_ref[0]
        u_ext = jnp.concatenate([(xp_ref[0] * mod + sh_ref[0]) * keep, u], axis=0)
        summed = _pick_window(grp, _window_sums(u_ext, _back))[POOL_HALO:]
        pooled = summed / _pool_counts(grp, i * ts, ts, tc) - u
        y_ref[0] = jnp.dot(pooled.astype(BF16), w_ref[0], preferred_element_type=F32) * scale_ref[...]

    tile, prev, _ = _halo_specs(ts, tc, POOL_HALO, n_s, lambda j: j)
    seq = pl.BlockSpec((1, 1, tc), lambda b, i, j: (b, 0, j))
    return _pc(body, name=name, grid=(b, n_s, POOL_GROUPS), out_shape=_sds(x.shape, F32),
               in_specs=[tile, prev, seq, seq, pl.BlockSpec((1, tc, tc), lambda b, i, j: (j, 0, 0)),
                         pl.BlockSpec((1, tc), lambda b, i, j: (0, j))],
               out_specs=pl.BlockSpec((1, ts, tc), lambda b, i, j: (b, i, j)),
               compiler_params=_cp("parallel", "parallel", "parallel"))(x, x, sc, sh, w, scale)


def _pool_bwd(name, x, sc, sh, dy, w, w_t, scale, cargo=()):
    b, s, d = x.shape
    tc = d // POOL_GROUPS
    ts = _div(s, 512, POOL_HALO)
    n_s = s // ts

    def body(x_ref, xp_ref, sc_ref, sh_ref, dy_ref, dyn_ref, w_ref, wt_ref, scale_ref, du_ref, dw_ref, dscale_ref):
        grp, bi, i = pl.program_id(0), pl.program_id(1), pl.program_id(2)

        @pl.when((bi == 0) & (i == 0))
        def _():
            dw_ref[...] = jnp.zeros_like(dw_ref)
            dscale_ref[...] = jnp.zeros_like(dscale_ref)

        keep_prev = _flag(i > 0)
        keep_next = _flag(i < n_s - 1)
        mod = 1.0 + sc_ref[0]
        u = x_ref[0] * mod + sh_ref[0]
        u_ext = jnp.concatenate([(xp_ref[0] * mod + sh_ref[0]) * keep_prev, u], axis=0)
        summed = _pick_window(grp, _window_sums(u_ext, _back))[POOL_HALO:]
        pooled = (summed / _pool_counts(grp, i * ts, ts, tc) - u).astype(BF16)
        dy_ = dy_ref[0].astype(F32)
        ymat = jnp.dot(pooled, w_ref[0], preferred_element_type=F32)
        dscale_ref[...] += jnp.sum(dy_ * ymat, axis=0, keepdims=True)
        dys_ext = (jnp.concatenate([dy_, dyn_ref[0].astype(F32) * keep_next], axis=0) * scale_ref[...]).astype(BF16)
        dw_ref[0] += lax.dot_general(pooled, dys_ext[:ts], (((0,), (0,)), ((), ())), preferred_element_type=F32)
        dpooled = jnp.dot(dys_ext, wt_ref[0], preferred_element_type=F32)
        e = dpooled / _pool_counts(grp, i * ts, ts + POOL_HALO, tc)
        du_ref[0] = _pick_window(grp, _window_sums(e, _ahead))[:ts] - dpooled[:ts]

    per = ts // POOL_HALO
    tile = pl.BlockSpec((1, ts, tc), lambda j, b, i: (b, i, j))
    prev = pl.BlockSpec((1, POOL_HALO, tc), lambda j, b, i: (b, jnp.maximum(i * per - 1, 0), j))
    nxt = pl.BlockSpec((1, POOL_HALO, tc), lambda j, b, i: (b, jnp.minimum((i + 1) * per, n_s * per - 1), j))
    seq = pl.BlockSpec((1, 1, tc), lambda j, b, i: (b, 0, j))
    wsp = pl.BlockSpec((1, tc, tc), lambda j, b, i: (j, 0, 0))
    vec = pl.BlockSpec((1, tc), lambda j, b, i: (0, j))
    (du, dw, dscale), received = _pc_cargo(
        body, cargo, name=name, grid=(POOL_GROUPS, b, n_s),
        out_shape=(_sds(x.shape, F32), _sds((POOL_GROUPS, tc, tc), F32), _sds((1, d), F32)),
        in_specs=[tile, prev, seq, seq, tile, nxt, wsp, wsp, vec],
        out_specs=(tile, wsp, vec))(x, x, sc, sh, dy, dy, w, w_t, scale)
    return du, dw, dscale, received


def _rope_swap(v):
    lane = lax.broadcasted_iota(jnp.int32, v.shape, v.ndim - 1)
    lo, hi = QK_NOPE, QK_NOPE + QK_ROPE // 2
    from_above = pltpu.roll(v, HEAD_PAD - QK_ROPE // 2, axis=v.ndim - 1)
    from_below = pltpu.roll(v, QK_ROPE // 2, axis=v.ndim - 1)
    return jnp.where((lane >= lo) & (lane < hi), from_above,
                     jnp.where((lane >= hi) & (lane < hi + QK_ROPE // 2), from_below, 0.0))


def _rope(v, cos_t, sin_t):
    return v * cos_t + _rope_swap(v) * sin_t


def _rope_t(dv, cos_t, sin_t):
    return dv * cos_t + _rope_swap(dv * sin_t)


def _rms(v, g):
    r = lax.rsqrt(jnp.mean(v * v, axis=-1, keepdims=True) + RMS_EPS)
    return v * r, r


def _mla_norm_fwd(name, a, qn, kvn, cos_t, sin_t):
    b, s, wa = a.shape
    ql, kvl = qn.shape[1], kvn.shape[1]
    ts = _div(s, 512, 16)

    def body(aq_ref, akv_ref, ape_ref, qn_ref, kvn_ref, cos_ref, sin_ref, cq_ref, ckv_ref, kpe_ref):
        yq, _ = _rms(aq_ref[0], None)
        cq_ref[0] = (yq * qn_ref[...]).astype(BF16)
        ykv, _ = _rms(akv_ref[0], None)
        ckv_ref[0] = (ykv * kvn_ref[...]).astype(BF16)
        kpe_ref[0] = _rope(ape_ref[0], cos_ref[0], sin_ref[0])

    tok = lambda w, col: pl.BlockSpec((1, ts, w), lambda b, i: (b, i, col))
    return _pc(body, name=name, grid=(b, s // ts),
               out_shape=(_sds((b, s, ql), BF16), _sds((b, s, kvl), BF16), _sds((b, s, HEAD_PAD), F32)),
               in_specs=[tok(ql, 0), tok(kvl, ql // kvl), tok(HEAD_PAD, (ql + kvl) // HEAD_PAD), _vec_spec(ql),
                         _vec_spec(kvl), tok(HEAD_PAD, 0), tok(HEAD_PAD, 0)],
               out_specs=(tok(ql, 0), tok(kvl, 0), tok(HEAD_PAD, 0)),
               compiler_params=_cp("parallel", "parallel"))(a, a, a, qn, kvn, cos_t, sin_t)


def _mla_norm_bwd(name, a, dcq, dckv, dkpe, qn, kvn):
    b, s, wa = a.shape
    ql, kvl = qn.shape[1], kvn.shape[1]
    ts = _div(s, 512, 16)

    def body(a_ref, dcq_ref, dckv_ref, dkpe_ref, qn_ref, kvn_ref, da_ref, dqn_ref, dkvn_ref):
        @pl.when((pl.program_id(0) == 0) & (pl.program_id(1) == 0))
        def _():
            dqn_ref[...] = jnp.zeros_like(dqn_ref)
            dkvn_ref[...] = jnp.zeros_like(dkvn_ref)

        def one(v, dc, g_ref, dg_ref):
            yv, r = _rms(v, None)
            dg_ref[...] += jnp.sum(dc * yv, axis=0, keepdims=True)
            dyv = dc * g_ref[...]
            return r * (dyv - yv * jnp.mean(dyv * yv, axis=-1, keepdims=True))

        av = a_ref[0]
        da_ref[0, :, 0:ql] = one(av[:, 0:ql], dcq_ref[0], qn_ref, dqn_ref).astype(BF16)
        da_ref[0, :, ql:ql + kvl] = one(av[:, ql:ql + kvl], dckv_ref[0], kvn_ref, dkvn_ref).astype(BF16)
        da_ref[0, :, ql + kvl:] = dkpe_ref[0].astype(BF16)

    return _pc(body, name=name, grid=(b, s // ts),
               out_shape=(_sds(a.shape, BF16), _sds((1, ql), F32), _sds((1, kvl), F32)),
               in_specs=[_tok_spec(ts, wa), _tok_spec(ts, ql), _tok_spec(ts, kvl), _tok_spec(ts, HEAD_PAD),
                         _vec_spec(ql), _vec_spec(kvl)],
               out_specs=(_tok_spec(ts, wa), _vec_spec(ql), _vec_spec(kvl)),
               compiler_params=_cp("arbitrary", "arbitrary"))(a, dcq, dckv, dkpe, qn, kvn)


def _mla_prep_fwd(name, q_raw, k_raw, kpe, cos_t, sin_t, n_heads):
    b, s, wq = q_raw.shape
    ts = _div(s, 256, 16)

    def body(q_ref, k_ref, kpe_ref, cos_ref, sin_ref, qo_ref, ko_ref):
        cos_, sin_, kpe_ = cos_ref[0], sin_ref[0], kpe_ref[0]
        for h in range(n_heads):
            lanes = slice(h * HEAD_PAD, (h + 1) * HEAD_PAD)
            qo_ref[0, :, lanes] = _rope(q_ref[0, :, lanes], cos_, sin_).astype(BF16)
            ko_ref[0, :, lanes] = (k_ref[0, :, lanes] + kpe_).astype(BF16)

    wide = pl.BlockSpec((1, ts, wq), lambda b, i: (b, i, 0))
    tok = pl.BlockSpec((1, ts, HEAD_PAD), lambda b, i: (b, i, 0))
    o = _sds(q_raw.shape, BF16)
    return _pc(body, name=name, grid=(b, s // ts), out_shape=(o, o),
               in_specs=[wide, wide, tok, tok, tok], out_specs=(wide, wide),
               compiler_params=_cp("parallel", "parallel"))(q_raw, k_raw, kpe, cos_t, sin_t)


def _mla_prep_bwd(name, dq, dk, cos_t, sin_t, n_heads):
    b, s, wq = dq.shape
    ts = _div(s, 256, 16)

    def body(dq_ref, dk_ref, cos_ref, sin_ref, dqr_ref, dkr_ref, dkpe_ref):
        cos_, sin_ = cos_ref[0], sin_ref[0]
        dk_sum = None
        for h in range(n_heads):
            lanes = slice(h * HEAD_PAD, (h + 1) * HEAD_PAD)
            dqr_ref[0, :, lanes] = _rope_t(dq_ref[0, :, lanes], cos_, sin_).astype(BF16)
            dk_h = dk_ref[0, :, lanes]
            dkr_ref[0, :, lanes] = dk_h.astype(BF16)
            dk_sum = dk_h if dk_sum is None else dk_sum + dk_h
        dkpe_ref[0] = _rope_t(dk_sum, cos_, sin_)

    wide = pl.BlockSpec((1, ts, wq), lambda b, i: (b, i, 0))
    tok = pl.BlockSpec((1, ts, HEAD_PAD), lambda b, i: (b, i, 0))
    return _pc(body, name=name, grid=(b, s // ts),
               out_shape=(_sds(dq.shape, BF16), _sds(dq.shape, BF16), _sds((b, s, HEAD_PAD), F32)),
               in_specs=[wide, wide, tok, tok], out_specs=(wide, wide, tok),
               compiler_params=_cp("parallel", "parallel"))(dq, dk, cos_t, sin_t)


FLASH_TILE = 1024
LOG2_E = 1.4426950408889634


def _heads_per_step(n_heads):
    return 2 if n_heads % 2 == 0 else 1


def _causal_mask(i, j, tq, tk):
    rows = lax.broadcasted_iota(jnp.int32, (tq, tk), 0) + i * tq
    cols = lax.broadcasted_iota(jnp.int32, (tq, tk), 1) + j * tk
    return cols <= rows


def _nt(a, b):
    return lax.dot_general(a, b, (((1,), (1,)), ((), ())), preferred_element_type=F32)


def _tn(a, b):
    return lax.dot_general(a, b, (((0,), (0,)), ((), ())), preferred_element_type=F32)


def _flash_fwd(name, q, k, v, n_heads, sm_scale, cargo=()):
    b, s, _ = q.shape
    t, hp = _div(s, FLASH_TILE, LANE), _heads_per_step(n_heads)
    n, w = s // t, hp * HEAD_PAD
    neg = float(jnp.finfo(jnp.float32).min)
    c2 = sm_scale * LOG2_E

    def body(q_ref, k_ref, v_ref, o_ref, lse_ref, m_ref, l_ref, acc_ref):
        i, j = pl.program_id(2), pl.program_id(3)

        @pl.when(j == 0)
        def _():
            m_ref[...] = jnp.full(m_ref.shape, neg, F32)
            l_ref[...] = jnp.zeros_like(l_ref)
            acc_ref[...] = jnp.zeros_like(acc_ref)

        def block(on_diagonal):
            for hh in range(hp):
                ln = slice(hh * HEAD_PAD, (hh + 1) * HEAD_PAD)
                sc = _nt(q_ref[0, :, ln], k_ref[0, :, ln])
                if on_diagonal:
                    sc = jnp.where(_causal_mask(i, j, t, t), sc, neg)
                m_old = m_ref[hh]
                m_new = jnp.maximum(m_old, jnp.max(sc, axis=-1, keepdims=True))
                p = jnp.exp2((sc - m_new) * c2)
                corr = jnp.exp2((m_old - m_new) * c2)
                l_ref[hh] = corr * l_ref[hh] + jnp.sum(p, axis=-1, keepdims=True)
                acc_ref[:, ln] = corr * acc_ref[:, ln] + jnp.dot(p.astype(BF16), v_ref[0, :, ln],
                                                                 preferred_element_type=F32)
                m_ref[hh] = m_new

        pl.when(j < i)(functools.partial(block, False))
        pl.when(j == i)(functools.partial(block, True))

        @pl.when(j == n - 1)
        def _():
            for hh in range(hp):
                ln = slice(hh * HEAD_PAD, (hh + 1) * HEAD_PAD)
                o_ref[0, :, ln] = (acc_ref[:, ln] / l_ref[hh]).astype(BF16)
                lse_ref[0, :, ln] = jnp.broadcast_to(m_ref[hh] * sm_scale + jnp.log(l_ref[hh]), (t, HEAD_PAD))

    qs = pl.BlockSpec((1, t, w), lambda b, h, i, j: (b, i, h))
    ks = pl.BlockSpec((1, t, w), lambda b, h, i, j: (b, jnp.minimum(j, i), h))
    (o, lse), fetched = _pc_cargo(
        body, cargo, name=name, grid=(b, n_heads // hp, n, n), route=_Fetch,
        out_shape=(_sds(q.shape, BF16), _sds(q.shape, F32)), in_specs=[qs, ks, ks], out_specs=(qs, qs),
        scratch_shapes=[pltpu.VMEM((hp, t, 1), F32), pltpu.VMEM((hp, t, 1), F32), pltpu.VMEM((t, w), F32)])(q, k, v)
    return o, lse, fetched


def _flash_bwd(name, q, k, v, o, lse, do, n_heads, sm_scale):
    b, s, _ = q.shape
    t, hp = _div(s, FLASH_TILE, LANE), _heads_per_step(n_heads)
    n, w = s // t, hp * HEAD_PAD
    c2 = sm_scale * LOG2_E

    def body(q_ref, k_ref, v_ref, o_ref, lse_ref, do_ref, dq_hbm, dk_ref, dv_ref, dq_acc, dk_acc, dv_acc, dq_sem):
        bi, hi, j, i = pl.program_id(0), pl.program_id(1), pl.program_id(2), pl.program_id(3)

        @pl.when(i == 0)
        def _():
            dk_acc[...] = jnp.zeros_like(dk_acc)
            dv_acc[...] = jnp.zeros_like(dv_acc)

        rows = pl.ds(pl.multiple_of(i * t, t), t)

        def block(on_diagonal):
            for hh in range(hp):
                ln = slice(hh * HEAD_PAD, (hh + 1) * HEAD_PAD)
                do_ = do_ref[0, :, ln]
                delta = jnp.sum(do_.astype(F32) * o_ref[0, :, ln].astype(F32), axis=-1, keepdims=True)
                sc = _nt(q_ref[0, :, ln], k_ref[0, :, ln])
                p = jnp.exp2(sc * c2 - lse_ref[0, :, hh * HEAD_PAD:hh * HEAD_PAD + 1] * LOG2_E)
                if on_diagonal:
                    p = jnp.where(_causal_mask(i, j, t, t), p, 0.0)
                dv_acc[:, ln] += _tn(p.astype(BF16), do_)
                dp = _nt(do_, v_ref[0, :, ln])
                ds = (p * (dp - delta)).astype(BF16)
                dk_acc[:, ln] += _tn(ds, q_ref[0, :, ln])
                dq_part = jnp.dot(ds, k_ref[0, :, ln], preferred_element_type=F32)

                @pl.when(j == 0)
                def _():
                    dq_acc[rows, ln] = dq_part

                @pl.when(j > 0)
                def _():
                    dq_acc[rows, ln] += dq_part

        pl.when(i > j)(functools.partial(block, False))
        pl.when(i == j)(functools.partial(block, True))

        @pl.when(i == j)
        def _():
            dq_acc[rows, :] = dq_acc[rows, :] * sm_scale
            done = pltpu.make_async_copy(dq_acc.at[rows, :], dq_hbm.at[bi, rows, pl.ds(pl.multiple_of(hi * w, w), w)],
                                         dq_sem)
            done.start()
            done.wait()

        @pl.when(i == n - 1)
        def _():
            dk_ref[0] = dk_acc[...] * sm_scale
            dv_ref[0] = dv_acc[...].astype(BF16)

    qs = pl.BlockSpec((1, t, w), lambda b, h, j, i: (b, jnp.maximum(i, j), h))
    ks = pl.BlockSpec((1, t, w), lambda b, h, j, i: (b, j, h))
    return _pc(body, name=name, grid=(b, n_heads // hp, n, n),
               out_shape=(_sds(q.shape, F32), _sds(q.shape, F32), _sds(q.shape, BF16)),
               in_specs=[qs, ks, ks, qs, qs, qs], out_specs=(pl.BlockSpec(memory_space=pltpu.HBM), ks, ks),
               scratch_shapes=[pltpu.VMEM((s, w), F32), pltpu.VMEM((t, w), F32), pltpu.VMEM((t, w), F32),
                               pltpu.SemaphoreType.DMA],
               compiler_params=_cp("arbitrary", "arbitrary", "arbitrary", "arbitrary"))(q, k, v, o, lse, do)


def _mod_fwd(name, c_all, w, bias):
    depth, d, n = w.shape
    rows = c_all.shape[0]

    def body(c_ref, w_ref, b_ref, o_ref):
        cv = c_ref[...]
        cond = (cv * _sigmoid(cv)).astype(BF16)
        o_ref[0] = jnp.dot(cond, w_ref[0].astype(BF16), preferred_element_type=F32) + b_ref[0]

    return _pc(body, name=name, grid=(depth,), out_shape=_sds((depth, rows, n), F32),
               in_specs=[pl.BlockSpec((rows, d), lambda l: (0, 0)), pl.BlockSpec((1, d, n), lambda l: (l, 0, 0)),
                         pl.BlockSpec((1, 1, n), lambda l: (l, 0, 0))],
               out_specs=pl.BlockSpec((1, rows, n), lambda l: (l, 0, 0)), compiler_params=_cp("parallel"))(c_all, w, bias)


def _mod_bwd(name, c_all, dmod_cols, dmod_all):
    depth, rows, n = dmod_cols.shape
    d = c_all.shape[1]
    n_all = dmod_all.shape[2]
    tn = _div(n, 512, LANE)

    def body(c_ref, dm_ref, dma_ref, gw_ref, gb_ref):
        cv = c_ref[...]
        cond = (cv * _sigmoid(cv)).astype(BF16)
        gw_ref[0] = _tn(cond, dm_ref[0].astype(BF16))

        @pl.when(pl.program_id(1) == 0)
        def _():
            gb_ref[0] = jnp.sum(dma_ref[0], axis=0, keepdims=True)

    return _pc(body, name=name, grid=(depth, n // tn),
               out_shape=(_sds((depth, d, n), F32), _sds((depth, 1, n_all), F32)),
               in_specs=[pl.BlockSpec((rows, d), lambda l, j: (0, 0)), pl.BlockSpec((1, rows, tn), lambda l, j: (l, 0, j)),
                         pl.BlockSpec((1, rows, n_all), lambda l, j: (l, 0, 0))],
               out_specs=(pl.BlockSpec((1, d, tn), lambda l, j: (l, 0, j)), pl.BlockSpec((1, 1, n_all), lambda l, j: (l, 0, 0))),
               compiler_params=_cp("parallel", "arbitrary"))(c_all, dmod_cols, dmod_all)


def _adamw(name, w, g, m, v):
    shape = w.shape
    cols = shape[-1]
    rows = _size(shape) // cols
    tr = _div(rows, max(SUBLANE, (2 ** 19) // cols // SUBLANE * SUBLANE), SUBLANE)
    c1 = 1.0 - ADAM_B1 ** ADAM_STEP
    c2 = 1.0 - ADAM_B2 ** ADAM_STEP

    def body(w_ref, g_ref, m_ref, v_ref, d_ref, mo_ref, vo_ref):
        gv = g_ref[...]
        m_new = ADAM_B1 * m_ref[...] + (1.0 - ADAM_B1) * gv
        v_new = ADAM_B2 * v_ref[...] + (1.0 - ADAM_B2) * (gv * gv)
        m_hat = m_new / c1
        v_hat = v_new / c2
        d_ref[...] = -ADAM_LR * (m_hat / (jnp.sqrt(v_hat) + ADAM_EPS) + ADAM_WD * w_ref[...])
        mo_ref[...] = m_new
        vo_ref[...] = v_new

    spec = pl.BlockSpec((tr, cols), lambda i: (i, 0))
    o = _sds((rows, cols), F32)
    outs = _pc(body, name=name, grid=(rows // tr,), out_shape=(o, o, o), in_specs=[spec] * 4, out_specs=(spec,) * 3,
               compiler_params=_cp("parallel"))(*[a.reshape(rows, cols) for a in (w, g, m, v)])
    return tuple(a.reshape(shape) for a in outs)


def _rope_tables(positions):
    half = QK_ROPE // 2
    inv_freq = ROPE_THETA ** (-jnp.arange(0, QK_ROPE, 2, dtype=F32) / QK_ROPE)
    ang = positions.astype(F32)[..., None] * inv_freq
    cos, sin = jnp.cos(ang), jnp.sin(ang)
    lead = positions.shape
    ones = jnp.ones(lead + (QK_NOPE,), F32)
    tail_one = jnp.ones(lead + (HEAD_PAD - QK_NOPE - QK_ROPE,), F32)
    cos_t = jnp.concatenate([ones, cos, cos, tail_one], axis=-1)
    sin_t = jnp.concatenate([0 * ones, -sin, sin, 0 * tail_one], axis=-1)
    return cos_t, sin_t


def _pad_heads(w, n_heads, parts, axis):
    w = jnp.moveaxis(w, axis, -1)
    lead = w.shape[:-1]
    per = w.shape[-1] // n_heads
    w = w.reshape(lead + (n_heads, per))
    kept = jnp.concatenate([w[..., a:b_] for a, b_ in parts], axis=-1)
    pad = HEAD_PAD - kept.shape[-1]
    kept = jnp.concatenate([kept, jnp.zeros(lead + (n_heads, pad), w.dtype)], axis=-1)
    return jnp.moveaxis(kept.reshape(lead + (n_heads * HEAD_PAD,)), -1, axis)


def _unpad_heads(g, n_heads, width, axis):
    g = jnp.moveaxis(g, axis, -1)
    lead = g.shape[:-1]
    g = g.reshape(lead + (n_heads, HEAD_PAD))[..., :width]
    return g, lead


def kernel(x, c, positions, mod_w, mod_b, ln_g, ln_b, pool_w, pool_scale, mla_w_a, mla_q_norm, mla_w_uq, mla_kv_norm, mla_w_ukv, mla_w_o, sc_w_in, sc_conv, sc_w_out, ffn_w_up, ffn_conv, ffn_conv_b, ffn_w_down, loss_target, m_mod_w, m_mod_b, m_ln_g, m_ln_b, m_pool_w, m_pool_scale, m_mla_w_a, m_mla_q_norm, m_mla_w_uq, m_mla_kv_norm, m_mla_w_ukv, m_mla_w_o, m_sc_w_in, m_sc_conv, m_sc_w_out, m_ffn_w_up, m_ffn_conv, m_ffn_conv_b, m_ffn_w_down, v_mod_w, v_mod_b, v_ln_g, v_ln_b, v_pool_w, v_pool_scale, v_mla_w_a, v_mla_q_norm, v_mla_w_uq, v_mla_kv_norm, v_mla_w_ukv, v_mla_w_o, v_sc_w_in, v_sc_conv, v_sc_w_out, v_ffn_w_up, v_ffn_conv, v_ffn_conv_b, v_ffn_w_down):
    wts = dict(mod_w=mod_w, mod_b=mod_b, ln_g=ln_g, ln_b=ln_b, pool_w=pool_w, pool_scale=pool_scale, mla_w_a=mla_w_a,
               mla_q_norm=mla_q_norm, mla_w_uq=mla_w_uq, mla_kv_norm=mla_kv_norm, mla_w_ukv=mla_w_ukv, mla_w_o=mla_w_o,
               sc_w_in=sc_w_in, sc_conv=sc_conv, sc_w_out=sc_w_out, ffn_w_up=ffn_w_up, ffn_conv=ffn_conv,
               ffn_conv_b=ffn_conv_b, ffn_w_down=ffn_w_down)
    mom1 = dict(mod_w=m_mod_w, mod_b=m_mod_b, ln_g=m_ln_g, ln_b=m_ln_b, pool_w=m_pool_w, pool_scale=m_pool_scale,
                mla_w_a=m_mla_w_a, mla_q_norm=m_mla_q_norm, mla_w_uq=m_mla_w_uq, mla_kv_norm=m_mla_kv_norm,
                mla_w_ukv=m_mla_w_ukv, mla_w_o=m_mla_w_o, sc_w_in=m_sc_w_in, sc_conv=m_sc_conv, sc_w_out=m_sc_w_out,
                ffn_w_up=m_ffn_w_up, ffn_conv=m_ffn_conv, ffn_conv_b=m_ffn_conv_b, ffn_w_down=m_ffn_w_down)
    mom2 = dict(mod_w=v_mod_w, mod_b=v_mod_b, ln_g=v_ln_g, ln_b=v_ln_b, pool_w=v_pool_w, pool_scale=v_pool_scale,
                mla_w_a=v_mla_w_a, mla_q_norm=v_mla_q_norm, mla_w_uq=v_mla_w_uq, mla_kv_norm=v_mla_kv_norm,
                mla_w_ukv=v_mla_w_ukv, mla_w_o=v_mla_w_o, sc_w_in=v_sc_w_in, sc_conv=v_sc_conv, sc_w_out=v_sc_w_out,
                ffn_w_up=v_ffn_w_up, ffn_conv=v_ffn_conv, ffn_conv_b=v_ffn_conv_b, ffn_w_down=v_ffn_w_down)

    bsz, seq, d = x.shape
    depth = mod_b.shape[0]
    n_tok = bsz * seq
    n_heads = d // V_HEAD
    ql, kvl = mla_q_norm.shape[1], mla_kv_norm.shape[1]
    alpha = float((2 * depth) ** 0.25)
    sm_scale = float((QK_NOPE + QK_ROPE) ** -0.5)
    mx, my, mc = lax.axis_index("x"), lax.axis_index("y"), lax.axis_index("c")
    chip = 2 * mx + my
    dev = 2 * chip + mc

    small_names = list(SMALL_SHARDED)
    small_pack, small_spans = _pack_rows([c] + [wts[n] for n in small_names], F32, SUBLANE)
    rows_small = small_pack.shape[0]
    small_all = _all_gather8("gather_small_params", small_pack, True).reshape(N_DEV, rows_small * PACK_COLS)
    c_all = small_all[:, :c.size].reshape(N_DEV * bsz, d)
    per_chip = small_all[0::2]
    full = dict(wts)
    for n, (off, shape) in zip(small_names, small_spans[1:]):
        blocks = per_chip[:, off:off + _size(shape)].reshape((N_CHIP,) + tuple(shape))
        full[n] = _join_chips(blocks, SMALL_SHARDED[n])

    n_mod = mod_w.shape[2]
    bias_cols = lax.dynamic_slice_in_dim(mod_b, chip * n_mod, n_mod, axis=1)[:, None, :]
    mod_cols = _mod_fwd("mod_fwd", c_all, mod_w, bias_cols)
    half_rows = (N_DEV * bsz) // 2
    mod_half = lax.dynamic_slice_in_dim(mod_cols, mc * half_rows, half_rows, axis=1).reshape(depth * half_rows, n_mod)
    mod_all = _all_gather8("gather_mod", mod_half, True).reshape(N_CHIP, 2, depth, half_rows, n_mod)
    mod_all = jnp.transpose(mod_all, (2, 1, 3, 0, 4)).reshape(depth, N_DEV * bsz, N_CHIP * n_mod)
    mod_mine = lax.dynamic_slice_in_dim(mod_all, dev * bsz, bsz, axis=1)
    mods = [[mod_mine[l, :, k * d:(k + 1) * d][:, None, :] for k in range(6)] for l in range(depth)]

    big_names = list(BIG)
    f_hid = ffn_w_down.shape[1] * N_CHIP
    host_layer = 1
    assert depth > host_layer

    def layer_of(n, i):
        if n == 'pool_w':
            return 3 * i
        if n.startswith('mla_'):
            return 3 * i + 1
        if n.startswith('sc_'):
            return 3 * i + 2
        return i

    last_layer_of_group = (0, host_layer)

    def group_of(n, i):
        return sum(layer_of(n, i) > top for top in last_layer_of_group)

    n_groups = len(last_layer_of_group) + 1
    span = {n: [[i for i in range(wts[n].shape[0]) if group_of(n, i) == g] for g in range(n_groups)] for n in big_names}
    members = [[n for n in big_names if span[n][g]] for g in range(n_groups)]

    def rows2d(a):
        return a.astype(BF16).reshape(-1, a.shape[-1])

    def layouts(bc):
        out = {}
        for n in ('pool_w', 'mla_w_a', 'mla_w_uq', 'mla_w_ukv', 'mla_w_o', 'sc_w_out'):
            if n in bc:
                out[n] = jnp.concatenate([bc[n][j] for j in range(N_CHIP)], axis=BIG[n])
        if 'ffn_w_up' in bc:
            nl = bc['ffn_w_up'].shape[1]
            out['up_cols'] = bc['ffn_w_up'].reshape(N_CHIP, nl * d, -1)
            out['up_rows'] = jnp.transpose(bc['ffn_w_up'], (1, 0, 3, 2)).reshape(1, nl * 2 * f_hid, d)
            out['down_rows'] = jnp.transpose(bc['ffn_w_down'], (1, 0, 2, 3)).reshape(1, nl * f_hid, d)
            out['down_t'] = jnp.transpose(bc['ffn_w_down'], (1, 3, 0, 2)).reshape(1, nl * d, f_hid)
        if 'sc_w_in' in bc:
            ns = bc['sc_w_in'].shape[1]
            out['in_cols'] = bc['sc_w_in'].reshape(N_CHIP, ns * d, -1)
            out['in_rows'] = jnp.transpose(bc['sc_w_in'], (1, 0, 3, 2)).reshape(1, ns * 3 * d, d)
        return out

    shards = [{n: rows2d(wts[n][span[n][g][0]:span[n][g][-1] + 1]) for n in members[g]} for g in range(n_groups)]
    lay = [None] * n_groups

    def by_chip(g, n, blocks):
        return blocks.reshape((N_CHIP, len(span[n][g])) + wts[n].shape[1:])

    def fetched_group(g, got):
        lay[g] = layouts({n: by_chip(g, n, lax.dynamic_update_index_in_dim(blocks, shards[g][n], chip, 0))
                          for n, blocks in got.items()})

    gathered = _gather_weights("gather_weights", [shards[0][n] for n in members[0]])
    lay[0] = layouts({n: by_chip(0, n, blocks) for n, blocks in zip(members[0], gathered)})

    def grp(n, i):
        g = group_of(n, i)
        return lay[g], i - span[n][g][0]

    nope_rope = [(0, QK_NOPE + QK_ROPE)]
    cos_t, sin_t = _rope_tables(positions)

    def t2(a):
        return a.reshape(n_tok, a.shape[-1])

    def t3(a):
        return a.reshape(bsz, seq, a.shape[-1])

    saved = []
    xin = x
    u = _modulate("modulate_in", x, mods[0][1], mods[0][0])
    loss_acc = None
    for l in range(depth):
        sh1, sc1, g1, sh2, sc2, g2 = mods[l]
        kind, j = l % 3, l // 3
        st = dict(x=xin)
        if kind == 0:
            grp_l, jj = grp('pool_w', j)
            w = grp_l['pool_w'][jj]
            st.update(w=w, w_t=jnp.swapaxes(w, 1, 2), scale=full['pool_scale'][j][None, :])
            y = _pool_fwd(f"pool_fwd_{l}", xin, sc1, sh1, st['w'], st['scale'])
        elif kind == 1:
            grp_l, jj = grp('mla_w_a', j)
            wa, wuq, wukv = grp_l['mla_w_a'][jj], grp_l['mla_w_uq'][jj], grp_l['mla_w_ukv'][jj]
            zeros = jnp.zeros((d, QK_NOPE), BF16)
            w_a = jnp.concatenate([wa[:, :ql + kvl], zeros, wa[:, ql + kvl:], zeros[:, :HEAD_PAD - QK_NOPE - QK_ROPE]], axis=1)
            w_uq = _pad_heads(wuq, n_heads, nope_rope, 1)
            w_k = _pad_heads(wukv, n_heads, [(0, QK_NOPE)], 1)
            w_v = _pad_heads(wukv, n_heads, [(QK_NOPE, QK_NOPE + V_HEAD)], 1)
            w_o = _pad_heads(grp_l['mla_w_o'][jj], n_heads, [(0, V_HEAD)], 0)
            qn, kvn = mla_q_norm[j][None, :], mla_kv_norm[j][None, :]
            a = t3(_mm_nn(f"mla_a_{l}", [(t2(u), _w2(w_a))], F32))
            cq, ckv, kpe = _mla_norm_fwd(f"mla_norm_fwd_{l}", a, qn, kvn, cos_t, sin_t)
            q_raw = t3(_mm_nn(f"mla_q_{l}", [(t2(cq), _w2(w_uq))], F32))
            k_raw = t3(_mm_nn(f"mla_k_{l}", [(t2(ckv), _w2(w_k))], F32))
            vh = t3(_mm_nn(f"mla_v_{l}", [(t2(ckv), _w2(w_v))], BF16))
            qh, kh = _mla_prep_fwd(f"mla_prep_fwd_{l}", q_raw, k_raw, kpe, cos_t, sin_t, n_heads)
            o, lse, fetched = _flash_fwd(f"flash_fwd_{l}", qh, kh, vh, n_heads, sm_scale,
                                         cargo=[shards[2][n] for n in members[2]] if l == host_layer else ())
            if l == host_layer:
                fetched_group(2, dict(zip(members[2], fetched)))
            y = t3(_mm_nn(f"mla_o_{l}", [(t2(o), _w2(w_o))], F32))
            st.update(u=u, w_a=w_a, w_uq=w_uq, w_k=w_k, w_v=w_v, w_o=w_o, qn=qn, kvn=kvn, a=a, cq=cq, ckv=ckv,
                      qh=qh, kh=kh, vh=vh, o=o, lse=lse)
        else:
            grp_l, jj = grp('sc_w_in', j)
            w_out, cw = grp_l['sc_w_out'][jj], full['sc_conv'][j]
            q = t3(_mm_nn(f"sc_in_{l}", [(t2(u), (grp_l['in_cols'], jj))], F32))
            r = _shortconv_fwd(f"shortconv_fwd_{l}", q, cw)
            y = t3(_mm_nn(f"sc_out_{l}", [(t2(r), _w2(w_out))], F32))
            st.update(u=u, w_out=w_out, cw=cw, q=q, r=r, in_rows=(grp_l['in_rows'], jj))
        lng, lnb = full['ln_g'][l], full['ln_b'][l]
        z1, xmid, u2 = _ln_mod_fwd(f"ln_mod_a_{l}", alpha, xin, y, g1, lng[0:1], lnb[0:1], sc2, sh2)
        cwf, cbf = full['ffn_conv'][l], ffn_conv_b[l][None, :]
        ffn_w, ll = grp('ffn_w_up', l)
        ffn_names = ('ffn_w_up', 'ffn_w_down')
        ride_mm = [n for n in members[1] if n not in ffn_names] if l == 0 else []
        ride_conv = [n for n in members[1] if n == 'ffn_w_up'] if l == 0 else []
        ride_down = [n for n in members[1] if n == 'ffn_w_down'] if l == 0 else []
        p = _mm_nn(f"ffn_up_{l}", [(t2(u2), (ffn_w['up_cols'], ll))], F32, cargo=[shards[1][n] for n in ride_mm],
                   route=_Fetch)
        got_mm = ()
        if ride_mm:
            p, got_mm = p
        p = t3(p)
        act, got_conv = _convglu_fwd(f"convglu_fwd_{l}", p, cwf, cbf, cargo=[shards[1][n] for n in ride_conv])
        y2 = _mm_nn(f"ffn_down_{l}", [(t2(act), (ffn_w['down_rows'], ll))], F32, cargo=[shards[1][n] for n in ride_down],
                    route=_Fetch, tm_cap=WIDE_K_ROWS)
        got_down = ()
        if ride_down:
            y2, got_down = y2
        y2 = t3(y2)
        if l == 0:
            fetched_group(1, {**dict(zip(ride_mm, got_mm)), **dict(zip(ride_conv, got_conv)),
                              **dict(zip(ride_down, got_down))})
        st.update(y1=y, z1=z1, xmid=xmid, u2=u2, p=p, act=act, y2=y2, cwf=cwf, cbf=cbf, lng=lng, lnb=lnb,
                  ffn_w=ffn_w, ll=ll)
        if l + 1 < depth:
            nsh1, nsc1 = mods[l + 1][0], mods[l + 1][1]
            z2, xin, u = _ln_mod_fwd(f"ln_mod_b_{l}", alpha, xmid, y2, g2, lng[1:2], lnb[1:2], nsc1, nsh1)
        else:
            z2, ct, loss_acc = _ln_loss_fwd("ln_loss", alpha, xmid, y2, g2, lng[1:2], lnb[1:2], loss_target)
        st.update(z2=z2)
        saved.append(st)
    loss = lax.psum(loss_acc[0, 0], ("x", "y", "c"))

    grads = {}
    dmods = [[None] * 6 for _ in range(depth)]
    g_ln_g = [[None, None] for _ in range(depth)]
    g_ln_b = [[None, None] for _ in range(depth)]
    stack = {n: [None] * wts[n].shape[0] for n in ('pool_scale', 'mla_q_norm', 'mla_kv_norm', 'sc_conv', 'ffn_conv',
                                                    'ffn_conv_b')}
    units = {n: [None] * wts[n].shape[0] for n in big_names}
    cargo_a, cargo_b, received = [], [], {}

    def landed(items, got):
        for (n, i, _), r in zip(items, got):
            received[(n, i)] = r

    upstream = (ct,)
    for l in reversed(range(depth)):
        st = saved[l]
        sh1, sc1, g1, sh2, sc2, g2 = mods[l]
        kind, j = l % 3, l // 3
        if len(upstream) > 1:
            upstream = (upstream[0], upstream[1], st['lnb'][1:2], upstream[2])
        res = _sub_bwd(f"sub_bwd_b_{l}", alpha, upstream, st['z2'], st['y2'], g2, st['lng'][1:2])
        dz2, dy2, dmods[l][5], g_ln_g[l][1], g_ln_b[l][1] = res[:5]
        if l + 1 < depth:
            dmods[l + 1][1], dmods[l + 1][0] = res[5], res[6]
        dy2f = t2(dy2)
        ffn_w, ll = st['ffn_w'], st['ll']
        da = t3(_mm_nn(f"ffn_down_bwd_{l}", [(dy2f, (ffn_w['down_t'], ll))], F32))
        units['ffn_w_down'][l] = _mm_tn(f"ffn_down_dw_{l}", t2(st['act']), [dy2f],
                                        out_dtype=BF16).reshape(N_CHIP, f_hid // N_CHIP, d)
        dpv, dpg, dcw, dcb, got = _convglu_bwd(f"convglu_bwd_{l}", st['p'], da, st['cwf'], st['cbf'],
                                               cargo=[u for _, _, u in cargo_a])
        landed(cargo_a, got)
        stack['ffn_conv'][l], stack['ffn_conv_b'][l] = dcw, dcb[0]
        down_unit = [('ffn_w_down', l, units['ffn_w_down'][l])]
        du2, got = _mm_nn(f"ffn_up_bwd_{l}", [(t2(dpv), (ffn_w['up_rows'], 2 * ll)),
                                              (t2(dpg), (ffn_w['up_rows'], 2 * ll + 1))], F32,
                          cargo=[units['ffn_w_down'][l]], tm_cap=WIDE_K_ROWS)
        landed(down_unit, got)
        du2 = t3(du2)
        res = _mm_tn(f"ffn_up_dw_{l}", t2(st['u2']), [t2(dpv), t2(dpg)], N_CHIP, out_dtype=BF16,
                     cargo=[u for _, _, u in cargo_b])
        if cargo_b:
            landed(cargo_b, res[1])
            res = res[0]
        units['ffn_w_up'][l] = res
        res = _sub_bwd(f"sub_bwd_a_{l}", alpha, (dz2, du2, st['lnb'][0:1], sc2), st['z1'], st['y1'], g1, st['lng'][0:1])
        dz1, dy1, dmods[l][2], g_ln_g[l][0], g_ln_b[l][0], dmods[l][4], dmods[l][3] = res
        dy1f = t2(dy1)
        if kind == 0:
            up_unit = [('ffn_w_up', l, units['ffn_w_up'][l])] if l == 0 else []
            du1, dw, dscale, got = _pool_bwd(f"pool_bwd_{l}", st['x'], sc1, sh1, dy1, st['w'], st['w_t'], st['scale'],
                                             cargo=[u for _, _, u in up_unit])
            landed(up_unit, got)
            stack['pool_scale'][j] = dscale[0]
            grp = dw.shape[1] // N_CHIP
            units['pool_w'][j] = jnp.transpose(dw.reshape(POOL_GROUPS, N_CHIP, grp, dw.shape[2]),
                                               (1, 0, 2, 3)).reshape(N_CHIP, POOL_GROUPS * grp, dw.shape[2])
        elif kind == 1:
            do = t3(_mm_nn(f"mla_o_bwd_{l}", [(dy1f, _w2(jnp.swapaxes(st['w_o'], 0, 1)))], BF16))
            gwo, _ = _unpad_heads(_mm_tn(f"mla_o_dw_{l}", t2(st['o']), [dy1f])[0], n_heads, V_HEAD, 0)
            units['mla_w_o'][j] = jnp.moveaxis(gwo.reshape(d, n_heads * V_HEAD), -1, 0).reshape(N_CHIP, -1, d)
            fa = (st['qh'], st['kh'], st['vh'], st['o'], st['lse'], do, n_heads, sm_scale)
            dq, dk, dv = _flash_bwd(f"flash_bwd_{l}", *fa)
            dq_raw, dk_raw, dkpe = _mla_prep_bwd(f"mla_prep_bwd_{l}", dq, dk, cos_t, sin_t, n_heads)
            dq_raw, dk_raw, dv_raw = t2(dq_raw), t2(dk_raw), t2(dv)
            dcq = t3(_mm_nn(f"mla_q_bwd_{l}", [(dq_raw, _w2(jnp.swapaxes(st['w_uq'], 0, 1)))], F32))
            dckv = t3(_mm_nn(f"mla_kv_bwd_{l}", [(dk_raw, _w2(jnp.swapaxes(st['w_k'], 0, 1))),
                                                   (dv_raw, _w2(jnp.swapaxes(st['w_v'], 0, 1)))], F32,
                            tm_cap=WIDE_K_ROWS))
            gq, _ = _unpad_heads(_mm_tn(f"mla_q_dw_{l}", t2(st['cq']), [dq_raw])[0], n_heads, QK_NOPE + QK_ROPE, 1)
            units['mla_w_uq'][j] = _cols_by_chip(gq.reshape(ql, n_heads * (QK_NOPE + QK_ROPE)))
            gkv = _mm_tn(f"mla_kv_dw_{l}", t2(st['ckv']), [dk_raw, dv_raw])[0]
            gk, _ = _unpad_heads(gkv[:, :n_heads * HEAD_PAD], n_heads, QK_NOPE, 1)
            gv, _ = _unpad_heads(gkv[:, n_heads * HEAD_PAD:], n_heads, V_HEAD, 1)
            units['mla_w_ukv'][j] = _cols_by_chip(
                jnp.concatenate([gk, gv], axis=-1).reshape(kvl, n_heads * (QK_NOPE + V_HEAD)))
            da_, dqn, dkvn = _mla_norm_bwd(f"mla_norm_bwd_{l}", st['a'], dcq, dckv, dkpe, st['qn'], st['kvn'])
            stack['mla_q_norm'][j], stack['mla_kv_norm'][j] = dqn[0], dkvn[0]
            du1 = t3(_mm_nn(f"mla_a_bwd_{l}", [(t2(da_), _w2(jnp.swapaxes(st['w_a'], 0, 1)))], F32))
            gwa = _mm_tn(f"mla_a_dw_{l}", t2(st['u']), [t2(da_)])[0]
            units['mla_w_a'][j] = _cols_by_chip(jnp.concatenate(
                [gwa[:, :ql + kvl], gwa[:, ql + kvl + QK_NOPE:ql + kvl + QK_NOPE + QK_ROPE]], axis=1))
        else:
            dr = t3(_mm_nn(f"sc_out_bwd_{l}", [(dy1f, _w2(jnp.swapaxes(st['w_out'], 0, 1)))], F32))
            units['sc_w_out'][j] = _mm_tn(f"sc_out_dw_{l}", t2(st['r']), [dy1f], out_dtype=BF16).reshape(N_CHIP, -1, d)
            dgb, dgc, dh, dcw = _shortconv_bwd(f"shortconv_bwd_{l}", st['q'], dr, st['cw'])
            stack['sc_conv'][j] = dcw
            parts = [t2(dgb), t2(dgc), t2(dh)]
            in_rows, jj = st['in_rows']
            du1 = t3(_mm_nn(f"sc_in_bwd_{l}", [(parts[k], (in_rows, 3 * jj + k)) for k in range(3)], F32,
                            tm_cap=WIDE_K_ROWS))
            units['sc_w_in'][j] = _cols_by_chip(jnp.concatenate(
                [_mm_tn(f"sc_in_dw_{k}_{l}", t2(st['u']), [parts[k]])[0] for k in range(3)], axis=1))
        upstream = (dz1, du1, sc1)
        mixer = {0: ['pool_w'], 1: ['mla_w_a', 'mla_w_uq', 'mla_w_ukv', 'mla_w_o'], 2: ['sc_w_in', 'sc_w_out']}[kind]
        for n in mixer:
            units[n][j] = units[n][j].astype(BF16)
        cargo_a = [('ffn_w_up', l, units['ffn_w_up'][l])] if l > 0 else []
        cargo_b = [(n, j, units[n][j]) for n in mixer]
    grad_x, dmods[0][1], dmods[0][0] = _input_bwd("input_bwd", alpha, upstream[0], upstream[1], x, mods[0][1])

    for n, parts in stack.items():
        grads[n] = jnp.stack(parts)
    grads['ln_g'] = jnp.stack([jnp.concatenate(r, axis=0) for r in g_ln_g])
    grads['ln_b'] = jnp.stack([jnp.concatenate(r, axis=0) for r in g_ln_b])
    dmod_mine = jnp.stack([jnp.concatenate([t[:, 0, :] for t in dmods[l]], axis=-1) for l in range(depth)])

    small_grad_names = small_names + ['mla_q_norm', 'mla_kv_norm', 'ffn_conv_b']
    sg_pack, sg_spans = _pack_rows([dmod_mine] + [grads[n] for n in small_grad_names], F32, SUBLANE)
    rows_sg = sg_pack.shape[0]
    sg_all = _all_gather8("gather_small_grads", sg_pack, True).reshape(N_DEV, rows_sg, PACK_COLS)
    dmod_all = sg_all.reshape(N_DEV, -1)[:, :dmod_mine.size].reshape(N_DEV, depth, bsz, 6 * d)
    dmod_all = jnp.transpose(dmod_all, (1, 0, 2, 3)).reshape(depth, N_DEV * bsz, 6 * d)
    sg_sum = _sum8("sum_small_grads", sg_all).reshape(-1)
    for n, (off, shape) in zip(small_grad_names, sg_spans[1:]):
        g_full = sg_sum[off:off + _size(shape)].reshape(shape)
        if n in SMALL_SHARDED:
            ax = SMALL_SHARDED[n]
            width = shape[ax] // N_CHIP
            g_full = lax.dynamic_slice_in_dim(g_full, chip * width, width, axis=ax)
        grads[n] = g_full
    dmod_cols = lax.dynamic_slice_in_dim(dmod_all, chip * n_mod, n_mod, axis=2)
    grads['mod_w'], gb = _mod_bwd("mod_bwd", c_all, dmod_cols, dmod_all)
    grads['mod_b'] = gb[:, 0, :]

    keys = [(n, i) for n in big_names for i in range(len(units[n]))]
    last = [(n, i, units[n][i]) for n, i in keys if (n, i) not in received]
    landed(last, _scatter_grads("scatter_big_grads", [u for _, _, u in last]))
    chip_core = jnp.stack([chip, mc]).astype(jnp.int32)
    bufs = _swap_halves("swap_big_grad_halves",
                        [_sum8_into_half(f"sum_big_grads_{n}_{i}", units[n][i], received[(n, i)], chip_core)
                         for n, i in keys])
    for n in big_names:
        grads[n] = jnp.stack([b for (m, _), b in zip(keys, bufs) if m == n]).reshape(wts[n].shape)

    deltas, new_m, new_v = {}, {}, {}
    for n in WEIGHTS:
        deltas[n], new_m[n], new_v[n] = _adamw(f"adamw_{n}", wts[n], grads[n], mom1[n], mom2[n])
    return (loss, grad_x, *[grads[n] for n in WEIGHTS], *[deltas[n] for n in WEIGHTS],
            *[new_m[n] for n in WEIGHTS], *[new_v[n] for n in WEIGHTS])
```

```python
import functools

import jax
import jax.numpy as jnp
from jax import lax
from jax.experimental import pallas as pl
from jax.experimental.pallas import tpu as pltpu

F32 = jnp.float32
BF16 = jnp.bfloat16
MESH = pl.DeviceIdType.MESH

N_DEV = 8
N_CHIP = 4
LANE = 128
SUBLANE = 8
VMEM_LIMIT_BYTES = 56 * 2 ** 20
PACK_COLS = 1024

LN_EPS = 1e-5
RMS_EPS = 1e-6
QK_NOPE, QK_ROPE, V_HEAD = 64, 32, 64
ROPE_THETA = 10000.0
HEAD_PAD = 128
POOL_GROUPS = 4
POOL_HALO = 16
CONV_HALO = 8
CONV_ROWS = 1024
WIDE_K_ROWS = 1024
ADAM_LR, ADAM_B1, ADAM_B2, ADAM_EPS, ADAM_WD, ADAM_STEP = 0.001, 0.9, 0.999, 1e-08, 0.01, 10

WEIGHTS = ['mod_w', 'mod_b', 'ln_g', 'ln_b', 'pool_w', 'pool_scale', 'mla_w_a', 'mla_q_norm', 'mla_w_uq',
           'mla_kv_norm', 'mla_w_ukv', 'mla_w_o', 'sc_w_in', 'sc_conv', 'sc_w_out', 'ffn_w_up', 'ffn_conv',
           'ffn_conv_b', 'ffn_w_down']
BIG = {'pool_w': 2, 'mla_w_a': 2, 'mla_w_uq': 2, 'mla_w_ukv': 2, 'mla_w_o': 1, 'sc_w_in': 2, 'sc_w_out': 1,
       'ffn_w_up': 2, 'ffn_w_down': 1}
SMALL_SHARDED = {'ln_g': 2, 'ln_b': 2, 'pool_scale': 1, 'sc_conv': 2, 'ffn_conv': 2}
REPLICATED = ['mod_b', 'mla_q_norm', 'mla_kv_norm', 'ffn_conv_b']


def _pc(body, **kw):
    return pl.pallas_call(body, **kw)


def _cp(*sem):
    return pltpu.CompilerParams(dimension_semantics=sem, vmem_limit_bytes=VMEM_LIMIT_BYTES)


def _div(n, cap, mult):
    best = None
    for d in range(mult, min(n, cap) + 1, mult):
        if n % d == 0:
            best = d
    return best if best is not None else n


def _sds(shape, dtype):
    return jax.ShapeDtypeStruct(tuple(shape), dtype)


def _flip(v, bit):
    return 1 - v if bit else v


def _all_gather8(name, x_shard, in_vmem):
    m_per, n = x_shard.shape
    space = pltpu.VMEM if in_vmem else pltpu.HBM

    def body(x_ref, out_ref, send_sems, recv_sems, local_sem):
        x, y, c = lax.axis_index("x"), lax.axis_index("y"), lax.axis_index("c")
        me, sibling = (x, y, c), (x, y, 1 - c)
        chips = [(1 - x, y), (x, 1 - y), (1 - x, 1 - y)]

        def rows(px, py, pc_):
            return out_ref.at[pl.ds((4 * px + 2 * py + pc_) * m_per, m_per), :]

        def copy(k, block, to, src=None):
            return pltpu.make_async_remote_copy(
                src_ref=rows(*block) if src is None else src, dst_ref=rows(*block),
                send_sem=send_sems.at[k], recv_sem=recv_sems.at[k], device_id=to, device_id_type=MESH)

        mine = pltpu.make_async_copy(x_ref, rows(*me), local_sem)
        mine.start()
        first = [copy(0, me, sibling, src=x_ref)]
        first += [copy(1 + j, me, (*chip, c), src=x_ref) for j, chip in enumerate(chips)]
        for cp in first:
            cp.start()
        passed = [copy(4 + j, (*chip, c), sibling) for j, chip in enumerate(chips)]
        for j, chip in enumerate(chips):
            copy(1 + j, (*chip, c), me).wait_recv()
            passed[j].start()
        copy(0, sibling, me).wait_recv()
        for j, chip in enumerate(chips):
            copy(4 + j, (*chip, 1 - c), me).wait_recv()
        for cp in first + passed:
            cp.wait_send()
        mine.wait()

    return _pc(
        body, name=name, out_shape=_sds((N_DEV * m_per, n), x_shard.dtype),
        in_specs=[pl.BlockSpec(memory_space=space)], out_specs=pl.BlockSpec(memory_space=space),
        scratch_shapes=[pltpu.SemaphoreType.DMA((7,)), pltpu.SemaphoreType.DMA((7,)), pltpu.SemaphoreType.DMA],
        compiler_params=pltpu.CompilerParams(vmem_limit_bytes=VMEM_LIMIT_BYTES),
    )(x_shard)


def _gather_weights(name, shards):
    n_t = len(shards)
    halves = [s.shape[0] // 2 for s in shards]

    def body(*refs):
        x_refs, o_refs = refs[:n_t], refs[n_t:2 * n_t]
        send_sems, recv_sems, local_sems = refs[2 * n_t:]
        x, y, c = lax.axis_index("x"), lax.axis_index("y"), lax.axis_index("c")
        me, sibling = (x, y, c), (x, y, 1 - c)
        chips = [(1 - x, y), (x, 1 - y), (1 - x, 1 - y)]

        def slot(t, px, py, pc_):
            return o_refs[t].at[4 * px + 2 * py + pc_]

        def my_rows(t):
            return x_refs[t].at[pl.ds(c * halves[t], halves[t]), :]

        def copy(t, k, block, to, src=None):
            return pltpu.make_async_remote_copy(
                src_ref=slot(t, *block) if src is None else src, dst_ref=slot(t, *block),
                send_sem=send_sems.at[t, k], recv_sem=recv_sems.at[t, k], device_id=to, device_id_type=MESH)

        local = [pltpu.make_async_copy(my_rows(t), slot(t, *me), local_sems.at[t]) for t in range(n_t)]
        for cp in local:
            cp.start()
        first = []
        for t in range(n_t):
            first += [copy(t, 1 + j, me, (*chip, c), src=my_rows(t)) for j, chip in enumerate(chips)]
            first.append(copy(t, 0, me, sibling, src=my_rows(t)))
        for cp in first:
            cp.start()
        passed = []
        for j, chip in enumerate(chips):
            for t in range(n_t):
                copy(t, 1 + j, (*chip, c), me).wait_recv()
                passed.append(copy(t, 4 + j, (*chip, c), sibling))
                passed[-1].start()
        for t in range(n_t):
            copy(t, 0, sibling, me).wait_recv()
        for j, chip in enumerate(chips):
            for t in range(n_t):
                copy(t, 4 + j, (*chip, 1 - c), me).wait_recv()
        for cp in first + passed:
            cp.wait_send()
        for cp in local:
            cp.wait()

    hbm = pl.BlockSpec(memory_space=pltpu.HBM)
    return _pc(
        body, name=name, out_shape=tuple(_sds((N_DEV, h, s.shape[1]), s.dtype) for h, s in zip(halves, shards)),
        in_specs=[hbm] * n_t, out_specs=(hbm,) * n_t,
        scratch_shapes=[pltpu.SemaphoreType.DMA((n_t, 7)), pltpu.SemaphoreType.DMA((n_t, 7)),
                        pltpu.SemaphoreType.DMA((n_t,))],
    )(*shards)


def _scatter_copies(u_refs, r_refs, send_sems, recv_sems):
    x, y, c = lax.axis_index("x"), lax.axis_index("y"), lax.axis_index("c")
    copies = []
    for k in range(1, N_DEV):
        px, py, pcc = _flip(x, (k >> 2) & 1), _flip(y, (k >> 1) & 1), _flip(c, k & 1)
        for t, (u_ref, r_ref) in enumerate(zip(u_refs, r_refs)):
            h = u_ref.shape[1] // 2
            copies.append(pltpu.make_async_remote_copy(
                src_ref=u_ref.at[2 * px + py, pl.ds(pcc * h, h), :], dst_ref=r_ref.at[k - 1],
                send_sem=send_sems.at[t, k - 1], recv_sem=recv_sems.at[t, k - 1],
                device_id=(px, py, pcc), device_id_type=MESH))
    return copies


def _scatter_shapes(units):
    return tuple(_sds((N_DEV - 1, u.shape[1] // 2, u.shape[2]), u.dtype) for u in units)


def _scatter_grads(name, units):
    n_u = len(units)

    def body(*refs):
        copies = _scatter_copies(refs[:n_u], refs[n_u:2 * n_u], refs[2 * n_u], refs[2 * n_u + 1])
        for cp in copies:
            cp.start()
        for cp in copies:
            cp.wait()

    hbm = pl.BlockSpec(memory_space=pltpu.HBM)
    return _pc(body, name=name, out_shape=_scatter_shapes(units), in_specs=[hbm] * n_u, out_specs=(hbm,) * n_u,
               scratch_shapes=[pltpu.SemaphoreType.DMA((n_u, 7)), pltpu.SemaphoreType.DMA((n_u, 7))])(*units)


class _Scatter:
    peers = N_DEV - 1
    shapes = staticmethod(_scatter_shapes)

    @staticmethod
    def copies(u_refs, r_refs, send_sems, recv_sems):
        both = _scatter_copies(u_refs, r_refs, send_sems, recv_sems)
        return both, both


class _Fetch:
    peers = N_CHIP - 1

    @staticmethod
    def shapes(units):
        return tuple(_sds((N_CHIP,) + u.shape, u.dtype) for u in units)

    @staticmethod
    def copies(u_refs, r_refs, send_sems, recv_sems):
        x, y, c = lax.axis_index("x"), lax.axis_index("y"), lax.axis_index("c")
        sends, recvs = [], []
        for k in range(1, N_CHIP):
            px, py = _flip(x, (k >> 1) & 1), _flip(y, k & 1)
            for t, (u_ref, r_ref) in enumerate(zip(u_refs, r_refs)):
                sends.append(pltpu.make_async_remote_copy(
                    src_ref=u_ref, dst_ref=r_ref.at[2 * x + y], send_sem=send_sems.at[t, k - 1],
                    recv_sem=recv_sems.at[t, k - 1], device_id=(px, py, c), device_id_type=MESH))
                recvs.append(pltpu.make_async_remote_copy(
                    src_ref=u_ref, dst_ref=r_ref.at[2 * px + py], send_sem=send_sems.at[t, k - 1],
                    recv_sem=recv_sems.at[t, k - 1], device_id=(px, py, c), device_id_type=MESH))
        return sends, recvs


def _pc_cargo(body, cargo, *, name, grid, in_specs, out_specs, out_shape, scratch_shapes=(), route=_Scatter):
    out_specs, out_shape = tuple(out_specs), tuple(out_shape)
    if not cargo:
        return lambda *args: (_pc(body, name=name, grid=grid, in_specs=list(in_specs), out_specs=out_specs,
                                  out_shape=out_shape, scratch_shapes=list(scratch_shapes),
                                  compiler_params=_cp(*["arbitrary"] * len(grid)))(*args), ())
    n_in, n_out, n_u, n_s = len(in_specs), len(out_specs), len(cargo), len(scratch_shapes)

    def wrapped(*refs):
        ins, u_refs = refs[:n_in], refs[n_in:n_in + n_u]
        outs = refs[n_in + n_u:n_in + n_u + n_out]
        r_refs = refs[n_in + n_u + n_out:n_in + 2 * n_u + n_out]
        scratch = refs[n_in + 2 * n_u + n_out:n_in + 2 * n_u + n_out + n_s]
        send_sems, recv_sems = refs[-2:]
        first = last = None
        for axis, extent in enumerate(grid):
            at_start, at_end = pl.program_id(axis) == 0, pl.program_id(axis) == extent - 1
            first = at_start if first is None else first & at_start
            last = at_end if last is None else last & at_end

        @pl.when(first)
        def _():
            sends, _ = route.copies(u_refs, r_refs, send_sems, recv_sems)
            for cp in sends:
                cp.start()

        body(*ins, *outs, *scratch)

        @pl.when(last)
        def _():
            sends, recvs = route.copies(u_refs, r_refs, send_sems, recv_sems)
            for cp in recvs:
                cp.wait_recv()
            for cp in sends:
                cp.wait_send()

    hbm = pl.BlockSpec(memory_space=pltpu.HBM)
    sems = pltpu.SemaphoreType.DMA((n_u, route.peers))
    call = _pc(wrapped, name=name, grid=grid, in_specs=list(in_specs) + [hbm] * n_u, out_specs=out_specs + (hbm,) * n_u,
               out_shape=out_shape + route.shapes(cargo), scratch_shapes=list(scratch_shapes) + [sems, sems],
               compiler_params=_cp(*["arbitrary"] * len(grid)))

    def run(*args):
        res = call(*args, *cargo)
        return tuple(res[:n_out]), tuple(res[n_out:])
    return run


def _swap_halves(name, bufs):
    n_u = len(bufs)

    def body(*refs):
        o_refs = refs[n_u:2 * n_u]
        send_sems, recv_sems = refs[2 * n_u:]
        x, y, c = lax.axis_index("x"), lax.axis_index("y"), lax.axis_index("c")

        def rows(u, core):
            h = bufs[u].shape[0] // 2
            return o_refs[u].at[pl.ds(core * h, h), :]

        sends = [pltpu.make_async_remote_copy(src_ref=rows(u, c), dst_ref=rows(u, c), send_sem=send_sems.at[u],
                                              recv_sem=recv_sems.at[u], device_id=(x, y, 1 - c), device_id_type=MESH)
                 for u in range(n_u)]
        recvs = [pltpu.make_async_remote_copy(src_ref=rows(u, c), dst_ref=rows(u, 1 - c), send_sem=send_sems.at[u],
                                              recv_sem=recv_sems.at[u], device_id=(x, y, 1 - c), device_id_type=MESH)
                 for u in range(n_u)]
        for cp in sends:
            cp.start()
        for cp in recvs:
            cp.wait_recv()
        for cp in sends:
            cp.wait_send()

    hbm = pl.BlockSpec(memory_space=pltpu.HBM)
    return _pc(
        body, name=name, out_shape=tuple(_sds(b.shape, b.dtype) for b in bufs), in_specs=[hbm] * n_u,
        out_specs=(hbm,) * n_u, input_output_aliases={u: u for u in range(n_u)},
        scratch_shapes=[pltpu.SemaphoreType.DMA((n_u,)), pltpu.SemaphoreType.DMA((n_u,))],
    )(*bufs)


def _sum8_into_half(name, unit, received, chip_core):
    _, h, n = received.shape
    tm = _div(h, 256, 16)
    per = h // tm

    def body(cc_ref, u_ref, p_ref, o_ref):
        acc = u_ref[0].astype(F32)
        for s in range(N_DEV - 1):
            acc = acc + p_ref[s].astype(F32)
        o_ref[...] = acc

    grid_spec = pltpu.PrefetchScalarGridSpec(
        num_scalar_prefetch=1, grid=(per,),
        in_specs=[pl.BlockSpec((1, tm, n), lambda i, cc_ref: (cc_ref[0], cc_ref[1] * per + i, 0)),
                  pl.BlockSpec((N_DEV - 1, tm, n), lambda i, cc_ref: (0, i, 0))],
        out_specs=pl.BlockSpec((tm, n), lambda i, cc_ref: (cc_ref[1] * per + i, 0)))
    return _pc(body, name=name, grid_spec=grid_spec, out_shape=_sds((2 * h, n), F32),
               compiler_params=_cp("arbitrary"))(chip_core, unit, received)


def _sum8(name, parts):
    _, m, n = parts.shape
    tm = _div(m, 256, SUBLANE)

    def body(p_ref, o_ref):
        acc = p_ref[0]
        for s in range(1, N_DEV):
            acc = acc + p_ref[s]
        o_ref[...] = acc

    return _pc(body, name=name, grid=(m // tm,), out_shape=_sds((m, n), F32),
               in_specs=[pl.BlockSpec((N_DEV, tm, n), lambda i: (0, i, 0))],
               out_specs=pl.BlockSpec((tm, n), lambda i: (i, 0)), compiler_params=_cp("parallel"))(parts)


def _pack_rows(arrays, dtype, row_mult):
    flat, spans, off = [], [], 0
    for a in arrays:
        flat.append(a.reshape(-1).astype(dtype))
        spans.append((off, a.shape))
        off += a.size
    quantum = row_mult * PACK_COLS
    total = -(-off // quantum) * quantum
    if total > off:
        flat.append(jnp.zeros((total - off,), dtype))
    return jnp.concatenate(flat).reshape(total // PACK_COLS, PACK_COLS), spans


def _size(shape):
    n = 1
    for s in shape:
        n *= s
    return n


def _join_chips(blocks, axis):
    return jnp.concatenate([blocks[j] for j in range(N_CHIP)], axis=axis)


def _cols_by_chip(g):
    k, n = g.shape
    return jnp.transpose(g.reshape(k, N_CHIP, n // N_CHIP), (1, 0, 2))


def _mm_nn(name, pairs, out_dtype, cargo=(), route=_Scatter, tm_cap=2048, tn_cap=1536):
    m = pairs[0][0].shape[0]
    nb, _, n4 = pairs[0][1][0].shape
    tm, tn = _div(m, tm_cap, 16), _div(n4, tn_cap, LANE)
    per = n4 // tn
    n_pairs = len(pairs)

    def body(*refs):
        o_ref = refs[-1]
        acc = jnp.dot(refs[0][...], refs[1][0], preferred_element_type=F32)
        for i in range(1, n_pairs):
            acc = acc + jnp.dot(refs[2 * i][...], refs[2 * i + 1][0], preferred_element_type=F32)
        o_ref[...] = acc.astype(o_ref.dtype)

    in_specs, args = [], []
    for a, (w, r) in pairs:
        k = a.shape[1]
        assert w.shape[0] == nb and w.shape[2] == n4 and w.shape[1] % k == 0
        in_specs += [pl.BlockSpec((tm, k), lambda j, i: (i, 0)),
                     pl.BlockSpec((1, k, tn), functools.partial(lambda j, i, r_: (j // per, r_, j % per), r_=r))]
        args += [a, w]
    if cargo:
        (out,), received = _pc_cargo(body, cargo, name=name, grid=(nb * per, m // tm), in_specs=in_specs, route=route,
                                     out_shape=[_sds((m, nb * n4), out_dtype)],
                                     out_specs=[pl.BlockSpec((tm, tn), lambda j, i: (i, j))])(*args)
        return out, received
    return _pc(body, name=name, grid=(nb * per, m // tm), out_shape=_sds((m, nb * n4), out_dtype), in_specs=in_specs,
               out_specs=pl.BlockSpec((tm, tn), lambda j, i: (i, j)), compiler_params=_cp("parallel", "parallel"))(*args)


def _mm_tn(name, x, ys, n_blocks=1, out_dtype=F32, cargo=(), tt_cap=1024):
    t, k = x.shape
    widths = [y.shape[1] for y in ys]
    n4 = sum(widths) // n_blocks
    common = n4
    for w in widths:
        common = _gcd(common, w)
    tk, tn, tt = _div(k, 1536, LANE), _div(common, 1536, LANE), _div(t, tt_cap, 16)
    per = n4 // tn
    starts, acc_w = [], 0
    for w in widths:
        starts.append(acc_w // tn)
        acc_w += w
    counts = [w // tn for w in widths]
    n_y = len(ys)

    def active(i, j):
        return (j >= starts[i]) & (j < starts[i] + counts[i])

    n_t = t // tt

    def body(*refs):
        x_ref, y_refs, o_ref, acc_ref = refs[0], refs[1:1 + n_y], refs[-2], refs[-1]
        j = pl.program_id(1)

        @pl.when(pl.program_id(2) == 0)
        def _():
            acc_ref[...] = jnp.zeros_like(acc_ref)

        for i in range(n_y):
            @pl.when(active(i, j))
            def _():
                acc_ref[...] += lax.dot_general(x_ref[...], y_refs[i][...], (((0,), (0,)), ((), ())),
                                                preferred_element_type=F32)

        @pl.when(pl.program_id(2) == n_t - 1)
        def _():
            o_ref[0] = acc_ref[...].astype(o_ref.dtype)

    def y_spec(i):
        def index(a, j, s):
            on = active(i, j)
            return jnp.where(on, s, 0), jnp.where(on, j - starts[i], 0)
        return pl.BlockSpec((tt, tn), index)

    (out,), received = _pc_cargo(
        body, cargo, name=name, grid=(k // tk, n_blocks * per, n_t), out_shape=[_sds((n_blocks, k, n4), out_dtype)],
        in_specs=[pl.BlockSpec((tt, tk), lambda a, j, s: (s, a))] + [y_spec(i) for i in range(n_y)],
        out_specs=[pl.BlockSpec((1, tk, tn), lambda a, j, s: (j // per, a, j % per))],
        scratch_shapes=[pltpu.VMEM((tk, tn), F32)])(x, *ys)
    return (out, received) if cargo else out


def _gcd(a, b):
    while b:
        a, b = b, a % b
    return a


def _w2(w):
    return (w[None], 0)


def _tok_spec(ts, d):
    return pl.BlockSpec((1, ts, d), lambda b, i: (b, i, 0))


def _seq_spec(d):
    return pl.BlockSpec((1, 1, d), lambda b, i: (b, 0, 0))


def _vec_spec(d):
    return pl.BlockSpec((1, d), lambda b, i: (0, 0))


def _ln_stats(z):
    mu = jnp.mean(z, axis=-1, keepdims=True)
    zc = z - mu
    var = jnp.mean(zc * zc, axis=-1, keepdims=True)
    rstd = lax.rsqrt(var + LN_EPS)
    return zc * rstd, rstd


def _modulate(name, x, sc, sh):
    b, s, d = x.shape
    ts = _div(s, 512, 16)

    def body(x_ref, sc_ref, sh_ref, u_ref):
        u_ref[0] = (x_ref[0] * (1.0 + sc_ref[0]) + sh_ref[0]).astype(BF16)

    return _pc(body, name=name, grid=(b, s // ts), out_shape=_sds(x.shape, BF16),
               in_specs=[_tok_spec(ts, d), _seq_spec(d), _seq_spec(d)], out_specs=_tok_spec(ts, d),
               compiler_params=_cp("parallel", "parallel"))(x, sc, sh)


def _ln_mod_fwd(name, alpha, x, y, g, lng, lnb, sc, sh):
    b, s, d = x.shape
    ts = _div(s, 512, 16)

    def body(x_ref, y_ref, g_ref, lng_ref, lnb_ref, sc_ref, sh_ref, z_ref, xn_ref, u_ref):
        z = alpha * x_ref[0] + (1.0 + g_ref[0]) * y_ref[0]
        xhat, _ = _ln_stats(z)
        xn = xhat * lng_ref[...] + lnb_ref[...]
        z_ref[0] = z
        xn_ref[0] = xn
        u_ref[0] = (xn * (1.0 + sc_ref[0]) + sh_ref[0]).astype(BF16)

    tok, seq, vec = _tok_spec(ts, d), _seq_spec(d), _vec_spec(d)
    return _pc(body, name=name, grid=(b, s // ts),
               out_shape=(_sds(x.shape, F32), _sds(x.shape, F32), _sds(x.shape, BF16)),
               in_specs=[tok, tok, seq, vec, vec, seq, seq], out_specs=(tok, tok, tok),
               compiler_params=_cp("parallel", "parallel"))(x, y, g, lng, lnb, sc, sh)


def _ln_loss_fwd(name, alpha, x, y, g, lng, lnb, target):
    b, s, d = x.shape
    ts = _div(s, 512, 16)

    def body(x_ref, y_ref, g_ref, lng_ref, lnb_ref, t_ref, z_ref, ct_ref, loss_ref):
        @pl.when((pl.program_id(0) == 0) & (pl.program_id(1) == 0))
        def _():
            loss_ref[...] = jnp.zeros_like(loss_ref)
        z = alpha * x_ref[0] + (1.0 + g_ref[0]) * y_ref[0]
        xhat, _ = _ln_stats(z)
        err = xhat * lng_ref[...] + lnb_ref[...] - t_ref[0]
        z_ref[0] = z
        ct_ref[0] = err / d
        part = 0.5 * jnp.sum(jnp.mean(err * err, axis=-1, keepdims=True))
        loss_ref[...] += jnp.full(loss_ref.shape, part, F32)

    tok, seq, vec = _tok_spec(ts, d), _seq_spec(d), _vec_spec(d)
    return _pc(body, name=name, grid=(b, s // ts),
               out_shape=(_sds(x.shape, F32), _sds(x.shape, F32), _sds((SUBLANE, LANE), F32)),
               in_specs=[tok, tok, seq, vec, vec, tok],
               out_specs=(tok, tok, pl.BlockSpec((SUBLANE, LANE), lambda b, i: (0, 0))),
               compiler_params=_cp("arbitrary", "arbitrary"))(x, y, g, lng, lnb, target)


def _sub_bwd(name, alpha, upstream, z, y, g, lng):
    b, s, d = z.shape
    ts = _div(s, 512, 16)
    last = len(upstream) == 1

    def body(*refs):
        if last:
            ct_ref, z_ref, y_ref, g_ref, lng_ref, dz_ref, dy_ref, dg_ref, dlng_ref, dlnb_ref = refs
        else:
            (dzn_ref, dun_ref, lnb_ref, scn_ref, z_ref, y_ref, g_ref, lng_ref,
             dz_ref, dy_ref, dg_ref, dlng_ref, dlnb_ref, dsc_ref, dsh_ref) = refs
        first_tile = pl.program_id(1) == 0

        @pl.when(first_tile & (pl.program_id(0) == 0))
        def _():
            dlng_ref[...] = jnp.zeros_like(dlng_ref)
            dlnb_ref[...] = jnp.zeros_like(dlnb_ref)

        @pl.when(first_tile)
        def _():
            dg_ref[...] = jnp.zeros_like(dg_ref)
            if not last:
                dsc_ref[...] = jnp.zeros_like(dsc_ref)
                dsh_ref[...] = jnp.zeros_like(dsh_ref)

        xhat, rstd = _ln_stats(z_ref[0])
        if last:
            ct = ct_ref[0]
        else:
            dun = dun_ref[0]
            ct = alpha * dzn_ref[0] + dun * (1.0 + scn_ref[0])
            xn = xhat * lng_ref[...] + lnb_ref[...]
            dsc_ref[0] += jnp.sum(dun * xn, axis=0, keepdims=True)
            dsh_ref[0] += jnp.sum(dun, axis=0, keepdims=True)
        dlng_ref[...] += jnp.sum(ct * xhat, axis=0, keepdims=True)
        dlnb_ref[...] += jnp.sum(ct, axis=0, keepdims=True)
        dxhat = ct * lng_ref[...]
        dz = rstd * (dxhat - jnp.mean(dxhat, axis=-1, keepdims=True)
                     - xhat * jnp.mean(dxhat * xhat, axis=-1, keepdims=True))
        dz_ref[0] = dz
        dy_ref[0] = ((1.0 + g_ref[0]) * dz).astype(BF16)
        dg_ref[0] += jnp.sum(dz * y_ref[0], axis=0, keepdims=True)

    tok, seq, vec = _tok_spec(ts, d), _seq_spec(d), _vec_spec(d)
    seq_out = _sds((b, 1, d), F32)
    out_shape = [_sds(z.shape, F32), _sds(z.shape, BF16), seq_out, _sds((1, d), F32), _sds((1, d), F32)]
    out_specs = [tok, tok, seq, vec, vec]
    if last:
        in_specs = [tok, tok, tok, seq, vec]
    else:
        in_specs = [tok, tok, vec, seq, tok, tok, seq, vec]
        out_shape += [seq_out, seq_out]
        out_specs += [seq, seq]
    return _pc(body, name=name, grid=(b, s // ts), out_shape=tuple(out_shape), in_specs=in_specs,
               out_specs=tuple(out_specs), compiler_params=_cp("arbitrary", "arbitrary"))(*upstream, z, y, g, lng)


def _input_bwd(name, alpha, dz, du, x, sc):
    b, s, d = x.shape
    ts = _div(s, 512, 16)

    def body(dz_ref, du_ref, x_ref, sc_ref, gx_ref, dsc_ref, dsh_ref):
        @pl.when(pl.program_id(1) == 0)
        def _():
            dsc_ref[...] = jnp.zeros_like(dsc_ref)
            dsh_ref[...] = jnp.zeros_like(dsh_ref)
        du_ = du_ref[0]
        gx_ref[0] = alpha * dz_ref[0] + du_ * (1.0 + sc_ref[0])
        dsc_ref[0] += jnp.sum(du_ * x_ref[0], axis=0, keepdims=True)
        dsh_ref[0] += jnp.sum(du_, axis=0, keepdims=True)

    tok, seq = _tok_spec(ts, d), _seq_spec(d)
    seq_out = _sds((b, 1, d), F32)
    return _pc(body, name=name, grid=(b, s // ts), out_shape=(_sds(x.shape, F32), seq_out, seq_out),
               in_specs=[tok, tok, tok, seq], out_specs=(tok, seq, seq),
               compiler_params=_cp("parallel", "arbitrary"))(dz, du, x, sc)


def _rows_iota(shape):
    return lax.broadcasted_iota(jnp.int32, shape, 0)


def _back(v, k):
    return pltpu.roll(v, k, axis=0)


def _ahead(v, k):
    return pltpu.roll(v, v.shape[0] - k, axis=0)


def _conv3(ext, w_ref):
    return w_ref[2:3, :] * ext + w_ref[1:2, :] * _back(ext, 1) + w_ref[0:1, :] * _back(ext, 2)


def _conv3_t(dh_ext, w_ref):
    return w_ref[2:3, :] * dh_ext + w_ref[1:2, :] * _ahead(dh_ext, 1) + w_ref[0:1, :] * _ahead(dh_ext, 2)


def _flag(cond):
    return jnp.where(cond, 1.0, 0.0).astype(F32)


def _sigmoid(v):
    return 1.0 / (1.0 + jnp.exp(-v))


def _halo_specs(ts, tc, halo, n_s, col):
    per = ts // halo
    tile = pl.BlockSpec((1, ts, tc), lambda b, i, j: (b, i, col(j)))
    prev = pl.BlockSpec((1, halo, tc), lambda b, i, j: (b, jnp.maximum(i * per - 1, 0), col(j)))
    nxt = pl.BlockSpec((1, halo, tc), lambda b, i, j: (b, jnp.minimum((i + 1) * per, n_s * per - 1), col(j)))
    return tile, prev, nxt


def _convglu_fwd(name, p, cw, cb, cargo=()):
    b, s, f2 = p.shape
    f = f2 // 2
    ts, tc = _div(s, CONV_ROWS, CONV_HALO), _div(f, 256, LANE)
    n_s, n_c = s // ts, f // tc

    def body(pv_ref, pvh_ref, pg_ref, pgh_ref, wv_ref, wg_ref, bv_ref, bg_ref, a_ref):
        keep = _flag(pl.program_id(1) > 0)

        def conv(t_ref, h_ref, w_ref, b_ref):
            ext = jnp.concatenate([h_ref[0] * keep, t_ref[0]], axis=0)
            return _conv3(ext, w_ref)[CONV_HALO:] + b_ref[...]

        val = conv(pv_ref, pvh_ref, wv_ref, bv_ref)
        gate = conv(pg_ref, pgh_ref, wg_ref, bg_ref)
        a_ref[0] = (gate * _sigmoid(gate) * val).astype(BF16)

    tv, hv, _ = _halo_specs(ts, tc, CONV_HALO, n_s, lambda j: j)
    tg, hg, _ = _halo_specs(ts, tc, CONV_HALO, n_s, lambda j: j + n_c)
    wv = pl.BlockSpec((3, tc), lambda b, i, j: (0, j))
    wg = pl.BlockSpec((3, tc), lambda b, i, j: (0, j + n_c))
    bv = pl.BlockSpec((1, tc), lambda b, i, j: (0, j))
    bg = pl.BlockSpec((1, tc), lambda b, i, j: (0, j + n_c))
    (act,), fetched = _pc_cargo(
        body, cargo, name=name, grid=(b, n_s, n_c), route=_Fetch, out_shape=[_sds((b, s, f), BF16)],
        in_specs=[tv, hv, tg, hg, wv, wg, bv, bg],
        out_specs=[pl.BlockSpec((1, ts, tc), lambda b, i, j: (b, i, j))])(p, p, p, p, cw, cw, cb, cb)
    return act, fetched


def _convglu_bwd(name, p, da, cw, cb, cargo=()):
    b, s, f2 = p.shape
    f = f2 // 2
    ts, tc = _div(s, CONV_ROWS, CONV_HALO), _div(f, 256, LANE)
    n_s, n_c = s // ts, f // tc

    def body(pv_ref, pvp_ref, pvn_ref, pg_ref, pgp_ref, pgn_ref, da_ref, dan_ref, wv_ref, wg_ref, bv_ref, bg_ref,
             dpv_ref, dpg_ref, dwv_ref, dwg_ref, dbv_ref, dbg_ref):
        bi, i = pl.program_id(1), pl.program_id(2)

        @pl.when((bi == 0) & (i == 0))
        def _():
            for r in (dwv_ref, dwg_ref, dbv_ref, dbg_ref):
                r[...] = jnp.zeros_like(r)

        keep_prev = _flag(i > 0)
        keep_next = _flag(i < n_s - 1)
        pv_ext = jnp.concatenate([pvp_ref[0] * keep_prev, pv_ref[0], pvn_ref[0]], axis=0)
        pg_ext = jnp.concatenate([pgp_ref[0] * keep_prev, pg_ref[0], pgn_ref[0]], axis=0)
        taps_v = (_back(pv_ext, 2), _back(pv_ext, 1), pv_ext)
        taps_g = (_back(pg_ext, 2), _back(pg_ext, 1), pg_ext)

        def conv(taps, w_ref, b_ref):
            return (w_ref[2:3, :] * taps[2] + w_ref[1:2, :] * taps[1] + w_ref[0:1, :] * taps[0])[CONV_HALO:] + b_ref[...]

        val, gate = conv(taps_v, wv_ref, bv_ref), conv(taps_g, wg_ref, bg_ref)
        da_ext = jnp.concatenate([da_ref[0], dan_ref[0] * keep_next], axis=0)
        sg = _sigmoid(gate)
        dval = da_ext * gate * sg
        dgate = da_ext * val * (sg * (1.0 + gate * (1.0 - sg)))
        dpv_ref[0] = _conv3_t(dval, wv_ref)[:ts].astype(BF16)
        dpg_ref[0] = _conv3_t(dgate, wg_ref)[:ts].astype(BF16)
        for dh, taps, dw_ref, db_ref in ((dval[:ts], taps_v, dwv_ref, dbv_ref), (dgate[:ts], taps_g, dwg_ref, dbg_ref)):
            db_ref[...] += jnp.sum(dh, axis=0, keepdims=True)
            for k in range(3):
                dw_ref[k:k + 1, :] += jnp.sum(dh * taps[k][CONV_HALO:CONV_HALO + ts], axis=0, keepdims=True)

    def specs(col):
        per = ts // CONV_HALO
        tile = pl.BlockSpec((1, ts, tc), lambda j, b, i: (b, i, col(j)))
        prev = pl.BlockSpec((1, CONV_HALO, tc), lambda j, b, i: (b, jnp.maximum(i * per - 1, 0), col(j)))
        nxt = pl.BlockSpec((1, CONV_HALO, tc), lambda j, b, i: (b, jnp.minimum((i + 1) * per, n_s * per - 1), col(j)))
        return tile, prev, nxt

    tv, pvp, pvn = specs(lambda j: j)
    tg, pgp, pgn = specs(lambda j: j + n_c)
    wv = pl.BlockSpec((3, tc), lambda j, b, i: (0, j))
    wg = pl.BlockSpec((3, tc), lambda j, b, i: (0, j + n_c))
    bv = pl.BlockSpec((1, tc), lambda j, b, i: (0, j))
    bg = pl.BlockSpec((1, tc), lambda j, b, i: (0, j + n_c))
    out_tile = pl.BlockSpec((1, ts, tc), lambda j, b, i: (b, i, j))
    acc3, acc1 = pl.BlockSpec((3, tc), lambda j, b, i: (0, j)), pl.BlockSpec((1, tc), lambda j, b, i: (0, j))
    (dpv, dpg, dwv, dwg, dbv, dbg), received = _pc_cargo(
        body, cargo, name=name, grid=(n_c, b, n_s),
        out_shape=(_sds((b, s, f), BF16), _sds((b, s, f), BF16), _sds((3, f), F32), _sds((3, f), F32),
                   _sds((1, f), F32), _sds((1, f), F32)),
        in_specs=[tv, pvp, pvn, tg, pgp, pgn, tv, pvn, wv, wg, bv, bg],
        out_specs=(out_tile, out_tile, acc3, acc3, acc1, acc1))(p, p, p, p, p, p, da, da, cw, cw, cb, cb)
    return dpv, dpg, jnp.concatenate([dwv, dwg], axis=1), jnp.concatenate([dbv, dbg], axis=1), received


def _shortconv_fwd(name, q, cw):
    b, s, d3 = q.shape
    d = d3 // 3
    ts, tc = _div(s, CONV_ROWS, CONV_HALO), _div(d, 256, LANE)
    n_s, n_c = s // ts, d // tc

    def body(gb_ref, gc_ref, gch_ref, h_ref, hh_ref, w_ref, r_ref):
        keep = _flag(pl.program_id(1) > 0)
        m_ext = jnp.concatenate([gch_ref[0] * hh_ref[0] * keep, gc_ref[0] * h_ref[0]], axis=0)
        r_ref[0] = (gb_ref[0] * _conv3(m_ext, w_ref)[CONV_HALO:]).astype(BF16)

    tb, _, _ = _halo_specs(ts, tc, CONV_HALO, n_s, lambda j: j)
    tcc, hc, _ = _halo_specs(ts, tc, CONV_HALO, n_s, lambda j: j + n_c)
    th, hh, _ = _halo_specs(ts, tc, CONV_HALO, n_s, lambda j: j + 2 * n_c)
    w = pl.BlockSpec((3, tc), lambda b, i, j: (0, j))
    return _pc(body, name=name, grid=(b, n_s, n_c), out_shape=_sds((b, s, d), BF16),
               in_specs=[tb, tcc, hc, th, hh, w], out_specs=pl.BlockSpec((1, ts, tc), lambda b, i, j: (b, i, j)),
               compiler_params=_cp("parallel", "parallel", "parallel"))(q, q, q, q, q, cw)


def _shortconv_bwd(name, q, dr, cw):
    b, s, d3 = q.shape
    d = d3 // 3
    ts, tc = _div(s, CONV_ROWS, CONV_HALO), _div(d, 256, LANE)
    n_s, n_c = s // ts, d // tc

    def body(gb_ref, gbn_ref, gc_ref, gcp_ref, h_ref, hp_ref, dr_ref, drn_ref, w_ref,
             dgb_ref, dgc_ref, dh_ref, dw_ref):
        bi, i = pl.program_id(1), pl.program_id(2)

        @pl.when((bi == 0) & (i == 0))
        def _():
            dw_ref[...] = jnp.zeros_like(dw_ref)

        keep_prev = _flag(i > 0)
        keep_next = _flag(i < n_s - 1)
        gc, h = gc_ref[0], h_ref[0]
        m_ext = jnp.concatenate([gcp_ref[0] * hp_ref[0] * keep_prev, gc * h], axis=0)
        cm = _conv3(m_ext, w_ref)[CONV_HALO:]
        dr_ = dr_ref[0]
        dgb_ref[0] = (dr_ * cm).astype(BF16)
        dcv_ext = jnp.concatenate([dr_ * gb_ref[0], drn_ref[0] * gbn_ref[0] * keep_next], axis=0)
        dm = _conv3_t(dcv_ext, w_ref)[:ts]
        dgc_ref[0] = (dm * h).astype(BF16)
        dh_ref[0] = (dm * gc).astype(BF16)
        dcv = dcv_ext[:ts]
        for k in range(3):
            shifted = m_ext if k == 2 else _back(m_ext, 2 - k)
            dw_ref[k:k + 1, :] += jnp.sum(dcv * shifted[CONV_HALO:], axis=0, keepdims=True)

    def specs(col):
        per = ts // CONV_HALO
        tile = pl.BlockSpec((1, ts, tc), lambda j, b, i: (b, i, col(j)))
        prev = pl.BlockSpec((1, CONV_HALO, tc), lambda j, b, i: (b, jnp.maximum(i * per - 1, 0), col(j)))
        nxt = pl.BlockSpec((1, CONV_HALO, tc), lambda j, b, i: (b, jnp.minimum((i + 1) * per, n_s * per - 1), col(j)))
        return tile, prev, nxt

    tb, _, nb = specs(lambda j: j)
    tcc, pc_, _ = specs(lambda j: j + n_c)
    th, ph, _ = specs(lambda j: j + 2 * n_c)
    w = pl.BlockSpec((3, tc), lambda j, b, i: (0, j))
    out_tile = pl.BlockSpec((1, ts, tc), lambda j, b, i: (b, i, j))
    o = _sds((b, s, d), BF16)
    return _pc(body, name=name, grid=(n_c, b, n_s), out_shape=(o, o, o, _sds((3, d), F32)),
               in_specs=[tb, nb, tcc, pc_, th, ph, tb, nb, w], out_specs=(out_tile, out_tile, out_tile, w),
               compiler_params=_cp("parallel", "arbitrary", "arbitrary"))(q, q, q, q, q, q, dr, dr, cw)


def _pick_window(group, cands):
    gid = jnp.full(cands[0].shape, group, jnp.int32)
    out = cands[-1]
    for k in range(len(cands) - 2, -1, -1):
        out = jnp.where(gid == k, cands[k], out)
    return out


def _window_sums(v, shift):
    s1 = v + shift(v, 1)
    s2 = s1 + shift(s1, 2)
    s3 = s2 + shift(s2, 4)
    s4 = s3 + shift(s3, 8)
    return [s1, s2, s3, s4]


def _pool_counts(group, first_row, n_rows, cols):
    t = _rows_iota((n_rows, cols)) + first_row
    window = _pick_window(group, [jnp.full((n_rows, cols), 2 << k, jnp.int32) for k in range(POOL_GROUPS)])
    return jnp.minimum(t + 1, window).astype(F32)


def _pool_fwd(name, x, sc, sh, w, scale):
    b, s, d = x.shape
    tc = d // POOL_GROUPS
    ts = _div(s, CONV_ROWS, POOL_HALO)
    n_s = s // ts

    def body(x_ref, xp_ref, sc_ref, sh_ref, w_ref, scale_ref, y_ref):
        i, grp = pl.program_id(1), pl.program_id(2)
        keep = _flag(i > 0)
        mod = 1.0 + sc_ref[0]
        u = x_ref[0] * mod + sh_ref[0]
        u_ext = jnp.concatenate([(xp_ref[0] * mod + sh_ref[0]) * keep, u], axis=0)
        summed = _pick_window(grp, _window_sums(u_ext, _back))[POOL_HALO:]
        pooled = summed / _pool_counts(grp, i * ts, ts, tc) - u
        y_ref[0] = jnp.dot(pooled.astype(BF16), w_ref[0], preferred_element_type=F32) * scale_ref[...]

    tile, prev, _ = _halo_specs(ts, tc, POOL_HALO, n_s, lambda j: j)
    seq = pl.BlockSpec((1, 1, tc), lambda b, i, j: (b, 0, j))
    return _pc(body, name=name, grid=(b, n_s, POOL_GROUPS), out_shape=_sds(x.shape, F32),
               in_specs=[tile, prev, seq, seq, pl.BlockSpec((1, tc, tc), lambda b, i, j: (j, 0, 0)),
                         pl.BlockSpec((1, tc), lambda b, i, j: (0, j))],
               out_specs=pl.BlockSpec((1, ts, tc), lambda b, i, j: (b, i, j)),
               compiler_params=_cp("parallel", "parallel", "parallel"))(x, x, sc, sh, w, scale)


def _pool_bwd(name, x, sc, sh, dy, w, w_t, scale, cargo=()):
    b, s, d = x.shape
    tc = d // POOL_GROUPS
    ts = _div(s, CONV_ROWS, POOL_HALO)
    n_s = s // ts

    def body(x_ref, xp_ref, sc_ref, sh_ref, dy_ref, dyn_ref, w_ref, wt_ref, scale_ref, du_ref, dw_ref, dscale_ref):
        grp, bi, i = pl.program_id(0), pl.program_id(1), pl.program_id(2)

        @pl.when((bi == 0) & (i == 0))
        def _():
            dw_ref[...] = jnp.zeros_like(dw_ref)
            dscale_ref[...] = jnp.zeros_like(dscale_ref)

        keep_prev = _flag(i > 0)
        keep_next = _flag(i < n_s - 1)
        mod = 1.0 + sc_ref[0]
        u = x_ref[0] * mod + sh_ref[0]
        u_ext = jnp.concatenate([(xp_ref[0] * mod + sh_ref[0]) * keep_prev, u], axis=0)
        summed = _pick_window(grp, _window_sums(u_ext, _back))[POOL_HALO:]
        pooled = (summed / _pool_counts(grp, i * ts, ts, tc) - u).astype(BF16)
        dy_ = dy_ref[0].astype(F32)
        ymat = jnp.dot(pooled, w_ref[0], preferred_element_type=F32)
        dscale_ref[...] += jnp.sum(dy_ * ymat, axis=0, keepdims=True)
        dys_ext = (jnp.concatenate([dy_, dyn_ref[0].astype(F32) * keep_next], axis=0) * scale_ref[...]).astype(BF16)
        dw_ref[0] += lax.dot_general(pooled, dys_ext[:ts], (((0,), (0,)), ((), ())), preferred_element_type=F32)
        dpooled = jnp.dot(dys_ext, wt_ref[0], preferred_element_type=F32)
        e = dpooled / _pool_counts(grp, i * ts, ts + POOL_HALO, tc)
        du_ref[0] = _pick_window(grp, _window_sums(e, _ahead))[:ts] - dpooled[:ts]

    per = ts // POOL_HALO
    tile = pl.BlockSpec((1, ts, tc), lambda j, b, i: (b, i, j))
    prev = pl.BlockSpec((1, POOL_HALO, tc), lambda j, b, i: (b, jnp.maximum(i * per - 1, 0), j))
    nxt = pl.BlockSpec((1, POOL_HALO, tc), lambda j, b, i: (b, jnp.minimum((i + 1) * per, n_s * per - 1), j))
    seq = pl.BlockSpec((1, 1, tc), lambda j, b, i: (b, 0, j))
    wsp = pl.BlockSpec((1, tc, tc), lambda j, b, i: (j, 0, 0))
    vec = pl.BlockSpec((1, tc), lambda j, b, i: (0, j))
    (du, dw, dscale), received = _pc_cargo(
        body, cargo, name=name, grid=(POOL_GROUPS, b, n_s),
        out_shape=(_sds(x.shape, F32), _sds((POOL_GROUPS, tc, tc), F32), _sds((1, d), F32)),
        in_specs=[tile, prev, seq, seq, tile, nxt, wsp, wsp, vec],
        out_specs=(tile, wsp, vec))(x, x, sc, sh, dy, dy, w, w_t, scale)
    return du, dw, dscale, received


def _rope_swap(v):
    lane = lax.broadcasted_iota(jnp.int32, v.shape, v.ndim - 1)
    lo, hi = QK_NOPE, QK_NOPE + QK_ROPE // 2
    from_above = pltpu.roll(v, HEAD_PAD - QK_ROPE // 2, axis=v.ndim - 1)
    from_below = pltpu.roll(v, QK_ROPE // 2, axis=v.ndim - 1)
    return jnp.where((lane >= lo) & (lane < hi), from_above,
                     jnp.where((lane >= hi) & (lane < hi + QK_ROPE // 2), from_below, 0.0))


def _rope(v, cos_t, sin_t):
    return v * cos_t + _rope_swap(v) * sin_t


def _rope_t(dv, cos_t, sin_t):
    return dv * cos_t + _rope_swap(dv * sin_t)


def _rms(v, g):
    r = lax.rsqrt(jnp.mean(v * v, axis=-1, keepdims=True) + RMS_EPS)
    return v * r, r


def _mla_norm_fwd(name, a, qn, kvn, cos_t, sin_t):
    b, s, wa = a.shape
    ql, kvl = qn.shape[1], kvn.shape[1]
    ts = _div(s, 512, 16)

    def body(aq_ref, akv_ref, ape_ref, qn_ref, kvn_ref, cos_ref, sin_ref, cq_ref, ckv_ref, kpe_ref):
        yq, _ = _rms(aq_ref[0], None)
        cq_ref[0] = (yq * qn_ref[...]).astype(BF16)
        ykv, _ = _rms(akv_ref[0], None)
        ckv_ref[0] = (ykv * kvn_ref[...]).astype(BF16)
        kpe_ref[0] = _rope(ape_ref[0], cos_ref[0], sin_ref[0])

    tok = lambda w, col: pl.BlockSpec((1, ts, w), lambda b, i: (b, i, col))
    return _pc(body, name=name, grid=(b, s // ts),
               out_shape=(_sds((b, s, ql), BF16), _sds((b, s, kvl), BF16), _sds((b, s, HEAD_PAD), F32)),
               in_specs=[tok(ql, 0), tok(kvl, ql // kvl), tok(HEAD_PAD, (ql + kvl) // HEAD_PAD), _vec_spec(ql),
                         _vec_spec(kvl), tok(HEAD_PAD, 0), tok(HEAD_PAD, 0)],
               out_specs=(tok(ql, 0), tok(kvl, 0), tok(HEAD_PAD, 0)),
               compiler_params=_cp("parallel", "parallel"))(a, a, a, qn, kvn, cos_t, sin_t)


def _mla_norm_bwd(name, a, dcq, dckv, dkpe, qn, kvn):
    b, s, wa = a.shape
    ql, kvl = qn.shape[1], kvn.shape[1]
    ts = _div(s, 512, 16)

    def body(a_ref, dcq_ref, dckv_ref, dkpe_ref, qn_ref, kvn_ref, da_ref, dqn_ref, dkvn_ref):
        @pl.when((pl.program_id(0) == 0) & (pl.program_id(1) == 0))
        def _():
            dqn_ref[...] = jnp.zeros_like(dqn_ref)
            dkvn_ref[...] = jnp.zeros_like(dkvn_ref)

        def one(v, dc, g_ref, dg_ref):
            yv, r = _rms(v, None)
            dg_ref[...] += jnp.sum(dc * yv, axis=0, keepdims=True)
            dyv = dc * g_ref[...]
            return r * (dyv - yv * jnp.mean(dyv * yv, axis=-1, keepdims=True))

        av = a_ref[0]
        da_ref[0, :, 0:ql] = one(av[:, 0:ql], dcq_ref[0], qn_ref, dqn_ref).astype(BF16)
        da_ref[0, :, ql:ql + kvl] = one(av[:, ql:ql + kvl], dckv_ref[0], kvn_ref, dkvn_ref).astype(BF16)
        da_ref[0, :, ql + kvl:] = dkpe_ref[0].astype(BF16)

    return _pc(body, name=name, grid=(b, s // ts),
               out_shape=(_sds(a.shape, BF16), _sds((1, ql), F32), _sds((1, kvl), F32)),
               in_specs=[_tok_spec(ts, wa), _tok_spec(ts, ql), _tok_spec(ts, kvl), _tok_spec(ts, HEAD_PAD),
                         _vec_spec(ql), _vec_spec(kvl)],
               out_specs=(_tok_spec(ts, wa), _vec_spec(ql), _vec_spec(kvl)),
               compiler_params=_cp("arbitrary", "arbitrary"))(a, dcq, dckv, dkpe, qn, kvn)


def _mla_prep_fwd(name, q_raw, k_raw, kpe, cos_t, sin_t, n_heads):
    b, s, wq = q_raw.shape
    ts = _div(s, 256, 16)

    def body(q_ref, k_ref, kpe_ref, cos_ref, sin_ref, qo_ref, ko_ref):
        cos_, sin_, kpe_ = cos_ref[0], sin_ref[0], kpe_ref[0]
        for h in range(n_heads):
            lanes = slice(h * HEAD_PAD, (h + 1) * HEAD_PAD)
            qo_ref[0, :, lanes] = _rope(q_ref[0, :, lanes], cos_, sin_).astype(BF16)
            ko_ref[0, :, lanes] = (k_ref[0, :, lanes] + kpe_).astype(BF16)

    wide = pl.BlockSpec((1, ts, wq), lambda b, i: (b, i, 0))
    tok = pl.BlockSpec((1, ts, HEAD_PAD), lambda b, i: (b, i, 0))
    o = _sds(q_raw.shape, BF16)
    return _pc(body, name=name, grid=(b, s // ts), out_shape=(o, o),
               in_specs=[wide, wide, tok, tok, tok], out_specs=(wide, wide),
               compiler_params=_cp("parallel", "parallel"))(q_raw, k_raw, kpe, cos_t, sin_t)


def _mla_prep_bwd(name, dq, dk, cos_t, sin_t, n_heads):
    b, s, wq = dq.shape
    ts = _div(s, 256, 16)

    def body(dq_ref, dk_ref, cos_ref, sin_ref, dqr_ref, dkr_ref, dkpe_ref):
        cos_, sin_ = cos_ref[0], sin_ref[0]
        dk_sum = None
        for h in range(n_heads):
            lanes = slice(h * HEAD_PAD, (h + 1) * HEAD_PAD)
            dqr_ref[0, :, lanes] = _rope_t(dq_ref[0, :, lanes], cos_, sin_).astype(BF16)
            dk_h = dk_ref[0, :, lanes]
            dkr_ref[0, :, lanes] = dk_h.astype(BF16)
            dk_sum = dk_h if dk_sum is None else dk_sum + dk_h
        dkpe_ref[0] = _rope_t(dk_sum, cos_, sin_)

    wide = pl.BlockSpec((1, ts, wq), lambda b, i: (b, i, 0))
    tok = pl.BlockSpec((1, ts, HEAD_PAD), lambda b, i: (b, i, 0))
    return _pc(body, name=name, grid=(b, s // ts),
               out_shape=(_sds(dq.shape, BF16), _sds(dq.shape, BF16), _sds((b, s, HEAD_PAD), F32)),
               in_specs=[wide, wide, tok, tok], out_specs=(wide, wide, tok),
               compiler_params=_cp("parallel", "parallel"))(dq, dk, cos_t, sin_t)


FLASH_TILE = 1024
LOG2_E = 1.4426950408889634


def _heads_per_step(n_heads):
    return 2 if n_heads % 2 == 0 else 1


def _causal_mask(i, j, tq, tk):
    rows = lax.broadcasted_iota(jnp.int32, (tq, tk), 0) + i * tq
    cols = lax.broadcasted_iota(jnp.int32, (tq, tk), 1) + j * tk
    return cols <= rows


def _nt(a, b):
    return lax.dot_general(a, b, (((1,), (1,)), ((), ())), preferred_element_type=F32)


def _tn(a, b):
    return lax.dot_general(a, b, (((0,), (0,)), ((), ())), preferred_element_type=F32)


def _flash_fwd(name, q, k, v, n_heads, sm_scale, cargo=()):
    b, s, _ = q.shape
    t, hp = _div(s, FLASH_TILE, LANE), _heads_per_step(n_heads)
    n, w = s // t, hp * HEAD_PAD
    neg = float(jnp.finfo(jnp.float32).min)
    c2 = sm_scale * LOG2_E

    def body(q_ref, k_ref, v_ref, o_ref, lse_ref, m_ref, l_ref, acc_ref):
        i, j = pl.program_id(2), pl.program_id(3)

        @pl.when(j == 0)
        def _():
            m_ref[...] = jnp.full(m_ref.shape, neg, F32)
            l_ref[...] = jnp.zeros_like(l_ref)
            acc_ref[...] = jnp.zeros_like(acc_ref)

        def block(on_diagonal):
            for hh in range(hp):
                ln = slice(hh * HEAD_PAD, (hh + 1) * HEAD_PAD)
                sc = _nt(q_ref[0, :, ln], k_ref[0, :, ln])
                if on_diagonal:
                    sc = jnp.where(_causal_mask(i, j, t, t), sc, neg)
                m_old = m_ref[hh]
                m_new = jnp.maximum(m_old, jnp.max(sc, axis=-1, keepdims=True))
                p = jnp.exp2((sc - m_new) * c2)
                corr = jnp.exp2((m_old - m_new) * c2)
                l_ref[hh] = corr * l_ref[hh] + jnp.sum(p, axis=-1, keepdims=True)
                acc_ref[:, ln] = corr * acc_ref[:, ln] + jnp.dot(p.astype(BF16), v_ref[0, :, ln],
                                                                 preferred_element_type=F32)
                m_ref[hh] = m_new

        pl.when(j < i)(functools.partial(block, False))
        pl.when(j == i)(functools.partial(block, True))

        @pl.when(j == n - 1)
        def _():
            for hh in range(hp):
                ln = slice(hh * HEAD_PAD, (hh + 1) * HEAD_PAD)
                o_ref[0, :, ln] = (acc_ref[:, ln] / l_ref[hh]).astype(BF16)
                lse_ref[0, :, ln] = jnp.broadcast_to(m_ref[hh] * sm_scale + jnp.log(l_ref[hh]), (t, HEAD_PAD))

    qs = pl.BlockSpec((1, t, w), lambda b, h, i, j: (b, i, h))
    ks = pl.BlockSpec((1, t, w), lambda b, h, i, j: (b, jnp.minimum(j, i), h))
    (o, lse), fetched = _pc_cargo(
        body, cargo, name=name, grid=(b, n_heads // hp, n, n), route=_Fetch,
        out_shape=(_sds(q.shape, BF16), _sds(q.shape, F32)), in_specs=[qs, ks, ks], out_specs=(qs, qs),
        scratch_shapes=[pltpu.VMEM((hp, t, 1), F32), pltpu.VMEM((hp, t, 1), F32), pltpu.VMEM((t, w), F32)])(q, k, v)
    return o, lse, fetched


def _flash_bwd(name, q, k, v, o, lse, do, n_heads, sm_scale):
    b, s, _ = q.shape
    t, hp = _div(s, FLASH_TILE, LANE), _heads_per_step(n_heads)
    n, w = s // t, hp * HEAD_PAD
    c2 = sm_scale * LOG2_E

    def body(q_ref, k_ref, v_ref, o_ref, lse_ref, do_ref, dq_hbm, dk_ref, dv_ref, dq_acc, dk_acc, dv_acc, dq_sem):
        bi, hi, j, i = pl.program_id(0), pl.program_id(1), pl.program_id(2), pl.program_id(3)

        @pl.when(i == 0)
        def _():
            dk_acc[...] = jnp.zeros_like(dk_acc)
            dv_acc[...] = jnp.zeros_like(dv_acc)

        rows = pl.ds(pl.multiple_of(i * t, t), t)

        def block(on_diagonal):
            for hh in range(hp):
                ln = slice(hh * HEAD_PAD, (hh + 1) * HEAD_PAD)
                do_ = do_ref[0, :, ln]
                delta = jnp.sum(do_.astype(F32) * o_ref[0, :, ln].astype(F32), axis=-1, keepdims=True)
                sc = _nt(q_ref[0, :, ln], k_ref[0, :, ln])
                p = jnp.exp2(sc * c2 - lse_ref[0, :, hh * HEAD_PAD:hh * HEAD_PAD + 1] * LOG2_E)
                if on_diagonal:
                    p = jnp.where(_causal_mask(i, j, t, t), p, 0.0)
                dv_acc[:, ln] += _tn(p.astype(BF16), do_)
                dp = _nt(do_, v_ref[0, :, ln])
                ds = (p * (dp - delta)).astype(BF16)
                dk_acc[:, ln] += _tn(ds, q_ref[0, :, ln])
                dq_part = jnp.dot(ds, k_ref[0, :, ln], preferred_element_type=F32)

                @pl.when(j == 0)
                def _():
                    dq_acc[rows, ln] = dq_part

                @pl.when(j > 0)
                def _():
                    dq_acc[rows, ln] += dq_part

        pl.when(i > j)(functools.partial(block, False))
        pl.when(i == j)(functools.partial(block, True))

        @pl.when(i == j)
        def _():
            dq_acc[rows, :] = dq_acc[rows, :] * sm_scale
            done = pltpu.make_async_copy(dq_acc.at[rows, :], dq_hbm.at[bi, rows, pl.ds(pl.multiple_of(hi * w, w), w)],
                                         dq_sem)
            done.start()
            done.wait()

        @pl.when(i == n - 1)
        def _():
            dk_ref[0] = dk_acc[...] * sm_scale
            dv_ref[0] = dv_acc[...].astype(BF16)

    qs = pl.BlockSpec((1, t, w), lambda b, h, j, i: (b, jnp.maximum(i, j), h))
    ks = pl.BlockSpec((1, t, w), lambda b, h, j, i: (b, j, h))
    return _pc(body, name=name, grid=(b, n_heads // hp, n, n),
               out_shape=(_sds(q.shape, F32), _sds(q.shape, F32), _sds(q.shape, BF16)),
               in_specs=[qs, ks, ks, qs, qs, qs], out_specs=(pl.BlockSpec(memory_space=pltpu.HBM), ks, ks),
               scratch_shapes=[pltpu.VMEM((s, w), F32), pltpu.VMEM((t, w), F32), pltpu.VMEM((t, w), F32),
                               pltpu.SemaphoreType.DMA],
               compiler_params=_cp("arbitrary", "arbitrary", "arbitrary", "arbitrary"))(q, k, v, o, lse, do)


def _mod_fwd(name, c_all, w, bias):
    depth, d, n = w.shape
    rows = c_all.shape[0]

    def body(c_ref, w_ref, b_ref, o_ref):
        cv = c_ref[...]
        cond = (cv * _sigmoid(cv)).astype(BF16)
        o_ref[0] = jnp.dot(cond, w_ref[0].astype(BF16), preferred_element_type=F32) + b_ref[0]

    return _pc(body, name=name, grid=(depth,), out_shape=_sds((depth, rows, n), F32),
               in_specs=[pl.BlockSpec((rows, d), lambda l: (0, 0)), pl.BlockSpec((1, d, n), lambda l: (l, 0, 0)),
                         pl.BlockSpec((1, 1, n), lambda l: (l, 0, 0))],
               out_specs=pl.BlockSpec((1, rows, n), lambda l: (l, 0, 0)), compiler_params=_cp("parallel"))(c_all, w, bias)


def _mod_bwd(name, c_all, dmod_cols, dmod_all):
    depth, rows, n = dmod_cols.shape
    d = c_all.shape[1]
    n_all = dmod_all.shape[2]
    tn = _div(n, 512, LANE)

    def body(c_ref, dm_ref, dma_ref, gw_ref, gb_ref):
        cv = c_ref[...]
        cond = (cv * _sigmoid(cv)).astype(BF16)
        gw_ref[0] = _tn(cond, dm_ref[0].astype(BF16))

        @pl.when(pl.program_id(1) == 0)
        def _():
            gb_ref[0] = jnp.sum(dma_ref[0], axis=0, keepdims=True)

    return _pc(body, name=name, grid=(depth, n // tn),
               out_shape=(_sds((depth, d, n), F32), _sds((depth, 1, n_all), F32)),
               in_specs=[pl.BlockSpec((rows, d), lambda l, j: (0, 0)), pl.BlockSpec((1, rows, tn), lambda l, j: (l, 0, j)),
                         pl.BlockSpec((1, rows, n_all), lambda l, j: (l, 0, 0))],
               out_specs=(pl.BlockSpec((1, d, tn), lambda l, j: (l, 0, j)), pl.BlockSpec((1, 1, n_all), lambda l, j: (l, 0, 0))),
               compiler_params=_cp("parallel", "arbitrary"))(c_all, dmod_cols, dmod_all)


def _adamw(name, w, g, m, v):
    shape = w.shape
    cols = shape[-1]
    rows = _size(shape) // cols
    tr = _div(rows, max(SUBLANE, (2 ** 19) // cols // SUBLANE * SUBLANE), SUBLANE)
    c1 = 1.0 - ADAM_B1 ** ADAM_STEP
    c2 = 1.0 - ADAM_B2 ** ADAM_STEP

    def body(w_ref, g_ref, m_ref, v_ref, d_ref, mo_ref, vo_ref):
        gv = g_ref[...]
        m_new = ADAM_B1 * m_ref[...] + (1.0 - ADAM_B1) * gv
        v_new = ADAM_B2 * v_ref[...] + (1.0 - ADAM_B2) * (gv * gv)
        m_hat = m_new / c1
        v_hat = v_new / c2
        d_ref[...] = -ADAM_LR * (m_hat / (jnp.sqrt(v_hat) + ADAM_EPS) + ADAM_WD * w_ref[...])
        mo_ref[...] = m_new
        vo_ref[...] = v_new

    spec = pl.BlockSpec((tr, cols), lambda i: (i, 0))
    o = _sds((rows, cols), F32)
    outs = _pc(body, name=name, grid=(rows // tr,), out_shape=(o, o, o), in_specs=[spec] * 4, out_specs=(spec,) * 3,
               compiler_params=_cp("parallel"))(*[a.reshape(rows, cols) for a in (w, g, m, v)])
    return tuple(a.reshape(shape) for a in outs)


def _rope_tables(positions):
    half = QK_ROPE // 2
    inv_freq = ROPE_THETA ** (-jnp.arange(0, QK_ROPE, 2, dtype=F32) / QK_ROPE)
    ang = positions.astype(F32)[..., None] * inv_freq
    cos, sin = jnp.cos(ang), jnp.sin(ang)
    lead = positions.shape
    ones = jnp.ones(lead + (QK_NOPE,), F32)
    tail_one = jnp.ones(lead + (HEAD_PAD - QK_NOPE - QK_ROPE,), F32)
    cos_t = jnp.concatenate([ones, cos, cos, tail_one], axis=-1)
    sin_t = jnp.concatenate([0 * ones, -sin, sin, 0 * tail_one], axis=-1)
    return cos_t, sin_t


def _pad_heads(w, n_heads, parts, axis):
    w = jnp.moveaxis(w, axis, -1)
    lead = w.shape[:-1]
    per = w.shape[-1] // n_heads
    w = w.reshape(lead + (n_heads, per))
    kept = jnp.concatenate([w[..., a:b_] for a, b_ in parts], axis=-1)
    pad = HEAD_PAD - kept.shape[-1]
    kept = jnp.concatenate([kept, jnp.zeros(lead + (n_heads, pad), w.dtype)], axis=-1)
    return jnp.moveaxis(kept.reshape(lead + (n_heads * HEAD_PAD,)), -1, axis)


def _unpad_heads(g, n_heads, width, axis):
    g = jnp.moveaxis(g, axis, -1)
    lead = g.shape[:-1]
    g = g.reshape(lead + (n_heads, HEAD_PAD))[..., :width]
    return g, lead


def kernel(x, c, positions, mod_w, mod_b, ln_g, ln_b, pool_w, pool_scale, mla_w_a, mla_q_norm, mla_w_uq, mla_kv_norm, mla_w_ukv, mla_w_o, sc_w_in, sc_conv, sc_w_out, ffn_w_up, ffn_conv, ffn_conv_b, ffn_w_down, loss_target, m_mod_w, m_mod_b, m_ln_g, m_ln_b, m_pool_w, m_pool_scale, m_mla_w_a, m_mla_q_norm, m_mla_w_uq, m_mla_kv_norm, m_mla_w_ukv, m_mla_w_o, m_sc_w_in, m_sc_conv, m_sc_w_out, m_ffn_w_up, m_ffn_conv, m_ffn_conv_b, m_ffn_w_down, v_mod_w, v_mod_b, v_ln_g, v_ln_b, v_pool_w, v_pool_scale, v_mla_w_a, v_mla_q_norm, v_mla_w_uq, v_mla_kv_norm, v_mla_w_ukv, v_mla_w_o, v_sc_w_in, v_sc_conv, v_sc_w_out, v_ffn_w_up, v_ffn_conv, v_ffn_conv_b, v_ffn_w_down):
    wts = dict(mod_w=mod_w, mod_b=mod_b, ln_g=ln_g, ln_b=ln_b, pool_w=pool_w, pool_scale=pool_scale, mla_w_a=mla_w_a,
               mla_q_norm=mla_q_norm, mla_w_uq=mla_w_uq, mla_kv_norm=mla_kv_norm, mla_w_ukv=mla_w_ukv, mla_w_o=mla_w_o,
               sc_w_in=sc_w_in, sc_conv=sc_conv, sc_w_out=sc_w_out, ffn_w_up=ffn_w_up, ffn_conv=ffn_conv,
               ffn_conv_b=ffn_conv_b, ffn_w_down=ffn_w_down)
    mom1 = dict(mod_w=m_mod_w, mod_b=m_mod_b, ln_g=m_ln_g, ln_b=m_ln_b, pool_w=m_pool_w, pool_scale=m_pool_scale,
                mla_w_a=m_mla_w_a, mla_q_norm=m_mla_q_norm, mla_w_uq=m_mla_w_uq, mla_kv_norm=m_mla_kv_norm,
                mla_w_ukv=m_mla_w_ukv, mla_w_o=m_mla_w_o, sc_w_in=m_sc_w_in, sc_conv=m_sc_conv, sc_w_out=m_sc_w_out,
                ffn_w_up=m_ffn_w_up, ffn_conv=m_ffn_conv, ffn_conv_b=m_ffn_conv_b, ffn_w_down=m_ffn_w_down)
    mom2 = dict(mod_w=v_mod_w, mod_b=v_mod_b, ln_g=v_ln_g, ln_b=v_ln_b, pool_w=v_pool_w, pool_scale=v_pool_scale,
                mla_w_a=v_mla_w_a, mla_q_norm=v_mla_q_norm, mla_w_uq=v_mla_w_uq, mla_kv_norm=v_mla_kv_norm,
                mla_w_ukv=v_mla_w_ukv, mla_w_o=v_mla_w_o, sc_w_in=v_sc_w_in, sc_conv=v_sc_conv, sc_w_out=v_sc_w_out,
                ffn_w_up=v_ffn_w_up, ffn_conv=v_ffn_conv, ffn_conv_b=v_ffn_conv_b, ffn_w_down=v_ffn_w_down)

    bsz, seq, d = x.shape
    depth = mod_b.shape[0]
    n_tok = bsz * seq
    n_heads = d // V_HEAD
    ql, kvl = mla_q_norm.shape[1], mla_kv_norm.shape[1]
    alpha = float((2 * depth) ** 0.25)
    sm_scale = float((QK_NOPE + QK_ROPE) ** -0.5)
    mx, my, mc = lax.axis_index("x"), lax.axis_index("y"), lax.axis_index("c")
    chip = 2 * mx + my
    dev = 2 * chip + mc

    small_names = list(SMALL_SHARDED)
    small_pack, small_spans = _pack_rows([c] + [wts[n] for n in small_names], F32, SUBLANE)
    rows_small = small_pack.shape[0]
    small_all = _all_gather8("gather_small_params", small_pack, True).reshape(N_DEV, rows_small * PACK_COLS)
    c_all = small_all[:, :c.size].reshape(N_DEV * bsz, d)
    per_chip = small_all[0::2]
    full = dict(wts)
    for n, (off, shape) in zip(small_names, small_spans[1:]):
        blocks = per_chip[:, off:off + _size(shape)].reshape((N_CHIP,) + tuple(shape))
        full[n] = _join_chips(blocks, SMALL_SHARDED[n])

    n_mod = mod_w.shape[2]
    bias_cols = lax.dynamic_slice_in_dim(mod_b, chip * n_mod, n_mod, axis=1)[:, None, :]
    mod_cols = _mod_fwd("mod_fwd", c_all, mod_w, bias_cols)
    half_rows = (N_DEV * bsz) // 2
    mod_half = lax.dynamic_slice_in_dim(mod_cols, mc * half_rows, half_rows, axis=1).reshape(depth * half_rows, n_mod)
    mod_all = _all_gather8("gather_mod", mod_half, True).reshape(N_CHIP, 2, depth, half_rows, n_mod)
    mod_all = jnp.transpose(mod_all, (2, 1, 3, 0, 4)).reshape(depth, N_DEV * bsz, N_CHIP * n_mod)
    mod_mine = lax.dynamic_slice_in_dim(mod_all, dev * bsz, bsz, axis=1)
    mods = [[mod_mine[l, :, k * d:(k + 1) * d][:, None, :] for k in range(6)] for l in range(depth)]

    big_names = list(BIG)
    f_hid = ffn_w_down.shape[1] * N_CHIP
    host_layer = 1
    assert depth > host_layer

    def layer_of(n, i):
        if n == 'pool_w':
            return 3 * i
        if n.startswith('mla_'):
            return 3 * i + 1
        if n.startswith('sc_'):
            return 3 * i + 2
        return i

    last_layer_of_group = (0, host_layer)

    def group_of(n, i):
        return sum(layer_of(n, i) > top for top in last_layer_of_group)

    n_groups = len(last_layer_of_group) + 1
    span = {n: [[i for i in range(wts[n].shape[0]) if group_of(n, i) == g] for g in range(n_groups)] for n in big_names}
    members = [[n for n in big_names if span[n][g]] for g in range(n_groups)]

    def rows2d(a):
        return a.astype(BF16).reshape(-1, a.shape[-1])

    def layouts(bc):
        out = {}
        for n in ('pool_w', 'mla_w_a', 'mla_w_uq', 'mla_w_ukv', 'mla_w_o', 'sc_w_out'):
            if n in bc:
                out[n] = jnp.concatenate([bc[n][j] for j in range(N_CHIP)], axis=BIG[n])
        if 'ffn_w_up' in bc:
            nl = bc['ffn_w_up'].shape[1]
            out['up_cols'] = bc['ffn_w_up'].reshape(N_CHIP, nl * d, -1)
            out['up_rows'] = jnp.transpose(bc['ffn_w_up'], (1, 0, 3, 2)).reshape(1, nl * 2 * f_hid, d)
            out['down_rows'] = jnp.transpose(bc['ffn_w_down'], (1, 0, 2, 3)).reshape(1, nl * f_hid, d)
            out['down_t'] = jnp.transpose(bc['ffn_w_down'], (1, 3, 0, 2)).reshape(1, nl * d, f_hid)
        if 'sc_w_in' in bc:
            ns = bc['sc_w_in'].shape[1]
            out['in_cols'] = bc['sc_w_in'].reshape(N_CHIP, ns * d, -1)
            out['in_rows'] = jnp.transpose(bc['sc_w_in'], (1, 0, 3, 2)).reshape(1, ns * 3 * d, d)
        return out

    shards = [{n: rows2d(wts[n][span[n][g][0]:span[n][g][-1] + 1]) for n in members[g]} for g in range(n_groups)]
    lay = [None] * n_groups

    def by_chip(g, n, blocks):
        return blocks.reshape((N_CHIP, len(span[n][g])) + wts[n].shape[1:])

    def fetched_group(g, got):
        lay[g] = layouts({n: by_chip(g, n, lax.dynamic_update_index_in_dim(blocks, shards[g][n], chip, 0))
                          for n, blocks in got.items()})

    gathered = _gather_weights("gather_weights", [shards[0][n] for n in members[0]])
    lay[0] = layouts({n: by_chip(0, n, blocks) for n, blocks in zip(members[0], gathered)})

    def grp(n, i):
        g = group_of(n, i)
        return lay[g], i - span[n][g][0]

    nope_rope = [(0, QK_NOPE + QK_ROPE)]
    cos_t, sin_t = _rope_tables(positions)

    def t2(a):
        return a.reshape(n_tok, a.shape[-1])

    def t3(a):
        return a.reshape(bsz, seq, a.shape[-1])

    saved = []
    xin = x
    u = _modulate("modulate_in", x, mods[0][1], mods[0][0])
    loss_acc = None
    for l in range(depth):
        sh1, sc1, g1, sh2, sc2, g2 = mods[l]
        kind, j = l % 3, l // 3
        st = dict(x=xin)
        if kind == 0:
            grp_l, jj = grp('pool_w', j)
            w = grp_l['pool_w'][jj]
            st.update(w=w, w_t=jnp.swapaxes(w, 1, 2), scale=full['pool_scale'][j][None, :])
            y = _pool_fwd(f"pool_fwd_{l}", xin, sc1, sh1, st['w'], st['scale'])
        elif kind == 1:
            grp_l, jj = grp('mla_w_a', j)
            wa, wuq, wukv = grp_l['mla_w_a'][jj], grp_l['mla_w_uq'][jj], grp_l['mla_w_ukv'][jj]
            zeros = jnp.zeros((d, QK_NOPE), BF16)
            w_a = jnp.concatenate([wa[:, :ql + kvl], zeros, wa[:, ql + kvl:], zeros[:, :HEAD_PAD - QK_NOPE - QK_ROPE]], axis=1)
            w_uq = _pad_heads(wuq, n_heads, nope_rope, 1)
            w_k = _pad_heads(wukv, n_heads, [(0, QK_NOPE)], 1)
            w_v = _pad_heads(wukv, n_heads, [(QK_NOPE, QK_NOPE + V_HEAD)], 1)
            w_o = _pad_heads(grp_l['mla_w_o'][jj], n_heads, [(0, V_HEAD)], 0)
            qn, kvn = mla_q_norm[j][None, :], mla_kv_norm[j][None, :]
            a = t3(_mm_nn(f"mla_a_{l}", [(t2(u), _w2(w_a))], F32))
            cq, ckv, kpe = _mla_norm_fwd(f"mla_norm_fwd_{l}", a, qn, kvn, cos_t, sin_t)
            q_raw = t3(_mm_nn(f"mla_q_{l}", [(t2(cq), _w2(w_uq))], F32))
            k_raw = t3(_mm_nn(f"mla_k_{l}", [(t2(ckv), _w2(w_k))], F32))
            vh = t3(_mm_nn(f"mla_v_{l}", [(t2(ckv), _w2(w_v))], BF16))
            qh, kh = _mla_prep_fwd(f"mla_prep_fwd_{l}", q_raw, k_raw, kpe, cos_t, sin_t, n_heads)
            o, lse, fetched = _flash_fwd(f"flash_fwd_{l}", qh, kh, vh, n_heads, sm_scale,
                                         cargo=[shards[2][n] for n in members[2]] if l == host_layer else ())
            if l == host_layer:
                fetched_group(2, dict(zip(members[2], fetched)))
            y = t3(_mm_nn(f"mla_o_{l}", [(t2(o), _w2(w_o))], F32))
            st.update(u=u, w_a=w_a, w_uq=w_uq, w_k=w_k, w_v=w_v, w_o=w_o, qn=qn, kvn=kvn, a=a, cq=cq, ckv=ckv,
                      qh=qh, kh=kh, vh=vh, o=o, lse=lse)
        else:
            grp_l, jj = grp('sc_w_in', j)
            w_out, cw = grp_l['sc_w_out'][jj], full['sc_conv'][j]
            q = t3(_mm_nn(f"sc_in_{l}", [(t2(u), (grp_l['in_cols'], jj))], F32))
            r = _shortconv_fwd(f"shortconv_fwd_{l}", q, cw)
            y = t3(_mm_nn(f"sc_out_{l}", [(t2(r), _w2(w_out))], F32))
            st.update(u=u, w_out=w_out, cw=cw, q=q, r=r, in_rows=(grp_l['in_rows'], jj))
        lng, lnb = full['ln_g'][l], full['ln_b'][l]
        z1, xmid, u2 = _ln_mod_fwd(f"ln_mod_a_{l}", alpha, xin, y, g1, lng[0:1], lnb[0:1], sc2, sh2)
        cwf, cbf = full['ffn_conv'][l], ffn_conv_b[l][None, :]
        ffn_w, ll = grp('ffn_w_up', l)
        ffn_names = ('ffn_w_up', 'ffn_w_down')
        ride_mm = [n for n in members[1] if n not in ffn_names] if l == 0 else []
        ride_conv = [n for n in members[1] if n == 'ffn_w_up'] if l == 0 else []
        ride_down = [n for n in members[1] if n == 'ffn_w_down'] if l == 0 else []
        p = _mm_nn(f"ffn_up_{l}", [(t2(u2), (ffn_w['up_cols'], ll))], F32, cargo=[shards[1][n] for n in ride_mm],
                   route=_Fetch)
        got_mm = ()
        if ride_mm:
            p, got_mm = p
        p = t3(p)
        act, got_conv = _convglu_fwd(f"convglu_fwd_{l}", p, cwf, cbf, cargo=[shards[1][n] for n in ride_conv])
        y2 = _mm_nn(f"ffn_down_{l}", [(t2(act), (ffn_w['down_rows'], ll))], F32, cargo=[shards[1][n] for n in ride_down],
                    route=_Fetch, tm_cap=WIDE_K_ROWS)
        got_down = ()
        if ride_down:
            y2, got_down = y2
        y2 = t3(y2)
        if l == 0:
            fetched_group(1, {**dict(zip(ride_mm, got_mm)), **dict(zip(ride_conv, got_conv)),
                              **dict(zip(ride_down, got_down))})
        st.update(y1=y, z1=z1, xmid=xmid, u2=u2, p=p, act=act, y2=y2, cwf=cwf, cbf=cbf, lng=lng, lnb=lnb,
                  ffn_w=ffn_w, ll=ll)
        if l + 1 < depth:
            nsh1, nsc1 = mods[l + 1][0], mods[l + 1][1]
            z2, xin, u = _ln_mod_fwd(f"ln_mod_b_{l}", alpha, xmid, y2, g2, lng[1:2], lnb[1:2], nsc1, nsh1)
        else:
            z2, ct, loss_acc = _ln_loss_fwd("ln_loss", alpha, xmid, y2, g2, lng[1:2], lnb[1:2], loss_target)
        st.update(z2=z2)
        saved.append(st)
    loss = lax.psum(loss_acc[0, 0], ("x", "y", "c"))

    grads = {}
    dmods = [[None] * 6 for _ in range(depth)]
    g_ln_g = [[None, None] for _ in range(depth)]
    g_ln_b = [[None, None] for _ in range(depth)]
    stack = {n: [None] * wts[n].shape[0] for n in ('pool_scale', 'mla_q_norm', 'mla_kv_norm', 'sc_conv', 'ffn_conv',
                                                    'ffn_conv_b')}
    units = {n: [None] * wts[n].shape[0] for n in big_names}
    cargo_a, cargo_b, received = [], [], {}

    def landed(items, got):
        for (n, i, _), r in zip(items, got):
            received[(n, i)] = r

    upstream = (ct,)
    for l in reversed(range(depth)):
        st = saved[l]
        sh1, sc1, g1, sh2, sc2, g2 = mods[l]
        kind, j = l % 3, l // 3
        if len(upstream) > 1:
            upstream = (upstream[0], upstream[1], st['lnb'][1:2], upstream[2])
        res = _sub_bwd(f"sub_bwd_b_{l}", alpha, upstream, st['z2'], st['y2'], g2, st['lng'][1:2])
        dz2, dy2, dmods[l][5], g_ln_g[l][1], g_ln_b[l][1] = res[:5]
        if l + 1 < depth:
            dmods[l + 1][1], dmods[l + 1][0] = res[5], res[6]
        dy2f = t2(dy2)
        ffn_w, ll = st['ffn_w'], st['ll']
        da = t3(_mm_nn(f"ffn_down_bwd_{l}", [(dy2f, (ffn_w['down_t'], ll))], F32))
        units['ffn_w_down'][l] = _mm_tn(f"ffn_down_dw_{l}", t2(st['act']), [dy2f],
                                        out_dtype=BF16).reshape(N_CHIP, f_hid // N_CHIP, d)
        dpv, dpg, dcw, dcb, got = _convglu_bwd(f"convglu_bwd_{l}", st['p'], da, st['cwf'], st['cbf'],
                                               cargo=[u for _, _, u in cargo_a])
        landed(cargo_a, got)
        stack['ffn_conv'][l], stack['ffn_conv_b'][l] = dcw, dcb[0]
        down_unit = [('ffn_w_down', l, units['ffn_w_down'][l])]
        du2, got = _mm_nn(f"ffn_up_bwd_{l}", [(t2(dpv), (ffn_w['up_rows'], 2 * ll)),
                                              (t2(dpg), (ffn_w['up_rows'], 2 * ll + 1))], F32,
                          cargo=[units['ffn_w_down'][l]], tm_cap=WIDE_K_ROWS)
        landed(down_unit, got)
        du2 = t3(du2)
        res = _mm_tn(f"ffn_up_dw_{l}", t2(st['u2']), [t2(dpv), t2(dpg)], N_CHIP, out_dtype=BF16,
                     cargo=[u for _, _, u in cargo_b])
        if cargo_b:
            landed(cargo_b, res[1])
            res = res[0]
        units['ffn_w_up'][l] = res
        res = _sub_bwd(f"sub_bwd_a_{l}", alpha, (dz2, du2, st['lnb'][0:1], sc2), st['z1'], st['y1'], g1, st['lng'][0:1])
        dz1, dy1, dmods[l][2], g_ln_g[l][0], g_ln_b[l][0], dmods[l][4], dmods[l][3] = res
        dy1f = t2(dy1)
        if kind == 0:
            up_unit = [('ffn_w_up', l, units['ffn_w_up'][l])] if l == 0 else []
            du1, dw, dscale, got = _pool_bwd(f"pool_bwd_{l}", st['x'], sc1, sh1, dy1, st['w'], st['w_t'], st['scale'],
                                             cargo=[u for _, _, u in up_unit])
            landed(up_unit, got)
            stack['pool_scale'][j] = dscale[0]
            grp = dw.shape[1] // N_CHIP
            units['pool_w'][j] = jnp.transpose(dw.reshape(POOL_GROUPS, N_CHIP, grp, dw.shape[2]),
                                               (1, 0, 2, 3)).reshape(N_CHIP, POOL_GROUPS * grp, dw.shape[2])
        elif kind == 1:
            do = t3(_mm_nn(f"mla_o_bwd_{l}", [(dy1f, _w2(jnp.swapaxes(st['w_o'], 0, 1)))], BF16))
            gwo, _ = _unpad_heads(_mm_tn(f"mla_o_dw_{l}", t2(st['o']), [dy1f])[0], n_heads, V_HEAD, 0)
            units['mla_w_o'][j] = jnp.moveaxis(gwo.reshape(d, n_heads * V_HEAD), -1, 0).reshape(N_CHIP, -1, d)
            fa = (st['qh'], st['kh'], st['vh'], st['o'], st['lse'], do, n_heads, sm_scale)
            dq, dk, dv = _flash_bwd(f"flash_bwd_{l}", *fa)
            dq_raw, dk_raw, dkpe = _mla_prep_bwd(f"mla_prep_bwd_{l}", dq, dk, cos_t, sin_t, n_heads)
            dq_raw, dk_raw, dv_raw = t2(dq_raw), t2(dk_raw), t2(dv)
            dcq = t3(_mm_nn(f"mla_q_bwd_{l}", [(dq_raw, _w2(jnp.swapaxes(st['w_uq'], 0, 1)))], F32))
            dckv = t3(_mm_nn(f"mla_kv_bwd_{l}", [(dk_raw, _w2(jnp.swapaxes(st['w_k'], 0, 1))),
                                                   (dv_raw, _w2(jnp.swapaxes(st['w_v'], 0, 1)))], F32,
                            tm_cap=WIDE_K_ROWS))
            gq, _ = _unpad_heads(_mm_tn(f"mla_q_dw_{l}", t2(st['cq']), [dq_raw])[0], n_heads, QK_NOPE + QK_ROPE, 1)
            units['mla_w_uq'][j] = _cols_by_chip(gq.reshape(ql, n_heads * (QK_NOPE + QK_ROPE)))
            gkv = _mm_tn(f"mla_kv_dw_{l}", t2(st['ckv']), [dk_raw, dv_raw])[0]
            gk, _ = _unpad_heads(gkv[:, :n_heads * HEAD_PAD], n_heads, QK_NOPE, 1)
            gv, _ = _unpad_heads(gkv[:, n_heads * HEAD_PAD:], n_heads, V_HEAD, 1)
            units['mla_w_ukv'][j] = _cols_by_chip(
                jnp.concatenate([gk, gv], axis=-1).reshape(kvl, n_heads * (QK_NOPE + V_HEAD)))
            da_, dqn, dkvn = _mla_norm_bwd(f"mla_norm_bwd_{l}", st['a'], dcq, dckv, dkpe, st['qn'], st['kvn'])
            stack['mla_q_norm'][j], stack['mla_kv_norm'][j] = dqn[0], dkvn[0]
            du1 = t3(_mm_nn(f"mla_a_bwd_{l}", [(t2(da_), _w2(jnp.swapaxes(st['w_a'], 0, 1)))], F32))
            gwa = _mm_tn(f"mla_a_dw_{l}", t2(st['u']), [t2(da_)])[0]
            units['mla_w_a'][j] = _cols_by_chip(jnp.concatenate(
                [gwa[:, :ql + kvl], gwa[:, ql + kvl + QK_NOPE:ql + kvl + QK_NOPE + QK_ROPE]], axis=1))
        else:
            dr = t3(_mm_nn(f"sc_out_bwd_{l}", [(dy1f, _w2(jnp.swapaxes(st['w_out'], 0, 1)))], F32))
            units['sc_w_out'][j] = _mm_tn(f"sc_out_dw_{l}", t2(st['r']), [dy1f], out_dtype=BF16).reshape(N_CHIP, -1, d)
            dgb, dgc, dh, dcw = _shortconv_bwd(f"shortconv_bwd_{l}", st['q'], dr, st['cw'])
            stack['sc_conv'][j] = dcw
            parts = [t2(dgb), t2(dgc), t2(dh)]
            in_rows, jj = st['in_rows']
            du1 = t3(_mm_nn(f"sc_in_bwd_{l}", [(parts[k], (in_rows, 3 * jj + k)) for k in range(3)], F32,
                            tm_cap=WIDE_K_ROWS))
            units['sc_w_in'][j] = _cols_by_chip(jnp.concatenate(
                [_mm_tn(f"sc_in_dw_{k}_{l}", t2(st['u']), [parts[k]])[0] for k in range(3)], axis=1))
        upstream = (dz1, du1, sc1)
        mixer = {0: ['pool_w'], 1: ['mla_w_a', 'mla_w_uq', 'mla_w_ukv', 'mla_w_o'], 2: ['sc_w_in', 'sc_w_out']}[kind]
        for n in mixer:
            units[n][j] = units[n][j].astype(BF16)
        cargo_a = [('ffn_w_up', l, units['ffn_w_up'][l])] if l > 0 else []
        cargo_b = [(n, j, units[n][j]) for n in mixer]
    grad_x, dmods[0][1], dmods[0][0] = _input_bwd("input_bwd", alpha, upstream[0], upstream[1], x, mods[0][1])

    for n, parts in stack.items():
        grads[n] = jnp.stack(parts)
    grads['ln_g'] = jnp.stack([jnp.concatenate(r, axis=0) for r in g_ln_g])
    grads['ln_b'] = jnp.stack([jnp.concatenate(r, axis=0) for r in g_ln_b])
    dmod_mine = jnp.stack([jnp.concatenate([t[:, 0, :] for t in dmods[l]], axis=-1) for l in range(depth)])

    small_grad_names = small_names + ['mla_q_norm', 'mla_kv_norm', 'ffn_conv_b']
    sg_pack, sg_spans = _pack_rows([dmod_mine] + [grads[n] for n in small_grad_names], F32, SUBLANE)
    rows_sg = sg_pack.shape[0]
    sg_all = _all_gather8("gather_small_grads", sg_pack, True).reshape(N_DEV, rows_sg, PACK_COLS)
    dmod_all = sg_all.reshape(N_DEV, -1)[:, :dmod_mine.size].reshape(N_DEV, depth, bsz, 6 * d)
    dmod_all = jnp.transpose(dmod_all, (1, 0, 2, 3)).reshape(depth, N_DEV * bsz, 6 * d)
    sg_sum = _sum8("sum_small_grads", sg_all).reshape(-1)
    for n, (off, shape) in zip(small_grad_names, sg_spans[1:]):
        g_full = sg_sum[off:off + _size(shape)].reshape(shape)
        if n in SMALL_SHARDED:
            ax = SMALL_SHARDED[n]
            width = shape[ax] // N_CHIP
            g_full = lax.dynamic_slice_in_dim(g_full, chip * width, width, axis=ax)
        grads[n] = g_full
    dmod_cols = lax.dynamic_slice_in_dim(dmod_all, chip * n_mod, n_mod, axis=2)
    grads['mod_w'], gb = _mod_bwd("mod_bwd", c_all, dmod_cols, dmod_all)
    grads['mod_b'] = gb[:, 0, :]

    keys = [(n, i) for n in big_names for i in range(len(units[n]))]
    last = [(n, i, units[n][i]) for n, i in keys if (n, i) not in received]
    landed(last, _scatter_grads("scatter_big_grads", [u for _, _, u in last]))
    chip_core = jnp.stack([chip, mc]).astype(jnp.int32)
    bufs = _swap_halves("swap_big_grad_halves",
                        [_sum8_into_half(f"sum_big_grads_{n}_{i}", units[n][i], received[(n, i)], chip_core)
                         for n, i in keys])
    for n in big_names:
        grads[n] = jnp.stack([b for (m, _), b in zip(keys, bufs) if m == n]).reshape(wts[n].shape)

    deltas, new_m, new_v = {}, {}, {}
    for n in WEIGHTS:
        deltas[n], new_m[n], new_v[n] = _adamw(f"adamw_{n}", wts[n], grads[n], mom1[n], mom2[n])
    return (loss, grad_x, *[grads[n] for n in WEIGHTS], *[deltas[n] for n in WEIGHTS],
            *[new_m[n] for n in WEIGHTS], *[new_v[n] for n in WEIGHTS])
```

```python
import functools

import jax
import jax.numpy as jnp
from jax import lax
from jax.experimental import pallas as pl
from jax.experimental.pallas import tpu as pltpu

F32 = jnp.float32
BF16 = jnp.bfloat16
MESH = pl.DeviceIdType.MESH

N_DEV = 8
N_CHIP = 4
LANE = 128
SUBLANE = 8
VMEM_LIMIT_BYTES = 56 * 2 ** 20
PACK_COLS = 1024

LN_EPS = 1e-5
RMS_EPS = 1e-6
QK_NOPE, QK_ROPE, V_HEAD = 64, 32, 64
ROPE_THETA = 10000.0
HEAD_PAD = 128
POOL_GROUPS = 4
POOL_HALO = 16
CONV_HALO = 8
CONV_ROWS = 1024
WIDE_K_ROWS = 1024
ADAM_LR, ADAM_B1, ADAM_B2, ADAM_EPS, ADAM_WD, ADAM_STEP = 0.001, 0.9, 0.999, 1e-08, 0.01, 10

WEIGHTS = ['mod_w', 'mod_b', 'ln_g', 'ln_b', 'pool_w', 'pool_scale', 'mla_w_a', 'mla_q_norm', 'mla_w_uq',
           'mla_kv_norm', 'mla_w_ukv', 'mla_w_o', 'sc_w_in', 'sc_conv', 'sc_w_out', 'ffn_w_up', 'ffn_conv',
           'ffn_conv_b', 'ffn_w_down']
BIG = {'pool_w': 2, 'mla_w_a': 2, 'mla_w_uq': 2, 'mla_w_ukv': 2, 'mla_w_o': 1, 'sc_w_in': 2, 'sc_w_out': 1,
       'ffn_w_up': 2, 'ffn_w_down': 1}
SMALL_SHARDED = {'ln_g': 2, 'ln_b': 2, 'pool_scale': 1, 'sc_conv': 2, 'ffn_conv': 2}
REPLICATED = ['mod_b', 'mla_q_norm', 'mla_kv_norm', 'ffn_conv_b']


def _pc(body, **kw):
    return pl.pallas_call(body, **kw)


def _cp(*sem):
    return pltpu.CompilerParams(dimension_semantics=sem, vmem_limit_bytes=VMEM_LIMIT_BYTES)


def _div(n, cap, mult):
    best = None
    for d in range(mult, min(n, cap) + 1, mult):
        if n % d == 0:
            best = d
    return best if best is not None else n


def _sds(shape, dtype):
    return jax.ShapeDtypeStruct(tuple(shape), dtype)


def _flip(v, bit):
    return 1 - v if bit else v


def _all_gather8(name, x_shard, in_vmem):
    m_per, n = x_shard.shape
    space = pltpu.VMEM if in_vmem else pltpu.HBM

    def body(x_ref, out_ref, send_sems, recv_sems, local_sem):
        x, y, c = lax.axis_index("x"), lax.axis_index("y"), lax.axis_index("c")
        me, sibling = (x, y, c), (x, y, 1 - c)
        chips = [(1 - x, y), (x, 1 - y), (1 - x, 1 - y)]

        def rows(px, py, pc_):
            return out_ref.at[pl.ds((4 * px + 2 * py + pc_) * m_per, m_per), :]

        def copy(k, block, to, src=None):
            return pltpu.make_async_remote_copy(
                src_ref=rows(*block) if src is None else src, dst_ref=rows(*block),
                send_sem=send_sems.at[k], recv_sem=recv_sems.at[k], device_id=to, device_id_type=MESH)

        mine = pltpu.make_async_copy(x_ref, rows(*me), local_sem)
        mine.start()
        first = [copy(0, me, sibling, src=x_ref)]
        first += [copy(1 + j, me, (*chip, c), src=x_ref) for j, chip in enumerate(chips)]
        for cp in first:
            cp.start()
        passed = [copy(4 + j, (*chip, c), sibling) for j, chip in enumerate(chips)]
        for j, chip in enumerate(chips):
            copy(1 + j, (*chip, c), me).wait_recv()
            passed[j].start()
        copy(0, sibling, me).wait_recv()
        for j, chip in enumerate(chips):
            copy(4 + j, (*chip, 1 - c), me).wait_recv()
        for cp in first + passed:
            cp.wait_send()
        mine.wait()

    return _pc(
        body, name=name, out_shape=_sds((N_DEV * m_per, n), x_shard.dtype),
        in_specs=[pl.BlockSpec(memory_space=space)], out_specs=pl.BlockSpec(memory_space=space),
        scratch_shapes=[pltpu.SemaphoreType.DMA((7,)), pltpu.SemaphoreType.DMA((7,)), pltpu.SemaphoreType.DMA],
        compiler_params=pltpu.CompilerParams(vmem_limit_bytes=VMEM_LIMIT_BYTES),
    )(x_shard)


def _gather_weights(name, shards):
    n_t = len(shards)
    halves = [s.shape[0] // 2 for s in shards]

    def body(*refs):
        x_refs, o_refs = refs[:n_t], refs[n_t:2 * n_t]
        send_sems, recv_sems, local_sems = refs[2 * n_t:]
        x, y, c = lax.axis_index("x"), lax.axis_index("y"), lax.axis_index("c")
        me, sibling = (x, y, c), (x, y, 1 - c)
        chips = [(1 - x, y), (x, 1 - y), (1 - x, 1 - y)]

        def slot(t, px, py, pc_):
            return o_refs[t].at[4 * px + 2 * py + pc_]

        def my_rows(t):
            return x_refs[t].at[pl.ds(c * halves[t], halves[t]), :]

        def copy(t, k, block, to, src=None):
            return pltpu.make_async_remote_copy(
                src_ref=slot(t, *block) if src is None else src, dst_ref=slot(t, *block),
                send_sem=send_sems.at[t, k], recv_sem=recv_sems.at[t, k], device_id=to, device_id_type=MESH)

        local = [pltpu.make_async_copy(my_rows(t), slot(t, *me), local_sems.at[t]) for t in range(n_t)]
        for cp in local:
            cp.start()
        first = []
        for t in range(n_t):
            first += [copy(t, 1 + j, me, (*chip, c), src=my_rows(t)) for j, chip in enumerate(chips)]
            first.append(copy(t, 0, me, sibling, src=my_rows(t)))
        for cp in first:
            cp.start()
        passed = []
        for j, chip in enumerate(chips):
            for t in range(n_t):
                copy(t, 1 + j, (*chip, c), me).wait_recv()
                passed.append(copy(t, 4 + j, (*chip, c), sibling))
                passed[-1].start()
        for t in range(n_t):
            copy(t, 0, sibling, me).wait_recv()
        for j, chip in enumerate(chips):
            for t in range(n_t):
                copy(t, 4 + j, (*chip, 1 - c), me).wait_recv()
        for cp in first + passed:
            cp.wait_send()
        for cp in local:
            cp.wait()

    hbm = pl.BlockSpec(memory_space=pltpu.HBM)
    return _pc(
        body, name=name, out_shape=tuple(_sds((N_DEV, h, s.shape[1]), s.dtype) for h, s in zip(halves, shards)),
        in_specs=[hbm] * n_t, out_specs=(hbm,) * n_t,
        scratch_shapes=[pltpu.SemaphoreType.DMA((n_t, 7)), pltpu.SemaphoreType.DMA((n_t, 7)),
                        pltpu.SemaphoreType.DMA((n_t,))],
    )(*shards)


def _scatter_copies(u_refs, r_refs, send_sems, recv_sems):
    x, y, c = lax.axis_index("x"), lax.axis_index("y"), lax.axis_index("c")
    copies = []
    for k in range(1, N_DEV):
        px, py, pcc = _flip(x, (k >> 2) & 1), _flip(y, (k >> 1) & 1), _flip(c, k & 1)
        for t, (u_ref, r_ref) in enumerate(zip(u_refs, r_refs)):
            h = u_ref.shape[1] // 2
            copies.append(pltpu.make_async_remote_copy(
                src_ref=u_ref.at[2 * px + py, pl.ds(pcc * h, h), :], dst_ref=r_ref.at[k - 1],
                send_sem=send_sems.at[t, k - 1], recv_sem=recv_sems.at[t, k - 1],
                device_id=(px, py, pcc), device_id_type=MESH))
    return copies


def _scatter_shapes(units):
    return tuple(_sds((N_DEV - 1, u.shape[1] // 2, u.shape[2]), u.dtype) for u in units)


def _scatter_grads(name, units):
    n_u = len(units)

    def body(*refs):
        copies = _scatter_copies(refs[:n_u], refs[n_u:2 * n_u], refs[2 * n_u], refs[2 * n_u + 1])
        for cp in copies:
            cp.start()
        for cp in copies:
            cp.wait()

    hbm = pl.BlockSpec(memory_space=pltpu.HBM)
    return _pc(body, name=name, out_shape=_scatter_shapes(units), in_specs=[hbm] * n_u, out_specs=(hbm,) * n_u,
               scratch_shapes=[pltpu.SemaphoreType.DMA((n_u, 7)), pltpu.SemaphoreType.DMA((n_u, 7))])(*units)


class _Scatter:
    peers = N_DEV - 1
    shapes = staticmethod(_scatter_shapes)

    @staticmethod
    def copies(u_refs, r_refs, send_sems, recv_sems):
        both = _scatter_copies(u_refs, r_refs, send_sems, recv_sems)
        return both, both


class _Fetch:
    peers = N_CHIP - 1

    @staticmethod
    def shapes(units):
        return tuple(_sds((N_CHIP,) + u.shape, u.dtype) for u in units)

    @staticmethod
    def copies(u_refs, r_refs, send_sems, recv_sems):
        x, y, c = lax.axis_index("x"), lax.axis_index("y"), lax.axis_index("c")
        sends, recvs = [], []
        for k in range(1, N_CHIP):
            px, py = _flip(x, (k >> 1) & 1), _flip(y, k & 1)
            for t, (u_ref, r_ref) in enumerate(zip(u_refs, r_refs)):
                sends.append(pltpu.make_async_remote_copy(
                    src_ref=u_ref, dst_ref=r_ref.at[2 * x + y], send_sem=send_sems.at[t, k - 1],
                    recv_sem=recv_sems.at[t, k - 1], device_id=(px, py, c), device_id_type=MESH))
                recvs.append(pltpu.make_async_remote_copy(
                    src_ref=u_ref, dst_ref=r_ref.at[2 * px + py], send_sem=send_sems.at[t, k - 1],
                    recv_sem=recv_sems.at[t, k - 1], device_id=(px, py, c), device_id_type=MESH))
        return sends, recvs


def _pc_cargo(body, cargo, *, name, grid, in_specs, out_specs, out_shape, scratch_shapes=(), route=_Scatter):
    out_specs, out_shape = tuple(out_specs), tuple(out_shape)
    if not cargo:
        return lambda *args: (_pc(body, name=name, grid=grid, in_specs=list(in_specs), out_specs=out_specs,
                                  out_shape=out_shape, scratch_shapes=list(scratch_shapes),
                                  compiler_params=_cp(*["arbitrary"] * len(grid)))(*args), ())
    n_in, n_out, n_u, n_s = len(in_specs), len(out_specs), len(cargo), len(scratch_shapes)

    def wrapped(*refs):
        ins, u_refs = refs[:n_in], refs[n_in:n_in + n_u]
        outs = refs[n_in + n_u:n_in + n_u + n_out]
        r_refs = refs[n_in + n_u + n_out:n_in + 2 * n_u + n_out]
        scratch = refs[n_in + 2 * n_u + n_out:n_in + 2 * n_u + n_out + n_s]
        send_sems, recv_sems = refs[-2:]
        first = last = None
        for axis, extent in enumerate(grid):
            at_start, at_end = pl.program_id(axis) == 0, pl.program_id(axis) == extent - 1
            first = at_start if first is None else first & at_start
            last = at_end if last is None else last & at_end

        @pl.when(first)
        def _():
            sends, _ = route.copies(u_refs, r_refs, send_sems, recv_sems)
            for cp in sends:
                cp.start()

        body(*ins, *outs, *scratch)

        @pl.when(last)
        def _():
            sends, recvs = route.copies(u_refs, r_refs, send_sems, recv_sems)
            for cp in recvs:
                cp.wait_recv()
            for cp in sends:
                cp.wait_send()

    hbm = pl.BlockSpec(memory_space=pltpu.HBM)
    sems = pltpu.SemaphoreType.DMA((n_u, route.peers))
    call = _pc(wrapped, name=name, grid=grid, in_specs=list(in_specs) + [hbm] * n_u, out_specs=out_specs + (hbm,) * n_u,
               out_shape=out_shape + route.shapes(cargo), scratch_shapes=list(scratch_shapes) + [sems, sems],
               compiler_params=_cp(*["arbitrary"] * len(grid)))

    def run(*args):
        res = call(*args, *cargo)
        return tuple(res[:n_out]), tuple(res[n_out:])
    return run


def _swap_halves(name, bufs):
    n_u = len(bufs)

    def body(*refs):
        o_refs = refs[n_u:2 * n_u]
        send_sems, recv_sems = refs[2 * n_u:]
        x, y, c = lax.axis_index("x"), lax.axis_index("y"), lax.axis_index("c")

        def rows(u, core):
            h = bufs[u].shape[0] // 2
            return o_refs[u].at[pl.ds(core * h, h), :]

        sends = [pltpu.make_async_remote_copy(src_ref=rows(u, c), dst_ref=rows(u, c), send_sem=send_sems.at[u],
                                              recv_sem=recv_sems.at[u], device_id=(x, y, 1 - c), device_id_type=MESH)
                 for u in range(n_u)]
        recvs = [pltpu.make_async_remote_copy(src_ref=rows(u, c), dst_ref=rows(u, 1 - c), send_sem=send_sems.at[u],
                                              recv_sem=recv_sems.at[u], device_id=(x, y, 1 - c), device_id_type=MESH)
                 for u in range(n_u)]
        for cp in sends:
            cp.start()
        for cp in recvs:
            cp.wait_recv()
        for cp in sends:
            cp.wait_send()

    hbm = pl.BlockSpec(memory_space=pltpu.HBM)
    return _pc(
        body, name=name, out_shape=tuple(_sds(b.shape, b.dtype) for b in bufs), in_specs=[hbm] * n_u,
        out_specs=(hbm,) * n_u, input_output_aliases={u: u for u in range(n_u)},
        scratch_shapes=[pltpu.SemaphoreType.DMA((n_u,)), pltpu.SemaphoreType.DMA((n_u,))],
    )(*bufs)


def _sum8_into_half(name, unit, received, chip_core):
    _, h, n = received.shape
    tm = _div(h, 256, 16)
    per = h // tm

    def body(cc_ref, u_ref, p_ref, o_ref):
        acc = u_ref[0].astype(F32)
        for s in range(N_DEV - 1):
            acc = acc + p_ref[s].astype(F32)
        o_ref[...] = acc

    grid_spec = pltpu.PrefetchScalarGridSpec(
        num_scalar_prefetch=1, grid=(per,),
        in_specs=[pl.BlockSpec((1, tm, n), lambda i, cc_ref: (cc_ref[0], cc_ref[1] * per + i, 0)),
                  pl.BlockSpec((N_DEV - 1, tm, n), lambda i, cc_ref: (0, i, 0))],
        out_specs=pl.BlockSpec((tm, n), lambda i, cc_ref: (cc_ref[1] * per + i, 0)))
    return _pc(body, name=name, grid_spec=grid_spec, out_shape=_sds((2 * h, n), F32),
               compiler_params=_cp("arbitrary"))(chip_core, unit, received)


def _sum8(name, parts):
    _, m, n = parts.shape
    tm = _div(m, 256, SUBLANE)

    def body(p_ref, o_ref):
        acc = p_ref[0]
        for s in range(1, N_DEV):
            acc = acc + p_ref[s]
        o_ref[...] = acc

    return _pc(body, name=name, grid=(m // tm,), out_shape=_sds((m, n), F32),
               in_specs=[pl.BlockSpec((N_DEV, tm, n), lambda i: (0, i, 0))],
               out_specs=pl.BlockSpec((tm, n), lambda i: (i, 0)), compiler_params=_cp("parallel"))(parts)


def _pack_rows(arrays, dtype, row_mult):
    flat, spans, off = [], [], 0
    for a in arrays:
        flat.append(a.reshape(-1).astype(dtype))
        spans.append((off, a.shape))
        off += a.size
    quantum = row_mult * PACK_COLS
    total = -(-off // quantum) * quantum
    if total > off:
        flat.append(jnp.zeros((total - off,), dtype))
    return jnp.concatenate(flat).reshape(total // PACK_COLS, PACK_COLS), spans


def _size(shape):
    n = 1
    for s in shape:
        n *= s
    return n


def _join_chips(blocks, axis):
    return jnp.concatenate([blocks[j] for j in range(N_CHIP)], axis=axis)


def _cols_by_chip(g):
    k, n = g.shape
    return jnp.transpose(g.reshape(k, N_CHIP, n // N_CHIP), (1, 0, 2))


def _mm_nn(name, pairs, out_dtype, cargo=(), route=_Scatter, tm_cap=2048, tn_cap=1536):
    m = pairs[0][0].shape[0]
    nb, _, n4 = pairs[0][1][0].shape
    tm, tn = _div(m, tm_cap, 16), _div(n4, tn_cap, LANE)
    per = n4 // tn
    n_pairs = len(pairs)

    def body(*refs):
        o_ref = refs[-1]
        acc = jnp.dot(refs[0][...], refs[1][0], preferred_element_type=F32)
        for i in range(1, n_pairs):
            acc = acc + jnp.dot(refs[2 * i][...], refs[2 * i + 1][0], preferred_element_type=F32)
        o_ref[...] = acc.astype(o_ref.dtype)

    in_specs, args = [], []
    for a, (w, r) in pairs:
        k = a.shape[1]
        assert w.shape[0] == nb and w.shape[2] == n4 and w.shape[1] % k == 0
        in_specs += [pl.BlockSpec((tm, k), lambda j, i: (i, 0)),
                     pl.BlockSpec((1, k, tn), functools.partial(lambda j, i, r_: (j // per, r_, j % per), r_=r))]
        args += [a, w]
    if cargo:
        (out,), received = _pc_cargo(body, cargo, name=name, grid=(nb * per, m // tm), in_specs=in_specs, route=route,
                                     out_shape=[_sds((m, nb * n4), out_dtype)],
                                     out_specs=[pl.BlockSpec((tm, tn), lambda j, i: (i, j))])(*args)
        return out, received
    return _pc(body, name=name, grid=(nb * per, m // tm), out_shape=_sds((m, nb * n4), out_dtype), in_specs=in_specs,
               out_specs=pl.BlockSpec((tm, tn), lambda j, i: (i, j)), compiler_params=_cp("parallel", "parallel"))(*args)


def _mm_tn(name, x, ys, n_blocks=1, out_dtype=F32, cargo=(), tt_cap=1024):
    t, k = x.shape
    widths = [y.shape[1] for y in ys]
    n4 = sum(widths) // n_blocks
    common = n4
    for w in widths:
        common = _gcd(common, w)
    tk, tn, tt = _div(k, 1536, LANE), _div(common, 1536, LANE), _div(t, tt_cap, 16)
    per = n4 // tn
    starts, acc_w = [], 0
    for w in widths:
        starts.append(acc_w // tn)
        acc_w += w
    counts = [w // tn for w in widths]
    n_y = len(ys)

    def active(i, j):
        return (j >= starts[i]) & (j < starts[i] + counts[i])

    n_t = t // tt

    def body(*refs):
        x_ref, y_refs, o_ref, acc_ref = refs[0], refs[1:1 + n_y], refs[-2], refs[-1]
        j = pl.program_id(1)

        @pl.when(pl.program_id(2) == 0)
        def _():
            acc_ref[...] = jnp.zeros_like(acc_ref)

        for i in range(n_y):
            @pl.when(active(i, j))
            def _():
                acc_ref[...] += lax.dot_general(x_ref[...], y_refs[i][...], (((0,), (0,)), ((), ())),
                                                preferred_element_type=F32)

        @pl.when(pl.program_id(2) == n_t - 1)
        def _():
            o_ref[0] = acc_ref[...].astype(o_ref.dtype)

    def y_spec(i):
        def index(a, j, s):
            on = active(i, j)
            return jnp.where(on, s, 0), jnp.where(on, j - starts[i], 0)
        return pl.BlockSpec((tt, tn), index)

    (out,), received = _pc_cargo(
        body, cargo, name=name, grid=(k // tk, n_blocks * per, n_t), out_shape=[_sds((n_blocks, k, n4), out_dtype)],
        in_specs=[pl.BlockSpec((tt, tk), lambda a, j, s: (s, a))] + [y_spec(i) for i in range(n_y)],
        out_specs=[pl.BlockSpec((1, tk, tn), lambda a, j, s: (j // per, a, j % per))],
        scratch_shapes=[pltpu.VMEM((tk, tn), F32)])(x, *ys)
    return (out, received) if cargo else out


def _gcd(a, b):
    while b:
        a, b = b, a % b
    return a


def _w2(w):
    return (w[None], 0)


def _tok_spec(ts, d):
    return pl.BlockSpec((1, ts, d), lambda b, i: (b, i, 0))


def _seq_spec(d):
    return pl.BlockSpec((1, 1, d), lambda b, i: (b, 0, 0))


def _vec_spec(d):
    return pl.BlockSpec((1, d), lambda b, i: (0, 0))


def _ln_stats(z):
    mu = jnp.mean(z, axis=-1, keepdims=True)
    zc = z - mu
    var = jnp.mean(zc * zc, axis=-1, keepdims=True)
    rstd = lax.rsqrt(var + LN_EPS)
    return zc * rstd, rstd


def _modulate(name, x, sc, sh):
    b, s, d = x.shape
    ts = _div(s, 512, 16)

    def body(x_ref, sc_ref, sh_ref, u_ref):
        u_ref[0] = (x_ref[0] * (1.0 + sc_ref[0]) + sh_ref[0]).astype(BF16)

    return _pc(body, name=name, grid=(b, s // ts), out_shape=_sds(x.shape, BF16),
               in_specs=[_tok_spec(ts, d), _seq_spec(d), _seq_spec(d)], out_specs=_tok_spec(ts, d),
               compiler_params=_cp("parallel", "parallel"))(x, sc, sh)


def _ln_mod_fwd(name, alpha, x, y, g, lng, lnb, sc, sh):
    b, s, d = x.shape
    ts = _div(s, 512, 16)

    def body(x_ref, y_ref, g_ref, lng_ref, lnb_ref, sc_ref, sh_ref, z_ref, xn_ref, u_ref):
        z = alpha * x_ref[0] + (1.0 + g_ref[0]) * y_ref[0]
        xhat, _ = _ln_stats(z)
        xn = xhat * lng_ref[...] + lnb_ref[...]
        z_ref[0] = z
        xn_ref[0] = xn
        u_ref[0] = (xn * (1.0 + sc_ref[0]) + sh_ref[0]).astype(BF16)

    tok, seq, vec = _tok_spec(ts, d), _seq_spec(d), _vec_spec(d)
    return _pc(body, name=name, grid=(b, s // ts),
               out_shape=(_sds(x.shape, F32), _sds(x.shape, F32), _sds(x.shape, BF16)),
               in_specs=[tok, tok, seq, vec, vec, seq, seq], out_specs=(tok, tok, tok),
               compiler_params=_cp("parallel", "parallel"))(x, y, g, lng, lnb, sc, sh)


def _ln_loss_fwd(name, alpha, x, y, g, lng, lnb, target):
    b, s, d = x.shape
    ts = _div(s, 512, 16)

    def body(x_ref, y_ref, g_ref, lng_ref, lnb_ref, t_ref, z_ref, ct_ref, loss_ref):
        @pl.when((pl.program_id(0) == 0) & (pl.program_id(1) == 0))
        def _():
            loss_ref[...] = jnp.zeros_like(loss_ref)
        z = alpha * x_ref[0] + (1.0 + g_ref[0]) * y_ref[0]
        xhat, _ = _ln_stats(z)
        err = xhat * lng_ref[...] + lnb_ref[...] - t_ref[0]
        z_ref[0] = z
        ct_ref[0] = err / d
        part = 0.5 * jnp.sum(jnp.mean(err * err, axis=-1, keepdims=True))
        loss_ref[...] += jnp.full(loss_ref.shape, part, F32)

    tok, seq, vec = _tok_spec(ts, d), _seq_spec(d), _vec_spec(d)
    return _pc(body, name=name, grid=(b, s // ts),
               out_shape=(_sds(x.shape, F32), _sds(x.shape, F32), _sds((SUBLANE, LANE), F32)),
               in_specs=[tok, tok, seq, vec, vec, tok],
               out_specs=(tok, tok, pl.BlockSpec((SUBLANE, LANE), lambda b, i: (0, 0))),
               compiler_params=_cp("arbitrary", "arbitrary"))(x, y, g, lng, lnb, target)


def _sub_bwd(name, alpha, upstream, z, y, g, lng):
    b, s, d = z.shape
    ts = _div(s, 512, 16)
    last = len(upstream) == 1

    def body(*refs):
        if last:
            ct_ref, z_ref, y_ref, g_ref, lng_ref, dz_ref, dy_ref, dg_ref, dlng_ref, dlnb_ref = refs
        else:
            (dzn_ref, dun_ref, lnb_ref, scn_ref, z_ref, y_ref, g_ref, lng_ref,
             dz_ref, dy_ref, dg_ref, dlng_ref, dlnb_ref, dsc_ref, dsh_ref) = refs
        first_tile = pl.program_id(1) == 0

        @pl.when(first_tile & (pl.program_id(0) == 0))
        def _():
            dlng_ref[...] = jnp.zeros_like(dlng_ref)
            dlnb_ref[...] = jnp.zeros_like(dlnb_ref)

        @pl.when(first_tile)
        def _():
            dg_ref[...] = jnp.zeros_like(dg_ref)
            if not last:
                dsc_ref[...] = jnp.zeros_like(dsc_ref)
                dsh_ref[...] = jnp.zeros_like(dsh_ref)

        xhat, rstd = _ln_stats(z_ref[0])
        if last:
            ct = ct_ref[0]
        else:
            dun = dun_ref[0]
            ct = alpha * dzn_ref[0] + dun * (1.0 + scn_ref[0])
            xn = xhat * lng_ref[...] + lnb_ref[...]
            dsc_ref[0] += jnp.sum(dun * xn, axis=0, keepdims=True)
            dsh_ref[0] += jnp.sum(dun, axis=0, keepdims=True)
        dlng_ref[...] += jnp.sum(ct * xhat, axis=0, keepdims=True)
        dlnb_ref[...] += jnp.sum(ct, axis=0, keepdims=True)
        dxhat = ct * lng_ref[...]
        dz = rstd * (dxhat - jnp.mean(dxhat, axis=-1, keepdims=True)
                     - xhat * jnp.mean(dxhat * xhat, axis=-1, keepdims=True))
        dz_ref[0] = dz
        dy_ref[0] = ((1.0 + g_ref[0]) * dz).astype(BF16)
        dg_ref[0] += jnp.sum(dz * y_ref[0], axis=0, keepdims=True)

    tok, seq, vec = _tok_spec(ts, d), _seq_spec(d), _vec_spec(d)
    seq_out = _sds((b, 1, d), F32)
    out_shape = [_sds(z.shape, F32), _sds(z.shape, BF16), seq_out, _sds((1, d), F32), _sds((1, d), F32)]
    out_specs = [tok, tok, seq, vec, vec]
    if last:
        in_specs = [tok, tok, tok, seq, vec]
    else:
        in_specs = [tok, tok, vec, seq, tok, tok, seq, vec]
        out_shape += [seq_out, seq_out]
        out_specs += [seq, seq]
    return _pc(body, name=name, grid=(b, s // ts), out_shape=tuple(out_shape), in_specs=in_specs,
               out_specs=tuple(out_specs), compiler_params=_cp("arbitrary", "arbitrary"))(*upstream, z, y, g, lng)


def _input_bwd(name, alpha, dz, du, x, sc):
    b, s, d = x.shape
    ts = _div(s, 512, 16)

    def body(dz_ref, du_ref, x_ref, sc_ref, gx_ref, dsc_ref, dsh_ref):
        @pl.when(pl.program_id(1) == 0)
        def _():
            dsc_ref[...] = jnp.zeros_like(dsc_ref)
            dsh_ref[...] = jnp.zeros_like(dsh_ref)
        du_ = du_ref[0]
        gx_ref[0] = alpha * dz_ref[0] + du_ * (1.0 + sc_ref[0])
        dsc_ref[0] += jnp.sum(du_ * x_ref[0], axis=0, keepdims=True)
        dsh_ref[0] += jnp.sum(du_, axis=0, keepdims=True)

    tok, seq = _tok_spec(ts, d), _seq_spec(d)
    seq_out = _sds((b, 1, d), F32)
    return _pc(body, name=name, grid=(b, s // ts), out_shape=(_sds(x.shape, F32), seq_out, seq_out),
               in_specs=[tok, tok, tok, seq], out_specs=(tok, seq, seq),
               compiler_params=_cp("parallel", "arbitrary"))(dz, du, x, sc)


def _rows_iota(shape):
    return lax.broadcasted_iota(jnp.int32, shape, 0)


def _back(v, k):
    return pltpu.roll(v, k, axis=0)


def _ahead(v, k):
    return pltpu.roll(v, v.shape[0] - k, axis=0)


def _conv3(ext, w_ref):
    return w_ref[2:3, :] * ext + w_ref[1:2, :] * _back(ext, 1) + w_ref[0:1, :] * _back(ext, 2)


def _conv3_t(dh_ext, w_ref):
    return w_ref[2:3, :] * dh_ext + w_ref[1:2, :] * _ahead(dh_ext, 1) + w_ref[0:1, :] * _ahead(dh_ext, 2)


def _flag(cond):
    return jnp.where(cond, 1.0, 0.0).astype(F32)


def _sigmoid(v):
    return 1.0 / (1.0 + jnp.exp(-v))


def _halo_specs(ts, tc, halo, n_s, col):
    per = ts // halo
    tile = pl.BlockSpec((1, ts, tc), lambda b, i, j: (b, i, col(j)))
    prev = pl.BlockSpec((1, halo, tc), lambda b, i, j: (b, jnp.maximum(i * per - 1, 0), col(j)))
    nxt = pl.BlockSpec((1, halo, tc), lambda b, i, j: (b, jnp.minimum((i + 1) * per, n_s * per - 1), col(j)))
    return tile, prev, nxt


def _convglu_fwd(name, p, cw, cb, cargo=()):
    b, s, f2 = p.shape
    f = f2 // 2
    ts, tc = _div(s, CONV_ROWS, CONV_HALO), _div(f, 256, LANE)
    n_s, n_c = s // ts, f // tc

    def body(pv_ref, pvh_ref, pg_ref, pgh_ref, wv_ref, wg_ref, bv_ref, bg_ref, a_ref):
        keep = _flag(pl.program_id(1) > 0)

        def conv(t_ref, h_ref, w_ref, b_ref):
            ext = jnp.concatenate([h_ref[0] * keep, t_ref[0]], axis=0)
            return _conv3(ext, w_ref)[CONV_HALO:] + b_ref[...]

        val = conv(pv_ref, pvh_ref, wv_ref, bv_ref)
        gate = conv(pg_ref, pgh_ref, wg_ref, bg_ref)
        a_ref[0] = (gate * _sigmoid(gate) * val).astype(BF16)

    tv, hv, _ = _halo_specs(ts, tc, CONV_HALO, n_s, lambda j: j)
    tg, hg, _ = _halo_specs(ts, tc, CONV_HALO, n_s, lambda j: j + n_c)
    wv = pl.BlockSpec((3, tc), lambda b, i, j: (0, j))
    wg = pl.BlockSpec((3, tc), lambda b, i, j: (0, j + n_c))
    bv = pl.BlockSpec((1, tc), lambda b, i, j: (0, j))
    bg = pl.BlockSpec((1, tc), lambda b, i, j: (0, j + n_c))
    (act,), fetched = _pc_cargo(
        body, cargo, name=name, grid=(b, n_s, n_c), route=_Fetch, out_shape=[_sds((b, s, f), BF16)],
        in_specs=[tv, hv, tg, hg, wv, wg, bv, bg],
        out_specs=[pl.BlockSpec((1, ts, tc), lambda b, i, j: (b, i, j))])(p, p, p, p, cw, cw, cb, cb)
    return act, fetched


def _convglu_bwd(name, p, da, cw, cb, cargo=()):
    b, s, f2 = p.shape
    f = f2 // 2
    ts, tc = _div(s, CONV_ROWS, CONV_HALO), _div(f, 256, LANE)
    n_s, n_c = s // ts, f // tc

    def body(pv_ref, pvp_ref, pvn_ref, pg_ref, pgp_ref, pgn_ref, da_ref, dan_ref, wv_ref, wg_ref, bv_ref, bg_ref,
             dpv_ref, dpg_ref, dwv_ref, dwg_ref, dbv_ref, dbg_ref):
        bi, i = pl.program_id(1), pl.program_id(2)

        @pl.when((bi == 0) & (i == 0))
        def _():
            for r in (dwv_ref, dwg_ref, dbv_ref, dbg_ref):
                r[...] = jnp.zeros_like(r)

        keep_prev = _flag(i > 0)
        keep_next = _flag(i < n_s - 1)
        pv_ext = jnp.concatenate([pvp_ref[0] * keep_prev, pv_ref[0], pvn_ref[0]], axis=0)
        pg_ext = jnp.concatenate([pgp_ref[0] * keep_prev, pg_ref[0], pgn_ref[0]], axis=0)
        taps_v = (_back(pv_ext, 2), _back(pv_ext, 1), pv_ext)
        taps_g = (_back(pg_ext, 2), _back(pg_ext, 1), pg_ext)

        def conv(taps, w_ref, b_ref):
            return (w_ref[2:3, :] * taps[2] + w_ref[1:2, :] * taps[1] + w_ref[0:1, :] * taps[0])[CONV_HALO:] + b_ref[...]

        val, gate = conv(taps_v, wv_ref, bv_ref), conv(taps_g, wg_ref, bg_ref)
        da_ext = jnp.concatenate([da_ref[0], dan_ref[0] * keep_next], axis=0)
        sg = _sigmoid(gate)
        dval = da_ext * gate * sg
        dgate = da_ext * val * (sg * (1.0 + gate * (1.0 - sg)))
        dpv_ref[0] = _conv3_t(dval, wv_ref)[:ts].astype(BF16)
        dpg_ref[0] = _conv3_t(dgate, wg_ref)[:ts].astype(BF16)
        for dh, taps, dw_ref, db_ref in ((dval[:ts], taps_v, dwv_ref, dbv_ref), (dgate[:ts], taps_g, dwg_ref, dbg_ref)):
            db_ref[...] += jnp.sum(dh, axis=0, keepdims=True)
            for k in range(3):
                dw_ref[k:k + 1, :] += jnp.sum(dh * taps[k][CONV_HALO:CONV_HALO + ts], axis=0, keepdims=True)

    def specs(col):
        per = ts // CONV_HALO
        tile = pl.BlockSpec((1, ts, tc), lambda j, b, i: (b, i, col(j)))
        prev = pl.BlockSpec((1, CONV_HALO, tc), lambda j, b, i: (b, jnp.maximum(i * per - 1, 0), col(j)))
        nxt = pl.BlockSpec((1, CONV_HALO, tc), lambda j, b, i: (b, jnp.minimum((i + 1) * per, n_s * per - 1), col(j)))
        return tile, prev, nxt

    tv, pvp, pvn = specs(lambda j: j)
    tg, pgp, pgn = specs(lambda j: j + n_c)
    wv = pl.BlockSpec((3, tc), lambda j, b, i: (0, j))
    wg = pl.BlockSpec((3, tc), lambda j, b, i: (0, j + n_c))
    bv = pl.BlockSpec((1, tc), lambda j, b, i: (0, j))
    bg = pl.BlockSpec((1, tc), lambda j, b, i: (0, j + n_c))
    out_tile = pl.BlockSpec((1, ts, tc), lambda j, b, i: (b, i, j))
    acc3, acc1 = pl.BlockSpec((3, tc), lambda j, b, i: (0, j)), pl.BlockSpec((1, tc), lambda j, b, i: (0, j))
    (dpv, dpg, dwv, dwg, dbv, dbg), received = _pc_cargo(
        body, cargo, name=name, grid=(n_c, b, n_s),
        out_shape=(_sds((b, s, f), BF16), _sds((b, s, f), BF16), _sds((3, f), F32), _sds((3, f), F32),
                   _sds((1, f), F32), _sds((1, f), F32)),
        in_specs=[tv, pvp, pvn, tg, pgp, pgn, tv, pvn, wv, wg, bv, bg],
        out_specs=(out_tile, out_tile, acc3, acc3, acc1, acc1))(p, p, p, p, p, p, da, da, cw, cw, cb, cb)
    return dpv, dpg, jnp.concatenate([dwv, dwg], axis=1), jnp.concatenate([dbv, dbg], axis=1), received


def _shortconv_fwd(name, q, cw):
    b, s, d3 = q.shape
    d = d3 // 3
    ts, tc = _div(s, CONV_ROWS, CONV_HALO), _div(d, 256, LANE)
    n_s, n_c = s // ts, d // tc

    def body(gb_ref, gc_ref, gch_ref, h_ref, hh_ref, w_ref, r_ref):
        keep = _flag(pl.program_id(1) > 0)
        m_ext = jnp.concatenate([gch_ref[0] * hh_ref[0] * keep, gc_ref[0] * h_ref[0]], axis=0)
        r_ref[0] = (gb_ref[0] * _conv3(m_ext, w_ref)[CONV_HALO:]).astype(BF16)

    tb, _, _ = _halo_specs(ts, tc, CONV_HALO, n_s, lambda j: j)
    tcc, hc, _ = _halo_specs(ts, tc, CONV_HALO, n_s, lambda j: j + n_c)
    th, hh, _ = _halo_specs(ts, tc, CONV_HALO, n_s, lambda j: j + 2 * n_c)
    w = pl.BlockSpec((3, tc), lambda b, i, j: (0, j))
    return _pc(body, name=name, grid=(b, n_s, n_c), out_shape=_sds((b, s, d), BF16),
               in_specs=[tb, tcc, hc, th, hh, w], out_specs=pl.BlockSpec((1, ts, tc), lambda b, i, j: (b, i, j)),
               compiler_params=_cp("parallel", "parallel", "parallel"))(q, q, q, q, q, cw)


def _shortconv_bwd(name, q, dr, cw):
    b, s, d3 = q.shape
    d = d3 // 3
    ts, tc = _div(s, CONV_ROWS, CONV_HALO), _div(d, 256, LANE)
    n_s, n_c = s // ts, d // tc

    def body(gb_ref, gbn_ref, gc_ref, gcp_ref, h_ref, hp_ref, dr_ref, drn_ref, w_ref,
             dgb_ref, dgc_ref, dh_ref, dw_ref):
        bi, i = pl.program_id(1), pl.program_id(2)

        @pl.when((bi == 0) & (i == 0))
        def _():
            dw_ref[...] = jnp.zeros_like(dw_ref)

        keep_prev = _flag(i > 0)
        keep_next = _flag(i < n_s - 1)
        gc, h = gc_ref[0], h_ref[0]
        m_ext = jnp.concatenate([gcp_ref[0] * hp_ref[0] * keep_prev, gc * h], axis=0)
        cm = _conv3(m_ext, w_ref)[CONV_HALO:]
        dr_ = dr_ref[0]
        dgb_ref[0] = (dr_ * cm).astype(BF16)
        dcv_ext = jnp.concatenate([dr_ * gb_ref[0], drn_ref[0] * gbn_ref[0] * keep_next], axis=0)
        dm = _conv3_t(dcv_ext, w_ref)[:ts]
        dgc_ref[0] = (dm * h).astype(BF16)
        dh_ref[0] = (dm * gc).astype(BF16)
        dcv = dcv_ext[:ts]
        for k in range(3):
            shifted = m_ext if k == 2 else _back(m_ext, 2 - k)
            dw_ref[k:k + 1, :] += jnp.sum(dcv * shifted[CONV_HALO:], axis=0, keepdims=True)

    def specs(col):
        per = ts // CONV_HALO
        tile = pl.BlockSpec((1, ts, tc), lambda j, b, i: (b, i, col(j)))
        prev = pl.BlockSpec((1, CONV_HALO, tc), lambda j, b, i: (b, jnp.maximum(i * per - 1, 0), col(j)))
        nxt = pl.BlockSpec((1, CONV_HALO, tc), lambda j, b, i: (b, jnp.minimum((i + 1) * per, n_s * per - 1), col(j)))
        return tile, prev, nxt

    tb, _, nb = specs(lambda j: j)
    tcc, pc_, _ = specs(lambda j: j + n_c)
    th, ph, _ = specs(lambda j: j + 2 * n_c)
    w = pl.BlockSpec((3, tc), lambda j, b, i: (0, j))
    out_tile = pl.BlockSpec((1, ts, tc), lambda j, b, i: (b, i, j))
    o = _sds((b, s, d), BF16)
    return _pc(body, name=name, grid=(n_c, b, n_s), out_shape=(o, o, o, _sds((3, d), F32)),
               in_specs=[tb, nb, tcc, pc_, th, ph, tb, nb, w], out_specs=(out_tile, out_tile, out_tile, w),
               compiler_params=_cp("parallel", "arbitrary", "arbitrary"))(q, q, q, q, q, q, dr, dr, cw)


def _pick_window(group, cands):
    gid = jnp.full(cands[0].shape, group, jnp.int32)
    out = cands[-1]
    for k in range(len(cands) - 2, -1, -1):
        out = jnp.where(gid == k, cands[k], out)
    return out


def _window_sums(v, shift):
    s1 = v + shift(v, 1)
    s2 = s1 + shift(s1, 2)
    s3 = s2 + shift(s2, 4)
    s4 = s3 + shift(s3, 8)
    return [s1, s2, s3, s4]


def _pool_counts(group, first_row, n_rows, cols):
    t = _rows_iota((n_rows, cols)) + first_row
    window = _pick_window(group, [jnp.full((n_rows, cols), 2 << k, jnp.int32) for k in range(POOL_GROUPS)])
    return jnp.minimum(t + 1, window).astype(F32)


def _pool_fwd(name, x, sc, sh, w, scale):
    b, s, d = x.shape
    tc = d // POOL_GROUPS
    ts = _div(s, CONV_ROWS, POOL_HALO)
    n_s = s // ts

    def body(x_ref, xp_ref, sc_ref, sh_ref, w_ref, scale_ref, y_ref):
        i, grp = pl.program_id(1), pl.program_id(2)
        keep = _flag(i > 0)
        mod = 1.0 + sc_ref[0]
        u = x_ref[0] * mod + sh_ref[0]
        u_ext = jnp.concatenate([(xp_ref[0] * mod + sh_ref[0]) * keep, u], axis=0)
        summed = _pick_window(grp, _window_sums(u_ext, _back))[POOL_HALO:]
        pooled = summed / _pool_counts(grp, i * ts, ts, tc) - u
        y_ref[0] = jnp.dot(pooled.astype(BF16), w_ref[0], preferred_element_type=F32) * scale_ref[...]

    tile, prev, _ = _halo_specs(ts, tc, POOL_HALO, n_s, lambda j: j)
    seq = pl.BlockSpec((1, 1, tc), lambda b, i, j: (b, 0, j))
    return _pc(body, name=name, grid=(b, n_s, POOL_GROUPS), out_shape=_sds(x.shape, F32),
               in_specs=[tile, prev, seq, seq, pl.BlockSpec((1, tc, tc), lambda b, i, j: (j, 0, 0)),
                         pl.BlockSpec((1, tc), lambda b, i, j: (0, j))],
               out_specs=pl.BlockSpec((1, ts, tc), lambda b, i, j: (b, i, j)),
               compiler_params=_cp("parallel", "parallel", "parallel"))(x, x, sc, sh, w, scale)


def _pool_bwd(name, x, sc, sh, dy, w, w_t, scale, cargo=()):
    b, s, d = x.shape
    tc = d // POOL_GROUPS
    ts = _div(s, CONV_ROWS, POOL_HALO)
    n_s = s // ts

    def body(x_ref, xp_ref, sc_ref, sh_ref, dy_ref, dyn_ref, w_ref, wt_ref, scale_ref, du_ref, dw_ref, dscale_ref):
        grp, bi, i = pl.program_id(0), pl.program_id(1), pl.program_id(2)

        @pl.when((bi == 0) & (i == 0))
        def _():
            dw_ref[...] = jnp.zeros_like(dw_ref)
            dscale_ref[...] = jnp.zeros_like(dscale_ref)

        keep_prev = _flag(i > 0)
        keep_next = _flag(i < n_s - 1)
        mod = 1.0 + sc_ref[0]
        u = x_ref[0] * mod + sh_ref[0]
        u_ext = jnp.concatenate([(xp_ref[0] * mod + sh_ref[0]) * keep_prev, u], axis=0)
        summed = _pick_window(grp, _window_sums(u_ext, _back))[POOL_HALO:]
        pooled = (summed / _pool_counts(grp, i * ts, ts, tc) - u).astype(BF16)
        dy_ = dy_ref[0].astype(F32)
        ymat = jnp.dot(pooled, w_ref[0], preferred_element_type=F32)
        dscale_ref[...] += jnp.sum(dy_ * ymat, axis=0, keepdims=True)
        dys_ext = (jnp.concatenate([dy_, dyn_ref[0].astype(F32) * keep_next], axis=0) * scale_ref[...]).astype(BF16)
        dw_ref[0] += lax.dot_general(pooled, dys_ext[:ts], (((0,), (0,)), ((), ())), preferred_element_type=F32)
        dpooled = jnp.dot(dys_ext, wt_ref[0], preferred_element_type=F32)
        e = dpooled / _pool_counts(grp, i * ts, ts + POOL_HALO, tc)
        du_ref[0] = _pick_window(grp, _window_sums(e, _ahead))[:ts] - dpooled[:ts]

    per = ts // POOL_HALO
    tile = pl.BlockSpec((1, ts, tc), lambda j, b, i: (b, i, j))
    prev = pl.BlockSpec((1, POOL_HALO, tc), lambda j, b, i: (b, jnp.maximum(i * per - 1, 0), j))
    nxt = pl.BlockSpec((1, POOL_HALO, tc), lambda j, b, i: (b, jnp.minimum((i + 1) * per, n_s * per - 1), j))
    seq = pl.BlockSpec((1, 1, tc), lambda j, b, i: (b, 0, j))
    wsp = pl.BlockSpec((1, tc, tc), lambda j, b, i: (j, 0, 0))
    vec = pl.BlockSpec((1, tc), lambda j, b, i: (0, j))
    (du, dw, dscale), received = _pc_cargo(
        body, cargo, name=name, grid=(POOL_GROUPS, b, n_s),
        out_shape=(_sds(x.shape, F32), _sds((POOL_GROUPS, tc, tc), F32), _sds((1, d), F32)),
        in_specs=[tile, prev, seq, seq, tile, nxt, wsp, wsp, vec],
        out_specs=(tile, wsp, vec))(x, x, sc, sh, dy, dy, w, w_t, scale)
    return du, dw, dscale, received


def _rope_swap(v):
    lane = lax.broadcasted_iota(jnp.int32, v.shape, v.ndim - 1)
    lo, hi = QK_NOPE, QK_NOPE + QK_ROPE // 2
    from_above = pltpu.roll(v, HEAD_PAD - QK_ROPE // 2, axis=v.ndim - 1)
    from_below = pltpu.roll(v, QK_ROPE // 2, axis=v.ndim - 1)
    return jnp.where((lane >= lo) & (lane < hi), from_above,
                     jnp.where((lane >= hi) & (lane < hi + QK_ROPE // 2), from_below, 0.0))


def _rope(v, cos_t, sin_t):
    return v * cos_t + _rope_swap(v) * sin_t


def _rope_t(dv, cos_t, sin_t):
    return dv * cos_t + _rope_swap(dv * sin_t)


def _rms(v, g):
    r = lax.rsqrt(jnp.mean(v * v, axis=-1, keepdims=True) + RMS_EPS)
    return v * r, r


def _mla_norm_fwd(name, a, qn, kvn, cos_t, sin_t):
    b, s, wa = a.shape
    ql, kvl = qn.shape[1], kvn.shape[1]
    ts = _div(s, 512, 16)

    def body(aq_ref, akv_ref, ape_ref, qn_ref, kvn_ref, cos_ref, sin_ref, cq_ref, ckv_ref, kpe_ref):
        yq, _ = _rms(aq_ref[0], None)
        cq_ref[0] = (yq * qn_ref[...]).astype(BF16)
        ykv, _ = _rms(akv_ref[0], None)
        ckv_ref[0] = (ykv * kvn_ref[...]).astype(BF16)
        kpe_ref[0] = _rope(ape_ref[0], cos_ref[0], sin_ref[0])

    tok = lambda w, col: pl.BlockSpec((1, ts, w), lambda b, i: (b, i, col))
    return _pc(body, name=name, grid=(b, s // ts),
               out_shape=(_sds((b, s, ql), BF16), _sds((b, s, kvl), BF16), _sds((b, s, HEAD_PAD), F32)),
               in_specs=[tok(ql, 0), tok(kvl, ql // kvl), tok(HEAD_PAD, (ql + kvl) // HEAD_PAD), _vec_spec(ql),
                         _vec_spec(kvl), tok(HEAD_PAD, 0), tok(HEAD_PAD, 0)],
               out_specs=(tok(ql, 0), tok(kvl, 0), tok(HEAD_PAD, 0)),
               compiler_params=_cp("parallel", "parallel"))(a, a, a, qn, kvn, cos_t, sin_t)


def _mla_norm_bwd(name, a, dcq, dckv, dkpe, qn, kvn):
    b, s, wa = a.shape
    ql, kvl = qn.shape[1], kvn.shape[1]
    ts = _div(s, 512, 16)

    def body(a_ref, dcq_ref, dckv_ref, dkpe_ref, qn_ref, kvn_ref, da_ref, dqn_ref, dkvn_ref):
        @pl.when((pl.program_id(0) == 0) & (pl.program_id(1) == 0))
        def _():
            dqn_ref[...] = jnp.zeros_like(dqn_ref)
            dkvn_ref[...] = jnp.zeros_like(dkvn_ref)

        def one(v, dc, g_ref, dg_ref):
            yv, r = _rms(v, None)
            dg_ref[...] += jnp.sum(dc * yv, axis=0, keepdims=True)
            dyv = dc * g_ref[...]
            return r * (dyv - yv * jnp.mean(dyv * yv, axis=-1, keepdims=True))

        av = a_ref[0]
        da_ref[0, :, 0:ql] = one(av[:, 0:ql], dcq_ref[0], qn_ref, dqn_ref).astype(BF16)
        da_ref[0, :, ql:ql + kvl] = one(av[:, ql:ql + kvl], dckv_ref[0], kvn_ref, dkvn_ref).astype(BF16)
        da_ref[0, :, ql + kvl:] = dkpe_ref[0].astype(BF16)

    return _pc(body, name=name, grid=(b, s // ts),
               out_shape=(_sds(a.shape, BF16), _sds((1, ql), F32), _sds((1, kvl), F32)),
               in_specs=[_tok_spec(ts, wa), _tok_spec(ts, ql), _tok_spec(ts, kvl), _tok_spec(ts, HEAD_PAD),
                         _vec_spec(ql), _vec_spec(kvl)],
               out_specs=(_tok_spec(ts, wa), _vec_spec(ql), _vec_spec(kvl)),
               compiler_params=_cp("arbitrary", "arbitrary"))(a, dcq, dckv, dkpe, qn, kvn)


def _mla_prep_fwd(name, q_raw, k_raw, kpe, cos_t, sin_t, n_heads):
    b, s, wq = q_raw.shape
    ts = _div(s, 256, 16)

    def body(q_ref, k_ref, kpe_ref, cos_ref, sin_ref, qo_ref, ko_ref):
        cos_, sin_, kpe_ = cos_ref[0], sin_ref[0], kpe_ref[0]
        for h in range(n_heads):
            lanes = slice(h * HEAD_PAD, (h + 1) * HEAD_PAD)
            qo_ref[0, :, lanes] = _rope(q_ref[0, :, lanes], cos_, sin_).astype(BF16)
            ko_ref[0, :, lanes] = (k_ref[0, :, lanes] + kpe_).astype(BF16)

    wide = pl.BlockSpec((1, ts, wq), lambda b, i: (b, i, 0))
    tok = pl.BlockSpec((1, ts, HEAD_PAD), lambda b, i: (b, i, 0))
    o = _sds(q_raw.shape, BF16)
    return _pc(body, name=name, grid=(b, s // ts), out_shape=(o, o),
               in_specs=[wide, wide, tok, tok, tok], out_specs=(wide, wide),
               compiler_params=_cp("parallel", "parallel"))(q_raw, k_raw, kpe, cos_t, sin_t)


def _mla_prep_bwd(name, dq, dk, cos_t, sin_t, n_heads):
    b, s, wq = dq.shape
    ts = _div(s, 256, 16)

    def body(dq_ref, dk_ref, cos_ref, sin_ref, dqr_ref, dkr_ref, dkpe_ref):
        cos_, sin_ = cos_ref[0], sin_ref[0]
        dk_sum = None
        for h in range(n_heads):
            lanes = slice(h * HEAD_PAD, (h + 1) * HEAD_PAD)
            dqr_ref[0, :, lanes] = _rope_t(dq_ref[0, :, lanes], cos_, sin_).astype(BF16)
            dk_h = dk_ref[0, :, lanes]
            dkr_ref[0, :, lanes] = dk_h.astype(BF16)
            dk_sum = dk_h if dk_sum is None else dk_sum + dk_h
        dkpe_ref[0] = _rope_t(dk_sum, cos_, sin_)

    wide = pl.BlockSpec((1, ts, wq), lambda b, i: (b, i, 0))
    tok = pl.BlockSpec((1, ts, HEAD_PAD), lambda b, i: (b, i, 0))
    return _pc(body, name=name, grid=(b, s // ts),
               out_shape=(_sds(dq.shape, BF16), _sds(dq.shape, BF16), _sds((b, s, HEAD_PAD), F32)),
               in_specs=[wide, wide, tok, tok], out_specs=(wide, wide, tok),
               compiler_params=_cp("parallel", "parallel"))(dq, dk, cos_t, sin_t)


FLASH_TILE = 1024
LOG2_E = 1.4426950408889634


def _heads_per_step(n_heads):
    return 2 if n_heads % 2 == 0 else 1


def _causal_mask(i, j, tq, tk):
    rows = lax.broadcasted_iota(jnp.int32, (tq, tk), 0) + i * tq
    cols = lax.broadcasted_iota(jnp.int32, (tq, tk), 1) + j * tk
    return cols <= rows


def _nt(a, b):
    return lax.dot_general(a, b, (((1,), (1,)), ((), ())), preferred_element_type=F32)


def _tn(a, b):
    return lax.dot_general(a, b, (((0,), (0,)), ((), ())), preferred_element_type=F32)


def _flash_fwd(name, q, k, v, n_heads, sm_scale, cargo=()):
    b, s, _ = q.shape
    t, hp = _div(s, FLASH_TILE, LANE), _heads_per_step(n_heads)
    n, w = s // t, hp * HEAD_PAD
    neg = float(jnp.finfo(jnp.float32).min)
    c2 = sm_scale * LOG2_E

    def body(q_ref, k_ref, v_ref, o_ref, lse_ref, m_ref, l_ref, acc_ref):
        i, j = pl.program_id(2), pl.program_id(3)

        @pl.when(j == 0)
        def _():
            m_ref[...] = jnp.full(m_ref.shape, neg, F32)
            l_ref[...] = jnp.zeros_like(l_ref)
            acc_ref[...] = jnp.zeros_like(acc_ref)

        def block(on_diagonal):
            for hh in range(hp):
                ln = slice(hh * HEAD_PAD, (hh + 1) * HEAD_PAD)
                sc = _nt(q_ref[0, :, ln], k_ref[0, :, ln])
                if on_diagonal:
                    sc = jnp.where(_causal_mask(i, j, t, t), sc, neg)
                m_old = m_ref[hh]
                m_new = jnp.maximum(m_old, jnp.max(sc, axis=-1, keepdims=True))
                p = jnp.exp2((sc - m_new) * c2)
                corr = jnp.exp2((m_old - m_new) * c2)
                l_ref[hh] = corr * l_ref[hh] + jnp.sum(p, axis=-1, keepdims=True)
                acc_ref[:, ln] = corr * acc_ref[:, ln] + jnp.dot(p.astype(BF16), v_ref[0, :, ln],
                                                                 preferred_element_type=F32)
                m_ref[hh] = m_new

        pl.when(j < i)(functools.partial(block, False))
        pl.when(j == i)(functools.partial(block, True))

        @pl.when(j == n - 1)
        def _():
            for hh in range(hp):
                ln = slice(hh * HEAD_PAD, (hh + 1) * HEAD_PAD)
                o_ref[0, :, ln] = (acc_ref[:, ln] / l_ref[hh]).astype(BF16)
                lse_ref[0, :, ln] = jnp.broadcast_to(m_ref[hh] * sm_scale + jnp.log(l_ref[hh]), (t, HEAD_PAD))

    qs = pl.BlockSpec((1, t, w), lambda b, h, i, j: (b, i, h))
    ks = pl.BlockSpec((1, t, w), lambda b, h, i, j: (b, jnp.minimum(j, i), h))
    (o, lse), fetched = _pc_cargo(
        body, cargo, name=name, grid=(b, n_heads // hp, n, n), route=_Fetch,
        out_shape=(_sds(q.shape, BF16), _sds(q.shape, F32)), in_specs=[qs, ks, ks], out_specs=(qs, qs),
        scratch_shapes=[pltpu.VMEM((hp, t, 1), F32), pltpu.VMEM((hp, t, 1), F32), pltpu.VMEM((t, w), F32)])(q, k, v)
    return o, lse, fetched


def _flash_bwd(name, q, k, v, o, lse, do, n_heads, sm_scale):
    b, s, _ = q.shape
    t, hp = _div(s, FLASH_TILE, LANE), _heads_per_step(n_heads)
    n, w = s // t, hp * HEAD_PAD
    c2 = sm_scale * LOG2_E

    def body(q_ref, k_ref, v_ref, o_ref, lse_ref, do_ref, dq_hbm, dk_ref, dv_ref, dq_acc, dk_acc, dv_acc, dq_sem):
        bi, hi, j, i = pl.program_id(0), pl.program_id(1), pl.program_id(2), pl.program_id(3)

        @pl.when(i == 0)
        def _():
            dk_acc[...] = jnp.zeros_like(dk_acc)
            dv_acc[...] = jnp.zeros_like(dv_acc)

        rows = pl.ds(pl.multiple_of(i * t, t), t)

        def block(on_diagonal):
            for hh in range(hp):
                ln = slice(hh * HEAD_PAD, (hh + 1) * HEAD_PAD)
                do_ = do_ref[0, :, ln]
                delta = jnp.sum(do_.astype(F32) * o_ref[0, :, ln].astype(F32), axis=-1, keepdims=True)
                sc = _nt(q_ref[0, :, ln], k_ref[0, :, ln])
                p = jnp.exp2(sc * c2 - lse_ref[0, :, hh * HEAD_PAD:hh * HEAD_PAD + 1] * LOG2_E)
                if on_diagonal:
                    p = jnp.where(_causal_mask(i, j, t, t), p, 0.0)
                dv_acc[:, ln] += _tn(p.astype(BF16), do_)
                dp = _nt(do_, v_ref[0, :, ln])
                ds = (p * (dp - delta)).astype(BF16)
                dk_acc[:, ln] += _tn(ds, q_ref[0, :, ln])
                dq_part = jnp.dot(ds, k_ref[0, :, ln], preferred_element_type=F32)

                @pl.when(j == 0)
                def _():
                    dq_acc[rows, ln] = dq_part

                @pl.when(j > 0)
                def _():
                    dq_acc[rows, ln] += dq_part

        pl.when(i > j)(functools.partial(block, False))
        pl.when(i == j)(functools.partial(block, True))

        @pl.when(i == j)
        def _():
            dq_acc[rows, :] = dq_acc[rows, :] * sm_scale
            done = pltpu.make_async_copy(dq_acc.at[rows, :], dq_hbm.at[bi, rows, pl.ds(pl.multiple_of(hi * w, w), w)],
                                         dq_sem)
            done.start()
            done.wait()

        @pl.when(i == n - 1)
        def _():
            dk_ref[0] = dk_acc[...] * sm_scale
            dv_ref[0] = dv_acc[...].astype(BF16)

    qs = pl.BlockSpec((1, t, w), lambda b, h, j, i: (b, jnp.maximum(i, j), h))
    ks = pl.BlockSpec((1, t, w), lambda b, h, j, i: (b, j, h))
    return _pc(body, name=name, grid=(b, n_heads // hp, n, n),
               out_shape=(_sds(q.shape, F32), _sds(q.shape, F32), _sds(q.shape, BF16)),
               in_specs=[qs, ks, ks, qs, qs, qs], out_specs=(pl.BlockSpec(memory_space=pltpu.HBM), ks, ks),
               scratch_shapes=[pltpu.VMEM((s, w), F32), pltpu.VMEM((t, w), F32), pltpu.VMEM((t, w), F32),
                               pltpu.SemaphoreType.DMA],
               compiler_params=_cp("arbitrary", "arbitrary", "arbitrary", "arbitrary"))(q, k, v, o, lse, do)


def _mod_fwd(name, c_all, w, bias):
    depth, d, n = w.shape
    rows = c_all.shape[0]

    def body(c_ref, w_ref, b_ref, o_ref):
        cv = c_ref[...]
        cond = (cv * _sigmoid(cv)).astype(BF16)
        o_ref[0] = jnp.dot(cond, w_ref[0].astype(BF16), preferred_element_type=F32) + b_ref[0]

    return _pc(body, name=name, grid=(depth,), out_shape=_sds((depth, rows, n), F32),
               in_specs=[pl.BlockSpec((rows, d), lambda l: (0, 0)), pl.BlockSpec((1, d, n), lambda l: (l, 0, 0)),
                         pl.BlockSpec((1, 1, n), lambda l: (l, 0, 0))],
               out_specs=pl.BlockSpec((1, rows, n), lambda l: (l, 0, 0)), compiler_params=_cp("parallel"))(c_all, w, bias)


def _mod_bwd(name, c_all, dmod_cols, dmod_all):
    depth, rows, n = dmod_cols.shape
    d = c_all.shape[1]
    n_all = dmod_all.shape[2]
    tn = _div(n, 512, LANE)

    def body(c_ref, dm_ref, dma_ref, gw_ref, gb_ref):
        cv = c_ref[...]
        cond = (cv * _sigmoid(cv)).astype(BF16)
        gw_ref[0] = _tn(cond, dm_ref[0].astype(BF16))

        @pl.when(pl.program_id(1) == 0)
        def _():
            gb_ref[0] = jnp.sum(dma_ref[0], axis=0, keepdims=True)

    return _pc(body, name=name, grid=(depth, n // tn),
               out_shape=(_sds((depth, d, n), F32), _sds((depth, 1, n_all), F32)),
               in_specs=[pl.BlockSpec((rows, d), lambda l, j: (0, 0)), pl.BlockSpec((1, rows, tn), lambda l, j: (l, 0, j)),
                         pl.BlockSpec((1, rows, n_all), lambda l, j: (l, 0, 0))],
               out_specs=(pl.BlockSpec((1, d, tn), lambda l, j: (l, 0, j)), pl.BlockSpec((1, 1, n_all), lambda l, j: (l, 0, 0))),
               compiler_params=_cp("parallel", "arbitrary"))(c_all, dmod_cols, dmod_all)


def _adamw(name, w, g, m, v):
    shape = w.shape
    cols = shape[-1]
    rows = _size(shape) // cols
    tr = _div(rows, max(SUBLANE, (2 ** 19) // cols // SUBLANE * SUBLANE), SUBLANE)
    c1 = 1.0 - ADAM_B1 ** ADAM_STEP
    c2 = 1.0 - ADAM_B2 ** ADAM_STEP

    def body(w_ref, g_ref, m_ref, v_ref, d_ref, mo_ref, vo_ref):
        gv = g_ref[...]
        m_new = ADAM_B1 * m_ref[...] + (1.0 - ADAM_B1) * gv
        v_new = ADAM_B2 * v_ref[...] + (1.0 - ADAM_B2) * (gv * gv)
        m_hat = m_new / c1
        v_hat = v_new / c2
        d_ref[...] = -ADAM_LR * (m_hat / (jnp.sqrt(v_hat) + ADAM_EPS) + ADAM_WD * w_ref[...])
        mo_ref[...] = m_new
        vo_ref[...] = v_new

    spec = pl.BlockSpec((tr, cols), lambda i: (i, 0))
    o = _sds((rows, cols), F32)
    outs = _pc(body, name=name, grid=(rows // tr,), out_shape=(o, o, o), in_specs=[spec] * 4, out_specs=(spec,) * 3,
               compiler_params=_cp("parallel"))(*[a.reshape(rows, cols) for a in (w, g, m, v)])
    return tuple(a.reshape(shape) for a in outs)


def _adamw_layers(name, w, g_layers, m, v):
    n_l = len(g_layers)
    rows, cols = g_layers[0].shape
    tr = _div(rows, max(SUBLANE, (2 ** 19) // cols // SUBLANE * SUBLANE), SUBLANE)
    c1 = 1.0 - ADAM_B1 ** ADAM_STEP
    c2 = 1.0 - ADAM_B2 ** ADAM_STEP

    def body(*refs):
        w_ref, m_ref, v_ref = refs[:3]
        g_refs = refs[3:3 + n_l]
        go_ref, d_ref, mo_ref, vo_ref = refs[3 + n_l:]
        layer = pl.program_id(0)
        for l in range(n_l):
            @pl.when(layer == l)
            def _():
                gv = g_refs[l][...]
                m_new = ADAM_B1 * m_ref[0] + (1.0 - ADAM_B1) * gv
                v_new = ADAM_B2 * v_ref[0] + (1.0 - ADAM_B2) * (gv * gv)
                m_hat = m_new / c1
                v_hat = v_new / c2
                d_ref[0] = -ADAM_LR * (m_hat / (jnp.sqrt(v_hat) + ADAM_EPS) + ADAM_WD * w_ref[0])
                mo_ref[0] = m_new
                vo_ref[0] = v_new
                go_ref[0] = gv

    def g_spec(l):
        return pl.BlockSpec((tr, cols), lambda layer, i: (jnp.where(layer == l, i, 0), 0))

    stacked = pl.BlockSpec((1, tr, cols), lambda layer, i: (layer, i, 0))
    o = _sds((n_l, rows, cols), F32)
    outs = _pc(body, name=name, grid=(n_l, rows // tr), out_shape=(o, o, o, o),
               in_specs=[stacked] * 3 + [g_spec(l) for l in range(n_l)], out_specs=(stacked,) * 4,
               compiler_params=_cp("parallel", "parallel"))(
        *[a.reshape(n_l, rows, cols) for a in (w, m, v)], *g_layers)
    return tuple(a.reshape(w.shape) for a in outs)


def _rope_tables(positions):
    half = QK_ROPE // 2
    inv_freq = ROPE_THETA ** (-jnp.arange(0, QK_ROPE, 2, dtype=F32) / QK_ROPE)
    ang = positions.astype(F32)[..., None] * inv_freq
    cos, sin = jnp.cos(ang), jnp.sin(ang)
    lead = positions.shape
    ones = jnp.ones(lead + (QK_NOPE,), F32)
    tail_one = jnp.ones(lead + (HEAD_PAD - QK_NOPE - QK_ROPE,), F32)
    cos_t = jnp.concatenate([ones, cos, cos, tail_one], axis=-1)
    sin_t = jnp.concatenate([0 * ones, -sin, sin, 0 * tail_one], axis=-1)
    return cos_t, sin_t


def _pad_heads(w, n_heads, parts, axis):
    w = jnp.moveaxis(w, axis, -1)
    lead = w.shape[:-1]
    per = w.shape[-1] // n_heads
    w = w.reshape(lead + (n_heads, per))
    kept = jnp.concatenate([w[..., a:b_] for a, b_ in parts], axis=-1)
    pad = HEAD_PAD - kept.shape[-1]
    kept = jnp.concatenate([kept, jnp.zeros(lead + (n_heads, pad), w.dtype)], axis=-1)
    return jnp.moveaxis(kept.reshape(lead + (n_heads * HEAD_PAD,)), -1, axis)


def _unpad_heads(g, n_heads, width, axis):
    g = jnp.moveaxis(g, axis, -1)
    lead = g.shape[:-1]
    g = g.reshape(lead + (n_heads, HEAD_PAD))[..., :width]
    return g, lead


def kernel(x, c, positions, mod_w, mod_b, ln_g, ln_b, pool_w, pool_scale, mla_w_a, mla_q_norm, mla_w_uq, mla_kv_norm, mla_w_ukv, mla_w_o, sc_w_in, sc_conv, sc_w_out, ffn_w_up, ffn_conv, ffn_conv_b, ffn_w_down, loss_target, m_mod_w, m_mod_b, m_ln_g, m_ln_b, m_pool_w, m_pool_scale, m_mla_w_a, m_mla_q_norm, m_mla_w_uq, m_mla_kv_norm, m_mla_w_ukv, m_mla_w_o, m_sc_w_in, m_sc_conv, m_sc_w_out, m_ffn_w_up, m_ffn_conv, m_ffn_conv_b, m_ffn_w_down, v_mod_w, v_mod_b, v_ln_g, v_ln_b, v_pool_w, v_pool_scale, v_mla_w_a, v_mla_q_norm, v_mla_w_uq, v_mla_kv_norm, v_mla_w_ukv, v_mla_w_o, v_sc_w_in, v_sc_conv, v_sc_w_out, v_ffn_w_up, v_ffn_conv, v_ffn_conv_b, v_ffn_w_down):
    wts = dict(mod_w=mod_w, mod_b=mod_b, ln_g=ln_g, ln_b=ln_b, pool_w=pool_w, pool_scale=pool_scale, mla_w_a=mla_w_a,
               mla_q_norm=mla_q_norm, mla_w_uq=mla_w_uq, mla_kv_norm=mla_kv_norm, mla_w_ukv=mla_w_ukv, mla_w_o=mla_w_o,
               sc_w_in=sc_w_in, sc_conv=sc_conv, sc_w_out=sc_w_out, ffn_w_up=ffn_w_up, ffn_conv=ffn_conv,
               ffn_conv_b=ffn_conv_b, ffn_w_down=ffn_w_down)
    mom1 = dict(mod_w=m_mod_w, mod_b=m_mod_b, ln_g=m_ln_g, ln_b=m_ln_b, pool_w=m_pool_w, pool_scale=m_pool_scale,
                mla_w_a=m_mla_w_a, mla_q_norm=m_mla_q_norm, mla_w_uq=m_mla_w_uq, mla_kv_norm=m_mla_kv_norm,
                mla_w_ukv=m_mla_w_ukv, mla_w_o=m_mla_w_o, sc_w_in=m_sc_w_in, sc_conv=m_sc_conv, sc_w_out=m_sc_w_out,
                ffn_w_up=m_ffn_w_up, ffn_conv=m_ffn_conv, ffn_conv_b=m_ffn_conv_b, ffn_w_down=m_ffn_w_down)
    mom2 = dict(mod_w=v_mod_w, mod_b=v_mod_b, ln_g=v_ln_g, ln_b=v_ln_b, pool_w=v_pool_w, pool_scale=v_pool_scale,
                mla_w_a=v_mla_w_a, mla_q_norm=v_mla_q_norm, mla_w_uq=v_mla_w_uq, mla_kv_norm=v_mla_kv_norm,
                mla_w_ukv=v_mla_w_ukv, mla_w_o=v_mla_w_o, sc_w_in=v_sc_w_in, sc_conv=v_sc_conv, sc_w_out=v_sc_w_out,
                ffn_w_up=v_ffn_w_up, ffn_conv=v_ffn_conv, ffn_conv_b=v_ffn_conv_b, ffn_w_down=v_ffn_w_down)

    bsz, seq, d = x.shape
    depth = mod_b.shape[0]
    n_tok = bsz * seq
    n_heads = d // V_HEAD
    ql, kvl = mla_q_norm.shape[1], mla_kv_norm.shape[1]
    alpha = float((2 * depth) ** 0.25)
    sm_scale = float((QK_NOPE + QK_ROPE) ** -0.5)
    mx, my, mc = lax.axis_index("x"), lax.axis_index("y"), lax.axis_index("c")
    chip = 2 * mx + my
    dev = 2 * chip + mc

    small_names = list(SMALL_SHARDED)
    small_pack, small_spans = _pack_rows([c] + [wts[n] for n in small_names], F32, SUBLANE)
    rows_small = small_pack.shape[0]
    small_all = _all_gather8("gather_small_params", small_pack, True).reshape(N_DEV, rows_small * PACK_COLS)
    c_all = small_all[:, :c.size].reshape(N_DEV * bsz, d)
    per_chip = small_all[0::2]
    full = dict(wts)
    for n, (off, shape) in zip(small_names, small_spans[1:]):
        blocks = per_chip[:, off:off + _size(shape)].reshape((N_CHIP,) + tuple(shape))
        full[n] = _join_chips(blocks, SMALL_SHARDED[n])

    n_mod = mod_w.shape[2]
    bias_cols = lax.dynamic_slice_in_dim(mod_b, chip * n_mod, n_mod, axis=1)[:, None, :]
    mod_cols = _mod_fwd("mod_fwd", c_all, mod_w, bias_cols)
    half_rows = (N_DEV * bsz) // 2
    mod_half = lax.dynamic_slice_in_dim(mod_cols, mc * half_rows, half_rows, axis=1).reshape(depth * half_rows, n_mod)
    mod_all = _all_gather8("gather_mod", mod_half, True).reshape(N_CHIP, 2, depth, half_rows, n_mod)
    mod_all = jnp.transpose(mod_all, (2, 1, 3, 0, 4)).reshape(depth, N_DEV * bsz, N_CHIP * n_mod)
    mod_mine = lax.dynamic_slice_in_dim(mod_all, dev * bsz, bsz, axis=1)
    mods = [[mod_mine[l, :, k * d:(k + 1) * d][:, None, :] for k in range(6)] for l in range(depth)]

    big_names = list(BIG)
    f_hid = ffn_w_down.shape[1] * N_CHIP
    host_layer = 1
    assert depth > host_layer

    def layer_of(n, i):
        if n == 'pool_w':
            return 3 * i
        if n.startswith('mla_'):
            return 3 * i + 1
        if n.startswith('sc_'):
            return 3 * i + 2
        return i

    last_layer_of_group = (0, host_layer)

    def group_of(n, i):
        return sum(layer_of(n, i) > top for top in last_layer_of_group)

    n_groups = len(last_layer_of_group) + 1
    span = {n: [[i for i in range(wts[n].shape[0]) if group_of(n, i) == g] for g in range(n_groups)] for n in big_names}
    members = [[n for n in big_names if span[n][g]] for g in range(n_groups)]

    def rows2d(a):
        return a.astype(BF16).reshape(-1, a.shape[-1])

    def layouts(bc):
        out = {}
        for n in ('pool_w', 'mla_w_a', 'mla_w_uq', 'mla_w_ukv', 'mla_w_o', 'sc_w_out'):
            if n in bc:
                out[n] = jnp.concatenate([bc[n][j] for j in range(N_CHIP)], axis=BIG[n])
        if 'ffn_w_up' in bc:
            nl = bc['ffn_w_up'].shape[1]
            out['up_cols'] = bc['ffn_w_up'].reshape(N_CHIP, nl * d, -1)
            out['up_rows'] = jnp.transpose(bc['ffn_w_up'], (1, 0, 3, 2)).reshape(1, nl * 2 * f_hid, d)
            out['down_rows'] = jnp.transpose(bc['ffn_w_down'], (1, 0, 2, 3)).reshape(1, nl * f_hid, d)
            out['down_t'] = jnp.transpose(bc['ffn_w_down'], (1, 3, 0, 2)).reshape(1, nl * d, f_hid)
        if 'sc_w_in' in bc:
            ns = bc['sc_w_in'].shape[1]
            out['in_cols'] = bc['sc_w_in'].reshape(N_CHIP, ns * d, -1)
            out['in_rows'] = jnp.transpose(bc['sc_w_in'], (1, 0, 3, 2)).reshape(1, ns * 3 * d, d)
        return out

    shards = [{n: rows2d(wts[n][span[n][g][0]:span[n][g][-1] + 1]) for n in members[g]} for g in range(n_groups)]
    lay = [None] * n_groups

    def by_chip(g, n, blocks):
        return blocks.reshape((N_CHIP, len(span[n][g])) + wts[n].shape[1:])

    def fetched_group(g, got):
        lay[g] = layouts({n: by_chip(g, n, lax.dynamic_update_index_in_dim(blocks, shards[g][n], chip, 0))
                          for n, blocks in got.items()})

    gathered = _gather_weights("gather_weights", [shards[0][n] for n in members[0]])
    lay[0] = layouts({n: by_chip(0, n, blocks) for n, blocks in zip(members[0], gathered)})

    def grp(n, i):
        g = group_of(n, i)
        return lay[g], i - span[n][g][0]

    nope_rope = [(0, QK_NOPE + QK_ROPE)]
    cos_t, sin_t = _rope_tables(positions)

    def t2(a):
        return a.reshape(n_tok, a.shape[-1])

    def t3(a):
        return a.reshape(bsz, seq, a.shape[-1])

    saved = []
    xin = x
    u = _modulate("modulate_in", x, mods[0][1], mods[0][0])
    loss_acc = None
    for l in range(depth):
        sh1, sc1, g1, sh2, sc2, g2 = mods[l]
        kind, j = l % 3, l // 3
        st = dict(x=xin)
        if kind == 0:
            grp_l, jj = grp('pool_w', j)
            w = grp_l['pool_w'][jj]
            st.update(w=w, w_t=jnp.swapaxes(w, 1, 2), scale=full['pool_scale'][j][None, :])
            y = _pool_fwd(f"pool_fwd_{l}", xin, sc1, sh1, st['w'], st['scale'])
        elif kind == 1:
            grp_l, jj = grp('mla_w_a', j)
            wa, wuq, wukv = grp_l['mla_w_a'][jj], grp_l['mla_w_uq'][jj], grp_l['mla_w_ukv'][jj]
            zeros = jnp.zeros((d, QK_NOPE), BF16)
            w_a = jnp.concatenate([wa[:, :ql + kvl], zeros, wa[:, ql + kvl:], zeros[:, :HEAD_PAD - QK_NOPE - QK_ROPE]], axis=1)
            w_uq = _pad_heads(wuq, n_heads, nope_rope, 1)
            w_k = _pad_heads(wukv, n_heads, [(0, QK_NOPE)], 1)
            w_v = _pad_heads(wukv, n_heads, [(QK_NOPE, QK_NOPE + V_HEAD)], 1)
            w_o = _pad_heads(grp_l['mla_w_o'][jj], n_heads, [(0, V_HEAD)], 0)
            qn, kvn = mla_q_norm[j][None, :], mla_kv_norm[j][None, :]
            a = t3(_mm_nn(f"mla_a_{l}", [(t2(u), _w2(w_a))], F32))
            cq, ckv, kpe = _mla_norm_fwd(f"mla_norm_fwd_{l}", a, qn, kvn, cos_t, sin_t)
            q_raw = t3(_mm_nn(f"mla_q_{l}", [(t2(cq), _w2(w_uq))], F32))
            k_raw = t3(_mm_nn(f"mla_k_{l}", [(t2(ckv), _w2(w_k))], F32))
            vh = t3(_mm_nn(f"mla_v_{l}", [(t2(ckv), _w2(w_v))], BF16))
            qh, kh = _mla_prep_fwd(f"mla_prep_fwd_{l}", q_raw, k_raw, kpe, cos_t, sin_t, n_heads)
            o, lse, fetched = _flash_fwd(f"flash_fwd_{l}", qh, kh, vh, n_heads, sm_scale,
                                         cargo=[shards[2][n] for n in members[2]] if l == host_layer else ())
            if l == host_layer:
                fetched_group(2, dict(zip(members[2], fetched)))
            y = t3(_mm_nn(f"mla_o_{l}", [(t2(o), _w2(w_o))], F32))
            st.update(u=u, w_a=w_a, w_uq=w_uq, w_k=w_k, w_v=w_v, w_o=w_o, qn=qn, kvn=kvn, a=a, cq=cq, ckv=ckv,
                      qh=qh, kh=kh, vh=vh, o=o, lse=lse)
        else:
            grp_l, jj = grp('sc_w_in', j)
            w_out, cw = grp_l['sc_w_out'][jj], full['sc_conv'][j]
            q = t3(_mm_nn(f"sc_in_{l}", [(t2(u), (grp_l['in_cols'], jj))], F32))
            r = _shortconv_fwd(f"shortconv_fwd_{l}", q, cw)
            y = t3(_mm_nn(f"sc_out_{l}", [(t2(r), _w2(w_out))], F32))
            st.update(u=u, w_out=w_out, cw=cw, q=q, r=r, in_rows=(grp_l['in_rows'], jj))
        lng, lnb = full['ln_g'][l], full['ln_b'][l]
        z1, xmid, u2 = _ln_mod_fwd(f"ln_mod_a_{l}", alpha, xin, y, g1, lng[0:1], lnb[0:1], sc2, sh2)
        cwf, cbf = full['ffn_conv'][l], ffn_conv_b[l][None, :]
        ffn_w, ll = grp('ffn_w_up', l)
        ffn_names = ('ffn_w_up', 'ffn_w_down')
        ride_mm = [n for n in members[1] if n not in ffn_names] if l == 0 else []
        ride_conv = [n for n in members[1] if n == 'ffn_w_up'] if l == 0 else []
        ride_down = [n for n in members[1] if n == 'ffn_w_down'] if l == 0 else []
        p = _mm_nn(f"ffn_up_{l}", [(t2(u2), (ffn_w['up_cols'], ll))], F32, cargo=[shards[1][n] for n in ride_mm],
                   route=_Fetch)
        got_mm = ()
        if ride_mm:
            p, got_mm = p
        p = t3(p)
        act, got_conv = _convglu_fwd(f"convglu_fwd_{l}", p, cwf, cbf, cargo=[shards[1][n] for n in ride_conv])
        y2 = _mm_nn(f"ffn_down_{l}", [(t2(act), (ffn_w['down_rows'], ll))], F32, cargo=[shards[1][n] for n in ride_down],
                    route=_Fetch, tm_cap=WIDE_K_ROWS)
        got_down = ()
        if ride_down:
            y2, got_down = y2
        y2 = t3(y2)
        if l == 0:
            fetched_group(1, {**dict(zip(ride_mm, got_mm)), **dict(zip(ride_conv, got_conv)),
                              **dict(zip(ride_down, got_down))})
        st.update(y1=y, z1=z1, xmid=xmid, u2=u2, p=p, act=act, y2=y2, cwf=cwf, cbf=cbf, lng=lng, lnb=lnb,
                  ffn_w=ffn_w, ll=ll)
        if l + 1 < depth:
            nsh1, nsc1 = mods[l + 1][0], mods[l + 1][1]
            z2, xin, u = _ln_mod_fwd(f"ln_mod_b_{l}", alpha, xmid, y2, g2, lng[1:2], lnb[1:2], nsc1, nsh1)
        else:
            z2, ct, loss_acc = _ln_loss_fwd("ln_loss", alpha, xmid, y2, g2, lng[1:2], lnb[1:2], loss_target)
        st.update(z2=z2)
        saved.append(st)
    loss = lax.psum(loss_acc[0, 0], ("x", "y", "c"))

    grads = {}
    dmods = [[None] * 6 for _ in range(depth)]
    g_ln_g = [[None, None] for _ in range(depth)]
    g_ln_b = [[None, None] for _ in range(depth)]
    stack = {n: [None] * wts[n].shape[0] for n in ('pool_scale', 'mla_q_norm', 'mla_kv_norm', 'sc_conv', 'ffn_conv',
                                                    'ffn_conv_b')}
    units = {n: [None] * wts[n].shape[0] for n in big_names}
    cargo_a, cargo_b, received = [], [], {}

    def landed(items, got):
        for (n, i, _), r in zip(items, got):
            received[(n, i)] = r

    upstream = (ct,)
    for l in reversed(range(depth)):
        st = saved[l]
        sh1, sc1, g1, sh2, sc2, g2 = mods[l]
        kind, j = l % 3, l // 3
        if len(upstream) > 1:
            upstream = (upstream[0], upstream[1], st['lnb'][1:2], upstream[2])
        res = _sub_bwd(f"sub_bwd_b_{l}", alpha, upstream, st['z2'], st['y2'], g2, st['lng'][1:2])
        dz2, dy2, dmods[l][5], g_ln_g[l][1], g_ln_b[l][1] = res[:5]
        if l + 1 < depth:
            dmods[l + 1][1], dmods[l + 1][0] = res[5], res[6]
        dy2f = t2(dy2)
        ffn_w, ll = st['ffn_w'], st['ll']
        da = t3(_mm_nn(f"ffn_down_bwd_{l}", [(dy2f, (ffn_w['down_t'], ll))], F32))
        units['ffn_w_down'][l] = _mm_tn(f"ffn_down_dw_{l}", t2(st['act']), [dy2f],
                                        out_dtype=BF16).reshape(N_CHIP, f_hid // N_CHIP, d)
        dpv, dpg, dcw, dcb, got = _convglu_bwd(f"convglu_bwd_{l}", st['p'], da, st['cwf'], st['cbf'],
                                               cargo=[u for _, _, u in cargo_a])
        landed(cargo_a, got)
        stack['ffn_conv'][l], stack['ffn_conv_b'][l] = dcw, dcb[0]
        down_unit = [('ffn_w_down', l, units['ffn_w_down'][l])]
        du2, got = _mm_nn(f"ffn_up_bwd_{l}", [(t2(dpv), (ffn_w['up_rows'], 2 * ll)),
                                              (t2(dpg), (ffn_w['up_rows'], 2 * ll + 1))], F32,
                          cargo=[units['ffn_w_down'][l]], tm_cap=WIDE_K_ROWS)
        landed(down_unit, got)
        du2 = t3(du2)
        res = _mm_tn(f"ffn_up_dw_{l}", t2(st['u2']), [t2(dpv), t2(dpg)], N_CHIP, out_dtype=BF16,
                     cargo=[u for _, _, u in cargo_b])
        if cargo_b:
            landed(cargo_b, res[1])
            res = res[0]
        units['ffn_w_up'][l] = res
        res = _sub_bwd(f"sub_bwd_a_{l}", alpha, (dz2, du2, st['lnb'][0:1], sc2), st['z1'], st['y1'], g1, st['lng'][0:1])
        dz1, dy1, dmods[l][2], g_ln_g[l][0], g_ln_b[l][0], dmods[l][4], dmods[l][3] = res
        dy1f = t2(dy1)
        if kind == 0:
            up_unit = [('ffn_w_up', l, units['ffn_w_up'][l])] if l == 0 else []
            du1, dw, dscale, got = _pool_bwd(f"pool_bwd_{l}", st['x'], sc1, sh1, dy1, st['w'], st['w_t'], st['scale'],
                                             cargo=[u for _, _, u in up_unit])
            landed(up_unit, got)
            stack['pool_scale'][j] = dscale[0]
            grp = dw.shape[1] // N_CHIP
            units['pool_w'][j] = jnp.transpose(dw.reshape(POOL_GROUPS, N_CHIP, grp, dw.shape[2]),
                                               (1, 0, 2, 3)).reshape(N_CHIP, POOL_GROUPS * grp, dw.shape[2])
        elif kind == 1:
            do = t3(_mm_nn(f"mla_o_bwd_{l}", [(dy1f, _w2(jnp.swapaxes(st['w_o'], 0, 1)))], BF16))
            gwo, _ = _unpad_heads(_mm_tn(f"mla_o_dw_{l}", t2(st['o']), [dy1f])[0], n_heads, V_HEAD, 0)
            units['mla_w_o'][j] = jnp.moveaxis(gwo.reshape(d, n_heads * V_HEAD), -1, 0).reshape(N_CHIP, -1, d)
            fa = (st['qh'], st['kh'], st['vh'], st['o'], st['lse'], do, n_heads, sm_scale)
            dq, dk, dv = _flash_bwd(f"flash_bwd_{l}", *fa)
            dq_raw, dk_raw, dkpe = _mla_prep_bwd(f"mla_prep_bwd_{l}", dq, dk, cos_t, sin_t, n_heads)
            dq_raw, dk_raw, dv_raw = t2(dq_raw), t2(dk_raw), t2(dv)
            dcq = t3(_mm_nn(f"mla_q_bwd_{l}", [(dq_raw, _w2(jnp.swapaxes(st['w_uq'], 0, 1)))], F32))
            dckv = t3(_mm_nn(f"mla_kv_bwd_{l}", [(dk_raw, _w2(jnp.swapaxes(st['w_k'], 0, 1))),
                                                   (dv_raw, _w2(jnp.swapaxes(st['w_v'], 0, 1)))], F32,
                            tm_cap=WIDE_K_ROWS))
            gq, _ = _unpad_heads(_mm_tn(f"mla_q_dw_{l}", t2(st['cq']), [dq_raw])[0], n_heads, QK_NOPE + QK_ROPE, 1)
            units['mla_w_uq'][j] = _cols_by_chip(gq.reshape(ql, n_heads * (QK_NOPE + QK_ROPE)))
            gkv = _mm_tn(f"mla_kv_dw_{l}", t2(st['ckv']), [dk_raw, dv_raw])[0]
            gk, _ = _unpad_heads(gkv[:, :n_heads * HEAD_PAD], n_heads, QK_NOPE, 1)
            gv, _ = _unpad_heads(gkv[:, n_heads * HEAD_PAD:], n_heads, V_HEAD, 1)
            units['mla_w_ukv'][j] = _cols_by_chip(
                jnp.concatenate([gk, gv], axis=-1).reshape(kvl, n_heads * (QK_NOPE + V_HEAD)))
            da_, dqn, dkvn = _mla_norm_bwd(f"mla_norm_bwd_{l}", st['a'], dcq, dckv, dkpe, st['qn'], st['kvn'])
            stack['mla_q_norm'][j], stack['mla_kv_norm'][j] = dqn[0], dkvn[0]
            du1 = t3(_mm_nn(f"mla_a_bwd_{l}", [(t2(da_), _w2(jnp.swapaxes(st['w_a'], 0, 1)))], F32))
            gwa = _mm_tn(f"mla_a_dw_{l}", t2(st['u']), [t2(da_)])[0]
            units['mla_w_a'][j] = _cols_by_chip(jnp.concatenate(
                [gwa[:, :ql + kvl], gwa[:, ql + kvl + QK_NOPE:ql + kvl + QK_NOPE + QK_ROPE]], axis=1))
        else:
            dr = t3(_mm_nn(f"sc_out_bwd_{l}", [(dy1f, _w2(jnp.swapaxes(st['w_out'], 0, 1)))], F32))
            units['sc_w_out'][j] = _mm_tn(f"sc_out_dw_{l}", t2(st['r']), [dy1f], out_dtype=BF16).reshape(N_CHIP, -1, d)
            dgb, dgc, dh, dcw = _shortconv_bwd(f"shortconv_bwd_{l}", st['q'], dr, st['cw'])
            stack['sc_conv'][j] = dcw
            parts = [t2(dgb), t2(dgc), t2(dh)]
            in_rows, jj = st['in_rows']
            du1 = t3(_mm_nn(f"sc_in_bwd_{l}", [(parts[k], (in_rows, 3 * jj + k)) for k in range(3)], F32,
                            tm_cap=WIDE_K_ROWS))
            units['sc_w_in'][j] = _cols_by_chip(jnp.concatenate(
                [_mm_tn(f"sc_in_dw_{k}_{l}", t2(st['u']), [parts[k]])[0] for k in range(3)], axis=1))
        upstream = (dz1, du1, sc1)
        mixer = {0: ['pool_w'], 1: ['mla_w_a', 'mla_w_uq', 'mla_w_ukv', 'mla_w_o'], 2: ['sc_w_in', 'sc_w_out']}[kind]
        for n in mixer:
            units[n][j] = units[n][j].astype(BF16)
        cargo_a = [('ffn_w_up', l, units['ffn_w_up'][l])] if l > 0 else []
        cargo_b = [(n, j, units[n][j]) for n in mixer]
    grad_x, dmods[0][1], dmods[0][0] = _input_bwd("input_bwd", alpha, upstream[0], upstream[1], x, mods[0][1])

    for n, parts in stack.items():
        grads[n] = jnp.stack(parts)
    grads['ln_g'] = jnp.stack([jnp.concatenate(r, axis=0) for r in g_ln_g])
    grads['ln_b'] = jnp.stack([jnp.concatenate(r, axis=0) for r in g_ln_b])
    dmod_mine = jnp.stack([jnp.concatenate([t[:, 0, :] for t in dmods[l]], axis=-1) for l in range(depth)])

    small_grad_names = small_names + ['mla_q_norm', 'mla_kv_norm', 'ffn_conv_b']
    sg_pack, sg_spans = _pack_rows([dmod_mine] + [grads[n] for n in small_grad_names], F32, SUBLANE)
    rows_sg = sg_pack.shape[0]
    sg_all = _all_gather8("gather_small_grads", sg_pack, True).reshape(N_DEV, rows_sg, PACK_COLS)
    dmod_all = sg_all.reshape(N_DEV, -1)[:, :dmod_mine.size].reshape(N_DEV, depth, bsz, 6 * d)
    dmod_all = jnp.transpose(dmod_all, (1, 0, 2, 3)).reshape(depth, N_DEV * bsz, 6 * d)
    sg_sum = _sum8("sum_small_grads", sg_all).reshape(-1)
    for n, (off, shape) in zip(small_grad_names, sg_spans[1:]):
        g_full = sg_sum[off:off + _size(shape)].reshape(shape)
        if n in SMALL_SHARDED:
            ax = SMALL_SHARDED[n]
            width = shape[ax] // N_CHIP
            g_full = lax.dynamic_slice_in_dim(g_full, chip * width, width, axis=ax)
        grads[n] = g_full
    dmod_cols = lax.dynamic_slice_in_dim(dmod_all, chip * n_mod, n_mod, axis=2)
    grads['mod_w'], gb = _mod_bwd("mod_bwd", c_all, dmod_cols, dmod_all)
    grads['mod_b'] = gb[:, 0, :]

    keys = [(n, i) for n in big_names for i in range(len(units[n]))]
    last = [(n, i, units[n][i]) for n, i in keys if (n, i) not in received]
    landed(last, _scatter_grads("scatter_big_grads", [u for _, _, u in last]))
    chip_core = jnp.stack([chip, mc]).astype(jnp.int32)
    bufs = _swap_halves("swap_big_grad_halves",
                        [_sum8_into_half(f"sum_big_grads_{n}_{i}", units[n][i], received[(n, i)], chip_core)
                         for n, i in keys])
    layer_bufs = {n: [b for (m, _), b in zip(keys, bufs) if m == n] for n in big_names}

    deltas, new_m, new_v = {}, {}, {}
    for n in WEIGHTS:
        if n in layer_bufs:
            grads[n], deltas[n], new_m[n], new_v[n] = _adamw_layers(f"adamw_{n}", wts[n], layer_bufs[n], mom1[n], mom2[n])
        else:
            deltas[n], new_m[n], new_v[n] = _adamw(f"adamw_{n}", wts[n], grads[n], mom1[n], mom2[n])
    return (loss, grad_x, *[grads[n] for n in WEIGHTS], *[deltas[n] for n in WEIGHTS],
            *[new_m[n] for n in WEIGHTS], *[new_v[n] for n in WEIGHTS])
```
